```python
import jax, jax.numpy as jnp
from jax import lax
import numpy as np

D_MODEL = 1024
BATCH = 16
SEQ = 2048
DEPTH = 2

N_META = 16
BLOCK = 128
N_PAD = BLOCK - N_META
NORM_EPS = 1e-6
NEG = -1e30

SB_HEADS = 8
SB_DIM = 64
SB_WIDTH = SB_HEADS * SB_DIM

MLA_HEADS = 8
MLA_Q_LORA = 256
MLA_KV_LORA = 128
MLA_NOPE = 64
MLA_ROPE = 32
MLA_V = 64
MLA_WIDTH = MLA_HEADS * MLA_V
ROPE_BASE = 10000.0

SWA_HEADS = 16
SWA_KV_HEADS = 2
SWA_DIM = 64
SWA_WINDOW = 128
SWA_WIDTH = SWA_HEADS * SWA_DIM

EVEN_SPLITS = [SB_WIDTH, SB_WIDTH, SB_WIDTH, SB_WIDTH,
               MLA_Q_LORA, MLA_KV_LORA, MLA_ROPE, MLA_WIDTH]
EVEN_IN = sum(EVEN_SPLITS)
EVEN_OUT = SB_WIDTH + MLA_WIDTH
ODD_SPLITS = [SWA_WIDTH, SWA_KV_HEADS * SWA_DIM, SWA_KV_HEADS * SWA_DIM, SWA_WIDTH]
ODD_IN = sum(ODD_SPLITS)
ODD_OUT = SWA_WIDTH

kernel_name = "hybrid_stickbreak_mla_swa_meta"


def _offsets(sizes):
    return [int(o) for o in np.cumsum(sizes)[:-1]]


def rmsnorm(x, g):
    xf = x.astype(jnp.float32)
    y = xf * lax.rsqrt(jnp.mean(xf * xf, axis=-1, keepdims=True) + NORM_EPS)
    return (y * g.astype(jnp.float32)).astype(x.dtype)


def apply_rope(x, pos):
    half = x.shape[-1] // 2
    inv = ROPE_BASE ** (-jnp.arange(half, dtype=jnp.float32) / half)
    ang = pos.astype(jnp.float32)[:, None] * inv[None, :]
    cos = jnp.cos(ang)[:, None, :]
    sin = jnp.sin(ang)[:, None, :]
    x1 = x[..., :half].astype(jnp.float32)
    x2 = x[..., half:].astype(jnp.float32)
    return jnp.concatenate([x1 * cos - x2 * sin, x1 * sin + x2 * cos], axis=-1).astype(x.dtype)


def alibi_slopes(n_heads):
    return 2.0 ** (-8.0 * (jnp.arange(n_heads, dtype=jnp.float32) + 1.0) / n_heads)


def stick_breaking_attention(q, k, v):
    Lp = q.shape[1]
    pos = jnp.arange(Lp)
    scale = SB_DIM ** -0.5
    outs = []
    for i in range(Lp // BLOCK):
        q0, q1 = i * BLOCK, (i + 1) * BLOCK
        z = jnp.einsum('bthd,bshd->bhts', q[:, q0:q1], k[:, :q1]).astype(jnp.float32) * scale
        t_pos = pos[q0:q1][:, None]
        s_pos = pos[:q1][None, :]
        mask = (s_pos < t_pos) & (s_pos >= N_PAD)
        log_beta = jax.nn.log_sigmoid(z)
        log_1m = jnp.where(mask, log_beta - z, 0.0)
        suffix = lax.cumsum(log_1m, axis=3, reverse=True) - log_1m
        a = jnp.where(mask, jnp.exp(log_beta + suffix), 0.0)
        outs.append(jnp.einsum('bhts,bshd->bthd', a.astype(v.dtype), v[:, :q1]))
    return jnp.concatenate(outs, axis=1)


def causal_block_softmax_attention(q, k, v, scale):
    Lp = q.shape[1]
    pos = jnp.arange(Lp)
    outs = []
    for i in range(Lp // BLOCK):
        q0, q1 = i * BLOCK, (i + 1) * BLOCK
        s = jnp.einsum('bthd,bshd->bhts', q[:, q0:q1], k[:, :q1]).astype(jnp.float32) * scale
        mask = (pos[None, :q1] <= pos[q0:q1, None]) & (pos[None, :q1] >= N_PAD)
        p = jax.nn.softmax(jnp.where(mask, s, NEG), axis=-1)
        outs.append(jnp.einsum('bhts,bshd->bthd', p.astype(v.dtype), v[:, :q1]))
    return jnp.concatenate(outs, axis=1)


def sliding_window_sink_attention(q, k, v, sinks):
    B, Lp = q.shape[0], q.shape[1]
    nb = Lp // BLOCK
    G = SWA_HEADS // SWA_KV_HEADS
    K = SWA_KV_HEADS
    qb = q.reshape(B, nb, BLOCK, K, G, SWA_DIM)
    kb = k.reshape(B, nb, BLOCK, K, SWA_DIM)
    vb = v.reshape(B, nb, BLOCK, K, SWA_DIM)
    shift = ((0, 0), (1, 0), (0, 0), (0, 0), (0, 0))
    k_band = jnp.concatenate([jnp.pad(kb[:, :-1], shift), kb], axis=2)
    v_band = jnp.concatenate([jnp.pad(vb[:, :-1], shift), vb], axis=2)
    k_meta = k[:, N_PAD:BLOCK]
    v_meta = v[:, N_PAD:BLOCK]
    blk = jnp.arange(nb)[:, None] * BLOCK
    t_pos = blk + jnp.arange(BLOCK)[None, :]
    s_pos = blk - BLOCK + jnp.arange(2 * BLOCK)[None, :]
    m_pos = N_PAD + jnp.arange(N_META)
    d_band = t_pos[:, :, None] - s_pos[:, None, :]
    d_meta = t_pos[:, :, None] - m_pos[None, None, :]
    band_ok = (d_band >= 0) & (d_band < SWA_WINDOW) & (s_pos[:, None, :] >= BLOCK)
    meta_ok = d_meta >= 0
    slopes = alibi_slopes(SWA_HEADS).reshape(K, G)[:, :, None, None]
    scale = SWA_DIM ** -0.5
    s_band = (jnp.einsum('bnqkgd,bnskd->bnkgqs', qb, k_band).astype(jnp.float32) * scale
              - slopes * d_band.astype(jnp.float32)[:, None, None])
    s_band = jnp.where(band_ok[:, None, None], s_band, NEG)
    s_meta = (jnp.einsum('bnqkgd,bmkd->bnkgqm', qb, k_meta).astype(jnp.float32) * scale
              - slopes * d_meta.astype(jnp.float32)[:, None, None])
    s_meta = jnp.where(meta_ok[:, None, None], s_meta, NEG)
    sink = jnp.broadcast_to(sinks.astype(jnp.float32).reshape(K, G, 1, 1),
                            s_band.shape[:-1] + (1,))
    p = jax.nn.softmax(jnp.concatenate([s_band, s_meta, sink], axis=-1), axis=-1)
    S = 2 * BLOCK
    p_band = p[..., :S].astype(v.dtype)
    p_meta = p[..., S:S + N_META].astype(v.dtype)
    o = (jnp.einsum('bnkgqs,bnskd->bnqkgd', p_band, v_band)
         + jnp.einsum('bnkgqm,bmkd->bnqkgd', p_meta, v_meta))
    return o.reshape(B, Lp, SWA_WIDTH)


def even_layer(h, pos, w_in, q_norm_g, kv_norm_g, w_uq, w_ukv, w_out):
    B, Lp = h.shape[0], h.shape[1]
    proj = h @ w_in
    q_sb, k_sb, v_sb, g_sb, c_q, c_kv, k_r, g_mla = jnp.split(proj, _offsets(EVEN_SPLITS), axis=-1)
    shp = (B, Lp, SB_HEADS, SB_DIM)
    o_sb = stick_breaking_attention(q_sb.reshape(shp), k_sb.reshape(shp), v_sb.reshape(shp))
    o_sb = o_sb.reshape(B, Lp, SB_WIDTH) * jax.nn.silu(g_sb)
    qh = (rmsnorm(c_q, q_norm_g) @ w_uq).reshape(B, Lp, MLA_HEADS, MLA_NOPE + MLA_ROPE)
    q_nope, q_rope = qh[..., :MLA_NOPE], qh[..., MLA_NOPE:]
    kvh = (rmsnorm(c_kv, kv_norm_g) @ w_ukv).reshape(B, Lp, MLA_HEADS, MLA_NOPE + MLA_V)
    k_nope, v_mla = kvh[..., :MLA_NOPE], kvh[..., MLA_NOPE:]
    k_rope = apply_rope(k_r[:, :, None, :], pos)
    q_full = jnp.concatenate([q_nope, apply_rope(q_rope, pos)], axis=-1)
    k_full = jnp.concatenate(
        [k_nope, jnp.broadcast_to(k_rope, (B, Lp, MLA_HEADS, MLA_ROPE))], axis=-1)
    o_mla = causal_block_softmax_attention(q_full, k_full, v_mla, (MLA_NOPE + MLA_ROPE) ** -0.5)
    o_mla = o_mla.reshape(B, Lp, MLA_WIDTH) * jax.nn.silu(g_mla)
    return jnp.concatenate([o_sb, o_mla], axis=-1) @ w_out


def odd_layer(h, w_in, sinks, w_out):
    B, Lp = h.shape[0], h.shape[1]
    proj = h @ w_in
    q, k, v, g = jnp.split(proj, _offsets(ODD_SPLITS), axis=-1)
    o = sliding_window_sink_attention(
        q.reshape(B, Lp, SWA_HEADS, SWA_DIM),
        k.reshape(B, Lp, SWA_KV_HEADS, SWA_DIM),
        v.reshape(B, Lp, SWA_KV_HEADS, SWA_DIM), sinks)
    return (o * jax.nn.silu(g)) @ w_out


def _fwd_setup_inputs(seed: int = 0) -> dict:
    key = jax.random.key(seed)
    ks = jax.random.split(key, 13)
    ne = (DEPTH + 1) // 2
    no = DEPTH // 2
    f32 = jnp.float32

    def w(k, shape, fan_in):
        return jax.random.normal(k, shape, f32) * (fan_in ** -0.5)

    def gain(k, shape):
        return 1.0 + 0.05 * jax.random.normal(k, shape, f32)

    return {
        "x": jax.random.normal(ks[0], (BATCH, SEQ, D_MODEL), f32),
        "meta": jax.random.normal(ks[1], (N_META, D_MODEL), f32),
        "norm_g": gain(ks[2], (DEPTH, D_MODEL)),
        "final_g": gain(ks[3], (D_MODEL,)),
        "ev_w_in": w(ks[4], (ne, D_MODEL, EVEN_IN), D_MODEL),
        "ev_q_norm_g": gain(ks[5], (ne, MLA_Q_LORA)),
        "ev_kv_norm_g": gain(ks[6], (ne, MLA_KV_LORA)),
        "ev_w_uq": w(ks[7], (ne, MLA_Q_LORA, MLA_HEADS * (MLA_NOPE + MLA_ROPE)), MLA_Q_LORA),
        "ev_w_ukv": w(ks[8], (ne, MLA_KV_LORA, MLA_HEADS * (MLA_NOPE + MLA_V)), MLA_KV_LORA),
        "ev_w_out": w(ks[9], (ne, EVEN_OUT, D_MODEL), EVEN_OUT),
        "od_w_in": w(ks[10], (no, D_MODEL, ODD_IN), D_MODEL),
        "od_sinks": 0.5 * jax.random.normal(ks[11], (no, SWA_HEADS), f32),
        "od_w_out": w(ks[12], (no, ODD_OUT, D_MODEL), ODD_OUT),
    }


def _fwd_reference(x, meta, norm_g, final_g, ev_w_in, ev_q_norm_g, ev_kv_norm_g, ev_w_uq,
              ev_w_ukv, ev_w_out, od_w_in, od_sinks, od_w_out):
    B = x.shape[0]
    meta_b = jnp.broadcast_to(meta.astype(x.dtype)[None], (B, N_META, D_MODEL))
    pad = jnp.zeros((B, N_PAD, D_MODEL), x.dtype)
    h = jnp.concatenate([pad, meta_b, x], axis=1)
    pos = jnp.arange(h.shape[1]) - N_PAD
    for layer in range(DEPTH):
        hn = rmsnorm(h, norm_g[layer])
        if layer % 2 == 0:
            i = layer // 2
            h = h + even_layer(hn, pos, ev_w_in[i], ev_q_norm_g[i], ev_kv_norm_g[i],
                               ev_w_uq[i], ev_w_ukv[i], ev_w_out[i])
        else:
            i = layer // 2
            h = h + odd_layer(hn, od_w_in[i], od_sinks[i], od_w_out[i])
    return rmsnorm(h, final_g)[:, BLOCK:]


import jax as _jax
import jax.numpy as _jnp

TWIN_FORMAT = 'train_step'
FWD_PARAMS = ['x', 'meta', 'norm_g', 'final_g', 'ev_w_in', 'ev_q_norm_g', 'ev_kv_norm_g', 'ev_w_uq', 'ev_w_ukv', 'ev_w_out', 'od_w_in', 'od_sinks', 'od_w_out']
TWIN_WEIGHTS = ['meta', 'norm_g', 'final_g', 'ev_w_in', 'ev_q_norm_g', 'ev_kv_norm_g', 'ev_w_uq', 'ev_w_ukv', 'ev_w_out', 'od_w_in', 'od_sinks', 'od_w_out']
TWIN_DIFF_INPUT = 'x'
TWIN_INPUTS = ['x', 'meta', 'norm_g', 'final_g', 'ev_w_in', 'ev_q_norm_g', 'ev_kv_norm_g', 'ev_w_uq', 'ev_w_ukv', 'ev_w_out', 'od_w_in', 'od_sinks', 'od_w_out', 'loss_target', 'm_meta', 'm_norm_g', 'm_final_g', 'm_ev_w_in', 'm_ev_q_norm_g', 'm_ev_kv_norm_g', 'm_ev_w_uq', 'm_ev_w_ukv', 'm_ev_w_out', 'm_od_w_in', 'm_od_sinks', 'm_od_w_out', 'v_meta', 'v_norm_g', 'v_final_g', 'v_ev_w_in', 'v_ev_q_norm_g', 'v_ev_kv_norm_g', 'v_ev_w_uq', 'v_ev_w_ukv', 'v_ev_w_out', 'v_od_w_in', 'v_od_sinks', 'v_od_w_out']
TWIN_OUTPUTS = ['loss', 'grad_x', 'grad_meta', 'grad_norm_g', 'grad_final_g', 'grad_ev_w_in', 'grad_ev_q_norm_g', 'grad_ev_kv_norm_g', 'grad_ev_w_uq', 'grad_ev_w_ukv', 'grad_ev_w_out', 'grad_od_w_in', 'grad_od_sinks', 'grad_od_w_out', 'delta_meta', 'delta_norm_g', 'delta_final_g', 'delta_ev_w_in', 'delta_ev_q_norm_g', 'delta_ev_kv_norm_g', 'delta_ev_w_uq', 'delta_ev_w_ukv', 'delta_ev_w_out', 'delta_od_w_in', 'delta_od_sinks', 'delta_od_w_out', 'new_m_meta', 'new_m_norm_g', 'new_m_final_g', 'new_m_ev_w_in', 'new_m_ev_q_norm_g', 'new_m_ev_kv_norm_g', 'new_m_ev_w_uq', 'new_m_ev_w_ukv', 'new_m_ev_w_out', 'new_m_od_w_in', 'new_m_od_sinks', 'new_m_od_w_out', 'new_v_meta', 'new_v_norm_g', 'new_v_final_g', 'new_v_ev_w_in', 'new_v_ev_q_norm_g', 'new_v_ev_kv_norm_g', 'new_v_ev_w_uq', 'new_v_ev_w_ukv', 'new_v_ev_w_out', 'new_v_od_w_in', 'new_v_od_sinks', 'new_v_od_w_out']
TWIN_LEAF_KINDS = {'loss': 'loss', 'grad_x': 'grad_x', 'grad_meta': 'grad_w', 'grad_norm_g': 'grad_w', 'grad_final_g': 'grad_w', 'grad_ev_w_in': 'grad_w', 'grad_ev_q_norm_g': 'grad_w', 'grad_ev_kv_norm_g': 'grad_w', 'grad_ev_w_uq': 'grad_w', 'grad_ev_w_ukv': 'grad_w', 'grad_ev_w_out': 'grad_w', 'grad_od_w_in': 'grad_w', 'grad_od_sinks': 'grad_w', 'grad_od_w_out': 'grad_w', 'delta_meta': 'delta_w', 'delta_norm_g': 'delta_w', 'delta_final_g': 'delta_w', 'delta_ev_w_in': 'delta_w', 'delta_ev_q_norm_g': 'delta_w', 'delta_ev_kv_norm_g': 'delta_w', 'delta_ev_w_uq': 'delta_w', 'delta_ev_w_ukv': 'delta_w', 'delta_ev_w_out': 'delta_w', 'delta_od_w_in': 'delta_w', 'delta_od_sinks': 'delta_w', 'delta_od_w_out': 'delta_w', 'new_m_meta': 'new_m', 'new_m_norm_g': 'new_m', 'new_m_final_g': 'new_m', 'new_m_ev_w_in': 'new_m', 'new_m_ev_q_norm_g': 'new_m', 'new_m_ev_kv_norm_g': 'new_m', 'new_m_ev_w_uq': 'new_m', 'new_m_ev_w_ukv': 'new_m', 'new_m_ev_w_out': 'new_m', 'new_m_od_w_in': 'new_m', 'new_m_od_sinks': 'new_m', 'new_m_od_w_out': 'new_m', 'new_v_meta': 'new_v', 'new_v_norm_g': 'new_v', 'new_v_final_g': 'new_v', 'new_v_ev_w_in': 'new_v', 'new_v_ev_q_norm_g': 'new_v', 'new_v_ev_kv_norm_g': 'new_v', 'new_v_ev_w_uq': 'new_v', 'new_v_ev_w_ukv': 'new_v', 'new_v_ev_w_out': 'new_v', 'new_v_od_w_in': 'new_v', 'new_v_od_sinks': 'new_v', 'new_v_od_w_out': 'new_v'}


def _forward(args):
    return _fwd_reference(*[args[k] for k in FWD_PARAMS])


def _output_shape():
    out = _jax.eval_shape(lambda: _forward(_fwd_setup_inputs(0)))
    return out.shape, out.dtype

N_MICROBATCH = 1
ADAM_LR = 0.001
ADAM_B1 = 0.9
ADAM_B2 = 0.999
ADAM_EPS = 1e-08
ADAM_WD = 0.01
ADAM_STEP = 10
PER_EXAMPLE_BATCH_AXIS = {'x': 0, 'loss_target': 0}
SHARED_INPUTS = []
_WEIGHT_DTYPES = {'meta': _jnp.float32, 'norm_g': _jnp.float32, 'final_g': _jnp.float32, 'ev_w_in': _jnp.float32, 'ev_q_norm_g': _jnp.float32, 'ev_kv_norm_g': _jnp.float32, 'ev_w_uq': _jnp.float32, 'ev_w_ukv': _jnp.float32, 'ev_w_out': _jnp.float32, 'od_w_in': _jnp.float32, 'od_sinks': _jnp.float32, 'od_w_out': _jnp.float32}
MOMENT_SCALE = {'meta': 3.310630e-03, 'norm_g': 7.024422e-02, 'final_g': 3.199082e+01, 'ev_w_in': 4.719000e-02, 'ev_q_norm_g': 2.707387e-02, 'ev_kv_norm_g': 5.106251e-02, 'ev_w_uq': 1.487058e-02, 'ev_w_ukv': 1.770329e-02, 'ev_w_out': 5.011543e-02, 'od_w_in': 3.598246e-02, 'od_sinks': 7.605811e-02, 'od_w_out': 2.956514e-02}


def _to_microbatches(a, axis):
    t = _jnp.moveaxis(a, axis, 0)
    t = t.reshape((N_MICROBATCH, t.shape[0] // N_MICROBATCH) + t.shape[1:])
    return _jnp.moveaxis(t, 1, axis + 1)


def setup_inputs(seed: int = 0) -> dict:
    inp = _fwd_setup_inputs(seed)
    key = _jax.random.fold_in(_jax.random.key(seed), 7919)
    shape, _ = _output_shape()
    out = dict(inp)
    out["loss_target"] = _jax.random.normal(_jax.random.fold_in(key, 0), shape, _jnp.float32)
    for i, name in enumerate(TWIN_WEIGHTS):
        w = inp[name].astype(_jnp.float32)
        if MOMENT_SCALE is None:
            s = _jnp.sqrt(_jnp.mean(_jnp.square(w)) + 1e-30)
        else:
            s = MOMENT_SCALE[name]
        km, kv = _jax.random.split(_jax.random.fold_in(key, i + 1))
        out[name] = w
        out["m_" + name] = s * _jax.random.normal(km, w.shape, _jnp.float32)
        out["v_" + name] = (s * s) * _jax.random.uniform(kv, w.shape, _jnp.float32, 0.5, 1.5)
    if N_MICROBATCH > 1:
        for name, axis in PER_EXAMPLE_BATCH_AXIS.items():
            out[name] = _to_microbatches(out[name], axis)
    return {'x': out['x'], 'meta': out['meta'], 'norm_g': out['norm_g'], 'final_g': out['final_g'], 'ev_w_in': out['ev_w_in'], 'ev_q_norm_g': out['ev_q_norm_g'], 'ev_kv_norm_g': out['ev_kv_norm_g'], 'ev_w_uq': out['ev_w_uq'], 'ev_w_ukv': out['ev_w_ukv'], 'ev_w_out': out['ev_w_out'], 'od_w_in': out['od_w_in'], 'od_sinks': out['od_sinks'], 'od_w_out': out['od_w_out'], 'loss_target': out['loss_target'], 'm_meta': out['m_meta'], 'm_norm_g': out['m_norm_g'], 'm_final_g': out['m_final_g'], 'm_ev_w_in': out['m_ev_w_in'], 'm_ev_q_norm_g': out['m_ev_q_norm_g'], 'm_ev_kv_norm_g': out['m_ev_kv_norm_g'], 'm_ev_w_uq': out['m_ev_w_uq'], 'm_ev_w_ukv': out['m_ev_w_ukv'], 'm_ev_w_out': out['m_ev_w_out'], 'm_od_w_in': out['m_od_w_in'], 'm_od_sinks': out['m_od_sinks'], 'm_od_w_out': out['m_od_w_out'], 'v_meta': out['v_meta'], 'v_norm_g': out['v_norm_g'], 'v_final_g': out['v_final_g'], 'v_ev_w_in': out['v_ev_w_in'], 'v_ev_q_norm_g': out['v_ev_q_norm_g'], 'v_ev_kv_norm_g': out['v_ev_kv_norm_g'], 'v_ev_w_uq': out['v_ev_w_uq'], 'v_ev_w_ukv': out['v_ev_w_ukv'], 'v_ev_w_out': out['v_ev_w_out'], 'v_od_w_in': out['v_od_w_in'], 'v_od_sinks': out['v_od_sinks'], 'v_od_w_out': out['v_od_w_out']}


def _loss(weights, diff, rest, loss_target):
    with _jax.named_scope("forward"):
        args = {**rest, TWIN_DIFF_INPUT: diff, **{k: w.astype(_WEIGHT_DTYPES[k]) for k, w in weights.items()}}
        y = _forward(args)
    with _jax.named_scope("loss_head"):
        err = _jnp.square(y.astype(_jnp.float32) - loss_target)
        return 0.5 * _jnp.sum(_jnp.mean(err, axis=-1)) if err.ndim else 0.5 * err


def _adamw(w, g, m, v):
    m = ADAM_B1 * m + (1.0 - ADAM_B1) * g
    v = ADAM_B2 * v + (1.0 - ADAM_B2) * _jnp.square(g)
    m_hat = m / (1.0 - ADAM_B1 ** ADAM_STEP)
    v_hat = v / (1.0 - ADAM_B2 ** ADAM_STEP)
    delta = -ADAM_LR * (m_hat / (_jnp.sqrt(v_hat) + ADAM_EPS) + ADAM_WD * w)
    return delta, m, v


def reference(x, meta, norm_g, final_g, ev_w_in, ev_q_norm_g, ev_kv_norm_g, ev_w_uq, ev_w_ukv, ev_w_out, od_w_in, od_sinks, od_w_out, loss_target, m_meta, m_norm_g, m_final_g, m_ev_w_in, m_ev_q_norm_g, m_ev_kv_norm_g, m_ev_w_uq, m_ev_w_ukv, m_ev_w_out, m_od_w_in, m_od_sinks, m_od_w_out, v_meta, v_norm_g, v_final_g, v_ev_w_in, v_ev_q_norm_g, v_ev_kv_norm_g, v_ev_w_uq, v_ev_w_ukv, v_ev_w_out, v_od_w_in, v_od_sinks, v_od_w_out):
    given = dict(x=x, meta=meta, norm_g=norm_g, final_g=final_g, ev_w_in=ev_w_in, ev_q_norm_g=ev_q_norm_g, ev_kv_norm_g=ev_kv_norm_g, ev_w_uq=ev_w_uq, ev_w_ukv=ev_w_ukv, ev_w_out=ev_w_out, od_w_in=od_w_in, od_sinks=od_sinks, od_w_out=od_w_out, loss_target=loss_target, m_meta=m_meta, m_norm_g=m_norm_g, m_final_g=m_final_g, m_ev_w_in=m_ev_w_in, m_ev_q_norm_g=m_ev_q_norm_g, m_ev_kv_norm_g=m_ev_kv_norm_g, m_ev_w_uq=m_ev_w_uq, m_ev_w_ukv=m_ev_w_ukv, m_ev_w_out=m_ev_w_out, m_od_w_in=m_od_w_in, m_od_sinks=m_od_sinks, m_od_w_out=m_od_w_out, v_meta=v_meta, v_norm_g=v_norm_g, v_final_g=v_final_g, v_ev_w_in=v_ev_w_in, v_ev_q_norm_g=v_ev_q_norm_g, v_ev_kv_norm_g=v_ev_kv_norm_g, v_ev_w_uq=v_ev_w_uq, v_ev_w_ukv=v_ev_w_ukv, v_ev_w_out=v_ev_w_out, v_od_w_in=v_od_w_in, v_od_sinks=v_od_sinks, v_od_w_out=v_od_w_out)
    weights = {n: given[n] for n in TWIN_WEIGHTS}
    shared = {n: given[n] for n in SHARED_INPUTS}
    per_example = {n: given[n] for n in ['x']}
    grad_fn = _jax.value_and_grad(_loss, argnums=(0, 1))

    def one_microbatch(ex, loss_target):
        ex = dict(ex)
        diff = ex.pop(TWIN_DIFF_INPUT)
        return grad_fn(weights, diff, {**shared, **ex}, loss_target)

    if N_MICROBATCH == 1:
        loss, (grad_w, grad_x) = one_microbatch(per_example, given["loss_target"])
    else:
        def body(carry, xs):
            loss_sum, grad_sum = carry
            l_k, (gw_k, gx_k) = one_microbatch(xs[0], xs[1])
            with _jax.named_scope("update"):
                return (loss_sum + l_k, _jax.tree.map(_jnp.add, grad_sum, gw_k)), gx_k

        init = (_jnp.zeros((), _jnp.float32), _jax.tree.map(_jnp.zeros_like, weights))
        (loss, grad_w), grad_x = _jax.lax.scan(body, init, (per_example, given["loss_target"]))
    with _jax.named_scope("update"):
        delta_w, new_m, new_v = {}, {}, {}
        for n in TWIN_WEIGHTS:
            delta_w[n], new_m[n], new_v[n] = _adamw(weights[n], grad_w[n], given["m_" + n], given["v_" + n])
    return (loss, grad_x, *[grad_w[n] for n in TWIN_WEIGHTS], *[delta_w[n] for n in TWIN_WEIGHTS],
            *[new_m[n] for n in TWIN_WEIGHTS], *[new_v[n] for n in TWIN_WEIGHTS])
```

```python
import jax
import jax.numpy as jnp
from jax import lax
from jax.experimental import pallas as pl
from jax.experimental.pallas import tpu as pltpu

F32 = jnp.float32
BF16 = jnp.bfloat16

D_MODEL = 1024
N_META = 16
BLK = 128
HALF = 64
N_PAD = BLK - N_META
NORM_EPS = 1e-6
NEG = -1e30
N_DEV = 8

SB_SCALE = 64 ** -0.5
MLA_SCALE = 96 ** -0.5
SWA_SCALE = 64 ** -0.5
ROPE_BASE = 10000.0

EV_IN_PAD = 3072
EV_MID_BLK = 4
EV_GMLA_BLK = 20
OD_K_BLK = 16
OD_V_BLK = 17

ADAM_LR = 0.001
ADAM_B1 = 0.9
ADAM_B2 = 0.999
ADAM_EPS = 1e-08
ADAM_WD = 0.01
ADAM_STEP = 10

PACK_ROWS = 7680


def _dot(a, b):
    return lax.dot_general(a, b, (((1,), (0,)), ((), ())), preferred_element_type=F32)


def _dot_nt(a, b):
    return lax.dot_general(a, b, (((1,), (1,)), ((), ())), preferred_element_type=F32)


def _dot_tn(a, b):
    return lax.dot_general(a, b, (((0,), (0,)), ((), ())), preferred_element_type=F32)


def _cum(x, tri):
    hi = x.astype(BF16)
    lo = (x - hi.astype(F32)).astype(BF16)
    return _dot(hi, tri) + _dot(lo, tri)


def _sigmoid(x):
    return 1.0 / (1.0 + jnp.exp(-x))


def _iotas():
    row = lax.broadcasted_iota(jnp.int32, (BLK, BLK), 0)
    lane = lax.broadcasted_iota(jnp.int32, (BLK, BLK), 1)
    return row, lane


def _rope(x, c, s1, s2):
    return x * c + pltpu.roll(x, BLK - 16, 1) * s1 + pltpu.roll(x, 16, 1) * s2


def _rope_t(x, c, s1, s2):
    return x * c - pltpu.roll(x, BLK - 16, 1) * s1 - pltpu.roll(x, 16, 1) * s2


def _params(sem, vmem_mb=None):
    kw = dict(dimension_semantics=sem)
    if vmem_mb is not None:
        kw["vmem_limit_bytes"] = vmem_mb << 20
    return pltpu.CompilerParams(**kw)


def _row_tile(t, cands):
    for c in cands:
        if t % c == 0:
            return c
    raise ValueError(t)


def _mm(a, w, mode, name, add=None, out_dtype=F32):
    m, k = a.shape
    n = w.shape[1] if mode == "nn" else w.shape[0]
    tm = _row_tile(m, (256, 128))

    def body(*refs):
        if add is None:
            a_ref, w_ref, o_ref = refs
        else:
            a_ref, w_ref, r_ref, o_ref = refs
        x = a_ref[...].astype(BF16)
        acc = _dot(x, w_ref[...]) if mode == "nn" else _dot_nt(x, w_ref[...])
        if add is not None:
            acc = acc + r_ref[...]
        o_ref[...] = acc.astype(o_ref.dtype)

    in_specs = [pl.BlockSpec((tm, k), lambda i: (i, 0)), pl.BlockSpec(w.shape, lambda i: (0, 0))]
    args = [a, w]
    if add is not None:
        in_specs.append(pl.BlockSpec((tm, n), lambda i: (i, 0)))
        args.append(add)
    return pl.pallas_call(
        body, name=name, grid=(m // tm,), in_specs=in_specs,
        out_specs=pl.BlockSpec((tm, n), lambda i: (i, 0)),
        out_shape=jax.ShapeDtypeStruct((m, n), out_dtype),
        compiler_params=_params(("parallel",), 48),
    )(*args)


def _mm_tn(x, dy, name):
    t, k = x.shape
    n = dy.shape[1]
    tt = _row_tile(t, (1088, 256, 128))
    tn = n if n <= 1152 else _row_tile(n, (1152, 1024))

    def body(x_ref, dy_ref, o_ref):
        @pl.when(pl.program_id(1) == 0)
        def _():
            o_ref[...] = jnp.zeros_like(o_ref)

        o_ref[...] += _dot_tn(x_ref[...].astype(BF16), dy_ref[...].astype(BF16))

    return pl.pallas_call(
        body, name=name, grid=(n // tn, t // tt),
        in_specs=[pl.BlockSpec((tt, k), lambda j, i: (i, 0)), pl.BlockSpec((tt, tn), lambda j, i: (i, j))],
        out_specs=pl.BlockSpec((k, tn), lambda j, i: (0, j)),
        out_shape=jax.ShapeDtypeStruct((k, n), F32),
        compiler_params=_params(("parallel", "arbitrary"), 48),
    )(x, dy)


def _rms_fwd(h, g, name):
    t, d = h.shape
    tr = _row_tile(t, (256, 128))

    def body(h_ref, g_ref, o_ref):
        x = h_ref[...]
        r = lax.rsqrt(jnp.mean(x * x, axis=1, keepdims=True) + NORM_EPS)
        o_ref[...] = (x * r * g_ref[...]).astype(o_ref.dtype)

    return pl.pallas_call(
        body, name=name, grid=(t // tr,),
        in_specs=[pl.BlockSpec((tr, d), lambda i: (i, 0)), pl.BlockSpec((1, d), lambda i: (0, 0))],
        out_specs=pl.BlockSpec((tr, d), lambda i: (i, 0)),
        out_shape=jax.ShapeDtypeStruct((t, d), BF16),
        compiler_params=_params(("parallel",)),
    )(h, g)


def _rms_bwd(h, g, dhn, dres, name):
    t, d = h.shape
    tr = _row_tile(t, (256, 128))

    def body(h_ref, g_ref, dhn_ref, dres_ref, dh_ref, dg_ref):
        @pl.when(pl.program_id(0) == 0)
        def _():
            dg_ref[...] = jnp.zeros_like(dg_ref)

        x = h_ref[...]
        r = lax.rsqrt(jnp.mean(x * x, axis=1, keepdims=True) + NORM_EPS)
        nx = x * r
        dy = dhn_ref[...]
        dn = dy * g_ref[...]
        dh_ref[...] = dres_ref[...] + r * (dn - nx * jnp.mean(dn * nx, axis=1, keepdims=True))
        dg_ref[...] += jnp.sum(dy * nx, axis=0, keepdims=True)

    row = pl.BlockSpec((tr, d), lambda i: (i, 0))
    vec = pl.BlockSpec((1, d), lambda i: (0, 0))
    return pl.pallas_call(
        body, name=name, grid=(t // tr,),
        in_specs=[row, vec, row, row], out_specs=[row, vec],
        out_shape=[jax.ShapeDtypeStruct((t, d), F32), jax.ShapeDtypeStruct((1, d), F32)],
        compiler_params=_params(("arbitrary",)),
    )(h, g, dhn, dres)


def _final(h2, tgt, g, nbatch, nb):
    t, d = h2.shape

    def body(h_ref, t_ref, g_ref, dh_ref, loss_ref, dg_ref):
        b = pl.program_id(0)
        i = pl.program_id(1)

        @pl.when((b == 0) & (i == 0))
        def _():
            loss_ref[...] = jnp.zeros_like(loss_ref)
            dg_ref[...] = jnp.zeros_like(dg_ref)

        x = h_ref[...]
        r = lax.rsqrt(jnp.mean(x * x, axis=1, keepdims=True) + NORM_EPS)
        nx = x * r
        gg = g_ref[...]
        live = jnp.where(i >= 1, 1.0, 0.0)
        err = (nx * gg - t_ref[...]) * live
        loss_ref[...] += jnp.sum(err * err, axis=0, keepdims=True)
        dy = err * (1.0 / d)
        dn = dy * gg
        dh_ref[...] = r * (dn - nx * jnp.mean(dn * nx, axis=1, keepdims=True))
        dg_ref[...] += jnp.sum(dy * nx, axis=0, keepdims=True)

    vec = pl.BlockSpec((1, d), lambda b, i: (0, 0))
    return pl.pallas_call(
        body, name="final_loss", grid=(nbatch, nb),
        in_specs=[pl.BlockSpec((BLK, d), lambda b, i: (b * nb + i, 0)),
                  pl.BlockSpec((BLK, d), lambda b, i: (b * (nb - 1) + jnp.maximum(i - 1, 0), 0)),
                  vec],
        out_specs=[pl.BlockSpec((BLK, d), lambda b, i: (b * nb + i, 0)), vec, vec],
        out_shape=[jax.ShapeDtypeStruct((t, d), F32), jax.ShapeDtypeStruct((1, d), F32),
                   jax.ShapeDtypeStruct((1, d), F32)],
        compiler_params=_params(("arbitrary", "arbitrary")),
    )(h2, tgt, g)


def _sb_logits(qm, kb, i, j, row, lane):
    z = _dot_nt(qm, kb) * SB_SCALE
    s_pos = j * BLK + lane
    mask = (s_pos < i * BLK + row) & (s_pos >= N_PAD)
    sp = jnp.log(1.0 + jnp.exp(-jnp.abs(z)))
    log_beta = jnp.minimum(z, 0.0) - sp
    log_1m = -jnp.maximum(z, 0.0) - sp
    return mask, log_beta, log_1m


def _sb_fwd(proj, nbatch, nb):
    lp = nb * BLK
    t = nbatch * lp

    def body(q_ref, k_ref, v_ref, g_ref, o_ref, og_ref, cm_ref, c_scr, acc_scr):
        i = pl.program_id(2)
        row, lane = _iotas()
        upper = (row > lane).astype(BF16)
        q = q_ref[...]
        zero = jnp.zeros((BLK, BLK), F32)
        cm_ref[...] = jnp.zeros_like(cm_ref)
        for hh in range(2):
            cols = slice(hh * BLK, (hh + 1) * BLK)
            hm = (lane >= HALF) if hh else (lane < HALF)
            qm = jnp.where(hm, q, jnp.zeros_like(q))
            c_scr[...] = zero
            acc_scr[...] = zero

            def step(jj, carry, qm=qm, cols=cols):
                j = i - jj
                off = pl.multiple_of(j * BLK, BLK)
                kb = k_ref[pl.ds(off, BLK), :]
                vb = v_ref[pl.ds(off, BLK), :]
                mask, log_beta, log_1m = _sb_logits(qm, kb, i, j, row, lane)
                l1 = jnp.where(mask, log_1m, 0.0)
                c = c_scr[...]
                suffix = _cum(l1, upper)
                a = jnp.where(mask, jnp.exp(log_beta + suffix + c), 0.0)
                acc_scr[...] += _dot(a.astype(BF16), vb)
                cm_ref[:, cols] += jnp.where(lane == j, c, 0.0)
                c_scr[...] = c + jnp.sum(l1, axis=1, keepdims=True)
                return carry

            lax.fori_loop(0, i + 1, step, 0)
            if hh == 0:
                o_ref[...] = acc_scr[...]
            else:
                o_ref[...] = jnp.where(hm, acc_scr[...], o_ref[...])
        out = o_ref[...]
        g = g_ref[...].astype(F32)
        og_ref[...] = (out * g * _sigmoid(g)).astype(og_ref.dtype)

    tile = lambda col0: pl.BlockSpec((BLK, BLK), lambda b, p, i: (b * nb + i, col0 + p))
    full = lambda col0: pl.BlockSpec((lp, BLK), lambda b, p, i: (b, col0 + p))
    return pl.pallas_call(
        body, name="sb_fwd", grid=(nbatch, 4, nb),
        in_specs=[tile(0), full(4), full(8), tile(12)],
        out_specs=[tile(0), tile(0), pl.BlockSpec((BLK, 2 * BLK), lambda b, p, i: (b * nb + i, p))],
        out_shape=[jax.ShapeDtypeStruct((t, 512), F32), jax.ShapeDtypeStruct((t, 512), BF16),
                   jax.ShapeDtypeStruct((t, 1024), F32)],
        scratch_shapes=[pltpu.VMEM((BLK, BLK), F32), pltpu.VMEM((BLK, BLK), F32)],
        compiler_params=_params(("parallel", "parallel", "arbitrary")),
    )(proj, proj, proj, proj)


def _sb_bwd(proj, o, dog, cm, nbatch, nb):
    lp = nb * BLK
    t = nbatch * lp

    def body(q_ref, k_ref, v_ref, g_ref, o_ref, dog_ref, cm_ref, dq_ref, dk_ref, dv_ref, dg_ref, c_scr, acc_scr):
        i = pl.program_id(2)

        @pl.when(i == 0)
        def _():
            dk_ref[...] = jnp.zeros_like(dk_ref)
            dv_ref[...] = jnp.zeros_like(dv_ref)

        row, lane = _iotas()
        upper = (row > lane).astype(BF16)
        lower = (row < lane).astype(BF16)
        g = g_ref[...].astype(F32)
        sig = _sigmoid(g)
        dog_v = dog_ref[...]
        dg_ref[...] = dog_v * o_ref[...] * (sig * (1.0 + g * (1.0 - sig)))
        dob = (dog_v * g * sig).astype(BF16)
        q = q_ref[...]
        zero = jnp.zeros((BLK, BLK), F32)
        for hh in range(2):
            cols = slice(hh * BLK, (hh + 1) * BLK)
            hm = (lane >= HALF) if hh else (lane < HALF)
            qm = jnp.where(hm, q, jnp.zeros_like(q))
            dom = jnp.where(hm, dob, jnp.zeros_like(dob))
            c_scr[...] = zero
            acc_scr[...] = zero

            def step(j, carry, qm=qm, dom=dom, cols=cols):
                off = pl.multiple_of(j * BLK, BLK)
                kb = k_ref[pl.ds(off, BLK), :]
                vb = v_ref[pl.ds(off, BLK), :]
                mask, log_beta, log_1m = _sb_logits(qm, kb, i, j, row, lane)
                l1 = jnp.where(mask, log_1m, 0.0)
                cj = jnp.sum(jnp.where(lane == j, cm_ref[:, cols], 0.0), axis=1, keepdims=True)
                suffix = _cum(l1, upper) + cj
                p = jnp.where(mask, jnp.exp(log_beta + suffix), 0.0)
                dzz = p * _dot_nt(dom, vb)
                c2 = c_scr[...]
                prefix = _cum(dzz, lower) + c2
                dz = jnp.where(mask, dzz * jnp.exp(log_1m) - jnp.exp(log_beta) * prefix, 0.0)
                dzb = (dz * SB_SCALE).astype(BF16)
                acc_scr[...] += _dot(dzb, kb)
                dk_ref[pl.ds(off, BLK), :] += _dot_tn(dzb, qm)
                dv_ref[pl.ds(off, BLK), :] += _dot_tn(p.astype(BF16), dom)
                c_scr[...] = c2 + jnp.sum(dzz, axis=1, keepdims=True)
                return carry

            lax.fori_loop(0, i + 1, step, 0)
            if hh == 0:
                dq_ref[...] = acc_scr[...]
            else:
                dq_ref[...] = jnp.where(hm, acc_scr[...], dq_ref[...])

    tile = lambda col0: pl.BlockSpec((BLK, BLK), lambda b, p, i: (b * nb + i, col0 + p))
    full = lambda col0: pl.BlockSpec((lp, BLK), lambda b, p, i: (b, col0 + p))
    wide = pl.BlockSpec((BLK, 2 * BLK), lambda b, p, i: (b * nb + i, p))
    sds = jax.ShapeDtypeStruct((t, 512), F32)
    return pl.pallas_call(
        body, name="sb_bwd", grid=(nbatch, 4, nb),
        in_specs=[tile(0), full(4), full(8), tile(12), tile(0), tile(0), wide],
        out_specs=[tile(0), full(0), full(0), tile(0)],
        out_shape=[sds, sds, sds, sds],
        scratch_shapes=[pltpu.VMEM((BLK, BLK), F32), pltpu.VMEM((BLK, BLK), F32)],
        compiler_params=_params(("parallel", "parallel", "arbitrary")),
    )(proj, proj, proj, proj, o, dog, cm)


def _mla_prep(proj, gq, gkv, tabs, nbatch, nb):
    t = proj.shape[0]

    def body(mid_ref, gq_ref, gkv_ref, c_ref, s1_ref, s2_ref, cq_ref, ckv_ref, kr_ref):
        cq = mid_ref[:, 0:256].astype(F32)
        r = lax.rsqrt(jnp.mean(cq * cq, axis=1, keepdims=True) + NORM_EPS)
        cq_ref[...] = (cq * r * gq_ref[...]).astype(BF16)
        ckv = mid_ref[:, 256:384].astype(F32)
        r = lax.rsqrt(jnp.mean(ckv * ckv, axis=1, keepdims=True) + NORM_EPS)
        ckv_ref[...] = (ckv * r * gkv_ref[...]).astype(BF16)
        kr = mid_ref[:, 384:512].astype(F32)
        kr_ref[...] = _rope(kr, c_ref[...], s1_ref[...], s2_ref[...]).astype(BF16)

    tab = pl.BlockSpec((BLK, BLK), lambda b, i: (i, 0))
    rowspec = lambda w: pl.BlockSpec((BLK, w), lambda b, i: (b * nb + i, 0))
    return pl.pallas_call(
        body, name="mla_prep", grid=(nbatch, nb),
        in_specs=[pl.BlockSpec((BLK, 512), lambda b, i: (b * nb + i, EV_MID_BLK)),
                  pl.BlockSpec((1, 256), lambda b, i: (0, 0)), pl.BlockSpec((1, 128), lambda b, i: (0, 0)),
                  tab, tab, tab],
        out_specs=[rowspec(256), rowspec(128), rowspec(128)],
        out_shape=[jax.ShapeDtypeStruct((t, 256), BF16), jax.ShapeDtypeStruct((t, 128), BF16),
                   jax.ShapeDtypeStruct((t, 128), BF16)],
        compiler_params=_params(("parallel", "parallel")),
    )(proj, gq, gkv, *tabs)


def _mla_prep_bwd(proj, gq, gkv, tabs, dcqn, dckvn, dkrot, nbatch, nb):
    t = proj.shape[0]

    def body(mid_ref, gq_ref, gkv_ref, c_ref, s1_ref, s2_ref, dcq_ref, dckv_ref, dkr_ref,
             dmid_ref, dgq_ref, dgkv_ref):
        @pl.when((pl.program_id(0) == 0) & (pl.program_id(1) == 0))
        def _():
            dgq_ref[...] = jnp.zeros_like(dgq_ref)
            dgkv_ref[...] = jnp.zeros_like(dgkv_ref)

        def norm_bwd(x, gain, dy, dgain_ref):
            r = lax.rsqrt(jnp.mean(x * x, axis=1, keepdims=True) + NORM_EPS)
            nx = x * r
            dn = dy * gain
            dgain_ref[...] += jnp.sum(dy * nx, axis=0, keepdims=True)
            return r * (dn - nx * jnp.mean(dn * nx, axis=1, keepdims=True))

        dmid_ref[:, 0:256] = norm_bwd(mid_ref[:, 0:256].astype(F32), gq_ref[...], dcq_ref[...], dgq_ref)
        dmid_ref[:, 256:384] = norm_bwd(mid_ref[:, 256:384].astype(F32), gkv_ref[...], dckv_ref[...], dgkv_ref)
        dmid_ref[:, 384:512] = _rope_t(dkr_ref[...], c_ref[...], s1_ref[...], s2_ref[...])

    tab = pl.BlockSpec((BLK, BLK), lambda b, i: (i, 0))
    rowspec = lambda w: pl.BlockSpec((BLK, w), lambda b, i: (b * nb + i, 0))
    vq = pl.BlockSpec((1, 256), lambda b, i: (0, 0))
    vkv = pl.BlockSpec((1, 128), lambda b, i: (0, 0))
    return pl.pallas_call(
        body, name="mla_prep_bwd", grid=(nbatch, nb),
        in_specs=[pl.BlockSpec((BLK, 512), lambda b, i: (b * nb + i, EV_MID_BLK)), vq, vkv, tab, tab, tab,
                  rowspec(256), rowspec(128), rowspec(128)],
        out_specs=[rowspec(512), vq, vkv],
        out_shape=[jax.ShapeDtypeStruct((t, 512), F32), jax.ShapeDtypeStruct((1, 256), F32),
                   jax.ShapeDtypeStruct((1, 128), F32)],
        compiler_params=_params(("arbitrary", "arbitrary")),
    )(proj, gq, gkv, *tabs, dcqn, dckvn, dkrot)


def _mla_scores(qf, kvb, krb, i, j, row, lane):
    kf = jnp.where(lane < HALF, kvb, krb)
    s = _dot_nt(qf, kf) * MLA_SCALE
    s_pos = j * BLK + lane
    mask = (s_pos <= i * BLK + row) & (s_pos >= N_PAD)
    return kf, jnp.where(mask, s, NEG), mask


def _mla_fwd(qh, kvh, krot, proj, tabs, nbatch, nb):
    lp = nb * BLK
    t = nbatch * lp

    def body(q_ref, kv_ref, kr_ref, g_ref, c_ref, s1_ref, s2_ref, o_ref, og_ref, lse_ref, m_scr, l_scr, acc_scr):
        i = pl.program_id(2)
        row, lane = _iotas()
        lo_m = lane < HALF
        zero = jnp.zeros((BLK, BLK), F32)
        for hh in range(2):
            cols = slice(hh * BLK, (hh + 1) * BLK)
            qf = _rope(q_ref[:, cols].astype(F32), c_ref[...], s1_ref[...], s2_ref[...]).astype(BF16)
            m_scr[...] = jnp.full((BLK, BLK), NEG, F32)
            l_scr[...] = zero
            acc_scr[...] = zero

            def step(j, carry, qf=qf, cols=cols):
                off = pl.multiple_of(j * BLK, BLK)
                kvb = kv_ref[pl.ds(off, BLK), cols]
                krb = kr_ref[pl.ds(off, BLK), :]
                _, s, _ = _mla_scores(qf, kvb, krb, i, j, row, lane)
                m = m_scr[...]
                m2 = jnp.maximum(m, jnp.max(s, axis=1, keepdims=True))
                p = jnp.exp(s - m2)
                alpha = jnp.exp(m - m2)
                l_scr[...] = alpha * l_scr[...] + jnp.sum(p, axis=1, keepdims=True)
                acc_scr[...] = alpha * acc_scr[...] + _dot(p.astype(BF16), kvb)
                m_scr[...] = m2
                return carry

            lax.fori_loop(0, i + 1, step, 0)
            l = l_scr[...]
            oh = acc_scr[...] / l
            lse = jnp.where(lane == hh, m_scr[...] + jnp.log(l), 0.0)
            if hh == 0:
                o_ref[...] = pltpu.roll(oh, HALF, 1)
                lse_ref[...] = lse
            else:
                o_ref[...] = jnp.where(lo_m, o_ref[...], oh)
                lse_ref[...] += lse
        out = o_ref[...]
        g = g_ref[...].astype(F32)
        og_ref[...] = (out * g * _sigmoid(g)).astype(og_ref.dtype)

    tile = pl.BlockSpec((BLK, BLK), lambda b, p, i: (b * nb + i, p))
    tab = pl.BlockSpec((BLK, BLK), lambda b, p, i: (i, 0))
    return pl.pallas_call(
        body, name="mla_fwd", grid=(nbatch, 4, nb),
        in_specs=[pl.BlockSpec((BLK, 2 * BLK), lambda b, p, i: (b * nb + i, p)),
                  pl.BlockSpec((lp, 2 * BLK), lambda b, p, i: (b, p)),
                  pl.BlockSpec((lp, BLK), lambda b, p, i: (b, 0)),
                  pl.BlockSpec((BLK, BLK), lambda b, p, i: (b * nb + i, EV_GMLA_BLK + p)),
                  tab, tab, tab],
        out_specs=[tile, tile, tile],
        out_shape=[jax.ShapeDtypeStruct((t, 512), F32), jax.ShapeDtypeStruct((t, 512), BF16),
                   jax.ShapeDtypeStruct((t, 512), F32)],
        scratch_shapes=[pltpu.VMEM((BLK, BLK), F32)] * 3,
        compiler_params=_params(("parallel", "parallel", "arbitrary")),
    )(qh, kvh, krot, proj, *tabs)


def _mla_bwd(qh, kvh, krot, proj, tabs, o, dog, lse, nbatch, nb):
    lp = nb * BLK
    t = nbatch * lp

    def body(q_ref, kv_ref, kr_ref, g_ref, c_ref, s1_ref, s2_ref, o_ref, dog_ref, lse_ref,
             dq_ref, dkv_ref, dkr_ref, dg_ref, acc_scr):
        p_id = pl.program_id(1)
        i = pl.program_id(2)

        @pl.when(i == 0)
        def _():
            dkv_ref[...] = jnp.zeros_like(dkv_ref)

        @pl.when((i == 0) & (p_id == 0))
        def _():
            dkr_ref[...] = jnp.zeros_like(dkr_ref)

        row, lane = _iotas()
        lo_m = lane < HALF
        g = g_ref[...].astype(F32)
        sig = _sigmoid(g)
        dog_v = dog_ref[...]
        o_v = o_ref[...]
        dg_ref[...] = dog_v * o_v * (sig * (1.0 + g * (1.0 - sig)))
        do = dog_v * g * sig
        do_o = do * o_v
        lse_blk = lse_ref[...]
        zero = jnp.zeros((BLK, BLK), F32)
        for hh in range(2):
            cols = slice(hh * BLK, (hh + 1) * BLK)
            hm = (lane >= HALF) if hh else lo_m
            delta = jnp.sum(jnp.where(hm, do_o, 0.0), axis=1, keepdims=True)
            lse_h = jnp.sum(jnp.where(lane == hh, lse_blk, 0.0), axis=1, keepdims=True)
            do_src = do if hh else pltpu.roll(do, HALF, 1)
            do_hi = jnp.where(lo_m, 0.0, do_src).astype(BF16)
            qf = _rope(q_ref[:, cols].astype(F32), c_ref[...], s1_ref[...], s2_ref[...]).astype(BF16)

            acc_scr[...] = zero

            def step(j, carry, qf=qf, cols=cols, delta=delta, lse_h=lse_h, do_hi=do_hi):
                off = pl.multiple_of(j * BLK, BLK)
                kvb = kv_ref[pl.ds(off, BLK), cols]
                krb = kr_ref[pl.ds(off, BLK), :]
                kf, s, _ = _mla_scores(qf, kvb, krb, i, j, row, lane)
                p = jnp.exp(s - lse_h)
                dp = _dot_nt(do_hi, kvb)
                ds = (p * (dp - delta) * MLA_SCALE).astype(BF16)
                acc_scr[...] += _dot(ds, kf)
                dkf = _dot_tn(ds, qf)
                dvv = _dot_tn(p.astype(BF16), do_hi)
                dkv_ref[pl.ds(off, BLK), cols] += jnp.where(lo_m, dkf, 0.0) + dvv
                dkr_ref[pl.ds(off, BLK), :] += jnp.where(lo_m, 0.0, dkf)
                return carry

            lax.fori_loop(0, i + 1, step, 0)
            dq_ref[:, cols] = _rope_t(acc_scr[...], c_ref[...], s1_ref[...], s2_ref[...])

    tile = pl.BlockSpec((BLK, BLK), lambda b, p, i: (b * nb + i, p))
    wide = pl.BlockSpec((BLK, 2 * BLK), lambda b, p, i: (b * nb + i, p))
    full2 = pl.BlockSpec((lp, 2 * BLK), lambda b, p, i: (b, p))
    full1 = pl.BlockSpec((lp, BLK), lambda b, p, i: (b, 0))
    tab = pl.BlockSpec((BLK, BLK), lambda b, p, i: (i, 0))
    return pl.pallas_call(
        body, name="mla_bwd", grid=(nbatch, 4, nb),
        in_specs=[wide, full2, full1,
                  pl.BlockSpec((BLK, BLK), lambda b, p, i: (b * nb + i, EV_GMLA_BLK + p)),
                  tab, tab, tab, tile, pl.BlockSpec((BLK, BLK), lambda b, p, i: (b * nb + i, 4 + p)), tile],
        out_specs=[wide, full2, full1, tile],
        out_shape=[jax.ShapeDtypeStruct((t, 1024), F32), jax.ShapeDtypeStruct((t, 1024), F32),
                   jax.ShapeDtypeStruct((t, 128), F32), jax.ShapeDtypeStruct((t, 512), F32)],
        scratch_shapes=[pltpu.VMEM((BLK, BLK), F32)],
        compiler_params=_params(("parallel", "arbitrary", "arbitrary")),
    )(qh, kvh, krot, proj, *tabs, o, dog, lse)


def _swa_setup(kk, i, k_refs, v_refs):
    row, lane = _iotas()
    own = (lane >= kk * HALF) & (lane < (kk + 1) * HALF)

    def dup(ref):
        x = ref[...].astype(F32)
        return jnp.where(own, x, pltpu.roll(x, HALF, 1)).astype(BF16)

    ks = [dup(r) for r in k_refs]
    vs = [dup(r) for r in v_refs]
    t_pos = i * BLK + row
    d_meta = t_pos - lane
    s_prev = (i - 1) * BLK + lane
    s_own = i * BLK + lane
    d_prev = t_pos - s_prev
    d_own = t_pos - s_own
    masks = [(lane >= N_PAD) & (d_meta >= 0),
             (d_prev >= 0) & (d_prev < BLK) & (s_prev >= BLK),
             (d_own >= 0) & (d_own < BLK) & (s_own >= BLK)]
    dists = [d_meta.astype(F32), d_prev.astype(F32), d_own.astype(F32)]
    return row, lane, own, ks, vs, masks, dists


def _swa_slope(kk, g_idx):
    return (2.0 ** (-(g_idx + 1) / 2.0)) * jnp.where(kk == 0, 1.0, 1.0 / 16.0)


def _swa_fwd(proj, sinks, nbatch, nb):
    lp = nb * BLK
    t = nbatch * lp

    def body(sink_ref, q_ref, ka, kb, kc, va, vb, vc, g_ref, o_ref, og_ref, lse_ref):
        kk = pl.program_id(1)
        i = pl.program_id(2)
        row, lane, own, ks, vs, masks, dists = _swa_setup(kk, i, (ka, kb, kc), (va, vb, vc))
        zero = jnp.zeros((BLK, BLK), F32)
        lse_out = zero
        for m in range(4):
            cols = slice(m * BLK, (m + 1) * BLK)
            qp = q_ref[:, cols]
            outp = zero
            for hh in range(2):
                g_idx = 2 * m + hh
                slope = _swa_slope(kk, g_idx)
                sink = sink_ref[kk, g_idx]
                hm = (lane >= HALF) if hh else (lane < HALF)
                qm = jnp.where(hm, qp, jnp.zeros_like(qp))
                ss = [jnp.where(masks[x], _dot_nt(qm, ks[x]) * SWA_SCALE - slope * dists[x], NEG)
                      for x in range(3)]
                mx = jnp.maximum(jnp.maximum(jnp.max(ss[0], axis=1, keepdims=True),
                                             jnp.max(ss[1], axis=1, keepdims=True)),
                                 jnp.maximum(jnp.max(ss[2], axis=1, keepdims=True), sink))
                ps = [jnp.exp(s - mx) for s in ss]
                l = (jnp.exp(sink - mx) + jnp.sum(ps[0], axis=1, keepdims=True)
                     + jnp.sum(ps[1], axis=1, keepdims=True) + jnp.sum(ps[2], axis=1, keepdims=True))
                acc = (_dot(ps[0].astype(BF16), vs[0]) + _dot(ps[1].astype(BF16), vs[1])
                       + _dot(ps[2].astype(BF16), vs[2]))
                outp = jnp.where(hm, acc / l, outp)
                lse_out = lse_out + jnp.where(lane == g_idx, mx + jnp.log(l), 0.0)
            o_ref[:, cols] = outp
            g = g_ref[:, cols].astype(F32)
            og_ref[:, cols] = (outp * g * _sigmoid(g)).astype(og_ref.dtype)
        lse_ref[...] = lse_out

    def kvspec(col, which):
        if which == 0:
            return pl.BlockSpec((BLK, BLK), lambda b, kk, i: (b * nb, col))
        if which == 1:
            return pl.BlockSpec((BLK, BLK), lambda b, kk, i: (b * nb + jnp.maximum(i - 1, 0), col))
        return pl.BlockSpec((BLK, BLK), lambda b, kk, i: (b * nb + i, col))

    wide = lambda c0: pl.BlockSpec((BLK, 512), lambda b, kk, i: (b * nb + i, c0 + kk))
    return pl.pallas_call(
        body, name="swa_fwd", grid=(nbatch, 2, nb),
        in_specs=[pl.BlockSpec(memory_space=pltpu.SMEM), wide(0),
                  kvspec(OD_K_BLK, 0), kvspec(OD_K_BLK, 1), kvspec(OD_K_BLK, 2),
                  kvspec(OD_V_BLK, 0), kvspec(OD_V_BLK, 1), kvspec(OD_V_BLK, 2), wide(2)],
        out_specs=[wide(0), wide(0), pl.BlockSpec((BLK, BLK), lambda b, kk, i: (b * nb + i, kk))],
        out_shape=[jax.ShapeDtypeStruct((t, 1024), F32), jax.ShapeDtypeStruct((t, 1024), BF16),
                   jax.ShapeDtypeStruct((t, 256), F32)],
        compiler_params=_params(("parallel", "parallel", "arbitrary")),
    )(sinks, proj, proj, proj, proj, proj, proj, proj, proj)


def _swa_bwd(proj, sinks, o, dog, lse, nbatch, nb):
    lp = nb * BLK
    t = nbatch * lp

    def body(sink_ref, q_ref, ka, kb, kc, va, vb, vc, g_ref, o_ref, dog_ref, lse_ref,
             dq_ref, dg_ref, dk_ref, dv_ref, dsink_ref):
        kk = pl.program_id(1)
        i = pl.program_id(2)

        @pl.when((kk == 0) & (i == 0))
        def _():
            dk_ref[...] = jnp.zeros_like(dk_ref)
            dv_ref[...] = jnp.zeros_like(dv_ref)

        @pl.when(i == 0)
        def _():
            dsink_ref[...] = jnp.zeros_like(dsink_ref)

        row, lane, own, ks, vs, masks, dists = _swa_setup(kk, i, (ka, kb, kc), (va, vb, vc))
        row8 = lax.broadcasted_iota(jnp.int32, (8, BLK), 0)
        zero = jnp.zeros((BLK, BLK), F32)
        dks = [zero, zero, zero]
        dvs = [zero, zero, zero]
        dsink = jnp.zeros((8, BLK), F32)
        lse_blk = lse_ref[...]
        for m in range(4):
            cols = slice(m * BLK, (m + 1) * BLK)
            qp = q_ref[:, cols]
            g = g_ref[:, cols].astype(F32)
            sig = _sigmoid(g)
            dog_v = dog_ref[:, cols]
            o_v = o_ref[:, cols]
            dg_ref[:, cols] = dog_v * o_v * (sig * (1.0 + g * (1.0 - sig)))
            do = dog_v * g * sig
            do_o = do * o_v
            dob = do.astype(BF16)
            dqp = zero
            for hh in range(2):
                g_idx = 2 * m + hh
                slope = _swa_slope(kk, g_idx)
                sink = sink_ref[kk, g_idx]
                hm = (lane >= HALF) if hh else (lane < HALF)
                qm = jnp.where(hm, qp, jnp.zeros_like(qp))
                dom = jnp.where(hm, dob, jnp.zeros_like(dob))
                delta = jnp.sum(jnp.where(hm, do_o, 0.0), axis=1, keepdims=True)
                lse_h = jnp.sum(jnp.where(lane == g_idx, lse_blk, 0.0), axis=1, keepdims=True)
                dqa = zero
                for x in range(3):
                    s = jnp.where(masks[x], _dot_nt(qm, ks[x]) * SWA_SCALE - slope * dists[x], NEG)
                    p = jnp.exp(s - lse_h)
                    dp = _dot_nt(dom, vs[x])
                    ds = (p * (dp - delta) * SWA_SCALE).astype(BF16)
                    dqa = dqa + _dot(ds, ks[x])
                    dks[x] = dks[x] + _dot_tn(ds, qm)
                    dvs[x] = dvs[x] + _dot_tn(p.astype(BF16), dom)
                dqp = jnp.where(hm, dqa, dqp)
                tot = jnp.sum(-jnp.exp(sink - lse_h) * delta, axis=0, keepdims=True)
                dsink = dsink + jnp.where(row8 == g_idx, tot, 0.0)
            dq_ref[:, cols] = dqp
        dsink_ref[...] += dsink

        def fold(x):
            return jnp.where(own, x + pltpu.roll(x, HALF, 1), 0.0)

        offs = [0, pl.multiple_of(jnp.maximum(i - 1, 0) * BLK, BLK), pl.multiple_of(i * BLK, BLK)]
        for x in range(3):
            dk_ref[pl.ds(offs[x], BLK), :] += fold(dks[x])
            dv_ref[pl.ds(offs[x], BLK), :] += fold(dvs[x])

    def kvspec(col, which):
        if which == 0:
            return pl.BlockSpec((BLK, BLK), lambda b, kk, i: (b * nb, col))
        if which == 1:
            return pl.BlockSpec((BLK, BLK), lambda b, kk, i: (b * nb + jnp.maximum(i - 1, 0), col))
        return pl.BlockSpec((BLK, BLK), lambda b, kk, i: (b * nb + i, col))

    wide = lambda c0: pl.BlockSpec((BLK, 512), lambda b, kk, i: (b * nb + i, c0 + kk))
    full = pl.BlockSpec((lp, BLK), lambda b, kk, i: (b, 0))
    return pl.pallas_call(
        body, name="swa_bwd", grid=(nbatch, 2, nb),
        in_specs=[pl.BlockSpec(memory_space=pltpu.SMEM), wide(0),
                  kvspec(OD_K_BLK, 0), kvspec(OD_K_BLK, 1), kvspec(OD_K_BLK, 2),
                  kvspec(OD_V_BLK, 0), kvspec(OD_V_BLK, 1), kvspec(OD_V_BLK, 2), wide(2),
                  wide(0), wide(0), pl.BlockSpec((BLK, BLK), lambda b, kk, i: (b * nb + i, kk))],
        out_specs=[wide(0), wide(0), full, full,
                   pl.BlockSpec((8, BLK), lambda b, kk, i: (b * 2 + kk, 0))],
        out_shape=[jax.ShapeDtypeStruct((t, 1024), F32), jax.ShapeDtypeStruct((t, 1024), F32),
                   jax.ShapeDtypeStruct((t, 128), F32), jax.ShapeDtypeStruct((t, 128), F32),
                   jax.ShapeDtypeStruct((nbatch * 16, BLK), F32)],
        compiler_params=_params(("parallel", "arbitrary", "arbitrary")),
    )(sinks, proj, proj, proj, proj, proj, proj, proj, proj, o, dog, lse)


def _rope_tables(lp):
    pos = (jnp.arange(lp) - N_PAD).astype(F32)
    inv = ROPE_BASE ** (-jnp.arange(16, dtype=F32) / 16.0)
    ang = pos[:, None] * inv[None, :]
    cos, sin = jnp.cos(ang), jnp.sin(ang)
    z16 = jnp.zeros((lp, 16), F32)
    c = jnp.concatenate([jnp.ones((lp, 64), F32), cos, cos, jnp.zeros((lp, 32), F32)], axis=1)
    s1 = jnp.concatenate([jnp.zeros((lp, 64), F32), -sin, z16, jnp.zeros((lp, 32), F32)], axis=1)
    s2 = jnp.concatenate([jnp.zeros((lp, 64), F32), z16, sin, jnp.zeros((lp, 32), F32)], axis=1)
    return c, s1, s2


def _local_step(h0, tgt, norm_g, final_g, gq, gkv, sinks, w_ie, w_uq, w_ukv, w_oe, w_io, w_oo, nbatch, nb):
    lp = nb * BLK
    tabs = _rope_tables(lp)
    g0, g1 = norm_g[0:1], norm_g[1:2]
    sinks2 = sinks.reshape(2, 8)

    hn0 = _rms_fwd(h0, g0, "rms_fwd0")
    proj_e = _mm(hn0, w_ie, "nn", "proj_even", out_dtype=BF16)
    o_sb, og_sb, cm = _sb_fwd(proj_e, nbatch, nb)
    cqn, ckvn, krot = _mla_prep(proj_e, gq, gkv, tabs, nbatch, nb)
    qh = _mm(cqn, w_uq, "nn", "mla_uq", out_dtype=BF16)
    kvh = _mm(ckvn, w_ukv, "nn", "mla_ukv", out_dtype=BF16)
    o_mla, og_mla, lse_m = _mla_fwd(qh, kvh, krot, proj_e, tabs, nbatch, nb)
    og_e = jnp.concatenate([og_sb, og_mla], axis=1)
    h1 = _mm(og_e, w_oe, "nn", "out_even", add=h0)
    hn1 = _rms_fwd(h1, g1, "rms_fwd1")
    proj_o = _mm(hn1, w_io, "nn", "proj_odd", out_dtype=BF16)
    o_o, og_o, lse_o = _swa_fwd(proj_o, sinks2, nbatch, nb)
    h2 = _mm(og_o, w_oo, "nn", "out_odd", add=h1)
    dh2, lossv, d_final_g = _final(h2, tgt, final_g, nbatch, nb)

    dog_o = _mm(dh2, w_oo, "nt", "d_out_odd")
    d_w_oo = _mm_tn(og_o, dh2, "dw_out_odd")
    dq_o, dg_o, dk_o, dv_o, dsink = _swa_bwd(proj_o, sinks2, o_o, dog_o, lse_o, nbatch, nb)
    dproj_o = jnp.concatenate([dq_o, dg_o, dk_o, dv_o], axis=1).astype(BF16)
    dhn1 = _mm(dproj_o, w_io, "nt", "d_proj_odd")
    d_w_io = _mm_tn(hn1, dproj_o, "dw_proj_odd")
    dh1, d_g1 = _rms_bwd(h1, g1, dhn1, dh2, "rms_bwd1")

    dog_e = _mm(dh1, w_oe, "nt", "d_out_even")
    d_w_oe = _mm_tn(og_e, dh1, "dw_out_even")
    dq_sb, dk_sb, dv_sb, dg_sb = _sb_bwd(proj_e, o_sb, dog_e, cm, nbatch, nb)
    dqh, dkvh, dkrot, dg_mla = _mla_bwd(qh, kvh, krot, proj_e, tabs, o_mla, dog_e, lse_m, nbatch, nb)
    dcqn = _mm(dqh, w_uq, "nt", "d_mla_uq")
    d_w_uq = _mm_tn(cqn, dqh, "dw_mla_uq")
    dckvn = _mm(dkvh, w_ukv, "nt", "d_mla_ukv")
    d_w_ukv = _mm_tn(ckvn, dkvh, "dw_mla_ukv")
    dmid, d_gq, d_gkv = _mla_prep_bwd(proj_e, gq, gkv, tabs, dcqn, dckvn, dkrot, nbatch, nb)
    dproj_e = jnp.concatenate([dq_sb, dk_sb, dv_sb, dg_sb, dmid, dg_mla], axis=1).astype(BF16)
    dhn0 = _mm(dproj_e, w_ie, "nt", "d_proj_even")
    d_w_ie = _mm_tn(hn0, dproj_e, "dw_proj_even")
    dh0, d_g0 = _rms_bwd(h0, g0, dhn0, dh1, "rms_bwd0")

    d_sinks = dsink.reshape(nbatch, 2, 8, BLK)[:, :, :, 0].sum(axis=0).reshape(1, 16)
    d_norm_g = jnp.concatenate([d_g0, d_g1], axis=0)
    return dict(lossv=lossv, dh0=dh0, norm_g=d_norm_g, final_g=d_final_g, gq=d_gq, gkv=d_gkv, sinks=d_sinks,
                w_ie=d_w_ie, w_uq=d_w_uq, w_ukv=d_w_ukv, w_oe=d_w_oe, w_io=d_w_io, w_oo=d_w_oo)


def _ev_in_to_compute(w):
    z = lambda n: jnp.zeros((w.shape[0], n), w.dtype)
    return jnp.concatenate([w[:, :2432], z(64), w[:, 2432:2464], z(32), w[:, 2464:]], axis=1)


def _ev_in_from_compute(w):
    return jnp.concatenate([w[:, :2432], w[:, 2496:2528], w[:, 2560:]], axis=1)


def _uq_to_compute(w):
    w3 = w.reshape(256, 8, 96)
    return jnp.concatenate([w3, jnp.zeros((256, 8, 32), w.dtype)], axis=2).reshape(256, 1024)


def _uq_from_compute(w):
    return w.reshape(256, 8, 128)[:, :, :96].reshape(256, 768)


def _od_in_to_compute(w):
    return jnp.concatenate([w[:, :1024], w[:, 1280:], w[:, 1024:1280]], axis=1)


def _od_in_from_compute(w):
    return jnp.concatenate([w[:, :1024], w[:, 2048:], w[:, 1024:2048]], axis=1)


_BIG = (("ev_w_in", 1024, 2976, 1), ("ev_w_uq", 256, 768, 1), ("ev_w_ukv", 128, 1024, 1),
        ("ev_w_out", 1024, 1024, 0), ("od_w_in", 1024, 2304, 1), ("od_w_out", 1024, 1024, 0))


def _shard_rows(rows, cols):
    return rows * cols // N_DEV // BLK


def _all_gather(x_shard):
    m_per, n = x_shard.shape

    def body(x_ref, out_ref, send_sems, recv_sems, local_sem):
        x, y, c = lax.axis_index("x"), lax.axis_index("y"), lax.axis_index("c")
        me, sibling = (x, y, c), (x, y, 1 - c)
        chips = [(1 - x, y), (x, 1 - y), (1 - x, 1 - y)]

        def rows(px, py, pc):
            return out_ref.at[pl.ds((4 * px + 2 * py + pc) * m_per, m_per), :]

        def copy(k, block, to, src=None):
            return pltpu.make_async_remote_copy(
                src_ref=rows(*block) if src is None else src, dst_ref=rows(*block),
                send_sem=send_sems.at[k], recv_sem=recv_sems.at[k],
                device_id=to, device_id_type=pl.DeviceIdType.MESH)

        mine = pltpu.make_async_copy(x_ref, rows(*me), local_sem)
        mine.start()
        first = [copy(0, me, sibling, src=x_ref)]
        first += [copy(1 + j, me, (*chip, c), src=x_ref) for j, chip in enumerate(chips)]
        for cp in first:
            cp.start()
        passed = [copy(4 + j, (*chip, c), sibling) for j, chip in enumerate(chips)]
        for j, chip in enumerate(chips):
            copy(1 + j, (*chip, c), me).wait_recv()
            passed[j].start()
        copy(0, sibling, me).wait_recv()
        for j, chip in enumerate(chips):
            copy(4 + j, (*chip, 1 - c), me).wait_recv()
        for cp in first + passed:
            cp.wait_send()
        mine.wait()

    return pl.pallas_call(
        body, name="gather_weights",
        out_shape=jax.ShapeDtypeStruct((N_DEV * m_per, n), x_shard.dtype),
        in_specs=[pl.BlockSpec(memory_space=pltpu.VMEM)],
        out_specs=pl.BlockSpec(memory_space=pltpu.VMEM),
        scratch_shapes=[pltpu.SemaphoreType.DMA((7,)), pltpu.SemaphoreType.DMA((7,)), pltpu.SemaphoreType.DMA],
        compiler_params=pltpu.CompilerParams(vmem_limit_bytes=48 << 20),
    )(x_shard)


def _reduce_scatter(big, small):
    _, rb, _ = big.shape
    _, rs, _ = small.shape
    chunk = 256

    def body(big_ref, small_ref, obig_ref, osmall_ref, rbig, rsmall, ssem, rsem, ssem2, rsem2, lsem):
        x, y, c = lax.axis_index("x"), lax.axis_index("y"), lax.axis_index("c")
        me = 4 * x + 2 * y + c
        own_b = pltpu.make_async_copy(big_ref.at[me], rbig.at[me], lsem.at[0])
        own_s = pltpu.make_async_copy(small_ref.at[me], rsmall.at[me], lsem.at[1])
        own_b.start()
        own_s.start()
        sends, recvs = [], []
        for d in range(1, N_DEV):
            px = x + ((d >> 2) & 1) - 2 * x * ((d >> 2) & 1)
            py = y + ((d >> 1) & 1) - 2 * y * ((d >> 1) & 1)
            pc = c + (d & 1) - 2 * c * (d & 1)
            pid = 4 * px + 2 * py + pc
            kw = dict(device_id=(px, py, pc), device_id_type=pl.DeviceIdType.MESH)
            sends.append(pltpu.make_async_remote_copy(
                src_ref=big_ref.at[pid], dst_ref=rbig.at[me], send_sem=ssem.at[d - 1], recv_sem=rsem.at[d - 1], **kw))
            sends.append(pltpu.make_async_remote_copy(
                src_ref=small_ref.at[pid], dst_ref=rsmall.at[me], send_sem=ssem2.at[d - 1], recv_sem=rsem2.at[d - 1], **kw))
            recvs.append(pltpu.make_async_remote_copy(
                src_ref=big_ref.at[pid], dst_ref=rbig.at[pid], send_sem=ssem.at[d - 1], recv_sem=rsem.at[d - 1], **kw))
            recvs.append(pltpu.make_async_remote_copy(
                src_ref=small_ref.at[pid], dst_ref=rsmall.at[pid], send_sem=ssem2.at[d - 1], recv_sem=rsem2.at[d - 1], **kw))
        for cp in sends:
            cp.start()
        own_b.wait()
        own_s.wait()
        for cp in recvs:
            cp.wait_recv()
        for cp in sends:
            cp.wait_send()

        acc = rsmall[0]
        for k in range(1, N_DEV):
            acc = acc + rsmall[k]
        osmall_ref[...] = acc

        def step(r, carry):
            off = pl.multiple_of(r * chunk, chunk)
            a = rbig[0, pl.ds(off, chunk), :].astype(F32)
            for k in range(1, N_DEV):
                a = a + rbig[k, pl.ds(off, chunk), :].astype(F32)
            obig_ref[pl.ds(off, chunk), :] = a
            return carry

        lax.fori_loop(0, rb // chunk, step, 0)

    return pl.pallas_call(
        body, name="reduce_grads",
        out_shape=[jax.ShapeDtypeStruct((rb, BLK), F32), jax.ShapeDtypeStruct((rs, BLK), F32)],
        in_specs=[pl.BlockSpec(memory_space=pl.ANY), pl.BlockSpec(memory_space=pl.ANY)],
        out_specs=[pl.BlockSpec(memory_space=pltpu.VMEM), pl.BlockSpec(memory_space=pltpu.VMEM)],
        scratch_shapes=[pltpu.VMEM((N_DEV, rb, BLK), BF16), pltpu.VMEM((N_DEV, rs, BLK), F32),
                        pltpu.SemaphoreType.DMA((7,)), pltpu.SemaphoreType.DMA((7,)),
                        pltpu.SemaphoreType.DMA((7,)), pltpu.SemaphoreType.DMA((7,)),
                        pltpu.SemaphoreType.DMA((2,))],
        compiler_params=pltpu.CompilerParams(vmem_limit_bytes=48 << 20),
    )(big, small)


def _adamw(ws, gs, ms, vs):
    n = len(ws)

    def body(*refs):
        ins, outs = refs[:4 * n], refs[4 * n:]
        for k in range(n):
            w_ref, g_ref, m_ref, v_ref = ins[4 * k:4 * k + 4]
            d_ref, nm_ref, nv_ref = outs[3 * k:3 * k + 3]

            def update(sl, w_ref=w_ref, g_ref=g_ref, m_ref=m_ref, v_ref=v_ref,
                       d_ref=d_ref, nm_ref=nm_ref, nv_ref=nv_ref):
                g = g_ref[sl]
                m = ADAM_B1 * m_ref[sl] + (1.0 - ADAM_B1) * g
                v = ADAM_B2 * v_ref[sl] + (1.0 - ADAM_B2) * (g * g)
                m_hat = m / (1.0 - ADAM_B1 ** ADAM_STEP)
                v_hat = v / (1.0 - ADAM_B2 ** ADAM_STEP)
                d_ref[sl] = -ADAM_LR * (m_hat / (jnp.sqrt(v_hat) + ADAM_EPS) + ADAM_WD * w_ref[sl])
                nm_ref[sl] = m
                nv_ref[sl] = v

            rows = w_ref.shape[0]
            if rows > BLK and rows % BLK == 0:
                def step(r, carry, update=update):
                    update((pl.ds(pl.multiple_of(r * BLK, BLK), BLK), slice(None)))
                    return carry

                lax.fori_loop(0, rows // BLK, step, 0)
            else:
                update((slice(None), slice(None)))

    args, out_shape = [], []
    for k in range(n):
        args += [ws[k], gs[k], ms[k], vs[k]]
        out_shape += [jax.ShapeDtypeStruct(ws[k].shape, F32)] * 3
    vm = pl.BlockSpec(memory_space=pltpu.VMEM)
    outs = pl.pallas_call(
        body, name="adamw", out_shape=out_shape,
        in_specs=[vm] * (4 * n), out_specs=[vm] * (3 * n),
        compiler_params=pltpu.CompilerParams(vmem_limit_bytes=48 << 20),
    )(*args)
    return [tuple(outs[3 * k:3 * k + 3]) for k in range(n)]


def kernel(x, meta, norm_g, final_g, ev_w_in, ev_q_norm_g, ev_kv_norm_g, ev_w_uq, ev_w_ukv, ev_w_out, od_w_in, od_sinks, od_w_out, loss_target, m_meta, m_norm_g, m_final_g, m_ev_w_in, m_ev_q_norm_g, m_ev_kv_norm_g, m_ev_w_uq, m_ev_w_ukv, m_ev_w_out, m_od_w_in, m_od_sinks, m_od_w_out, v_meta, v_norm_g, v_final_g, v_ev_w_in, v_ev_q_norm_g, v_ev_kv_norm_g, v_ev_w_uq, v_ev_w_ukv, v_ev_w_out, v_od_w_in, v_od_sinks, v_od_w_out):
    nbatch, seq, d = x.shape
    nb = seq // BLK + 1
    lp = nb * BLK
    shards = dict(ev_w_in=ev_w_in[0], ev_w_uq=ev_w_uq[0], ev_w_ukv=ev_w_ukv[0], ev_w_out=ev_w_out[0],
                  od_w_in=od_w_in[0], od_w_out=od_w_out[0])

    parts = [shards[name].astype(BF16).reshape(-1, BLK) for name, _, _, _ in _BIG]
    parts.append(lax.bitcast_convert_type(meta, BF16).reshape(2 * N_META, BLK))
    gathered = _all_gather(jnp.concatenate(parts, axis=0)).reshape(N_DEV, PACK_ROWS, BLK)
    full, off = {}, 0
    for name, rows, cols, axis in _BIG:
        n = _shard_rows(rows, cols)
        blk = gathered[:, off:off + n]
        off += n
        if axis == 1:
            full[name] = blk.reshape(N_DEV, rows, cols // N_DEV).transpose(1, 0, 2).reshape(rows, cols)
        else:
            full[name] = blk.reshape(rows, cols)
    meta_bits = gathered[:, off:off + 2 * N_META].reshape(N_DEV, N_META, BLK, 2)
    meta_full = lax.bitcast_convert_type(meta_bits, F32).transpose(1, 0, 2).reshape(N_META, d)

    head = jnp.concatenate([jnp.zeros((N_PAD, d), F32), meta_full], axis=0)
    h0 = jnp.concatenate([jnp.broadcast_to(head[None], (nbatch, BLK, d)), x], axis=1).reshape(nbatch * lp, d)
    grads = _local_step(
        h0, loss_target.reshape(nbatch * seq, d), norm_g, final_g.reshape(1, d), ev_q_norm_g, ev_kv_norm_g,
        od_sinks, _ev_in_to_compute(full["ev_w_in"]), _uq_to_compute(full["ev_w_uq"]), full["ev_w_ukv"],
        full["ev_w_out"], _od_in_to_compute(full["od_w_in"]), full["od_w_out"], nbatch, nb)
    dh0 = grads["dh0"].reshape(nbatch, lp, d)
    grad_x = dh0[:, BLK:]
    loss = lax.psum(0.5 / d * jnp.sum(grads["lossv"]), ("x", "y", "c"))

    gfull = dict(ev_w_in=_ev_in_from_compute(grads["w_ie"]), ev_w_uq=_uq_from_compute(grads["w_uq"]),
                 ev_w_ukv=grads["w_ukv"], ev_w_out=grads["w_oe"],
                 od_w_in=_od_in_from_compute(grads["w_io"]), od_w_out=grads["w_oo"])
    chunks = []
    for name, rows, cols, axis in _BIG:
        g = gfull[name]
        if axis == 1:
            g = g.reshape(rows, N_DEV, cols // N_DEV).transpose(1, 0, 2)
        chunks.append(g.reshape(N_DEV, _shard_rows(rows, cols), BLK))
    chunks.append(jnp.zeros((N_DEV, 2 * N_META, BLK), F32))
    big = jnp.concatenate(chunks, axis=1).astype(BF16)
    d_meta = dh0[:, N_PAD:BLK].sum(axis=0).reshape(N_META, N_DEV, BLK).transpose(1, 0, 2)
    pad = lambda a, n: jnp.concatenate([a.reshape(1, -1), jnp.zeros((1, n - a.size), F32)], axis=1)
    rep = jnp.concatenate([grads["norm_g"].reshape(1, -1), grads["final_g"], pad(grads["gq"], 512),
                           pad(grads["gkv"], 256), pad(grads["sinks"], 256)], axis=1).reshape(32, BLK)
    small = jnp.concatenate([d_meta, jnp.broadcast_to(rep[None], (N_DEV, 32, BLK))], axis=1)
    red_big, red_small = _reduce_scatter(big, small)

    g_shard, off = {}, 0
    for name, rows, cols, axis in _BIG:
        n = _shard_rows(rows, cols)
        shape = (rows, cols // N_DEV) if axis == 1 else (rows // N_DEV, cols)
        g_shard[name] = red_big[off:off + n].reshape(shape)
        off += n
    rep = red_small[N_META:].reshape(1, -1)
    g_small = dict(meta=red_small[:N_META], norm_g=rep[:, :2 * d].reshape(2, d), final_g=rep[:, 2 * d:3 * d],
                   ev_q_norm_g=rep[:, 3 * d:3 * d + 256], ev_kv_norm_g=rep[:, 3 * d + 512:3 * d + 640],
                   od_sinks=rep[:, 3 * d + 768:3 * d + 784])

    names = ["meta", "norm_g", "final_g", "ev_w_in", "ev_q_norm_g", "ev_kv_norm_g", "ev_w_uq", "ev_w_ukv",
             "ev_w_out", "od_w_in", "od_sinks", "od_w_out"]
    given = dict(meta=(meta, m_meta, v_meta), norm_g=(norm_g, m_norm_g, v_norm_g),
                 final_g=(final_g, m_final_g, v_final_g), ev_w_in=(ev_w_in, m_ev_w_in, v_ev_w_in),
                 ev_q_norm_g=(ev_q_norm_g, m_ev_q_norm_g, v_ev_q_norm_g),
                 ev_kv_norm_g=(ev_kv_norm_g, m_ev_kv_norm_g, v_ev_kv_norm_g),
                 ev_w_uq=(ev_w_uq, m_ev_w_uq, v_ev_w_uq), ev_w_ukv=(ev_w_ukv, m_ev_w_ukv, v_ev_w_ukv),
                 ev_w_out=(ev_w_out, m_ev_w_out, v_ev_w_out), od_w_in=(od_w_in, m_od_w_in, v_od_w_in),
                 od_sinks=(od_sinks, m_od_sinks, v_od_sinks), od_w_out=(od_w_out, m_od_w_out, v_od_w_out))
    ws, gs, ms, vs = [], [], [], []
    for name in names:
        g2 = g_shard[name] if name in g_shard else g_small[name]
        w, m, v = given[name]
        ws.append(w.reshape(g2.shape))
        ms.append(m.reshape(g2.shape))
        vs.append(v.reshape(g2.shape))
        gs.append(g2)
    upd = _adamw(ws, gs, ms, vs)
    shape_of = {name: given[name][0].shape for name in names}
    grads_out = [gs[k].reshape(shape_of[n]) for k, n in enumerate(names)]
    deltas = [upd[k][0].reshape(shape_of[n]) for k, n in enumerate(names)]
    new_m = [upd[k][1].reshape(shape_of[n]) for k, n in enumerate(names)]
    new_v = [upd[k][2].reshape(shape_of[n]) for k, n in enumerate(names)]
    return (loss, grad_x, *grads_out, *deltas, *new_m, *new_v)
```

```python
import jax
import jax.numpy as jnp
from jax import lax
from jax.experimental import pallas as pl
from jax.experimental.pallas import tpu as pltpu

F32 = jnp.float32
BF16 = jnp.bfloat16

D_MODEL = 1024
N_META = 16
BLK = 128
HALF = 64
N_PAD = BLK - N_META
NORM_EPS = 1e-6
NEG = -1e30
N_DEV = 8

SB_SCALE = 64 ** -0.5
MLA_SCALE = 96 ** -0.5
SWA_SCALE = 64 ** -0.5
ROPE_BASE = 10000.0

EV_IN_PAD = 3072
EV_MID_BLK = 4
EV_GMLA_BLK = 20
OD_K_BLK = 16
OD_V_BLK = 17

ADAM_LR = 0.001
ADAM_B1 = 0.9
ADAM_B2 = 0.999
ADAM_EPS = 1e-08
ADAM_WD = 0.01
ADAM_STEP = 10

PACK_ROWS = 7680


def _dot(a, b):
    return lax.dot_general(a, b, (((1,), (0,)), ((), ())), preferred_element_type=F32)


def _dot_nt(a, b):
    return lax.dot_general(a, b, (((1,), (1,)), ((), ())), preferred_element_type=F32)


def _dot_tn(a, b):
    return lax.dot_general(a, b, (((0,), (0,)), ((), ())), preferred_element_type=F32)


def _split(x):
    hi = x.astype(BF16)
    return hi, (x - hi.astype(F32)).astype(BF16)


def _sigmoid(x):
    return 1.0 / (1.0 + jnp.exp(-x))


def _iotas():
    row = lax.broadcasted_iota(jnp.int32, (BLK, BLK), 0)
    lane = lax.broadcasted_iota(jnp.int32, (BLK, BLK), 1)
    return row, lane


def _rope(x, c, s1, s2):
    return x * c + pltpu.roll(x, BLK - 16, 1) * s1 + pltpu.roll(x, 16, 1) * s2


def _rope_t(x, c, s1, s2):
    return x * c - pltpu.roll(x, BLK - 16, 1) * s1 - pltpu.roll(x, 16, 1) * s2


def _params(sem, vmem_mb=None):
    kw = dict(dimension_semantics=sem)
    if vmem_mb is not None:
        kw["vmem_limit_bytes"] = vmem_mb << 20
    return pltpu.CompilerParams(**kw)


def _row_tile(t, cands):
    for c in cands:
        if t % c == 0:
            return c
    raise ValueError(t)


def _mm(a, w, mode, name, add=None, out_dtype=F32):
    m, k = a.shape
    n = w.shape[1] if mode == "nn" else w.shape[0]
    tm = _row_tile(m, (256, 128))

    def body(*refs):
        if add is None:
            a_ref, w_ref, o_ref = refs
        else:
            a_ref, w_ref, r_ref, o_ref = refs
        x = a_ref[...].astype(BF16)
        acc = _dot(x, w_ref[...]) if mode == "nn" else _dot_nt(x, w_ref[...])
        if add is not None:
            acc = acc + r_ref[...]
        o_ref[...] = acc.astype(o_ref.dtype)

    in_specs = [pl.BlockSpec((tm, k), lambda i: (i, 0)), pl.BlockSpec(w.shape, lambda i: (0, 0))]
    args = [a, w]
    if add is not None:
        in_specs.append(pl.BlockSpec((tm, n), lambda i: (i, 0)))
        args.append(add)
    return pl.pallas_call(
        body, name=name, grid=(m // tm,), in_specs=in_specs,
        out_specs=pl.BlockSpec((tm, n), lambda i: (i, 0)),
        out_shape=jax.ShapeDtypeStruct((m, n), out_dtype),
        compiler_params=_params(("parallel",), 48),
    )(*args)


def _mm_tn(x, dy, name):
    t, k = x.shape
    n = dy.shape[1]
    tt = _row_tile(t, (1088, 256, 128))
    tn = n if n <= 1152 else _row_tile(n, (1152, 1024))

    def body(x_ref, dy_ref, o_ref):
        @pl.when(pl.program_id(1) == 0)
        def _():
            o_ref[...] = jnp.zeros_like(o_ref)

        o_ref[...] += _dot_tn(x_ref[...].astype(BF16), dy_ref[...].astype(BF16))

    return pl.pallas_call(
        body, name=name, grid=(n // tn, t // tt),
        in_specs=[pl.BlockSpec((tt, k), lambda j, i: (i, 0)), pl.BlockSpec((tt, tn), lambda j, i: (i, j))],
        out_specs=pl.BlockSpec((k, tn), lambda j, i: (0, j)),
        out_shape=jax.ShapeDtypeStruct((k, n), F32),
        compiler_params=_params(("parallel", "arbitrary"), 48),
    )(x, dy)


def _rms_fwd(h, g, name):
    t, d = h.shape
    tr = _row_tile(t, (256, 128))

    def body(h_ref, g_ref, o_ref):
        x = h_ref[...]
        r = lax.rsqrt(jnp.mean(x * x, axis=1, keepdims=True) + NORM_EPS)
        o_ref[...] = (x * r * g_ref[...]).astype(o_ref.dtype)

    return pl.pallas_call(
        body, name=name, grid=(t // tr,),
        in_specs=[pl.BlockSpec((tr, d), lambda i: (i, 0)), pl.BlockSpec((1, d), lambda i: (0, 0))],
        out_specs=pl.BlockSpec((tr, d), lambda i: (i, 0)),
        out_shape=jax.ShapeDtypeStruct((t, d), BF16),
        compiler_params=_params(("parallel",)),
    )(h, g)


def _rms_bwd(h, g, dhn, dres, name):
    t, d = h.shape
    tr = _row_tile(t, (256, 128))

    def body(h_ref, g_ref, dhn_ref, dres_ref, dh_ref, dg_ref):
        @pl.when(pl.program_id(0) == 0)
        def _():
            dg_ref[...] = jnp.zeros_like(dg_ref)

        x = h_ref[...]
        r = lax.rsqrt(jnp.mean(x * x, axis=1, keepdims=True) + NORM_EPS)
        nx = x * r
        dy = dhn_ref[...]
        dn = dy * g_ref[...]
        dh_ref[...] = dres_ref[...] + r * (dn - nx * jnp.mean(dn * nx, axis=1, keepdims=True))
        dg_ref[...] += jnp.sum(dy * nx, axis=0, keepdims=True)

    row = pl.BlockSpec((tr, d), lambda i: (i, 0))
    vec = pl.BlockSpec((1, d), lambda i: (0, 0))
    return pl.pallas_call(
        body, name=name, grid=(t // tr,),
        in_specs=[row, vec, row, row], out_specs=[row, vec],
        out_shape=[jax.ShapeDtypeStruct((t, d), F32), jax.ShapeDtypeStruct((1, d), F32)],
        compiler_params=_params(("arbitrary",)),
    )(h, g, dhn, dres)


def _final(h2, tgt, g, nbatch, nb):
    t, d = h2.shape

    def body(h_ref, t_ref, g_ref, dh_ref, loss_ref, dg_ref):
        b = pl.program_id(0)
        i = pl.program_id(1)

        @pl.when((b == 0) & (i == 0))
        def _():
            loss_ref[...] = jnp.zeros_like(loss_ref)
            dg_ref[...] = jnp.zeros_like(dg_ref)

        x = h_ref[...]
        r = lax.rsqrt(jnp.mean(x * x, axis=1, keepdims=True) + NORM_EPS)
        nx = x * r
        gg = g_ref[...]
        live = jnp.where(i >= 1, 1.0, 0.0)
        err = (nx * gg - t_ref[...]) * live
        loss_ref[...] += jnp.sum(err * err, axis=0, keepdims=True)
        dy = err * (1.0 / d)
        dn = dy * gg
        dh_ref[...] = r * (dn - nx * jnp.mean(dn * nx, axis=1, keepdims=True))
        dg_ref[...] += jnp.sum(dy * nx, axis=0, keepdims=True)

    vec = pl.BlockSpec((1, d), lambda b, i: (0, 0))
    return pl.pallas_call(
        body, name="final_loss", grid=(nbatch, nb),
        in_specs=[pl.BlockSpec((BLK, d), lambda b, i: (b * nb + i, 0)),
                  pl.BlockSpec((BLK, d), lambda b, i: (b * (nb - 1) + jnp.maximum(i - 1, 0), 0)),
                  vec],
        out_specs=[pl.BlockSpec((BLK, d), lambda b, i: (b * nb + i, 0)), vec, vec],
        out_shape=[jax.ShapeDtypeStruct((t, d), F32), jax.ShapeDtypeStruct((1, d), F32),
                   jax.ShapeDtypeStruct((1, d), F32)],
        compiler_params=_params(("arbitrary", "arbitrary")),
    )(h2, tgt, g)


def _sb_logits(qk):
    z = qk * SB_SCALE
    sp = jnp.log(1.0 + jnp.exp(-jnp.abs(z)))
    log_beta = jnp.minimum(z, 0.0) - sp
    log_1m = -jnp.maximum(z, 0.0) - sp
    return log_beta, log_1m


def _tri_ones(tri):
    return jnp.concatenate([tri.astype(BF16), jnp.ones((BLK, BLK), BF16)], axis=1)


def _sb_fwd(proj, nbatch, nb):
    lp = nb * BLK
    t = nbatch * lp

    def body(q_ref, k_ref, v_ref, g_ref, o_ref, og_ref, cm_ref, c_scr):
        i = pl.program_id(1)
        row, lane = _iotas()
        lo_m = lane < HALF
        upper = _tri_ones(row > lane)
        t_pos = i * BLK + row
        cm_ref[...] = jnp.zeros_like(cm_ref)
        c_scr[...] = jnp.zeros_like(c_scr)
        o_ref[...] = jnp.zeros_like(o_ref)

        def step(jj, carry):
            j = i - jj
            off = pl.multiple_of(j * BLK, BLK)
            s_pos = j * BLK + lane
            mask = (s_pos < t_pos) & (s_pos >= N_PAD)
            onehot = lane == j
            heads = range(8)
            pcs = [slice(p * BLK, (p + 1) * BLK) for p in range(4)]
            qs = [q_ref[:, pc] for pc in pcs]
            kbs = [k_ref[pl.ds(off, BLK), pc] for pc in pcs]
            vbs = [v_ref[pl.ds(off, BLK), pc] for pc in pcs]
            zs = [_dot_nt(jnp.where((lane >= HALF) if h % 2 else lo_m, qs[h // 2], jnp.zeros_like(qs[0])),
                          kbs[h // 2]) for h in heads]
            lbs, parts = [], []
            for h in heads:
                log_beta, log_1m = _sb_logits(zs[h])
                lbs.append(log_beta)
                parts.append(_split(jnp.where(mask, log_1m, 0.0)))
            css = [_dot(parts[h][0], upper) + _dot(parts[h][1], upper) for h in heads]
            avs = []
            for h in heads:
                c = c_scr[h]
                avs.append(jnp.where(mask, jnp.exp(lbs[h] + css[h][:, :BLK] + c), 0.0).astype(BF16))
                cm_ref[:, h * BLK:(h + 1) * BLK] += jnp.where(onehot, c, 0.0)
                c_scr[h] = c + css[h][:, BLK:]
            accs = [_dot(avs[h], vbs[h // 2]) for h in heads]
            for p in range(4):
                o_ref[:, pcs[p]] += jnp.where(lo_m, accs[2 * p], accs[2 * p + 1])
            return carry

        lax.fori_loop(0, i + 1, step, 0)
        g = g_ref[...].astype(F32)
        og_ref[...] = (o_ref[...] * g * _sigmoid(g)).astype(og_ref.dtype)

    tile = lambda col: pl.BlockSpec((BLK, 512), lambda b, i: (b * nb + i, col))
    full = lambda col: pl.BlockSpec((lp, 512), lambda b, i: (b, col))
    return pl.pallas_call(
        body, name="sb_fwd", grid=(nbatch, nb),
        in_specs=[tile(0), full(1), full(2), tile(3)],
        out_specs=[tile(0), tile(0), pl.BlockSpec((BLK, 1024), lambda b, i: (b * nb + i, 0))],
        out_shape=[jax.ShapeDtypeStruct((t, 512), F32), jax.ShapeDtypeStruct((t, 512), BF16),
                   jax.ShapeDtypeStruct((t, 1024), F32)],
        scratch_shapes=[pltpu.VMEM((8, BLK, BLK), F32)],
        compiler_params=_params(("parallel", "arbitrary"), 48),
    )(proj, proj, proj, proj)


def _sb_bwd(proj, o, dog, cm, nbatch, nb):
    lp = nb * BLK
    t = nbatch * lp

    def body(q_ref, k_ref, v_ref, g_ref, o_ref, dog_ref, cm_ref, dq_ref, dk_ref, dv_ref, dg_ref, c_scr, do_scr):
        i = pl.program_id(1)

        @pl.when(i == 0)
        def _():
            dk_ref[...] = jnp.zeros_like(dk_ref)
            dv_ref[...] = jnp.zeros_like(dv_ref)

        row, lane = _iotas()
        lo_m = lane < HALF
        upper = (row > lane).astype(BF16)
        lower = _tri_ones(row < lane)
        t_pos = i * BLK + row
        g = g_ref[...].astype(F32)
        sig = _sigmoid(g)
        dog_v = dog_ref[...]
        dg_ref[...] = dog_v * o_ref[...] * (sig * (1.0 + g * (1.0 - sig)))
        do_scr[...] = (dog_v * g * sig).astype(BF16)
        c_scr[...] = jnp.zeros_like(c_scr)
        dq_ref[...] = jnp.zeros_like(dq_ref)

        def step(j, carry):
            off = pl.multiple_of(j * BLK, BLK)
            s_pos = j * BLK + lane
            mask = (s_pos < t_pos) & (s_pos >= N_PAD)
            onehot = lane == j
            heads = range(8)
            pcs = [slice(p * BLK, (p + 1) * BLK) for p in range(4)]
            kbs = [k_ref[pl.ds(off, BLK), pc] for pc in pcs]
            vbs = [v_ref[pl.ds(off, BLK), pc] for pc in pcs]
            qms, doms = [], []
            for h in heads:
                hm = (lane >= HALF) if h % 2 else lo_m
                q = q_ref[:, pcs[h // 2]]
                dob = do_scr[:, pcs[h // 2]]
                qms.append(jnp.where(hm, q, jnp.zeros_like(q)))
                doms.append(jnp.where(hm, dob, jnp.zeros_like(dob)))
            zs = [_dot_nt(qms[h], kbs[h // 2]) for h in heads]
            dps = [_dot_nt(doms[h], vbs[h // 2]) for h in heads]
            lbs, l1s, parts = [], [], []
            for h in heads:
                log_beta, log_1m = _sb_logits(zs[h])
                lbs.append(log_beta)
                l1s.append(log_1m)
                parts.append(_split(jnp.where(mask, log_1m, 0.0)))
            sufs = [_dot(parts[h][0], upper) + _dot(parts[h][1], upper) for h in heads]
            prs, dzzs, parts = [], [], []
            for h in heads:
                cj = jnp.sum(jnp.where(onehot, cm_ref[:, h * BLK:(h + 1) * BLK], 0.0), axis=1, keepdims=True)
                pr = jnp.where(mask, jnp.exp(lbs[h] + sufs[h] + cj), 0.0)
                dzz = pr * dps[h]
                prs.append(pr.astype(BF16))
                dzzs.append(dzz)
                parts.append(_split(dzz))
            css = [_dot(parts[h][0], lower) + _dot(parts[h][1], lower) for h in heads]
            dzbs = []
            for h in heads:
                c2 = c_scr[h]
                dz = jnp.where(mask, dzzs[h] * jnp.exp(l1s[h]) - jnp.exp(lbs[h]) * (css[h][:, :BLK] + c2), 0.0)
                dzbs.append((dz * SB_SCALE).astype(BF16))
                c_scr[h] = c2 + css[h][:, BLK:]
            dqs = [_dot(dzbs[h], kbs[h // 2]) for h in heads]
            dks = [_dot_tn(dzbs[h], qms[h]) for h in heads]
            dvs = [_dot_tn(prs[h], doms[h]) for h in heads]
            for p in range(4):
                dq_ref[:, pcs[p]] += jnp.where(lo_m, dqs[2 * p], dqs[2 * p + 1])
                dk_ref[pl.ds(off, BLK), pcs[p]] += dks[2 * p] + dks[2 * p + 1]
                dv_ref[pl.ds(off, BLK), pcs[p]] += dvs[2 * p] + dvs[2 * p + 1]
            return carry

        lax.fori_loop(0, i + 1, step, 0)

    tile = lambda col: pl.BlockSpec((BLK, 512), lambda b, i: (b * nb + i, col))
    full = lambda col: pl.BlockSpec((lp, 512), lambda b, i: (b, col))
    sds = jax.ShapeDtypeStruct((t, 512), F32)
    return pl.pallas_call(
        body, name="sb_bwd", grid=(nbatch, nb),
        in_specs=[tile(0), full(1), full(2), tile(3), tile(0), tile(0),
                  pl.BlockSpec((BLK, 1024), lambda b, i: (b * nb + i, 0))],
        out_specs=[tile(0), full(0), full(0), tile(0)],
        out_shape=[sds, sds, sds, sds],
        scratch_shapes=[pltpu.VMEM((8, BLK, BLK), F32), pltpu.VMEM((BLK, 512), BF16)],
        compiler_params=_params(("parallel", "arbitrary"), 56),
    )(proj, proj, proj, proj, o, dog, cm)


def _mla_prep(proj, gq, gkv, tabs, nbatch, nb):
    t = proj.shape[0]

    def body(mid_ref, gq_ref, gkv_ref, c_ref, s1_ref, s2_ref, cq_ref, ckv_ref, kr_ref):
        cq = mid_ref[:, 0:256].astype(F32)
        r = lax.rsqrt(jnp.mean(cq * cq, axis=1, keepdims=True) + NORM_EPS)
        cq_ref[...] = (cq * r * gq_ref[...]).astype(BF16)
        ckv = mid_ref[:, 256:384].astype(F32)
        r = lax.rsqrt(jnp.mean(ckv * ckv, axis=1, keepdims=True) + NORM_EPS)
        ckv_ref[...] = (ckv * r * gkv_ref[...]).astype(BF16)
        kr = mid_ref[:, 384:512].astype(F32)
        kr_ref[...] = _rope(kr, c_ref[...], s1_ref[...], s2_ref[...]).astype(BF16)

    tab = pl.BlockSpec((BLK, BLK), lambda b, i: (i, 0))
    rowspec = lambda w: pl.BlockSpec((BLK, w), lambda b, i: (b * nb + i, 0))
    return pl.pallas_call(
        body, name="mla_prep", grid=(nbatch, nb),
        in_specs=[pl.BlockSpec((BLK, 512), lambda b, i: (b * nb + i, EV_MID_BLK)),
                  pl.BlockSpec((1, 256), lambda b, i: (0, 0)), pl.BlockSpec((1, 128), lambda b, i: (0, 0)),
                  tab, tab, tab],
        out_specs=[rowspec(256), rowspec(128), rowspec(128)],
        out_shape=[jax.ShapeDtypeStruct((t, 256), BF16), jax.ShapeDtypeStruct((t, 128), BF16),
                   jax.ShapeDtypeStruct((t, 128), BF16)],
        compiler_params=_params(("parallel", "parallel")),
    )(proj, gq, gkv, *tabs)


def _mla_prep_bwd(proj, gq, gkv, tabs, dcqn, dckvn, dkrot, nbatch, nb):
    t = proj.shape[0]

    def body(mid_ref, gq_ref, gkv_ref, c_ref, s1_ref, s2_ref, dcq_ref, dckv_ref, dkr_ref,
             dmid_ref, dgq_ref, dgkv_ref):
        @pl.when((pl.program_id(0) == 0) & (pl.program_id(1) == 0))
        def _():
            dgq_ref[...] = jnp.zeros_like(dgq_ref)
            dgkv_ref[...] = jnp.zeros_like(dgkv_ref)

        def norm_bwd(x, gain, dy, dgain_ref):
            r = lax.rsqrt(jnp.mean(x * x, axis=1, keepdims=True) + NORM_EPS)
            nx = x * r
            dn = dy * gain
            dgain_ref[...] += jnp.sum(dy * nx, axis=0, keepdims=True)
            return r * (dn - nx * jnp.mean(dn * nx, axis=1, keepdims=True))

        dmid_ref[:, 0:256] = norm_bwd(mid_ref[:, 0:256].astype(F32), gq_ref[...], dcq_ref[...], dgq_ref)
        dmid_ref[:, 256:384] = norm_bwd(mid_ref[:, 256:384].astype(F32), gkv_ref[...], dckv_ref[...], dgkv_ref)
        dmid_ref[:, 384:512] = _rope_t(dkr_ref[...], c_ref[...], s1_ref[...], s2_ref[...])

    tab = pl.BlockSpec((BLK, BLK), lambda b, i: (i, 0))
    rowspec = lambda w: pl.BlockSpec((BLK, w), lambda b, i: (b * nb + i, 0))
    vq = pl.BlockSpec((1, 256), lambda b, i: (0, 0))
    vkv = pl.BlockSpec((1, 128), lambda b, i: (0, 0))
    return pl.pallas_call(
        body, name="mla_prep_bwd", grid=(nbatch, nb),
        in_specs=[pl.BlockSpec((BLK, 512), lambda b, i: (b * nb + i, EV_MID_BLK)), vq, vkv, tab, tab, tab,
                  rowspec(256), rowspec(128), rowspec(128)],
        out_specs=[rowspec(512), vq, vkv],
        out_shape=[jax.ShapeDtypeStruct((t, 512), F32), jax.ShapeDtypeStruct((1, 256), F32),
                   jax.ShapeDtypeStruct((1, 128), F32)],
        compiler_params=_params(("arbitrary", "arbitrary")),
    )(proj, gq, gkv, *tabs, dcqn, dckvn, dkrot)


def _mla_scores(qf, kvb, krb, mask, lo_m):
    kf = jnp.where(lo_m, kvb, krb)
    s = _dot_nt(qf, kf) * MLA_SCALE
    return kf, jnp.where(mask, s, NEG)


def _mla_fwd(qh, kvh, krot, proj, tabs, nbatch, nb):
    lp = nb * BLK
    t = nbatch * lp

    def body(q_ref, kv_ref, kr_ref, g_ref, c_ref, s1_ref, s2_ref, o_ref, og_ref, lse_ref,
             qf_scr, m_scr, l_scr, acc_scr):
        i = pl.program_id(1)
        row, lane = _iotas()
        lo_m = lane < HALF
        t_pos = i * BLK + row
        ones = jnp.ones((BLK, BLK), BF16)
        for h in range(8):
            hc = slice(h * BLK, (h + 1) * BLK)
            qf_scr[:, hc] = _rope(q_ref[:, hc].astype(F32), c_ref[...], s1_ref[...], s2_ref[...]).astype(BF16)
        m_scr[...] = jnp.full(m_scr.shape, NEG, F32)
        l_scr[...] = jnp.zeros_like(l_scr)
        acc_scr[...] = jnp.zeros_like(acc_scr)

        def step(j, carry):
            off = pl.multiple_of(j * BLK, BLK)
            s_pos = j * BLK + lane
            mask = (s_pos <= t_pos) & (s_pos >= N_PAD)
            krb = kr_ref[pl.ds(off, BLK), :]
            heads = range(8)
            hcs = [slice(h * BLK, (h + 1) * BLK) for h in heads]
            kvbs = [kv_ref[pl.ds(off, BLK), hc] for hc in hcs]
            ss = [_mla_scores(qf_scr[:, hcs[h]], kvbs[h], krb, mask, lo_m)[1] for h in heads]
            ps, alphas = [], []
            for h in heads:
                m = m_scr[h]
                m2 = jnp.maximum(m, jnp.max(ss[h], axis=1, keepdims=True))
                ps.append(jnp.exp(ss[h] - m2).astype(BF16))
                alphas.append(jnp.exp(m - m2))
                m_scr[h] = m2
            pvs = [_dot(ps[h], jnp.concatenate([kvbs[h], ones], axis=1)) for h in heads]
            for h in heads:
                l_scr[h] = alphas[h] * l_scr[h] + pvs[h][:, BLK:]
                acc_scr[h] = alphas[h] * acc_scr[h] + pvs[h][:, :BLK]
            return carry

        lax.fori_loop(0, i + 1, step, 0)
        lse = jnp.zeros((BLK, BLK), F32)
        for p in range(4):
            pc = slice(p * BLK, (p + 1) * BLK)
            o0 = acc_scr[2 * p] / l_scr[2 * p]
            o1 = acc_scr[2 * p + 1] / l_scr[2 * p + 1]
            o_ref[:, pc] = jnp.where(lo_m, pltpu.roll(o0, HALF, 1), o1)
            for h in (2 * p, 2 * p + 1):
                lse = lse + jnp.where(lane == h, m_scr[h] + jnp.log(l_scr[h]), 0.0)
        lse_ref[...] = lse
        g = g_ref[...].astype(F32)
        og_ref[...] = (o_ref[...] * g * _sigmoid(g)).astype(og_ref.dtype)

    tile = pl.BlockSpec((BLK, 512), lambda b, i: (b * nb + i, 0))
    tab = pl.BlockSpec((BLK, BLK), lambda b, i: (i, 0))
    heads = pltpu.VMEM((8, BLK, BLK), F32)
    return pl.pallas_call(
        body, name="mla_fwd", grid=(nbatch, nb),
        in_specs=[pl.BlockSpec((BLK, 1024), lambda b, i: (b * nb + i, 0)),
                  pl.BlockSpec((lp, 1024), lambda b, i: (b, 0)),
                  pl.BlockSpec((lp, BLK), lambda b, i: (b, 0)),
                  pl.BlockSpec((BLK, 512), lambda b, i: (b * nb + i, EV_GMLA_BLK // 4)),
                  tab, tab, tab],
        out_specs=[tile, tile, pl.BlockSpec((BLK, BLK), lambda b, i: (b * nb + i, 0))],
        out_shape=[jax.ShapeDtypeStruct((t, 512), F32), jax.ShapeDtypeStruct((t, 512), BF16),
                   jax.ShapeDtypeStruct((t, BLK), F32)],
        scratch_shapes=[pltpu.VMEM((BLK, 1024), BF16), heads, heads, heads],
        compiler_params=_params(("parallel", "arbitrary"), 48),
    )(qh, kvh, krot, proj, *tabs)


def _mla_bwd(qh, kvh, krot, proj, tabs, o, dog, lse, nbatch, nb):
    lp = nb * BLK
    t = nbatch * lp

    def body(q_ref, kv_ref, kr_ref, g_ref, c_ref, s1_ref, s2_ref, o_ref, dog_ref, lse_ref,
             dq_ref, dkv_ref, dkr_ref, dg_ref, qf_scr, do_scr, stat_scr, acc_scr):
        i = pl.program_id(1)

        @pl.when(i == 0)
        def _():
            dkv_ref[...] = jnp.zeros_like(dkv_ref)
            dkr_ref[...] = jnp.zeros_like(dkr_ref)

        row, lane = _iotas()
        lo_m = lane < HALF
        t_pos = i * BLK + row
        g = g_ref[...].astype(F32)
        sig = _sigmoid(g)
        dog_v = dog_ref[...]
        o_v = o_ref[...]
        dg_ref[...] = dog_v * o_v * (sig * (1.0 + g * (1.0 - sig)))
        do = dog_v * g * sig
        do_o = do * o_v
        lse_blk = lse_ref[...]
        zero = jnp.zeros((BLK, BLK), F32)
        for h in range(8):
            hc = slice(h * BLK, (h + 1) * BLK)
            pc = slice((h // 2) * BLK, (h // 2 + 1) * BLK)
            qf_scr[:, hc] = _rope(q_ref[:, hc].astype(F32), c_ref[...], s1_ref[...], s2_ref[...]).astype(BF16)
            dop = do[:, pc]
            do_src = dop if h % 2 else pltpu.roll(dop, HALF, 1)
            do_scr[:, hc] = jnp.where(lo_m, 0.0, do_src).astype(BF16)
            hm = (lane >= HALF) if h % 2 else lo_m
            stat_scr[h] = zero + jnp.sum(jnp.where(hm, do_o[:, pc], 0.0), axis=1, keepdims=True)
            stat_scr[8 + h] = zero + jnp.sum(jnp.where(lane == h, lse_blk, 0.0), axis=1, keepdims=True)
        acc_scr[...] = jnp.zeros_like(acc_scr)

        def step(j, carry):
            off = pl.multiple_of(j * BLK, BLK)
            s_pos = j * BLK + lane
            mask = (s_pos <= t_pos) & (s_pos >= N_PAD)
            krb = kr_ref[pl.ds(off, BLK), :]
            heads = range(8)
            hcs = [slice(h * BLK, (h + 1) * BLK) for h in heads]
            kvbs = [kv_ref[pl.ds(off, BLK), hc] for hc in hcs]
            qfs = [qf_scr[:, hc] for hc in hcs]
            dos = [do_scr[:, hc] for hc in hcs]
            scored = [_mla_scores(qfs[h], kvbs[h], krb, mask, lo_m) for h in heads]
            dps = [_dot_nt(dos[h], kvbs[h]) for h in heads]
            pbs, dss = [], []
            for h in heads:
                p = jnp.exp(scored[h][1] - stat_scr[8 + h])
                pbs.append(p.astype(BF16))
                dss.append((p * (dps[h] - stat_scr[h]) * MLA_SCALE).astype(BF16))
            dqs = [_dot(dss[h], scored[h][0]) for h in heads]
            dkfs = [_dot_tn(dss[h], qfs[h]) for h in heads]
            dvvs = [_dot_tn(pbs[h], dos[h]) for h in heads]
            dkr = zero
            for h in heads:
                acc_scr[h] += dqs[h]
                dkv_ref[pl.ds(off, BLK), hcs[h]] += jnp.where(lo_m, dkfs[h], 0.0) + dvvs[h]
                dkr = dkr + jnp.where(lo_m, 0.0, dkfs[h])
            dkr_ref[pl.ds(off, BLK), :] += dkr
            return carry

        lax.fori_loop(0, i + 1, step, 0)
        for h in range(8):
            hc = slice(h * BLK, (h + 1) * BLK)
            dq_ref[:, hc] = _rope_t(acc_scr[h], c_ref[...], s1_ref[...], s2_ref[...])

    tile = lambda col: pl.BlockSpec((BLK, 512), lambda b, i: (b * nb + i, col))
    wide = pl.BlockSpec((BLK, 1024), lambda b, i: (b * nb + i, 0))
    full8 = pl.BlockSpec((lp, 1024), lambda b, i: (b, 0))
    full1 = pl.BlockSpec((lp, BLK), lambda b, i: (b, 0))
    tab = pl.BlockSpec((BLK, BLK), lambda b, i: (i, 0))
    return pl.pallas_call(
        body, name="mla_bwd", grid=(nbatch, nb),
        in_specs=[wide, full8, full1, tile(EV_GMLA_BLK // 4), tab, tab, tab, tile(0), tile(1),
                  pl.BlockSpec((BLK, BLK), lambda b, i: (b * nb + i, 0))],
        out_specs=[wide, full8, full1, tile(0)],
        out_shape=[jax.ShapeDtypeStruct((t, 1024), F32), jax.ShapeDtypeStruct((t, 1024), F32),
                   jax.ShapeDtypeStruct((t, 128), F32), jax.ShapeDtypeStruct((t, 512), F32)],
        scratch_shapes=[pltpu.VMEM((BLK, 1024), BF16), pltpu.VMEM((BLK, 1024), BF16),
                        pltpu.VMEM((16, BLK, BLK), F32), pltpu.VMEM((8, BLK, BLK), F32)],
        compiler_params=_params(("parallel", "arbitrary"), 56),
    )(qh, kvh, krot, proj, *tabs, o, dog, lse)


def _swa_setup(kk, i, k_refs, v_refs):
    row, lane = _iotas()
    own = (lane >= kk * HALF) & (lane < (kk + 1) * HALF)

    def dup(ref):
        x = ref[...].astype(F32)
        return jnp.where(own, x, pltpu.roll(x, HALF, 1)).astype(BF16)

    kcat = jnp.concatenate([dup(r) for r in k_refs], axis=0)
    vcat = jnp.concatenate([dup(r) for r in v_refs], axis=0)
    row3 = lax.broadcasted_iota(jnp.int32, (BLK, 3 * BLK), 0)
    lane3 = lax.broadcasted_iota(jnp.int32, (BLK, 3 * BLK), 1)
    is_meta = lane3 < BLK
    k_pos = jnp.where(is_meta, lane3, (i - 2) * BLK + lane3)
    d = i * BLK + row3 - k_pos
    mask = (d >= 0) & (d < jnp.where(is_meta, 1 << 20, BLK)) & (k_pos >= jnp.where(is_meta, N_PAD, BLK))
    return lane, own, kcat, vcat, mask, d.astype(F32)


def _swa_slope(kk, g_idx):
    return (2.0 ** (-(g_idx + 1) / 2.0)) * jnp.where(kk == 0, 1.0, 1.0 / 16.0)


def _swa_fwd(proj, sinks, nbatch, nb):
    lp = nb * BLK
    t = nbatch * lp

    def body(sink_ref, q_ref, ka, kb, kc, va, vb, vc, g_ref, o_ref, og_ref, lse_ref):
        kk = pl.program_id(1)
        i = pl.program_id(2)
        lane, own, kcat, vcat, mask, dist = _swa_setup(kk, i, (ka, kb, kc), (va, vb, vc))
        lo_m = lane < HALF
        heads = range(8)
        qms = []
        for h in heads:
            qp = q_ref[:, (h // 2) * BLK:(h // 2 + 1) * BLK]
            qms.append(jnp.where((lane >= HALF) if h % 2 else lo_m, qp, jnp.zeros_like(qp)))
        qks = [_dot_nt(qms[h], kcat) for h in heads]
        ps, ls, lses = [], [], []
        for h in heads:
            sink = sink_ref[kk, h]
            s = jnp.where(mask, qks[h] * SWA_SCALE - _swa_slope(kk, h) * dist, NEG)
            mx = jnp.maximum(jnp.max(s, axis=1, keepdims=True), sink)
            p = jnp.exp(s - mx)
            l = jnp.exp(sink - mx) + jnp.sum(p, axis=1, keepdims=True)
            ps.append(p.astype(BF16))
            ls.append(l)
            lses.append(mx + jnp.log(l))
        pvs = [_dot(ps[h], vcat) for h in heads]
        lse_out = jnp.zeros((BLK, BLK), F32)
        for m in range(4):
            cols = slice(m * BLK, (m + 1) * BLK)
            outp = jnp.where(lo_m, pvs[2 * m] / ls[2 * m], pvs[2 * m + 1] / ls[2 * m + 1])
            o_ref[:, cols] = outp
            g = g_ref[:, cols].astype(F32)
            og_ref[:, cols] = (outp * g * _sigmoid(g)).astype(og_ref.dtype)
            for h in (2 * m, 2 * m + 1):
                lse_out = lse_out + jnp.where(lane == h, lses[h], 0.0)
        lse_ref[...] = lse_out

    def kvspec(col, which):
        if which == 0:
            return pl.BlockSpec((BLK, BLK), lambda b, kk, i: (b * nb, col))
        if which == 1:
            return pl.BlockSpec((BLK, BLK), lambda b, kk, i: (b * nb + jnp.maximum(i - 1, 0), col))
        return pl.BlockSpec((BLK, BLK), lambda b, kk, i: (b * nb + i, col))

    wide = lambda c0: pl.BlockSpec((BLK, 512), lambda b, kk, i: (b * nb + i, c0 + kk))
    return pl.pallas_call(
        body, name="swa_fwd", grid=(nbatch, 2, nb),
        in_specs=[pl.BlockSpec(memory_space=pltpu.SMEM), wide(0),
                  kvspec(OD_K_BLK, 0), kvspec(OD_K_BLK, 1), kvspec(OD_K_BLK, 2),
                  kvspec(OD_V_BLK, 0), kvspec(OD_V_BLK, 1), kvspec(OD_V_BLK, 2), wide(2)],
        out_specs=[wide(0), wide(0), pl.BlockSpec((BLK, BLK), lambda b, kk, i: (b * nb + i, kk))],
        out_shape=[jax.ShapeDtypeStruct((t, 1024), F32), jax.ShapeDtypeStruct((t, 1024), BF16),
                   jax.ShapeDtypeStruct((t, 256), F32)],
        compiler_params=_params(("parallel", "parallel", "arbitrary")),
    )(sinks, proj, proj, proj, proj, proj, proj, proj, proj)


def _swa_bwd(proj, sinks, o, dog, lse, nbatch, nb):
    lp = nb * BLK
    t = nbatch * lp

    def body(sink_ref, q_ref, ka, kb, kc, va, vb, vc, g_ref, o_ref, dog_ref, lse_ref,
             dq_ref, dg_ref, dk_ref, dv_ref, dsink_ref):
        kk = pl.program_id(1)
        i = pl.program_id(2)

        @pl.when((kk == 0) & (i == 0))
        def _():
            dk_ref[...] = jnp.zeros_like(dk_ref)
            dv_ref[...] = jnp.zeros_like(dv_ref)

        @pl.when(i == 0)
        def _():
            dsink_ref[...] = jnp.zeros_like(dsink_ref)

        lane, own, kcat, vcat, mask, dist = _swa_setup(kk, i, (ka, kb, kc), (va, vb, vc))
        lo_m = lane < HALF
        row8 = lax.broadcasted_iota(jnp.int32, (8, BLK), 0)
        lse_blk = lse_ref[...]
        heads = range(8)
        qms, doms, deltas, lse_hs = [], [], [], []
        for m in range(4):
            cols = slice(m * BLK, (m + 1) * BLK)
            qp = q_ref[:, cols]
            g = g_ref[:, cols].astype(F32)
            sig = _sigmoid(g)
            dog_v = dog_ref[:, cols]
            o_v = o_ref[:, cols]
            dg_ref[:, cols] = dog_v * o_v * (sig * (1.0 + g * (1.0 - sig)))
            do = dog_v * g * sig
            do_o = do * o_v
            dob = do.astype(BF16)
            for h in (2 * m, 2 * m + 1):
                hm = (lane >= HALF) if h % 2 else lo_m
                qms.append(jnp.where(hm, qp, jnp.zeros_like(qp)))
                doms.append(jnp.where(hm, dob, jnp.zeros_like(dob)))
                deltas.append(jnp.sum(jnp.where(hm, do_o, 0.0), axis=1, keepdims=True))
                lse_hs.append(jnp.sum(jnp.where(lane == h, lse_blk, 0.0), axis=1, keepdims=True))
        qks = [_dot_nt(qms[h], kcat) for h in heads]
        dps = [_dot_nt(doms[h], vcat) for h in heads]
        pbs, dss = [], []
        dsink = jnp.zeros((8, BLK), F32)
        for h in heads:
            s = jnp.where(mask, qks[h] * SWA_SCALE - _swa_slope(kk, h) * dist, NEG)
            p = jnp.exp(s - lse_hs[h])
            pbs.append(p.astype(BF16))
            dss.append((p * (dps[h] - deltas[h]) * SWA_SCALE).astype(BF16))
            tot = jnp.sum(-jnp.exp(sink_ref[kk, h] - lse_hs[h]) * deltas[h], axis=0, keepdims=True)
            dsink = dsink + jnp.where(row8 == h, tot, 0.0)
        dsink_ref[...] += dsink
        dqs = [_dot(dss[h], kcat) for h in heads]
        dks = [_dot_tn(dss[h], qms[h]) for h in heads]
        dvs = [_dot_tn(pbs[h], doms[h]) for h in heads]
        for m in range(4):
            dq_ref[:, m * BLK:(m + 1) * BLK] = jnp.where(lo_m, dqs[2 * m], dqs[2 * m + 1])
        dk = dks[0]
        dv = dvs[0]
        for h in range(1, 8):
            dk = dk + dks[h]
            dv = dv + dvs[h]
        offs = [0, pl.multiple_of(jnp.maximum(i - 1, 0) * BLK, BLK), pl.multiple_of(i * BLK, BLK)]
        for x in range(3):
            rows = slice(x * BLK, (x + 1) * BLK)
            dkx, dvx = dk[rows], dv[rows]
            dk_ref[pl.ds(offs[x], BLK), :] += jnp.where(own, dkx + pltpu.roll(dkx, HALF, 1), 0.0)
            dv_ref[pl.ds(offs[x], BLK), :] += jnp.where(own, dvx + pltpu.roll(dvx, HALF, 1), 0.0)

    def kvspec(col, which):
        if which == 0:
            return pl.BlockSpec((BLK, BLK), lambda b, kk, i: (b * nb, col))
        if which == 1:
            return pl.BlockSpec((BLK, BLK), lambda b, kk, i: (b * nb + jnp.maximum(i - 1, 0), col))
        return pl.BlockSpec((BLK, BLK), lambda b, kk, i: (b * nb + i, col))

    wide = lambda c0: pl.BlockSpec((BLK, 512), lambda b, kk, i: (b * nb + i, c0 + kk))
    full = pl.BlockSpec((lp, BLK), lambda b, kk, i: (b, 0))
    return pl.pallas_call(
        body, name="swa_bwd", grid=(nbatch, 2, nb),
        in_specs=[pl.BlockSpec(memory_space=pltpu.SMEM), wide(0),
                  kvspec(OD_K_BLK, 0), kvspec(OD_K_BLK, 1), kvspec(OD_K_BLK, 2),
                  kvspec(OD_V_BLK, 0), kvspec(OD_V_BLK, 1), kvspec(OD_V_BLK, 2), wide(2),
                  wide(0), wide(0), pl.BlockSpec((BLK, BLK), lambda b, kk, i: (b * nb + i, kk))],
        out_specs=[wide(0), wide(0), full, full,
                   pl.BlockSpec((8, BLK), lambda b, kk, i: (b * 2 + kk, 0))],
        out_shape=[jax.ShapeDtypeStruct((t, 1024), F32), jax.ShapeDtypeStruct((t, 1024), F32),
                   jax.ShapeDtypeStruct((t, 128), F32), jax.ShapeDtypeStruct((t, 128), F32),
                   jax.ShapeDtypeStruct((nbatch * 16, BLK), F32)],
        compiler_params=_params(("parallel", "arbitrary", "arbitrary")),
    )(sinks, proj, proj, proj, proj, proj, proj, proj, proj, o, dog, lse)


def _rope_tables(lp):
    pos = (jnp.arange(lp) - N_PAD).astype(F32)
    inv = ROPE_BASE ** (-jnp.arange(16, dtype=F32) / 16.0)
    ang = pos[:, None] * inv[None, :]
    cos, sin = jnp.cos(ang), jnp.sin(ang)
    z16 = jnp.zeros((lp, 16), F32)
    c = jnp.concatenate([jnp.ones((lp, 64), F32), cos, cos, jnp.zeros((lp, 32), F32)], axis=1)
    s1 = jnp.concatenate([jnp.zeros((lp, 64), F32), -sin, z16, jnp.zeros((lp, 32), F32)], axis=1)
    s2 = jnp.concatenate([jnp.zeros((lp, 64), F32), z16, sin, jnp.zeros((lp, 32), F32)], axis=1)
    return c, s1, s2


def _local_step(h0, tgt, norm_g, final_g, gq, gkv, sinks, w_ie, w_uq, w_ukv, w_oe, w_io, w_oo, nbatch, nb):
    lp = nb * BLK
    tabs = _rope_tables(lp)
    g0, g1 = norm_g[0:1], norm_g[1:2]
    sinks2 = sinks.reshape(2, 8)

    hn0 = _rms_fwd(h0, g0, "rms_fwd0")
    proj_e = _mm(hn0, w_ie, "nn", "proj_even", out_dtype=BF16)
    o_sb, og_sb, cm = _sb_fwd(proj_e, nbatch, nb)
    cqn, ckvn, krot = _mla_prep(proj_e, gq, gkv, tabs, nbatch, nb)
    qh = _mm(cqn, w_uq, "nn", "mla_uq", out_dtype=BF16)
    kvh = _mm(ckvn, w_ukv, "nn", "mla_ukv", out_dtype=BF16)
    o_mla, og_mla, lse_m = _mla_fwd(qh, kvh, krot, proj_e, tabs, nbatch, nb)
    og_e = jnp.concatenate([og_sb, og_mla], axis=1)
    h1 = _mm(og_e, w_oe, "nn", "out_even", add=h0)
    hn1 = _rms_fwd(h1, g1, "rms_fwd1")
    proj_o = _mm(hn1, w_io, "nn", "proj_odd", out_dtype=BF16)
    o_o, og_o, lse_o = _swa_fwd(proj_o, sinks2, nbatch, nb)
    h2 = _mm(og_o, w_oo, "nn", "out_odd", add=h1)
    dh2, lossv, d_final_g = _final(h2, tgt, final_g, nbatch, nb)

    dog_o = _mm(dh2, w_oo, "nt", "d_out_odd")
    d_w_oo = _mm_tn(og_o, dh2, "dw_out_odd")
    dq_o, dg_o, dk_o, dv_o, dsink = _swa_bwd(proj_o, sinks2, o_o, dog_o, lse_o, nbatch, nb)
    dproj_o = jnp.concatenate([dq_o, dg_o, dk_o, dv_o], axis=1).astype(BF16)
    dhn1 = _mm(dproj_o, w_io, "nt", "d_proj_odd")
    d_w_io = _mm_tn(hn1, dproj_o, "dw_proj_odd")
    dh1, d_g1 = _rms_bwd(h1, g1, dhn1, dh2, "rms_bwd1")

    dog_e = _mm(dh1, w_oe, "nt", "d_out_even")
    d_w_oe = _mm_tn(og_e, dh1, "dw_out_even")
    dq_sb, dk_sb, dv_sb, dg_sb = _sb_bwd(proj_e, o_sb, dog_e, cm, nbatch, nb)
    dqh, dkvh, dkrot, dg_mla = _mla_bwd(qh, kvh, krot, proj_e, tabs, o_mla, dog_e, lse_m, nbatch, nb)
    dcqn = _mm(dqh, w_uq, "nt", "d_mla_uq")
    d_w_uq = _mm_tn(cqn, dqh, "dw_mla_uq")
    dckvn = _mm(dkvh, w_ukv, "nt", "d_mla_ukv")
    d_w_ukv = _mm_tn(ckvn, dkvh, "dw_mla_ukv")
    dmid, d_gq, d_gkv = _mla_prep_bwd(proj_e, gq, gkv, tabs, dcqn, dckvn, dkrot, nbatch, nb)
    dproj_e = jnp.concatenate([dq_sb, dk_sb, dv_sb, dg_sb, dmid, dg_mla], axis=1).astype(BF16)
    dhn0 = _mm(dproj_e, w_ie, "nt", "d_proj_even")
    d_w_ie = _mm_tn(hn0, dproj_e, "dw_proj_even")
    dh0, d_g0 = _rms_bwd(h0, g0, dhn0, dh1, "rms_bwd0")

    d_sinks = dsink.reshape(nbatch, 2, 8, BLK)[:, :, :, 0].sum(axis=0).reshape(1, 16)
    d_norm_g = jnp.concatenate([d_g0, d_g1], axis=0)
    return dict(lossv=lossv, dh0=dh0, norm_g=d_norm_g, final_g=d_final_g, gq=d_gq, gkv=d_gkv, sinks=d_sinks,
                w_ie=d_w_ie, w_uq=d_w_uq, w_ukv=d_w_ukv, w_oe=d_w_oe, w_io=d_w_io, w_oo=d_w_oo)


def _ev_in_to_compute(w):
    z = lambda n: jnp.zeros((w.shape[0], n), w.dtype)
    return jnp.concatenate([w[:, :2432], z(64), w[:, 2432:2464], z(32), w[:, 2464:]], axis=1)


def _ev_in_from_compute(w):
    return jnp.concatenate([w[:, :2432], w[:, 2496:2528], w[:, 2560:]], axis=1)


def _uq_to_compute(w):
    w3 = w.reshape(256, 8, 96)
    return jnp.concatenate([w3, jnp.zeros((256, 8, 32), w.dtype)], axis=2).reshape(256, 1024)


def _uq_from_compute(w):
    return w.reshape(256, 8, 128)[:, :, :96].reshape(256, 768)


def _od_in_to_compute(w):
    return jnp.concatenate([w[:, :1024], w[:, 1280:], w[:, 1024:1280]], axis=1)


def _od_in_from_compute(w):
    return jnp.concatenate([w[:, :1024], w[:, 2048:], w[:, 1024:2048]], axis=1)


_BIG = (("ev_w_in", 1024, 2976, 1), ("ev_w_uq", 256, 768, 1), ("ev_w_ukv", 128, 1024, 1),
        ("ev_w_out", 1024, 1024, 0), ("od_w_in", 1024, 2304, 1), ("od_w_out", 1024, 1024, 0))


def _shard_rows(rows, cols):
    return rows * cols // N_DEV // BLK


def _all_gather(x_shard):
    m_per, n = x_shard.shape

    def body(x_ref, out_ref, send_sems, recv_sems, local_sem):
        x, y, c = lax.axis_index("x"), lax.axis_index("y"), lax.axis_index("c")
        me, sibling = (x, y, c), (x, y, 1 - c)
        chips = [(1 - x, y), (x, 1 - y), (1 - x, 1 - y)]

        def rows(px, py, pc):
            return out_ref.at[pl.ds((4 * px + 2 * py + pc) * m_per, m_per), :]

        def copy(k, block, to, src=None):
            return pltpu.make_async_remote_copy(
                src_ref=rows(*block) if src is None else src, dst_ref=rows(*block),
                send_sem=send_sems.at[k], recv_sem=recv_sems.at[k],
                device_id=to, device_id_type=pl.DeviceIdType.MESH)

        mine = pltpu.make_async_copy(x_ref, rows(*me), local_sem)
        mine.start()
        first = [copy(0, me, sibling, src=x_ref)]
        first += [copy(1 + j, me, (*chip, c), src=x_ref) for j, chip in enumerate(chips)]
        for cp in first:
            cp.start()
        passed = [copy(4 + j, (*chip, c), sibling) for j, chip in enumerate(chips)]
        for j, chip in enumerate(chips):
            copy(1 + j, (*chip, c), me).wait_recv()
            passed[j].start()
        copy(0, sibling, me).wait_recv()
        for j, chip in enumerate(chips):
            copy(4 + j, (*chip, 1 - c), me).wait_recv()
        for cp in first + passed:
            cp.wait_send()
        mine.wait()

    return pl.pallas_call(
        body, name="gather_weights",
        out_shape=jax.ShapeDtypeStruct((N_DEV * m_per, n), x_shard.dtype),
        in_specs=[pl.BlockSpec(memory_space=pltpu.VMEM)],
        out_specs=pl.BlockSpec(memory_space=pltpu.VMEM),
        scratch_shapes=[pltpu.SemaphoreType.DMA((7,)), pltpu.SemaphoreType.DMA((7,)), pltpu.SemaphoreType.DMA],
        compiler_params=pltpu.CompilerParams(vmem_limit_bytes=48 << 20),
    )(x_shard)


def _reduce_scatter(big, small):
    _, rb, _ = big.shape
    _, rs, _ = small.shape
    chunk = 256

    def body(big_ref, small_ref, obig_ref, osmall_ref, rbig, rsmall, ssem, rsem, ssem2, rsem2, lsem):
        x, y, c = lax.axis_index("x"), lax.axis_index("y"), lax.axis_index("c")
        me = 4 * x + 2 * y + c
        own_b = pltpu.make_async_copy(big_ref.at[me], rbig.at[me], lsem.at[0])
        own_s = pltpu.make_async_copy(small_ref.at[me], rsmall.at[me], lsem.at[1])
        own_b.start()
        own_s.start()
        sends, recvs = [], []
        for d in range(1, N_DEV):
            px = x + ((d >> 2) & 1) - 2 * x * ((d >> 2) & 1)
            py = y + ((d >> 1) & 1) - 2 * y * ((d >> 1) & 1)
            pc = c + (d & 1) - 2 * c * (d & 1)
            pid = 4 * px + 2 * py + pc
            kw = dict(device_id=(px, py, pc), device_id_type=pl.DeviceIdType.MESH)
            sends.append(pltpu.make_async_remote_copy(
                src_ref=big_ref.at[pid], dst_ref=rbig.at[me], send_sem=ssem.at[d - 1], recv_sem=rsem.at[d - 1], **kw))
            sends.append(pltpu.make_async_remote_copy(
                src_ref=small_ref.at[pid], dst_ref=rsmall.at[me], send_sem=ssem2.at[d - 1], recv_sem=rsem2.at[d - 1], **kw))
            recvs.append(pltpu.make_async_remote_copy(
                src_ref=big_ref.at[pid], dst_ref=rbig.at[pid], send_sem=ssem.at[d - 1], recv_sem=rsem.at[d - 1], **kw))
            recvs.append(pltpu.make_async_remote_copy(
                src_ref=small_ref.at[pid], dst_ref=rsmall.at[pid], send_sem=ssem2.at[d - 1], recv_sem=rsem2.at[d - 1], **kw))
        for cp in sends:
            cp.start()
        own_b.wait()
        own_s.wait()
        for cp in recvs:
            cp.wait_recv()
        for cp in sends:
            cp.wait_send()

        acc = rsmall[0]
        for k in range(1, N_DEV):
            acc = acc + rsmall[k]
        osmall_ref[...] = acc

        def step(r, carry):
            off = pl.multiple_of(r * chunk, chunk)
            a = rbig[0, pl.ds(off, chunk), :].astype(F32)
            for k in range(1, N_DEV):
                a = a + rbig[k, pl.ds(off, chunk), :].astype(F32)
            obig_ref[pl.ds(off, chunk), :] = a
            return carry

        lax.fori_loop(0, rb // chunk, step, 0)

    return pl.pallas_call(
        body, name="reduce_grads",
        out_shape=[jax.ShapeDtypeStruct((rb, BLK), F32), jax.ShapeDtypeStruct((rs, BLK), F32)],
        in_specs=[pl.BlockSpec(memory_space=pl.ANY), pl.BlockSpec(memory_space=pl.ANY)],
        out_specs=[pl.BlockSpec(memory_space=pltpu.VMEM), pl.BlockSpec(memory_space=pltpu.VMEM)],
        scratch_shapes=[pltpu.VMEM((N_DEV, rb, BLK), BF16), pltpu.VMEM((N_DEV, rs, BLK), F32),
                        pltpu.SemaphoreType.DMA((7,)), pltpu.SemaphoreType.DMA((7,)),
                        pltpu.SemaphoreType.DMA((7,)), pltpu.SemaphoreType.DMA((7,)),
                        pltpu.SemaphoreType.DMA((2,))],
        compiler_params=pltpu.CompilerParams(vmem_limit_bytes=48 << 20),
    )(big, small)


def _adamw(ws, gs, ms, vs):
    n = len(ws)

    def body(*refs):
        ins, outs = refs[:4 * n], refs[4 * n:]
        for k in range(n):
            w_ref, g_ref, m_ref, v_ref = ins[4 * k:4 * k + 4]
            d_ref, nm_ref, nv_ref = outs[3 * k:3 * k + 3]

            def update(sl, w_ref=w_ref, g_ref=g_ref, m_ref=m_ref, v_ref=v_ref,
                       d_ref=d_ref, nm_ref=nm_ref, nv_ref=nv_ref):
                g = g_ref[sl]
                m = ADAM_B1 * m_ref[sl] + (1.0 - ADAM_B1) * g
                v = ADAM_B2 * v_ref[sl] + (1.0 - ADAM_B2) * (g * g)
                m_hat = m / (1.0 - ADAM_B1 ** ADAM_STEP)
                v_hat = v / (1.0 - ADAM_B2 ** ADAM_STEP)
                d_ref[sl] = -ADAM_LR * (m_hat / (jnp.sqrt(v_hat) + ADAM_EPS) + ADAM_WD * w_ref[sl])
                nm_ref[sl] = m
                nv_ref[sl] = v

            rows = w_ref.shape[0]
            if rows > BLK and rows % BLK == 0:
                def step(r, carry, update=update):
                    update((pl.ds(pl.multiple_of(r * BLK, BLK), BLK), slice(None)))
                    return carry

                lax.fori_loop(0, rows // BLK, step, 0)
            else:
                update((slice(None), slice(None)))

    args, out_shape = [], []
    for k in range(n):
        args += [ws[k], gs[k], ms[k], vs[k]]
        out_shape += [jax.ShapeDtypeStruct(ws[k].shape, F32)] * 3
    vm = pl.BlockSpec(memory_space=pltpu.VMEM)
    outs = pl.pallas_call(
        body, name="adamw", out_shape=out_shape,
        in_specs=[vm] * (4 * n), out_specs=[vm] * (3 * n),
        compiler_params=pltpu.CompilerParams(vmem_limit_bytes=48 << 20),
    )(*args)
    return [tuple(outs[3 * k:3 * k + 3]) for k in range(n)]


def kernel(x, meta, norm_g, final_g, ev_w_in, ev_q_norm_g, ev_kv_norm_g, ev_w_uq, ev_w_ukv, ev_w_out, od_w_in, od_sinks, od_w_out, loss_target, m_meta, m_norm_g, m_final_g, m_ev_w_in, m_ev_q_norm_g, m_ev_kv_norm_g, m_ev_w_uq, m_ev_w_ukv, m_ev_w_out, m_od_w_in, m_od_sinks, m_od_w_out, v_meta, v_norm_g, v_final_g, v_ev_w_in, v_ev_q_norm_g, v_ev_kv_norm_g, v_ev_w_uq, v_ev_w_ukv, v_ev_w_out, v_od_w_in, v_od_sinks, v_od_w_out):
    nbatch, seq, d = x.shape
    nb = seq // BLK + 1
    lp = nb * BLK
    shards = dict(ev_w_in=ev_w_in[0], ev_w_uq=ev_w_uq[0], ev_w_ukv=ev_w_ukv[0], ev_w_out=ev_w_out[0],
                  od_w_in=od_w_in[0], od_w_out=od_w_out[0])

    parts = [shards[name].astype(BF16).reshape(-1, BLK) for name, _, _, _ in _BIG]
    parts.append(lax.bitcast_convert_type(meta, BF16).reshape(2 * N_META, BLK))
    gathered = _all_gather(jnp.concatenate(parts, axis=0)).reshape(N_DEV, PACK_ROWS, BLK)
    full, off = {}, 0
    for name, rows, cols, axis in _BIG:
        n = _shard_rows(rows, cols)
        blk = gathered[:, off:off + n]
        off += n
        if axis == 1:
            full[name] = blk.reshape(N_DEV, rows, cols // N_DEV).transpose(1, 0, 2).reshape(rows, cols)
        else:
            full[name] = blk.reshape(rows, cols)
    meta_bits = gathered[:, off:off + 2 * N_META].reshape(N_DEV, N_META, BLK, 2)
    meta_full = lax.bitcast_convert_type(meta_bits, F32).transpose(1, 0, 2).reshape(N_META, d)

    head = jnp.concatenate([jnp.zeros((N_PAD, d), F32), meta_full], axis=0)
    h0 = jnp.concatenate([jnp.broadcast_to(head[None], (nbatch, BLK, d)), x], axis=1).reshape(nbatch * lp, d)
    grads = _local_step(
        h0, loss_target.reshape(nbatch * seq, d), norm_g, final_g.reshape(1, d), ev_q_norm_g, ev_kv_norm_g,
        od_sinks, _ev_in_to_compute(full["ev_w_in"]), _uq_to_compute(full["ev_w_uq"]), full["ev_w_ukv"],
        full["ev_w_out"], _od_in_to_compute(full["od_w_in"]), full["od_w_out"], nbatch, nb)
    dh0 = grads["dh0"].reshape(nbatch, lp, d)
    grad_x = dh0[:, BLK:]
    loss = lax.psum(0.5 / d * jnp.sum(grads["lossv"]), ("x", "y", "c"))

    gfull = dict(ev_w_in=_ev_in_from_compute(grads["w_ie"]), ev_w_uq=_uq_from_compute(grads["w_uq"]),
                 ev_w_ukv=grads["w_ukv"], ev_w_out=grads["w_oe"],
                 od_w_in=_od_in_from_compute(grads["w_io"]), od_w_out=grads["w_oo"])
    chunks = []
    for name, rows, cols, axis in _BIG:
        g = gfull[name]
        if axis == 1:
            g = g.reshape(rows, N_DEV, cols // N_DEV).transpose(1, 0, 2)
        chunks.append(g.reshape(N_DEV, _shard_rows(rows, cols), BLK))
    chunks.append(jnp.zeros((N_DEV, 2 * N_META, BLK), F32))
    big = jnp.concatenate(chunks, axis=1).astype(BF16)
    d_meta = dh0[:, N_PAD:BLK].sum(axis=0).reshape(N_META, N_DEV, BLK).transpose(1, 0, 2)
    pad = lambda a, n: jnp.concatenate([a.reshape(1, -1), jnp.zeros((1, n - a.size), F32)], axis=1)
    rep = jnp.concatenate([grads["norm_g"].reshape(1, -1), grads["final_g"], pad(grads["gq"], 512),
                           pad(grads["gkv"], 256), pad(grads["sinks"], 256)], axis=1).reshape(32, BLK)
    small = jnp.concatenate([d_meta, jnp.broadcast_to(rep[None], (N_DEV, 32, BLK))], axis=1)
    red_big, red_small = _reduce_scatter(big, small)

    g_shard, off = {}, 0
    for name, rows, cols, axis in _BIG:
        n = _shard_rows(rows, cols)
        shape = (rows, cols // N_DEV) if axis == 1 else (rows // N_DEV, cols)
        g_shard[name] = red_big[off:off + n].reshape(shape)
        off += n
    rep = red_small[N_META:].reshape(1, -1)
    g_small = dict(meta=red_small[:N_META], norm_g=rep[:, :2 * d].reshape(2, d), final_g=rep[:, 2 * d:3 * d],
                   ev_q_norm_g=rep[:, 3 * d:3 * d + 256], ev_kv_norm_g=rep[:, 3 * d + 512:3 * d + 640],
                   od_sinks=rep[:, 3 * d + 768:3 * d + 784])

    names = ["meta", "norm_g", "final_g", "ev_w_in", "ev_q_norm_g", "ev_kv_norm_g", "ev_w_uq", "ev_w_ukv",
             "ev_w_out", "od_w_in", "od_sinks", "od_w_out"]
    given = dict(meta=(meta, m_meta, v_meta), norm_g=(norm_g, m_norm_g, v_norm_g),
                 final_g=(final_g, m_final_g, v_final_g), ev_w_in=(ev_w_in, m_ev_w_in, v_ev_w_in),
                 ev_q_norm_g=(ev_q_norm_g, m_ev_q_norm_g, v_ev_q_norm_g),
                 ev_kv_norm_g=(ev_kv_norm_g, m_ev_kv_norm_g, v_ev_kv_norm_g),
                 ev_w_uq=(ev_w_uq, m_ev_w_uq, v_ev_w_uq), ev_w_ukv=(ev_w_ukv, m_ev_w_ukv, v_ev_w_ukv),
                 ev_w_out=(ev_w_out, m_ev_w_out, v_ev_w_out), od_w_in=(od_w_in, m_od_w_in, v_od_w_in),
                 od_sinks=(od_sinks, m_od_sinks, v_od_sinks), od_w_out=(od_w_out, m_od_w_out, v_od_w_out))
    ws, gs, ms, vs = [], [], [], []
    for name in names:
        g2 = g_shard[name] if name in g_shard else g_small[name]
        w, m, v = given[name]
        ws.append(w.reshape(g2.shape))
        ms.append(m.reshape(g2.shape))
        vs.append(v.reshape(g2.shape))
        gs.append(g2)
    upd = _adamw(ws, gs, ms, vs)
    shape_of = {name: given[name][0].shape for name in names}
    grads_out = [gs[k].reshape(shape_of[n]) for k, n in enumerate(names)]
    deltas = [upd[k][0].reshape(shape_of[n]) for k, n in enumerate(names)]
    new_m = [upd[k][1].reshape(shape_of[n]) for k, n in enumerate(names)]
    new_v = [upd[k][2].reshape(shape_of[n]) for k, n in enumerate(names)]
    return (loss, grad_x, *grads_out, *deltas, *new_m, *new_v)
```

```python
import jax
import jax.numpy as jnp
from jax import lax
from jax.experimental import pallas as pl
from jax.experimental.pallas import tpu as pltpu

F32 = jnp.float32
BF16 = jnp.bfloat16

D_MODEL = 1024
N_META = 16
BLK = 128
HALF = 64
N_PAD = BLK - N_META
NORM_EPS = 1e-6
NEG = -1e30
N_DEV = 8

SB_SCALE = 64 ** -0.5
MLA_SCALE = 96 ** -0.5
SWA_SCALE = 64 ** -0.5
ROPE_BASE = 10000.0

EV_IN_PAD = 3072
EV_MID_BLK = 4
EV_GMLA_BLK = 20
OD_K_BLK = 16
OD_V_BLK = 17

ADAM_LR = 0.001
ADAM_B1 = 0.9
ADAM_B2 = 0.999
ADAM_EPS = 1e-08
ADAM_WD = 0.01
ADAM_STEP = 10

PACK_ROWS = 7680


def _dot(a, b):
    return lax.dot_general(a, b, (((1,), (0,)), ((), ())), preferred_element_type=F32)


def _dot_nt(a, b):
    return lax.dot_general(a, b, (((1,), (1,)), ((), ())), preferred_element_type=F32)


def _dot_tn(a, b):
    return lax.dot_general(a, b, (((0,), (0,)), ((), ())), preferred_element_type=F32)


def _split(x):
    hi = x.astype(BF16)
    return hi, (x - hi.astype(F32)).astype(BF16)


def _sigmoid(x):
    return 1.0 / (1.0 + jnp.exp(-x))


def _iotas():
    row = lax.broadcasted_iota(jnp.int32, (BLK, BLK), 0)
    lane = lax.broadcasted_iota(jnp.int32, (BLK, BLK), 1)
    return row, lane


def _rope(x, c, s1, s2):
    return x * c + pltpu.roll(x, BLK - 16, 1) * s1 + pltpu.roll(x, 16, 1) * s2


def _rope_t(x, c, s1, s2):
    return x * c - pltpu.roll(x, BLK - 16, 1) * s1 - pltpu.roll(x, 16, 1) * s2


def _params(sem, vmem_mb=None):
    kw = dict(dimension_semantics=sem)
    if vmem_mb is not None:
        kw["vmem_limit_bytes"] = vmem_mb << 20
    return pltpu.CompilerParams(**kw)


def _row_tile(t, cands):
    for c in cands:
        if t % c == 0:
            return c
    raise ValueError(t)


def _mm(a, w, mode, name, add=None, out_dtype=F32):
    pieces = list(a) if isinstance(a, (list, tuple)) else [a]
    m = pieces[0].shape[0]
    n = w.shape[1] if mode == "nn" else w.shape[0]
    tm = _row_tile(m, (256, 128))
    widths = [p.shape[1] for p in pieces]
    offs = [sum(widths[:i]) for i in range(len(widths))]

    def body(*refs):
        p_refs = refs[:len(pieces)]
        w_ref = refs[len(pieces)]
        o_ref = refs[-1]
        acc = None
        for p_ref, off, wd in zip(p_refs, offs, widths):
            x = p_ref[...].astype(BF16)
            part = _dot(x, w_ref[off:off + wd, :]) if mode == "nn" else _dot_nt(x, w_ref[:, off:off + wd])
            acc = part if acc is None else acc + part
        if add is not None:
            acc = acc + refs[len(pieces) + 1][...]
        o_ref[...] = acc.astype(o_ref.dtype)

    in_specs = [pl.BlockSpec((tm, wd), lambda i: (i, 0)) for wd in widths]
    in_specs.append(pl.BlockSpec(w.shape, lambda i: (0, 0)))
    args = pieces + [w]
    if add is not None:
        in_specs.append(pl.BlockSpec((tm, n), lambda i: (i, 0)))
        args.append(add)
    return pl.pallas_call(
        body, name=name, grid=(m // tm,), in_specs=in_specs,
        out_specs=pl.BlockSpec((tm, n), lambda i: (i, 0)),
        out_shape=jax.ShapeDtypeStruct((m, n), out_dtype),
        compiler_params=_params(("parallel",), 48),
    )(*args)


def _mm_tn(x, pieces, name):
    t, k = x.shape
    tt = _row_tile(t, (544, 256, 128))
    widths = [p.shape[1] for p in pieces]

    def body(*refs):
        x_ref = refs[0]
        d_refs = refs[1:1 + len(pieces)]
        o_refs = refs[1 + len(pieces):]

        @pl.when(pl.program_id(0) == 0)
        def _():
            for o_ref in o_refs:
                o_ref[...] = jnp.zeros_like(o_ref)

        xb = x_ref[...].astype(BF16)
        for d_ref, o_ref in zip(d_refs, o_refs):
            o_ref[...] += _dot_tn(xb, d_ref[...].astype(BF16))

    return pl.pallas_call(
        body, name=name, grid=(t // tt,),
        in_specs=[pl.BlockSpec((tt, k), lambda i: (i, 0))] + [pl.BlockSpec((tt, wd), lambda i: (i, 0)) for wd in widths],
        out_specs=[pl.BlockSpec((k, wd), lambda i: (0, 0)) for wd in widths],
        out_shape=[jax.ShapeDtypeStruct((k, wd), F32) for wd in widths],
        compiler_params=_params(("arbitrary",), 56),
    )(x, *pieces)


def _rms_fwd(h, g, name):
    t, d = h.shape
    tr = _row_tile(t, (256, 128))

    def body(h_ref, g_ref, o_ref):
        x = h_ref[...]
        r = lax.rsqrt(jnp.mean(x * x, axis=1, keepdims=True) + NORM_EPS)
        o_ref[...] = (x * r * g_ref[...]).astype(o_ref.dtype)

    return pl.pallas_call(
        body, name=name, grid=(t // tr,),
        in_specs=[pl.BlockSpec((tr, d), lambda i: (i, 0)), pl.BlockSpec((1, d), lambda i: (0, 0))],
        out_specs=pl.BlockSpec((tr, d), lambda i: (i, 0)),
        out_shape=jax.ShapeDtypeStruct((t, d), BF16),
        compiler_params=_params(("parallel",)),
    )(h, g)


def _rms_bwd(h, g, dhn, dres, name):
    t, d = h.shape
    tr = _row_tile(t, (256, 128))

    def body(h_ref, g_ref, dhn_ref, dres_ref, dh_ref, dg_ref):
        @pl.when(pl.program_id(0) == 0)
        def _():
            dg_ref[...] = jnp.zeros_like(dg_ref)

        x = h_ref[...]
        r = lax.rsqrt(jnp.mean(x * x, axis=1, keepdims=True) + NORM_EPS)
        nx = x * r
        dy = dhn_ref[...]
        dn = dy * g_ref[...]
        dh_ref[...] = dres_ref[...] + r * (dn - nx * jnp.mean(dn * nx, axis=1, keepdims=True))
        dg_ref[...] += jnp.sum(dy * nx, axis=0, keepdims=True)

    row = pl.BlockSpec((tr, d), lambda i: (i, 0))
    vec = pl.BlockSpec((1, d), lambda i: (0, 0))
    return pl.pallas_call(
        body, name=name, grid=(t // tr,),
        in_specs=[row, vec, row, row], out_specs=[row, vec],
        out_shape=[jax.ShapeDtypeStruct((t, d), F32), jax.ShapeDtypeStruct((1, d), F32)],
        compiler_params=_params(("arbitrary",)),
    )(h, g, dhn, dres)


def _final(h2, tgt, g, nbatch, nb):
    t, d = h2.shape

    def body(h_ref, t_ref, g_ref, dh_ref, loss_ref, dg_ref):
        b = pl.program_id(0)
        i = pl.program_id(1)

        @pl.when((b == 0) & (i == 0))
        def _():
            loss_ref[...] = jnp.zeros_like(loss_ref)
            dg_ref[...] = jnp.zeros_like(dg_ref)

        x = h_ref[...]
        r = lax.rsqrt(jnp.mean(x * x, axis=1, keepdims=True) + NORM_EPS)
        nx = x * r
        gg = g_ref[...]
        live = jnp.where(i >= 1, 1.0, 0.0)
        err = (nx * gg - t_ref[...]) * live
        loss_ref[...] += jnp.sum(err * err, axis=0, keepdims=True)
        dy = err * (1.0 / d)
        dn = dy * gg
        dh_ref[...] = r * (dn - nx * jnp.mean(dn * nx, axis=1, keepdims=True))
        dg_ref[...] += jnp.sum(dy * nx, axis=0, keepdims=True)

    vec = pl.BlockSpec((1, d), lambda b, i: (0, 0))
    return pl.pallas_call(
        body, name="final_loss", grid=(nbatch, nb),
        in_specs=[pl.BlockSpec((BLK, d), lambda b, i: (b * nb + i, 0)),
                  pl.BlockSpec((BLK, d), lambda b, i: (b * (nb - 1) + jnp.maximum(i - 1, 0), 0)),
                  vec],
        out_specs=[pl.BlockSpec((BLK, d), lambda b, i: (b * nb + i, 0)), vec, vec],
        out_shape=[jax.ShapeDtypeStruct((t, d), F32), jax.ShapeDtypeStruct((1, d), F32),
                   jax.ShapeDtypeStruct((1, d), F32)],
        compiler_params=_params(("arbitrary", "arbitrary")),
    )(h2, tgt, g)


def _sb_logits(z):
    log_beta = jnp.minimum(z, 0.0) - jnp.log(1.0 + jnp.exp(-jnp.abs(z)))
    return log_beta, log_beta - z


def _tri2(tri, with_ones):
    m = tri.astype(BF16)
    if with_ones:
        m = jnp.concatenate([m, jnp.ones((BLK, BLK), BF16)], axis=1)
    return jnp.concatenate([m, m], axis=0)


def _head_masked(x, lane, scale=None):
    out = []
    for h in range(8):
        xp = x[:, (h // 2) * BLK:(h // 2 + 1) * BLK]
        xm = jnp.where((lane >= HALF) if h % 2 else (lane < HALF), xp, jnp.zeros_like(xp))
        out.append(xm if scale is None else xm * scale)
    return jnp.concatenate(out, axis=1)


def _sb_fwd(proj, nbatch, nb):
    lp = nb * BLK
    t = nbatch * lp

    def body(q_ref, k_ref, v_ref, g_ref, o_ref, og_ref, cm_ref, c_scr, qm_scr):
        i = pl.program_id(1)
        row, lane = _iotas()
        lo_m = lane < HALF
        upper = _tri2(row > lane, True)
        t_pos = i * BLK + row
        cm_ref[...] = jnp.zeros_like(cm_ref)
        c_scr[...] = jnp.zeros_like(c_scr)
        o_ref[...] = jnp.zeros_like(o_ref)
        qm_scr[...] = _head_masked(q_ref[...], lane, SB_SCALE)

        def step(jj, carry):
            j = i - jj
            off = pl.multiple_of(j * BLK, BLK)
            s_pos = j * BLK + lane
            mask = (s_pos < t_pos) & (s_pos >= N_PAD)
            onehot = lane == j
            heads = range(8)
            hcs = [slice(h * BLK, (h + 1) * BLK) for h in heads]
            kbs = [k_ref[pl.ds(off, BLK), hc] for hc in hcs[:4]]
            vbs = [v_ref[pl.ds(off, BLK), hc] for hc in hcs[:4]]
            zs = [_dot_nt(qm_scr[:, hcs[h]], kbs[h // 2]) for h in heads]
            lbs, parts = [], []
            for h in heads:
                log_beta, log_1m = _sb_logits(zs[h])
                lbs.append(log_beta)
                hi, lo = _split(jnp.where(mask, log_1m, 0.0))
                parts.append(jnp.concatenate([hi, lo], axis=1))
            css = [_dot(parts[h], upper) for h in heads]
            avs = []
            for h in heads:
                c = c_scr[h]
                avs.append(jnp.where(mask, jnp.exp(lbs[h] + css[h][:, :BLK] + c), 0.0).astype(BF16))
                cm_ref[:, hcs[h]] = jnp.where(onehot, c, cm_ref[:, hcs[h]])
                c_scr[h] = c + css[h][:, BLK:]
            accs = [_dot(avs[h], vbs[h // 2]) for h in heads]
            for p in range(4):
                o_ref[:, hcs[p]] += jnp.where(lo_m, accs[2 * p], accs[2 * p + 1])
            return carry

        lax.fori_loop(0, i + 1, step, 0)
        g = g_ref[...].astype(F32)
        og_ref[...] = (o_ref[...] * g * _sigmoid(g)).astype(og_ref.dtype)

    tile = lambda col: pl.BlockSpec((BLK, 512), lambda b, i: (b * nb + i, col))
    full = lambda col: pl.BlockSpec((lp, 512), lambda b, i: (b, col))
    return pl.pallas_call(
        body, name="sb_fwd", grid=(nbatch, nb),
        in_specs=[tile(0), full(1), full(2), tile(3)],
        out_specs=[tile(0), tile(0), pl.BlockSpec((BLK, 1024), lambda b, i: (b * nb + i, 0))],
        out_shape=[jax.ShapeDtypeStruct((t, 512), F32), jax.ShapeDtypeStruct((t, 512), BF16),
                   jax.ShapeDtypeStruct((t, 1024), F32)],
        scratch_shapes=[pltpu.VMEM((8, BLK, BLK), F32), pltpu.VMEM((BLK, 1024), BF16)],
        compiler_params=_params(("parallel", "arbitrary"), 48),
    )(proj, proj, proj, proj)


def _sb_bwd(proj, o, dog, cm, nbatch, nb):
    lp = nb * BLK
    t = nbatch * lp

    def body(q_ref, k_ref, v_ref, g_ref, o_ref, dog_ref, cm_ref, dq_ref, dk_ref, dv_ref, dg_ref,
             c_scr, qm_scr, dom_scr, dq_scr):
        i = pl.program_id(1)

        @pl.when(i == 0)
        def _():
            dk_ref[...] = jnp.zeros_like(dk_ref)
            dv_ref[...] = jnp.zeros_like(dv_ref)

        row, lane = _iotas()
        lo_m = lane < HALF
        upper = _tri2(row > lane, False)
        lower = _tri2(row < lane, True)
        t_pos = i * BLK + row
        g = g_ref[...].astype(F32)
        sig = _sigmoid(g)
        dog_v = dog_ref[...]
        dg_ref[...] = (dog_v * o_ref[...] * (sig * (1.0 + g * (1.0 - sig)))).astype(dg_ref.dtype)
        dom_scr[...] = _head_masked((dog_v * g * sig).astype(BF16), lane)
        qm_scr[...] = _head_masked(q_ref[...], lane, SB_SCALE)
        c_scr[...] = jnp.zeros_like(c_scr)
        dq_scr[...] = jnp.zeros_like(dq_scr)

        def step(j, carry):
            off = pl.multiple_of(j * BLK, BLK)
            s_pos = j * BLK + lane
            mask = (s_pos < t_pos) & (s_pos >= N_PAD)
            onehot = lane == j
            heads = range(8)
            hcs = [slice(h * BLK, (h + 1) * BLK) for h in heads]
            kbs = [k_ref[pl.ds(off, BLK), hc] for hc in hcs[:4]]
            vbs = [v_ref[pl.ds(off, BLK), hc] for hc in hcs[:4]]
            zs = [_dot_nt(qm_scr[:, hcs[h]], kbs[h // 2]) for h in heads]
            dps = [_dot_nt(dom_scr[:, hcs[h]], vbs[h // 2]) for h in heads]
            lbs, l1s, parts = [], [], []
            for h in heads:
                log_beta, log_1m = _sb_logits(zs[h])
                lbs.append(log_beta)
                l1s.append(log_1m)
                hi, lo = _split(jnp.where(mask, log_1m, 0.0))
                parts.append(jnp.concatenate([hi, lo], axis=1))
            sufs = [_dot(parts[h], upper) for h in heads]
            prs, dzzs, parts = [], [], []
            for h in heads:
                cj = jnp.sum(jnp.where(onehot, cm_ref[:, hcs[h]], 0.0), axis=1, keepdims=True)
                pr = jnp.where(mask, jnp.exp(lbs[h] + sufs[h] + cj), 0.0)
                dzz = pr * dps[h]
                prs.append(pr.astype(BF16))
                dzzs.append(dzz)
                hi, lo = _split(dzz)
                parts.append(jnp.concatenate([hi, lo], axis=1))
            css = [_dot(parts[h], lower) for h in heads]
            dzbs = []
            for h in heads:
                c2 = c_scr[h]
                dz = jnp.where(mask, dzzs[h] * jnp.exp(l1s[h]) - jnp.exp(lbs[h]) * (css[h][:, :BLK] + c2), 0.0)
                dzbs.append(dz.astype(BF16))
                c_scr[h] = c2 + css[h][:, BLK:]
            dqs = [_dot(dzbs[h], kbs[h // 2]) for h in heads]
            dks = [_dot_tn(dzbs[h], qm_scr[:, hcs[h]]) for h in heads]
            dvs = [_dot_tn(prs[h], dom_scr[:, hcs[h]]) for h in heads]
            for p in range(4):
                dq_scr[:, hcs[p]] += jnp.where(lo_m, dqs[2 * p], dqs[2 * p + 1])
                dk_ref[pl.ds(off, BLK), hcs[p]] += dks[2 * p] + dks[2 * p + 1]
                dv_ref[pl.ds(off, BLK), hcs[p]] += dvs[2 * p] + dvs[2 * p + 1]
            return carry

        lax.fori_loop(0, i + 1, step, 0)
        dq_ref[...] = (dq_scr[...] * SB_SCALE).astype(dq_ref.dtype)

    tile = lambda col: pl.BlockSpec((BLK, 512), lambda b, i: (b * nb + i, col))
    full = lambda col: pl.BlockSpec((lp, 512), lambda b, i: (b, col))
    acc = jax.ShapeDtypeStruct((t, 512), F32)
    once = jax.ShapeDtypeStruct((t, 512), BF16)
    return pl.pallas_call(
        body, name="sb_bwd", grid=(nbatch, nb),
        in_specs=[tile(0), full(1), full(2), tile(3), tile(0), tile(0),
                  pl.BlockSpec((BLK, 1024), lambda b, i: (b * nb + i, 0))],
        out_specs=[tile(0), full(0), full(0), tile(0)],
        out_shape=[once, acc, acc, once],
        scratch_shapes=[pltpu.VMEM((8, BLK, BLK), F32), pltpu.VMEM((BLK, 1024), BF16),
                        pltpu.VMEM((BLK, 1024), BF16), pltpu.VMEM((BLK, 512), F32)],
        compiler_params=_params(("parallel", "arbitrary"), 56),
    )(proj, proj, proj, proj, o, dog, cm)


def _mla_prep(proj, gq, gkv, tabs, nbatch, nb):
    t = proj.shape[0]

    def body(mid_ref, gq_ref, gkv_ref, c_ref, s1_ref, s2_ref, cq_ref, ckv_ref, kr_ref):
        cq = mid_ref[:, 0:256].astype(F32)
        r = lax.rsqrt(jnp.mean(cq * cq, axis=1, keepdims=True) + NORM_EPS)
        cq_ref[...] = (cq * r * gq_ref[...]).astype(BF16)
        ckv = mid_ref[:, 256:384].astype(F32)
        r = lax.rsqrt(jnp.mean(ckv * ckv, axis=1, keepdims=True) + NORM_EPS)
        ckv_ref[...] = (ckv * r * gkv_ref[...]).astype(BF16)
        kr = mid_ref[:, 384:512].astype(F32)
        kr_ref[...] = _rope(kr, c_ref[...], s1_ref[...], s2_ref[...]).astype(BF16)

    tab = pl.BlockSpec((BLK, BLK), lambda b, i: (i, 0))
    rowspec = lambda w: pl.BlockSpec((BLK, w), lambda b, i: (b * nb + i, 0))
    return pl.pallas_call(
        body, name="mla_prep", grid=(nbatch, nb),
        in_specs=[pl.BlockSpec((BLK, 512), lambda b, i: (b * nb + i, EV_MID_BLK)),
                  pl.BlockSpec((1, 256), lambda b, i: (0, 0)), pl.BlockSpec((1, 128), lambda b, i: (0, 0)),
                  tab, tab, tab],
        out_specs=[rowspec(256), rowspec(128), rowspec(128)],
        out_shape=[jax.ShapeDtypeStruct((t, 256), BF16), jax.ShapeDtypeStruct((t, 128), BF16),
                   jax.ShapeDtypeStruct((t, 128), BF16)],
        compiler_params=_params(("parallel", "parallel")),
    )(proj, gq, gkv, *tabs)


def _mla_prep_bwd(proj, gq, gkv, tabs, dcqn, dckvn, dkrot, nbatch, nb):
    t = proj.shape[0]

    def body(mid_ref, gq_ref, gkv_ref, c_ref, s1_ref, s2_ref, dcq_ref, dckv_ref, dkr_ref,
             dmid_ref, dgq_ref, dgkv_ref):
        @pl.when((pl.program_id(0) == 0) & (pl.program_id(1) == 0))
        def _():
            dgq_ref[...] = jnp.zeros_like(dgq_ref)
            dgkv_ref[...] = jnp.zeros_like(dgkv_ref)

        def norm_bwd(x, gain, dy, dgain_ref):
            r = lax.rsqrt(jnp.mean(x * x, axis=1, keepdims=True) + NORM_EPS)
            nx = x * r
            dn = dy * gain
            dgain_ref[...] += jnp.sum(dy * nx, axis=0, keepdims=True)
            return r * (dn - nx * jnp.mean(dn * nx, axis=1, keepdims=True))

        dmid_ref[:, 0:256] = norm_bwd(
            mid_ref[:, 0:256].astype(F32), gq_ref[...], dcq_ref[...], dgq_ref).astype(BF16)
        dmid_ref[:, 256:384] = norm_bwd(
            mid_ref[:, 256:384].astype(F32), gkv_ref[...], dckv_ref[...], dgkv_ref).astype(BF16)
        dmid_ref[:, 384:512] = _rope_t(dkr_ref[...], c_ref[...], s1_ref[...], s2_ref[...]).astype(BF16)

    tab = pl.BlockSpec((BLK, BLK), lambda b, i: (i, 0))
    rowspec = lambda w: pl.BlockSpec((BLK, w), lambda b, i: (b * nb + i, 0))
    vq = pl.BlockSpec((1, 256), lambda b, i: (0, 0))
    vkv = pl.BlockSpec((1, 128), lambda b, i: (0, 0))
    return pl.pallas_call(
        body, name="mla_prep_bwd", grid=(nbatch, nb),
        in_specs=[pl.BlockSpec((BLK, 512), lambda b, i: (b * nb + i, EV_MID_BLK)), vq, vkv, tab, tab, tab,
                  rowspec(256), rowspec(128), rowspec(128)],
        out_specs=[rowspec(512), vq, vkv],
        out_shape=[jax.ShapeDtypeStruct((t, 512), BF16), jax.ShapeDtypeStruct((1, 256), F32),
                   jax.ShapeDtypeStruct((1, 128), F32)],
        compiler_params=_params(("arbitrary", "arbitrary")),
    )(proj, gq, gkv, *tabs, dcqn, dckvn, dkrot)


def _mla_scores(qf, kvb, krb, mask, lo_m):
    kf = jnp.where(lo_m, kvb, krb)
    s = _dot_nt(qf, kf) * MLA_SCALE
    return kf, jnp.where(mask, s, NEG)


def _mla_fwd(qh, kvh, krot, proj, tabs, nbatch, nb):
    lp = nb * BLK
    t = nbatch * lp

    def body(q_ref, kv_ref, kr_ref, g_ref, c_ref, s1_ref, s2_ref, o_ref, og_ref, lse_ref,
             qf_scr, m_scr, l_scr, acc_scr):
        i = pl.program_id(1)
        row, lane = _iotas()
        lo_m = lane < HALF
        t_pos = i * BLK + row
        ones = jnp.ones((BLK, BLK), BF16)
        for h in range(8):
            hc = slice(h * BLK, (h + 1) * BLK)
            qf_scr[:, hc] = _rope(q_ref[:, hc].astype(F32), c_ref[...], s1_ref[...], s2_ref[...]).astype(BF16)
        m_scr[...] = jnp.full(m_scr.shape, NEG, F32)
        l_scr[...] = jnp.zeros_like(l_scr)
        acc_scr[...] = jnp.zeros_like(acc_scr)

        def step(j, carry):
            off = pl.multiple_of(j * BLK, BLK)
            s_pos = j * BLK + lane
            mask = (s_pos <= t_pos) & (s_pos >= N_PAD)
            krb = kr_ref[pl.ds(off, BLK), :]
            heads = range(8)
            hcs = [slice(h * BLK, (h + 1) * BLK) for h in heads]
            kvbs = [kv_ref[pl.ds(off, BLK), hc] for hc in hcs]
            ss = [_mla_scores(qf_scr[:, hcs[h]], kvbs[h], krb, mask, lo_m)[1] for h in heads]
            ps, alphas = [], []
            for h in heads:
                m = m_scr[h]
                m2 = jnp.maximum(m, jnp.max(ss[h], axis=1, keepdims=True))
                ps.append(jnp.exp(ss[h] - m2).astype(BF16))
                alphas.append(jnp.exp(m - m2))
                m_scr[h] = m2
            pvs = [_dot(ps[h], jnp.concatenate([kvbs[h], ones], axis=1)) for h in heads]
            for h in heads:
                l_scr[h] = alphas[h] * l_scr[h] + pvs[h][:, BLK:]
                acc_scr[h] = alphas[h] * acc_scr[h] + pvs[h][:, :BLK]
            return carry

        lax.fori_loop(0, i + 1, step, 0)
        lse = jnp.zeros((BLK, BLK), F32)
        for p in range(4):
            pc = slice(p * BLK, (p + 1) * BLK)
            o0 = acc_scr[2 * p] / l_scr[2 * p]
            o1 = acc_scr[2 * p + 1] / l_scr[2 * p + 1]
            o_ref[:, pc] = jnp.where(lo_m, pltpu.roll(o0, HALF, 1), o1)
            for h in (2 * p, 2 * p + 1):
                lse = lse + jnp.where(lane == h, m_scr[h] + jnp.log(l_scr[h]), 0.0)
        lse_ref[...] = lse
        g = g_ref[...].astype(F32)
        og_ref[...] = (o_ref[...] * g * _sigmoid(g)).astype(og_ref.dtype)

    tile = pl.BlockSpec((BLK, 512), lambda b, i: (b * nb + i, 0))
    tab = pl.BlockSpec((BLK, BLK), lambda b, i: (i, 0))
    heads = pltpu.VMEM((8, BLK, BLK), F32)
    return pl.pallas_call(
        body, name="mla_fwd", grid=(nbatch, nb),
        in_specs=[pl.BlockSpec((BLK, 1024), lambda b, i: (b * nb + i, 0)),
                  pl.BlockSpec((lp, 1024), lambda b, i: (b, 0)),
                  pl.BlockSpec((lp, BLK), lambda b, i: (b, 0)),
                  pl.BlockSpec((BLK, 512), lambda b, i: (b * nb + i, EV_GMLA_BLK // 4)),
                  tab, tab, tab],
        out_specs=[tile, tile, pl.BlockSpec((BLK, BLK), lambda b, i: (b * nb + i, 0))],
        out_shape=[jax.ShapeDtypeStruct((t, 512), F32), jax.ShapeDtypeStruct((t, 512), BF16),
                   jax.ShapeDtypeStruct((t, BLK), F32)],
        scratch_shapes=[pltpu.VMEM((BLK, 1024), BF16), heads, heads, heads],
        compiler_params=_params(("parallel", "arbitrary"), 48),
    )(qh, kvh, krot, proj, *tabs)


def _mla_bwd(qh, kvh, krot, proj, tabs, o, dog, lse, nbatch, nb):
    lp = nb * BLK
    t = nbatch * lp

    def body(q_ref, kv_ref, kr_ref, g_ref, c_ref, s1_ref, s2_ref, o_ref, dog_ref, lse_ref,
             dq_ref, dkv_ref, dkr_ref, dg_ref, qf_scr, do_scr, stat_scr, acc_scr):
        i = pl.program_id(1)

        @pl.when(i == 0)
        def _():
            dkv_ref[...] = jnp.zeros_like(dkv_ref)
            dkr_ref[...] = jnp.zeros_like(dkr_ref)

        row, lane = _iotas()
        lo_m = lane < HALF
        t_pos = i * BLK + row
        g = g_ref[...].astype(F32)
        sig = _sigmoid(g)
        dog_v = dog_ref[...]
        o_v = o_ref[...]
        dg_ref[...] = (dog_v * o_v * (sig * (1.0 + g * (1.0 - sig)))).astype(dg_ref.dtype)
        do = dog_v * g * sig
        do_o = do * o_v
        lse_blk = lse_ref[...]
        zero = jnp.zeros((BLK, BLK), F32)
        for h in range(8):
            hc = slice(h * BLK, (h + 1) * BLK)
            pc = slice((h // 2) * BLK, (h // 2 + 1) * BLK)
            qf_scr[:, hc] = _rope(q_ref[:, hc].astype(F32), c_ref[...], s1_ref[...], s2_ref[...]).astype(BF16)
            dop = do[:, pc]
            do_src = dop if h % 2 else pltpu.roll(dop, HALF, 1)
            do_scr[:, hc] = jnp.where(lo_m, 0.0, do_src).astype(BF16)
            hm = (lane >= HALF) if h % 2 else lo_m
            stat_scr[h] = zero + jnp.sum(jnp.where(hm, do_o[:, pc], 0.0), axis=1, keepdims=True)
            stat_scr[8 + h] = zero + jnp.sum(jnp.where(lane == h, lse_blk, 0.0), axis=1, keepdims=True)
        acc_scr[...] = jnp.zeros_like(acc_scr)

        def step(j, carry):
            off = pl.multiple_of(j * BLK, BLK)
            s_pos = j * BLK + lane
            mask = (s_pos <= t_pos) & (s_pos >= N_PAD)
            krb = kr_ref[pl.ds(off, BLK), :]
            heads = range(8)
            hcs = [slice(h * BLK, (h + 1) * BLK) for h in heads]
            kvbs = [kv_ref[pl.ds(off, BLK), hc] for hc in hcs]
            qfs = [qf_scr[:, hc] for hc in hcs]
            dos = [do_scr[:, hc] for hc in hcs]
            scored = [_mla_scores(qfs[h], kvbs[h], krb, mask, lo_m) for h in heads]
            dps = [_dot_nt(dos[h], kvbs[h]) for h in heads]
            pbs, dss = [], []
            for h in heads:
                p = jnp.exp(scored[h][1] - stat_scr[8 + h])
                pbs.append(p.astype(BF16))
                dss.append((p * (dps[h] - stat_scr[h]) * MLA_SCALE).astype(BF16))
            dqs = [_dot(dss[h], scored[h][0]) for h in heads]
            dkfs = [_dot_tn(dss[h], qfs[h]) for h in heads]
            dvvs = [_dot_tn(pbs[h], dos[h]) for h in heads]
            dkr = zero
            for h in heads:
                acc_scr[h] += dqs[h]
                dkv_ref[pl.ds(off, BLK), hcs[h]] += jnp.where(lo_m, dkfs[h], 0.0) + dvvs[h]
                dkr = dkr + jnp.where(lo_m, 0.0, dkfs[h])
            dkr_ref[pl.ds(off, BLK), :] += dkr
            return carry

        lax.fori_loop(0, i + 1, step, 0)
        for h in range(8):
            hc = slice(h * BLK, (h + 1) * BLK)
            dq_ref[:, hc] = _rope_t(acc_scr[h], c_ref[...], s1_ref[...], s2_ref[...]).astype(dq_ref.dtype)

    tile = lambda col: pl.BlockSpec((BLK, 512), lambda b, i: (b * nb + i, col))
    wide = pl.BlockSpec((BLK, 1024), lambda b, i: (b * nb + i, 0))
    full8 = pl.BlockSpec((lp, 1024), lambda b, i: (b, 0))
    full1 = pl.BlockSpec((lp, BLK), lambda b, i: (b, 0))
    tab = pl.BlockSpec((BLK, BLK), lambda b, i: (i, 0))
    return pl.pallas_call(
        body, name="mla_bwd", grid=(nbatch, nb),
        in_specs=[wide, full8, full1, tile(EV_GMLA_BLK // 4), tab, tab, tab, tile(0), tile(1),
                  pl.BlockSpec((BLK, BLK), lambda b, i: (b * nb + i, 0))],
        out_specs=[wide, full8, full1, tile(0)],
        out_shape=[jax.ShapeDtypeStruct((t, 1024), BF16), jax.ShapeDtypeStruct((t, 1024), F32),
                   jax.ShapeDtypeStruct((t, 128), F32), jax.ShapeDtypeStruct((t, 512), BF16)],
        scratch_shapes=[pltpu.VMEM((BLK, 1024), BF16), pltpu.VMEM((BLK, 1024), BF16),
                        pltpu.VMEM((16, BLK, BLK), F32), pltpu.VMEM((8, BLK, BLK), F32)],
        compiler_params=_params(("parallel", "arbitrary"), 56),
    )(qh, kvh, krot, proj, *tabs, o, dog, lse)


def _swa_setup(kk, i, k_refs, v_refs):
    row, lane = _iotas()
    own = (lane >= kk * HALF) & (lane < (kk + 1) * HALF)

    def dup(ref):
        x = ref[...].astype(F32)
        return jnp.where(own, x, pltpu.roll(x, HALF, 1)).astype(BF16)

    kcat = jnp.concatenate([dup(r) for r in k_refs], axis=0)
    vcat = jnp.concatenate([dup(r) for r in v_refs], axis=0)
    row3 = lax.broadcasted_iota(jnp.int32, (BLK, 3 * BLK), 0)
    lane3 = lax.broadcasted_iota(jnp.int32, (BLK, 3 * BLK), 1)
    is_meta = lane3 < BLK
    k_pos = jnp.where(is_meta, lane3, (i - 2) * BLK + lane3)
    d = i * BLK + row3 - k_pos
    mask = (d >= 0) & (d < jnp.where(is_meta, 1 << 20, BLK)) & (k_pos >= jnp.where(is_meta, N_PAD, BLK))
    return lane, own, kcat, vcat, mask, d.astype(F32)


def _swa_slope(kk, g_idx):
    return (2.0 ** (-(g_idx + 1) / 2.0)) * jnp.where(kk == 0, 1.0, 1.0 / 16.0)


def _swa_fwd(proj, sinks, nbatch, nb):
    lp = nb * BLK
    t = nbatch * lp

    def body(sink_ref, q_ref, ka, kb, kc, va, vb, vc, g_ref, o_ref, og_ref, lse_ref):
        kk = pl.program_id(1)
        i = pl.program_id(2)
        lane, own, kcat, vcat, mask, dist = _swa_setup(kk, i, (ka, kb, kc), (va, vb, vc))
        lo_m = lane < HALF
        heads = range(8)
        qms = []
        for h in heads:
            qp = q_ref[:, (h // 2) * BLK:(h // 2 + 1) * BLK]
            qms.append(jnp.where((lane >= HALF) if h % 2 else lo_m, qp, jnp.zeros_like(qp)))
        qks = [_dot_nt(qms[h], kcat) for h in heads]
        ps, ls, lses = [], [], []
        for h in heads:
            sink = sink_ref[kk, h]
            s = jnp.where(mask, qks[h] * SWA_SCALE - _swa_slope(kk, h) * dist, NEG)
            mx = jnp.maximum(jnp.max(s, axis=1, keepdims=True), sink)
            p = jnp.exp(s - mx)
            l = jnp.exp(sink - mx) + jnp.sum(p, axis=1, keepdims=True)
            ps.append(p.astype(BF16))
            ls.append(l)
            lses.append(mx + jnp.log(l))
        pvs = [_dot(ps[h], vcat) for h in heads]
        lse_out = jnp.zeros((BLK, BLK), F32)
        for m in range(4):
            cols = slice(m * BLK, (m + 1) * BLK)
            outp = jnp.where(lo_m, pvs[2 * m] / ls[2 * m], pvs[2 * m + 1] / ls[2 * m + 1])
            o_ref[:, cols] = outp
            g = g_ref[:, cols].astype(F32)
            og_ref[:, cols] = (outp * g * _sigmoid(g)).astype(og_ref.dtype)
            for h in (2 * m, 2 * m + 1):
                lse_out = lse_out + jnp.where(lane == h, lses[h], 0.0)
        lse_ref[...] = lse_out

    def kvspec(col, which):
        if which == 0:
            return pl.BlockSpec((BLK, BLK), lambda b, kk, i: (b * nb, col))
        if which == 1:
            return pl.BlockSpec((BLK, BLK), lambda b, kk, i: (b * nb + jnp.maximum(i - 1, 0), col))
        return pl.BlockSpec((BLK, BLK), lambda b, kk, i: (b * nb + i, col))

    wide = lambda c0: pl.BlockSpec((BLK, 512), lambda b, kk, i: (b * nb + i, c0 + kk))
    return pl.pallas_call(
        body, name="swa_fwd", grid=(nbatch, 2, nb),
        in_specs=[pl.BlockSpec(memory_space=pltpu.SMEM), wide(0),
                  kvspec(OD_K_BLK, 0), kvspec(OD_K_BLK, 1), kvspec(OD_K_BLK, 2),
                  kvspec(OD_V_BLK, 0), kvspec(OD_V_BLK, 1), kvspec(OD_V_BLK, 2), wide(2)],
        out_specs=[wide(0), wide(0), pl.BlockSpec((BLK, BLK), lambda b, kk, i: (b * nb + i, kk))],
        out_shape=[jax.ShapeDtypeStruct((t, 1024), F32), jax.ShapeDtypeStruct((t, 1024), BF16),
                   jax.ShapeDtypeStruct((t, 256), F32)],
        compiler_params=_params(("parallel", "parallel", "arbitrary")),
    )(sinks, proj, proj, proj, proj, proj, proj, proj, proj)


def _swa_bwd(proj, sinks, o, dog, lse, nbatch, nb):
    lp = nb * BLK
    t = nbatch * lp

    def body(sink_ref, q_ref, ka, kb, kc, va, vb, vc, g_ref, o_ref, dog_ref, lse_ref,
             dq_ref, dg_ref, dk_ref, dv_ref, dsink_ref):
        kk = pl.program_id(1)
        i = pl.program_id(2)

        @pl.when((kk == 0) & (i == 0))
        def _():
            dk_ref[...] = jnp.zeros_like(dk_ref)
            dv_ref[...] = jnp.zeros_like(dv_ref)

        @pl.when(i == 0)
        def _():
            dsink_ref[...] = jnp.zeros_like(dsink_ref)

        lane, own, kcat, vcat, mask, dist = _swa_setup(kk, i, (ka, kb, kc), (va, vb, vc))
        lo_m = lane < HALF
        row8 = lax.broadcasted_iota(jnp.int32, (8, BLK), 0)
        lse_blk = lse_ref[...]
        heads = range(8)
        qms, doms, deltas, lse_hs = [], [], [], []
        for m in range(4):
            cols = slice(m * BLK, (m + 1) * BLK)
            qp = q_ref[:, cols]
            g = g_ref[:, cols].astype(F32)
            sig = _sigmoid(g)
            dog_v = dog_ref[:, cols]
            o_v = o_ref[:, cols]
            dg_ref[:, cols] = (dog_v * o_v * (sig * (1.0 + g * (1.0 - sig)))).astype(dg_ref.dtype)
            do = dog_v * g * sig
            do_o = do * o_v
            dob = do.astype(BF16)
            for h in (2 * m, 2 * m + 1):
                hm = (lane >= HALF) if h % 2 else lo_m
                qms.append(jnp.where(hm, qp, jnp.zeros_like(qp)))
                doms.append(jnp.where(hm, dob, jnp.zeros_like(dob)))
                deltas.append(jnp.sum(jnp.where(hm, do_o, 0.0), axis=1, keepdims=True))
                lse_hs.append(jnp.sum(jnp.where(lane == h, lse_blk, 0.0), axis=1, keepdims=True))
        qks = [_dot_nt(qms[h], kcat) for h in heads]
        dps = [_dot_nt(doms[h], vcat) for h in heads]
        pbs, dss = [], []
        dsink = jnp.zeros((8, BLK), F32)
        for h in heads:
            s = jnp.where(mask, qks[h] * SWA_SCALE - _swa_slope(kk, h) * dist, NEG)
            p = jnp.exp(s - lse_hs[h])
            pbs.append(p.astype(BF16))
            dss.append((p * (dps[h] - deltas[h]) * SWA_SCALE).astype(BF16))
            tot = jnp.sum(-jnp.exp(sink_ref[kk, h] - lse_hs[h]) * deltas[h], axis=0, keepdims=True)
            dsink = dsink + jnp.where(row8 == h, tot, 0.0)
        dsink_ref[...] += dsink
        dqs = [_dot(dss[h], kcat) for h in heads]
        dks = [_dot_tn(dss[h], qms[h]) for h in heads]
        dvs = [_dot_tn(pbs[h], doms[h]) for h in heads]
        for m in range(4):
            dq_ref[:, m * BLK:(m + 1) * BLK] = jnp.where(lo_m, dqs[2 * m], dqs[2 * m + 1]).astype(dq_ref.dtype)
        dk = dks[0]
        dv = dvs[0]
        for h in range(1, 8):
            dk = dk + dks[h]
            dv = dv + dvs[h]
        offs = [0, pl.multiple_of(jnp.maximum(i - 1, 0) * BLK, BLK), pl.multiple_of(i * BLK, BLK)]
        for x in range(3):
            rows = slice(x * BLK, (x + 1) * BLK)
            dkx, dvx = dk[rows], dv[rows]
            dk_ref[pl.ds(offs[x], BLK), :] += jnp.where(own, dkx + pltpu.roll(dkx, HALF, 1), 0.0)
            dv_ref[pl.ds(offs[x], BLK), :] += jnp.where(own, dvx + pltpu.roll(dvx, HALF, 1), 0.0)

    def kvspec(col, which):
        if which == 0:
            return pl.BlockSpec((BLK, BLK), lambda b, kk, i: (b * nb, col))
        if which == 1:
            return pl.BlockSpec((BLK, BLK), lambda b, kk, i: (b * nb + jnp.maximum(i - 1, 0), col))
        return pl.BlockSpec((BLK, BLK), lambda b, kk, i: (b * nb + i, col))

    wide = lambda c0: pl.BlockSpec((BLK, 512), lambda b, kk, i: (b * nb + i, c0 + kk))
    full = pl.BlockSpec((lp, BLK), lambda b, kk, i: (b, 0))
    return pl.pallas_call(
        body, name="swa_bwd", grid=(nbatch, 2, nb),
        in_specs=[pl.BlockSpec(memory_space=pltpu.SMEM), wide(0),
                  kvspec(OD_K_BLK, 0), kvspec(OD_K_BLK, 1), kvspec(OD_K_BLK, 2),
                  kvspec(OD_V_BLK, 0), kvspec(OD_V_BLK, 1), kvspec(OD_V_BLK, 2), wide(2),
                  wide(0), wide(0), pl.BlockSpec((BLK, BLK), lambda b, kk, i: (b * nb + i, kk))],
        out_specs=[wide(0), wide(0), full, full,
                   pl.BlockSpec((8, BLK), lambda b, kk, i: (b * 2 + kk, 0))],
        out_shape=[jax.ShapeDtypeStruct((t, 1024), BF16), jax.ShapeDtypeStruct((t, 1024), BF16),
                   jax.ShapeDtypeStruct((t, 128), F32), jax.ShapeDtypeStruct((t, 128), F32),
                   jax.ShapeDtypeStruct((nbatch * 16, BLK), F32)],
        compiler_params=_params(("parallel", "arbitrary", "arbitrary")),
    )(sinks, proj, proj, proj, proj, proj, proj, proj, proj, o, dog, lse)


def _rope_tables(lp):
    pos = (jnp.arange(lp) - N_PAD).astype(F32)
    inv = ROPE_BASE ** (-jnp.arange(16, dtype=F32) / 16.0)
    ang = pos[:, None] * inv[None, :]
    cos, sin = jnp.cos(ang), jnp.sin(ang)
    z16 = jnp.zeros((lp, 16), F32)
    c = jnp.concatenate([jnp.ones((lp, 64), F32), cos, cos, jnp.zeros((lp, 32), F32)], axis=1)
    s1 = jnp.concatenate([jnp.zeros((lp, 64), F32), -sin, z16, jnp.zeros((lp, 32), F32)], axis=1)
    s2 = jnp.concatenate([jnp.zeros((lp, 64), F32), z16, sin, jnp.zeros((lp, 32), F32)], axis=1)
    return c, s1, s2


def _local_step(h0, tgt, norm_g, final_g, gq, gkv, sinks, w_ie, w_uq, w_ukv, w_oe, w_io, w_oo, nbatch, nb):
    lp = nb * BLK
    tabs = _rope_tables(lp)
    g0, g1 = norm_g[0:1], norm_g[1:2]
    sinks2 = sinks.reshape(2, 8)

    hn0 = _rms_fwd(h0, g0, "rms_fwd0")
    proj_e = _mm(hn0, w_ie, "nn", "proj_even", out_dtype=BF16)
    o_sb, og_sb, cm = _sb_fwd(proj_e, nbatch, nb)
    cqn, ckvn, krot = _mla_prep(proj_e, gq, gkv, tabs, nbatch, nb)
    qh = _mm(cqn, w_uq, "nn", "mla_uq", out_dtype=BF16)
    kvh = _mm(ckvn, w_ukv, "nn", "mla_ukv", out_dtype=BF16)
    o_mla, og_mla, lse_m = _mla_fwd(qh, kvh, krot, proj_e, tabs, nbatch, nb)
    h1 = _mm([og_sb, og_mla], w_oe, "nn", "out_even", add=h0)
    hn1 = _rms_fwd(h1, g1, "rms_fwd1")
    proj_o = _mm(hn1, w_io, "nn", "proj_odd", out_dtype=BF16)
    o_o, og_o, lse_o = _swa_fwd(proj_o, sinks2, nbatch, nb)
    h2 = _mm(og_o, w_oo, "nn", "out_odd", add=h1)
    dh2, lossv, d_final_g = _final(h2, tgt, final_g, nbatch, nb)

    dog_o = _mm(dh2, w_oo, "nt", "d_out_odd")
    d_w_oo, = _mm_tn(og_o, [dh2], "dw_out_odd")
    dq_o, dg_o, dk_o, dv_o, dsink = _swa_bwd(proj_o, sinks2, o_o, dog_o, lse_o, nbatch, nb)
    dproj_o = [dq_o, dg_o, dk_o, dv_o]
    dhn1 = _mm(dproj_o, w_io, "nt", "d_proj_odd")
    dw_q, dw_g, dw_k, dw_v = _mm_tn(hn1, dproj_o, "dw_proj_odd")
    dh1, d_g1 = _rms_bwd(h1, g1, dhn1, dh2, "rms_bwd1")

    dog_e = _mm(dh1, w_oe, "nt", "d_out_even")
    d_w_oe_sb, = _mm_tn(og_sb, [dh1], "dw_out_even_sb")
    d_w_oe_mla, = _mm_tn(og_mla, [dh1], "dw_out_even_mla")
    dq_sb, dk_sb, dv_sb, dg_sb = _sb_bwd(proj_e, o_sb, dog_e, cm, nbatch, nb)
    dqh, dkvh, dkrot, dg_mla = _mla_bwd(qh, kvh, krot, proj_e, tabs, o_mla, dog_e, lse_m, nbatch, nb)
    dcqn = _mm(dqh, w_uq, "nt", "d_mla_uq")
    d_w_uq, = _mm_tn(cqn, [dqh], "dw_mla_uq")
    dckvn = _mm(dkvh, w_ukv, "nt", "d_mla_ukv")
    d_w_ukv, = _mm_tn(ckvn, [dkvh], "dw_mla_ukv")
    dmid, d_gq, d_gkv = _mla_prep_bwd(proj_e, gq, gkv, tabs, dcqn, dckvn, dkrot, nbatch, nb)
    dproj_e = [dq_sb, dk_sb, dv_sb, dg_sb, dmid, dg_mla]
    dhn0 = _mm(dproj_e, w_ie, "nt", "d_proj_even")
    dw_e = _mm_tn(hn0, dproj_e, "dw_proj_even")
    dh0, d_g0 = _rms_bwd(h0, g0, dhn0, dh1, "rms_bwd0")

    d_sinks = dsink.reshape(nbatch, 2, 8, BLK)[:, :, :, 0].sum(axis=0).reshape(1, 16)
    d_norm_g = jnp.concatenate([d_g0, d_g1], axis=0)
    d_ev_w_in = jnp.concatenate(list(dw_e[:4]) + [dw_e[4][:, :384], dw_e[4][:, 448:480], dw_e[5]], axis=1)
    d_od_w_in = jnp.concatenate([dw_q, dw_k, dw_v, dw_g], axis=1)
    d_ev_w_out = jnp.concatenate([d_w_oe_sb, d_w_oe_mla], axis=0)
    return dict(lossv=lossv, dh0=dh0, norm_g=d_norm_g, final_g=d_final_g, gq=d_gq, gkv=d_gkv, sinks=d_sinks,
                ev_w_in=d_ev_w_in, ev_w_uq=_uq_from_compute(d_w_uq), ev_w_ukv=d_w_ukv, ev_w_out=d_ev_w_out,
                od_w_in=d_od_w_in, od_w_out=d_w_oo)


def _ev_in_to_compute(w):
    z = lambda n: jnp.zeros((w.shape[0], n), w.dtype)
    return jnp.concatenate([w[:, :2432], z(64), w[:, 2432:2464], z(32), w[:, 2464:]], axis=1)


def _uq_to_compute(w):
    w3 = w.reshape(256, 8, 96)
    return jnp.concatenate([w3, jnp.zeros((256, 8, 32), w.dtype)], axis=2).reshape(256, 1024)


def _uq_from_compute(w):
    return w.reshape(256, 8, 128)[:, :, :96].reshape(256, 768)


def _od_in_to_compute(w):
    return jnp.concatenate([w[:, :1024], w[:, 1280:], w[:, 1024:1280]], axis=1)


_BIG = (("ev_w_in", 1024, 2976, 1), ("ev_w_uq", 256, 768, 1), ("ev_w_ukv", 128, 1024, 1),
        ("ev_w_out", 1024, 1024, 0), ("od_w_in", 1024, 2304, 1), ("od_w_out", 1024, 1024, 0))


def _shard_rows(rows, cols):
    return rows * cols // N_DEV // BLK


def _all_gather(x_shard):
    m_per, n = x_shard.shape

    def body(x_ref, out_ref, send_sems, recv_sems, local_sem):
        x, y, c = lax.axis_index("x"), lax.axis_index("y"), lax.axis_index("c")
        me, sibling = (x, y, c), (x, y, 1 - c)
        chips = [(1 - x, y), (x, 1 - y), (1 - x, 1 - y)]

        def rows(px, py, pc):
            return out_ref.at[pl.ds((4 * px + 2 * py + pc) * m_per, m_per), :]

        def copy(k, block, to, src=None):
            return pltpu.make_async_remote_copy(
                src_ref=rows(*block) if src is None else src, dst_ref=rows(*block),
                send_sem=send_sems.at[k], recv_sem=recv_sems.at[k],
                device_id=to, device_id_type=pl.DeviceIdType.MESH)

        mine = pltpu.make_async_copy(x_ref, rows(*me), local_sem)
        mine.start()
        first = [copy(0, me, sibling, src=x_ref)]
        first += [copy(1 + j, me, (*chip, c), src=x_ref) for j, chip in enumerate(chips)]
        for cp in first:
            cp.start()
        passed = [copy(4 + j, (*chip, c), sibling) for j, chip in enumerate(chips)]
        for j, chip in enumerate(chips):
            copy(1 + j, (*chip, c), me).wait_recv()
            passed[j].start()
        copy(0, sibling, me).wait_recv()
        for j, chip in enumerate(chips):
            copy(4 + j, (*chip, 1 - c), me).wait_recv()
        for cp in first + passed:
            cp.wait_send()
        mine.wait()

    return pl.pallas_call(
        body, name="gather_weights",
        out_shape=jax.ShapeDtypeStruct((N_DEV * m_per, n), x_shard.dtype),
        in_specs=[pl.BlockSpec(memory_space=pltpu.VMEM)],
        out_specs=pl.BlockSpec(memory_space=pltpu.VMEM),
        scratch_shapes=[pltpu.SemaphoreType.DMA((7,)), pltpu.SemaphoreType.DMA((7,)), pltpu.SemaphoreType.DMA],
        compiler_params=pltpu.CompilerParams(vmem_limit_bytes=48 << 20),
    )(x_shard)


def _reduce_scatter(big, small):
    _, rb, _ = big.shape
    _, rs, _ = small.shape
    chunk = 256

    def body(big_ref, small_ref, obig_ref, osmall_ref, rbig, rsmall, ssem, rsem, ssem2, rsem2, lsem):
        x, y, c = lax.axis_index("x"), lax.axis_index("y"), lax.axis_index("c")
        me = 4 * x + 2 * y + c
        own_b = pltpu.make_async_copy(big_ref.at[me], rbig.at[me], lsem.at[0])
        own_s = pltpu.make_async_copy(small_ref.at[me], rsmall.at[me], lsem.at[1])
        own_b.start()
        own_s.start()
        sends, recvs = [], []
        for d in range(1, N_DEV):
            px = x + ((d >> 2) & 1) - 2 * x * ((d >> 2) & 1)
            py = y + ((d >> 1) & 1) - 2 * y * ((d >> 1) & 1)
            pc = c + (d & 1) - 2 * c * (d & 1)
            pid = 4 * px + 2 * py + pc
            kw = dict(device_id=(px, py, pc), device_id_type=pl.DeviceIdType.MESH)
            sends.append(pltpu.make_async_remote_copy(
                src_ref=big_ref.at[pid], dst_ref=rbig.at[me], send_sem=ssem.at[d - 1], recv_sem=rsem.at[d - 1], **kw))
            sends.append(pltpu.make_async_remote_copy(
                src_ref=small_ref.at[pid], dst_ref=rsmall.at[me], send_sem=ssem2.at[d - 1], recv_sem=rsem2.at[d - 1], **kw))
            recvs.append(pltpu.make_async_remote_copy(
                src_ref=big_ref.at[pid], dst_ref=rbig.at[pid], send_sem=ssem.at[d - 1], recv_sem=rsem.at[d - 1], **kw))
            recvs.append(pltpu.make_async_remote_copy(
                src_ref=small_ref.at[pid], dst_ref=rsmall.at[pid], send_sem=ssem2.at[d - 1], recv_sem=rsem2.at[d - 1], **kw))
        for cp in sends:
            cp.start()
        own_b.wait()
        own_s.wait()
        for cp in recvs:
            cp.wait_recv()
        for cp in sends:
            cp.wait_send()

        acc = rsmall[0]
        for k in range(1, N_DEV):
            acc = acc + rsmall[k]
        osmall_ref[...] = acc

        def step(r, carry):
            off = pl.multiple_of(r * chunk, chunk)
            a = rbig[0, pl.ds(off, chunk), :].astype(F32)
            for k in range(1, N_DEV):
                a = a + rbig[k, pl.ds(off, chunk), :].astype(F32)
            obig_ref[pl.ds(off, chunk), :] = a
            return carry

        lax.fori_loop(0, rb // chunk, step, 0)

    return pl.pallas_call(
        body, name="reduce_grads",
        out_shape=[jax.ShapeDtypeStruct((rb, BLK), F32), jax.ShapeDtypeStruct((rs, BLK), F32)],
        in_specs=[pl.BlockSpec(memory_space=pl.ANY), pl.BlockSpec(memory_space=pl.ANY)],
        out_specs=[pl.BlockSpec(memory_space=pltpu.VMEM), pl.BlockSpec(memory_space=pltpu.VMEM)],
        scratch_shapes=[pltpu.VMEM((N_DEV, rb, BLK), BF16), pltpu.VMEM((N_DEV, rs, BLK), F32),
                        pltpu.SemaphoreType.DMA((7,)), pltpu.SemaphoreType.DMA((7,)),
                        pltpu.SemaphoreType.DMA((7,)), pltpu.SemaphoreType.DMA((7,)),
                        pltpu.SemaphoreType.DMA((2,))],
        compiler_params=pltpu.CompilerParams(vmem_limit_bytes=48 << 20),
    )(big, small)


def _adamw(ws, gs, ms, vs):
    n = len(ws)

    def body(*refs):
        ins, outs = refs[:4 * n], refs[4 * n:]
        for k in range(n):
            w_ref, g_ref, m_ref, v_ref = ins[4 * k:4 * k + 4]
            d_ref, nm_ref, nv_ref = outs[3 * k:3 * k + 3]

            def update(sl, w_ref=w_ref, g_ref=g_ref, m_ref=m_ref, v_ref=v_ref,
                       d_ref=d_ref, nm_ref=nm_ref, nv_ref=nv_ref):
                g = g_ref[sl]
                m = ADAM_B1 * m_ref[sl] + (1.0 - ADAM_B1) * g
                v = ADAM_B2 * v_ref[sl] + (1.0 - ADAM_B2) * (g * g)
                m_hat = m / (1.0 - ADAM_B1 ** ADAM_STEP)
                v_hat = v / (1.0 - ADAM_B2 ** ADAM_STEP)
                d_ref[sl] = -ADAM_LR * (m_hat / (jnp.sqrt(v_hat) + ADAM_EPS) + ADAM_WD * w_ref[sl])
                nm_ref[sl] = m
                nv_ref[sl] = v

            rows = w_ref.shape[0]
            if rows > BLK and rows % BLK == 0:
                def step(r, carry, update=update):
                    update((pl.ds(pl.multiple_of(r * BLK, BLK), BLK), slice(None)))
                    return carry

                lax.fori_loop(0, rows // BLK, step, 0)
            else:
                update((slice(None), slice(None)))

    args, out_shape = [], []
    for k in range(n):
        args += [ws[k], gs[k], ms[k], vs[k]]
        out_shape += [jax.ShapeDtypeStruct(ws[k].shape, F32)] * 3
    vm = pl.BlockSpec(memory_space=pltpu.VMEM)
    outs = pl.pallas_call(
        body, name="adamw", out_shape=out_shape,
        in_specs=[vm] * (4 * n), out_specs=[vm] * (3 * n),
        compiler_params=pltpu.CompilerParams(vmem_limit_bytes=48 << 20),
    )(*args)
    return [tuple(outs[3 * k:3 * k + 3]) for k in range(n)]


def kernel(x, meta, norm_g, final_g, ev_w_in, ev_q_norm_g, ev_kv_norm_g, ev_w_uq, ev_w_ukv, ev_w_out, od_w_in, od_sinks, od_w_out, loss_target, m_meta, m_norm_g, m_final_g, m_ev_w_in, m_ev_q_norm_g, m_ev_kv_norm_g, m_ev_w_uq, m_ev_w_ukv, m_ev_w_out, m_od_w_in, m_od_sinks, m_od_w_out, v_meta, v_norm_g, v_final_g, v_ev_w_in, v_ev_q_norm_g, v_ev_kv_norm_g, v_ev_w_uq, v_ev_w_ukv, v_ev_w_out, v_od_w_in, v_od_sinks, v_od_w_out):
    nbatch, seq, d = x.shape
    nb = seq // BLK + 1
    lp = nb * BLK
    shards = dict(ev_w_in=ev_w_in[0], ev_w_uq=ev_w_uq[0], ev_w_ukv=ev_w_ukv[0], ev_w_out=ev_w_out[0],
                  od_w_in=od_w_in[0], od_w_out=od_w_out[0])

    parts = [shards[name].astype(BF16).reshape(-1, BLK) for name, _, _, _ in _BIG]
    parts.append(lax.bitcast_convert_type(meta, BF16).reshape(2 * N_META, BLK))
    gathered = _all_gather(jnp.concatenate(parts, axis=0)).reshape(N_DEV, PACK_ROWS, BLK)
    full, off = {}, 0
    for name, rows, cols, axis in _BIG:
        n = _shard_rows(rows, cols)
        blk = gathered[:, off:off + n]
        off += n
        if axis == 1:
            full[name] = blk.reshape(N_DEV, rows, cols // N_DEV).transpose(1, 0, 2).reshape(rows, cols)
        else:
            full[name] = blk.reshape(rows, cols)
    meta_bits = gathered[:, off:off + 2 * N_META].reshape(N_DEV, N_META, BLK, 2)
    meta_full = lax.bitcast_convert_type(meta_bits, F32).transpose(1, 0, 2).reshape(N_META, d)

    head = jnp.concatenate([jnp.zeros((N_PAD, d), F32), meta_full], axis=0)
    h0 = jnp.concatenate([jnp.broadcast_to(head[None], (nbatch, BLK, d)), x], axis=1).reshape(nbatch * lp, d)
    grads = _local_step(
        h0, loss_target.reshape(nbatch * seq, d), norm_g, final_g.reshape(1, d), ev_q_norm_g, ev_kv_norm_g,
        od_sinks, _ev_in_to_compute(full["ev_w_in"]), _uq_to_compute(full["ev_w_uq"]), full["ev_w_ukv"],
        full["ev_w_out"], _od_in_to_compute(full["od_w_in"]), full["od_w_out"], nbatch, nb)
    dh0 = grads["dh0"].reshape(nbatch, lp, d)
    grad_x = dh0[:, BLK:]
    loss = lax.psum(0.5 / d * jnp.sum(grads["lossv"]), ("x", "y", "c"))

    chunks = []
    for name, rows, cols, axis in _BIG:
        g = grads[name]
        if axis == 1:
            g = g.reshape(rows, N_DEV, cols // N_DEV).transpose(1, 0, 2)
        chunks.append(g.reshape(N_DEV, _shard_rows(rows, cols), BLK))
    chunks.append(jnp.zeros((N_DEV, 2 * N_META, BLK), F32))
    big = jnp.concatenate(chunks, axis=1).astype(BF16)
    d_meta = dh0[:, N_PAD:BLK].sum(axis=0).reshape(N_META, N_DEV, BLK).transpose(1, 0, 2)
    pad = lambda a, n: jnp.concatenate([a.reshape(1, -1), jnp.zeros((1, n - a.size), F32)], axis=1)
    rep = jnp.concatenate([grads["norm_g"].reshape(1, -1), grads["final_g"], pad(grads["gq"], 512),
                           pad(grads["gkv"], 256), pad(grads["sinks"], 256)], axis=1).reshape(32, BLK)
    small = jnp.concatenate([d_meta, jnp.broadcast_to(rep[None], (N_DEV, 32, BLK))], axis=1)
    red_big, red_small = _reduce_scatter(big, small)

    g_shard, off = {}, 0
    for name, rows, cols, axis in _BIG:
        n = _shard_rows(rows, cols)
        shape = (rows, cols // N_DEV) if axis == 1 else (rows // N_DEV, cols)
        g_shard[name] = red_big[off:off + n].reshape(shape)
        off += n
    rep = red_small[N_META:].reshape(1, -1)
    g_small = dict(meta=red_small[:N_META], norm_g=rep[:, :2 * d].reshape(2, d), final_g=rep[:, 2 * d:3 * d],
                   ev_q_norm_g=rep[:, 3 * d:3 * d + 256], ev_kv_norm_g=rep[:, 3 * d + 512:3 * d + 640],
                   od_sinks=rep[:, 3 * d + 768:3 * d + 784])

    names = ["meta", "norm_g", "final_g", "ev_w_in", "ev_q_norm_g", "ev_kv_norm_g", "ev_w_uq", "ev_w_ukv",
             "ev_w_out", "od_w_in", "od_sinks", "od_w_out"]
    given = dict(meta=(meta, m_meta, v_meta), norm_g=(norm_g, m_norm_g, v_norm_g),
                 final_g=(final_g, m_final_g, v_final_g), ev_w_in=(ev_w_in, m_ev_w_in, v_ev_w_in),
                 ev_q_norm_g=(ev_q_norm_g, m_ev_q_norm_g, v_ev_q_norm_g),
                 ev_kv_norm_g=(ev_kv_norm_g, m_ev_kv_norm_g, v_ev_kv_norm_g),
                 ev_w_uq=(ev_w_uq, m_ev_w_uq, v_ev_w_uq), ev_w_ukv=(ev_w_ukv, m_ev_w_ukv, v_ev_w_ukv),
                 ev_w_out=(ev_w_out, m_ev_w_out, v_ev_w_out), od_w_in=(od_w_in, m_od_w_in, v_od_w_in),
                 od_sinks=(od_sinks, m_od_sinks, v_od_sinks), od_w_out=(od_w_out, m_od_w_out, v_od_w_out))
    ws, gs, ms, vs = [], [], [], []
    for name in names:
        g2 = g_shard[name] if name in g_shard else g_small[name]
        w, m, v = given[name]
        ws.append(w.reshape(g2.shape))
        ms.append(m.reshape(g2.shape))
        vs.append(v.reshape(g2.shape))
        gs.append(g2)
    upd = _adamw(ws, gs, ms, vs)
    shape_of = {name: given[name][0].shape for name in names}
    grads_out = [gs[k].reshape(shape_of[n]) for k, n in enumerate(names)]
    deltas = [upd[k][0].reshape(shape_of[n]) for k, n in enumerate(names)]
    new_m = [upd[k][1].reshape(shape_of[n]) for k, n in enumerate(names)]
    new_v = [upd[k][2].reshape(shape_of[n]) for k, n in enumerate(names)]
    return (loss, grad_x, *grads_out, *deltas, *new_m, *new_v)
```

```python
import jax
import jax.numpy as jnp
from jax import lax
from jax.experimental import pallas as pl
from jax.experimental.pallas import tpu as pltpu

F32 = jnp.float32
BF16 = jnp.bfloat16

D_MODEL = 1024
N_META = 16
BLK = 128
HALF = 64
N_PAD = BLK - N_META
NORM_EPS = 1e-6
NEG = -1e30
N_DEV = 8

SB_SCALE = 64 ** -0.5
MLA_SCALE = 96 ** -0.5
SWA_SCALE = 64 ** -0.5
ROPE_BASE = 10000.0

EV_IN_PAD = 3072
EV_MID_BLK = 4
EV_GMLA_BLK = 20
OD_K_BLK = 16
OD_V_BLK = 17

ADAM_LR = 0.001
ADAM_B1 = 0.9
ADAM_B2 = 0.999
ADAM_EPS = 1e-08
ADAM_WD = 0.01
ADAM_STEP = 10

PACK_ROWS = 7680


def _dot(a, b):
    return lax.dot_general(a, b, (((1,), (0,)), ((), ())), preferred_element_type=F32)


def _dot_nt(a, b):
    return lax.dot_general(a, b, (((1,), (1,)), ((), ())), preferred_element_type=F32)


def _dot_tn(a, b):
    return lax.dot_general(a, b, (((0,), (0,)), ((), ())), preferred_element_type=F32)


def _split(x):
    hi = x.astype(BF16)
    return hi, (x - hi.astype(F32)).astype(BF16)


def _sigmoid(x):
    return 1.0 / (1.0 + jnp.exp(-x))


def _iotas():
    row = lax.broadcasted_iota(jnp.int32, (BLK, BLK), 0)
    lane = lax.broadcasted_iota(jnp.int32, (BLK, BLK), 1)
    return row, lane


WIDE = 2 * BLK


def _key_mask(i, first_key, width, strict):
    t_pos = i * BLK + lax.broadcasted_iota(jnp.int32, (BLK, width), 0)
    s_pos = first_key + lax.broadcasted_iota(jnp.int32, (BLK, width), 1)
    seen = (s_pos < t_pos) if strict else (s_pos <= t_pos)
    return seen & (s_pos >= N_PAD)


def _widen(x, width):
    return x if width == BLK else jnp.concatenate([x] * (width // BLK), axis=1)


def _over_key_blocks(n, block, reverse):
    pairs = n // 2
    last = pl.multiple_of((n - 1) * BLK, BLK)

    def step(jj, carry):
        jp = (pairs - 1 - jj) if reverse else jj
        block(pl.multiple_of(jp * WIDE, WIDE), WIDE)
        return carry

    if reverse:
        pl.when(n % 2 == 1)(lambda: block(last, BLK))
        lax.fori_loop(0, pairs, step, 0)
    else:
        lax.fori_loop(0, pairs, step, 0)
        pl.when(n % 2 == 1)(lambda: block(last, BLK))


def _rope(x, c, s1, s2):
    return x * c + pltpu.roll(x, BLK - 16, 1) * s1 + pltpu.roll(x, 16, 1) * s2


def _rope_t(x, c, s1, s2):
    return x * c - pltpu.roll(x, BLK - 16, 1) * s1 - pltpu.roll(x, 16, 1) * s2


def _params(sem, vmem_mb=None):
    kw = dict(dimension_semantics=sem)
    if vmem_mb is not None:
        kw["vmem_limit_bytes"] = vmem_mb << 20
    return pltpu.CompilerParams(**kw)


def _row_tile(t, cands):
    for c in cands:
        if t % c == 0:
            return c
    raise ValueError(t)


def _mm(a, w, mode, name, add=None, out_dtype=F32):
    pieces = list(a) if isinstance(a, (list, tuple)) else [a]
    m = pieces[0].shape[0]
    n = w.shape[1] if mode == "nn" else w.shape[0]
    tm = _row_tile(m, (256, 128))
    widths = [p.shape[1] for p in pieces]
    offs = [sum(widths[:i]) for i in range(len(widths))]

    def body(*refs):
        p_refs = refs[:len(pieces)]
        w_ref = refs[len(pieces)]
        o_ref = refs[-1]
        acc = None
        for p_ref, off, wd in zip(p_refs, offs, widths):
            x = p_ref[...].astype(BF16)
            part = _dot(x, w_ref[off:off + wd, :]) if mode == "nn" else _dot_nt(x, w_ref[:, off:off + wd])
            acc = part if acc is None else acc + part
        if add is not None:
            acc = acc + refs[len(pieces) + 1][...]
        o_ref[...] = acc.astype(o_ref.dtype)

    in_specs = [pl.BlockSpec((tm, wd), lambda i: (i, 0)) for wd in widths]
    in_specs.append(pl.BlockSpec(w.shape, lambda i: (0, 0)))
    args = pieces + [w]
    if add is not None:
        in_specs.append(pl.BlockSpec((tm, n), lambda i: (i, 0)))
        args.append(add)
    return pl.pallas_call(
        body, name=name, grid=(m // tm,), in_specs=in_specs,
        out_specs=pl.BlockSpec((tm, n), lambda i: (i, 0)),
        out_shape=jax.ShapeDtypeStruct((m, n), out_dtype),
        compiler_params=_params(("parallel",), 48),
    )(*args)


def _mm_tn(x, pieces, name):
    t, k = x.shape
    tt = _row_tile(t, (544, 256, 128))
    widths = [p.shape[1] for p in pieces]

    def body(*refs):
        x_ref = refs[0]
        d_refs = refs[1:1 + len(pieces)]
        o_refs = refs[1 + len(pieces):]

        @pl.when(pl.program_id(0) == 0)
        def _():
            for o_ref in o_refs:
                o_ref[...] = jnp.zeros_like(o_ref)

        xb = x_ref[...].astype(BF16)
        for d_ref, o_ref in zip(d_refs, o_refs):
            o_ref[...] += _dot_tn(xb, d_ref[...].astype(BF16))

    return pl.pallas_call(
        body, name=name, grid=(t // tt,),
        in_specs=[pl.BlockSpec((tt, k), lambda i: (i, 0))] + [pl.BlockSpec((tt, wd), lambda i: (i, 0)) for wd in widths],
        out_specs=[pl.BlockSpec((k, wd), lambda i: (0, 0)) for wd in widths],
        out_shape=[jax.ShapeDtypeStruct((k, wd), F32) for wd in widths],
        compiler_params=_params(("arbitrary",), 56),
    )(x, *pieces)


def _rms_fwd(h, g, name):
    t, d = h.shape
    tr = _row_tile(t, (256, 128))

    def body(h_ref, g_ref, o_ref):
        x = h_ref[...]
        r = lax.rsqrt(jnp.mean(x * x, axis=1, keepdims=True) + NORM_EPS)
        o_ref[...] = (x * r * g_ref[...]).astype(o_ref.dtype)

    return pl.pallas_call(
        body, name=name, grid=(t // tr,),
        in_specs=[pl.BlockSpec((tr, d), lambda i: (i, 0)), pl.BlockSpec((1, d), lambda i: (0, 0))],
        out_specs=pl.BlockSpec((tr, d), lambda i: (i, 0)),
        out_shape=jax.ShapeDtypeStruct((t, d), BF16),
        compiler_params=_params(("parallel",)),
    )(h, g)


def _rms_bwd(h, g, dhn, dres, name):
    t, d = h.shape
    tr = _row_tile(t, (256, 128))

    def body(h_ref, g_ref, dhn_ref, dres_ref, dh_ref, dg_ref):
        @pl.when(pl.program_id(0) == 0)
        def _():
            dg_ref[...] = jnp.zeros_like(dg_ref)

        x = h_ref[...]
        r = lax.rsqrt(jnp.mean(x * x, axis=1, keepdims=True) + NORM_EPS)
        nx = x * r
        dy = dhn_ref[...]
        dn = dy * g_ref[...]
        dh_ref[...] = dres_ref[...] + r * (dn - nx * jnp.mean(dn * nx, axis=1, keepdims=True))
        dg_ref[...] += jnp.sum(dy * nx, axis=0, keepdims=True)

    row = pl.BlockSpec((tr, d), lambda i: (i, 0))
    vec = pl.BlockSpec((1, d), lambda i: (0, 0))
    return pl.pallas_call(
        body, name=name, grid=(t // tr,),
        in_specs=[row, vec, row, row], out_specs=[row, vec],
        out_shape=[jax.ShapeDtypeStruct((t, d), F32), jax.ShapeDtypeStruct((1, d), F32)],
        compiler_params=_params(("arbitrary",)),
    )(h, g, dhn, dres)


def _final(h2, tgt, g, nbatch, nb):
    t, d = h2.shape

    def body(h_ref, t_ref, g_ref, dh_ref, loss_ref, dg_ref):
        b = pl.program_id(0)
        i = pl.program_id(1)

        @pl.when((b == 0) & (i == 0))
        def _():
            loss_ref[...] = jnp.zeros_like(loss_ref)
            dg_ref[...] = jnp.zeros_like(dg_ref)

        x = h_ref[...]
        r = lax.rsqrt(jnp.mean(x * x, axis=1, keepdims=True) + NORM_EPS)
        nx = x * r
        gg = g_ref[...]
        live = jnp.where(i >= 1, 1.0, 0.0)
        err = (nx * gg - t_ref[...]) * live
        loss_ref[...] += jnp.sum(err * err, axis=0, keepdims=True)
        dy = err * (1.0 / d)
        dn = dy * gg
        dh_ref[...] = r * (dn - nx * jnp.mean(dn * nx, axis=1, keepdims=True))
        dg_ref[...] += jnp.sum(dy * nx, axis=0, keepdims=True)

    vec = pl.BlockSpec((1, d), lambda b, i: (0, 0))
    return pl.pallas_call(
        body, name="final_loss", grid=(nbatch, nb),
        in_specs=[pl.BlockSpec((BLK, d), lambda b, i: (b * nb + i, 0)),
                  pl.BlockSpec((BLK, d), lambda b, i: (b * (nb - 1) + jnp.maximum(i - 1, 0), 0)),
                  vec],
        out_specs=[pl.BlockSpec((BLK, d), lambda b, i: (b * nb + i, 0)), vec, vec],
        out_shape=[jax.ShapeDtypeStruct((t, d), F32), jax.ShapeDtypeStruct((1, d), F32),
                   jax.ShapeDtypeStruct((1, d), F32)],
        compiler_params=_params(("arbitrary", "arbitrary")),
    )(h2, tgt, g)


def _sb_logits(z):
    log_beta = jnp.minimum(z, 0.0) - jnp.log(1.0 + jnp.exp(-jnp.abs(z)))
    return log_beta, log_beta - z


def _tri(width, after):
    j = lax.broadcasted_iota(jnp.int32, (width, width), 0)
    s = lax.broadcasted_iota(jnp.int32, (width, width), 1)
    return (j > s) if after else (j < s)


def _tri2(tri, with_ones):
    m = tri.astype(BF16)
    if with_ones:
        m = jnp.concatenate([m, jnp.ones((tri.shape[0], BLK), BF16)], axis=1)
    return jnp.concatenate([m, m], axis=0)


def _block_sums(x, tri_ones, after):
    hi, lo = _split(x)
    subs = [_dot(jnp.concatenate([hi[:, s:s + BLK], lo[:, s:s + BLK]], axis=1), tri_ones)
            for s in range(0, x.shape[1], BLK)]
    if len(subs) == 1:
        return subs[0][:, :BLK], subs[0][:, BLK:]
    first, second = subs
    total = first[:, BLK:] + second[:, BLK:]
    if after:
        return jnp.concatenate([first[:, :BLK] + second[:, BLK:], second[:, :BLK]], axis=1), total
    return jnp.concatenate([first[:, :BLK], second[:, :BLK] + first[:, BLK:]], axis=1), total


def _head_masked(x, lane, scale=None):
    out = []
    for h in range(8):
        xp = x[:, (h // 2) * BLK:(h // 2 + 1) * BLK]
        xm = jnp.where((lane >= HALF) if h % 2 else (lane < HALF), xp, jnp.zeros_like(xp))
        out.append(xm if scale is None else xm * scale)
    return jnp.concatenate(out, axis=1)


def _sb_fwd(proj, nbatch, nb):
    lp = nb * BLK
    t = nbatch * lp

    def body(q_ref, k_ref, v_ref, g_ref, o_ref, og_ref, cm_ref, c_scr, qm_scr):
        i = pl.program_id(1)
        _, lane = _iotas()
        lo_m = lane < HALF
        cm_ref[...] = jnp.zeros_like(cm_ref)
        c_scr[...] = jnp.zeros_like(c_scr)
        o_ref[...] = jnp.zeros_like(o_ref)
        qm_scr[...] = _head_masked(q_ref[...], lane, SB_SCALE)

        def block(off, width):
            mask = _key_mask(i, off, width, strict=True)
            upper = _tri2(_tri(BLK, after=True), True)
            onehot = lane == off // WIDE
            heads = range(8)
            hcs = [slice(h * BLK, (h + 1) * BLK) for h in heads]
            kbs = [k_ref[pl.ds(off, width), hc] for hc in hcs[:4]]
            vbs = [v_ref[pl.ds(off, width), hc] for hc in hcs[:4]]
            zs = [_dot_nt(qm_scr[:, hcs[h]], kbs[h // 2]) for h in heads]
            lbs, l1s = [], []
            for h in heads:
                log_beta, log_1m = _sb_logits(zs[h])
                lbs.append(log_beta)
                l1s.append(jnp.where(mask, log_1m, 0.0))
            css = [_block_sums(l1s[h], upper, after=True) for h in heads]
            avs = []
            for h in heads:
                c = c_scr[h]
                avs.append(jnp.where(mask, jnp.exp(lbs[h] + css[h][0] + _widen(c, width)), 0.0).astype(BF16))
                if width == WIDE:
                    cm_ref[:, hcs[h]] = jnp.where(onehot, c, cm_ref[:, hcs[h]])
                c_scr[h] = c + css[h][1]
            accs = [_dot(avs[h], vbs[h // 2]) for h in heads]
            for p in range(4):
                o_ref[:, hcs[p]] += jnp.where(lo_m, accs[2 * p], accs[2 * p + 1])

        _over_key_blocks(i + 1, block, reverse=True)
        g = g_ref[...].astype(F32)
        og_ref[...] = (o_ref[...] * g * _sigmoid(g)).astype(og_ref.dtype)

    tile = lambda col: pl.BlockSpec((BLK, 512), lambda b, i: (b * nb + i, col))
    full = lambda col: pl.BlockSpec((lp, 512), lambda b, i: (b, col))
    return pl.pallas_call(
        body, name="sb_fwd", grid=(nbatch, nb),
        in_specs=[tile(0), full(1), full(2), tile(3)],
        out_specs=[tile(0), tile(0), pl.BlockSpec((BLK, 1024), lambda b, i: (b * nb + i, 0))],
        out_shape=[jax.ShapeDtypeStruct((t, 512), F32), jax.ShapeDtypeStruct((t, 512), BF16),
                   jax.ShapeDtypeStruct((t, 1024), F32)],
        scratch_shapes=[pltpu.VMEM((8, BLK, BLK), F32), pltpu.VMEM((BLK, 1024), BF16)],
        compiler_params=_params(("parallel", "arbitrary"), 48),
    )(proj, proj, proj, proj)


def _sb_bwd(proj, o, dog, cm, nbatch, nb):
    lp = nb * BLK
    t = nbatch * lp

    def body(q_ref, k_ref, v_ref, g_ref, o_ref, dog_ref, cm_ref, dq_ref, dk_ref, dv_ref, dg_ref,
             c_scr, qm_scr, dom_scr, dq_scr):
        i = pl.program_id(1)

        @pl.when(i == 0)
        def _():
            dk_ref[...] = jnp.zeros_like(dk_ref)
            dv_ref[...] = jnp.zeros_like(dv_ref)

        _, lane = _iotas()
        lo_m = lane < HALF
        g = g_ref[...].astype(F32)
        sig = _sigmoid(g)
        dog_v = dog_ref[...]
        dg_ref[...] = (dog_v * o_ref[...] * (sig * (1.0 + g * (1.0 - sig)))).astype(dg_ref.dtype)
        dom_scr[...] = _head_masked((dog_v * g * sig).astype(BF16), lane)
        qm_scr[...] = _head_masked(q_ref[...], lane, SB_SCALE)
        c_scr[...] = jnp.zeros_like(c_scr)
        dq_scr[...] = jnp.zeros_like(dq_scr)

        def block(off, width):
            mask = _key_mask(i, off, width, strict=True)
            upper = _tri2(_tri(BLK, after=True), True)
            lower = _tri2(_tri(BLK, after=False), True)
            onehot = lane == off // WIDE
            heads = range(8)
            hcs = [slice(h * BLK, (h + 1) * BLK) for h in heads]
            kbs = [k_ref[pl.ds(off, width), hc] for hc in hcs[:4]]
            vbs = [v_ref[pl.ds(off, width), hc] for hc in hcs[:4]]
            zs = [_dot_nt(qm_scr[:, hcs[h]], kbs[h // 2]) for h in heads]
            dps = [_dot_nt(dom_scr[:, hcs[h]], vbs[h // 2]) for h in heads]
            lbs, l1s = [], []
            for h in heads:
                log_beta, log_1m = _sb_logits(zs[h])
                lbs.append(log_beta)
                l1s.append(log_1m)
            sufs = [_block_sums(jnp.where(mask, l1s[h], 0.0), upper, after=True)[0] for h in heads]
            prs, dzzs = [], []
            for h in heads:
                expo = lbs[h] + sufs[h]
                if width == WIDE:
                    expo = expo + jnp.sum(jnp.where(onehot, cm_ref[:, hcs[h]], 0.0), axis=1, keepdims=True)
                pr = jnp.where(mask, jnp.exp(expo), 0.0)
                dzz = pr * dps[h]
                prs.append(pr.astype(BF16))
                dzzs.append(dzz)
            css = [_block_sums(dzzs[h], lower, after=False) for h in heads]
            dzbs = []
            for h in heads:
                c2 = c_scr[h]
                prefix = css[h][0] + _widen(c2, width)
                dz = jnp.where(mask, dzzs[h] * jnp.exp(l1s[h]) - jnp.exp(lbs[h]) * prefix, 0.0)
                dzbs.append(dz.astype(BF16))
                c_scr[h] = c2 + css[h][1]
            dqs = [_dot(dzbs[h], kbs[h // 2]) for h in heads]
            dks = [_dot_tn(dzbs[h], qm_scr[:, hcs[h]]) for h in heads]
            dvs = [_dot_tn(prs[h], dom_scr[:, hcs[h]]) for h in heads]
            for p in range(4):
                dq_scr[:, hcs[p]] += jnp.where(lo_m, dqs[2 * p], dqs[2 * p + 1])
                dk_ref[pl.ds(off, width), hcs[p]] += dks[2 * p] + dks[2 * p + 1]
                dv_ref[pl.ds(off, width), hcs[p]] += dvs[2 * p] + dvs[2 * p + 1]

        _over_key_blocks(i + 1, block, reverse=False)
        dq_ref[...] = (dq_scr[...] * SB_SCALE).astype(dq_ref.dtype)

    tile = lambda col: pl.BlockSpec((BLK, 512), lambda b, i: (b * nb + i, col))
    full = lambda col: pl.BlockSpec((lp, 512), lambda b, i: (b, col))
    acc = jax.ShapeDtypeStruct((t, 512), F32)
    once = jax.ShapeDtypeStruct((t, 512), BF16)
    return pl.pallas_call(
        body, name="sb_bwd", grid=(nbatch, nb),
        in_specs=[tile(0), full(1), full(2), tile(3), tile(0), tile(0),
                  pl.BlockSpec((BLK, 1024), lambda b, i: (b * nb + i, 0))],
        out_specs=[tile(0), full(0), full(0), tile(0)],
        out_shape=[once, acc, acc, once],
        scratch_shapes=[pltpu.VMEM((8, BLK, BLK), F32), pltpu.VMEM((BLK, 1024), BF16),
                        pltpu.VMEM((BLK, 1024), BF16), pltpu.VMEM((BLK, 512), F32)],
        compiler_params=_params(("parallel", "arbitrary"), 56),
    )(proj, proj, proj, proj, o, dog, cm)


def _mla_prep(proj, gq, gkv, tabs, nbatch, nb):
    t = proj.shape[0]

    def body(mid_ref, gq_ref, gkv_ref, c_ref, s1_ref, s2_ref, cq_ref, ckv_ref, kr_ref):
        cq = mid_ref[:, 0:256].astype(F32)
        r = lax.rsqrt(jnp.mean(cq * cq, axis=1, keepdims=True) + NORM_EPS)
        cq_ref[...] = (cq * r * gq_ref[...]).astype(BF16)
        ckv = mid_ref[:, 256:384].astype(F32)
        r = lax.rsqrt(jnp.mean(ckv * ckv, axis=1, keepdims=True) + NORM_EPS)
        ckv_ref[...] = (ckv * r * gkv_ref[...]).astype(BF16)
        kr = mid_ref[:, 384:512].astype(F32)
        kr_ref[...] = _rope(kr, c_ref[...], s1_ref[...], s2_ref[...]).astype(BF16)

    tab = pl.BlockSpec((BLK, BLK), lambda b, i: (i, 0))
    rowspec = lambda w: pl.BlockSpec((BLK, w), lambda b, i: (b * nb + i, 0))
    return pl.pallas_call(
        body, name="mla_prep", grid=(nbatch, nb),
        in_specs=[pl.BlockSpec((BLK, 512), lambda b, i: (b * nb + i, EV_MID_BLK)),
                  pl.BlockSpec((1, 256), lambda b, i: (0, 0)), pl.BlockSpec((1, 128), lambda b, i: (0, 0)),
                  tab, tab, tab],
        out_specs=[rowspec(256), rowspec(128), rowspec(128)],
        out_shape=[jax.ShapeDtypeStruct((t, 256), BF16), jax.ShapeDtypeStruct((t, 128), BF16),
                   jax.ShapeDtypeStruct((t, 128), BF16)],
        compiler_params=_params(("parallel", "parallel")),
    )(proj, gq, gkv, *tabs)


def _mla_prep_bwd(proj, gq, gkv, tabs, dcqn, dckvn, dkrot, nbatch, nb):
    t = proj.shape[0]

    def body(mid_ref, gq_ref, gkv_ref, c_ref, s1_ref, s2_ref, dcq_ref, dckv_ref, dkr_ref,
             dmid_ref, dgq_ref, dgkv_ref):
        @pl.when((pl.program_id(0) == 0) & (pl.program_id(1) == 0))
        def _():
            dgq_ref[...] = jnp.zeros_like(dgq_ref)
            dgkv_ref[...] = jnp.zeros_like(dgkv_ref)

        def norm_bwd(x, gain, dy, dgain_ref):
            r = lax.rsqrt(jnp.mean(x * x, axis=1, keepdims=True) + NORM_EPS)
            nx = x * r
            dn = dy * gain
            dgain_ref[...] += jnp.sum(dy * nx, axis=0, keepdims=True)
            return r * (dn - nx * jnp.mean(dn * nx, axis=1, keepdims=True))

        dmid_ref[:, 0:256] = norm_bwd(
            mid_ref[:, 0:256].astype(F32), gq_ref[...], dcq_ref[...], dgq_ref).astype(BF16)
        dmid_ref[:, 256:384] = norm_bwd(
            mid_ref[:, 256:384].astype(F32), gkv_ref[...], dckv_ref[...], dgkv_ref).astype(BF16)
        dmid_ref[:, 384:512] = _rope_t(dkr_ref[...], c_ref[...], s1_ref[...], s2_ref[...]).astype(BF16)

    tab = pl.BlockSpec((BLK, BLK), lambda b, i: (i, 0))
    rowspec = lambda w: pl.BlockSpec((BLK, w), lambda b, i: (b * nb + i, 0))
    vq = pl.BlockSpec((1, 256), lambda b, i: (0, 0))
    vkv = pl.BlockSpec((1, 128), lambda b, i: (0, 0))
    return pl.pallas_call(
        body, name="mla_prep_bwd", grid=(nbatch, nb),
        in_specs=[pl.BlockSpec((BLK, 512), lambda b, i: (b * nb + i, EV_MID_BLK)), vq, vkv, tab, tab, tab,
                  rowspec(256), rowspec(128), rowspec(128)],
        out_specs=[rowspec(512), vq, vkv],
        out_shape=[jax.ShapeDtypeStruct((t, 512), BF16), jax.ShapeDtypeStruct((1, 256), F32),
                   jax.ShapeDtypeStruct((1, 128), F32)],
        compiler_params=_params(("arbitrary", "arbitrary")),
    )(proj, gq, gkv, *tabs, dcqn, dckvn, dkrot)


def _mla_scores(qf, kvb, krb, mask, lo_m):
    kf = jnp.where(lo_m, kvb, krb)
    s = _dot_nt(qf, kf) * MLA_SCALE
    return kf, jnp.where(mask, s, NEG)


def _mla_fwd(qh, kvh, krot, proj, tabs, nbatch, nb):
    lp = nb * BLK
    t = nbatch * lp

    def body(q_ref, kv_ref, kr_ref, g_ref, c_ref, s1_ref, s2_ref, o_ref, og_ref, lse_ref,
             qf_scr, m_scr, l_scr, acc_scr):
        i = pl.program_id(1)
        row, lane = _iotas()
        lo_m = lane < HALF
        for h in range(8):
            hc = slice(h * BLK, (h + 1) * BLK)
            qf_scr[:, hc] = _rope(q_ref[:, hc].astype(F32), c_ref[...], s1_ref[...], s2_ref[...]).astype(BF16)
        m_scr[...] = jnp.full(m_scr.shape, NEG, F32)
        l_scr[...] = jnp.zeros_like(l_scr)
        acc_scr[...] = jnp.zeros_like(acc_scr)

        def block(off, width):
            mask = _key_mask(i, off, width, strict=False)
            lo_k = lax.broadcasted_iota(jnp.int32, (width, BLK), 1) < HALF
            ones = jnp.ones((width, BLK), BF16)
            krb = kr_ref[pl.ds(off, width), :]
            heads = range(8)
            hcs = [slice(h * BLK, (h + 1) * BLK) for h in heads]
            kvbs = [kv_ref[pl.ds(off, width), hc] for hc in hcs]
            ss = [_mla_scores(qf_scr[:, hcs[h]], kvbs[h], krb, mask, lo_k)[1] for h in heads]
            ps, alphas = [], []
            for h in heads:
                m = m_scr[h]
                m2 = jnp.maximum(m, jnp.max(ss[h], axis=1, keepdims=True))
                ps.append(jnp.exp(ss[h] - _widen(m2, width)).astype(BF16))
                alphas.append(jnp.exp(m - m2))
                m_scr[h] = m2
            pvs = [_dot(ps[h], jnp.concatenate([kvbs[h], ones], axis=1)) for h in heads]
            for h in heads:
                l_scr[h] = alphas[h] * l_scr[h] + pvs[h][:, BLK:]
                acc_scr[h] = alphas[h] * acc_scr[h] + pvs[h][:, :BLK]

        _over_key_blocks(i + 1, block, reverse=False)
        lse = jnp.zeros((BLK, BLK), F32)
        for p in range(4):
            pc = slice(p * BLK, (p + 1) * BLK)
            o0 = acc_scr[2 * p] / l_scr[2 * p]
            o1 = acc_scr[2 * p + 1] / l_scr[2 * p + 1]
            o_ref[:, pc] = jnp.where(lo_m, pltpu.roll(o0, HALF, 1), o1)
            for h in (2 * p, 2 * p + 1):
                lse = lse + jnp.where(lane == h, m_scr[h] + jnp.log(l_scr[h]), 0.0)
        lse_ref[...] = lse
        g = g_ref[...].astype(F32)
        og_ref[...] = (o_ref[...] * g * _sigmoid(g)).astype(og_ref.dtype)

    tile = pl.BlockSpec((BLK, 512), lambda b, i: (b * nb + i, 0))
    tab = pl.BlockSpec((BLK, BLK), lambda b, i: (i, 0))
    heads = pltpu.VMEM((8, BLK, BLK), F32)
    return pl.pallas_call(
        body, name="mla_fwd", grid=(nbatch, nb),
        in_specs=[pl.BlockSpec((BLK, 1024), lambda b, i: (b * nb + i, 0)),
                  pl.BlockSpec((lp, 1024), lambda b, i: (b, 0)),
                  pl.BlockSpec((lp, BLK), lambda b, i: (b, 0)),
                  pl.BlockSpec((BLK, 512), lambda b, i: (b * nb + i, EV_GMLA_BLK // 4)),
                  tab, tab, tab],
        out_specs=[tile, tile, pl.BlockSpec((BLK, BLK), lambda b, i: (b * nb + i, 0))],
        out_shape=[jax.ShapeDtypeStruct((t, 512), F32), jax.ShapeDtypeStruct((t, 512), BF16),
                   jax.ShapeDtypeStruct((t, BLK), F32)],
        scratch_shapes=[pltpu.VMEM((BLK, 1024), BF16), heads, heads, heads],
        compiler_params=_params(("parallel", "arbitrary"), 48),
    )(qh, kvh, krot, proj, *tabs)


def _mla_bwd(qh, kvh, krot, proj, tabs, o, dog, lse, nbatch, nb):
    lp = nb * BLK
    t = nbatch * lp

    def body(q_ref, kv_ref, kr_ref, g_ref, c_ref, s1_ref, s2_ref, o_ref, dog_ref, lse_ref,
             dq_ref, dkv_ref, dkr_ref, dg_ref, qf_scr, do_scr, stat_scr, acc_scr):
        i = pl.program_id(1)

        @pl.when(i == 0)
        def _():
            dkv_ref[...] = jnp.zeros_like(dkv_ref)
            dkr_ref[...] = jnp.zeros_like(dkr_ref)

        row, lane = _iotas()
        lo_m = lane < HALF
        g = g_ref[...].astype(F32)
        sig = _sigmoid(g)
        dog_v = dog_ref[...]
        o_v = o_ref[...]
        dg_ref[...] = (dog_v * o_v * (sig * (1.0 + g * (1.0 - sig)))).astype(dg_ref.dtype)
        do = dog_v * g * sig
        do_o = do * o_v
        lse_blk = lse_ref[...]
        zero = jnp.zeros((BLK, BLK), F32)
        for h in range(8):
            hc = slice(h * BLK, (h + 1) * BLK)
            pc = slice((h // 2) * BLK, (h // 2 + 1) * BLK)
            qf_scr[:, hc] = _rope(q_ref[:, hc].astype(F32), c_ref[...], s1_ref[...], s2_ref[...]).astype(BF16)
            dop = do[:, pc]
            do_src = dop if h % 2 else pltpu.roll(dop, HALF, 1)
            do_scr[:, hc] = jnp.where(lo_m, 0.0, do_src).astype(BF16)
            hm = (lane >= HALF) if h % 2 else lo_m
            stat_scr[h] = zero + jnp.sum(jnp.where(hm, do_o[:, pc], 0.0), axis=1, keepdims=True)
            stat_scr[8 + h] = zero + jnp.sum(jnp.where(lane == h, lse_blk, 0.0), axis=1, keepdims=True)
        acc_scr[...] = jnp.zeros_like(acc_scr)

        def block(off, width):
            mask = _key_mask(i, off, width, strict=False)
            lo_k = lax.broadcasted_iota(jnp.int32, (width, BLK), 1) < HALF
            krb = kr_ref[pl.ds(off, width), :]
            heads = range(8)
            hcs = [slice(h * BLK, (h + 1) * BLK) for h in heads]
            kvbs = [kv_ref[pl.ds(off, width), hc] for hc in hcs]
            qfs = [qf_scr[:, hc] for hc in hcs]
            dos = [do_scr[:, hc] for hc in hcs]
            scored = [_mla_scores(qfs[h], kvbs[h], krb, mask, lo_k) for h in heads]
            dps = [_dot_nt(dos[h], kvbs[h]) for h in heads]
            pbs, dss = [], []
            for h in heads:
                p = jnp.exp(scored[h][1] - _widen(stat_scr[8 + h], width))
                pbs.append(p.astype(BF16))
                dss.append((p * (dps[h] - _widen(stat_scr[h], width)) * MLA_SCALE).astype(BF16))
            dqs = [_dot(dss[h], scored[h][0]) for h in heads]
            dkfs = [_dot_tn(dss[h], qfs[h]) for h in heads]
            dvvs = [_dot_tn(pbs[h], dos[h]) for h in heads]
            dkr = jnp.zeros((width, BLK), F32)
            for h in heads:
                acc_scr[h] += dqs[h]
                dkv_ref[pl.ds(off, width), hcs[h]] += jnp.where(lo_k, dkfs[h], 0.0) + dvvs[h]
                dkr = dkr + jnp.where(lo_k, 0.0, dkfs[h])
            dkr_ref[pl.ds(off, width), :] += dkr

        _over_key_blocks(i + 1, block, reverse=False)
        for h in range(8):
            hc = slice(h * BLK, (h + 1) * BLK)
            dq_ref[:, hc] = _rope_t(acc_scr[h], c_ref[...], s1_ref[...], s2_ref[...]).astype(dq_ref.dtype)

    tile = lambda col: pl.BlockSpec((BLK, 512), lambda b, i: (b * nb + i, col))
    wide = pl.BlockSpec((BLK, 1024), lambda b, i: (b * nb + i, 0))
    full8 = pl.BlockSpec((lp, 1024), lambda b, i: (b, 0))
    full1 = pl.BlockSpec((lp, BLK), lambda b, i: (b, 0))
    tab = pl.BlockSpec((BLK, BLK), lambda b, i: (i, 0))
    return pl.pallas_call(
        body, name="mla_bwd", grid=(nbatch, nb),
        in_specs=[wide, full8, full1, tile(EV_GMLA_BLK // 4), tab, tab, tab, tile(0), tile(1),
                  pl.BlockSpec((BLK, BLK), lambda b, i: (b * nb + i, 0))],
        out_specs=[wide, full8, full1, tile(0)],
        out_shape=[jax.ShapeDtypeStruct((t, 1024), BF16), jax.ShapeDtypeStruct((t, 1024), F32),
                   jax.ShapeDtypeStruct((t, 128), F32), jax.ShapeDtypeStruct((t, 512), BF16)],
        scratch_shapes=[pltpu.VMEM((BLK, 1024), BF16), pltpu.VMEM((BLK, 1024), BF16),
                        pltpu.VMEM((16, BLK, BLK), F32), pltpu.VMEM((8, BLK, BLK), F32)],
        compiler_params=_params(("parallel", "arbitrary"), 56),
    )(qh, kvh, krot, proj, *tabs, o, dog, lse)


def _swa_setup(kk, i, k_refs, v_refs):
    row, lane = _iotas()
    own = (lane >= kk * HALF) & (lane < (kk + 1) * HALF)

    def dup(ref):
        x = ref[...].astype(F32)
        return jnp.where(own, x, pltpu.roll(x, HALF, 1)).astype(BF16)

    kcat = jnp.concatenate([dup(r) for r in k_refs], axis=0)
    vcat = jnp.concatenate([dup(r) for r in v_refs], axis=0)
    row3 = lax.broadcasted_iota(jnp.int32, (BLK, 3 * BLK), 0)
    lane3 = lax.broadcasted_iota(jnp.int32, (BLK, 3 * BLK), 1)
    is_meta = lane3 < BLK
    k_pos = jnp.where(is_meta, lane3, (i - 2) * BLK + lane3)
    d = i * BLK + row3 - k_pos
    mask = (d >= 0) & (d < jnp.where(is_meta, 1 << 20, BLK)) & (k_pos >= jnp.where(is_meta, N_PAD, BLK))
    return lane, own, kcat, vcat, mask, d.astype(F32)


def _swa_slope(kk, g_idx):
    return (2.0 ** (-(g_idx + 1) / 2.0)) * jnp.where(kk == 0, 1.0, 1.0 / 16.0)


def _swa_fwd(proj, sinks, nbatch, nb):
    lp = nb * BLK
    t = nbatch * lp

    def body(sink_ref, q_ref, ka, kb, kc, va, vb, vc, g_ref, o_ref, og_ref, lse_ref):
        kk = pl.program_id(1)
        i = pl.program_id(2)
        lane, own, kcat, vcat, mask, dist = _swa_setup(kk, i, (ka, kb, kc), (va, vb, vc))
        lo_m = lane < HALF
        heads = range(8)
        qms = []
        for h in heads:
            qp = q_ref[:, (h // 2) * BLK:(h // 2 + 1) * BLK]
            qms.append(jnp.where((lane >= HALF) if h % 2 else lo_m, qp, jnp.zeros_like(qp)))
        qks = [_dot_nt(qms[h], kcat) for h in heads]
        ps, ls, lses = [], [], []
        for h in heads:
            sink = sink_ref[kk, h]
            s = jnp.where(mask, qks[h] * SWA_SCALE - _swa_slope(kk, h) * dist, NEG)
            mx = jnp.maximum(jnp.max(s, axis=1, keepdims=True), sink)
            p = jnp.exp(s - mx)
            l = jnp.exp(sink - mx) + jnp.sum(p, axis=1, keepdims=True)
            ps.append(p.astype(BF16))
            ls.append(l)
            lses.append(mx + jnp.log(l))
        pvs = [_dot(ps[h], vcat) for h in heads]
        lse_out = jnp.zeros((BLK, BLK), F32)
        for m in range(4):
            cols = slice(m * BLK, (m + 1) * BLK)
            outp = jnp.where(lo_m, pvs[2 * m] / ls[2 * m], pvs[2 * m + 1] / ls[2 * m + 1])
            o_ref[:, cols] = outp
            g = g_ref[:, cols].astype(F32)
            og_ref[:, cols] = (outp * g * _sigmoid(g)).astype(og_ref.dtype)
            for h in (2 * m, 2 * m + 1):
                lse_out = lse_out + jnp.where(lane == h, lses[h], 0.0)
        lse_ref[...] = lse_out

    def kvspec(col, which):
        if which == 0:
            return pl.BlockSpec((BLK, BLK), lambda b, kk, i: (b * nb, col))
        if which == 1:
            return pl.BlockSpec((BLK, BLK), lambda b, kk, i: (b * nb + jnp.maximum(i - 1, 0), col))
        return pl.BlockSpec((BLK, BLK), lambda b, kk, i: (b * nb + i, col))

    wide = lambda c0: pl.BlockSpec((BLK, 512), lambda b, kk, i: (b * nb + i, c0 + kk))
    return pl.pallas_call(
        body, name="swa_fwd", grid=(nbatch, 2, nb),
        in_specs=[pl.BlockSpec(memory_space=pltpu.SMEM), wide(0),
                  kvspec(OD_K_BLK, 0), kvspec(OD_K_BLK, 1), kvspec(OD_K_BLK, 2),
                  kvspec(OD_V_BLK, 0), kvspec(OD_V_BLK, 1), kvspec(OD_V_BLK, 2), wide(2)],
        out_specs=[wide(0), wide(0), pl.BlockSpec((BLK, BLK), lambda b, kk, i: (b * nb + i, kk))],
        out_shape=[jax.ShapeDtypeStruct((t, 1024), F32), jax.ShapeDtypeStruct((t, 1024), BF16),
                   jax.ShapeDtypeStruct((t, 256), F32)],
        compiler_params=_params(("parallel", "parallel", "arbitrary")),
    )(sinks, proj, proj, proj, proj, proj, proj, proj, proj)


def _swa_bwd(proj, sinks, o, dog, lse, nbatch, nb):
    lp = nb * BLK
    t = nbatch * lp

    def body(sink_ref, q_ref, ka, kb, kc, va, vb, vc, g_ref, o_ref, dog_ref, lse_ref,
             dq_ref, dg_ref, dk_ref, dv_ref, dsink_ref):
        kk = pl.program_id(1)
        i = pl.program_id(2)

        @pl.when((kk == 0) & (i == 0))
        def _():
            dk_ref[...] = jnp.zeros_like(dk_ref)
            dv_ref[...] = jnp.zeros_like(dv_ref)

        @pl.when(i == 0)
        def _():
            dsink_ref[...] = jnp.zeros_like(dsink_ref)

        lane, own, kcat, vcat, mask, dist = _swa_setup(kk, i, (ka, kb, kc), (va, vb, vc))
        lo_m = lane < HALF
        row8 = lax.broadcasted_iota(jnp.int32, (8, BLK), 0)
        lse_blk = lse_ref[...]
        heads = range(8)
        qms, doms, deltas, lse_hs = [], [], [], []
        for m in range(4):
            cols = slice(m * BLK, (m + 1) * BLK)
            qp = q_ref[:, cols]
            g = g_ref[:, cols].astype(F32)
            sig = _sigmoid(g)
            dog_v = dog_ref[:, cols]
            o_v = o_ref[:, cols]
            dg_ref[:, cols] = (dog_v * o_v * (sig * (1.0 + g * (1.0 - sig)))).astype(dg_ref.dtype)
            do = dog_v * g * sig
            do_o = do * o_v
            dob = do.astype(BF16)
            for h in (2 * m, 2 * m + 1):
                hm = (lane >= HALF) if h % 2 else lo_m
                qms.append(jnp.where(hm, qp, jnp.zeros_like(qp)))
                doms.append(jnp.where(hm, dob, jnp.zeros_like(dob)))
                deltas.append(jnp.sum(jnp.where(hm, do_o, 0.0), axis=1, keepdims=True))
                lse_hs.append(jnp.sum(jnp.where(lane == h, lse_blk, 0.0), axis=1, keepdims=True))
        qks = [_dot_nt(qms[h], kcat) for h in heads]
        dps = [_dot_nt(doms[h], vcat) for h in heads]
        pbs, dss = [], []
        dsink = jnp.zeros((8, BLK), F32)
        for h in heads:
            s = jnp.where(mask, qks[h] * SWA_SCALE - _swa_slope(kk, h) * dist, NEG)
            p = jnp.exp(s - lse_hs[h])
            pbs.append(p.astype(BF16))
            dss.append((p * (dps[h] - deltas[h]) * SWA_SCALE).astype(BF16))
            tot = jnp.sum(-jnp.exp(sink_ref[kk, h] - lse_hs[h]) * deltas[h], axis=0, keepdims=True)
            dsink = dsink + jnp.where(row8 == h, tot, 0.0)
        dsink_ref[...] += dsink
        dqs = [_dot(dss[h], kcat) for h in heads]
        dks = [_dot_tn(dss[h], qms[h]) for h in heads]
        dvs = [_dot_tn(pbs[h], doms[h]) for h in heads]
        for m in range(4):
            dq_ref[:, m * BLK:(m + 1) * BLK] = jnp.where(lo_m, dqs[2 * m], dqs[2 * m + 1]).astype(dq_ref.dtype)
        dk = dks[0]
        dv = dvs[0]
        for h in range(1, 8):
            dk = dk + dks[h]
            dv = dv + dvs[h]
        offs = [0, pl.multiple_of(jnp.maximum(i - 1, 0) * BLK, BLK), pl.multiple_of(i * BLK, BLK)]
        for x in range(3):
            rows = slice(x * BLK, (x + 1) * BLK)
            dkx, dvx = dk[rows], dv[rows]
            dk_ref[pl.ds(offs[x], BLK), :] += jnp.where(own, dkx + pltpu.roll(dkx, HALF, 1), 0.0)
            dv_ref[pl.ds(offs[x], BLK), :] += jnp.where(own, dvx + pltpu.roll(dvx, HALF, 1), 0.0)

    def kvspec(col, which):
        if which == 0:
            return pl.BlockSpec((BLK, BLK), lambda b, kk, i: (b * nb, col))
        if which == 1:
            return pl.BlockSpec((BLK, BLK), lambda b, kk, i: (b * nb + jnp.maximum(i - 1, 0), col))
        return pl.BlockSpec((BLK, BLK), lambda b, kk, i: (b * nb + i, col))

    wide = lambda c0: pl.BlockSpec((BLK, 512), lambda b, kk, i: (b * nb + i, c0 + kk))
    full = pl.BlockSpec((lp, BLK), lambda b, kk, i: (b, 0))
    return pl.pallas_call(
        body, name="swa_bwd", grid=(nbatch, 2, nb),
        in_specs=[pl.BlockSpec(memory_space=pltpu.SMEM), wide(0),
                  kvspec(OD_K_BLK, 0), kvspec(OD_K_BLK, 1), kvspec(OD_K_BLK, 2),
                  kvspec(OD_V_BLK, 0), kvspec(OD_V_BLK, 1), kvspec(OD_V_BLK, 2), wide(2),
                  wide(0), wide(0), pl.BlockSpec((BLK, BLK), lambda b, kk, i: (b * nb + i, kk))],
        out_specs=[wide(0), wide(0), full, full,
                   pl.BlockSpec((8, BLK), lambda b, kk, i: (b * 2 + kk, 0))],
        out_shape=[jax.ShapeDtypeStruct((t, 1024), BF16), jax.ShapeDtypeStruct((t, 1024), BF16),
                   jax.ShapeDtypeStruct((t, 128), F32), jax.ShapeDtypeStruct((t, 128), F32),
                   jax.ShapeDtypeStruct((nbatch * 16, BLK), F32)],
        compiler_params=_params(("parallel", "arbitrary", "arbitrary")),
    )(sinks, proj, proj, proj, proj, proj, proj, proj, proj, o, dog, lse)


def _rope_tables(lp):
    pos = (jnp.arange(lp) - N_PAD).astype(F32)
    inv = ROPE_BASE ** (-jnp.arange(16, dtype=F32) / 16.0)
    ang = pos[:, None] * inv[None, :]
    cos, sin = jnp.cos(ang), jnp.sin(ang)
    z16 = jnp.zeros((lp, 16), F32)
    c = jnp.concatenate([jnp.ones((lp, 64), F32), cos, cos, jnp.zeros((lp, 32), F32)], axis=1)
    s1 = jnp.concatenate([jnp.zeros((lp, 64), F32), -sin, z16, jnp.zeros((lp, 32), F32)], axis=1)
    s2 = jnp.concatenate([jnp.zeros((lp, 64), F32), z16, sin, jnp.zeros((lp, 32), F32)], axis=1)
    return c, s1, s2


def _local_step(h0, tgt, norm_g, final_g, gq, gkv, sinks, w_ie, w_uq, w_ukv, w_oe, w_io, w_oo, nbatch, nb):
    lp = nb * BLK
    tabs = _rope_tables(lp)
    g0, g1 = norm_g[0:1], norm_g[1:2]
    sinks2 = sinks.reshape(2, 8)

    hn0 = _rms_fwd(h0, g0, "rms_fwd0")
    proj_e = _mm(hn0, w_ie, "nn", "proj_even", out_dtype=BF16)
    o_sb, og_sb, cm = _sb_fwd(proj_e, nbatch, nb)
    cqn, ckvn, krot = _mla_prep(proj_e, gq, gkv, tabs, nbatch, nb)
    qh = _mm(cqn, w_uq, "nn", "mla_uq", out_dtype=BF16)
    kvh = _mm(ckvn, w_ukv, "nn", "mla_ukv", out_dtype=BF16)
    o_mla, og_mla, lse_m = _mla_fwd(qh, kvh, krot, proj_e, tabs, nbatch, nb)
    h1 = _mm([og_sb, og_mla], w_oe, "nn", "out_even", add=h0)
    hn1 = _rms_fwd(h1, g1, "rms_fwd1")
    proj_o = _mm(hn1, w_io, "nn", "proj_odd", out_dtype=BF16)
    o_o, og_o, lse_o = _swa_fwd(proj_o, sinks2, nbatch, nb)
    h2 = _mm(og_o, w_oo, "nn", "out_odd", add=h1)
    dh2, lossv, d_final_g = _final(h2, tgt, final_g, nbatch, nb)

    dog_o = _mm(dh2, w_oo, "nt", "d_out_odd")
    d_w_oo, = _mm_tn(og_o, [dh2], "dw_out_odd")
    dq_o, dg_o, dk_o, dv_o, dsink = _swa_bwd(proj_o, sinks2, o_o, dog_o, lse_o, nbatch, nb)
    dproj_o = [dq_o, dg_o, dk_o, dv_o]
    dhn1 = _mm(dproj_o, w_io, "nt", "d_proj_odd")
    dw_q, dw_g, dw_k, dw_v = _mm_tn(hn1, dproj_o, "dw_proj_odd")
    dh1, d_g1 = _rms_bwd(h1, g1, dhn1, dh2, "rms_bwd1")

    dog_e = _mm(dh1, w_oe, "nt", "d_out_even")
    d_w_oe_sb, = _mm_tn(og_sb, [dh1], "dw_out_even_sb")
    d_w_oe_mla, = _mm_tn(og_mla, [dh1], "dw_out_even_mla")
    dq_sb, dk_sb, dv_sb, dg_sb = _sb_bwd(proj_e, o_sb, dog_e, cm, nbatch, nb)
    dqh, dkvh, dkrot, dg_mla = _mla_bwd(qh, kvh, krot, proj_e, tabs, o_mla, dog_e, lse_m, nbatch, nb)
    dcqn = _mm(dqh, w_uq, "nt", "d_mla_uq")
    d_w_uq, = _mm_tn(cqn, [dqh], "dw_mla_uq")
    dckvn = _mm(dkvh, w_ukv, "nt", "d_mla_ukv")
    d_w_ukv, = _mm_tn(ckvn, [dkvh], "dw_mla_ukv")
    dmid, d_gq, d_gkv = _mla_prep_bwd(proj_e, gq, gkv, tabs, dcqn, dckvn, dkrot, nbatch, nb)
    dproj_e = [dq_sb, dk_sb, dv_sb, dg_sb, dmid, dg_mla]
    dhn0 = _mm(dproj_e, w_ie, "nt", "d_proj_even")
    dw_e = _mm_tn(hn0, dproj_e, "dw_proj_even")
    dh0, d_g0 = _rms_bwd(h0, g0, dhn0, dh1, "rms_bwd0")

    d_sinks = dsink.reshape(nbatch, 2, 8, BLK)[:, :, :, 0].sum(axis=0).reshape(1, 16)
    d_norm_g = jnp.concatenate([d_g0, d_g1], axis=0)
    d_ev_w_in = jnp.concatenate(list(dw_e[:4]) + [dw_e[4][:, :384], dw_e[4][:, 448:480], dw_e[5]], axis=1)
    d_od_w_in = jnp.concatenate([dw_q, dw_k, dw_v, dw_g], axis=1)
    d_ev_w_out = jnp.concatenate([d_w_oe_sb, d_w_oe_mla], axis=0)
    return dict(lossv=lossv, dh0=dh0, norm_g=d_norm_g, final_g=d_final_g, gq=d_gq, gkv=d_gkv, sinks=d_sinks,
                ev_w_in=d_ev_w_in, ev_w_uq=_uq_from_compute(d_w_uq), ev_w_ukv=d_w_ukv, ev_w_out=d_ev_w_out,
                od_w_in=d_od_w_in, od_w_out=d_w_oo)


def _ev_in_to_compute(w):
    z = lambda n: jnp.zeros((w.shape[0], n), w.dtype)
    return jnp.concatenate([w[:, :2432], z(64), w[:, 2432:2464], z(32), w[:, 2464:]], axis=1)


def _uq_to_compute(w):
    w3 = w.reshape(256, 8, 96)
    return jnp.concatenate([w3, jnp.zeros((256, 8, 32), w.dtype)], axis=2).reshape(256, 1024)


def _uq_from_compute(w):
    return w.reshape(256, 8, 128)[:, :, :96].reshape(256, 768)


def _od_in_to_compute(w):
    return jnp.concatenate([w[:, :1024], w[:, 1280:], w[:, 1024:1280]], axis=1)


_BIG = (("ev_w_in", 1024, 2976, 1), ("ev_w_uq", 256, 768, 1), ("ev_w_ukv", 128, 1024, 1),
        ("ev_w_out", 1024, 1024, 0), ("od_w_in", 1024, 2304, 1), ("od_w_out", 1024, 1024, 0))


def _shard_rows(rows, cols):
    return rows * cols // N_DEV // BLK


def _all_gather(x_shard):
    m_per, n = x_shard.shape

    def body(x_ref, out_ref, send_sems, recv_sems, local_sem):
        x, y, c = lax.axis_index("x"), lax.axis_index("y"), lax.axis_index("c")
        me, sibling = (x, y, c), (x, y, 1 - c)
        chips = [(1 - x, y), (x, 1 - y), (1 - x, 1 - y)]

        def rows(px, py, pc):
            return out_ref.at[pl.ds((4 * px + 2 * py + pc) * m_per, m_per), :]

        def copy(k, block, to, src=None):
            return pltpu.make_async_remote_copy(
                src_ref=rows(*block) if src is None else src, dst_ref=rows(*block),
                send_sem=send_sems.at[k], recv_sem=recv_sems.at[k],
                device_id=to, device_id_type=pl.DeviceIdType.MESH)

        mine = pltpu.make_async_copy(x_ref, rows(*me), local_sem)
        mine.start()
        first = [copy(0, me, sibling, src=x_ref)]
        first += [copy(1 + j, me, (*chip, c), src=x_ref) for j, chip in enumerate(chips)]
        for cp in first:
            cp.start()
        passed = [copy(4 + j, (*chip, c), sibling) for j, chip in enumerate(chips)]
        for j, chip in enumerate(chips):
            copy(1 + j, (*chip, c), me).wait_recv()
            passed[j].start()
        copy(0, sibling, me).wait_recv()
        for j, chip in enumerate(chips):
            copy(4 + j, (*chip, 1 - c), me).wait_recv()
        for cp in first + passed:
            cp.wait_send()
        mine.wait()

    return pl.pallas_call(
        body, name="gather_weights",
        out_shape=jax.ShapeDtypeStruct((N_DEV * m_per, n), x_shard.dtype),
        in_specs=[pl.BlockSpec(memory_space=pltpu.VMEM)],
        out_specs=pl.BlockSpec(memory_space=pltpu.VMEM),
        scratch_shapes=[pltpu.SemaphoreType.DMA((7,)), pltpu.SemaphoreType.DMA((7,)), pltpu.SemaphoreType.DMA],
        compiler_params=pltpu.CompilerParams(vmem_limit_bytes=48 << 20),
    )(x_shard)


def _reduce_scatter(big, small):
    _, rb, _ = big.shape
    _, rs, _ = small.shape
    chunk = 256

    def body(big_ref, small_ref, obig_ref, osmall_ref, rbig, rsmall, ssem, rsem, ssem2, rsem2, lsem):
        x, y, c = lax.axis_index("x"), lax.axis_index("y"), lax.axis_index("c")
        me = 4 * x + 2 * y + c
        own_b = pltpu.make_async_copy(big_ref.at[me], rbig.at[me], lsem.at[0])
        own_s = pltpu.make_async_copy(small_ref.at[me], rsmall.at[me], lsem.at[1])
        own_b.start()
        own_s.start()
        sends, recvs = [], []
        for d in range(1, N_DEV):
            px = x + ((d >> 2) & 1) - 2 * x * ((d >> 2) & 1)
            py = y + ((d >> 1) & 1) - 2 * y * ((d >> 1) & 1)
            pc = c + (d & 1) - 2 * c * (d & 1)
            pid = 4 * px + 2 * py + pc
            kw = dict(device_id=(px, py, pc), device_id_type=pl.DeviceIdType.MESH)
            sends.append(pltpu.make_async_remote_copy(
                src_ref=big_ref.at[pid], dst_ref=rbig.at[me], send_sem=ssem.at[d - 1], recv_sem=rsem.at[d - 1], **kw))
            sends.append(pltpu.make_async_remote_copy(
                src_ref=small_ref.at[pid], dst_ref=rsmall.at[me], send_sem=ssem2.at[d - 1], recv_sem=rsem2.at[d - 1], **kw))
            recvs.append(pltpu.make_async_remote_copy(
                src_ref=big_ref.at[pid], dst_ref=rbig.at[pid], send_sem=ssem.at[d - 1], recv_sem=rsem.at[d - 1], **kw))
            recvs.append(pltpu.make_async_remote_copy(
                src_ref=small_ref.at[pid], dst_ref=rsmall.at[pid], send_sem=ssem2.at[d - 1], recv_sem=rsem2.at[d - 1], **kw))
        for cp in sends:
            cp.start()
        own_b.wait()
        own_s.wait()
        for cp in recvs:
            cp.wait_recv()
        for cp in sends:
            cp.wait_send()

        acc = rsmall[0]
        for k in range(1, N_DEV):
            acc = acc + rsmall[k]
        osmall_ref[...] = acc

        def step(r, carry):
            off = pl.multiple_of(r * chunk, chunk)
            a = rbig[0, pl.ds(off, chunk), :].astype(F32)
            for k in range(1, N_DEV):
                a = a + rbig[k, pl.ds(off, chunk), :].astype(F32)
            obig_ref[pl.ds(off, chunk), :] = a
            return carry

        lax.fori_loop(0, rb // chunk, step, 0)

    return pl.pallas_call(
        body, name="reduce_grads",
        out_shape=[jax.ShapeDtypeStruct((rb, BLK), F32), jax.ShapeDtypeStruct((rs, BLK), F32)],
        in_specs=[pl.BlockSpec(memory_space=pl.ANY), pl.BlockSpec(memory_space=pl.ANY)],
        out_specs=[pl.BlockSpec(memory_space=pltpu.VMEM), pl.BlockSpec(memory_space=pltpu.VMEM)],
        scratch_shapes=[pltpu.VMEM((N_DEV, rb, BLK), BF16), pltpu.VMEM((N_DEV, rs, BLK), F32),
                        pltpu.SemaphoreType.DMA((7,)), pltpu.SemaphoreType.DMA((7,)),
                        pltpu.SemaphoreType.DMA((7,)), pltpu.SemaphoreType.DMA((7,)),
                        pltpu.SemaphoreType.DMA((2,))],
        compiler_params=pltpu.CompilerParams(vmem_limit_bytes=48 << 20),
    )(big, small)


def _adamw(ws, gs, ms, vs):
    n = len(ws)

    def body(*refs):
        ins, outs = refs[:4 * n], refs[4 * n:]
        for k in range(n):
            w_ref, g_ref, m_ref, v_ref = ins[4 * k:4 * k + 4]
            d_ref, nm_ref, nv_ref = outs[3 * k:3 * k + 3]

            def update(sl, w_ref=w_ref, g_ref=g_ref, m_ref=m_ref, v_ref=v_ref,
                       d_ref=d_ref, nm_ref=nm_ref, nv_ref=nv_ref):
                g = g_ref[sl]
                m = ADAM_B1 * m_ref[sl] + (1.0 - ADAM_B1) * g
                v = ADAM_B2 * v_ref[sl] + (1.0 - ADAM_B2) * (g * g)
                m_hat = m / (1.0 - ADAM_B1 ** ADAM_STEP)
                v_hat = v / (1.0 - ADAM_B2 ** ADAM_STEP)
                d_ref[sl] = -ADAM_LR * (m_hat / (jnp.sqrt(v_hat) + ADAM_EPS) + ADAM_WD * w_ref[sl])
                nm_ref[sl] = m
                nv_ref[sl] = v

            rows = w_ref.shape[0]
            if rows > BLK and rows % BLK == 0:
                def step(r, carry, update=update):
                    update((pl.ds(pl.multiple_of(r * BLK, BLK), BLK), slice(None)))
                    return carry

                lax.fori_loop(0, rows // BLK, step, 0)
            else:
                update((slice(None), slice(None)))

    args, out_shape = [], []
    for k in range(n):
        args += [ws[k], gs[k], ms[k], vs[k]]
        out_shape += [jax.ShapeDtypeStruct(ws[k].shape, F32)] * 3
    vm = pl.BlockSpec(memory_space=pltpu.VMEM)
    outs = pl.pallas_call(
        body, name="adamw", out_shape=out_shape,
        in_specs=[vm] * (4 * n), out_specs=[vm] * (3 * n),
        compiler_params=pltpu.CompilerParams(vmem_limit_bytes=48 << 20),
    )(*args)
    return [tuple(outs[3 * k:3 * k + 3]) for k in range(n)]


def kernel(x, meta, norm_g, final_g, ev_w_in, ev_q_norm_g, ev_kv_norm_g, ev_w_uq, ev_w_ukv, ev_w_out, od_w_in, od_sinks, od_w_out, loss_target, m_meta, m_norm_g, m_final_g, m_ev_w_in, m_ev_q_norm_g, m_ev_kv_norm_g, m_ev_w_uq, m_ev_w_ukv, m_ev_w_out, m_od_w_in, m_od_sinks, m_od_w_out, v_meta, v_norm_g, v_final_g, v_ev_w_in, v_ev_q_norm_g, v_ev_kv_norm_g, v_ev_w_uq, v_ev_w_ukv, v_ev_w_out, v_od_w_in, v_od_sinks, v_od_w_out):
    nbatch, seq, d = x.shape
    nb = seq // BLK + 1
    lp = nb * BLK
    shards = dict(ev_w_in=ev_w_in[0], ev_w_uq=ev_w_uq[0], ev_w_ukv=ev_w_ukv[0], ev_w_out=ev_w_out[0],
                  od_w_in=od_w_in[0], od_w_out=od_w_out[0])

    parts = [shards[name].astype(BF16).reshape(-1, BLK) for name, _, _, _ in _BIG]
    parts.append(lax.bitcast_convert_type(meta, BF16).reshape(2 * N_META, BLK))
    gathered = _all_gather(jnp.concatenate(parts, axis=0)).reshape(N_DEV, PACK_ROWS, BLK)
    full, off = {}, 0
    for name, rows, cols, axis in _BIG:
        n = _shard_rows(rows, cols)
        blk = gathered[:, off:off + n]
        off += n
        if axis == 1:
            full[name] = blk.reshape(N_DEV, rows, cols // N_DEV).transpose(1, 0, 2).reshape(rows, cols)
        else:
            full[name] = blk.reshape(rows, cols)
    meta_bits = gathered[:, off:off + 2 * N_META].reshape(N_DEV, N_META, BLK, 2)
    meta_full = lax.bitcast_convert_type(meta_bits, F32).transpose(1, 0, 2).reshape(N_META, d)

    head = jnp.concatenate([jnp.zeros((N_PAD, d), F32), meta_full], axis=0)
    h0 = jnp.concatenate([jnp.broadcast_to(head[None], (nbatch, BLK, d)), x], axis=1).reshape(nbatch * lp, d)
    grads = _local_step(
        h0, loss_target.reshape(nbatch * seq, d), norm_g, final_g.reshape(1, d), ev_q_norm_g, ev_kv_norm_g,
        od_sinks, _ev_in_to_compute(full["ev_w_in"]), _uq_to_compute(full["ev_w_uq"]), full["ev_w_ukv"],
        full["ev_w_out"], _od_in_to_compute(full["od_w_in"]), full["od_w_out"], nbatch, nb)
    dh0 = grads["dh0"].reshape(nbatch, lp, d)
    grad_x = dh0[:, BLK:]
    loss = lax.psum(0.5 / d * jnp.sum(grads["lossv"]), ("x", "y", "c"))

    chunks = []
    for name, rows, cols, axis in _BIG:
        g = grads[name]
        if axis == 1:
            g = g.reshape(rows, N_DEV, cols // N_DEV).transpose(1, 0, 2)
        chunks.append(g.reshape(N_DEV, _shard_rows(rows, cols), BLK))
    chunks.append(jnp.zeros((N_DEV, 2 * N_META, BLK), F32))
    big = jnp.concatenate(chunks, axis=1).astype(BF16)
    d_meta = dh0[:, N_PAD:BLK].sum(axis=0).reshape(N_META, N_DEV, BLK).transpose(1, 0, 2)
    pad = lambda a, n: jnp.concatenate([a.reshape(1, -1), jnp.zeros((1, n - a.size), F32)], axis=1)
    rep = jnp.concatenate([grads["norm_g"].reshape(1, -1), grads["final_g"], pad(grads["gq"], 512),
                           pad(grads["gkv"], 256), pad(grads["sinks"], 256)], axis=1).reshape(32, BLK)
    small = jnp.concatenate([d_meta, jnp.broadcast_to(rep[None], (N_DEV, 32, BLK))], axis=1)
    red_big, red_small = _reduce_scatter(big, small)

    g_shard, off = {}, 0
    for name, rows, cols, axis in _BIG:
        n = _shard_rows(rows, cols)
        shape = (rows, cols // N_DEV) if axis == 1 else (rows // N_DEV, cols)
        g_shard[name] = red_big[off:off + n].reshape(shape)
        off += n
    rep = red_small[N_META:].reshape(1, -1)
    g_small = dict(meta=red_small[:N_META], norm_g=rep[:, :2 * d].reshape(2, d), final_g=rep[:, 2 * d:3 * d],
                   ev_q_norm_g=rep[:, 3 * d:3 * d + 256], ev_kv_norm_g=rep[:, 3 * d + 512:3 * d + 640],
                   od_sinks=rep[:, 3 * d + 768:3 * d + 784])

    names = ["meta", "norm_g", "final_g", "ev_w_in", "ev_q_norm_g", "ev_kv_norm_g", "ev_w_uq", "ev_w_ukv",
             "ev_w_out", "od_w_in", "od_sinks", "od_w_out"]
    given = dict(meta=(meta, m_meta, v_meta), norm_g=(norm_g, m_norm_g, v_norm_g),
                 final_g=(final_g, m_final_g, v_final_g), ev_w_in=(ev_w_in, m_ev_w_in, v_ev_w_in),
                 ev_q_norm_g=(ev_q_norm_g, m_ev_q_norm_g, v_ev_q_norm_g),
                 ev_kv_norm_g=(ev_kv_norm_g, m_ev_kv_norm_g, v_ev_kv_norm_g),
                 ev_w_uq=(ev_w_uq, m_ev_w_uq, v_ev_w_uq), ev_w_ukv=(ev_w_ukv, m_ev_w_ukv, v_ev_w_ukv),
                 ev_w_out=(ev_w_out, m_ev_w_out, v_ev_w_out), od_w_in=(od_w_in, m_od_w_in, v_od_w_in),
                 od_sinks=(od_sinks, m_od_sinks, v_od_sinks), od_w_out=(od_w_out, m_od_w_out, v_od_w_out))
    ws, gs, ms, vs = [], [], [], []
    for name in names:
        g2 = g_shard[name] if name in g_shard else g_small[name]
        w, m, v = given[name]
        ws.append(w.reshape(g2.shape))
        ms.append(m.reshape(g2.shape))
        vs.append(v.reshape(g2.shape))
        gs.append(g2)
    upd = _adamw(ws, gs, ms, vs)
    shape_of = {name: given[name][0].shape for name in names}
    grads_out = [gs[k].reshape(shape_of[n]) for k, n in enumerate(names)]
    deltas = [upd[k][0].reshape(shape_of[n]) for k, n in enumerate(names)]
    new_m = [upd[k][1].reshape(shape_of[n]) for k, n in enumerate(names)]
    new_v = [upd[k][2].reshape(shape_of[n]) for k, n in enumerate(names)]
    return (loss, grad_x, *grads_out, *deltas, *new_m, *new_v)
```

```python
import jax
import jax.numpy as jnp
from jax import lax
from jax.experimental import pallas as pl
from jax.experimental.pallas import tpu as pltpu

F32 = jnp.float32
BF16 = jnp.bfloat16

D_MODEL = 1024
N_META = 16
BLK = 128
HALF = 64
N_PAD = BLK - N_META
NORM_EPS = 1e-6
NEG = -1e30
N_DEV = 8

SB_SCALE = 64 ** -0.5
MLA_SCALE = 96 ** -0.5
SWA_SCALE = 64 ** -0.5
ROPE_BASE = 10000.0

EV_IN_PAD = 3072
EV_MID_BLK = 4
EV_GMLA_BLK = 20
OD_K_BLK = 16
OD_V_BLK = 17

ADAM_LR = 0.001
ADAM_B1 = 0.9
ADAM_B2 = 0.999
ADAM_EPS = 1e-08
ADAM_WD = 0.01
ADAM_STEP = 10


def _dot(a, b):
    return lax.dot_general(a, b, (((1,), (0,)), ((), ())), preferred_element_type=F32)


def _dot_nt(a, b):
    return lax.dot_general(a, b, (((1,), (1,)), ((), ())), preferred_element_type=F32)


def _dot_tn(a, b):
    return lax.dot_general(a, b, (((0,), (0,)), ((), ())), preferred_element_type=F32)


def _split(x):
    hi = x.astype(BF16)
    return hi, (x - hi.astype(F32)).astype(BF16)


def _sigmoid(x):
    return 1.0 / (1.0 + jnp.exp(-x))


def _iotas():
    row = lax.broadcasted_iota(jnp.int32, (BLK, BLK), 0)
    lane = lax.broadcasted_iota(jnp.int32, (BLK, BLK), 1)
    return row, lane


WIDE = 2 * BLK


def _key_mask(i, first_key, width, strict):
    t_pos = i * BLK + lax.broadcasted_iota(jnp.int32, (BLK, width), 0)
    s_pos = first_key + lax.broadcasted_iota(jnp.int32, (BLK, width), 1)
    seen = (s_pos < t_pos) if strict else (s_pos <= t_pos)
    return seen & (s_pos >= N_PAD)


def _widen(x, width):
    return x if width == BLK else jnp.concatenate([x] * (width // BLK), axis=1)


def _over_key_blocks(n, block, reverse):
    pairs = n // 2
    last = pl.multiple_of((n - 1) * BLK, BLK)

    def step(jj, carry):
        jp = (pairs - 1 - jj) if reverse else jj
        block(pl.multiple_of(jp * WIDE, WIDE), WIDE)
        return carry

    if reverse:
        pl.when(n % 2 == 1)(lambda: block(last, BLK))
        lax.fori_loop(0, pairs, step, 0)
    else:
        lax.fori_loop(0, pairs, step, 0)
        pl.when(n % 2 == 1)(lambda: block(last, BLK))


def _rope(x, c, s1, s2):
    return x * c + pltpu.roll(x, BLK - 16, 1) * s1 + pltpu.roll(x, 16, 1) * s2


def _rope_t(x, c, s1, s2):
    return x * c - pltpu.roll(x, BLK - 16, 1) * s1 - pltpu.roll(x, 16, 1) * s2


def _params(sem, vmem_mb=None):
    kw = dict(dimension_semantics=sem)
    if vmem_mb is not None:
        kw["vmem_limit_bytes"] = vmem_mb << 20
    return pltpu.CompilerParams(**kw)


def _row_tile(t, cands):
    for c in cands:
        if t % c == 0:
            return c
    raise ValueError(t)


def _mm(a, w, mode, name, add=None, out_dtype=F32):
    pieces = list(a) if isinstance(a, (list, tuple)) else [a]
    m = pieces[0].shape[0]
    n = w.shape[1] if mode == "nn" else w.shape[0]
    tm = _row_tile(m, (256, 128))
    widths = [p.shape[1] for p in pieces]
    offs = [sum(widths[:i]) for i in range(len(widths))]

    def body(*refs):
        p_refs = refs[:len(pieces)]
        w_ref = refs[len(pieces)]
        o_ref = refs[-1]
        acc = None
        for p_ref, off, wd in zip(p_refs, offs, widths):
            x = p_ref[...].astype(BF16)
            part = _dot(x, w_ref[off:off + wd, :]) if mode == "nn" else _dot_nt(x, w_ref[:, off:off + wd])
            acc = part if acc is None else acc + part
        if add is not None:
            acc = acc + refs[len(pieces) + 1][...]
        o_ref[...] = acc.astype(o_ref.dtype)

    in_specs = [pl.BlockSpec((tm, wd), lambda i: (i, 0)) for wd in widths]
    in_specs.append(pl.BlockSpec(w.shape, lambda i: (0, 0)))
    args = pieces + [w]
    if add is not None:
        in_specs.append(pl.BlockSpec((tm, n), lambda i: (i, 0)))
        args.append(add)
    return pl.pallas_call(
        body, name=name, grid=(m // tm,), in_specs=in_specs,
        out_specs=pl.BlockSpec((tm, n), lambda i: (i, 0)),
        out_shape=jax.ShapeDtypeStruct((m, n), out_dtype),
        compiler_params=_params(("parallel",), 48),
    )(*args)


def _mm_tn(x, pieces, name):
    t, k = x.shape
    tt = _row_tile(t, (544, 256, 128))
    widths = [p.shape[1] for p in pieces]

    def body(*refs):
        x_ref = refs[0]
        d_refs = refs[1:1 + len(pieces)]
        o_refs = refs[1 + len(pieces):]

        @pl.when(pl.program_id(0) == 0)
        def _():
            for o_ref in o_refs:
                o_ref[...] = jnp.zeros_like(o_ref)

        xb = x_ref[...].astype(BF16)
        for d_ref, o_ref in zip(d_refs, o_refs):
            o_ref[...] += _dot_tn(xb, d_ref[...].astype(BF16))

    return pl.pallas_call(
        body, name=name, grid=(t // tt,),
        in_specs=[pl.BlockSpec((tt, k), lambda i: (i, 0))] + [pl.BlockSpec((tt, wd), lambda i: (i, 0)) for wd in widths],
        out_specs=[pl.BlockSpec((k, wd), lambda i: (0, 0)) for wd in widths],
        out_shape=[jax.ShapeDtypeStruct((k, wd), F32) for wd in widths],
        compiler_params=_params(("arbitrary",), 56),
    )(x, *pieces)


def _rms_fwd(h, g, name):
    t, d = h.shape
    tr = _row_tile(t, (256, 128))

    def body(h_ref, g_ref, o_ref):
        x = h_ref[...]
        r = lax.rsqrt(jnp.mean(x * x, axis=1, keepdims=True) + NORM_EPS)
        o_ref[...] = (x * r * g_ref[...]).astype(o_ref.dtype)

    return pl.pallas_call(
        body, name=name, grid=(t // tr,),
        in_specs=[pl.BlockSpec((tr, d), lambda i: (i, 0)), pl.BlockSpec((1, d), lambda i: (0, 0))],
        out_specs=pl.BlockSpec((tr, d), lambda i: (i, 0)),
        out_shape=jax.ShapeDtypeStruct((t, d), BF16),
        compiler_params=_params(("parallel",)),
    )(h, g)


def _rms_bwd(h, g, dhn, dres, name):
    t, d = h.shape
    tr = _row_tile(t, (256, 128))

    def body(h_ref, g_ref, dhn_ref, dres_ref, dh_ref, dg_ref):
        @pl.when(pl.program_id(0) == 0)
        def _():
            dg_ref[...] = jnp.zeros_like(dg_ref)

        x = h_ref[...]
        r = lax.rsqrt(jnp.mean(x * x, axis=1, keepdims=True) + NORM_EPS)
        nx = x * r
        dy = dhn_ref[...]
        dn = dy * g_ref[...]
        dh_ref[...] = dres_ref[...] + r * (dn - nx * jnp.mean(dn * nx, axis=1, keepdims=True))
        dg_ref[...] += jnp.sum(dy * nx, axis=0, keepdims=True)

    row = pl.BlockSpec((tr, d), lambda i: (i, 0))
    vec = pl.BlockSpec((1, d), lambda i: (0, 0))
    return pl.pallas_call(
        body, name=name, grid=(t // tr,),
        in_specs=[row, vec, row, row], out_specs=[row, vec],
        out_shape=[jax.ShapeDtypeStruct((t, d), F32), jax.ShapeDtypeStruct((1, d), F32)],
        compiler_params=_params(("arbitrary",)),
    )(h, g, dhn, dres)


def _final(h2, tgt, g, nbatch, nb):
    t, d = h2.shape

    def body(h_ref, t_ref, g_ref, dh_ref, loss_ref, dg_ref):
        b = pl.program_id(0)
        i = pl.program_id(1)

        @pl.when((b == 0) & (i == 0))
        def _():
            loss_ref[...] = jnp.zeros_like(loss_ref)
            dg_ref[...] = jnp.zeros_like(dg_ref)

        x = h_ref[...]
        r = lax.rsqrt(jnp.mean(x * x, axis=1, keepdims=True) + NORM_EPS)
        nx = x * r
        gg = g_ref[...]
        live = jnp.where(i >= 1, 1.0, 0.0)
        err = (nx * gg - t_ref[...]) * live
        loss_ref[...] += jnp.sum(err * err, axis=0, keepdims=True)
        dy = err * (1.0 / d)
        dn = dy * gg
        dh_ref[...] = r * (dn - nx * jnp.mean(dn * nx, axis=1, keepdims=True))
        dg_ref[...] += jnp.sum(dy * nx, axis=0, keepdims=True)

    vec = pl.BlockSpec((1, d), lambda b, i: (0, 0))
    return pl.pallas_call(
        body, name="final_loss", grid=(nbatch, nb),
        in_specs=[pl.BlockSpec((BLK, d), lambda b, i: (b * nb + i, 0)),
                  pl.BlockSpec((BLK, d), lambda b, i: (b * (nb - 1) + jnp.maximum(i - 1, 0), 0)),
                  vec],
        out_specs=[pl.BlockSpec((BLK, d), lambda b, i: (b * nb + i, 0)), vec, vec],
        out_shape=[jax.ShapeDtypeStruct((t, d), F32), jax.ShapeDtypeStruct((1, d), F32),
                   jax.ShapeDtypeStruct((1, d), F32)],
        compiler_params=_params(("arbitrary", "arbitrary")),
    )(h2, tgt, g)


def _sb_logits(z):
    log_beta = jnp.minimum(z, 0.0) - jnp.log(1.0 + jnp.exp(-jnp.abs(z)))
    return log_beta, log_beta - z


def _tri(width, after):
    j = lax.broadcasted_iota(jnp.int32, (width, width), 0)
    s = lax.broadcasted_iota(jnp.int32, (width, width), 1)
    return (j > s) if after else (j < s)


def _tri2(tri, with_ones):
    m = tri.astype(BF16)
    if with_ones:
        m = jnp.concatenate([m, jnp.ones((tri.shape[0], BLK), BF16)], axis=1)
    return jnp.concatenate([m, m], axis=0)


def _block_sums(x, tri_ones, after):
    hi, lo = _split(x)
    subs = [_dot(jnp.concatenate([hi[:, s:s + BLK], lo[:, s:s + BLK]], axis=1), tri_ones)
            for s in range(0, x.shape[1], BLK)]
    if len(subs) == 1:
        return subs[0][:, :BLK], subs[0][:, BLK:]
    first, second = subs
    total = first[:, BLK:] + second[:, BLK:]
    if after:
        return jnp.concatenate([first[:, :BLK] + second[:, BLK:], second[:, :BLK]], axis=1), total
    return jnp.concatenate([first[:, :BLK], second[:, :BLK] + first[:, BLK:]], axis=1), total


def _head_masked(x, lane, scale=None):
    out = []
    for h in range(8):
        xp = x[:, (h // 2) * BLK:(h // 2 + 1) * BLK]
        xm = jnp.where((lane >= HALF) if h % 2 else (lane < HALF), xp, jnp.zeros_like(xp))
        out.append(xm if scale is None else xm * scale)
    return jnp.concatenate(out, axis=1)


def _sb_fwd(proj, nbatch, nb):
    lp = nb * BLK
    t = nbatch * lp

    def body(q_ref, k_ref, v_ref, g_ref, o_ref, og_ref, cm_ref, c_scr, qm_scr):
        i = pl.program_id(1)
        _, lane = _iotas()
        lo_m = lane < HALF
        cm_ref[...] = jnp.zeros_like(cm_ref)
        c_scr[...] = jnp.zeros_like(c_scr)
        o_ref[...] = jnp.zeros_like(o_ref)
        qm_scr[...] = _head_masked(q_ref[...], lane, SB_SCALE)

        def block(off, width):
            mask = _key_mask(i, off, width, strict=True)
            upper = _tri2(_tri(BLK, after=True), True)
            onehot = lane == off // WIDE
            heads = range(8)
            hcs = [slice(h * BLK, (h + 1) * BLK) for h in heads]
            kbs = [k_ref[pl.ds(off, width), hc] for hc in hcs[:4]]
            vbs = [v_ref[pl.ds(off, width), hc] for hc in hcs[:4]]
            zs = [_dot_nt(qm_scr[:, hcs[h]], kbs[h // 2]) for h in heads]
            lbs, l1s = [], []
            for h in heads:
                log_beta, log_1m = _sb_logits(zs[h])
                lbs.append(log_beta)
                l1s.append(jnp.where(mask, log_1m, 0.0))
            css = [_block_sums(l1s[h], upper, after=True) for h in heads]
            avs = []
            for h in heads:
                c = c_scr[h]
                avs.append(jnp.where(mask, jnp.exp(lbs[h] + css[h][0] + _widen(c, width)), 0.0).astype(BF16))
                if width == WIDE:
                    cm_ref[:, hcs[h]] = jnp.where(onehot, c, cm_ref[:, hcs[h]])
                c_scr[h] = c + css[h][1]
            accs = [_dot(avs[h], vbs[h // 2]) for h in heads]
            for p in range(4):
                o_ref[:, hcs[p]] += jnp.where(lo_m, accs[2 * p], accs[2 * p + 1])

        _over_key_blocks(i + 1, block, reverse=True)
        g = g_ref[...].astype(F32)
        og_ref[...] = (o_ref[...] * g * _sigmoid(g)).astype(og_ref.dtype)

    tile = lambda col: pl.BlockSpec((BLK, 512), lambda b, i: (b * nb + i, col))
    full = lambda col: pl.BlockSpec((lp, 512), lambda b, i: (b, col))
    return pl.pallas_call(
        body, name="sb_fwd", grid=(nbatch, nb),
        in_specs=[tile(0), full(1), full(2), tile(3)],
        out_specs=[tile(0), tile(0), pl.BlockSpec((BLK, 1024), lambda b, i: (b * nb + i, 0))],
        out_shape=[jax.ShapeDtypeStruct((t, 512), F32), jax.ShapeDtypeStruct((t, 512), BF16),
                   jax.ShapeDtypeStruct((t, 1024), F32)],
        scratch_shapes=[pltpu.VMEM((8, BLK, BLK), F32), pltpu.VMEM((BLK, 1024), BF16)],
        compiler_params=_params(("parallel", "arbitrary"), 48),
    )(proj, proj, proj, proj)


def _sb_bwd(proj, o, dog, cm, nbatch, nb):
    lp = nb * BLK
    t = nbatch * lp

    def body(q_ref, k_ref, v_ref, g_ref, o_ref, dog_ref, cm_ref, dq_ref, dk_ref, dv_ref, dg_ref,
             c_scr, qm_scr, dom_scr, dq_scr):
        i = pl.program_id(1)

        @pl.when(i == 0)
        def _():
            dk_ref[...] = jnp.zeros_like(dk_ref)
            dv_ref[...] = jnp.zeros_like(dv_ref)

        _, lane = _iotas()
        lo_m = lane < HALF
        g = g_ref[...].astype(F32)
        sig = _sigmoid(g)
        dog_v = dog_ref[...]
        dg_ref[...] = (dog_v * o_ref[...] * (sig * (1.0 + g * (1.0 - sig)))).astype(dg_ref.dtype)
        dom_scr[...] = _head_masked((dog_v * g * sig).astype(BF16), lane)
        qm_scr[...] = _head_masked(q_ref[...], lane, SB_SCALE)
        c_scr[...] = jnp.zeros_like(c_scr)
        dq_scr[...] = jnp.zeros_like(dq_scr)

        def block(off, width):
            mask = _key_mask(i, off, width, strict=True)
            upper = _tri2(_tri(BLK, after=True), True)
            lower = _tri2(_tri(BLK, after=False), True)
            onehot = lane == off // WIDE
            heads = range(8)
            hcs = [slice(h * BLK, (h + 1) * BLK) for h in heads]
            kbs = [k_ref[pl.ds(off, width), hc] for hc in hcs[:4]]
            vbs = [v_ref[pl.ds(off, width), hc] for hc in hcs[:4]]
            zs = [_dot_nt(qm_scr[:, hcs[h]], kbs[h // 2]) for h in heads]
            dps = [_dot_nt(dom_scr[:, hcs[h]], vbs[h // 2]) for h in heads]
            lbs, l1s = [], []
            for h in heads:
                log_beta, log_1m = _sb_logits(zs[h])
                lbs.append(log_beta)
                l1s.append(log_1m)
            sufs = [_block_sums(jnp.where(mask, l1s[h], 0.0), upper, after=True)[0] for h in heads]
            prs, dzzs = [], []
            for h in heads:
                expo = lbs[h] + sufs[h]
                if width == WIDE:
                    expo = expo + jnp.sum(jnp.where(onehot, cm_ref[:, hcs[h]], 0.0), axis=1, keepdims=True)
                pr = jnp.where(mask, jnp.exp(expo), 0.0)
                dzz = pr * dps[h]
                prs.append(pr.astype(BF16))
                dzzs.append(dzz)
            css = [_block_sums(dzzs[h], lower, after=False) for h in heads]
            dzbs = []
            for h in heads:
                c2 = c_scr[h]
                prefix = css[h][0] + _widen(c2, width)
                dz = jnp.where(mask, dzzs[h] * jnp.exp(l1s[h]) - jnp.exp(lbs[h]) * prefix, 0.0)
                dzbs.append(dz.astype(BF16))
                c_scr[h] = c2 + css[h][1]
            dqs = [_dot(dzbs[h], kbs[h // 2]) for h in heads]
            dks = [_dot_tn(dzbs[h], qm_scr[:, hcs[h]]) for h in heads]
            dvs = [_dot_tn(prs[h], dom_scr[:, hcs[h]]) for h in heads]
            for p in range(4):
                dq_scr[:, hcs[p]] += jnp.where(lo_m, dqs[2 * p], dqs[2 * p + 1])
                dk_ref[pl.ds(off, width), hcs[p]] += dks[2 * p] + dks[2 * p + 1]
                dv_ref[pl.ds(off, width), hcs[p]] += dvs[2 * p] + dvs[2 * p + 1]

        _over_key_blocks(i + 1, block, reverse=False)
        dq_ref[...] = (dq_scr[...] * SB_SCALE).astype(dq_ref.dtype)

    tile = lambda col: pl.BlockSpec((BLK, 512), lambda b, i: (b * nb + i, col))
    full = lambda col: pl.BlockSpec((lp, 512), lambda b, i: (b, col))
    acc = jax.ShapeDtypeStruct((t, 512), F32)
    once = jax.ShapeDtypeStruct((t, 512), BF16)
    return pl.pallas_call(
        body, name="sb_bwd", grid=(nbatch, nb),
        in_specs=[tile(0), full(1), full(2), tile(3), tile(0), tile(0),
                  pl.BlockSpec((BLK, 1024), lambda b, i: (b * nb + i, 0))],
        out_specs=[tile(0), full(0), full(0), tile(0)],
        out_shape=[once, acc, acc, once],
        scratch_shapes=[pltpu.VMEM((8, BLK, BLK), F32), pltpu.VMEM((BLK, 1024), BF16),
                        pltpu.VMEM((BLK, 1024), BF16), pltpu.VMEM((BLK, 512), F32)],
        compiler_params=_params(("parallel", "arbitrary"), 56),
    )(proj, proj, proj, proj, o, dog, cm)


def _mla_prep(proj, gq, gkv, tabs, nbatch, nb):
    t = proj.shape[0]

    def body(mid_ref, gq_ref, gkv_ref, c_ref, s1_ref, s2_ref, cq_ref, ckv_ref, kr_ref):
        cq = mid_ref[:, 0:256].astype(F32)
        r = lax.rsqrt(jnp.mean(cq * cq, axis=1, keepdims=True) + NORM_EPS)
        cq_ref[...] = (cq * r * gq_ref[...]).astype(BF16)
        ckv = mid_ref[:, 256:384].astype(F32)
        r = lax.rsqrt(jnp.mean(ckv * ckv, axis=1, keepdims=True) + NORM_EPS)
        ckv_ref[...] = (ckv * r * gkv_ref[...]).astype(BF16)
        kr = mid_ref[:, 384:512].astype(F32)
        kr_ref[...] = _rope(kr, c_ref[...], s1_ref[...], s2_ref[...]).astype(BF16)

    tab = pl.BlockSpec((BLK, BLK), lambda b, i: (i, 0))
    rowspec = lambda w: pl.BlockSpec((BLK, w), lambda b, i: (b * nb + i, 0))
    return pl.pallas_call(
        body, name="mla_prep", grid=(nbatch, nb),
        in_specs=[pl.BlockSpec((BLK, 512), lambda b, i: (b * nb + i, EV_MID_BLK)),
                  pl.BlockSpec((1, 256), lambda b, i: (0, 0)), pl.BlockSpec((1, 128), lambda b, i: (0, 0)),
                  tab, tab, tab],
        out_specs=[rowspec(256), rowspec(128), rowspec(128)],
        out_shape=[jax.ShapeDtypeStruct((t, 256), BF16), jax.ShapeDtypeStruct((t, 128), BF16),
                   jax.ShapeDtypeStruct((t, 128), BF16)],
        compiler_params=_params(("parallel", "parallel")),
    )(proj, gq, gkv, *tabs)


def _mla_prep_bwd(proj, gq, gkv, tabs, dcqn, dckvn, dkrot, nbatch, nb):
    t = proj.shape[0]

    def body(mid_ref, gq_ref, gkv_ref, c_ref, s1_ref, s2_ref, dcq_ref, dckv_ref, dkr_ref,
             dmid_ref, dgq_ref, dgkv_ref):
        @pl.when((pl.program_id(0) == 0) & (pl.program_id(1) == 0))
        def _():
            dgq_ref[...] = jnp.zeros_like(dgq_ref)
            dgkv_ref[...] = jnp.zeros_like(dgkv_ref)

        def norm_bwd(x, gain, dy, dgain_ref):
            r = lax.rsqrt(jnp.mean(x * x, axis=1, keepdims=True) + NORM_EPS)
            nx = x * r
            dn = dy * gain
            dgain_ref[...] += jnp.sum(dy * nx, axis=0, keepdims=True)
            return r * (dn - nx * jnp.mean(dn * nx, axis=1, keepdims=True))

        dmid_ref[:, 0:256] = norm_bwd(
            mid_ref[:, 0:256].astype(F32), gq_ref[...], dcq_ref[...], dgq_ref).astype(BF16)
        dmid_ref[:, 256:384] = norm_bwd(
            mid_ref[:, 256:384].astype(F32), gkv_ref[...], dckv_ref[...], dgkv_ref).astype(BF16)
        dmid_ref[:, 384:512] = _rope_t(dkr_ref[...], c_ref[...], s1_ref[...], s2_ref[...]).astype(BF16)

    tab = pl.BlockSpec((BLK, BLK), lambda b, i: (i, 0))
    rowspec = lambda w: pl.BlockSpec((BLK, w), lambda b, i: (b * nb + i, 0))
    vq = pl.BlockSpec((1, 256), lambda b, i: (0, 0))
    vkv = pl.BlockSpec((1, 128), lambda b, i: (0, 0))
    return pl.pallas_call(
        body, name="mla_prep_bwd", grid=(nbatch, nb),
        in_specs=[pl.BlockSpec((BLK, 512), lambda b, i: (b * nb + i, EV_MID_BLK)), vq, vkv, tab, tab, tab,
                  rowspec(256), rowspec(128), rowspec(128)],
        out_specs=[rowspec(512), vq, vkv],
        out_shape=[jax.ShapeDtypeStruct((t, 512), BF16), jax.ShapeDtypeStruct((1, 256), F32),
                   jax.ShapeDtypeStruct((1, 128), F32)],
        compiler_params=_params(("arbitrary", "arbitrary")),
    )(proj, gq, gkv, *tabs, dcqn, dckvn, dkrot)


def _mla_scores(qf, kvb, krb, mask, lo_m):
    kf = jnp.where(lo_m, kvb, krb)
    s = _dot_nt(qf, kf) * MLA_SCALE
    return kf, jnp.where(mask, s, NEG)


def _mla_fwd(qh, kvh, krot, proj, tabs, nbatch, nb):
    lp = nb * BLK
    t = nbatch * lp

    def body(q_ref, kv_ref, kr_ref, g_ref, c_ref, s1_ref, s2_ref, o_ref, og_ref, lse_ref,
             qf_scr, m_scr, l_scr, acc_scr):
        i = pl.program_id(1)
        row, lane = _iotas()
        lo_m = lane < HALF
        for h in range(8):
            hc = slice(h * BLK, (h + 1) * BLK)
            qf_scr[:, hc] = _rope(q_ref[:, hc].astype(F32), c_ref[...], s1_ref[...], s2_ref[...]).astype(BF16)
        m_scr[...] = jnp.full(m_scr.shape, NEG, F32)
        l_scr[...] = jnp.zeros_like(l_scr)
        acc_scr[...] = jnp.zeros_like(acc_scr)

        def block(off, width):
            mask = _key_mask(i, off, width, strict=False)
            lo_k = lax.broadcasted_iota(jnp.int32, (width, BLK), 1) < HALF
            ones = jnp.ones((width, BLK), BF16)
            krb = kr_ref[pl.ds(off, width), :]
            heads = range(8)
            hcs = [slice(h * BLK, (h + 1) * BLK) for h in heads]
            kvbs = [kv_ref[pl.ds(off, width), hc] for hc in hcs]
            ss = [_mla_scores(qf_scr[:, hcs[h]], kvbs[h], krb, mask, lo_k)[1] for h in heads]
            ps, alphas = [], []
            for h in heads:
                m = m_scr[h]
                m2 = jnp.maximum(m, jnp.max(ss[h], axis=1, keepdims=True))
                ps.append(jnp.exp(ss[h] - _widen(m2, width)).astype(BF16))
                alphas.append(jnp.exp(m - m2))
                m_scr[h] = m2
            pvs = [_dot(ps[h], jnp.concatenate([kvbs[h], ones], axis=1)) for h in heads]
            for h in heads:
                l_scr[h] = alphas[h] * l_scr[h] + pvs[h][:, BLK:]
                acc_scr[h] = alphas[h] * acc_scr[h] + pvs[h][:, :BLK]

        _over_key_blocks(i + 1, block, reverse=False)
        lse = jnp.zeros((BLK, BLK), F32)
        for p in range(4):
            pc = slice(p * BLK, (p + 1) * BLK)
            o0 = acc_scr[2 * p] / l_scr[2 * p]
            o1 = acc_scr[2 * p + 1] / l_scr[2 * p + 1]
            o_ref[:, pc] = jnp.where(lo_m, pltpu.roll(o0, HALF, 1), o1)
            for h in (2 * p, 2 * p + 1):
                lse = lse + jnp.where(lane == h, m_scr[h] + jnp.log(l_scr[h]), 0.0)
        lse_ref[...] = lse
        g = g_ref[...].astype(F32)
        og_ref[...] = (o_ref[...] * g * _sigmoid(g)).astype(og_ref.dtype)

    tile = pl.BlockSpec((BLK, 512), lambda b, i: (b * nb + i, 0))
    tab = pl.BlockSpec((BLK, BLK), lambda b, i: (i, 0))
    heads = pltpu.VMEM((8, BLK, BLK), F32)
    return pl.pallas_call(
        body, name="mla_fwd", grid=(nbatch, nb),
        in_specs=[pl.BlockSpec((BLK, 1024), lambda b, i: (b * nb + i, 0)),
                  pl.BlockSpec((lp, 1024), lambda b, i: (b, 0)),
                  pl.BlockSpec((lp, BLK), lambda b, i: (b, 0)),
                  pl.BlockSpec((BLK, 512), lambda b, i: (b * nb + i, EV_GMLA_BLK // 4)),
                  tab, tab, tab],
        out_specs=[tile, tile, pl.BlockSpec((BLK, BLK), lambda b, i: (b * nb + i, 0))],
        out_shape=[jax.ShapeDtypeStruct((t, 512), F32), jax.ShapeDtypeStruct((t, 512), BF16),
                   jax.ShapeDtypeStruct((t, BLK), F32)],
        scratch_shapes=[pltpu.VMEM((BLK, 1024), BF16), heads, heads, heads],
        compiler_params=_params(("parallel", "arbitrary"), 48),
    )(qh, kvh, krot, proj, *tabs)


def _mla_bwd(qh, kvh, krot, proj, tabs, o, dog, lse, nbatch, nb):
    lp = nb * BLK
    t = nbatch * lp

    def body(q_ref, kv_ref, kr_ref, g_ref, c_ref, s1_ref, s2_ref, o_ref, dog_ref, lse_ref,
             dq_ref, dkv_ref, dkr_ref, dg_ref, qf_scr, do_scr, stat_scr, acc_scr):
        i = pl.program_id(1)

        @pl.when(i == 0)
        def _():
            dkv_ref[...] = jnp.zeros_like(dkv_ref)
            dkr_ref[...] = jnp.zeros_like(dkr_ref)

        row, lane = _iotas()
        lo_m = lane < HALF
        g = g_ref[...].astype(F32)
        sig = _sigmoid(g)
        dog_v = dog_ref[...]
        o_v = o_ref[...]
        dg_ref[...] = (dog_v * o_v * (sig * (1.0 + g * (1.0 - sig)))).astype(dg_ref.dtype)
        do = dog_v * g * sig
        do_o = do * o_v
        lse_blk = lse_ref[...]
        zero = jnp.zeros((BLK, BLK), F32)
        for h in range(8):
            hc = slice(h * BLK, (h + 1) * BLK)
            pc = slice((h // 2) * BLK, (h // 2 + 1) * BLK)
            qf_scr[:, hc] = _rope(q_ref[:, hc].astype(F32), c_ref[...], s1_ref[...], s2_ref[...]).astype(BF16)
            dop = do[:, pc]
            do_src = dop if h % 2 else pltpu.roll(dop, HALF, 1)
            do_scr[:, hc] = jnp.where(lo_m, 0.0, do_src).astype(BF16)
            hm = (lane >= HALF) if h % 2 else lo_m
            stat_scr[h] = zero + jnp.sum(jnp.where(hm, do_o[:, pc], 0.0), axis=1, keepdims=True)
            stat_scr[8 + h] = zero + jnp.sum(jnp.where(lane == h, lse_blk, 0.0), axis=1, keepdims=True)
        acc_scr[...] = jnp.zeros_like(acc_scr)

        def block(off, width):
            mask = _key_mask(i, off, width, strict=False)
            lo_k = lax.broadcasted_iota(jnp.int32, (width, BLK), 1) < HALF
            krb = kr_ref[pl.ds(off, width), :]
            heads = range(8)
            hcs = [slice(h * BLK, (h + 1) * BLK) for h in heads]
            kvbs = [kv_ref[pl.ds(off, width), hc] for hc in hcs]
            qfs = [qf_scr[:, hc] for hc in hcs]
            dos = [do_scr[:, hc] for hc in hcs]
            scored = [_mla_scores(qfs[h], kvbs[h], krb, mask, lo_k) for h in heads]
            dps = [_dot_nt(dos[h], kvbs[h]) for h in heads]
            pbs, dss = [], []
            for h in heads:
                p = jnp.exp(scored[h][1] - _widen(stat_scr[8 + h], width))
                pbs.append(p.astype(BF16))
                dss.append((p * (dps[h] - _widen(stat_scr[h], width)) * MLA_SCALE).astype(BF16))
            dqs = [_dot(dss[h], scored[h][0]) for h in heads]
            dkfs = [_dot_tn(dss[h], qfs[h]) for h in heads]
            dvvs = [_dot_tn(pbs[h], dos[h]) for h in heads]
            dkr = jnp.zeros((width, BLK), F32)
            for h in heads:
                acc_scr[h] += dqs[h]
                dkv_ref[pl.ds(off, width), hcs[h]] += jnp.where(lo_k, dkfs[h], 0.0) + dvvs[h]
                dkr = dkr + jnp.where(lo_k, 0.0, dkfs[h])
            dkr_ref[pl.ds(off, width), :] += dkr

        _over_key_blocks(i + 1, block, reverse=False)
        for h in range(8):
            hc = slice(h * BLK, (h + 1) * BLK)
            dq_ref[:, hc] = _rope_t(acc_scr[h], c_ref[...], s1_ref[...], s2_ref[...]).astype(dq_ref.dtype)

    tile = lambda col: pl.BlockSpec((BLK, 512), lambda b, i: (b * nb + i, col))
    wide = pl.BlockSpec((BLK, 1024), lambda b, i: (b * nb + i, 0))
    full8 = pl.BlockSpec((lp, 1024), lambda b, i: (b, 0))
    full1 = pl.BlockSpec((lp, BLK), lambda b, i: (b, 0))
    tab = pl.BlockSpec((BLK, BLK), lambda b, i: (i, 0))
    return pl.pallas_call(
        body, name="mla_bwd", grid=(nbatch, nb),
        in_specs=[wide, full8, full1, tile(EV_GMLA_BLK // 4), tab, tab, tab, tile(0), tile(1),
                  pl.BlockSpec((BLK, BLK), lambda b, i: (b * nb + i, 0))],
        out_specs=[wide, full8, full1, tile(0)],
        out_shape=[jax.ShapeDtypeStruct((t, 1024), BF16), jax.ShapeDtypeStruct((t, 1024), F32),
                   jax.ShapeDtypeStruct((t, 128), F32), jax.ShapeDtypeStruct((t, 512), BF16)],
        scratch_shapes=[pltpu.VMEM((BLK, 1024), BF16), pltpu.VMEM((BLK, 1024), BF16),
                        pltpu.VMEM((16, BLK, BLK), F32), pltpu.VMEM((8, BLK, BLK), F32)],
        compiler_params=_params(("parallel", "arbitrary"), 56),
    )(qh, kvh, krot, proj, *tabs, o, dog, lse)


def _swa_setup(kk, i, k_refs, v_refs):
    row, lane = _iotas()
    own = (lane >= kk * HALF) & (lane < (kk + 1) * HALF)

    def dup(ref):
        x = ref[...].astype(F32)
        return jnp.where(own, x, pltpu.roll(x, HALF, 1)).astype(BF16)

    kcat = jnp.concatenate([dup(r) for r in k_refs], axis=0)
    vcat = jnp.concatenate([dup(r) for r in v_refs], axis=0)
    row3 = lax.broadcasted_iota(jnp.int32, (BLK, 3 * BLK), 0)
    lane3 = lax.broadcasted_iota(jnp.int32, (BLK, 3 * BLK), 1)
    is_meta = lane3 < BLK
    k_pos = jnp.where(is_meta, lane3, (i - 2) * BLK + lane3)
    d = i * BLK + row3 - k_pos
    mask = (d >= 0) & (d < jnp.where(is_meta, 1 << 20, BLK)) & (k_pos >= jnp.where(is_meta, N_PAD, BLK))
    return lane, own, kcat, vcat, mask, d.astype(F32)


def _swa_slope(kk, g_idx):
    return (2.0 ** (-(g_idx + 1) / 2.0)) * jnp.where(kk == 0, 1.0, 1.0 / 16.0)


def _swa_fwd(proj, sinks, nbatch, nb):
    lp = nb * BLK
    t = nbatch * lp

    def body(sink_ref, q_ref, ka, kb, kc, va, vb, vc, g_ref, o_ref, og_ref, lse_ref):
        kk = pl.program_id(1)
        i = pl.program_id(2)
        lane, own, kcat, vcat, mask, dist = _swa_setup(kk, i, (ka, kb, kc), (va, vb, vc))
        lo_m = lane < HALF
        heads = range(8)
        qms = []
        for h in heads:
            qp = q_ref[:, (h // 2) * BLK:(h // 2 + 1) * BLK]
            qms.append(jnp.where((lane >= HALF) if h % 2 else lo_m, qp, jnp.zeros_like(qp)))
        qks = [_dot_nt(qms[h], kcat) for h in heads]
        ps, ls, lses = [], [], []
        for h in heads:
            sink = sink_ref[kk, h]
            s = jnp.where(mask, qks[h] * SWA_SCALE - _swa_slope(kk, h) * dist, NEG)
            mx = jnp.maximum(jnp.max(s, axis=1, keepdims=True), sink)
            p = jnp.exp(s - mx)
            l = jnp.exp(sink - mx) + jnp.sum(p, axis=1, keepdims=True)
            ps.append(p.astype(BF16))
            ls.append(l)
            lses.append(mx + jnp.log(l))
        pvs = [_dot(ps[h], vcat) for h in heads]
        lse_out = jnp.zeros((BLK, BLK), F32)
        for m in range(4):
            cols = slice(m * BLK, (m + 1) * BLK)
            outp = jnp.where(lo_m, pvs[2 * m] / ls[2 * m], pvs[2 * m + 1] / ls[2 * m + 1])
            o_ref[:, cols] = outp
            g = g_ref[:, cols].astype(F32)
            og_ref[:, cols] = (outp * g * _sigmoid(g)).astype(og_ref.dtype)
            for h in (2 * m, 2 * m + 1):
                lse_out = lse_out + jnp.where(lane == h, lses[h], 0.0)
        lse_ref[...] = lse_out

    def kvspec(col, which):
        if which == 0:
            return pl.BlockSpec((BLK, BLK), lambda b, kk, i: (b * nb, col))
        if which == 1:
            return pl.BlockSpec((BLK, BLK), lambda b, kk, i: (b * nb + jnp.maximum(i - 1, 0), col))
        return pl.BlockSpec((BLK, BLK), lambda b, kk, i: (b * nb + i, col))

    wide = lambda c0: pl.BlockSpec((BLK, 512), lambda b, kk, i: (b * nb + i, c0 + kk))
    return pl.pallas_call(
        body, name="swa_fwd", grid=(nbatch, 2, nb),
        in_specs=[pl.BlockSpec(memory_space=pltpu.SMEM), wide(0),
                  kvspec(OD_K_BLK, 0), kvspec(OD_K_BLK, 1), kvspec(OD_K_BLK, 2),
                  kvspec(OD_V_BLK, 0), kvspec(OD_V_BLK, 1), kvspec(OD_V_BLK, 2), wide(2)],
        out_specs=[wide(0), wide(0), pl.BlockSpec((BLK, BLK), lambda b, kk, i: (b * nb + i, kk))],
        out_shape=[jax.ShapeDtypeStruct((t, 1024), F32), jax.ShapeDtypeStruct((t, 1024), BF16),
                   jax.ShapeDtypeStruct((t, 256), F32)],
        compiler_params=_params(("parallel", "parallel", "arbitrary")),
    )(sinks, proj, proj, proj, proj, proj, proj, proj, proj)


def _swa_bwd(proj, sinks, o, dog, lse, nbatch, nb):
    lp = nb * BLK
    t = nbatch * lp

    def body(sink_ref, q_ref, ka, kb, kc, va, vb, vc, g_ref, o_ref, dog_ref, lse_ref,
             dq_ref, dg_ref, dk_ref, dv_ref, dsink_ref):
        kk = pl.program_id(1)
        i = pl.program_id(2)

        @pl.when((kk == 0) & (i == 0))
        def _():
            dk_ref[...] = jnp.zeros_like(dk_ref)
            dv_ref[...] = jnp.zeros_like(dv_ref)

        @pl.when(i == 0)
        def _():
            dsink_ref[...] = jnp.zeros_like(dsink_ref)

        lane, own, kcat, vcat, mask, dist = _swa_setup(kk, i, (ka, kb, kc), (va, vb, vc))
        lo_m = lane < HALF
        row8 = lax.broadcasted_iota(jnp.int32, (8, BLK), 0)
        lse_blk = lse_ref[...]
        heads = range(8)
        qms, doms, deltas, lse_hs = [], [], [], []
        for m in range(4):
            cols = slice(m * BLK, (m + 1) * BLK)
            qp = q_ref[:, cols]
            g = g_ref[:, cols].astype(F32)
            sig = _sigmoid(g)
            dog_v = dog_ref[:, cols]
            o_v = o_ref[:, cols]
            dg_ref[:, cols] = (dog_v * o_v * (sig * (1.0 + g * (1.0 - sig)))).astype(dg_ref.dtype)
            do = dog_v * g * sig
            do_o = do * o_v
            dob = do.astype(BF16)
            for h in (2 * m, 2 * m + 1):
                hm = (lane >= HALF) if h % 2 else lo_m
                qms.append(jnp.where(hm, qp, jnp.zeros_like(qp)))
                doms.append(jnp.where(hm, dob, jnp.zeros_like(dob)))
                deltas.append(jnp.sum(jnp.where(hm, do_o, 0.0), axis=1, keepdims=True))
                lse_hs.append(jnp.sum(jnp.where(lane == h, lse_blk, 0.0), axis=1, keepdims=True))
        qks = [_dot_nt(qms[h], kcat) for h in heads]
        dps = [_dot_nt(doms[h], vcat) for h in heads]
        pbs, dss = [], []
        dsink = jnp.zeros((8, BLK), F32)
        for h in heads:
            s = jnp.where(mask, qks[h] * SWA_SCALE - _swa_slope(kk, h) * dist, NEG)
            p = jnp.exp(s - lse_hs[h])
            pbs.append(p.astype(BF16))
            dss.append((p * (dps[h] - deltas[h]) * SWA_SCALE).astype(BF16))
            tot = jnp.sum(-jnp.exp(sink_ref[kk, h] - lse_hs[h]) * deltas[h], axis=0, keepdims=True)
            dsink = dsink + jnp.where(row8 == h, tot, 0.0)
        dsink_ref[...] += dsink
        dqs = [_dot(dss[h], kcat) for h in heads]
        dks = [_dot_tn(dss[h], qms[h]) for h in heads]
        dvs = [_dot_tn(pbs[h], doms[h]) for h in heads]
        for m in range(4):
            dq_ref[:, m * BLK:(m + 1) * BLK] = jnp.where(lo_m, dqs[2 * m], dqs[2 * m + 1]).astype(dq_ref.dtype)
        dk = dks[0]
        dv = dvs[0]
        for h in range(1, 8):
            dk = dk + dks[h]
            dv = dv + dvs[h]
        offs = [0, pl.multiple_of(jnp.maximum(i - 1, 0) * BLK, BLK), pl.multiple_of(i * BLK, BLK)]
        for x in range(3):
            rows = slice(x * BLK, (x + 1) * BLK)
            dkx, dvx = dk[rows], dv[rows]
            dk_ref[pl.ds(offs[x], BLK), :] += jnp.where(own, dkx + pltpu.roll(dkx, HALF, 1), 0.0)
            dv_ref[pl.ds(offs[x], BLK), :] += jnp.where(own, dvx + pltpu.roll(dvx, HALF, 1), 0.0)

    def kvspec(col, which):
        if which == 0:
            return pl.BlockSpec((BLK, BLK), lambda b, kk, i: (b * nb, col))
        if which == 1:
            return pl.BlockSpec((BLK, BLK), lambda b, kk, i: (b * nb + jnp.maximum(i - 1, 0), col))
        return pl.BlockSpec((BLK, BLK), lambda b, kk, i: (b * nb + i, col))

    wide = lambda c0: pl.BlockSpec((BLK, 512), lambda b, kk, i: (b * nb + i, c0 + kk))
    full = pl.BlockSpec((lp, BLK), lambda b, kk, i: (b, 0))
    return pl.pallas_call(
        body, name="swa_bwd", grid=(nbatch, 2, nb),
        in_specs=[pl.BlockSpec(memory_space=pltpu.SMEM), wide(0),
                  kvspec(OD_K_BLK, 0), kvspec(OD_K_BLK, 1), kvspec(OD_K_BLK, 2),
                  kvspec(OD_V_BLK, 0), kvspec(OD_V_BLK, 1), kvspec(OD_V_BLK, 2), wide(2),
                  wide(0), wide(0), pl.BlockSpec((BLK, BLK), lambda b, kk, i: (b * nb + i, kk))],
        out_specs=[wide(0), wide(0), full, full,
                   pl.BlockSpec((8, BLK), lambda b, kk, i: (b * 2 + kk, 0))],
        out_shape=[jax.ShapeDtypeStruct((t, 1024), BF16), jax.ShapeDtypeStruct((t, 1024), BF16),
                   jax.ShapeDtypeStruct((t, 128), F32), jax.ShapeDtypeStruct((t, 128), F32),
                   jax.ShapeDtypeStruct((nbatch * 16, BLK), F32)],
        compiler_params=_params(("parallel", "arbitrary", "arbitrary")),
    )(sinks, proj, proj, proj, proj, proj, proj, proj, proj, o, dog, lse)


def _rope_tables(lp):
    pos = (jnp.arange(lp) - N_PAD).astype(F32)
    inv = ROPE_BASE ** (-jnp.arange(16, dtype=F32) / 16.0)
    ang = pos[:, None] * inv[None, :]
    cos, sin = jnp.cos(ang), jnp.sin(ang)
    z16 = jnp.zeros((lp, 16), F32)
    c = jnp.concatenate([jnp.ones((lp, 64), F32), cos, cos, jnp.zeros((lp, 32), F32)], axis=1)
    s1 = jnp.concatenate([jnp.zeros((lp, 64), F32), -sin, z16, jnp.zeros((lp, 32), F32)], axis=1)
    s2 = jnp.concatenate([jnp.zeros((lp, 64), F32), z16, sin, jnp.zeros((lp, 32), F32)], axis=1)
    return c, s1, s2


def _local_step(h0, tgt, norm_g, final_g, gq, gkv, sinks, w_ie, w_uq, w_ukv, w_oe, w_io, w_oo, nbatch, nb):
    lp = nb * BLK
    tabs = _rope_tables(lp)
    g0, g1 = norm_g[0:1], norm_g[1:2]
    sinks2 = sinks.reshape(2, 8)

    hn0 = _rms_fwd(h0, g0, "rms_fwd0")
    proj_e = _mm(hn0, w_ie, "nn", "proj_even", out_dtype=BF16)
    o_sb, og_sb, cm = _sb_fwd(proj_e, nbatch, nb)
    cqn, ckvn, krot = _mla_prep(proj_e, gq, gkv, tabs, nbatch, nb)
    qh = _mm(cqn, w_uq, "nn", "mla_uq", out_dtype=BF16)
    kvh = _mm(ckvn, w_ukv, "nn", "mla_ukv", out_dtype=BF16)
    o_mla, og_mla, lse_m = _mla_fwd(qh, kvh, krot, proj_e, tabs, nbatch, nb)
    h1 = _mm([og_sb, og_mla], w_oe, "nn", "out_even", add=h0)
    hn1 = _rms_fwd(h1, g1, "rms_fwd1")
    proj_o = _mm(hn1, w_io, "nn", "proj_odd", out_dtype=BF16)
    o_o, og_o, lse_o = _swa_fwd(proj_o, sinks2, nbatch, nb)
    h2 = _mm(og_o, w_oo, "nn", "out_odd", add=h1)
    dh2, lossv, d_final_g = _final(h2, tgt, final_g, nbatch, nb)

    dog_o = _mm(dh2, w_oo, "nt", "d_out_odd")
    d_w_oo, = _mm_tn(og_o, [dh2], "dw_out_odd")
    dq_o, dg_o, dk_o, dv_o, dsink = _swa_bwd(proj_o, sinks2, o_o, dog_o, lse_o, nbatch, nb)
    dproj_o = [dq_o, dg_o, dk_o, dv_o]
    dhn1 = _mm(dproj_o, w_io, "nt", "d_proj_odd")
    dw_q, dw_g, dw_k, dw_v = _mm_tn(hn1, dproj_o, "dw_proj_odd")
    dh1, d_g1 = _rms_bwd(h1, g1, dhn1, dh2, "rms_bwd1")

    dog_e = _mm(dh1, w_oe, "nt", "d_out_even")
    d_w_oe_sb, = _mm_tn(og_sb, [dh1], "dw_out_even_sb")
    d_w_oe_mla, = _mm_tn(og_mla, [dh1], "dw_out_even_mla")
    dq_sb, dk_sb, dv_sb, dg_sb = _sb_bwd(proj_e, o_sb, dog_e, cm, nbatch, nb)
    dqh, dkvh, dkrot, dg_mla = _mla_bwd(qh, kvh, krot, proj_e, tabs, o_mla, dog_e, lse_m, nbatch, nb)
    dcqn = _mm(dqh, w_uq, "nt", "d_mla_uq")
    d_w_uq, = _mm_tn(cqn, [dqh], "dw_mla_uq")
    dckvn = _mm(dkvh, w_ukv, "nt", "d_mla_ukv")
    d_w_ukv, = _mm_tn(ckvn, [dkvh], "dw_mla_ukv")
    dmid, d_gq, d_gkv = _mla_prep_bwd(proj_e, gq, gkv, tabs, dcqn, dckvn, dkrot, nbatch, nb)
    dproj_e = [dq_sb, dk_sb, dv_sb, dg_sb, dmid, dg_mla]
    dhn0 = _mm(dproj_e, w_ie, "nt", "d_proj_even")
    dw_e = _mm_tn(hn0, dproj_e, "dw_proj_even")
    dh0, d_g0 = _rms_bwd(h0, g0, dhn0, dh1, "rms_bwd0")

    d_sinks = dsink.reshape(nbatch, 2, 8, BLK)[:, :, :, 0].sum(axis=0).reshape(1, 16)
    d_norm_g = jnp.concatenate([d_g0, d_g1], axis=0)
    d_ev_w_in = jnp.concatenate(list(dw_e[:4]) + [dw_e[4][:, :384], dw_e[4][:, 448:480], dw_e[5]], axis=1)
    d_od_w_in = jnp.concatenate([dw_q, dw_k, dw_v, dw_g], axis=1)
    d_ev_w_out = jnp.concatenate([d_w_oe_sb, d_w_oe_mla], axis=0)
    return dict(lossv=lossv, dh0=dh0, norm_g=d_norm_g, final_g=d_final_g, gq=d_gq, gkv=d_gkv, sinks=d_sinks,
                ev_w_in=d_ev_w_in, ev_w_uq=_uq_from_compute(d_w_uq), ev_w_ukv=d_w_ukv, ev_w_out=d_ev_w_out,
                od_w_in=d_od_w_in, od_w_out=d_w_oo)


def _ev_in_to_compute(w):
    z = lambda n: jnp.zeros((w.shape[0], n), w.dtype)
    return jnp.concatenate([w[:, :2432], z(64), w[:, 2432:2464], z(32), w[:, 2464:]], axis=1)


def _uq_to_compute(w):
    w3 = w.reshape(256, 8, 96)
    return jnp.concatenate([w3, jnp.zeros((256, 8, 32), w.dtype)], axis=2).reshape(256, 1024)


def _uq_from_compute(w):
    return w.reshape(256, 8, 128)[:, :, :96].reshape(256, 768)


def _od_in_to_compute(w):
    return jnp.concatenate([w[:, :1024], w[:, 1280:], w[:, 1024:1280]], axis=1)


_BIG = (("ev_w_in", 1024, 2976, 1), ("ev_w_uq", 256, 768, 1), ("ev_w_ukv", 128, 1024, 1),
        ("ev_w_out", 1024, 1024, 0), ("od_w_in", 1024, 2304, 1), ("od_w_out", 1024, 1024, 0))


def _all_gather(shards, name):
    n = len(shards)

    def body(*refs):
        xs, outs = refs[:n], refs[n:2 * n]
        send_sems, recv_sems, local_sems = refs[2 * n:]
        x, y, c = lax.axis_index("x"), lax.axis_index("y"), lax.axis_index("c")
        me, sibling = (x, y, c), (x, y, 1 - c)
        chips = [(1 - x, y), (x, 1 - y), (1 - x, 1 - y)]
        arrays = range(n)

        def copy(k, a, block, to, from_input=False):
            px, py, pc = block
            dst = outs[a].at[4 * px + 2 * py + pc]
            return pltpu.make_async_remote_copy(
                src_ref=xs[a] if from_input else dst, dst_ref=dst,
                send_sem=send_sems.at[k, a], recv_sem=recv_sems.at[k, a],
                device_id=to, device_id_type=pl.DeviceIdType.MESH)

        mine = [pltpu.make_async_copy(xs[a], outs[a].at[4 * x + 2 * y + c], local_sems.at[a]) for a in arrays]
        for cp in mine:
            cp.start()
        first = [copy(0, a, me, sibling, True) for a in arrays]
        for j, chip in enumerate(chips):
            first += [copy(1 + j, a, me, (*chip, c), True) for a in arrays]
        for cp in first:
            cp.start()
        passed = []
        for j, chip in enumerate(chips):
            for a in arrays:
                copy(1 + j, a, (*chip, c), me).wait_recv()
                passed.append(copy(4 + j, a, (*chip, c), sibling))
                passed[-1].start()
        for a in arrays:
            copy(0, a, sibling, me).wait_recv()
        for j, chip in enumerate(chips):
            for a in arrays:
                copy(4 + j, a, (*chip, 1 - c), me).wait_recv()
        for cp in first + passed:
            cp.wait_send()
        for cp in mine:
            cp.wait()

    hbm = pl.BlockSpec(memory_space=pl.ANY)
    return pl.pallas_call(
        body, name=name,
        out_shape=[jax.ShapeDtypeStruct((N_DEV,) + s.shape, s.dtype) for s in shards],
        in_specs=[hbm] * n, out_specs=[hbm] * n,
        scratch_shapes=[pltpu.SemaphoreType.DMA((7, n)), pltpu.SemaphoreType.DMA((7, n)),
                        pltpu.SemaphoreType.DMA((n,))],
    )(*shards)


def _exchange_sum(chunked, small, name):
    n = len(chunked)
    arrs = list(chunked) + [small]

    def body(*refs):
        ins, outs = refs[:n + 1], refs[n + 1:2 * n + 2]
        bufs = refs[2 * n + 2:3 * n + 3]
        send_sems, recv_sems, local_sems = refs[3 * n + 3:]
        x, y, c = lax.axis_index("x"), lax.axis_index("y"), lax.axis_index("c")
        me = 4 * x + 2 * y + c
        own = [pltpu.make_async_copy(ins[a].at[me], bufs[a].at[me], local_sems.at[a]) for a in range(n + 1)]
        for cp in own:
            cp.start()
        sends, recvs = [], []
        for d in range(1, N_DEV):
            px = x + ((d >> 2) & 1) - 2 * x * ((d >> 2) & 1)
            py = y + ((d >> 1) & 1) - 2 * y * ((d >> 1) & 1)
            pc = c + (d & 1) - 2 * c * (d & 1)
            pid = 4 * px + 2 * py + pc
            for a in range(n + 1):
                kw = dict(send_sem=send_sems.at[d - 1, a], recv_sem=recv_sems.at[d - 1, a],
                          device_id=(px, py, pc), device_id_type=pl.DeviceIdType.MESH)
                sends.append(pltpu.make_async_remote_copy(src_ref=ins[a].at[pid], dst_ref=bufs[a].at[me], **kw))
                recvs.append(pltpu.make_async_remote_copy(src_ref=ins[a].at[pid], dst_ref=bufs[a].at[pid], **kw))
        for cp in sends:
            cp.start()
        for cp in own:
            cp.wait()
        for cp in recvs:
            cp.wait_recv()
        for cp in sends:
            cp.wait_send()
        for a in range(n + 1):
            _sum_slots(bufs[a], outs[a])

    hbm = pl.BlockSpec(memory_space=pl.ANY)
    vm = pl.BlockSpec(memory_space=pltpu.VMEM)
    return pl.pallas_call(
        body, name=name,
        out_shape=[jax.ShapeDtypeStruct(a.shape[1:], F32) for a in arrs],
        in_specs=[hbm] * (n + 1), out_specs=[vm] * (n + 1),
        scratch_shapes=[pltpu.VMEM(a.shape, a.dtype) for a in arrs]
        + [pltpu.SemaphoreType.DMA((7, n + 1)), pltpu.SemaphoreType.DMA((7, n + 1)),
           pltpu.SemaphoreType.DMA((n + 1,))],
        compiler_params=pltpu.CompilerParams(vmem_limit_bytes=48 << 20),
    )(*arrs)


def _sum_slots(buf, out):
    rows = buf.shape[1]

    def add(sl):
        acc = buf[(0,) + sl].astype(F32)
        for k in range(1, N_DEV):
            acc = acc + buf[(k,) + sl].astype(F32)
        out[sl] = acc

    if rows > BLK and rows % BLK == 0:
        def step(r, carry):
            add((pl.ds(pl.multiple_of(r * BLK, BLK), BLK), slice(None)))
            return carry

        lax.fori_loop(0, rows // BLK, step, 0)
    else:
        add((slice(None), slice(None)))


def _adamw(ws, gs, ms, vs):
    n = len(ws)

    def body(*refs):
        ins, outs = refs[:4 * n], refs[4 * n:]
        for k in range(n):
            w_ref, g_ref, m_ref, v_ref = ins[4 * k:4 * k + 4]
            d_ref, nm_ref, nv_ref = outs[3 * k:3 * k + 3]

            def update(sl, w_ref=w_ref, g_ref=g_ref, m_ref=m_ref, v_ref=v_ref,
                       d_ref=d_ref, nm_ref=nm_ref, nv_ref=nv_ref):
                g = g_ref[sl]
                m = ADAM_B1 * m_ref[sl] + (1.0 - ADAM_B1) * g
                v = ADAM_B2 * v_ref[sl] + (1.0 - ADAM_B2) * (g * g)
                m_hat = m / (1.0 - ADAM_B1 ** ADAM_STEP)
                v_hat = v / (1.0 - ADAM_B2 ** ADAM_STEP)
                d_ref[sl] = -ADAM_LR * (m_hat / (jnp.sqrt(v_hat) + ADAM_EPS) + ADAM_WD * w_ref[sl])
                nm_ref[sl] = m
                nv_ref[sl] = v

            rows = w_ref.shape[0]
            if rows > BLK and rows % BLK == 0:
                def step(r, carry, update=update):
                    update((pl.ds(pl.multiple_of(r * BLK, BLK), BLK), slice(None)))
                    return carry

                lax.fori_loop(0, rows // BLK, step, 0)
            else:
                update((slice(None), slice(None)))

    args, out_shape = [], []
    for k in range(n):
        args += [ws[k], gs[k], ms[k], vs[k]]
        out_shape += [jax.ShapeDtypeStruct(ws[k].shape, F32)] * 3
    vm = pl.BlockSpec(memory_space=pltpu.VMEM)
    outs = pl.pallas_call(
        body, name="adamw", out_shape=out_shape,
        in_specs=[vm] * (4 * n), out_specs=[vm] * (3 * n),
        compiler_params=pltpu.CompilerParams(vmem_limit_bytes=48 << 20),
    )(*args)
    return [tuple(outs[3 * k:3 * k + 3]) for k in range(n)]


def kernel(x, meta, norm_g, final_g, ev_w_in, ev_q_norm_g, ev_kv_norm_g, ev_w_uq, ev_w_ukv, ev_w_out, od_w_in, od_sinks, od_w_out, loss_target, m_meta, m_norm_g, m_final_g, m_ev_w_in, m_ev_q_norm_g, m_ev_kv_norm_g, m_ev_w_uq, m_ev_w_ukv, m_ev_w_out, m_od_w_in, m_od_sinks, m_od_w_out, v_meta, v_norm_g, v_final_g, v_ev_w_in, v_ev_q_norm_g, v_ev_kv_norm_g, v_ev_w_uq, v_ev_w_ukv, v_ev_w_out, v_od_w_in, v_od_sinks, v_od_w_out):
    nbatch, seq, d = x.shape
    nb = seq // BLK + 1
    lp = nb * BLK
    shards = dict(ev_w_in=ev_w_in[0], ev_w_uq=ev_w_uq[0], ev_w_ukv=ev_w_ukv[0], ev_w_out=ev_w_out[0],
                  od_w_in=od_w_in[0], od_w_out=od_w_out[0])

    gathered = _all_gather([shards[name].astype(BF16) for name, _, _, _ in _BIG] + [meta], "gather_weights")
    full = {}
    for (name, rows, cols, axis), blk in zip(_BIG, gathered):
        full[name] = blk.transpose(1, 0, 2).reshape(rows, cols) if axis == 1 else blk.reshape(rows, cols)
    meta_full = gathered[-1].transpose(1, 0, 2).reshape(N_META, d)

    head = jnp.concatenate([jnp.zeros((N_PAD, d), F32), meta_full], axis=0)
    h0 = jnp.concatenate([jnp.broadcast_to(head[None], (nbatch, BLK, d)), x], axis=1).reshape(nbatch * lp, d)
    grads = _local_step(
        h0, loss_target.reshape(nbatch * seq, d), norm_g, final_g.reshape(1, d), ev_q_norm_g, ev_kv_norm_g,
        od_sinks, _ev_in_to_compute(full["ev_w_in"]), _uq_to_compute(full["ev_w_uq"]), full["ev_w_ukv"],
        full["ev_w_out"], _od_in_to_compute(full["od_w_in"]), full["od_w_out"], nbatch, nb)
    dh0 = grads["dh0"].reshape(nbatch, lp, d)
    grad_x = dh0[:, BLK:]
    loss = lax.psum(0.5 / d * jnp.sum(grads["lossv"]), ("x", "y", "c"))

    chunks = []
    for name, rows, cols, axis in _BIG:
        g = grads[name].astype(BF16)
        chunks.append(g.reshape(rows, N_DEV, cols // N_DEV).transpose(1, 0, 2) if axis == 1
                      else g.reshape(N_DEV, rows // N_DEV, cols))
    d_meta = dh0[:, N_PAD:BLK].sum(axis=0).reshape(N_META, N_DEV, BLK).transpose(1, 0, 2)
    pad = lambda a, n: jnp.concatenate([a.reshape(1, -1), jnp.zeros((1, n - a.size), F32)], axis=1)
    rep = jnp.concatenate([grads["norm_g"].reshape(1, -1), grads["final_g"], pad(grads["gq"], 512),
                           pad(grads["gkv"], 256), pad(grads["sinks"], 256)], axis=1).reshape(32, BLK)
    small = jnp.concatenate([d_meta, jnp.broadcast_to(rep[None], (N_DEV, 32, BLK))], axis=1)
    reduced = _exchange_sum(chunks, small, "reduce_grads")
    g_shard = {name: red for (name, _, _, _), red in zip(_BIG, reduced)}
    red_small = reduced[-1]
    rep = red_small[N_META:].reshape(1, -1)
    g_small = dict(meta=red_small[:N_META], norm_g=rep[:, :2 * d].reshape(2, d), final_g=rep[:, 2 * d:3 * d],
                   ev_q_norm_g=rep[:, 3 * d:3 * d + 256], ev_kv_norm_g=rep[:, 3 * d + 512:3 * d + 640],
                   od_sinks=rep[:, 3 * d + 768:3 * d + 784])

    names = ["meta", "norm_g", "final_g", "ev_w_in", "ev_q_norm_g", "ev_kv_norm_g", "ev_w_uq", "ev_w_ukv",
             "ev_w_out", "od_w_in", "od_sinks", "od_w_out"]
    given = dict(meta=(meta, m_meta, v_meta), norm_g=(norm_g, m_norm_g, v_norm_g),
                 final_g=(final_g, m_final_g, v_final_g), ev_w_in=(ev_w_in, m_ev_w_in, v_ev_w_in),
                 ev_q_norm_g=(ev_q_norm_g, m_ev_q_norm_g, v_ev_q_norm_g),
                 ev_kv_norm_g=(ev_kv_norm_g, m_ev_kv_norm_g, v_ev_kv_norm_g),
                 ev_w_uq=(ev_w_uq, m_ev_w_uq, v_ev_w_uq), ev_w_ukv=(ev_w_ukv, m_ev_w_ukv, v_ev_w_ukv),
                 ev_w_out=(ev_w_out, m_ev_w_out, v_ev_w_out), od_w_in=(od_w_in, m_od_w_in, v_od_w_in),
                 od_sinks=(od_sinks, m_od_sinks, v_od_sinks), od_w_out=(od_w_out, m_od_w_out, v_od_w_out))
    ws, gs, ms, vs = [], [], [], []
    for name in names:
        g2 = g_shard[name] if name in g_shard else g_small[name]
        w, m, v = given[name]
        ws.append(w.reshape(g2.shape))
        ms.append(m.reshape(g2.shape))
        vs.append(v.reshape(g2.shape))
        gs.append(g2)
    upd = _adamw(ws, gs, ms, vs)
    shape_of = {name: given[name][0].shape for name in names}
    grads_out = [gs[k].reshape(shape_of[n]) for k, n in enumerate(names)]
    deltas = [upd[k][0].reshape(shape_of[n]) for k, n in enumerate(names)]
    new_m = [upd[k][1].reshape(shape_of[n]) for k, n in enumerate(names)]
    new_v = [upd[k][2].reshape(shape_of[n]) for k, n in enumerate(names)]
    return (loss, grad_x, *grads_out, *deltas, *new_m, *new_v)
```

```python
import jax
import jax.numpy as jnp
from jax import lax
from jax.experimental import pallas as pl
from jax.experimental.pallas import tpu as pltpu

F32 = jnp.float32
BF16 = jnp.bfloat16

D_MODEL = 1024
N_META = 16
BLK = 128
HALF = 64
N_PAD = BLK - N_META
NORM_EPS = 1e-6
NEG = -1e30
N_DEV = 8

SB_SCALE = 64 ** -0.5
MLA_SCALE = 96 ** -0.5
SWA_SCALE = 64 ** -0.5
ROPE_BASE = 10000.0

EV_IN_PAD = 3072
EV_MID_BLK = 4
EV_GMLA_BLK = 20
OD_K_BLK = 16
OD_V_BLK = 17

ADAM_LR = 0.001
ADAM_B1 = 0.9
ADAM_B2 = 0.999
ADAM_EPS = 1e-08
ADAM_WD = 0.01
ADAM_STEP = 10


def _dot(a, b):
    return lax.dot_general(a, b, (((1,), (0,)), ((), ())), preferred_element_type=F32)


def _dot_nt(a, b):
    return lax.dot_general(a, b, (((1,), (1,)), ((), ())), preferred_element_type=F32)


def _dot_tn(a, b):
    return lax.dot_general(a, b, (((0,), (0,)), ((), ())), preferred_element_type=F32)


def _split(x):
    hi = x.astype(BF16)
    return hi, (x - hi.astype(F32)).astype(BF16)


def _sigmoid(x):
    return 1.0 / (1.0 + jnp.exp(-x))


def _iotas():
    row = lax.broadcasted_iota(jnp.int32, (BLK, BLK), 0)
    lane = lax.broadcasted_iota(jnp.int32, (BLK, BLK), 1)
    return row, lane


WIDE = 2 * BLK


def _key_mask(i, first_key, width, strict):
    t_pos = i * BLK + lax.broadcasted_iota(jnp.int32, (BLK, width), 0)
    s_pos = first_key + lax.broadcasted_iota(jnp.int32, (BLK, width), 1)
    seen = (s_pos < t_pos) if strict else (s_pos <= t_pos)
    return seen & (s_pos >= N_PAD)


def _widen(x, width):
    return x if width == BLK else jnp.concatenate([x] * (width // BLK), axis=1)


def _over_key_blocks(n, block, reverse):
    pairs = n // 2
    last = pl.multiple_of((n - 1) * BLK, BLK)

    def step(jj, carry):
        jp = (pairs - 1 - jj) if reverse else jj
        block(pl.multiple_of(jp * WIDE, WIDE), WIDE)
        return carry

    if reverse:
        pl.when(n % 2 == 1)(lambda: block(last, BLK))
        lax.fori_loop(0, pairs, step, 0)
    else:
        lax.fori_loop(0, pairs, step, 0)
        pl.when(n % 2 == 1)(lambda: block(last, BLK))


def _rope(x, c, s1, s2):
    return x * c + pltpu.roll(x, BLK - 16, 1) * s1 + pltpu.roll(x, 16, 1) * s2


def _rope_t(x, c, s1, s2):
    return x * c - pltpu.roll(x, BLK - 16, 1) * s1 - pltpu.roll(x, 16, 1) * s2


def _params(sem, vmem_mb=None):
    kw = dict(dimension_semantics=sem)
    if vmem_mb is not None:
        kw["vmem_limit_bytes"] = vmem_mb << 20
    return pltpu.CompilerParams(**kw)


def _row_tile(t, cands):
    for c in cands:
        if t % c == 0:
            return c
    raise ValueError(t)


def _mm(a, w, mode, name, add=None, out_dtype=F32):
    pieces = list(a) if isinstance(a, (list, tuple)) else [a]
    m = pieces[0].shape[0]
    n = w.shape[1] if mode == "nn" else w.shape[0]
    tm = _row_tile(m, (256, 128))
    widths = [p.shape[1] for p in pieces]
    offs = [sum(widths[:i]) for i in range(len(widths))]

    def body(*refs):
        p_refs = refs[:len(pieces)]
        w_ref = refs[len(pieces)]
        o_ref = refs[-1]
        acc = None
        for p_ref, off, wd in zip(p_refs, offs, widths):
            x = p_ref[...].astype(BF16)
            part = _dot(x, w_ref[off:off + wd, :]) if mode == "nn" else _dot_nt(x, w_ref[:, off:off + wd])
            acc = part if acc is None else acc + part
        if add is not None:
            acc = acc + refs[len(pieces) + 1][...]
        o_ref[...] = acc.astype(o_ref.dtype)

    in_specs = [pl.BlockSpec((tm, wd), lambda i: (i, 0)) for wd in widths]
    in_specs.append(pl.BlockSpec(w.shape, lambda i: (0, 0)))
    args = pieces + [w]
    if add is not None:
        in_specs.append(pl.BlockSpec((tm, n), lambda i: (i, 0)))
        args.append(add)
    return pl.pallas_call(
        body, name=name, grid=(m // tm,), in_specs=in_specs,
        out_specs=pl.BlockSpec((tm, n), lambda i: (i, 0)),
        out_shape=jax.ShapeDtypeStruct((m, n), out_dtype),
        compiler_params=_params(("parallel",), 48),
    )(*args)


def _mm_tn(x, pieces, name):
    t, k = x.shape
    tt = _row_tile(t, (544, 256, 128))
    widths = [p.shape[1] for p in pieces]

    def body(*refs):
        x_ref = refs[0]
        d_refs = refs[1:1 + len(pieces)]
        o_refs = refs[1 + len(pieces):]

        @pl.when(pl.program_id(0) == 0)
        def _():
            for o_ref in o_refs:
                o_ref[...] = jnp.zeros_like(o_ref)

        xb = x_ref[...].astype(BF16)
        for d_ref, o_ref in zip(d_refs, o_refs):
            o_ref[...] += _dot_tn(xb, d_ref[...].astype(BF16))

    return pl.pallas_call(
        body, name=name, grid=(t // tt,),
        in_specs=[pl.BlockSpec((tt, k), lambda i: (i, 0))] + [pl.BlockSpec((tt, wd), lambda i: (i, 0)) for wd in widths],
        out_specs=[pl.BlockSpec((k, wd), lambda i: (0, 0)) for wd in widths],
        out_shape=[jax.ShapeDtypeStruct((k, wd), F32) for wd in widths],
        compiler_params=_params(("arbitrary",), 56),
    )(x, *pieces)


def _rms_fwd(h, g, name):
    t, d = h.shape
    tr = _row_tile(t, (256, 128))

    def body(h_ref, g_ref, o_ref):
        x = h_ref[...]
        r = lax.rsqrt(jnp.mean(x * x, axis=1, keepdims=True) + NORM_EPS)
        o_ref[...] = (x * r * g_ref[...]).astype(o_ref.dtype)

    return pl.pallas_call(
        body, name=name, grid=(t // tr,),
        in_specs=[pl.BlockSpec((tr, d), lambda i: (i, 0)), pl.BlockSpec((1, d), lambda i: (0, 0))],
        out_specs=pl.BlockSpec((tr, d), lambda i: (i, 0)),
        out_shape=jax.ShapeDtypeStruct((t, d), BF16),
        compiler_params=_params(("parallel",)),
    )(h, g)


def _rms_bwd(h, g, dhn, dres, name):
    t, d = h.shape
    tr = _row_tile(t, (256, 128))

    def body(h_ref, g_ref, dhn_ref, dres_ref, dh_ref, dg_ref):
        @pl.when(pl.program_id(0) == 0)
        def _():
            dg_ref[...] = jnp.zeros_like(dg_ref)

        x = h_ref[...]
        r = lax.rsqrt(jnp.mean(x * x, axis=1, keepdims=True) + NORM_EPS)
        nx = x * r
        dy = dhn_ref[...]
        dn = dy * g_ref[...]
        dh_ref[...] = dres_ref[...] + r * (dn - nx * jnp.mean(dn * nx, axis=1, keepdims=True))
        dg_ref[...] += jnp.sum(dy * nx, axis=0, keepdims=True)

    row = pl.BlockSpec((tr, d), lambda i: (i, 0))
    vec = pl.BlockSpec((1, d), lambda i: (0, 0))
    return pl.pallas_call(
        body, name=name, grid=(t // tr,),
        in_specs=[row, vec, row, row], out_specs=[row, vec],
        out_shape=[jax.ShapeDtypeStruct((t, d), F32), jax.ShapeDtypeStruct((1, d), F32)],
        compiler_params=_params(("arbitrary",)),
    )(h, g, dhn, dres)


def _final(h2, tgt, g, nbatch, nb):
    t, d = h2.shape

    def body(h_ref, t_ref, g_ref, dh_ref, loss_ref, dg_ref):
        b = pl.program_id(0)
        i = pl.program_id(1)

        @pl.when((b == 0) & (i == 0))
        def _():
            loss_ref[...] = jnp.zeros_like(loss_ref)
            dg_ref[...] = jnp.zeros_like(dg_ref)

        x = h_ref[...]
        r = lax.rsqrt(jnp.mean(x * x, axis=1, keepdims=True) + NORM_EPS)
        nx = x * r
        gg = g_ref[...]
        live = jnp.where(i >= 1, 1.0, 0.0)
        err = (nx * gg - t_ref[...]) * live
        loss_ref[...] += jnp.sum(err * err, axis=0, keepdims=True)
        dy = err * (1.0 / d)
        dn = dy * gg
        dh_ref[...] = r * (dn - nx * jnp.mean(dn * nx, axis=1, keepdims=True))
        dg_ref[...] += jnp.sum(dy * nx, axis=0, keepdims=True)

    vec = pl.BlockSpec((1, d), lambda b, i: (0, 0))
    return pl.pallas_call(
        body, name="final_loss", grid=(nbatch, nb),
        in_specs=[pl.BlockSpec((BLK, d), lambda b, i: (b * nb + i, 0)),
                  pl.BlockSpec((BLK, d), lambda b, i: (b * (nb - 1) + jnp.maximum(i - 1, 0), 0)),
                  vec],
        out_specs=[pl.BlockSpec((BLK, d), lambda b, i: (b * nb + i, 0)), vec, vec],
        out_shape=[jax.ShapeDtypeStruct((t, d), F32), jax.ShapeDtypeStruct((1, d), F32),
                   jax.ShapeDtypeStruct((1, d), F32)],
        compiler_params=_params(("arbitrary", "arbitrary")),
    )(h2, tgt, g)


def _direct_exchange(srcs, dsts, send_sems, recv_sems, local_sems, chunked):
    x, y, c = lax.axis_index("x"), lax.axis_index("y"), lax.axis_index("c")
    me = 4 * x + 2 * y + c
    arrays = range(len(srcs))
    local = [pltpu.make_async_copy(srcs[a].at[me] if chunked else srcs[a], dsts[a].at[me], local_sems.at[a])
             for a in arrays]
    sends, recvs = [], []
    for d in range(1, N_DEV):
        px = x + ((d >> 2) & 1) - 2 * x * ((d >> 2) & 1)
        py = y + ((d >> 1) & 1) - 2 * y * ((d >> 1) & 1)
        pc = c + (d & 1) - 2 * c * (d & 1)
        pid = 4 * px + 2 * py + pc
        for a in arrays:
            kw = dict(send_sem=send_sems.at[d - 1, a], recv_sem=recv_sems.at[d - 1, a],
                      device_id=(px, py, pc), device_id_type=pl.DeviceIdType.MESH)
            src = srcs[a].at[pid] if chunked else srcs[a]
            sends.append(pltpu.make_async_remote_copy(src_ref=src, dst_ref=dsts[a].at[me], **kw))
            recvs.append(pltpu.make_async_remote_copy(src_ref=src, dst_ref=dsts[a].at[pid], **kw))
    return local, sends, recvs


def _exchange_beside(first, last, srcs, dsts, sems, chunked):
    local, sends, recvs = _direct_exchange(srcs, dsts, *sems, chunked)

    @pl.when(first)
    def _():
        for cp in local + sends:
            cp.start()

    @pl.when(last)
    def _():
        for cp in local:
            cp.wait()
        for cp in recvs:
            cp.wait_recv()
        for cp in sends:
            cp.wait_send()


def _exchange_sems(n):
    return [pltpu.SemaphoreType.DMA((7, n)), pltpu.SemaphoreType.DMA((7, n)), pltpu.SemaphoreType.DMA((n,))]


def _sb_logits(z):
    log_beta = jnp.minimum(z, 0.0) - jnp.log(1.0 + jnp.exp(-jnp.abs(z)))
    return log_beta, log_beta - z


def _tri(width, after):
    j = lax.broadcasted_iota(jnp.int32, (width, width), 0)
    s = lax.broadcasted_iota(jnp.int32, (width, width), 1)
    return (j > s) if after else (j < s)


def _tri2(tri, with_ones):
    m = tri.astype(BF16)
    if with_ones:
        m = jnp.concatenate([m, jnp.ones((tri.shape[0], BLK), BF16)], axis=1)
    return jnp.concatenate([m, m], axis=0)


def _block_sums(x, tri_ones, after):
    hi, lo = _split(x)
    subs = [_dot(jnp.concatenate([hi[:, s:s + BLK], lo[:, s:s + BLK]], axis=1), tri_ones)
            for s in range(0, x.shape[1], BLK)]
    if len(subs) == 1:
        return subs[0][:, :BLK], subs[0][:, BLK:]
    first, second = subs
    total = first[:, BLK:] + second[:, BLK:]
    if after:
        return jnp.concatenate([first[:, :BLK] + second[:, BLK:], second[:, :BLK]], axis=1), total
    return jnp.concatenate([first[:, :BLK], second[:, :BLK] + first[:, BLK:]], axis=1), total


def _head_masked(x, lane, scale=None):
    out = []
    for h in range(8):
        xp = x[:, (h // 2) * BLK:(h // 2 + 1) * BLK]
        xm = jnp.where((lane >= HALF) if h % 2 else (lane < HALF), xp, jnp.zeros_like(xp))
        out.append(xm if scale is None else xm * scale)
    return jnp.concatenate(out, axis=1)


def _sb_fwd(proj, shards, nbatch, nb):
    lp = nb * BLK
    t = nbatch * lp

    ns = len(shards)

    def body(*refs):
        q_ref, k_ref, v_ref, g_ref = refs[:4]
        shard_refs = refs[4:4 + ns]
        o_ref, og_ref, cm_ref = refs[4 + ns:7 + ns]
        gathered_refs = refs[7 + ns:7 + 2 * ns]
        c_scr, qm_scr = refs[7 + 2 * ns:9 + 2 * ns]
        b = pl.program_id(0)
        i = pl.program_id(1)
        _exchange_beside((b == 0) & (i == 0), (b == nbatch - 1) & (i == nb - 1),
                         shard_refs, gathered_refs, refs[9 + 2 * ns:], chunked=False)
        _, lane = _iotas()
        lo_m = lane < HALF
        cm_ref[...] = jnp.zeros_like(cm_ref)
        c_scr[...] = jnp.zeros_like(c_scr)
        o_ref[...] = jnp.zeros_like(o_ref)
        qm_scr[...] = _head_masked(q_ref[...], lane, SB_SCALE)

        def block(off, width):
            mask = _key_mask(i, off, width, strict=True)
            upper = _tri2(_tri(BLK, after=True), True)
            onehot = lane == off // WIDE
            heads = range(8)
            hcs = [slice(h * BLK, (h + 1) * BLK) for h in heads]
            kbs = [k_ref[pl.ds(off, width), hc] for hc in hcs[:4]]
            vbs = [v_ref[pl.ds(off, width), hc] for hc in hcs[:4]]
            zs = [_dot_nt(qm_scr[:, hcs[h]], kbs[h // 2]) for h in heads]
            lbs, l1s = [], []
            for h in heads:
                log_beta, log_1m = _sb_logits(zs[h])
                lbs.append(log_beta)
                l1s.append(jnp.where(mask, log_1m, 0.0))
            css = [_block_sums(l1s[h], upper, after=True) for h in heads]
            avs = []
            for h in heads:
                c = c_scr[h]
                avs.append(jnp.where(mask, jnp.exp(lbs[h] + css[h][0] + _widen(c, width)), 0.0).astype(BF16))
                if width == WIDE:
                    cm_ref[:, hcs[h]] = jnp.where(onehot, c, cm_ref[:, hcs[h]])
                c_scr[h] = c + css[h][1]
            accs = [_dot(avs[h], vbs[h // 2]) for h in heads]
            for p in range(4):
                o_ref[:, hcs[p]] += jnp.where(lo_m, accs[2 * p], accs[2 * p + 1])

        _over_key_blocks(i + 1, block, reverse=True)
        g = g_ref[...].astype(F32)
        og_ref[...] = (o_ref[...] * g * _sigmoid(g)).astype(og_ref.dtype)

    tile = lambda col: pl.BlockSpec((BLK, 512), lambda b, i: (b * nb + i, col))
    full = lambda col: pl.BlockSpec((lp, 512), lambda b, i: (b, col))
    hbm = pl.BlockSpec(memory_space=pl.ANY)
    outs = pl.pallas_call(
        body, name="sb_fwd", grid=(nbatch, nb),
        in_specs=[tile(0), full(1), full(2), tile(3)] + [hbm] * ns,
        out_specs=[tile(0), tile(0), pl.BlockSpec((BLK, 1024), lambda b, i: (b * nb + i, 0))] + [hbm] * ns,
        out_shape=[jax.ShapeDtypeStruct((t, 512), F32), jax.ShapeDtypeStruct((t, 512), BF16),
                   jax.ShapeDtypeStruct((t, 1024), F32)]
        + [jax.ShapeDtypeStruct((N_DEV,) + s.shape, s.dtype) for s in shards],
        scratch_shapes=[pltpu.VMEM((8, BLK, BLK), F32), pltpu.VMEM((BLK, 1024), BF16)] + _exchange_sems(ns),
        compiler_params=_params(("arbitrary", "arbitrary"), 48),
    )(proj, proj, proj, proj, *shards)
    return outs[0], outs[1], outs[2], list(outs[3:])


def _sb_bwd(proj, o, dog, cm, chunked, nbatch, nb):
    lp = nb * BLK
    t = nbatch * lp
    ns = len(chunked)

    def body(*refs):
        q_ref, k_ref, v_ref, g_ref, o_ref, dog_ref, cm_ref = refs[:7]
        chunk_refs = refs[7:7 + ns]
        dq_ref, dk_ref, dv_ref, dg_ref = refs[7 + ns:11 + ns]
        received_refs = refs[11 + ns:11 + 2 * ns]
        c_scr, qm_scr, dom_scr, dq_scr = refs[11 + 2 * ns:15 + 2 * ns]
        b = pl.program_id(0)
        i = pl.program_id(1)
        _exchange_beside((b == 0) & (i == 0), (b == nbatch - 1) & (i == nb - 1),
                         chunk_refs, received_refs, refs[15 + 2 * ns:], chunked=True)

        @pl.when(i == 0)
        def _():
            dk_ref[...] = jnp.zeros_like(dk_ref)
            dv_ref[...] = jnp.zeros_like(dv_ref)

        _, lane = _iotas()
        lo_m = lane < HALF
        g = g_ref[...].astype(F32)
        sig = _sigmoid(g)
        dog_v = dog_ref[...]
        dg_ref[...] = (dog_v * o_ref[...] * (sig * (1.0 + g * (1.0 - sig)))).astype(dg_ref.dtype)
        dom_scr[...] = _head_masked((dog_v * g * sig).astype(BF16), lane)
        qm_scr[...] = _head_masked(q_ref[...], lane, SB_SCALE)
        c_scr[...] = jnp.zeros_like(c_scr)
        dq_scr[...] = jnp.zeros_like(dq_scr)

        def block(off, width):
            mask = _key_mask(i, off, width, strict=True)
            upper = _tri2(_tri(BLK, after=True), True)
            lower = _tri2(_tri(BLK, after=False), True)
            onehot = lane == off // WIDE
            heads = range(8)
            hcs = [slice(h * BLK, (h + 1) * BLK) for h in heads]
            kbs = [k_ref[pl.ds(off, width), hc] for hc in hcs[:4]]
            vbs = [v_ref[pl.ds(off, width), hc] for hc in hcs[:4]]
            zs = [_dot_nt(qm_scr[:, hcs[h]], kbs[h // 2]) for h in heads]
            dps = [_dot_nt(dom_scr[:, hcs[h]], vbs[h // 2]) for h in heads]
            lbs, l1s = [], []
            for h in heads:
                log_beta, log_1m = _sb_logits(zs[h])
                lbs.append(log_beta)
                l1s.append(log_1m)
            sufs = [_block_sums(jnp.where(mask, l1s[h], 0.0), upper, after=True)[0] for h in heads]
            prs, dzzs = [], []
            for h in heads:
                expo = lbs[h] + sufs[h]
                if width == WIDE:
                    expo = expo + jnp.sum(jnp.where(onehot, cm_ref[:, hcs[h]], 0.0), axis=1, keepdims=True)
                pr = jnp.where(mask, jnp.exp(expo), 0.0)
                dzz = pr * dps[h]
                prs.append(pr.astype(BF16))
                dzzs.append(dzz)
            css = [_block_sums(dzzs[h], lower, after=False) for h in heads]
            dzbs = []
            for h in heads:
                c2 = c_scr[h]
                prefix = css[h][0] + _widen(c2, width)
                dz = jnp.where(mask, dzzs[h] * jnp.exp(l1s[h]) - jnp.exp(lbs[h]) * prefix, 0.0)
                dzbs.append(dz.astype(BF16))
                c_scr[h] = c2 + css[h][1]
            dqs = [_dot(dzbs[h], kbs[h // 2]) for h in heads]
            dks = [_dot_tn(dzbs[h], qm_scr[:, hcs[h]]) for h in heads]
            dvs = [_dot_tn(prs[h], dom_scr[:, hcs[h]]) for h in heads]
            for p in range(4):
                dq_scr[:, hcs[p]] += jnp.where(lo_m, dqs[2 * p], dqs[2 * p + 1])
                dk_ref[pl.ds(off, width), hcs[p]] += dks[2 * p] + dks[2 * p + 1]
                dv_ref[pl.ds(off, width), hcs[p]] += dvs[2 * p] + dvs[2 * p + 1]

        _over_key_blocks(i + 1, block, reverse=False)
        dq_ref[...] = (dq_scr[...] * SB_SCALE).astype(dq_ref.dtype)

    tile = lambda col: pl.BlockSpec((BLK, 512), lambda b, i: (b * nb + i, col))
    full = lambda col: pl.BlockSpec((lp, 512), lambda b, i: (b, col))
    acc = jax.ShapeDtypeStruct((t, 512), F32)
    once = jax.ShapeDtypeStruct((t, 512), BF16)
    hbm = pl.BlockSpec(memory_space=pl.ANY)
    outs = pl.pallas_call(
        body, name="sb_bwd", grid=(nbatch, nb),
        in_specs=[tile(0), full(1), full(2), tile(3), tile(0), tile(0),
                  pl.BlockSpec((BLK, 1024), lambda b, i: (b * nb + i, 0))] + [hbm] * ns,
        out_specs=[tile(0), full(0), full(0), tile(0)] + [hbm] * ns,
        out_shape=[once, acc, acc, once] + [jax.ShapeDtypeStruct(a.shape, a.dtype) for a in chunked],
        scratch_shapes=[pltpu.VMEM((8, BLK, BLK), F32), pltpu.VMEM((BLK, 1024), BF16),
                        pltpu.VMEM((BLK, 1024), BF16), pltpu.VMEM((BLK, 512), F32)] + _exchange_sems(ns),
        compiler_params=_params(("arbitrary", "arbitrary"), 56),
    )(proj, proj, proj, proj, o, dog, cm, *chunked)
    return outs[0], outs[1], outs[2], outs[3], list(outs[4:])


def _mla_prep(proj, gq, gkv, tabs, nbatch, nb):
    t = proj.shape[0]

    def body(mid_ref, gq_ref, gkv_ref, c_ref, s1_ref, s2_ref, cq_ref, ckv_ref, kr_ref):
        cq = mid_ref[:, 0:256].astype(F32)
        r = lax.rsqrt(jnp.mean(cq * cq, axis=1, keepdims=True) + NORM_EPS)
        cq_ref[...] = (cq * r * gq_ref[...]).astype(BF16)
        ckv = mid_ref[:, 256:384].astype(F32)
        r = lax.rsqrt(jnp.mean(ckv * ckv, axis=1, keepdims=True) + NORM_EPS)
        ckv_ref[...] = (ckv * r * gkv_ref[...]).astype(BF16)
        kr = mid_ref[:, 384:512].astype(F32)
        kr_ref[...] = _rope(kr, c_ref[...], s1_ref[...], s2_ref[...]).astype(BF16)

    tab = pl.BlockSpec((BLK, BLK), lambda b, i: (i, 0))
    rowspec = lambda w: pl.BlockSpec((BLK, w), lambda b, i: (b * nb + i, 0))
    return pl.pallas_call(
        body, name="mla_prep", grid=(nbatch, nb),
        in_specs=[pl.BlockSpec((BLK, 512), lambda b, i: (b * nb + i, EV_MID_BLK)),
                  pl.BlockSpec((1, 256), lambda b, i: (0, 0)), pl.BlockSpec((1, 128), lambda b, i: (0, 0)),
                  tab, tab, tab],
        out_specs=[rowspec(256), rowspec(128), rowspec(128)],
        out_shape=[jax.ShapeDtypeStruct((t, 256), BF16), jax.ShapeDtypeStruct((t, 128), BF16),
                   jax.ShapeDtypeStruct((t, 128), BF16)],
        compiler_params=_params(("parallel", "parallel")),
    )(proj, gq, gkv, *tabs)


def _mla_prep_bwd(proj, gq, gkv, tabs, dcqn, dckvn, dkrot, nbatch, nb):
    t = proj.shape[0]

    def body(mid_ref, gq_ref, gkv_ref, c_ref, s1_ref, s2_ref, dcq_ref, dckv_ref, dkr_ref,
             dmid_ref, dgq_ref, dgkv_ref):
        @pl.when((pl.program_id(0) == 0) & (pl.program_id(1) == 0))
        def _():
            dgq_ref[...] = jnp.zeros_like(dgq_ref)
            dgkv_ref[...] = jnp.zeros_like(dgkv_ref)

        def norm_bwd(x, gain, dy, dgain_ref):
            r = lax.rsqrt(jnp.mean(x * x, axis=1, keepdims=True) + NORM_EPS)
            nx = x * r
            dn = dy * gain
            dgain_ref[...] += jnp.sum(dy * nx, axis=0, keepdims=True)
            return r * (dn - nx * jnp.mean(dn * nx, axis=1, keepdims=True))

        dmid_ref[:, 0:256] = norm_bwd(
            mid_ref[:, 0:256].astype(F32), gq_ref[...], dcq_ref[...], dgq_ref).astype(BF16)
        dmid_ref[:, 256:384] = norm_bwd(
            mid_ref[:, 256:384].astype(F32), gkv_ref[...], dckv_ref[...], dgkv_ref).astype(BF16)
        dmid_ref[:, 384:512] = _rope_t(dkr_ref[...], c_ref[...], s1_ref[...], s2_ref[...]).astype(BF16)

    tab = pl.BlockSpec((BLK, BLK), lambda b, i: (i, 0))
    rowspec = lambda w: pl.BlockSpec((BLK, w), lambda b, i: (b * nb + i, 0))
    vq = pl.BlockSpec((1, 256), lambda b, i: (0, 0))
    vkv = pl.BlockSpec((1, 128), lambda b, i: (0, 0))
    return pl.pallas_call(
        body, name="mla_prep_bwd", grid=(nbatch, nb),
        in_specs=[pl.BlockSpec((BLK, 512), lambda b, i: (b * nb + i, EV_MID_BLK)), vq, vkv, tab, tab, tab,
                  rowspec(256), rowspec(128), rowspec(128)],
        out_specs=[rowspec(512), vq, vkv],
        out_shape=[jax.ShapeDtypeStruct((t, 512), BF16), jax.ShapeDtypeStruct((1, 256), F32),
                   jax.ShapeDtypeStruct((1, 128), F32)],
        compiler_params=_params(("arbitrary", "arbitrary")),
    )(proj, gq, gkv, *tabs, dcqn, dckvn, dkrot)


def _mla_scores(qf, kvb, krb, mask, lo_m):
    kf = jnp.where(lo_m, kvb, krb)
    s = _dot_nt(qf, kf) * MLA_SCALE
    return kf, jnp.where(mask, s, NEG)


def _mla_fwd(qh, kvh, krot, proj, tabs, nbatch, nb):
    lp = nb * BLK
    t = nbatch * lp

    def body(q_ref, kv_ref, kr_ref, g_ref, c_ref, s1_ref, s2_ref, o_ref, og_ref, lse_ref,
             qf_scr, m_scr, l_scr, acc_scr):
        i = pl.program_id(1)
        row, lane = _iotas()
        lo_m = lane < HALF
        for h in range(8):
            hc = slice(h * BLK, (h + 1) * BLK)
            qf_scr[:, hc] = _rope(q_ref[:, hc].astype(F32), c_ref[...], s1_ref[...], s2_ref[...]).astype(BF16)
        m_scr[...] = jnp.full(m_scr.shape, NEG, F32)
        l_scr[...] = jnp.zeros_like(l_scr)
        acc_scr[...] = jnp.zeros_like(acc_scr)

        def block(off, width):
            mask = _key_mask(i, off, width, strict=False)
            lo_k = lax.broadcasted_iota(jnp.int32, (width, BLK), 1) < HALF
            ones = jnp.ones((width, BLK), BF16)
            krb = kr_ref[pl.ds(off, width), :]
            heads = range(8)
            hcs = [slice(h * BLK, (h + 1) * BLK) for h in heads]
            kvbs = [kv_ref[pl.ds(off, width), hc] for hc in hcs]
            ss = [_mla_scores(qf_scr[:, hcs[h]], kvbs[h], krb, mask, lo_k)[1] for h in heads]
            ps, alphas = [], []
            for h in heads:
                m = m_scr[h]
                m2 = jnp.maximum(m, jnp.max(ss[h], axis=1, keepdims=True))
                ps.append(jnp.exp(ss[h] - _widen(m2, width)).astype(BF16))
                alphas.append(jnp.exp(m - m2))
                m_scr[h] = m2
            pvs = [_dot(ps[h], jnp.concatenate([kvbs[h], ones], axis=1)) for h in heads]
            for h in heads:
                l_scr[h] = alphas[h] * l_scr[h] + pvs[h][:, BLK:]
                acc_scr[h] = alphas[h] * acc_scr[h] + pvs[h][:, :BLK]

        _over_key_blocks(i + 1, block, reverse=False)
        lse = jnp.zeros((BLK, BLK), F32)
        for p in range(4):
            pc = slice(p * BLK, (p + 1) * BLK)
            o0 = acc_scr[2 * p] / l_scr[2 * p]
            o1 = acc_scr[2 * p + 1] / l_scr[2 * p + 1]
            o_ref[:, pc] = jnp.where(lo_m, pltpu.roll(o0, HALF, 1), o1)
            for h in (2 * p, 2 * p + 1):
                lse = lse + jnp.where(lane == h, m_scr[h] + jnp.log(l_scr[h]), 0.0)
        lse_ref[...] = lse
        g = g_ref[...].astype(F32)
        og_ref[...] = (o_ref[...] * g * _sigmoid(g)).astype(og_ref.dtype)

    tile = pl.BlockSpec((BLK, 512), lambda b, i: (b * nb + i, 0))
    tab = pl.BlockSpec((BLK, BLK), lambda b, i: (i, 0))
    heads = pltpu.VMEM((8, BLK, BLK), F32)
    return pl.pallas_call(
        body, name="mla_fwd", grid=(nbatch, nb),
        in_specs=[pl.BlockSpec((BLK, 1024), lambda b, i: (b * nb + i, 0)),
                  pl.BlockSpec((lp, 1024), lambda b, i: (b, 0)),
                  pl.BlockSpec((lp, BLK), lambda b, i: (b, 0)),
                  pl.BlockSpec((BLK, 512), lambda b, i: (b * nb + i, EV_GMLA_BLK // 4)),
                  tab, tab, tab],
        out_specs=[tile, tile, pl.BlockSpec((BLK, BLK), lambda b, i: (b * nb + i, 0))],
        out_shape=[jax.ShapeDtypeStruct((t, 512), F32), jax.ShapeDtypeStruct((t, 512), BF16),
                   jax.ShapeDtypeStruct((t, BLK), F32)],
        scratch_shapes=[pltpu.VMEM((BLK, 1024), BF16), heads, heads, heads],
        compiler_params=_params(("parallel", "arbitrary"), 48),
    )(qh, kvh, krot, proj, *tabs)


def _mla_bwd(qh, kvh, krot, proj, tabs, o, dog, lse, nbatch, nb):
    lp = nb * BLK
    t = nbatch * lp

    def body(q_ref, kv_ref, kr_ref, g_ref, c_ref, s1_ref, s2_ref, o_ref, dog_ref, lse_ref,
             dq_ref, dkv_ref, dkr_ref, dg_ref, qf_scr, do_scr, stat_scr, acc_scr):
        i = pl.program_id(1)

        @pl.when(i == 0)
        def _():
            dkv_ref[...] = jnp.zeros_like(dkv_ref)
            dkr_ref[...] = jnp.zeros_like(dkr_ref)

        row, lane = _iotas()
        lo_m = lane < HALF
        g = g_ref[...].astype(F32)
        sig = _sigmoid(g)
        dog_v = dog_ref[...]
        o_v = o_ref[...]
        dg_ref[...] = (dog_v * o_v * (sig * (1.0 + g * (1.0 - sig)))).astype(dg_ref.dtype)
        do = dog_v * g * sig
        do_o = do * o_v
        lse_blk = lse_ref[...]
        zero = jnp.zeros((BLK, BLK), F32)
        for h in range(8):
            hc = slice(h * BLK, (h + 1) * BLK)
            pc = slice((h // 2) * BLK, (h // 2 + 1) * BLK)
            qf_scr[:, hc] = _rope(q_ref[:, hc].astype(F32), c_ref[...], s1_ref[...], s2_ref[...]).astype(BF16)
            dop = do[:, pc]
            do_src = dop if h % 2 else pltpu.roll(dop, HALF, 1)
            do_scr[:, hc] = jnp.where(lo_m, 0.0, do_src).astype(BF16)
            hm = (lane >= HALF) if h % 2 else lo_m
            stat_scr[h] = zero + jnp.sum(jnp.where(hm, do_o[:, pc], 0.0), axis=1, keepdims=True)
            stat_scr[8 + h] = zero + jnp.sum(jnp.where(lane == h, lse_blk, 0.0), axis=1, keepdims=True)
        acc_scr[...] = jnp.zeros_like(acc_scr)

        def block(off, width):
            mask = _key_mask(i, off, width, strict=False)
            lo_k = lax.broadcasted_iota(jnp.int32, (width, BLK), 1) < HALF
            krb = kr_ref[pl.ds(off, width), :]
            heads = range(8)
            hcs = [slice(h * BLK, (h + 1) * BLK) for h in heads]
            kvbs = [kv_ref[pl.ds(off, width), hc] for hc in hcs]
            qfs = [qf_scr[:, hc] for hc in hcs]
            dos = [do_scr[:, hc] for hc in hcs]
            scored = [_mla_scores(qfs[h], kvbs[h], krb, mask, lo_k) for h in heads]
            dps = [_dot_nt(dos[h], kvbs[h]) for h in heads]
            pbs, dss = [], []
            for h in heads:
                p = jnp.exp(scored[h][1] - _widen(stat_scr[8 + h], width))
                pbs.append(p.astype(BF16))
                dss.append((p * (dps[h] - _widen(stat_scr[h], width)) * MLA_SCALE).astype(BF16))
            dqs = [_dot(dss[h], scored[h][0]) for h in heads]
            dkfs = [_dot_tn(dss[h], qfs[h]) for h in heads]
            dvvs = [_dot_tn(pbs[h], dos[h]) for h in heads]
            dkr = jnp.zeros((width, BLK), F32)
            for h in heads:
                acc_scr[h] += dqs[h]
                dkv_ref[pl.ds(off, width), hcs[h]] += jnp.where(lo_k, dkfs[h], 0.0) + dvvs[h]
                dkr = dkr + jnp.where(lo_k, 0.0, dkfs[h])
            dkr_ref[pl.ds(off, width), :] += dkr

        _over_key_blocks(i + 1, block, reverse=False)
        for h in range(8):
            hc = slice(h * BLK, (h + 1) * BLK)
            dq_ref[:, hc] = _rope_t(acc_scr[h], c_ref[...], s1_ref[...], s2_ref[...]).astype(dq_ref.dtype)

    tile = lambda col: pl.BlockSpec((BLK, 512), lambda b, i: (b * nb + i, col))
    wide = pl.BlockSpec((BLK, 1024), lambda b, i: (b * nb + i, 0))
    full8 = pl.BlockSpec((lp, 1024), lambda b, i: (b, 0))
    full1 = pl.BlockSpec((lp, BLK), lambda b, i: (b, 0))
    tab = pl.BlockSpec((BLK, BLK), lambda b, i: (i, 0))
    return pl.pallas_call(
        body, name="mla_bwd", grid=(nbatch, nb),
        in_specs=[wide, full8, full1, tile(EV_GMLA_BLK // 4), tab, tab, tab, tile(0), tile(1),
                  pl.BlockSpec((BLK, BLK), lambda b, i: (b * nb + i, 0))],
        out_specs=[wide, full8, full1, tile(0)],
        out_shape=[jax.ShapeDtypeStruct((t, 1024), BF16), jax.ShapeDtypeStruct((t, 1024), F32),
                   jax.ShapeDtypeStruct((t, 128), F32), jax.ShapeDtypeStruct((t, 512), BF16)],
        scratch_shapes=[pltpu.VMEM((BLK, 1024), BF16), pltpu.VMEM((BLK, 1024), BF16),
                        pltpu.VMEM((16, BLK, BLK), F32), pltpu.VMEM((8, BLK, BLK), F32)],
        compiler_params=_params(("parallel", "arbitrary"), 56),
    )(qh, kvh, krot, proj, *tabs, o, dog, lse)


def _swa_setup(kk, i, k_refs, v_refs):
    row, lane = _iotas()
    own = (lane >= kk * HALF) & (lane < (kk + 1) * HALF)

    def dup(ref):
        x = ref[...].astype(F32)
        return jnp.where(own, x, pltpu.roll(x, HALF, 1)).astype(BF16)

    kcat = jnp.concatenate([dup(r) for r in k_refs], axis=0)
    vcat = jnp.concatenate([dup(r) for r in v_refs], axis=0)
    row3 = lax.broadcasted_iota(jnp.int32, (BLK, 3 * BLK), 0)
    lane3 = lax.broadcasted_iota(jnp.int32, (BLK, 3 * BLK), 1)
    is_meta = lane3 < BLK
    k_pos = jnp.where(is_meta, lane3, (i - 2) * BLK + lane3)
    d = i * BLK + row3 - k_pos
    mask = (d >= 0) & (d < jnp.where(is_meta, 1 << 20, BLK)) & (k_pos >= jnp.where(is_meta, N_PAD, BLK))
    return lane, own, kcat, vcat, mask, d.astype(F32)


def _swa_slope(kk, g_idx):
    return (2.0 ** (-(g_idx + 1) / 2.0)) * jnp.where(kk == 0, 1.0, 1.0 / 16.0)


def _swa_fwd(proj, sinks, nbatch, nb):
    lp = nb * BLK
    t = nbatch * lp

    def body(sink_ref, q_ref, ka, kb, kc, va, vb, vc, g_ref, o_ref, og_ref, lse_ref):
        kk = pl.program_id(1)
        i = pl.program_id(2)
        lane, own, kcat, vcat, mask, dist = _swa_setup(kk, i, (ka, kb, kc), (va, vb, vc))
        lo_m = lane < HALF
        heads = range(8)
        qms = []
        for h in heads:
            qp = q_ref[:, (h // 2) * BLK:(h // 2 + 1) * BLK]
            qms.append(jnp.where((lane >= HALF) if h % 2 else lo_m, qp, jnp.zeros_like(qp)))
        qks = [_dot_nt(qms[h], kcat) for h in heads]
        ps, ls, lses = [], [], []
        for h in heads:
            sink = sink_ref[kk, h]
            s = jnp.where(mask, qks[h] * SWA_SCALE - _swa_slope(kk, h) * dist, NEG)
            mx = jnp.maximum(jnp.max(s, axis=1, keepdims=True), sink)
            p = jnp.exp(s - mx)
            l = jnp.exp(sink - mx) + jnp.sum(p, axis=1, keepdims=True)
            ps.append(p.astype(BF16))
            ls.append(l)
            lses.append(mx + jnp.log(l))
        pvs = [_dot(ps[h], vcat) for h in heads]
        lse_out = jnp.zeros((BLK, BLK), F32)
        for m in range(4):
            cols = slice(m * BLK, (m + 1) * BLK)
            outp = jnp.where(lo_m, pvs[2 * m] / ls[2 * m], pvs[2 * m + 1] / ls[2 * m + 1])
            o_ref[:, cols] = outp
            g = g_ref[:, cols].astype(F32)
            og_ref[:, cols] = (outp * g * _sigmoid(g)).astype(og_ref.dtype)
            for h in (2 * m, 2 * m + 1):
                lse_out = lse_out + jnp.where(lane == h, lses[h], 0.0)
        lse_ref[...] = lse_out

    def kvspec(col, which):
        if which == 0:
            return pl.BlockSpec((BLK, BLK), lambda b, kk, i: (b * nb, col))
        if which == 1:
            return pl.BlockSpec((BLK, BLK), lambda b, kk, i: (b * nb + jnp.maximum(i - 1, 0), col))
        return pl.BlockSpec((BLK, BLK), lambda b, kk, i: (b * nb + i, col))

    wide = lambda c0: pl.BlockSpec((BLK, 512), lambda b, kk, i: (b * nb + i, c0 + kk))
    return pl.pallas_call(
        body, name="swa_fwd", grid=(nbatch, 2, nb),
        in_specs=[pl.BlockSpec(memory_space=pltpu.SMEM), wide(0),
                  kvspec(OD_K_BLK, 0), kvspec(OD_K_BLK, 1), kvspec(OD_K_BLK, 2),
                  kvspec(OD_V_BLK, 0), kvspec(OD_V_BLK, 1), kvspec(OD_V_BLK, 2), wide(2)],
        out_specs=[wide(0), wide(0), pl.BlockSpec((BLK, BLK), lambda b, kk, i: (b * nb + i, kk))],
        out_shape=[jax.ShapeDtypeStruct((t, 1024), F32), jax.ShapeDtypeStruct((t, 1024), BF16),
                   jax.ShapeDtypeStruct((t, 256), F32)],
        compiler_params=_params(("parallel", "parallel", "arbitrary")),
    )(sinks, proj, proj, proj, proj, proj, proj, proj, proj)


def _swa_bwd(proj, sinks, o, dog, lse, nbatch, nb):
    lp = nb * BLK
    t = nbatch * lp

    def body(sink_ref, q_ref, ka, kb, kc, va, vb, vc, g_ref, o_ref, dog_ref, lse_ref,
             dq_ref, dg_ref, dk_ref, dv_ref, dsink_ref):
        kk = pl.program_id(1)
        i = pl.program_id(2)

        @pl.when((kk == 0) & (i == 0))
        def _():
            dk_ref[...] = jnp.zeros_like(dk_ref)
            dv_ref[...] = jnp.zeros_like(dv_ref)

        @pl.when(i == 0)
        def _():
            dsink_ref[...] = jnp.zeros_like(dsink_ref)

        lane, own, kcat, vcat, mask, dist = _swa_setup(kk, i, (ka, kb, kc), (va, vb, vc))
        lo_m = lane < HALF
        row8 = lax.broadcasted_iota(jnp.int32, (8, BLK), 0)
        lse_blk = lse_ref[...]
        heads = range(8)
        qms, doms, deltas, lse_hs = [], [], [], []
        for m in range(4):
            cols = slice(m * BLK, (m + 1) * BLK)
            qp = q_ref[:, cols]
            g = g_ref[:, cols].astype(F32)
            sig = _sigmoid(g)
            dog_v = dog_ref[:, cols]
            o_v = o_ref[:, cols]
            dg_ref[:, cols] = (dog_v * o_v * (sig * (1.0 + g * (1.0 - sig)))).astype(dg_ref.dtype)
            do = dog_v * g * sig
            do_o = do * o_v
            dob = do.astype(BF16)
            for h in (2 * m, 2 * m + 1):
                hm = (lane >= HALF) if h % 2 else lo_m
                qms.append(jnp.where(hm, qp, jnp.zeros_like(qp)))
                doms.append(jnp.where(hm, dob, jnp.zeros_like(dob)))
                deltas.append(jnp.sum(jnp.where(hm, do_o, 0.0), axis=1, keepdims=True))
                lse_hs.append(jnp.sum(jnp.where(lane == h, lse_blk, 0.0), axis=1, keepdims=True))
        qks = [_dot_nt(qms[h], kcat) for h in heads]
        dps = [_dot_nt(doms[h], vcat) for h in heads]
        pbs, dss = [], []
        dsink = jnp.zeros((8, BLK), F32)
        for h in heads:
            s = jnp.where(mask, qks[h] * SWA_SCALE - _swa_slope(kk, h) * dist, NEG)
            p = jnp.exp(s - lse_hs[h])
            pbs.append(p.astype(BF16))
            dss.append((p * (dps[h] - deltas[h]) * SWA_SCALE).astype(BF16))
            tot = jnp.sum(-jnp.exp(sink_ref[kk, h] - lse_hs[h]) * deltas[h], axis=0, keepdims=True)
            dsink = dsink + jnp.where(row8 == h, tot, 0.0)
        dsink_ref[...] += dsink
        dqs = [_dot(dss[h], kcat) for h in heads]
        dks = [_dot_tn(dss[h], qms[h]) for h in heads]
        dvs = [_dot_tn(pbs[h], doms[h]) for h in heads]
        for m in range(4):
            dq_ref[:, m * BLK:(m + 1) * BLK] = jnp.where(lo_m, dqs[2 * m], dqs[2 * m + 1]).astype(dq_ref.dtype)
        dk = dks[0]
        dv = dvs[0]
        for h in range(1, 8):
            dk = dk + dks[h]
            dv = dv + dvs[h]
        offs = [0, pl.multiple_of(jnp.maximum(i - 1, 0) * BLK, BLK), pl.multiple_of(i * BLK, BLK)]
        for x in range(3):
            rows = slice(x * BLK, (x + 1) * BLK)
            dkx, dvx = dk[rows], dv[rows]
            dk_ref[pl.ds(offs[x], BLK), :] += jnp.where(own, dkx + pltpu.roll(dkx, HALF, 1), 0.0)
            dv_ref[pl.ds(offs[x], BLK), :] += jnp.where(own, dvx + pltpu.roll(dvx, HALF, 1), 0.0)

    def kvspec(col, which):
        if which == 0:
            return pl.BlockSpec((BLK, BLK), lambda b, kk, i: (b * nb, col))
        if which == 1:
            return pl.BlockSpec((BLK, BLK), lambda b, kk, i: (b * nb + jnp.maximum(i - 1, 0), col))
        return pl.BlockSpec((BLK, BLK), lambda b, kk, i: (b * nb + i, col))

    wide = lambda c0: pl.BlockSpec((BLK, 512), lambda b, kk, i: (b * nb + i, c0 + kk))
    full = pl.BlockSpec((lp, BLK), lambda b, kk, i: (b, 0))
    return pl.pallas_call(
        body, name="swa_bwd", grid=(nbatch, 2, nb),
        in_specs=[pl.BlockSpec(memory_space=pltpu.SMEM), wide(0),
                  kvspec(OD_K_BLK, 0), kvspec(OD_K_BLK, 1), kvspec(OD_K_BLK, 2),
                  kvspec(OD_V_BLK, 0), kvspec(OD_V_BLK, 1), kvspec(OD_V_BLK, 2), wide(2),
                  wide(0), wide(0), pl.BlockSpec((BLK, BLK), lambda b, kk, i: (b * nb + i, kk))],
        out_specs=[wide(0), wide(0), full, full,
                   pl.BlockSpec((8, BLK), lambda b, kk, i: (b * 2 + kk, 0))],
        out_shape=[jax.ShapeDtypeStruct((t, 1024), BF16), jax.ShapeDtypeStruct((t, 1024), BF16),
                   jax.ShapeDtypeStruct((t, 128), F32), jax.ShapeDtypeStruct((t, 128), F32),
                   jax.ShapeDtypeStruct((nbatch * 16, BLK), F32)],
        compiler_params=_params(("parallel", "arbitrary", "arbitrary")),
    )(sinks, proj, proj, proj, proj, proj, proj, proj, proj, o, dog, lse)


def _rope_tables(lp):
    pos = (jnp.arange(lp) - N_PAD).astype(F32)
    inv = ROPE_BASE ** (-jnp.arange(16, dtype=F32) / 16.0)
    ang = pos[:, None] * inv[None, :]
    cos, sin = jnp.cos(ang), jnp.sin(ang)
    z16 = jnp.zeros((lp, 16), F32)
    c = jnp.concatenate([jnp.ones((lp, 64), F32), cos, cos, jnp.zeros((lp, 32), F32)], axis=1)
    s1 = jnp.concatenate([jnp.zeros((lp, 64), F32), -sin, z16, jnp.zeros((lp, 32), F32)], axis=1)
    s2 = jnp.concatenate([jnp.zeros((lp, 64), F32), z16, sin, jnp.zeros((lp, 32), F32)], axis=1)
    return c, s1, s2


def _local_step(h0, tgt, norm_g, final_g, gq, gkv, sinks, w_ie, late_shards, nbatch, nb):
    lp = nb * BLK
    tabs = _rope_tables(lp)
    g0, g1 = norm_g[0:1], norm_g[1:2]
    sinks2 = sinks.reshape(2, 8)

    hn0 = _rms_fwd(h0, g0, "rms_fwd0")
    proj_e = _mm(hn0, w_ie, "nn", "proj_even", out_dtype=BF16)
    o_sb, og_sb, cm, gathered = _sb_fwd(proj_e, [late_shards[name] for name in _LATE], nbatch, nb)
    full = {name: _unchunk(name, blk) for name, blk in zip(_LATE, gathered)}
    w_uq, w_ukv, w_oe = _uq_to_compute(full["ev_w_uq"]), full["ev_w_ukv"], full["ev_w_out"]
    w_io, w_oo = _od_in_to_compute(full["od_w_in"]), full["od_w_out"]
    cqn, ckvn, krot = _mla_prep(proj_e, gq, gkv, tabs, nbatch, nb)
    qh = _mm(cqn, w_uq, "nn", "mla_uq", out_dtype=BF16)
    kvh = _mm(ckvn, w_ukv, "nn", "mla_ukv", out_dtype=BF16)
    o_mla, og_mla, lse_m = _mla_fwd(qh, kvh, krot, proj_e, tabs, nbatch, nb)
    h1 = _mm([og_sb, og_mla], w_oe, "nn", "out_even", add=h0)
    hn1 = _rms_fwd(h1, g1, "rms_fwd1")
    proj_o = _mm(hn1, w_io, "nn", "proj_odd", out_dtype=BF16)
    o_o, og_o, lse_o = _swa_fwd(proj_o, sinks2, nbatch, nb)
    h2 = _mm(og_o, w_oo, "nn", "out_odd", add=h1)
    dh2, lossv, d_final_g = _final(h2, tgt, final_g, nbatch, nb)

    dog_o = _mm(dh2, w_oo, "nt", "d_out_odd")
    d_w_oo, = _mm_tn(og_o, [dh2], "dw_out_odd")
    dq_o, dg_o, dk_o, dv_o, dsink = _swa_bwd(proj_o, sinks2, o_o, dog_o, lse_o, nbatch, nb)
    dproj_o = [dq_o, dg_o, dk_o, dv_o]
    dhn1 = _mm(dproj_o, w_io, "nt", "d_proj_odd")
    dw_q, dw_g, dw_k, dw_v = _mm_tn(hn1, dproj_o, "dw_proj_odd")
    dh1, d_g1 = _rms_bwd(h1, g1, dhn1, dh2, "rms_bwd1")

    dog_e = _mm(dh1, w_oe, "nt", "d_out_even")
    d_w_oe_sb, = _mm_tn(og_sb, [dh1], "dw_out_even_sb")
    d_w_oe_mla, = _mm_tn(og_mla, [dh1], "dw_out_even_mla")
    early = dict(od_w_in=jnp.concatenate([dw_q, dw_k, dw_v, dw_g], axis=1), od_w_out=d_w_oo,
                 ev_w_out=jnp.concatenate([d_w_oe_sb, d_w_oe_mla], axis=0))
    dq_sb, dk_sb, dv_sb, dg_sb, received = _sb_bwd(
        proj_e, o_sb, dog_e, cm, [_chunk(name, early[name]) for name in _EARLY_GRADS], nbatch, nb)
    dqh, dkvh, dkrot, dg_mla = _mla_bwd(qh, kvh, krot, proj_e, tabs, o_mla, dog_e, lse_m, nbatch, nb)
    dcqn = _mm(dqh, w_uq, "nt", "d_mla_uq")
    d_w_uq, = _mm_tn(cqn, [dqh], "dw_mla_uq")
    dckvn = _mm(dkvh, w_ukv, "nt", "d_mla_ukv")
    d_w_ukv, = _mm_tn(ckvn, [dkvh], "dw_mla_ukv")
    dmid, d_gq, d_gkv = _mla_prep_bwd(proj_e, gq, gkv, tabs, dcqn, dckvn, dkrot, nbatch, nb)
    dproj_e = [dq_sb, dk_sb, dv_sb, dg_sb, dmid, dg_mla]
    dhn0 = _mm(dproj_e, w_ie, "nt", "d_proj_even")
    dw_e = _mm_tn(hn0, dproj_e, "dw_proj_even")
    dh0, d_g0 = _rms_bwd(h0, g0, dhn0, dh1, "rms_bwd0")

    d_sinks = dsink.reshape(nbatch, 2, 8, BLK)[:, :, :, 0].sum(axis=0).reshape(1, 16)
    d_norm_g = jnp.concatenate([d_g0, d_g1], axis=0)
    d_ev_w_in = jnp.concatenate(list(dw_e[:4]) + [dw_e[4][:, :384], dw_e[4][:, 448:480], dw_e[5]], axis=1)
    return dict(lossv=lossv, dh0=dh0, norm_g=d_norm_g, final_g=d_final_g, gq=d_gq, gkv=d_gkv, sinks=d_sinks,
                ev_w_in=d_ev_w_in, ev_w_uq=_uq_from_compute(d_w_uq), ev_w_ukv=d_w_ukv, received=received)


def _ev_in_to_compute(w):
    z = lambda n: jnp.zeros((w.shape[0], n), w.dtype)
    return jnp.concatenate([w[:, :2432], z(64), w[:, 2432:2464], z(32), w[:, 2464:]], axis=1)


def _uq_to_compute(w):
    w3 = w.reshape(256, 8, 96)
    return jnp.concatenate([w3, jnp.zeros((256, 8, 32), w.dtype)], axis=2).reshape(256, 1024)


def _uq_from_compute(w):
    return w.reshape(256, 8, 128)[:, :, :96].reshape(256, 768)


def _od_in_to_compute(w):
    return jnp.concatenate([w[:, :1024], w[:, 1280:], w[:, 1024:1280]], axis=1)


_BIG = dict(ev_w_in=(1024, 2976, 1), ev_w_uq=(256, 768, 1), ev_w_ukv=(128, 1024, 1),
            ev_w_out=(1024, 1024, 0), od_w_in=(1024, 2304, 1), od_w_out=(1024, 1024, 0))
_LATE = ("ev_w_uq", "ev_w_ukv", "ev_w_out", "od_w_in", "od_w_out")
_EARLY_GRADS = ("od_w_in", "od_w_out", "ev_w_out")
_LAST_GRADS = ("ev_w_in", "ev_w_uq", "ev_w_ukv")


def _unchunk(name, blk):
    rows, cols, axis = _BIG[name]
    return blk.transpose(1, 0, 2).reshape(rows, cols) if axis == 1 else blk.reshape(rows, cols)


def _chunk(name, g):
    rows, cols, axis = _BIG[name]
    g = g.astype(BF16)
    return g.reshape(rows, N_DEV, cols // N_DEV).transpose(1, 0, 2) if axis == 1 else g.reshape(N_DEV, rows // N_DEV, cols)


def _all_gather(shards, name):
    n = len(shards)

    def body(*refs):
        xs, outs = refs[:n], refs[n:2 * n]
        send_sems, recv_sems, local_sems = refs[2 * n:]
        x, y, c = lax.axis_index("x"), lax.axis_index("y"), lax.axis_index("c")
        me, sibling = (x, y, c), (x, y, 1 - c)
        chips = [(1 - x, y), (x, 1 - y), (1 - x, 1 - y)]
        arrays = range(n)

        def copy(k, a, block, to, from_input=False):
            px, py, pc = block
            dst = outs[a].at[4 * px + 2 * py + pc]
            return pltpu.make_async_remote_copy(
                src_ref=xs[a] if from_input else dst, dst_ref=dst,
                send_sem=send_sems.at[k, a], recv_sem=recv_sems.at[k, a],
                device_id=to, device_id_type=pl.DeviceIdType.MESH)

        mine = [pltpu.make_async_copy(xs[a], outs[a].at[4 * x + 2 * y + c], local_sems.at[a]) for a in arrays]
        for cp in mine:
            cp.start()
        first = [copy(0, a, me, sibling, True) for a in arrays]
        for j, chip in enumerate(chips):
            first += [copy(1 + j, a, me, (*chip, c), True) for a in arrays]
        for cp in first:
            cp.start()
        passed = []
        for j, chip in enumerate(chips):
            for a in arrays:
                copy(1 + j, a, (*chip, c), me).wait_recv()
                passed.append(copy(4 + j, a, (*chip, c), sibling))
                passed[-1].start()
        for a in arrays:
            copy(0, a, sibling, me).wait_recv()
        for j, chip in enumerate(chips):
            for a in arrays:
                copy(4 + j, a, (*chip, 1 - c), me).wait_recv()
        for cp in first + passed:
            cp.wait_send()
        for cp in mine:
            cp.wait()

    hbm = pl.BlockSpec(memory_space=pl.ANY)
    return pl.pallas_call(
        body, name=name,
        out_shape=[jax.ShapeDtypeStruct((N_DEV,) + s.shape, s.dtype) for s in shards],
        in_specs=[hbm] * n, out_specs=[hbm] * n,
        scratch_shapes=[pltpu.SemaphoreType.DMA((7, n)), pltpu.SemaphoreType.DMA((7, n)),
                        pltpu.SemaphoreType.DMA((n,))],
    )(*shards)


def _exchange_sum(chunked, received, name):
    n, m = len(chunked), len(received)
    arrs = list(chunked) + list(received)

    def body(*refs):
        ins, outs = refs[:n + m], refs[n + m:2 * (n + m)]
        bufs = refs[2 * (n + m):3 * (n + m)]
        send_sems, recv_sems, local_sems, load_sems = refs[3 * (n + m):]
        loads = [pltpu.make_async_copy(ins[n + a], bufs[n + a], load_sems.at[a]) for a in range(m)]
        for cp in loads:
            cp.start()
        local, sends, recvs = _direct_exchange(ins[:n], bufs[:n], send_sems, recv_sems, local_sems, chunked=True)
        for cp in local + sends:
            cp.start()
        for a, cp in enumerate(loads):
            cp.wait()
            _sum_slots(bufs[n + a], outs[n + a])
        for cp in local:
            cp.wait()
        for cp in recvs:
            cp.wait_recv()
        for cp in sends:
            cp.wait_send()
        for a in range(n):
            _sum_slots(bufs[a], outs[a])

    hbm = pl.BlockSpec(memory_space=pl.ANY)
    vm = pl.BlockSpec(memory_space=pltpu.VMEM)
    return pl.pallas_call(
        body, name=name,
        out_shape=[jax.ShapeDtypeStruct(a.shape[1:], F32) for a in arrs],
        in_specs=[hbm] * (n + m), out_specs=[vm] * (n + m),
        scratch_shapes=[pltpu.VMEM(a.shape, a.dtype) for a in arrs] + _exchange_sems(n)
        + [pltpu.SemaphoreType.DMA((max(m, 1),))],
        compiler_params=pltpu.CompilerParams(vmem_limit_bytes=48 << 20),
    )(*arrs)


def _sum_slots(buf, out):
    rows = buf.shape[1]

    def add(sl):
        acc = buf[(0,) + sl].astype(F32)
        for k in range(1, N_DEV):
            acc = acc + buf[(k,) + sl].astype(F32)
        out[sl] = acc

    if rows > BLK and rows % BLK == 0:
        def step(r, carry):
            add((pl.ds(pl.multiple_of(r * BLK, BLK), BLK), slice(None)))
            return carry

        lax.fori_loop(0, rows // BLK, step, 0)
    else:
        add((slice(None), slice(None)))


def _adamw(ws, gs, ms, vs):
    n = len(ws)

    def body(*refs):
        ins, outs = refs[:4 * n], refs[4 * n:]
        for k in range(n):
            w_ref, g_ref, m_ref, v_ref = ins[4 * k:4 * k + 4]
            d_ref, nm_ref, nv_ref = outs[3 * k:3 * k + 3]

            def update(sl, w_ref=w_ref, g_ref=g_ref, m_ref=m_ref, v_ref=v_ref,
                       d_ref=d_ref, nm_ref=nm_ref, nv_ref=nv_ref):
                g = g_ref[sl]
                m = ADAM_B1 * m_ref[sl] + (1.0 - ADAM_B1) * g
                v = ADAM_B2 * v_ref[sl] + (1.0 - ADAM_B2) * (g * g)
                m_hat = m / (1.0 - ADAM_B1 ** ADAM_STEP)
                v_hat = v / (1.0 - ADAM_B2 ** ADAM_STEP)
                d_ref[sl] = -ADAM_LR * (m_hat / (jnp.sqrt(v_hat) + ADAM_EPS) + ADAM_WD * w_ref[sl])
                nm_ref[sl] = m
                nv_ref[sl] = v

            rows = w_ref.shape[0]
            if rows > BLK and rows % BLK == 0:
                def step(r, carry, update=update):
                    update((pl.ds(pl.multiple_of(r * BLK, BLK), BLK), slice(None)))
                    return carry

                lax.fori_loop(0, rows // BLK, step, 0)
            else:
                update((slice(None), slice(None)))

    args, out_shape = [], []
    for k in range(n):
        args += [ws[k], gs[k], ms[k], vs[k]]
        out_shape += [jax.ShapeDtypeStruct(ws[k].shape, F32)] * 3
    vm = pl.BlockSpec(memory_space=pltpu.VMEM)
    outs = pl.pallas_call(
        body, name="adamw", out_shape=out_shape,
        in_specs=[vm] * (4 * n), out_specs=[vm] * (3 * n),
        compiler_params=pltpu.CompilerParams(vmem_limit_bytes=48 << 20),
    )(*args)
    return [tuple(outs[3 * k:3 * k + 3]) for k in range(n)]


def kernel(x, meta, norm_g, final_g, ev_w_in, ev_q_norm_g, ev_kv_norm_g, ev_w_uq, ev_w_ukv, ev_w_out, od_w_in, od_sinks, od_w_out, loss_target, m_meta, m_norm_g, m_final_g, m_ev_w_in, m_ev_q_norm_g, m_ev_kv_norm_g, m_ev_w_uq, m_ev_w_ukv, m_ev_w_out, m_od_w_in, m_od_sinks, m_od_w_out, v_meta, v_norm_g, v_final_g, v_ev_w_in, v_ev_q_norm_g, v_ev_kv_norm_g, v_ev_w_uq, v_ev_w_ukv, v_ev_w_out, v_od_w_in, v_od_sinks, v_od_w_out):
    nbatch, seq, d = x.shape
    nb = seq // BLK + 1
    lp = nb * BLK
    shards = dict(ev_w_in=ev_w_in[0], ev_w_uq=ev_w_uq[0], ev_w_ukv=ev_w_ukv[0], ev_w_out=ev_w_out[0],
                  od_w_in=od_w_in[0], od_w_out=od_w_out[0])

    w_ie_blocks, meta_blocks = _all_gather([shards["ev_w_in"].astype(BF16), meta], "gather_weights")
    meta_full = meta_blocks.transpose(1, 0, 2).reshape(N_META, d)

    head = jnp.concatenate([jnp.zeros((N_PAD, d), F32), meta_full], axis=0)
    h0 = jnp.concatenate([jnp.broadcast_to(head[None], (nbatch, BLK, d)), x], axis=1).reshape(nbatch * lp, d)
    grads = _local_step(
        h0, loss_target.reshape(nbatch * seq, d), norm_g, final_g.reshape(1, d), ev_q_norm_g, ev_kv_norm_g,
        od_sinks, _ev_in_to_compute(_unchunk("ev_w_in", w_ie_blocks)),
        {name: shards[name].astype(BF16) for name in _LATE}, nbatch, nb)
    dh0 = grads["dh0"].reshape(nbatch, lp, d)
    grad_x = dh0[:, BLK:]
    loss = lax.psum(0.5 / d * jnp.sum(grads["lossv"]), ("x", "y", "c"))

    d_meta = dh0[:, N_PAD:BLK].sum(axis=0).reshape(N_META, N_DEV, BLK).transpose(1, 0, 2)
    pad = lambda a, n: jnp.concatenate([a.reshape(1, -1), jnp.zeros((1, n - a.size), F32)], axis=1)
    rep = jnp.concatenate([grads["norm_g"].reshape(1, -1), grads["final_g"], pad(grads["gq"], 512),
                           pad(grads["gkv"], 256), pad(grads["sinks"], 256)], axis=1).reshape(32, BLK)
    small = jnp.concatenate([d_meta, jnp.broadcast_to(rep[None], (N_DEV, 32, BLK))], axis=1)
    reduced = _exchange_sum([_chunk(name, grads[name]) for name in _LAST_GRADS] + [small],
                            grads["received"], "reduce_grads")
    g_shard = dict(zip(_LAST_GRADS + ("small",) + _EARLY_GRADS, reduced))
    red_small = g_shard.pop("small")
    rep = red_small[N_META:].reshape(1, -1)
    g_small = dict(meta=red_small[:N_META], norm_g=rep[:, :2 * d].reshape(2, d), final_g=rep[:, 2 * d:3 * d],
                   ev_q_norm_g=rep[:, 3 * d:3 * d + 256], ev_kv_norm_g=rep[:, 3 * d + 512:3 * d + 640],
                   od_sinks=rep[:, 3 * d + 768:3 * d + 784])

    names = ["meta", "norm_g", "final_g", "ev_w_in", "ev_q_norm_g", "ev_kv_norm_g", "ev_w_uq", "ev_w_ukv",
             "ev_w_out", "od_w_in", "od_sinks", "od_w_out"]
    given = dict(meta=(meta, m_meta, v_meta), norm_g=(norm_g, m_norm_g, v_norm_g),
                 final_g=(final_g, m_final_g, v_final_g), ev_w_in=(ev_w_in, m_ev_w_in, v_ev_w_in),
                 ev_q_norm_g=(ev_q_norm_g, m_ev_q_norm_g, v_ev_q_norm_g),
                 ev_kv_norm_g=(ev_kv_norm_g, m_ev_kv_norm_g, v_ev_kv_norm_g),
                 ev_w_uq=(ev_w_uq, m_ev_w_uq, v_ev_w_uq), ev_w_ukv=(ev_w_ukv, m_ev_w_ukv, v_ev_w_ukv),
                 ev_w_out=(ev_w_out, m_ev_w_out, v_ev_w_out), od_w_in=(od_w_in, m_od_w_in, v_od_w_in),
                 od_sinks=(od_sinks, m_od_sinks, v_od_sinks), od_w_out=(od_w_out, m_od_w_out, v_od_w_out))
    ws, gs, ms, vs = [], [], [], []
    for name in names:
        g2 = g_shard[name] if name in g_shard else g_small[name]
        w, m, v = given[name]
        ws.append(w.reshape(g2.shape))
        ms.append(m.reshape(g2.shape))
        vs.append(v.reshape(g2.shape))
        gs.append(g2)
    upd = _adamw(ws, gs, ms, vs)
    shape_of = {name: given[name][0].shape for name in names}
    grads_out = [gs[k].reshape(shape_of[n]) for k, n in enumerate(names)]
    deltas = [upd[k][0].reshape(shape_of[n]) for k, n in enumerate(names)]
    new_m = [upd[k][1].reshape(shape_of[n]) for k, n in enumerate(names)]
    new_v = [upd[k][2].reshape(shape_of[n]) for k, n in enumerate(names)]
    return (loss, grad_x, *grads_out, *deltas, *new_m, *new_v)
```

```python
import jax
import jax.numpy as jnp
from jax import lax
from jax.experimental import pallas as pl
from jax.experimental.pallas import tpu as pltpu

F32 = jnp.float32
BF16 = jnp.bfloat16

D_MODEL = 1024
N_META = 16
BLK = 128
HALF = 64
N_PAD = BLK - N_META
NORM_EPS = 1e-6
NEG = -1e30
N_DEV = 8

SB_SCALE = 64 ** -0.5
MLA_SCALE = 96 ** -0.5
SWA_SCALE = 64 ** -0.5
ROPE_BASE = 10000.0

EV_IN_PAD = 3072
EV_MID_BLK = 4
EV_GMLA_BLK = 20
OD_K_BLK = 16
OD_V_BLK = 17

ADAM_LR = 0.001
ADAM_B1 = 0.9
ADAM_B2 = 0.999
ADAM_EPS = 1e-08
ADAM_WD = 0.01
ADAM_STEP = 10


def _dot(a, b):
    return lax.dot_general(a, b, (((1,), (0,)), ((), ())), preferred_element_type=F32)


def _dot_nt(a, b):
    return lax.dot_general(a, b, (((1,), (1,)), ((), ())), preferred_element_type=F32)


def _dot_tn(a, b):
    return lax.dot_general(a, b, (((0,), (0,)), ((), ())), preferred_element_type=F32)


def _split(x):
    hi = x.astype(BF16)
    return hi, (x - hi.astype(F32)).astype(BF16)


def _sigmoid(x):
    return 1.0 / (1.0 + jnp.exp(-x))


def _iotas():
    row = lax.broadcasted_iota(jnp.int32, (BLK, BLK), 0)
    lane = lax.broadcasted_iota(jnp.int32, (BLK, BLK), 1)
    return row, lane


WIDE = 2 * BLK


def _key_mask(i, first_key, width, strict):
    t_pos = i * BLK + lax.broadcasted_iota(jnp.int32, (BLK, width), 0)
    s_pos = first_key + lax.broadcasted_iota(jnp.int32, (BLK, width), 1)
    seen = (s_pos < t_pos) if strict else (s_pos <= t_pos)
    return seen & (s_pos >= N_PAD)


def _widen(x, width):
    return x if width == BLK else jnp.concatenate([x] * (width // BLK), axis=1)


def _over_key_blocks(n, block, reverse):
    pairs = n // 2
    last = pl.multiple_of((n - 1) * BLK, BLK)

    def step(jj, carry):
        jp = (pairs - 1 - jj) if reverse else jj
        block(pl.multiple_of(jp * WIDE, WIDE), WIDE)
        return carry

    if reverse:
        pl.when(n % 2 == 1)(lambda: block(last, BLK))
        lax.fori_loop(0, pairs, step, 0)
    else:
        lax.fori_loop(0, pairs, step, 0)
        pl.when(n % 2 == 1)(lambda: block(last, BLK))


def _rope(x, c, s1, s2):
    return x * c + pltpu.roll(x, BLK - 16, 1) * s1 + pltpu.roll(x, 16, 1) * s2


def _rope_t(x, c, s1, s2):
    return x * c - pltpu.roll(x, BLK - 16, 1) * s1 - pltpu.roll(x, 16, 1) * s2


def _params(sem, vmem_mb=None):
    kw = dict(dimension_semantics=sem)
    if vmem_mb is not None:
        kw["vmem_limit_bytes"] = vmem_mb << 20
    return pltpu.CompilerParams(**kw)


def _row_tile(t, cands):
    for c in cands:
        if t % c == 0:
            return c
    raise ValueError(t)


def _mm(a, w, mode, name, add=None, out_dtype=F32):
    pieces = list(a) if isinstance(a, (list, tuple)) else [a]
    m = pieces[0].shape[0]
    n = w.shape[1] if mode == "nn" else w.shape[0]
    tm = _row_tile(m, (256, 128))
    widths = [p.shape[1] for p in pieces]
    offs = [sum(widths[:i]) for i in range(len(widths))]

    def body(*refs):
        p_refs = refs[:len(pieces)]
        w_ref = refs[len(pieces)]
        o_ref = refs[-1]
        acc = None
        for p_ref, off, wd in zip(p_refs, offs, widths):
            x = p_ref[...].astype(BF16)
            part = _dot(x, w_ref[off:off + wd, :]) if mode == "nn" else _dot_nt(x, w_ref[:, off:off + wd])
            acc = part if acc is None else acc + part
        if add is not None:
            acc = acc + refs[len(pieces) + 1][...]
        o_ref[...] = acc.astype(o_ref.dtype)

    in_specs = [pl.BlockSpec((tm, wd), lambda i: (i, 0)) for wd in widths]
    in_specs.append(pl.BlockSpec(w.shape, lambda i: (0, 0)))
    args = pieces + [w]
    if add is not None:
        in_specs.append(pl.BlockSpec((tm, n), lambda i: (i, 0)))
        args.append(add)
    return pl.pallas_call(
        body, name=name, grid=(m // tm,), in_specs=in_specs,
        out_specs=pl.BlockSpec((tm, n), lambda i: (i, 0)),
        out_shape=jax.ShapeDtypeStruct((m, n), out_dtype),
        compiler_params=_params(("parallel",), 48),
    )(*args)


def _mm_tn(x, pieces, name):
    t, k = x.shape
    tt = _row_tile(t, (544, 256, 128))
    widths = [p.shape[1] for p in pieces]

    def body(*refs):
        x_ref = refs[0]
        d_refs = refs[1:1 + len(pieces)]
        o_refs = refs[1 + len(pieces):]

        @pl.when(pl.program_id(0) == 0)
        def _():
            for o_ref in o_refs:
                o_ref[...] = jnp.zeros_like(o_ref)

        xb = x_ref[...].astype(BF16)
        for d_ref, o_ref in zip(d_refs, o_refs):
            o_ref[...] += _dot_tn(xb, d_ref[...].astype(BF16))

    return pl.pallas_call(
        body, name=name, grid=(t // tt,),
        in_specs=[pl.BlockSpec((tt, k), lambda i: (i, 0))] + [pl.BlockSpec((tt, wd), lambda i: (i, 0)) for wd in widths],
        out_specs=[pl.BlockSpec((k, wd), lambda i: (0, 0)) for wd in widths],
        out_shape=[jax.ShapeDtypeStruct((k, wd), F32) for wd in widths],
        compiler_params=_params(("arbitrary",), 56),
    )(x, *pieces)


def _norm_mm(h, g, w, name):
    t, d = h.shape
    n = w.shape[1]
    tr = _row_tile(t, (256, 128))

    def body(h_ref, g_ref, w_ref, o_ref, hn_ref):
        x = h_ref[...]
        r = lax.rsqrt(jnp.mean(x * x, axis=1, keepdims=True) + NORM_EPS)
        hn = (x * r * g_ref[...]).astype(BF16)
        hn_ref[...] = hn
        o_ref[...] = _dot(hn, w_ref[...]).astype(o_ref.dtype)

    row = lambda width: pl.BlockSpec((tr, width), lambda i: (i, 0))
    return pl.pallas_call(
        body, name=name, grid=(t // tr,),
        in_specs=[row(d), pl.BlockSpec((1, d), lambda i: (0, 0)), pl.BlockSpec(w.shape, lambda i: (0, 0))],
        out_specs=[row(n), row(d)],
        out_shape=[jax.ShapeDtypeStruct((t, n), BF16), jax.ShapeDtypeStruct((t, d), BF16)],
        compiler_params=_params(("parallel",), 48),
    )(h, g, w)


def _mm_norm_bwd(pieces, w, h, g, dres, name):
    t, d = h.shape
    tr = _row_tile(t, (256, 128))
    widths = [p.shape[1] for p in pieces]
    offs = [sum(widths[:i]) for i in range(len(widths))]

    def body(*refs):
        p_refs = refs[:len(pieces)]
        w_ref, h_ref, g_ref, dres_ref, dh_ref, dg_ref = refs[len(pieces):]

        @pl.when(pl.program_id(0) == 0)
        def _():
            dg_ref[...] = jnp.zeros_like(dg_ref)

        dy = None
        for p_ref, off, wd in zip(p_refs, offs, widths):
            part = _dot_nt(p_ref[...].astype(BF16), w_ref[:, off:off + wd])
            dy = part if dy is None else dy + part
        x = h_ref[...]
        r = lax.rsqrt(jnp.mean(x * x, axis=1, keepdims=True) + NORM_EPS)
        nx = x * r
        dn = dy * g_ref[...]
        dh_ref[...] = dres_ref[...] + r * (dn - nx * jnp.mean(dn * nx, axis=1, keepdims=True))
        dg_ref[...] += jnp.sum(dy * nx, axis=0, keepdims=True)

    row = lambda width: pl.BlockSpec((tr, width), lambda i: (i, 0))
    vec = pl.BlockSpec((1, d), lambda i: (0, 0))
    return pl.pallas_call(
        body, name=name, grid=(t // tr,),
        in_specs=[row(wd) for wd in widths] + [pl.BlockSpec(w.shape, lambda i: (0, 0)), row(d), vec, row(d)],
        out_specs=[row(d), vec],
        out_shape=[jax.ShapeDtypeStruct((t, d), F32), jax.ShapeDtypeStruct((1, d), F32)],
        compiler_params=_params(("arbitrary",), 48),
    )(*pieces, w, h, g, dres)


def _final(h2, tgt, g, nbatch, nb):
    t, d = h2.shape

    def body(h_ref, t_ref, g_ref, dh_ref, loss_ref, dg_ref):
        b = pl.program_id(0)
        i = pl.program_id(1)

        @pl.when((b == 0) & (i == 0))
        def _():
            loss_ref[...] = jnp.zeros_like(loss_ref)
            dg_ref[...] = jnp.zeros_like(dg_ref)

        x = h_ref[...]
        r = lax.rsqrt(jnp.mean(x * x, axis=1, keepdims=True) + NORM_EPS)
        nx = x * r
        gg = g_ref[...]
        live = jnp.where(i >= 1, 1.0, 0.0)
        err = (nx * gg - t_ref[...]) * live
        loss_ref[...] += jnp.sum(err * err, axis=0, keepdims=True)
        dy = err * (1.0 / d)
        dn = dy * gg
        dh_ref[...] = r * (dn - nx * jnp.mean(dn * nx, axis=1, keepdims=True))
        dg_ref[...] += jnp.sum(dy * nx, axis=0, keepdims=True)

    vec = pl.BlockSpec((1, d), lambda b, i: (0, 0))
    return pl.pallas_call(
        body, name="final_loss", grid=(nbatch, nb),
        in_specs=[pl.BlockSpec((BLK, d), lambda b, i: (b * nb + i, 0)),
                  pl.BlockSpec((BLK, d), lambda b, i: (b * (nb - 1) + jnp.maximum(i - 1, 0), 0)),
                  vec],
        out_specs=[pl.BlockSpec((BLK, d), lambda b, i: (b * nb + i, 0)), vec, vec],
        out_shape=[jax.ShapeDtypeStruct((t, d), F32), jax.ShapeDtypeStruct((1, d), F32),
                   jax.ShapeDtypeStruct((1, d), F32)],
        compiler_params=_params(("arbitrary", "arbitrary")),
    )(h2, tgt, g)


def _direct_exchange(srcs, dsts, send_sems, recv_sems, local_sems, chunked):
    x, y, c = lax.axis_index("x"), lax.axis_index("y"), lax.axis_index("c")
    me = 4 * x + 2 * y + c
    arrays = range(len(srcs))
    local = [pltpu.make_async_copy(srcs[a].at[me] if chunked else srcs[a], dsts[a].at[me], local_sems.at[a])
             for a in arrays]
    sends, recvs = [], []
    for d in range(1, N_DEV):
        px = x + ((d >> 2) & 1) - 2 * x * ((d >> 2) & 1)
        py = y + ((d >> 1) & 1) - 2 * y * ((d >> 1) & 1)
        pc = c + (d & 1) - 2 * c * (d & 1)
        pid = 4 * px + 2 * py + pc
        for a in arrays:
            kw = dict(send_sem=send_sems.at[d - 1, a], recv_sem=recv_sems.at[d - 1, a],
                      device_id=(px, py, pc), device_id_type=pl.DeviceIdType.MESH)
            src = srcs[a].at[pid] if chunked else srcs[a]
            sends.append(pltpu.make_async_remote_copy(src_ref=src, dst_ref=dsts[a].at[me], **kw))
            recvs.append(pltpu.make_async_remote_copy(src_ref=src, dst_ref=dsts[a].at[pid], **kw))
    return local, sends, recvs


def _exchange_beside(first, last, srcs, dsts, sems, chunked):
    local, sends, recvs = _direct_exchange(srcs, dsts, *sems, chunked)

    @pl.when(first)
    def _():
        for cp in local + sends:
            cp.start()

    @pl.when(last)
    def _():
        for cp in local:
            cp.wait()
        for cp in recvs:
            cp.wait_recv()
        for cp in sends:
            cp.wait_send()


def _exchange_sems(n):
    return [pltpu.SemaphoreType.DMA((7, n)), pltpu.SemaphoreType.DMA((7, n)), pltpu.SemaphoreType.DMA((n,))]


def _sb_logits(z):
    log_beta = jnp.minimum(z, 0.0) - jnp.log(1.0 + jnp.exp(-jnp.abs(z)))
    return log_beta, log_beta - z


def _tri(width, after):
    j = lax.broadcasted_iota(jnp.int32, (width, width), 0)
    s = lax.broadcasted_iota(jnp.int32, (width, width), 1)
    return (j > s) if after else (j < s)


def _tri2(tri, with_ones):
    m = tri.astype(BF16)
    if with_ones:
        m = jnp.concatenate([m, jnp.ones((tri.shape[0], BLK), BF16)], axis=1)
    return jnp.concatenate([m, m], axis=0)


def _block_sums(x, tri_ones, after):
    hi, lo = _split(x)
    subs = [_dot(jnp.concatenate([hi[:, s:s + BLK], lo[:, s:s + BLK]], axis=1), tri_ones)
            for s in range(0, x.shape[1], BLK)]
    if len(subs) == 1:
        return subs[0][:, :BLK], subs[0][:, BLK:]
    first, second = subs
    total = first[:, BLK:] + second[:, BLK:]
    if after:
        return jnp.concatenate([first[:, :BLK] + second[:, BLK:], second[:, :BLK]], axis=1), total
    return jnp.concatenate([first[:, :BLK], second[:, :BLK] + first[:, BLK:]], axis=1), total


def _head_masked(x, lane, scale=None):
    out = []
    for h in range(8):
        xp = x[:, (h // 2) * BLK:(h // 2 + 1) * BLK]
        xm = jnp.where((lane >= HALF) if h % 2 else (lane < HALF), xp, jnp.zeros_like(xp))
        out.append(xm if scale is None else xm * scale)
    return jnp.concatenate(out, axis=1)


def _sb_fwd(proj, shards, nbatch, nb):
    lp = nb * BLK
    t = nbatch * lp

    ns = len(shards)

    def body(*refs):
        q_ref, k_ref, v_ref, g_ref = refs[:4]
        shard_refs = refs[4:4 + ns]
        o_ref, og_ref, cm_ref = refs[4 + ns:7 + ns]
        gathered_refs = refs[7 + ns:7 + 2 * ns]
        c_scr, qm_scr = refs[7 + 2 * ns:9 + 2 * ns]
        b = pl.program_id(0)
        i = pl.program_id(1)
        _exchange_beside((b == 0) & (i == 0), (b == nbatch - 1) & (i == nb - 1),
                         shard_refs, gathered_refs, refs[9 + 2 * ns:], chunked=False)
        _, lane = _iotas()
        lo_m = lane < HALF
        cm_ref[...] = jnp.zeros_like(cm_ref)
        c_scr[...] = jnp.zeros_like(c_scr)
        o_ref[...] = jnp.zeros_like(o_ref)
        qm_scr[...] = _head_masked(q_ref[...], lane, SB_SCALE)

        def block(off, width):
            mask = _key_mask(i, off, width, strict=True)
            upper = _tri2(_tri(BLK, after=True), True)
            onehot = lane == off // WIDE
            heads = range(8)
            hcs = [slice(h * BLK, (h + 1) * BLK) for h in heads]
            kbs = [k_ref[pl.ds(off, width), hc] for hc in hcs[:4]]
            vbs = [v_ref[pl.ds(off, width), hc] for hc in hcs[:4]]
            zs = [_dot_nt(qm_scr[:, hcs[h]], kbs[h // 2]) for h in heads]
            lbs, l1s = [], []
            for h in heads:
                log_beta, log_1m = _sb_logits(zs[h])
                lbs.append(log_beta)
                l1s.append(jnp.where(mask, log_1m, 0.0))
            css = [_block_sums(l1s[h], upper, after=True) for h in heads]
            avs = []
            for h in heads:
                c = c_scr[h]
                avs.append(jnp.where(mask, jnp.exp(lbs[h] + css[h][0] + _widen(c, width)), 0.0).astype(BF16))
                if width == WIDE:
                    cm_ref[:, hcs[h]] = jnp.where(onehot, c, cm_ref[:, hcs[h]])
                c_scr[h] = c + css[h][1]
            accs = [_dot(avs[h], vbs[h // 2]) for h in heads]
            for p in range(4):
                o_ref[:, hcs[p]] += jnp.where(lo_m, accs[2 * p], accs[2 * p + 1])

        _over_key_blocks(i + 1, block, reverse=True)
        g = g_ref[...].astype(F32)
        og_ref[...] = (o_ref[...] * g * _sigmoid(g)).astype(og_ref.dtype)

    tile = lambda col: pl.BlockSpec((BLK, 512), lambda b, i: (b * nb + i, col))
    full = lambda col: pl.BlockSpec((lp, 512), lambda b, i: (b, col))
    hbm = pl.BlockSpec(memory_space=pl.ANY)
    outs = pl.pallas_call(
        body, name="sb_fwd", grid=(nbatch, nb),
        in_specs=[tile(0), full(1), full(2), tile(3)] + [hbm] * ns,
        out_specs=[tile(0), tile(0), pl.BlockSpec((BLK, 1024), lambda b, i: (b * nb + i, 0))] + [hbm] * ns,
        out_shape=[jax.ShapeDtypeStruct((t, 512), F32), jax.ShapeDtypeStruct((t, 512), BF16),
                   jax.ShapeDtypeStruct((t, 1024), F32)]
        + [jax.ShapeDtypeStruct((N_DEV,) + s.shape, s.dtype) for s in shards],
        scratch_shapes=[pltpu.VMEM((8, BLK, BLK), F32), pltpu.VMEM((BLK, 1024), BF16)] + _exchange_sems(ns),
        compiler_params=_params(("arbitrary", "arbitrary"), 48),
    )(proj, proj, proj, proj, *shards)
    return outs[0], outs[1], outs[2], list(outs[3:])


def _sb_bwd(proj, o, dog, cm, chunked, nbatch, nb):
    lp = nb * BLK
    t = nbatch * lp
    ns = len(chunked)

    def body(*refs):
        q_ref, k_ref, v_ref, g_ref, o_ref, dog_ref, cm_ref = refs[:7]
        chunk_refs = refs[7:7 + ns]
        dq_ref, dk_ref, dv_ref, dg_ref = refs[7 + ns:11 + ns]
        received_refs = refs[11 + ns:11 + 2 * ns]
        c_scr, qm_scr, dom_scr, dq_scr = refs[11 + 2 * ns:15 + 2 * ns]
        b = pl.program_id(0)
        i = pl.program_id(1)
        _exchange_beside((b == 0) & (i == 0), (b == nbatch - 1) & (i == nb - 1),
                         chunk_refs, received_refs, refs[15 + 2 * ns:], chunked=True)

        @pl.when(i == 0)
        def _():
            dk_ref[...] = jnp.zeros_like(dk_ref)
            dv_ref[...] = jnp.zeros_like(dv_ref)

        _, lane = _iotas()
        lo_m = lane < HALF
        g = g_ref[...].astype(F32)
        sig = _sigmoid(g)
        dog_v = dog_ref[...]
        dg_ref[...] = (dog_v * o_ref[...] * (sig * (1.0 + g * (1.0 - sig)))).astype(dg_ref.dtype)
        dom_scr[...] = _head_masked((dog_v * g * sig).astype(BF16), lane)
        qm_scr[...] = _head_masked(q_ref[...], lane, SB_SCALE)
        c_scr[...] = jnp.zeros_like(c_scr)
        dq_scr[...] = jnp.zeros_like(dq_scr)

        def block(off, width):
            mask = _key_mask(i, off, width, strict=True)
            upper = _tri2(_tri(BLK, after=True), True)
            lower = _tri2(_tri(BLK, after=False), True)
            onehot = lane == off // WIDE
            heads = range(8)
            hcs = [slice(h * BLK, (h + 1) * BLK) for h in heads]
            kbs = [k_ref[pl.ds(off, width), hc] for hc in hcs[:4]]
            vbs = [v_ref[pl.ds(off, width), hc] for hc in hcs[:4]]
            zs = [_dot_nt(qm_scr[:, hcs[h]], kbs[h // 2]) for h in heads]
            dps = [_dot_nt(dom_scr[:, hcs[h]], vbs[h // 2]) for h in heads]
            lbs, l1s = [], []
            for h in heads:
                log_beta, log_1m = _sb_logits(zs[h])
                lbs.append(log_beta)
                l1s.append(log_1m)
            sufs = [_block_sums(jnp.where(mask, l1s[h], 0.0), upper, after=True)[0] for h in heads]
            prs, dzzs = [], []
            for h in heads:
                expo = lbs[h] + sufs[h]
                if width == WIDE:
                    expo = expo + jnp.sum(jnp.where(onehot, cm_ref[:, hcs[h]], 0.0), axis=1, keepdims=True)
                pr = jnp.where(mask, jnp.exp(expo), 0.0)
                dzz = pr * dps[h]
                prs.append(pr.astype(BF16))
                dzzs.append(dzz)
            css = [_block_sums(dzzs[h], lower, after=False) for h in heads]
            dzbs = []
            for h in heads:
                c2 = c_scr[h]
                prefix = css[h][0] + _widen(c2, width)
                dz = jnp.where(mask, dzzs[h] * jnp.exp(l1s[h]) - jnp.exp(lbs[h]) * prefix, 0.0)
                dzbs.append(dz.astype(BF16))
                c_scr[h] = c2 + css[h][1]
            dqs = [_dot(dzbs[h], kbs[h // 2]) for h in heads]
            dks = [_dot_tn(dzbs[h], qm_scr[:, hcs[h]]) for h in heads]
            dvs = [_dot_tn(prs[h], dom_scr[:, hcs[h]]) for h in heads]
            for p in range(4):
                dq_scr[:, hcs[p]] += jnp.where(lo_m, dqs[2 * p], dqs[2 * p + 1])
                dk_ref[pl.ds(off, width), hcs[p]] += dks[2 * p] + dks[2 * p + 1]
                dv_ref[pl.ds(off, width), hcs[p]] += dvs[2 * p] + dvs[2 * p + 1]

        _over_key_blocks(i + 1, block, reverse=False)
        dq_ref[...] = (dq_scr[...] * SB_SCALE).astype(dq_ref.dtype)

    tile = lambda col: pl.BlockSpec((BLK, 512), lambda b, i: (b * nb + i, col))
    full = lambda col: pl.BlockSpec((lp, 512), lambda b, i: (b, col))
    acc = jax.ShapeDtypeStruct((t, 512), F32)
    once = jax.ShapeDtypeStruct((t, 512), BF16)
    hbm = pl.BlockSpec(memory_space=pl.ANY)
    outs = pl.pallas_call(
        body, name="sb_bwd", grid=(nbatch, nb),
        in_specs=[tile(0), full(1), full(2), tile(3), tile(0), tile(0),
                  pl.BlockSpec((BLK, 1024), lambda b, i: (b * nb + i, 0))] + [hbm] * ns,
        out_specs=[tile(0), full(0), full(0), tile(0)] + [hbm] * ns,
        out_shape=[once, acc, acc, once] + [jax.ShapeDtypeStruct(a.shape, a.dtype) for a in chunked],
        scratch_shapes=[pltpu.VMEM((8, BLK, BLK), F32), pltpu.VMEM((BLK, 1024), BF16),
                        pltpu.VMEM((BLK, 1024), BF16), pltpu.VMEM((BLK, 512), F32)] + _exchange_sems(ns),
        compiler_params=_params(("arbitrary", "arbitrary"), 56),
    )(proj, proj, proj, proj, o, dog, cm, *chunked)
    return outs[0], outs[1], outs[2], outs[3], list(outs[4:])


def _mla_prep(proj, gq, gkv, tabs, nbatch, nb):
    t = proj.shape[0]

    def body(mid_ref, gq_ref, gkv_ref, c_ref, s1_ref, s2_ref, cq_ref, ckv_ref, kr_ref):
        cq = mid_ref[:, 0:256].astype(F32)
        r = lax.rsqrt(jnp.mean(cq * cq, axis=1, keepdims=True) + NORM_EPS)
        cq_ref[...] = (cq * r * gq_ref[...]).astype(BF16)
        ckv = mid_ref[:, 256:384].astype(F32)
        r = lax.rsqrt(jnp.mean(ckv * ckv, axis=1, keepdims=True) + NORM_EPS)
        ckv_ref[...] = (ckv * r * gkv_ref[...]).astype(BF16)
        kr = mid_ref[:, 384:512].astype(F32)
        kr_ref[...] = _rope(kr, c_ref[...], s1_ref[...], s2_ref[...]).astype(BF16)

    tab = pl.BlockSpec((BLK, BLK), lambda b, i: (i, 0))
    rowspec = lambda w: pl.BlockSpec((BLK, w), lambda b, i: (b * nb + i, 0))
    return pl.pallas_call(
        body, name="mla_prep", grid=(nbatch, nb),
        in_specs=[pl.BlockSpec((BLK, 512), lambda b, i: (b * nb + i, EV_MID_BLK)),
                  pl.BlockSpec((1, 256), lambda b, i: (0, 0)), pl.BlockSpec((1, 128), lambda b, i: (0, 0)),
                  tab, tab, tab],
        out_specs=[rowspec(256), rowspec(128), rowspec(128)],
        out_shape=[jax.ShapeDtypeStruct((t, 256), BF16), jax.ShapeDtypeStruct((t, 128), BF16),
                   jax.ShapeDtypeStruct((t, 128), BF16)],
        compiler_params=_params(("parallel", "parallel")),
    )(proj, gq, gkv, *tabs)


def _mla_prep_bwd(proj, gq, gkv, tabs, dcqn, dckvn, dkrot, nbatch, nb):
    t = proj.shape[0]

    def body(mid_ref, gq_ref, gkv_ref, c_ref, s1_ref, s2_ref, dcq_ref, dckv_ref, dkr_ref,
             dmid_ref, dgq_ref, dgkv_ref):
        @pl.when((pl.program_id(0) == 0) & (pl.program_id(1) == 0))
        def _():
            dgq_ref[...] = jnp.zeros_like(dgq_ref)
            dgkv_ref[...] = jnp.zeros_like(dgkv_ref)

        def norm_bwd(x, gain, dy, dgain_ref):
            r = lax.rsqrt(jnp.mean(x * x, axis=1, keepdims=True) + NORM_EPS)
            nx = x * r
            dn = dy * gain
            dgain_ref[...] += jnp.sum(dy * nx, axis=0, keepdims=True)
            return r * (dn - nx * jnp.mean(dn * nx, axis=1, keepdims=True))

        dmid_ref[:, 0:256] = norm_bwd(
            mid_ref[:, 0:256].astype(F32), gq_ref[...], dcq_ref[...], dgq_ref).astype(BF16)
        dmid_ref[:, 256:384] = norm_bwd(
            mid_ref[:, 256:384].astype(F32), gkv_ref[...], dckv_ref[...], dgkv_ref).astype(BF16)
        dmid_ref[:, 384:512] = _rope_t(dkr_ref[...], c_ref[...], s1_ref[...], s2_ref[...]).astype(BF16)

    tab = pl.BlockSpec((BLK, BLK), lambda b, i: (i, 0))
    rowspec = lambda w: pl.BlockSpec((BLK, w), lambda b, i: (b * nb + i, 0))
    vq = pl.BlockSpec((1, 256), lambda b, i: (0, 0))
    vkv = pl.BlockSpec((1, 128), lambda b, i: (0, 0))
    return pl.pallas_call(
        body, name="mla_prep_bwd", grid=(nbatch, nb),
        in_specs=[pl.BlockSpec((BLK, 512), lambda b, i: (b * nb + i, EV_MID_BLK)), vq, vkv, tab, tab, tab,
                  rowspec(256), rowspec(128), rowspec(128)],
        out_specs=[rowspec(512), vq, vkv],
        out_shape=[jax.ShapeDtypeStruct((t, 512), BF16), jax.ShapeDtypeStruct((1, 256), F32),
                   jax.ShapeDtypeStruct((1, 128), F32)],
        compiler_params=_params(("arbitrary", "arbitrary")),
    )(proj, gq, gkv, *tabs, dcqn, dckvn, dkrot)


def _mla_scores(qf, kvb, krb, mask, lo_m):
    kf = jnp.where(lo_m, kvb, krb)
    return kf, jnp.where(mask, _dot_nt(qf, kf), NEG)


def _mla_fwd(qh, kvh, krot, proj, tabs, nbatch, nb):
    lp = nb * BLK
    t = nbatch * lp

    def body(q_ref, kv_ref, kr_ref, g_ref, c_ref, s1_ref, s2_ref, o_ref, og_ref, lse_ref,
             qf_scr, m_scr, l_scr, acc_scr):
        i = pl.program_id(1)
        row, lane = _iotas()
        lo_m = lane < HALF
        for h in range(8):
            hc = slice(h * BLK, (h + 1) * BLK)
            qf_scr[:, hc] = (_rope(q_ref[:, hc].astype(F32), c_ref[...], s1_ref[...], s2_ref[...])
                             * MLA_SCALE).astype(BF16)
        m_scr[...] = jnp.full(m_scr.shape, NEG, F32)
        l_scr[...] = jnp.zeros_like(l_scr)
        acc_scr[...] = jnp.zeros_like(acc_scr)

        def block(off, width):
            mask = _key_mask(i, off, width, strict=False)
            lo_k = lax.broadcasted_iota(jnp.int32, (width, BLK), 1) < HALF
            ones = jnp.ones((width, BLK), BF16)
            krb = kr_ref[pl.ds(off, width), :]
            heads = range(8)
            hcs = [slice(h * BLK, (h + 1) * BLK) for h in heads]
            kvbs = [kv_ref[pl.ds(off, width), hc] for hc in hcs]
            ss = [_mla_scores(qf_scr[:, hcs[h]], kvbs[h], krb, mask, lo_k)[1] for h in heads]
            ps, alphas = [], []
            for h in heads:
                m = m_scr[h]
                m2 = jnp.maximum(m, jnp.max(ss[h], axis=1, keepdims=True))
                ps.append(jnp.exp(ss[h] - _widen(m2, width)).astype(BF16))
                alphas.append(jnp.exp(m - m2))
                m_scr[h] = m2
            pvs = [_dot(ps[h], jnp.concatenate([kvbs[h], ones], axis=1)) for h in heads]
            for h in heads:
                l_scr[h] = alphas[h] * l_scr[h] + pvs[h][:, BLK:]
                acc_scr[h] = alphas[h] * acc_scr[h] + pvs[h][:, :BLK]

        _over_key_blocks(i + 1, block, reverse=False)
        lse = jnp.zeros((BLK, BLK), F32)
        for p in range(4):
            pc = slice(p * BLK, (p + 1) * BLK)
            o0 = acc_scr[2 * p] / l_scr[2 * p]
            o1 = acc_scr[2 * p + 1] / l_scr[2 * p + 1]
            o_ref[:, pc] = jnp.where(lo_m, pltpu.roll(o0, HALF, 1), o1)
            for h in (2 * p, 2 * p + 1):
                lse = lse + jnp.where(lane == h, m_scr[h] + jnp.log(l_scr[h]), 0.0)
        lse_ref[...] = lse
        g = g_ref[...].astype(F32)
        og_ref[...] = (o_ref[...] * g * _sigmoid(g)).astype(og_ref.dtype)

    tile = pl.BlockSpec((BLK, 512), lambda b, i: (b * nb + i, 0))
    tab = pl.BlockSpec((BLK, BLK), lambda b, i: (i, 0))
    heads = pltpu.VMEM((8, BLK, BLK), F32)
    return pl.pallas_call(
        body, name="mla_fwd", grid=(nbatch, nb),
        in_specs=[pl.BlockSpec((BLK, 1024), lambda b, i: (b * nb + i, 0)),
                  pl.BlockSpec((lp, 1024), lambda b, i: (b, 0)),
                  pl.BlockSpec((lp, BLK), lambda b, i: (b, 0)),
                  pl.BlockSpec((BLK, 512), lambda b, i: (b * nb + i, EV_GMLA_BLK // 4)),
                  tab, tab, tab],
        out_specs=[tile, tile, pl.BlockSpec((BLK, BLK), lambda b, i: (b * nb + i, 0))],
        out_shape=[jax.ShapeDtypeStruct((t, 512), F32), jax.ShapeDtypeStruct((t, 512), BF16),
                   jax.ShapeDtypeStruct((t, BLK), F32)],
        scratch_shapes=[pltpu.VMEM((BLK, 1024), BF16), heads, heads, heads],
        compiler_params=_params(("parallel", "arbitrary"), 48),
    )(qh, kvh, krot, proj, *tabs)


def _mla_bwd(qh, kvh, krot, proj, tabs, o, dog, lse, nbatch, nb):
    lp = nb * BLK
    t = nbatch * lp

    def body(q_ref, kv_ref, kr_ref, g_ref, c_ref, s1_ref, s2_ref, o_ref, dog_ref, lse_ref,
             dq_ref, dkv_ref, dkr_ref, dg_ref, qf_scr, do_scr, stat_scr, acc_scr):
        i = pl.program_id(1)

        @pl.when(i == 0)
        def _():
            dkv_ref[...] = jnp.zeros_like(dkv_ref)
            dkr_ref[...] = jnp.zeros_like(dkr_ref)

        row, lane = _iotas()
        lo_m = lane < HALF
        g = g_ref[...].astype(F32)
        sig = _sigmoid(g)
        dog_v = dog_ref[...]
        o_v = o_ref[...]
        dg_ref[...] = (dog_v * o_v * (sig * (1.0 + g * (1.0 - sig)))).astype(dg_ref.dtype)
        do = dog_v * g * sig
        do_o = do * o_v
        lse_blk = lse_ref[...]
        zero = jnp.zeros((BLK, BLK), F32)
        for h in range(8):
            hc = slice(h * BLK, (h + 1) * BLK)
            pc = slice((h // 2) * BLK, (h // 2 + 1) * BLK)
            qf_scr[:, hc] = (_rope(q_ref[:, hc].astype(F32), c_ref[...], s1_ref[...], s2_ref[...])
                             * MLA_SCALE).astype(BF16)
            dop = do[:, pc]
            do_src = dop if h % 2 else pltpu.roll(dop, HALF, 1)
            do_scr[:, hc] = jnp.where(lo_m, 0.0, do_src).astype(BF16)
            hm = (lane >= HALF) if h % 2 else lo_m
            stat_scr[h] = zero + jnp.sum(jnp.where(hm, do_o[:, pc], 0.0), axis=1, keepdims=True)
            stat_scr[8 + h] = zero + jnp.sum(jnp.where(lane == h, lse_blk, 0.0), axis=1, keepdims=True)
        acc_scr[...] = jnp.zeros_like(acc_scr)

        def block(off, width):
            mask = _key_mask(i, off, width, strict=False)
            lo_k = lax.broadcasted_iota(jnp.int32, (width, BLK), 1) < HALF
            krb = kr_ref[pl.ds(off, width), :]
            heads = range(8)
            hcs = [slice(h * BLK, (h + 1) * BLK) for h in heads]
            kvbs = [kv_ref[pl.ds(off, width), hc] for hc in hcs]
            qfs = [qf_scr[:, hc] for hc in hcs]
            dos = [do_scr[:, hc] for hc in hcs]
            scored = [_mla_scores(qfs[h], kvbs[h], krb, mask, lo_k) for h in heads]
            dps = [_dot_nt(dos[h], kvbs[h]) for h in heads]
            pbs, dss = [], []
            for h in heads:
                p = jnp.exp(scored[h][1] - _widen(stat_scr[8 + h], width))
                pbs.append(p.astype(BF16))
                dss.append((p * (dps[h] - _widen(stat_scr[h], width))).astype(BF16))
            dqs = [_dot(dss[h], scored[h][0]) for h in heads]
            dkfs = [_dot_tn(dss[h], qfs[h]) for h in heads]
            dvvs = [_dot_tn(pbs[h], dos[h]) for h in heads]
            dkr = jnp.zeros((width, BLK), F32)
            for h in heads:
                acc_scr[h] += dqs[h]
                dkv_ref[pl.ds(off, width), hcs[h]] += jnp.where(lo_k, dkfs[h], 0.0) + dvvs[h]
                dkr = dkr + jnp.where(lo_k, 0.0, dkfs[h])
            dkr_ref[pl.ds(off, width), :] += dkr

        _over_key_blocks(i + 1, block, reverse=False)
        for h in range(8):
            hc = slice(h * BLK, (h + 1) * BLK)
            dq_ref[:, hc] = _rope_t(acc_scr[h] * MLA_SCALE, c_ref[...], s1_ref[...], s2_ref[...]).astype(dq_ref.dtype)

    tile = lambda col: pl.BlockSpec((BLK, 512), lambda b, i: (b * nb + i, col))
    wide = pl.BlockSpec((BLK, 1024), lambda b, i: (b * nb + i, 0))
    full8 = pl.BlockSpec((lp, 1024), lambda b, i: (b, 0))
    full1 = pl.BlockSpec((lp, BLK), lambda b, i: (b, 0))
    tab = pl.BlockSpec((BLK, BLK), lambda b, i: (i, 0))
    return pl.pallas_call(
        body, name="mla_bwd", grid=(nbatch, nb),
        in_specs=[wide, full8, full1, tile(EV_GMLA_BLK // 4), tab, tab, tab, tile(0), tile(1),
                  pl.BlockSpec((BLK, BLK), lambda b, i: (b * nb + i, 0))],
        out_specs=[wide, full8, full1, tile(0)],
        out_shape=[jax.ShapeDtypeStruct((t, 1024), BF16), jax.ShapeDtypeStruct((t, 1024), F32),
                   jax.ShapeDtypeStruct((t, 128), F32), jax.ShapeDtypeStruct((t, 512), BF16)],
        scratch_shapes=[pltpu.VMEM((BLK, 1024), BF16), pltpu.VMEM((BLK, 1024), BF16),
                        pltpu.VMEM((16, BLK, BLK), F32), pltpu.VMEM((8, BLK, BLK), F32)],
        compiler_params=_params(("parallel", "arbitrary"), 56),
    )(qh, kvh, krot, proj, *tabs, o, dog, lse)


def _swa_setup(kk, i, k_refs, v_refs):
    row, lane = _iotas()
    own = (lane >= kk * HALF) & (lane < (kk + 1) * HALF)

    def dup(ref):
        x = ref[...].astype(F32)
        return jnp.where(own, x, pltpu.roll(x, HALF, 1)).astype(BF16)

    kcat = jnp.concatenate([dup(r) for r in k_refs], axis=0)
    vcat = jnp.concatenate([dup(r) for r in v_refs], axis=0)
    row3 = lax.broadcasted_iota(jnp.int32, (BLK, 3 * BLK), 0)
    lane3 = lax.broadcasted_iota(jnp.int32, (BLK, 3 * BLK), 1)
    is_meta = lane3 < BLK
    k_pos = jnp.where(is_meta, lane3, (i - 2) * BLK + lane3)
    d = i * BLK + row3 - k_pos
    mask = (d >= 0) & (d < jnp.where(is_meta, 1 << 20, BLK)) & (k_pos >= jnp.where(is_meta, N_PAD, BLK))
    return lane, own, kcat, vcat, mask, d.astype(F32)


def _swa_slope(kk, g_idx):
    return (2.0 ** (-(g_idx + 1) / 2.0)) * jnp.where(kk == 0, 1.0, 1.0 / 16.0)


def _swa_fwd(proj, sinks, nbatch, nb):
    lp = nb * BLK
    t = nbatch * lp

    def body(sink_ref, q_ref, ka, kb, kc, va, vb, vc, g_ref, o_ref, og_ref, lse_ref):
        kk = pl.program_id(1)
        i = pl.program_id(2)
        lane, own, kcat, vcat, mask, dist = _swa_setup(kk, i, (ka, kb, kc), (va, vb, vc))
        lo_m = lane < HALF
        heads = range(8)
        qms = []
        for h in heads:
            qp = q_ref[:, (h // 2) * BLK:(h // 2 + 1) * BLK]
            qms.append(jnp.where((lane >= HALF) if h % 2 else lo_m, qp, jnp.zeros_like(qp)) * SWA_SCALE)
        qks = [_dot_nt(qms[h], kcat) for h in heads]
        ps, ls, lses = [], [], []
        for h in heads:
            sink = sink_ref[kk, h]
            s = jnp.where(mask, qks[h] - _swa_slope(kk, h) * dist, NEG)
            mx = jnp.maximum(jnp.max(s, axis=1, keepdims=True), sink)
            p = jnp.exp(s - mx)
            l = jnp.exp(sink - mx) + jnp.sum(p, axis=1, keepdims=True)
            ps.append(p.astype(BF16))
            ls.append(l)
            lses.append(mx + jnp.log(l))
        pvs = [_dot(ps[h], vcat) for h in heads]
        lse_out = jnp.zeros((BLK, BLK), F32)
        for m in range(4):
            cols = slice(m * BLK, (m + 1) * BLK)
            outp = jnp.where(lo_m, pvs[2 * m] / ls[2 * m], pvs[2 * m + 1] / ls[2 * m + 1])
            o_ref[:, cols] = outp
            g = g_ref[:, cols].astype(F32)
            og_ref[:, cols] = (outp * g * _sigmoid(g)).astype(og_ref.dtype)
            for h in (2 * m, 2 * m + 1):
                lse_out = lse_out + jnp.where(lane == h, lses[h], 0.0)
        lse_ref[...] = lse_out

    def kvspec(col, which):
        if which == 0:
            return pl.BlockSpec((BLK, BLK), lambda b, kk, i: (b * nb, col))
        if which == 1:
            return pl.BlockSpec((BLK, BLK), lambda b, kk, i: (b * nb + jnp.maximum(i - 1, 0), col))
        return pl.BlockSpec((BLK, BLK), lambda b, kk, i: (b * nb + i, col))

    wide = lambda c0: pl.BlockSpec((BLK, 512), lambda b, kk, i: (b * nb + i, c0 + kk))
    return pl.pallas_call(
        body, name="swa_fwd", grid=(nbatch, 2, nb),
        in_specs=[pl.BlockSpec(memory_space=pltpu.SMEM), wide(0),
                  kvspec(OD_K_BLK, 0), kvspec(OD_K_BLK, 1), kvspec(OD_K_BLK, 2),
                  kvspec(OD_V_BLK, 0), kvspec(OD_V_BLK, 1), kvspec(OD_V_BLK, 2), wide(2)],
        out_specs=[wide(0), wide(0), pl.BlockSpec((BLK, BLK), lambda b, kk, i: (b * nb + i, kk))],
        out_shape=[jax.ShapeDtypeStruct((t, 1024), F32), jax.ShapeDtypeStruct((t, 1024), BF16),
                   jax.ShapeDtypeStruct((t, 256), F32)],
        compiler_params=_params(("parallel", "parallel", "arbitrary")),
    )(sinks, proj, proj, proj, proj, proj, proj, proj, proj)


def _swa_bwd(proj, sinks, o, dog, lse, nbatch, nb):
    lp = nb * BLK
    t = nbatch * lp

    def body(sink_ref, q_ref, ka, kb, kc, va, vb, vc, g_ref, o_ref, dog_ref, lse_ref,
             dq_ref, dg_ref, dk_ref, dv_ref, dsink_ref):
        kk = pl.program_id(1)
        i = pl.program_id(2)

        @pl.when((kk == 0) & (i == 0))
        def _():
            dk_ref[...] = jnp.zeros_like(dk_ref)
            dv_ref[...] = jnp.zeros_like(dv_ref)

        @pl.when(i == 0)
        def _():
            dsink_ref[...] = jnp.zeros_like(dsink_ref)

        lane, own, kcat, vcat, mask, dist = _swa_setup(kk, i, (ka, kb, kc), (va, vb, vc))
        lo_m = lane < HALF
        row8 = lax.broadcasted_iota(jnp.int32, (8, BLK), 0)
        lse_blk = lse_ref[...]
        heads = range(8)
        qms, doms, deltas, lse_hs = [], [], [], []
        for m in range(4):
            cols = slice(m * BLK, (m + 1) * BLK)
            qp = q_ref[:, cols]
            g = g_ref[:, cols].astype(F32)
            sig = _sigmoid(g)
            dog_v = dog_ref[:, cols]
            o_v = o_ref[:, cols]
            dg_ref[:, cols] = (dog_v * o_v * (sig * (1.0 + g * (1.0 - sig)))).astype(dg_ref.dtype)
            do = dog_v * g * sig
            do_o = do * o_v
            dob = do.astype(BF16)
            for h in (2 * m, 2 * m + 1):
                hm = (lane >= HALF) if h % 2 else lo_m
                qms.append(jnp.where(hm, qp, jnp.zeros_like(qp)) * SWA_SCALE)
                doms.append(jnp.where(hm, dob, jnp.zeros_like(dob)))
                deltas.append(jnp.sum(jnp.where(hm, do_o, 0.0), axis=1, keepdims=True))
                lse_hs.append(jnp.sum(jnp.where(lane == h, lse_blk, 0.0), axis=1, keepdims=True))
        qks = [_dot_nt(qms[h], kcat) for h in heads]
        dps = [_dot_nt(doms[h], vcat) for h in heads]
        pbs, dss = [], []
        dsink = jnp.zeros((8, BLK), F32)
        for h in heads:
            s = jnp.where(mask, qks[h] - _swa_slope(kk, h) * dist, NEG)
            p = jnp.exp(s - lse_hs[h])
            pbs.append(p.astype(BF16))
            dss.append((p * (dps[h] - deltas[h])).astype(BF16))
            tot = jnp.sum(-jnp.exp(sink_ref[kk, h] - lse_hs[h]) * deltas[h], axis=0, keepdims=True)
            dsink = dsink + jnp.where(row8 == h, tot, 0.0)
        dsink_ref[...] += dsink
        dqs = [_dot(dss[h], kcat) for h in heads]
        dks = [_dot_tn(dss[h], qms[h]) for h in heads]
        dvs = [_dot_tn(pbs[h], doms[h]) for h in heads]
        for m in range(4):
            dq_ref[:, m * BLK:(m + 1) * BLK] = (
                jnp.where(lo_m, dqs[2 * m], dqs[2 * m + 1]) * SWA_SCALE).astype(dq_ref.dtype)
        dk = dks[0]
        dv = dvs[0]
        for h in range(1, 8):
            dk = dk + dks[h]
            dv = dv + dvs[h]
        offs = [0, pl.multiple_of(jnp.maximum(i - 1, 0) * BLK, BLK), pl.multiple_of(i * BLK, BLK)]
        for x in range(3):
            rows = slice(x * BLK, (x + 1) * BLK)
            dkx, dvx = dk[rows], dv[rows]
            dk_ref[pl.ds(offs[x], BLK), :] += jnp.where(own, dkx + pltpu.roll(dkx, HALF, 1), 0.0)
            dv_ref[pl.ds(offs[x], BLK), :] += jnp.where(own, dvx + pltpu.roll(dvx, HALF, 1), 0.0)

    def kvspec(col, which):
        if which == 0:
            return pl.BlockSpec((BLK, BLK), lambda b, kk, i: (b * nb, col))
        if which == 1:
            return pl.BlockSpec((BLK, BLK), lambda b, kk, i: (b * nb + jnp.maximum(i - 1, 0), col))
        return pl.BlockSpec((BLK, BLK), lambda b, kk, i: (b * nb + i, col))

    wide = lambda c0: pl.BlockSpec((BLK, 512), lambda b, kk, i: (b * nb + i, c0 + kk))
    full = pl.BlockSpec((lp, BLK), lambda b, kk, i: (b, 0))
    return pl.pallas_call(
        body, name="swa_bwd", grid=(nbatch, 2, nb),
        in_specs=[pl.BlockSpec(memory_space=pltpu.SMEM), wide(0),
                  kvspec(OD_K_BLK, 0), kvspec(OD_K_BLK, 1), kvspec(OD_K_BLK, 2),
                  kvspec(OD_V_BLK, 0), kvspec(OD_V_BLK, 1), kvspec(OD_V_BLK, 2), wide(2),
                  wide(0), wide(0), pl.BlockSpec((BLK, BLK), lambda b, kk, i: (b * nb + i, kk))],
        out_specs=[wide(0), wide(0), full, full,
                   pl.BlockSpec((8, BLK), lambda b, kk, i: (b * 2 + kk, 0))],
        out_shape=[jax.ShapeDtypeStruct((t, 1024), BF16), jax.ShapeDtypeStruct((t, 1024), BF16),
                   jax.ShapeDtypeStruct((t, 128), F32), jax.ShapeDtypeStruct((t, 128), F32),
                   jax.ShapeDtypeStruct((nbatch * 16, BLK), F32)],
        compiler_params=_params(("parallel", "arbitrary", "arbitrary")),
    )(sinks, proj, proj, proj, proj, proj, proj, proj, proj, o, dog, lse)


def _rope_tables(lp):
    pos = (jnp.arange(lp) - N_PAD).astype(F32)
    inv = ROPE_BASE ** (-jnp.arange(16, dtype=F32) / 16.0)
    ang = pos[:, None] * inv[None, :]
    cos, sin = jnp.cos(ang), jnp.sin(ang)
    z16 = jnp.zeros((lp, 16), F32)
    c = jnp.concatenate([jnp.ones((lp, 64), F32), cos, cos, jnp.zeros((lp, 32), F32)], axis=1)
    s1 = jnp.concatenate([jnp.zeros((lp, 64), F32), -sin, z16, jnp.zeros((lp, 32), F32)], axis=1)
    s2 = jnp.concatenate([jnp.zeros((lp, 64), F32), z16, sin, jnp.zeros((lp, 32), F32)], axis=1)
    return c, s1, s2


def _local_step(h0, tgt, norm_g, final_g, gq, gkv, sinks, w_ie, late_shards, nbatch, nb):
    lp = nb * BLK
    tabs = _rope_tables(lp)
    g0, g1 = norm_g[0:1], norm_g[1:2]
    sinks2 = sinks.reshape(2, 8)

    proj_e, hn0 = _norm_mm(h0, g0, w_ie, "proj_even")
    o_sb, og_sb, cm, gathered = _sb_fwd(proj_e, [late_shards[name] for name in _LATE], nbatch, nb)
    full = {name: _unchunk(name, blk) for name, blk in zip(_LATE, gathered)}
    w_uq, w_ukv, w_oe = _uq_to_compute(full["ev_w_uq"]), full["ev_w_ukv"], full["ev_w_out"]
    w_io, w_oo = _od_in_to_compute(full["od_w_in"]), full["od_w_out"]
    cqn, ckvn, krot = _mla_prep(proj_e, gq, gkv, tabs, nbatch, nb)
    qh = _mm(cqn, w_uq, "nn", "mla_uq", out_dtype=BF16)
    kvh = _mm(ckvn, w_ukv, "nn", "mla_ukv", out_dtype=BF16)
    o_mla, og_mla, lse_m = _mla_fwd(qh, kvh, krot, proj_e, tabs, nbatch, nb)
    h1 = _mm([og_sb, og_mla], w_oe, "nn", "out_even", add=h0)
    proj_o, hn1 = _norm_mm(h1, g1, w_io, "proj_odd")
    o_o, og_o, lse_o = _swa_fwd(proj_o, sinks2, nbatch, nb)
    h2 = _mm(og_o, w_oo, "nn", "out_odd", add=h1)
    dh2, lossv, d_final_g = _final(h2, tgt, final_g, nbatch, nb)

    dog_o = _mm(dh2, w_oo, "nt", "d_out_odd")
    d_w_oo, = _mm_tn(og_o, [dh2], "dw_out_odd")
    dq_o, dg_o, dk_o, dv_o, dsink = _swa_bwd(proj_o, sinks2, o_o, dog_o, lse_o, nbatch, nb)
    dproj_o = [dq_o, dg_o, dk_o, dv_o]
    dh1, d_g1 = _mm_norm_bwd(dproj_o, w_io, h1, g1, dh2, "d_proj_odd")
    dw_q, dw_g, dw_k, dw_v = _mm_tn(hn1, dproj_o, "dw_proj_odd")

    dog_e = _mm(dh1, w_oe, "nt", "d_out_even")
    d_w_oe_sb, = _mm_tn(og_sb, [dh1], "dw_out_even_sb")
    d_w_oe_mla, = _mm_tn(og_mla, [dh1], "dw_out_even_mla")
    early = dict(od_w_in=jnp.concatenate([dw_q, dw_k, dw_v, dw_g], axis=1), od_w_out=d_w_oo,
                 ev_w_out=jnp.concatenate([d_w_oe_sb, d_w_oe_mla], axis=0))
    dq_sb, dk_sb, dv_sb, dg_sb, received = _sb_bwd(
        proj_e, o_sb, dog_e, cm, [_chunk(name, early[name]) for name in _EARLY_GRADS], nbatch, nb)
    dqh, dkvh, dkrot, dg_mla = _mla_bwd(qh, kvh, krot, proj_e, tabs, o_mla, dog_e, lse_m, nbatch, nb)
    dcqn = _mm(dqh, w_uq, "nt", "d_mla_uq")
    d_w_uq, = _mm_tn(cqn, [dqh], "dw_mla_uq")
    dckvn = _mm(dkvh, w_ukv, "nt", "d_mla_ukv")
    d_w_ukv, = _mm_tn(ckvn, [dkvh], "dw_mla_ukv")
    dmid, d_gq, d_gkv = _mla_prep_bwd(proj_e, gq, gkv, tabs, dcqn, dckvn, dkrot, nbatch, nb)
    dproj_e = [dq_sb, dk_sb, dv_sb, dg_sb, dmid, dg_mla]
    dh0, d_g0 = _mm_norm_bwd(dproj_e, w_ie, h0, g0, dh1, "d_proj_even")
    dw_e = _mm_tn(hn0, dproj_e, "dw_proj_even")

    d_sinks = dsink.reshape(nbatch, 2, 8, BLK)[:, :, :, 0].sum(axis=0).reshape(1, 16)
    d_norm_g = jnp.concatenate([d_g0, d_g1], axis=0)
    d_ev_w_in = jnp.concatenate(list(dw_e[:4]) + [dw_e[4][:, :384], dw_e[4][:, 448:480], dw_e[5]], axis=1)
    return dict(lossv=lossv, dh0=dh0, norm_g=d_norm_g, final_g=d_final_g, gq=d_gq, gkv=d_gkv, sinks=d_sinks,
                ev_w_in=d_ev_w_in, ev_w_uq=_uq_from_compute(d_w_uq), ev_w_ukv=d_w_ukv, received=received)


def _ev_in_to_compute(w):
    z = lambda n: jnp.zeros((w.shape[0], n), w.dtype)
    return jnp.concatenate([w[:, :2432], z(64), w[:, 2432:2464], z(32), w[:, 2464:]], axis=1)


def _uq_to_compute(w):
    w3 = w.reshape(256, 8, 96)
    return jnp.concatenate([w3, jnp.zeros((256, 8, 32), w.dtype)], axis=2).reshape(256, 1024)


def _uq_from_compute(w):
    return w.reshape(256, 8, 128)[:, :, :96].reshape(256, 768)


def _od_in_to_compute(w):
    return jnp.concatenate([w[:, :1024], w[:, 1280:], w[:, 1024:1280]], axis=1)


_BIG = dict(ev_w_in=(1024, 2976, 1), ev_w_uq=(256, 768, 1), ev_w_ukv=(128, 1024, 1),
            ev_w_out=(1024, 1024, 0), od_w_in=(1024, 2304, 1), od_w_out=(1024, 1024, 0))
_LATE = ("ev_w_uq", "ev_w_ukv", "ev_w_out", "od_w_in", "od_w_out")
_EARLY_GRADS = ("od_w_in", "od_w_out", "ev_w_out")
_LAST_GRADS = ("ev_w_in", "ev_w_uq", "ev_w_ukv")


def _unchunk(name, blk):
    rows, cols, axis = _BIG[name]
    return blk.transpose(1, 0, 2).reshape(rows, cols) if axis == 1 else blk.reshape(rows, cols)


def _chunk(name, g):
    rows, cols, axis = _BIG[name]
    g = g.astype(BF16)
    return g.reshape(rows, N_DEV, cols // N_DEV).transpose(1, 0, 2) if axis == 1 else g.reshape(N_DEV, rows // N_DEV, cols)


def _all_gather(shards, name):
    n = len(shards)

    def body(*refs):
        xs, outs = refs[:n], refs[n:2 * n]
        send_sems, recv_sems, local_sems = refs[2 * n:]
        x, y, c = lax.axis_index("x"), lax.axis_index("y"), lax.axis_index("c")
        me, sibling = (x, y, c), (x, y, 1 - c)
        chips = [(1 - x, y), (x, 1 - y), (1 - x, 1 - y)]
        arrays = range(n)

        def copy(k, a, block, to, from_input=False):
            px, py, pc = block
            dst = outs[a].at[4 * px + 2 * py + pc]
            return pltpu.make_async_remote_copy(
                src_ref=xs[a] if from_input else dst, dst_ref=dst,
                send_sem=send_sems.at[k, a], recv_sem=recv_sems.at[k, a],
                device_id=to, device_id_type=pl.DeviceIdType.MESH)

        mine = [pltpu.make_async_copy(xs[a], outs[a].at[4 * x + 2 * y + c], local_sems.at[a]) for a in arrays]
        for cp in mine:
            cp.start()
        first = [copy(0, a, me, sibling, True) for a in arrays]
        for j, chip in enumerate(chips):
            first += [copy(1 + j, a, me, (*chip, c), True) for a in arrays]
        for cp in first:
            cp.start()
        passed = []
        for j, chip in enumerate(chips):
            for a in arrays:
                copy(1 + j, a, (*chip, c), me).wait_recv()
                passed.append(copy(4 + j, a, (*chip, c), sibling))
                passed[-1].start()
        for a in arrays:
            copy(0, a, sibling, me).wait_recv()
        for j, chip in enumerate(chips):
            for a in arrays:
                copy(4 + j, a, (*chip, 1 - c), me).wait_recv()
        for cp in first + passed:
            cp.wait_send()
        for cp in mine:
            cp.wait()

    hbm = pl.BlockSpec(memory_space=pl.ANY)
    return pl.pallas_call(
        body, name=name,
        out_shape=[jax.ShapeDtypeStruct((N_DEV,) + s.shape, s.dtype) for s in shards],
        in_specs=[hbm] * n, out_specs=[hbm] * n,
        scratch_shapes=[pltpu.SemaphoreType.DMA((7, n)), pltpu.SemaphoreType.DMA((7, n)),
                        pltpu.SemaphoreType.DMA((n,))],
    )(*shards)


def _exchange_sum(chunked, received, name):
    n, m = len(chunked), len(received)
    arrs = list(chunked) + list(received)

    def body(*refs):
        ins, outs = refs[:n + m], refs[n + m:2 * (n + m)]
        bufs = refs[2 * (n + m):3 * (n + m)]
        send_sems, recv_sems, local_sems, load_sems = refs[3 * (n + m):]
        loads = [pltpu.make_async_copy(ins[n + a], bufs[n + a], load_sems.at[a]) for a in range(m)]
        for cp in loads:
            cp.start()
        local, sends, recvs = _direct_exchange(ins[:n], bufs[:n], send_sems, recv_sems, local_sems, chunked=True)
        for cp in local + sends:
            cp.start()
        for a, cp in enumerate(loads):
            cp.wait()
            _sum_slots(bufs[n + a], outs[n + a])
        for cp in local:
            cp.wait()
        for cp in recvs:
            cp.wait_recv()
        for cp in sends:
            cp.wait_send()
        for a in range(n):
            _sum_slots(bufs[a], outs[a])

    hbm = pl.BlockSpec(memory_space=pl.ANY)
    vm = pl.BlockSpec(memory_space=pltpu.VMEM)
    return pl.pallas_call(
        body, name=name,
        out_shape=[jax.ShapeDtypeStruct(a.shape[1:], F32) for a in arrs],
        in_specs=[hbm] * (n + m), out_specs=[vm] * (n + m),
        scratch_shapes=[pltpu.VMEM(a.shape, a.dtype) for a in arrs] + _exchange_sems(n)
        + [pltpu.SemaphoreType.DMA((max(m, 1),))],
        compiler_params=pltpu.CompilerParams(vmem_limit_bytes=48 << 20),
    )(*arrs)


def _sum_slots(buf, out):
    rows = buf.shape[1]

    def add(sl):
        acc = buf[(0,) + sl].astype(F32)
        for k in range(1, N_DEV):
            acc = acc + buf[(k,) + sl].astype(F32)
        out[sl] = acc

    if rows > BLK and rows % BLK == 0:
        def step(r, carry):
            add((pl.ds(pl.multiple_of(r * BLK, BLK), BLK), slice(None)))
            return carry

        lax.fori_loop(0, rows // BLK, step, 0)
    else:
        add((slice(None), slice(None)))


def _adamw(ws, gs, ms, vs):
    n = len(ws)

    def body(*refs):
        ins, outs = refs[:4 * n], refs[4 * n:]
        for k in range(n):
            w_ref, g_ref, m_ref, v_ref = ins[4 * k:4 * k + 4]
            d_ref, nm_ref, nv_ref = outs[3 * k:3 * k + 3]

            def update(sl, w_ref=w_ref, g_ref=g_ref, m_ref=m_ref, v_ref=v_ref,
                       d_ref=d_ref, nm_ref=nm_ref, nv_ref=nv_ref):
                g = g_ref[sl]
                m = ADAM_B1 * m_ref[sl] + (1.0 - ADAM_B1) * g
                v = ADAM_B2 * v_ref[sl] + (1.0 - ADAM_B2) * (g * g)
                m_hat = m / (1.0 - ADAM_B1 ** ADAM_STEP)
                v_hat = v / (1.0 - ADAM_B2 ** ADAM_STEP)
                d_ref[sl] = -ADAM_LR * (m_hat / (jnp.sqrt(v_hat) + ADAM_EPS) + ADAM_WD * w_ref[sl])
                nm_ref[sl] = m
                nv_ref[sl] = v

            rows = w_ref.shape[0]
            if rows > BLK and rows % BLK == 0:
                def step(r, carry, update=update):
                    update((pl.ds(pl.multiple_of(r * BLK, BLK), BLK), slice(None)))
                    return carry

                lax.fori_loop(0, rows // BLK, step, 0)
            else:
                update((slice(None), slice(None)))

    args, out_shape = [], []
    for k in range(n):
        args += [ws[k], gs[k], ms[k], vs[k]]
        out_shape += [jax.ShapeDtypeStruct(ws[k].shape, F32)] * 3
    vm = pl.BlockSpec(memory_space=pltpu.VMEM)
    outs = pl.pallas_call(
        body, name="adamw", out_shape=out_shape,
        in_specs=[vm] * (4 * n), out_specs=[vm] * (3 * n),
        compiler_params=pltpu.CompilerParams(vmem_limit_bytes=48 << 20),
    )(*args)
    return [tuple(outs[3 * k:3 * k + 3]) for k in range(n)]


def kernel(x, meta, norm_g, final_g, ev_w_in, ev_q_norm_g, ev_kv_norm_g, ev_w_uq, ev_w_ukv, ev_w_out, od_w_in, od_sinks, od_w_out, loss_target, m_meta, m_norm_g, m_final_g, m_ev_w_in, m_ev_q_norm_g, m_ev_kv_norm_g, m_ev_w_uq, m_ev_w_ukv, m_ev_w_out, m_od_w_in, m_od_sinks, m_od_w_out, v_meta, v_norm_g, v_final_g, v_ev_w_in, v_ev_q_norm_g, v_ev_kv_norm_g, v_ev_w_uq, v_ev_w_ukv, v_ev_w_out, v_od_w_in, v_od_sinks, v_od_w_out):
    nbatch, seq, d = x.shape
    nb = seq // BLK + 1
    lp = nb * BLK
    shards = dict(ev_w_in=ev_w_in[0], ev_w_uq=ev_w_uq[0], ev_w_ukv=ev_w_ukv[0], ev_w_out=ev_w_out[0],
                  od_w_in=od_w_in[0], od_w_out=od_w_out[0])

    w_ie_blocks, meta_blocks = _all_gather([shards["ev_w_in"].astype(BF16), meta], "gather_weights")
    meta_full = meta_blocks.transpose(1, 0, 2).reshape(N_META, d)

    head = jnp.concatenate([jnp.zeros((N_PAD, d), F32), meta_full], axis=0)
    h0 = jnp.concatenate([jnp.broadcast_to(head[None], (nbatch, BLK, d)), x], axis=1).reshape(nbatch * lp, d)
    grads = _local_step(
        h0, loss_target.reshape(nbatch * seq, d), norm_g, final_g.reshape(1, d), ev_q_norm_g, ev_kv_norm_g,
        od_sinks, _ev_in_to_compute(_unchunk("ev_w_in", w_ie_blocks)),
        {name: shards[name].astype(BF16) for name in _LATE}, nbatch, nb)
    dh0 = grads["dh0"].reshape(nbatch, lp, d)
    grad_x = dh0[:, BLK:]
    loss = lax.psum(0.5 / d * jnp.sum(grads["lossv"]), ("x", "y", "c"))

    d_meta = dh0[:, N_PAD:BLK].sum(axis=0).reshape(N_META, N_DEV, BLK).transpose(1, 0, 2)
    pad = lambda a, n: jnp.concatenate([a.reshape(1, -1), jnp.zeros((1, n - a.size), F32)], axis=1)
    rep = jnp.concatenate([grads["norm_g"].reshape(1, -1), grads["final_g"], pad(grads["gq"], 512),
                           pad(grads["gkv"], 256), pad(grads["sinks"], 256)], axis=1).reshape(32, BLK)
    small = jnp.concatenate([d_meta, jnp.broadcast_to(rep[None], (N_DEV, 32, BLK))], axis=1)
    reduced = _exchange_sum([_chunk(name, grads[name]) for name in _LAST_GRADS] + [small],
                            grads["received"], "reduce_grads")
    g_shard = dict(zip(_LAST_GRADS + ("small",) + _EARLY_GRADS, reduced))
    red_small = g_shard.pop("small")
    rep = red_small[N_META:].reshape(1, -1)
    g_small = dict(meta=red_small[:N_META], norm_g=rep[:, :2 * d].reshape(2, d), final_g=rep[:, 2 * d:3 * d],
                   ev_q_norm_g=rep[:, 3 * d:3 * d + 256], ev_kv_norm_g=rep[:, 3 * d + 512:3 * d + 640],
                   od_sinks=rep[:, 3 * d + 768:3 * d + 784])

    names = ["meta", "norm_g", "final_g", "ev_w_in", "ev_q_norm_g", "ev_kv_norm_g", "ev_w_uq", "ev_w_ukv",
             "ev_w_out", "od_w_in", "od_sinks", "od_w_out"]
    given = dict(meta=(meta, m_meta, v_meta), norm_g=(norm_g, m_norm_g, v_norm_g),
                 final_g=(final_g, m_final_g, v_final_g), ev_w_in=(ev_w_in, m_ev_w_in, v_ev_w_in),
                 ev_q_norm_g=(ev_q_norm_g, m_ev_q_norm_g, v_ev_q_norm_g),
                 ev_kv_norm_g=(ev_kv_norm_g, m_ev_kv_norm_g, v_ev_kv_norm_g),
                 ev_w_uq=(ev_w_uq, m_ev_w_uq, v_ev_w_uq), ev_w_ukv=(ev_w_ukv, m_ev_w_ukv, v_ev_w_ukv),
                 ev_w_out=(ev_w_out, m_ev_w_out, v_ev_w_out), od_w_in=(od_w_in, m_od_w_in, v_od_w_in),
                 od_sinks=(od_sinks, m_od_sinks, v_od_sinks), od_w_out=(od_w_out, m_od_w_out, v_od_w_out))
    ws, gs, ms, vs = [], [], [], []
    for name in names:
        g2 = g_shard[name] if name in g_shard else g_small[name]
        w, m, v = given[name]
        ws.append(w.reshape(g2.shape))
        ms.append(m.reshape(g2.shape))
        vs.append(v.reshape(g2.shape))
        gs.append(g2)
    upd = _adamw(ws, gs, ms, vs)
    shape_of = {name: given[name][0].shape for name in names}
    grads_out = [gs[k].reshape(shape_of[n]) for k, n in enumerate(names)]
    deltas = [upd[k][0].reshape(shape_of[n]) for k, n in enumerate(names)]
    new_m = [upd[k][1].reshape(shape_of[n]) for k, n in enumerate(names)]
    new_v = [upd[k][2].reshape(shape_of[n]) for k, n in enumerate(names)]
    return (loss, grad_x, *grads_out, *deltas, *new_m, *new_v)
```

```python
import jax
import jax.numpy as jnp
from jax import lax
from jax.experimental import pallas as pl
from jax.experimental.pallas import tpu as pltpu

F32 = jnp.float32
BF16 = jnp.bfloat16

D_MODEL = 1024
N_META = 16
BLK = 128
HALF = 64
N_PAD = BLK - N_META
NORM_EPS = 1e-6
NEG = -1e30
N_DEV = 8

SB_SCALE = 64 ** -0.5
MLA_SCALE = 96 ** -0.5
SWA_SCALE = 64 ** -0.5
ROPE_BASE = 10000.0

EV_IN_PAD = 3072
EV_MID_BLK = 4
EV_GMLA_BLK = 20
OD_K_BLK = 16
OD_V_BLK = 17

ADAM_LR = 0.001
ADAM_B1 = 0.9
ADAM_B2 = 0.999
ADAM_EPS = 1e-08
ADAM_WD = 0.01
ADAM_STEP = 10


def _dot(a, b):
    return lax.dot_general(a, b, (((1,), (0,)), ((), ())), preferred_element_type=F32)


def _dot_nt(a, b):
    return lax.dot_general(a, b, (((1,), (1,)), ((), ())), preferred_element_type=F32)


def _dot_tn(a, b):
    return lax.dot_general(a, b, (((0,), (0,)), ((), ())), preferred_element_type=F32)


def _split(x):
    hi = x.astype(BF16)
    return hi, (x - hi.astype(F32)).astype(BF16)


def _sigmoid(x):
    return 1.0 / (1.0 + jnp.exp(-x))


def _iotas():
    row = lax.broadcasted_iota(jnp.int32, (BLK, BLK), 0)
    lane = lax.broadcasted_iota(jnp.int32, (BLK, BLK), 1)
    return row, lane


WIDE = 2 * BLK
HEAD_GROUPS = (range(0, 8),)


def _key_mask(i, first_key, width, strict):
    t_pos = i * BLK + lax.broadcasted_iota(jnp.int32, (BLK, width), 0)
    s_pos = first_key + lax.broadcasted_iota(jnp.int32, (BLK, width), 1)
    seen = (s_pos < t_pos) if strict else (s_pos <= t_pos)
    return seen & (s_pos >= N_PAD)


def _widen(x, width):
    return x if width == BLK else jnp.concatenate([x] * (width // BLK), axis=1)


def _over_key_blocks(n, block, reverse):
    pairs = n // 2
    last = pl.multiple_of((n - 1) * BLK, BLK)

    def step(jj, carry):
        jp = (pairs - 1 - jj) if reverse else jj
        block(pl.multiple_of(jp * WIDE, WIDE), WIDE)
        return carry

    if reverse:
        pl.when(n % 2 == 1)(lambda: block(last, BLK))
        lax.fori_loop(0, pairs, step, 0)
    else:
        lax.fori_loop(0, pairs, step, 0)
        pl.when(n % 2 == 1)(lambda: block(last, BLK))


def _rope(x, c, s1, s2):
    return x * c + pltpu.roll(x, BLK - 16, 1) * s1 + pltpu.roll(x, 16, 1) * s2


def _rope_t(x, c, s1, s2):
    return x * c - pltpu.roll(x, BLK - 16, 1) * s1 - pltpu.roll(x, 16, 1) * s2


def _params(sem, vmem_mb=None):
    kw = dict(dimension_semantics=sem)
    if vmem_mb is not None:
        kw["vmem_limit_bytes"] = vmem_mb << 20
    return pltpu.CompilerParams(**kw)


def _row_tile(t, cands):
    for c in cands:
        if t % c == 0:
            return c
    raise ValueError(t)


def _mm(a, w, mode, name, add=None, out_dtype=F32):
    pieces = list(a) if isinstance(a, (list, tuple)) else [a]
    m = pieces[0].shape[0]
    n = w.shape[1] if mode == "nn" else w.shape[0]
    tm = _row_tile(m, (256, 128))
    widths = [p.shape[1] for p in pieces]
    offs = [sum(widths[:i]) for i in range(len(widths))]

    def body(*refs):
        p_refs = refs[:len(pieces)]
        w_ref = refs[len(pieces)]
        o_ref = refs[-1]
        acc = None
        for p_ref, off, wd in zip(p_refs, offs, widths):
            x = p_ref[...].astype(BF16)
            part = _dot(x, w_ref[off:off + wd, :]) if mode == "nn" else _dot_nt(x, w_ref[:, off:off + wd])
            acc = part if acc is None else acc + part
        if add is not None:
            acc = acc + refs[len(pieces) + 1][...]
        o_ref[...] = acc.astype(o_ref.dtype)

    in_specs = [pl.BlockSpec((tm, wd), lambda i: (i, 0)) for wd in widths]
    in_specs.append(pl.BlockSpec(w.shape, lambda i: (0, 0)))
    args = pieces + [w]
    if add is not None:
        in_specs.append(pl.BlockSpec((tm, n), lambda i: (i, 0)))
        args.append(add)
    return pl.pallas_call(
        body, name=name, grid=(m // tm,), in_specs=in_specs,
        out_specs=pl.BlockSpec((tm, n), lambda i: (i, 0)),
        out_shape=jax.ShapeDtypeStruct((m, n), out_dtype),
        compiler_params=_params(("parallel",), 48),
    )(*args)


def _mm_tn(x, pieces, name):
    t, k = x.shape
    tt = _row_tile(t, (544, 256, 128))
    widths = [p.shape[1] for p in pieces]

    def body(*refs):
        x_ref = refs[0]
        d_refs = refs[1:1 + len(pieces)]
        o_refs = refs[1 + len(pieces):]

        @pl.when(pl.program_id(0) == 0)
        def _():
            for o_ref in o_refs:
                o_ref[...] = jnp.zeros_like(o_ref)

        xb = x_ref[...].astype(BF16)
        for d_ref, o_ref in zip(d_refs, o_refs):
            o_ref[...] += _dot_tn(xb, d_ref[...].astype(BF16))

    return pl.pallas_call(
        body, name=name, grid=(t // tt,),
        in_specs=[pl.BlockSpec((tt, k), lambda i: (i, 0))] + [pl.BlockSpec((tt, wd), lambda i: (i, 0)) for wd in widths],
        out_specs=[pl.BlockSpec((k, wd), lambda i: (0, 0)) for wd in widths],
        out_shape=[jax.ShapeDtypeStruct((k, wd), F32) for wd in widths],
        compiler_params=_params(("arbitrary",), 56),
    )(x, *pieces)


def _norm_mm(h, g, w, name):
    t, d = h.shape
    n = w.shape[1]
    tr = _row_tile(t, (256, 128))

    def body(h_ref, g_ref, w_ref, o_ref, hn_ref):
        x = h_ref[...]
        r = lax.rsqrt(jnp.mean(x * x, axis=1, keepdims=True) + NORM_EPS)
        hn = (x * r * g_ref[...]).astype(BF16)
        hn_ref[...] = hn
        o_ref[...] = _dot(hn, w_ref[...]).astype(o_ref.dtype)

    row = lambda width: pl.BlockSpec((tr, width), lambda i: (i, 0))
    return pl.pallas_call(
        body, name=name, grid=(t // tr,),
        in_specs=[row(d), pl.BlockSpec((1, d), lambda i: (0, 0)), pl.BlockSpec(w.shape, lambda i: (0, 0))],
        out_specs=[row(n), row(d)],
        out_shape=[jax.ShapeDtypeStruct((t, n), BF16), jax.ShapeDtypeStruct((t, d), BF16)],
        compiler_params=_params(("parallel",), 48),
    )(h, g, w)


def _mm_norm_bwd(pieces, w, h, g, dres, name):
    t, d = h.shape
    tr = _row_tile(t, (256, 128))
    widths = [p.shape[1] for p in pieces]
    offs = [sum(widths[:i]) for i in range(len(widths))]

    def body(*refs):
        p_refs = refs[:len(pieces)]
        w_ref, h_ref, g_ref, dres_ref, dh_ref, dg_ref = refs[len(pieces):]

        @pl.when(pl.program_id(0) == 0)
        def _():
            dg_ref[...] = jnp.zeros_like(dg_ref)

        dy = None
        for p_ref, off, wd in zip(p_refs, offs, widths):
            part = _dot_nt(p_ref[...].astype(BF16), w_ref[:, off:off + wd])
            dy = part if dy is None else dy + part
        x = h_ref[...]
        r = lax.rsqrt(jnp.mean(x * x, axis=1, keepdims=True) + NORM_EPS)
        nx = x * r
        dn = dy * g_ref[...]
        dh_ref[...] = dres_ref[...] + r * (dn - nx * jnp.mean(dn * nx, axis=1, keepdims=True))
        dg_ref[...] += jnp.sum(dy * nx, axis=0, keepdims=True)

    row = lambda width: pl.BlockSpec((tr, width), lambda i: (i, 0))
    vec = pl.BlockSpec((1, d), lambda i: (0, 0))
    return pl.pallas_call(
        body, name=name, grid=(t // tr,),
        in_specs=[row(wd) for wd in widths] + [pl.BlockSpec(w.shape, lambda i: (0, 0)), row(d), vec, row(d)],
        out_specs=[row(d), vec],
        out_shape=[jax.ShapeDtypeStruct((t, d), F32), jax.ShapeDtypeStruct((1, d), F32)],
        compiler_params=_params(("arbitrary",), 48),
    )(*pieces, w, h, g, dres)


def _final(h2, tgt, g, nbatch, nb):
    t, d = h2.shape

    def body(h_ref, t_ref, g_ref, dh_ref, loss_ref, dg_ref):
        b = pl.program_id(0)
        i = pl.program_id(1)

        @pl.when((b == 0) & (i == 0))
        def _():
            loss_ref[...] = jnp.zeros_like(loss_ref)
            dg_ref[...] = jnp.zeros_like(dg_ref)

        x = h_ref[...]
        r = lax.rsqrt(jnp.mean(x * x, axis=1, keepdims=True) + NORM_EPS)
        nx = x * r
        gg = g_ref[...]
        live = jnp.where(i >= 1, 1.0, 0.0)
        err = (nx * gg - t_ref[...]) * live
        loss_ref[...] += jnp.sum(err * err, axis=0, keepdims=True)
        dy = err * (1.0 / d)
        dn = dy * gg
        dh_ref[...] = r * (dn - nx * jnp.mean(dn * nx, axis=1, keepdims=True))
        dg_ref[...] += jnp.sum(dy * nx, axis=0, keepdims=True)

    vec = pl.BlockSpec((1, d), lambda b, i: (0, 0))
    return pl.pallas_call(
        body, name="final_loss", grid=(nbatch, nb),
        in_specs=[pl.BlockSpec((BLK, d), lambda b, i: (b * nb + i, 0)),
                  pl.BlockSpec((BLK, d), lambda b, i: (b * (nb - 1) + jnp.maximum(i - 1, 0), 0)),
                  vec],
        out_specs=[pl.BlockSpec((BLK, d), lambda b, i: (b * nb + i, 0)), vec, vec],
        out_shape=[jax.ShapeDtypeStruct((t, d), F32), jax.ShapeDtypeStruct((1, d), F32),
                   jax.ShapeDtypeStruct((1, d), F32)],
        compiler_params=_params(("arbitrary", "arbitrary")),
    )(h2, tgt, g)


def _direct_exchange(srcs, dsts, send_sems, recv_sems, local_sems, chunked):
    x, y, c = lax.axis_index("x"), lax.axis_index("y"), lax.axis_index("c")
    me = 4 * x + 2 * y + c
    arrays = range(len(srcs))
    local = [pltpu.make_async_copy(srcs[a].at[me] if chunked else srcs[a], dsts[a].at[me], local_sems.at[a])
             for a in arrays]
    sends, recvs = [], []
    for d in range(1, N_DEV):
        px = x + ((d >> 2) & 1) - 2 * x * ((d >> 2) & 1)
        py = y + ((d >> 1) & 1) - 2 * y * ((d >> 1) & 1)
        pc = c + (d & 1) - 2 * c * (d & 1)
        pid = 4 * px + 2 * py + pc
        for a in arrays:
            kw = dict(send_sem=send_sems.at[d - 1, a], recv_sem=recv_sems.at[d - 1, a],
                      device_id=(px, py, pc), device_id_type=pl.DeviceIdType.MESH)
            src = srcs[a].at[pid] if chunked else srcs[a]
            sends.append(pltpu.make_async_remote_copy(src_ref=src, dst_ref=dsts[a].at[me], **kw))
            recvs.append(pltpu.make_async_remote_copy(src_ref=src, dst_ref=dsts[a].at[pid], **kw))
    return local, sends, recvs


def _exchange_beside(first, last, srcs, dsts, sems, chunked):
    local, sends, recvs = _direct_exchange(srcs, dsts, *sems, chunked)

    @pl.when(first)
    def _():
        for cp in local + sends:
            cp.start()

    @pl.when(last)
    def _():
        for cp in local:
            cp.wait()
        for cp in recvs:
            cp.wait_recv()
        for cp in sends:
            cp.wait_send()


def _exchange_sems(n):
    return [pltpu.SemaphoreType.DMA((7, n)), pltpu.SemaphoreType.DMA((7, n)), pltpu.SemaphoreType.DMA((n,))]


def _sb_logits(z):
    log_beta = jnp.minimum(z, 0.0) - jnp.log(1.0 + jnp.exp(-jnp.abs(z)))
    return log_beta, log_beta - z


def _tri(width, after):
    j = lax.broadcasted_iota(jnp.int32, (width, width), 0)
    s = lax.broadcasted_iota(jnp.int32, (width, width), 1)
    return (j > s) if after else (j < s)


def _tri2(tri, with_ones):
    m = tri.astype(BF16)
    if with_ones:
        m = jnp.concatenate([m, jnp.ones((tri.shape[0], BLK), BF16)], axis=1)
    return jnp.concatenate([m, m], axis=0)


def _block_sums(x, tri_ones, after):
    hi, lo = _split(x)
    subs = [_dot(jnp.concatenate([hi[:, s:s + BLK], lo[:, s:s + BLK]], axis=1), tri_ones)
            for s in range(0, x.shape[1], BLK)]
    if len(subs) == 1:
        return subs[0][:, :BLK], subs[0][:, BLK:]
    first, second = subs
    total = first[:, BLK:] + second[:, BLK:]
    if after:
        return jnp.concatenate([first[:, :BLK] + second[:, BLK:], second[:, :BLK]], axis=1), total
    return jnp.concatenate([first[:, :BLK], second[:, :BLK] + first[:, BLK:]], axis=1), total


def _head_masked(x, lane, scale=None):
    out = []
    for h in range(8):
        xp = x[:, (h // 2) * BLK:(h // 2 + 1) * BLK]
        xm = jnp.where((lane >= HALF) if h % 2 else (lane < HALF), xp, jnp.zeros_like(xp))
        out.append(xm if scale is None else xm * scale)
    return jnp.concatenate(out, axis=1)


def _sb_fwd(proj, shards, nbatch, nb):
    lp = nb * BLK
    t = nbatch * lp

    ns = len(shards)

    def body(*refs):
        q_ref, k_ref, v_ref, g_ref = refs[:4]
        shard_refs = refs[4:4 + ns]
        o_ref, og_ref, cm_ref = refs[4 + ns:7 + ns]
        gathered_refs = refs[7 + ns:7 + 2 * ns]
        c_scr, qm_scr = refs[7 + 2 * ns:9 + 2 * ns]
        b = pl.program_id(0)
        i = pl.program_id(1)
        _exchange_beside((b == 0) & (i == 0), (b == nbatch - 1) & (i == nb - 1),
                         shard_refs, gathered_refs, refs[9 + 2 * ns:], chunked=False)
        _, lane = _iotas()
        lo_m = lane < HALF
        cm_ref[...] = jnp.zeros_like(cm_ref)
        c_scr[...] = jnp.zeros_like(c_scr)
        o_ref[...] = jnp.zeros_like(o_ref)
        qm_scr[...] = _head_masked(q_ref[...], lane, SB_SCALE)

        def block(off, width):
            mask = _key_mask(i, off, width, strict=True)
            upper = _tri2(_tri(BLK, after=True), True)
            onehot = lane == off // WIDE
            heads = range(8)
            hcs = [slice(h * BLK, (h + 1) * BLK) for h in heads]
            kbs = [k_ref[pl.ds(off, width), hc] for hc in hcs[:4]]
            vbs = [v_ref[pl.ds(off, width), hc] for hc in hcs[:4]]
            zs = [_dot_nt(qm_scr[:, hcs[h]], kbs[h // 2]) for h in heads]
            lbs, l1s = [], []
            for h in heads:
                log_beta, log_1m = _sb_logits(zs[h])
                lbs.append(log_beta)
                l1s.append(jnp.where(mask, log_1m, 0.0))
            css = [_block_sums(l1s[h], upper, after=True) for h in heads]
            avs = []
            for h in heads:
                c = c_scr[h]
                avs.append(jnp.where(mask, jnp.exp(lbs[h] + css[h][0] + _widen(c, width)), 0.0).astype(BF16))
                if width == WIDE:
                    cm_ref[:, hcs[h]] = jnp.where(onehot, c, cm_ref[:, hcs[h]])
                c_scr[h] = c + css[h][1]
            accs = [_dot(avs[h], vbs[h // 2]) for h in heads]
            for p in range(4):
                o_ref[:, hcs[p]] += jnp.where(lo_m, accs[2 * p], accs[2 * p + 1])

        _over_key_blocks(i + 1, block, reverse=True)
        g = g_ref[...].astype(F32)
        og_ref[...] = (o_ref[...] * g * _sigmoid(g)).astype(og_ref.dtype)

    tile = lambda col: pl.BlockSpec((BLK, 512), lambda b, i: (b * nb + i, col))
    full = lambda col: pl.BlockSpec((lp, 512), lambda b, i: (b, col))
    hbm = pl.BlockSpec(memory_space=pl.ANY)
    outs = pl.pallas_call(
        body, name="sb_fwd", grid=(nbatch, nb),
        in_specs=[tile(0), full(1), full(2), tile(3)] + [hbm] * ns,
        out_specs=[tile(0), tile(0), pl.BlockSpec((BLK, 1024), lambda b, i: (b * nb + i, 0))] + [hbm] * ns,
        out_shape=[jax.ShapeDtypeStruct((t, 512), F32), jax.ShapeDtypeStruct((t, 512), BF16),
                   jax.ShapeDtypeStruct((t, 1024), F32)]
        + [jax.ShapeDtypeStruct((N_DEV,) + s.shape, s.dtype) for s in shards],
        scratch_shapes=[pltpu.VMEM((8, BLK, BLK), F32), pltpu.VMEM((BLK, 1024), BF16)] + _exchange_sems(ns),
        compiler_params=_params(("arbitrary", "arbitrary"), 48),
    )(proj, proj, proj, proj, *shards)
    return outs[0], outs[1], outs[2], list(outs[3:])


def _sb_bwd(proj, o, dog, cm, chunked, nbatch, nb):
    lp = nb * BLK
    t = nbatch * lp
    ns = len(chunked)

    def body(*refs):
        q_ref, k_ref, v_ref, g_ref, o_ref, dog_ref, cm_ref = refs[:7]
        chunk_refs = refs[7:7 + ns]
        dq_ref, dk_ref, dv_ref, dg_ref = refs[7 + ns:11 + ns]
        received_refs = refs[11 + ns:11 + 2 * ns]
        c_scr, qm_scr, dom_scr, dq_scr = refs[11 + 2 * ns:15 + 2 * ns]
        b = pl.program_id(0)
        i = pl.program_id(1)
        _exchange_beside((b == 0) & (i == 0), (b == nbatch - 1) & (i == nb - 1),
                         chunk_refs, received_refs, refs[15 + 2 * ns:], chunked=True)

        @pl.when(i == 0)
        def _():
            dk_ref[...] = jnp.zeros_like(dk_ref)
            dv_ref[...] = jnp.zeros_like(dv_ref)

        _, lane = _iotas()
        lo_m = lane < HALF
        g = g_ref[...].astype(F32)
        sig = _sigmoid(g)
        dog_v = dog_ref[...]
        dg_ref[...] = (dog_v * o_ref[...] * (sig * (1.0 + g * (1.0 - sig)))).astype(dg_ref.dtype)
        dom_scr[...] = _head_masked((dog_v * g * sig).astype(BF16), lane)
        qm_scr[...] = _head_masked(q_ref[...], lane, SB_SCALE)
        c_scr[...] = jnp.zeros_like(c_scr)
        dq_scr[...] = jnp.zeros_like(dq_scr)

        def block(off, width):
            mask = _key_mask(i, off, width, strict=True)
            upper = _tri2(_tri(BLK, after=True), True)
            lower = _tri2(_tri(BLK, after=False), True)
            onehot = lane == off // WIDE
            hcs = [slice(h * BLK, (h + 1) * BLK) for h in range(8)]
            kbs = [k_ref[pl.ds(off, width), hc] for hc in hcs[:4]]
            vbs = [v_ref[pl.ds(off, width), hc] for hc in hcs[:4]]
            for heads in HEAD_GROUPS:
                zs = {h: _dot_nt(qm_scr[:, hcs[h]], kbs[h // 2]) for h in heads}
                dps = {h: _dot_nt(dom_scr[:, hcs[h]], vbs[h // 2]) for h in heads}
                lbs, l1s = {}, {}
                for h in heads:
                    lbs[h], l1s[h] = _sb_logits(zs[h])
                sufs = {h: _block_sums(jnp.where(mask, l1s[h], 0.0), upper, after=True)[0] for h in heads}
                prs, dzzs = {}, {}
                for h in heads:
                    expo = lbs[h] + sufs[h]
                    if width == WIDE:
                        expo = expo + jnp.sum(jnp.where(onehot, cm_ref[:, hcs[h]], 0.0), axis=1, keepdims=True)
                    pr = jnp.where(mask, jnp.exp(expo), 0.0)
                    dzzs[h] = pr * dps[h]
                    prs[h] = pr.astype(BF16)
                css = {h: _block_sums(dzzs[h], lower, after=False) for h in heads}
                dzbs = {}
                for h in heads:
                    c2 = c_scr[h]
                    prefix = css[h][0] + _widen(c2, width)
                    dz = jnp.where(mask, dzzs[h] * jnp.exp(l1s[h]) - jnp.exp(lbs[h]) * prefix, 0.0)
                    dzbs[h] = dz.astype(BF16)
                    c_scr[h] = c2 + css[h][1]
                dqs = {h: _dot(dzbs[h], kbs[h // 2]) for h in heads}
                dks = {h: _dot_tn(dzbs[h], qm_scr[:, hcs[h]]) for h in heads}
                dvs = {h: _dot_tn(prs[h], dom_scr[:, hcs[h]]) for h in heads}
                for p in sorted({h // 2 for h in heads}):
                    dq_scr[:, hcs[p]] += jnp.where(lo_m, dqs[2 * p], dqs[2 * p + 1])
                    dk_ref[pl.ds(off, width), hcs[p]] += dks[2 * p] + dks[2 * p + 1]
                    dv_ref[pl.ds(off, width), hcs[p]] += dvs[2 * p] + dvs[2 * p + 1]

        _over_key_blocks(i + 1, block, reverse=False)
        dq_ref[...] = (dq_scr[...] * SB_SCALE).astype(dq_ref.dtype)

    tile = lambda col: pl.BlockSpec((BLK, 512), lambda b, i: (b * nb + i, col))
    full = lambda col: pl.BlockSpec((lp, 512), lambda b, i: (b, col))
    acc = jax.ShapeDtypeStruct((t, 512), F32)
    once = jax.ShapeDtypeStruct((t, 512), BF16)
    hbm = pl.BlockSpec(memory_space=pl.ANY)
    outs = pl.pallas_call(
        body, name="sb_bwd", grid=(nbatch, nb),
        in_specs=[tile(0), full(1), full(2), tile(3), tile(0), tile(0),
                  pl.BlockSpec((BLK, 1024), lambda b, i: (b * nb + i, 0))] + [hbm] * ns,
        out_specs=[tile(0), full(0), full(0), tile(0)] + [hbm] * ns,
        out_shape=[once, acc, acc, once] + [jax.ShapeDtypeStruct(a.shape, a.dtype) for a in chunked],
        scratch_shapes=[pltpu.VMEM((8, BLK, BLK), F32), pltpu.VMEM((BLK, 1024), BF16),
                        pltpu.VMEM((BLK, 1024), BF16), pltpu.VMEM((BLK, 512), F32)] + _exchange_sems(ns),
        compiler_params=_params(("arbitrary", "arbitrary"), 56),
    )(proj, proj, proj, proj, o, dog, cm, *chunked)
    return outs[0], outs[1], outs[2], outs[3], list(outs[4:])


def _mla_prep(proj, gq, gkv, tabs, nbatch, nb):
    t = proj.shape[0]

    def body(mid_ref, gq_ref, gkv_ref, c_ref, s1_ref, s2_ref, cq_ref, ckv_ref, kr_ref):
        cq = mid_ref[:, 0:256].astype(F32)
        r = lax.rsqrt(jnp.mean(cq * cq, axis=1, keepdims=True) + NORM_EPS)
        cq_ref[...] = (cq * r * gq_ref[...]).astype(BF16)
        ckv = mid_ref[:, 256:384].astype(F32)
        r = lax.rsqrt(jnp.mean(ckv * ckv, axis=1, keepdims=True) + NORM_EPS)
        ckv_ref[...] = (ckv * r * gkv_ref[...]).astype(BF16)
        kr = mid_ref[:, 384:512].astype(F32)
        kr_ref[...] = _rope(kr, c_ref[...], s1_ref[...], s2_ref[...]).astype(BF16)

    tab = pl.BlockSpec((BLK, BLK), lambda b, i: (i, 0))
    rowspec = lambda w: pl.BlockSpec((BLK, w), lambda b, i: (b * nb + i, 0))
    return pl.pallas_call(
        body, name="mla_prep", grid=(nbatch, nb),
        in_specs=[pl.BlockSpec((BLK, 512), lambda b, i: (b * nb + i, EV_MID_BLK)),
                  pl.BlockSpec((1, 256), lambda b, i: (0, 0)), pl.BlockSpec((1, 128), lambda b, i: (0, 0)),
                  tab, tab, tab],
        out_specs=[rowspec(256), rowspec(128), rowspec(128)],
        out_shape=[jax.ShapeDtypeStruct((t, 256), BF16), jax.ShapeDtypeStruct((t, 128), BF16),
                   jax.ShapeDtypeStruct((t, 128), BF16)],
        compiler_params=_params(("parallel", "parallel")),
    )(proj, gq, gkv, *tabs)


def _mla_prep_bwd(proj, gq, gkv, tabs, dcqn, dckvn, dkrot, nbatch, nb):
    t = proj.shape[0]

    def body(mid_ref, gq_ref, gkv_ref, c_ref, s1_ref, s2_ref, dcq_ref, dckv_ref, dkr_ref,
             dmid_ref, dgq_ref, dgkv_ref):
        @pl.when((pl.program_id(0) == 0) & (pl.program_id(1) == 0))
        def _():
            dgq_ref[...] = jnp.zeros_like(dgq_ref)
            dgkv_ref[...] = jnp.zeros_like(dgkv_ref)

        def norm_bwd(x, gain, dy, dgain_ref):
            r = lax.rsqrt(jnp.mean(x * x, axis=1, keepdims=True) + NORM_EPS)
            nx = x * r
            dn = dy * gain
            dgain_ref[...] += jnp.sum(dy * nx, axis=0, keepdims=True)
            return r * (dn - nx * jnp.mean(dn * nx, axis=1, keepdims=True))

        dmid_ref[:, 0:256] = norm_bwd(
            mid_ref[:, 0:256].astype(F32), gq_ref[...], dcq_ref[...], dgq_ref).astype(BF16)
        dmid_ref[:, 256:384] = norm_bwd(
            mid_ref[:, 256:384].astype(F32), gkv_ref[...], dckv_ref[...], dgkv_ref).astype(BF16)
        dmid_ref[:, 384:512] = _rope_t(dkr_ref[...], c_ref[...], s1_ref[...], s2_ref[...]).astype(BF16)

    tab = pl.BlockSpec((BLK, BLK), lambda b, i: (i, 0))
    rowspec = lambda w: pl.BlockSpec((BLK, w), lambda b, i: (b * nb + i, 0))
    vq = pl.BlockSpec((1, 256), lambda b, i: (0, 0))
    vkv = pl.BlockSpec((1, 128), lambda b, i: (0, 0))
    return pl.pallas_call(
        body, name="mla_prep_bwd", grid=(nbatch, nb),
        in_specs=[pl.BlockSpec((BLK, 512), lambda b, i: (b * nb + i, EV_MID_BLK)), vq, vkv, tab, tab, tab,
                  rowspec(256), rowspec(128), rowspec(128)],
        out_specs=[rowspec(512), vq, vkv],
        out_shape=[jax.ShapeDtypeStruct((t, 512), BF16), jax.ShapeDtypeStruct((1, 256), F32),
                   jax.ShapeDtypeStruct((1, 128), F32)],
        compiler_params=_params(("arbitrary", "arbitrary")),
    )(proj, gq, gkv, *tabs, dcqn, dckvn, dkrot)


def _mla_scores(qf, kvb, krb, mask, lo_m):
    kf = jnp.where(lo_m, kvb, krb)
    return kf, jnp.where(mask, _dot_nt(qf, kf), NEG)


def _mla_fwd(qh, kvh, krot, proj, tabs, nbatch, nb):
    lp = nb * BLK
    t = nbatch * lp

    def body(q_ref, kv_ref, kr_ref, g_ref, c_ref, s1_ref, s2_ref, o_ref, og_ref, lse_ref,
             qf_scr, m_scr, l_scr, acc_scr):
        i = pl.program_id(1)
        row, lane = _iotas()
        lo_m = lane < HALF
        for h in range(8):
            hc = slice(h * BLK, (h + 1) * BLK)
            qf_scr[:, hc] = (_rope(q_ref[:, hc].astype(F32), c_ref[...], s1_ref[...], s2_ref[...])
                             * MLA_SCALE).astype(BF16)
        m_scr[...] = jnp.full(m_scr.shape, NEG, F32)
        l_scr[...] = jnp.zeros_like(l_scr)
        acc_scr[...] = jnp.zeros_like(acc_scr)

        def block(off, width):
            mask = _key_mask(i, off, width, strict=False)
            lo_k = lax.broadcasted_iota(jnp.int32, (width, BLK), 1) < HALF
            ones = jnp.ones((width, BLK), BF16)
            krb = kr_ref[pl.ds(off, width), :]
            heads = range(8)
            hcs = [slice(h * BLK, (h + 1) * BLK) for h in heads]
            kvbs = [kv_ref[pl.ds(off, width), hc] for hc in hcs]
            ss = [_mla_scores(qf_scr[:, hcs[h]], kvbs[h], krb, mask, lo_k)[1] for h in heads]
            ps, alphas = [], []
            for h in heads:
                m = m_scr[h]
                m2 = jnp.maximum(m, jnp.max(ss[h], axis=1, keepdims=True))
                ps.append(jnp.exp(ss[h] - _widen(m2, width)).astype(BF16))
                alphas.append(jnp.exp(m - m2))
                m_scr[h] = m2
            pvs = [_dot(ps[h], jnp.concatenate([kvbs[h], ones], axis=1)) for h in heads]
            for h in heads:
                l_scr[h] = alphas[h] * l_scr[h] + pvs[h][:, BLK:]
                acc_scr[h] = alphas[h] * acc_scr[h] + pvs[h][:, :BLK]

        _over_key_blocks(i + 1, block, reverse=False)
        lse = jnp.zeros((BLK, BLK), F32)
        for p in range(4):
            pc = slice(p * BLK, (p + 1) * BLK)
            o0 = acc_scr[2 * p] / l_scr[2 * p]
            o1 = acc_scr[2 * p + 1] / l_scr[2 * p + 1]
            o_ref[:, pc] = jnp.where(lo_m, pltpu.roll(o0, HALF, 1), o1)
            for h in (2 * p, 2 * p + 1):
                lse = lse + jnp.where(lane == h, m_scr[h] + jnp.log(l_scr[h]), 0.0)
        lse_ref[...] = lse
        g = g_ref[...].astype(F32)
        og_ref[...] = (o_ref[...] * g * _sigmoid(g)).astype(og_ref.dtype)

    tile = pl.BlockSpec((BLK, 512), lambda b, i: (b * nb + i, 0))
    tab = pl.BlockSpec((BLK, BLK), lambda b, i: (i, 0))
    heads = pltpu.VMEM((8, BLK, BLK), F32)
    return pl.pallas_call(
        body, name="mla_fwd", grid=(nbatch, nb),
        in_specs=[pl.BlockSpec((BLK, 1024), lambda b, i: (b * nb + i, 0)),
                  pl.BlockSpec((lp, 1024), lambda b, i: (b, 0)),
                  pl.BlockSpec((lp, BLK), lambda b, i: (b, 0)),
                  pl.BlockSpec((BLK, 512), lambda b, i: (b * nb + i, EV_GMLA_BLK // 4)),
                  tab, tab, tab],
        out_specs=[tile, tile, pl.BlockSpec((BLK, BLK), lambda b, i: (b * nb + i, 0))],
        out_shape=[jax.ShapeDtypeStruct((t, 512), F32), jax.ShapeDtypeStruct((t, 512), BF16),
                   jax.ShapeDtypeStruct((t, BLK), F32)],
        scratch_shapes=[pltpu.VMEM((BLK, 1024), BF16), heads, heads, heads],
        compiler_params=_params(("parallel", "arbitrary"), 48),
    )(qh, kvh, krot, proj, *tabs)


def _mla_bwd(qh, kvh, krot, proj, tabs, o, dog, lse, nbatch, nb):
    lp = nb * BLK
    t = nbatch * lp

    def body(q_ref, kv_ref, kr_ref, g_ref, c_ref, s1_ref, s2_ref, o_ref, dog_ref, lse_ref,
             dq_ref, dkv_ref, dkr_ref, dg_ref, qf_scr, do_scr, stat_scr, acc_scr):
        i = pl.program_id(1)

        @pl.when(i == 0)
        def _():
            dkv_ref[...] = jnp.zeros_like(dkv_ref)
            dkr_ref[...] = jnp.zeros_like(dkr_ref)

        row, lane = _iotas()
        lo_m = lane < HALF
        g = g_ref[...].astype(F32)
        sig = _sigmoid(g)
        dog_v = dog_ref[...]
        o_v = o_ref[...]
        dg_ref[...] = (dog_v * o_v * (sig * (1.0 + g * (1.0 - sig)))).astype(dg_ref.dtype)
        do = dog_v * g * sig
        do_o = do * o_v
        lse_blk = lse_ref[...]
        zero = jnp.zeros((BLK, BLK), F32)
        for h in range(8):
            hc = slice(h * BLK, (h + 1) * BLK)
            pc = slice((h // 2) * BLK, (h // 2 + 1) * BLK)
            qf_scr[:, hc] = (_rope(q_ref[:, hc].astype(F32), c_ref[...], s1_ref[...], s2_ref[...])
                             * MLA_SCALE).astype(BF16)
            dop = do[:, pc]
            do_src = dop if h % 2 else pltpu.roll(dop, HALF, 1)
            do_scr[:, hc] = jnp.where(lo_m, 0.0, do_src).astype(BF16)
            hm = (lane >= HALF) if h % 2 else lo_m
            stat_scr[h] = zero + jnp.sum(jnp.where(hm, do_o[:, pc], 0.0), axis=1, keepdims=True)
            stat_scr[8 + h] = zero + jnp.sum(jnp.where(lane == h, lse_blk, 0.0), axis=1, keepdims=True)
        acc_scr[...] = jnp.zeros_like(acc_scr)

        def block(off, width):
            mask = _key_mask(i, off, width, strict=False)
            lo_k = lax.broadcasted_iota(jnp.int32, (width, BLK), 1) < HALF
            krb = kr_ref[pl.ds(off, width), :]
            heads = range(8)
            hcs = [slice(h * BLK, (h + 1) * BLK) for h in heads]
            kvbs = [kv_ref[pl.ds(off, width), hc] for hc in hcs]
            qfs = [qf_scr[:, hc] for hc in hcs]
            dos = [do_scr[:, hc] for hc in hcs]
            scored = [_mla_scores(qfs[h], kvbs[h], krb, mask, lo_k) for h in heads]
            dps = [_dot_nt(dos[h], kvbs[h]) for h in heads]
            pbs, dss = [], []
            for h in heads:
                p = jnp.exp(scored[h][1] - _widen(stat_scr[8 + h], width))
                pbs.append(p.astype(BF16))
                dss.append((p * (dps[h] - _widen(stat_scr[h], width))).astype(BF16))
            dqs = [_dot(dss[h], scored[h][0]) for h in heads]
            dkfs = [_dot_tn(dss[h], qfs[h]) for h in heads]
            dvvs = [_dot_tn(pbs[h], dos[h]) for h in heads]
            dkr = jnp.zeros((width, BLK), F32)
            for h in heads:
                acc_scr[h] += dqs[h]
                dkv_ref[pl.ds(off, width), hcs[h]] += jnp.where(lo_k, dkfs[h], 0.0) + dvvs[h]
                dkr = dkr + jnp.where(lo_k, 0.0, dkfs[h])
            dkr_ref[pl.ds(off, width), :] += dkr

        _over_key_blocks(i + 1, block, reverse=False)
        for h in range(8):
            hc = slice(h * BLK, (h + 1) * BLK)
            dq_ref[:, hc] = _rope_t(acc_scr[h] * MLA_SCALE, c_ref[...], s1_ref[...], s2_ref[...]).astype(dq_ref.dtype)

    tile = lambda col: pl.BlockSpec((BLK, 512), lambda b, i: (b * nb + i, col))
    wide = pl.BlockSpec((BLK, 1024), lambda b, i: (b * nb + i, 0))
    full8 = pl.BlockSpec((lp, 1024), lambda b, i: (b, 0))
    full1 = pl.BlockSpec((lp, BLK), lambda b, i: (b, 0))
    tab = pl.BlockSpec((BLK, BLK), lambda b, i: (i, 0))
    return pl.pallas_call(
        body, name="mla_bwd", grid=(nbatch, nb),
        in_specs=[wide, full8, full1, tile(EV_GMLA_BLK // 4), tab, tab, tab, tile(0), tile(1),
                  pl.BlockSpec((BLK, BLK), lambda b, i: (b * nb + i, 0))],
        out_specs=[wide, full8, full1, tile(0)],
        out_shape=[jax.ShapeDtypeStruct((t, 1024), BF16), jax.ShapeDtypeStruct((t, 1024), F32),
                   jax.ShapeDtypeStruct((t, 128), F32), jax.ShapeDtypeStruct((t, 512), BF16)],
        scratch_shapes=[pltpu.VMEM((BLK, 1024), BF16), pltpu.VMEM((BLK, 1024), BF16),
                        pltpu.VMEM((16, BLK, BLK), F32), pltpu.VMEM((8, BLK, BLK), F32)],
        compiler_params=_params(("parallel", "arbitrary"), 56),
    )(qh, kvh, krot, proj, *tabs, o, dog, lse)


def _swa_setup(kk, i, k_refs, v_refs):
    row, lane = _iotas()
    own = (lane >= kk * HALF) & (lane < (kk + 1) * HALF)

    def dup(ref):
        x = ref[...].astype(F32)
        return jnp.where(own, x, pltpu.roll(x, HALF, 1)).astype(BF16)

    kcat = jnp.concatenate([dup(r) for r in k_refs], axis=0)
    vcat = jnp.concatenate([dup(r) for r in v_refs], axis=0)
    row2 = lax.broadcasted_iota(jnp.int32, (BLK, 2 * BLK), 0)
    lane2 = lax.broadcasted_iota(jnp.int32, (BLK, 2 * BLK), 1)
    is_meta = lane2 < BLK
    in_own = lane2 - BLK <= row2
    k_pos = jnp.where(is_meta, lane2, jnp.where(in_own, (i - 1) * BLK, (i - 2) * BLK) + lane2)
    d = i * BLK + row2 - k_pos
    mask = (d >= 0) & (k_pos >= jnp.where(is_meta, N_PAD, BLK))
    return lane, lane <= row, own, kcat, vcat, mask, d.astype(F32)


def _swa_fold(x, in_own):
    return jnp.concatenate([x[:, :BLK], jnp.where(in_own, x[:, 2 * BLK:], x[:, BLK:2 * BLK])], axis=1)


def _swa_unfold(x, in_own):
    w = x[:, BLK:]
    zero = jnp.zeros_like(w)
    return jnp.concatenate([x[:, :BLK], jnp.where(in_own, zero, w), jnp.where(in_own, w, zero)], axis=1)


def _swa_slope(kk, g_idx):
    return (2.0 ** (-(g_idx + 1) / 2.0)) * jnp.where(kk == 0, 1.0, 1.0 / 16.0)


def _swa_fwd(proj, sinks, nbatch, nb):
    lp = nb * BLK
    t = nbatch * lp

    def body(sink_ref, q_ref, ka, kb, kc, va, vb, vc, g_ref, o_ref, og_ref, lse_ref):
        kk = pl.program_id(1)
        i = pl.program_id(2)
        lane, in_own, own, kcat, vcat, mask, dist = _swa_setup(kk, i, (ka, kb, kc), (va, vb, vc))
        lo_m = lane < HALF
        heads = range(8)
        qms = []
        for h in heads:
            qp = q_ref[:, (h // 2) * BLK:(h // 2 + 1) * BLK]
            qms.append(jnp.where((lane >= HALF) if h % 2 else lo_m, qp, jnp.zeros_like(qp)) * SWA_SCALE)
        qks = [_dot_nt(qms[h], kcat) for h in heads]
        ps, ls, lses = [], [], []
        for h in heads:
            sink = sink_ref[kk, h]
            s = jnp.where(mask, _swa_fold(qks[h], in_own) - _swa_slope(kk, h) * dist, NEG)
            mx = jnp.maximum(jnp.max(s, axis=1, keepdims=True), sink)
            p = jnp.exp(s - mx)
            l = jnp.exp(sink - mx) + jnp.sum(p, axis=1, keepdims=True)
            ps.append(_swa_unfold(p.astype(BF16), in_own))
            ls.append(l)
            lses.append(mx + jnp.log(l))
        pvs = [_dot(ps[h], vcat) for h in heads]
        lse_out = jnp.zeros((BLK, BLK), F32)
        for m in range(4):
            cols = slice(m * BLK, (m + 1) * BLK)
            outp = jnp.where(lo_m, pvs[2 * m] / ls[2 * m], pvs[2 * m + 1] / ls[2 * m + 1])
            o_ref[:, cols] = outp
            g = g_ref[:, cols].astype(F32)
            og_ref[:, cols] = (outp * g * _sigmoid(g)).astype(og_ref.dtype)
            for h in (2 * m, 2 * m + 1):
                lse_out = lse_out + jnp.where(lane == h, lses[h], 0.0)
        lse_ref[...] = lse_out

    def kvspec(col, which):
        if which == 0:
            return pl.BlockSpec((BLK, BLK), lambda b, kk, i: (b * nb, col))
        if which == 1:
            return pl.BlockSpec((BLK, BLK), lambda b, kk, i: (b * nb + jnp.maximum(i - 1, 0), col))
        return pl.BlockSpec((BLK, BLK), lambda b, kk, i: (b * nb + i, col))

    wide = lambda c0: pl.BlockSpec((BLK, 512), lambda b, kk, i: (b * nb + i, c0 + kk))
    return pl.pallas_call(
        body, name="swa_fwd", grid=(nbatch, 2, nb),
        in_specs=[pl.BlockSpec(memory_space=pltpu.SMEM), wide(0),
                  kvspec(OD_K_BLK, 0), kvspec(OD_K_BLK, 1), kvspec(OD_K_BLK, 2),
                  kvspec(OD_V_BLK, 0), kvspec(OD_V_BLK, 1), kvspec(OD_V_BLK, 2), wide(2)],
        out_specs=[wide(0), wide(0), pl.BlockSpec((BLK, BLK), lambda b, kk, i: (b * nb + i, kk))],
        out_shape=[jax.ShapeDtypeStruct((t, 1024), F32), jax.ShapeDtypeStruct((t, 1024), BF16),
                   jax.ShapeDtypeStruct((t, 256), F32)],
        compiler_params=_params(("parallel", "parallel", "arbitrary")),
    )(sinks, proj, proj, proj, proj, proj, proj, proj, proj)


def _swa_bwd(proj, sinks, o, dog, lse, nbatch, nb):
    lp = nb * BLK
    t = nbatch * lp

    def body(sink_ref, q_ref, ka, kb, kc, va, vb, vc, g_ref, o_ref, dog_ref, lse_ref,
             dq_ref, dg_ref, dk_ref, dv_ref, dsink_ref):
        kk = pl.program_id(1)
        i = pl.program_id(2)

        @pl.when((kk == 0) & (i == 0))
        def _():
            dk_ref[...] = jnp.zeros_like(dk_ref)
            dv_ref[...] = jnp.zeros_like(dv_ref)

        @pl.when(i == 0)
        def _():
            dsink_ref[...] = jnp.zeros_like(dsink_ref)

        lane, in_own, own, kcat, vcat, mask, dist = _swa_setup(kk, i, (ka, kb, kc), (va, vb, vc))
        lo_m = lane < HALF
        row8 = lax.broadcasted_iota(jnp.int32, (8, BLK), 0)
        lse_blk = lse_ref[...]
        heads = range(8)
        qms, doms, deltas, lse_hs = [], [], [], []
        for m in range(4):
            cols = slice(m * BLK, (m + 1) * BLK)
            qp = q_ref[:, cols]
            g = g_ref[:, cols].astype(F32)
            sig = _sigmoid(g)
            dog_v = dog_ref[:, cols]
            o_v = o_ref[:, cols]
            dg_ref[:, cols] = (dog_v * o_v * (sig * (1.0 + g * (1.0 - sig)))).astype(dg_ref.dtype)
            do = dog_v * g * sig
            do_o = do * o_v
            dob = do.astype(BF16)
            for h in (2 * m, 2 * m + 1):
                hm = (lane >= HALF) if h % 2 else lo_m
                qms.append(jnp.where(hm, qp, jnp.zeros_like(qp)) * SWA_SCALE)
                doms.append(jnp.where(hm, dob, jnp.zeros_like(dob)))
                deltas.append(jnp.sum(jnp.where(hm, do_o, 0.0), axis=1, keepdims=True))
                lse_hs.append(jnp.sum(jnp.where(lane == h, lse_blk, 0.0), axis=1, keepdims=True))
        qks = [_dot_nt(qms[h], kcat) for h in heads]
        dps = [_dot_nt(doms[h], vcat) for h in heads]
        pbs, dss = [], []
        dsink = jnp.zeros((8, BLK), F32)
        for h in heads:
            s = jnp.where(mask, _swa_fold(qks[h], in_own) - _swa_slope(kk, h) * dist, NEG)
            p = jnp.exp(s - lse_hs[h])
            pbs.append(_swa_unfold(p.astype(BF16), in_own))
            dss.append(_swa_unfold((p * (_swa_fold(dps[h], in_own) - deltas[h])).astype(BF16), in_own))
            tot = jnp.sum(-jnp.exp(sink_ref[kk, h] - lse_hs[h]) * deltas[h], axis=0, keepdims=True)
            dsink = dsink + jnp.where(row8 == h, tot, 0.0)
        dsink_ref[...] += dsink
        dqs = [_dot(dss[h], kcat) for h in heads]
        dks = [_dot_tn(dss[h], qms[h]) for h in heads]
        dvs = [_dot_tn(pbs[h], doms[h]) for h in heads]
        for m in range(4):
            dq_ref[:, m * BLK:(m + 1) * BLK] = (
                jnp.where(lo_m, dqs[2 * m], dqs[2 * m + 1]) * SWA_SCALE).astype(dq_ref.dtype)
        dk = dks[0]
        dv = dvs[0]
        for h in range(1, 8):
            dk = dk + dks[h]
            dv = dv + dvs[h]
        offs = [0, pl.multiple_of(jnp.maximum(i - 1, 0) * BLK, BLK), pl.multiple_of(i * BLK, BLK)]
        for x in range(3):
            rows = slice(x * BLK, (x + 1) * BLK)
            dkx, dvx = dk[rows], dv[rows]
            dk_ref[pl.ds(offs[x], BLK), :] += jnp.where(own, dkx + pltpu.roll(dkx, HALF, 1), 0.0)
            dv_ref[pl.ds(offs[x], BLK), :] += jnp.where(own, dvx + pltpu.roll(dvx, HALF, 1), 0.0)

    def kvspec(col, which):
        if which == 0:
            return pl.BlockSpec((BLK, BLK), lambda b, kk, i: (b * nb, col))
        if which == 1:
            return pl.BlockSpec((BLK, BLK), lambda b, kk, i: (b * nb + jnp.maximum(i - 1, 0), col))
        return pl.BlockSpec((BLK, BLK), lambda b, kk, i: (b * nb + i, col))

    wide = lambda c0: pl.BlockSpec((BLK, 512), lambda b, kk, i: (b * nb + i, c0 + kk))
    full = pl.BlockSpec((lp, BLK), lambda b, kk, i: (b, 0))
    return pl.pallas_call(
        body, name="swa_bwd", grid=(nbatch, 2, nb),
        in_specs=[pl.BlockSpec(memory_space=pltpu.SMEM), wide(0),
                  kvspec(OD_K_BLK, 0), kvspec(OD_K_BLK, 1), kvspec(OD_K_BLK, 2),
                  kvspec(OD_V_BLK, 0), kvspec(OD_V_BLK, 1), kvspec(OD_V_BLK, 2), wide(2),
                  wide(0), wide(0), pl.BlockSpec((BLK, BLK), lambda b, kk, i: (b * nb + i, kk))],
        out_specs=[wide(0), wide(0), full, full,
                   pl.BlockSpec((8, BLK), lambda b, kk, i: (b * 2 + kk, 0))],
        out_shape=[jax.ShapeDtypeStruct((t, 1024), BF16), jax.ShapeDtypeStruct((t, 1024), BF16),
                   jax.ShapeDtypeStruct((t, 128), F32), jax.ShapeDtypeStruct((t, 128), F32),
                   jax.ShapeDtypeStruct((nbatch * 16, BLK), F32)],
        compiler_params=_params(("parallel", "arbitrary", "arbitrary")),
    )(sinks, proj, proj, proj, proj, proj, proj, proj, proj, o, dog, lse)


def _rope_tables(lp):
    pos = (jnp.arange(lp) - N_PAD).astype(F32)
    inv = ROPE_BASE ** (-jnp.arange(16, dtype=F32) / 16.0)
    ang = pos[:, None] * inv[None, :]
    cos, sin = jnp.cos(ang), jnp.sin(ang)
    z16 = jnp.zeros((lp, 16), F32)
    c = jnp.concatenate([jnp.ones((lp, 64), F32), cos, cos, jnp.zeros((lp, 32), F32)], axis=1)
    s1 = jnp.concatenate([jnp.zeros((lp, 64), F32), -sin, z16, jnp.zeros((lp, 32), F32)], axis=1)
    s2 = jnp.concatenate([jnp.zeros((lp, 64), F32), z16, sin, jnp.zeros((lp, 32), F32)], axis=1)
    return c, s1, s2


def _local_step(h0, tgt, norm_g, final_g, gq, gkv, sinks, w_ie, late_shards, nbatch, nb):
    lp = nb * BLK
    tabs = _rope_tables(lp)
    g0, g1 = norm_g[0:1], norm_g[1:2]
    sinks2 = sinks.reshape(2, 8)

    proj_e, hn0 = _norm_mm(h0, g0, w_ie, "proj_even")
    o_sb, og_sb, cm, gathered = _sb_fwd(proj_e, [late_shards[name] for name in _LATE], nbatch, nb)
    full = {name: _unchunk(name, blk) for name, blk in zip(_LATE, gathered)}
    w_uq, w_ukv, w_oe = _uq_to_compute(full["ev_w_uq"]), full["ev_w_ukv"], full["ev_w_out"]
    w_io, w_oo = _od_in_to_compute(full["od_w_in"]), full["od_w_out"]
    cqn, ckvn, krot = _mla_prep(proj_e, gq, gkv, tabs, nbatch, nb)
    qh = _mm(cqn, w_uq, "nn", "mla_uq", out_dtype=BF16)
    kvh = _mm(ckvn, w_ukv, "nn", "mla_ukv", out_dtype=BF16)
    o_mla, og_mla, lse_m = _mla_fwd(qh, kvh, krot, proj_e, tabs, nbatch, nb)
    h1 = _mm([og_sb, og_mla], w_oe, "nn", "out_even", add=h0)
    proj_o, hn1 = _norm_mm(h1, g1, w_io, "proj_odd")
    o_o, og_o, lse_o = _swa_fwd(proj_o, sinks2, nbatch, nb)
    h2 = _mm(og_o, w_oo, "nn", "out_odd", add=h1)
    dh2, lossv, d_final_g = _final(h2, tgt, final_g, nbatch, nb)

    dog_o = _mm(dh2, w_oo, "nt", "d_out_odd")
    d_w_oo, = _mm_tn(og_o, [dh2], "dw_out_odd")
    dq_o, dg_o, dk_o, dv_o, dsink = _swa_bwd(proj_o, sinks2, o_o, dog_o, lse_o, nbatch, nb)
    dproj_o = [dq_o, dg_o, dk_o, dv_o]
    dh1, d_g1 = _mm_norm_bwd(dproj_o, w_io, h1, g1, dh2, "d_proj_odd")
    dw_q, dw_g, dw_k, dw_v = _mm_tn(hn1, dproj_o, "dw_proj_odd")

    dog_e = _mm(dh1, w_oe, "nt", "d_out_even")
    d_w_oe_sb, = _mm_tn(og_sb, [dh1], "dw_out_even_sb")
    d_w_oe_mla, = _mm_tn(og_mla, [dh1], "dw_out_even_mla")
    early = dict(od_w_in=jnp.concatenate([dw_q, dw_k, dw_v, dw_g], axis=1), od_w_out=d_w_oo,
                 ev_w_out=jnp.concatenate([d_w_oe_sb, d_w_oe_mla], axis=0))
    dq_sb, dk_sb, dv_sb, dg_sb, received = _sb_bwd(
        proj_e, o_sb, dog_e, cm, [_chunk(name, early[name]) for name in _EARLY_GRADS], nbatch, nb)
    dqh, dkvh, dkrot, dg_mla = _mla_bwd(qh, kvh, krot, proj_e, tabs, o_mla, dog_e, lse_m, nbatch, nb)
    dcqn = _mm(dqh, w_uq, "nt", "d_mla_uq")
    d_w_uq, = _mm_tn(cqn, [dqh], "dw_mla_uq")
    dckvn = _mm(dkvh, w_ukv, "nt", "d_mla_ukv")
    d_w_ukv, = _mm_tn(ckvn, [dkvh], "dw_mla_ukv")
    dmid, d_gq, d_gkv = _mla_prep_bwd(proj_e, gq, gkv, tabs, dcqn, dckvn, dkrot, nbatch, nb)
    dproj_e = [dq_sb, dk_sb, dv_sb, dg_sb, dmid, dg_mla]
    dh0, d_g0 = _mm_norm_bwd(dproj_e, w_ie, h0, g0, dh1, "d_proj_even")
    dw_e = _mm_tn(hn0, dproj_e, "dw_proj_even")

    d_sinks = dsink.reshape(nbatch, 2, 8, BLK)[:, :, :, 0].sum(axis=0).reshape(1, 16)
    d_norm_g = jnp.concatenate([d_g0, d_g1], axis=0)
    d_ev_w_in = jnp.concatenate(list(dw_e[:4]) + [dw_e[4][:, :384], dw_e[4][:, 448:480], dw_e[5]], axis=1)
    return dict(lossv=lossv, dh0=dh0, norm_g=d_norm_g, final_g=d_final_g, gq=d_gq, gkv=d_gkv, sinks=d_sinks,
                ev_w_in=d_ev_w_in, ev_w_uq=_uq_from_compute(d_w_uq), ev_w_ukv=d_w_ukv, received=received)


def _ev_in_to_compute(w):
    z = lambda n: jnp.zeros((w.shape[0], n), w.dtype)
    return jnp.concatenate([w[:, :2432], z(64), w[:, 2432:2464], z(32), w[:, 2464:]], axis=1)


def _uq_to_compute(w):
    w3 = w.reshape(256, 8, 96)
    return jnp.concatenate([w3, jnp.zeros((256, 8, 32), w.dtype)], axis=2).reshape(256, 1024)


def _uq_from_compute(w):
    return w.reshape(256, 8, 128)[:, :, :96].reshape(256, 768)


def _od_in_to_compute(w):
    return jnp.concatenate([w[:, :1024], w[:, 1280:], w[:, 1024:1280]], axis=1)


_BIG = dict(ev_w_in=(1024, 2976, 1), ev_w_uq=(256, 768, 1), ev_w_ukv=(128, 1024, 1),
            ev_w_out=(1024, 1024, 0), od_w_in=(1024, 2304, 1), od_w_out=(1024, 1024, 0))
_LATE = ("ev_w_uq", "ev_w_ukv", "ev_w_out", "od_w_in", "od_w_out")
_EARLY_GRADS = ("od_w_in", "od_w_out", "ev_w_out")
_LAST_GRADS = ("ev_w_in", "ev_w_uq", "ev_w_ukv")


def _unchunk(name, blk):
    rows, cols, axis = _BIG[name]
    return blk.transpose(1, 0, 2).reshape(rows, cols) if axis == 1 else blk.reshape(rows, cols)


def _chunk(name, g):
    rows, cols, axis = _BIG[name]
    g = g.astype(BF16)
    return g.reshape(rows, N_DEV, cols // N_DEV).transpose(1, 0, 2) if axis == 1 else g.reshape(N_DEV, rows // N_DEV, cols)


def _all_gather(shards, name):
    n = len(shards)

    def body(*refs):
        xs, outs = refs[:n], refs[n:2 * n]
        send_sems, recv_sems, local_sems = refs[2 * n:]
        x, y, c = lax.axis_index("x"), lax.axis_index("y"), lax.axis_index("c")
        me, sibling = (x, y, c), (x, y, 1 - c)
        chips = [(1 - x, y), (x, 1 - y), (1 - x, 1 - y)]
        arrays = range(n)

        def copy(k, a, block, to, from_input=False):
            px, py, pc = block
            dst = outs[a].at[4 * px + 2 * py + pc]
            return pltpu.make_async_remote_copy(
                src_ref=xs[a] if from_input else dst, dst_ref=dst,
                send_sem=send_sems.at[k, a], recv_sem=recv_sems.at[k, a],
                device_id=to, device_id_type=pl.DeviceIdType.MESH)

        mine = [pltpu.make_async_copy(xs[a], outs[a].at[4 * x + 2 * y + c], local_sems.at[a]) for a in arrays]
        for cp in mine:
            cp.start()
        first = [copy(0, a, me, sibling, True) for a in arrays]
        for j, chip in enumerate(chips):
            first += [copy(1 + j, a, me, (*chip, c), True) for a in arrays]
        for cp in first:
            cp.start()
        passed = []
        for j, chip in enumerate(chips):
            for a in arrays:
                copy(1 + j, a, (*chip, c), me).wait_recv()
                passed.append(copy(4 + j, a, (*chip, c), sibling))
                passed[-1].start()
        for a in arrays:
            copy(0, a, sibling, me).wait_recv()
        for j, chip in enumerate(chips):
            for a in arrays:
                copy(4 + j, a, (*chip, 1 - c), me).wait_recv()
        for cp in first + passed:
            cp.wait_send()
        for cp in mine:
            cp.wait()

    hbm = pl.BlockSpec(memory_space=pl.ANY)
    return pl.pallas_call(
        body, name=name,
        out_shape=[jax.ShapeDtypeStruct((N_DEV,) + s.shape, s.dtype) for s in shards],
        in_specs=[hbm] * n, out_specs=[hbm] * n,
        scratch_shapes=[pltpu.SemaphoreType.DMA((7, n)), pltpu.SemaphoreType.DMA((7, n)),
                        pltpu.SemaphoreType.DMA((n,))],
    )(*shards)


def _exchange_sum(chunked, received, name):
    n, m = len(chunked), len(received)
    arrs = list(chunked) + list(received)

    def body(*refs):
        ins, outs = refs[:n + m], refs[n + m:2 * (n + m)]
        bufs = refs[2 * (n + m):3 * (n + m)]
        send_sems, recv_sems, local_sems, load_sems = refs[3 * (n + m):]
        loads = [pltpu.make_async_copy(ins[n + a], bufs[n + a], load_sems.at[a]) for a in range(m)]
        for cp in loads:
            cp.start()
        local, sends, recvs = _direct_exchange(ins[:n], bufs[:n], send_sems, recv_sems, local_sems, chunked=True)
        for cp in local + sends:
            cp.start()
        for a, cp in enumerate(loads):
            cp.wait()
            _sum_slots(bufs[n + a], outs[n + a])
        for cp in local:
            cp.wait()
        for cp in recvs:
            cp.wait_recv()
        for cp in sends:
            cp.wait_send()
        for a in range(n):
            _sum_slots(bufs[a], outs[a])

    hbm = pl.BlockSpec(memory_space=pl.ANY)
    vm = pl.BlockSpec(memory_space=pltpu.VMEM)
    return pl.pallas_call(
        body, name=name,
        out_shape=[jax.ShapeDtypeStruct(a.shape[1:], F32) for a in arrs],
        in_specs=[hbm] * (n + m), out_specs=[vm] * (n + m),
        scratch_shapes=[pltpu.VMEM(a.shape, a.dtype) for a in arrs] + _exchange_sems(n)
        + [pltpu.SemaphoreType.DMA((max(m, 1),))],
        compiler_params=pltpu.CompilerParams(vmem_limit_bytes=48 << 20),
    )(*arrs)


def _sum_slots(buf, out):
    rows = buf.shape[1]

    def add(sl):
        acc = buf[(0,) + sl].astype(F32)
        for k in range(1, N_DEV):
            acc = acc + buf[(k,) + sl].astype(F32)
        out[sl] = acc

    if rows > BLK and rows % BLK == 0:
        def step(r, carry):
            add((pl.ds(pl.multiple_of(r * BLK, BLK), BLK), slice(None)))
            return carry

        lax.fori_loop(0, rows // BLK, step, 0)
    else:
        add((slice(None), slice(None)))


def _adamw(ws, gs, ms, vs):
    n = len(ws)

    def body(*refs):
        ins, outs = refs[:4 * n], refs[4 * n:]
        for k in range(n):
            w_ref, g_ref, m_ref, v_ref = ins[4 * k:4 * k + 4]
            d_ref, nm_ref, nv_ref = outs[3 * k:3 * k + 3]

            def update(sl, w_ref=w_ref, g_ref=g_ref, m_ref=m_ref, v_ref=v_ref,
                       d_ref=d_ref, nm_ref=nm_ref, nv_ref=nv_ref):
                g = g_ref[sl]
                m = ADAM_B1 * m_ref[sl] + (1.0 - ADAM_B1) * g
                v = ADAM_B2 * v_ref[sl] + (1.0 - ADAM_B2) * (g * g)
                m_hat = m / (1.0 - ADAM_B1 ** ADAM_STEP)
                v_hat = v / (1.0 - ADAM_B2 ** ADAM_STEP)
                d_ref[sl] = -ADAM_LR * (m_hat / (jnp.sqrt(v_hat) + ADAM_EPS) + ADAM_WD * w_ref[sl])
                nm_ref[sl] = m
                nv_ref[sl] = v

            rows = w_ref.shape[0]
            if rows > BLK and rows % BLK == 0:
                def step(r, carry, update=update):
                    update((pl.ds(pl.multiple_of(r * BLK, BLK), BLK), slice(None)))
                    return carry

                lax.fori_loop(0, rows // BLK, step, 0)
            else:
                update((slice(None), slice(None)))

    args, out_shape = [], []
    for k in range(n):
        args += [ws[k], gs[k], ms[k], vs[k]]
        out_shape += [jax.ShapeDtypeStruct(ws[k].shape, F32)] * 3
    vm = pl.BlockSpec(memory_space=pltpu.VMEM)
    outs = pl.pallas_call(
        body, name="adamw", out_shape=out_shape,
        in_specs=[vm] * (4 * n), out_specs=[vm] * (3 * n),
        compiler_params=pltpu.CompilerParams(vmem_limit_bytes=48 << 20),
    )(*args)
    return [tuple(outs[3 * k:3 * k + 3]) for k in range(n)]


def kernel(x, meta, norm_g, final_g, ev_w_in, ev_q_norm_g, ev_kv_norm_g, ev_w_uq, ev_w_ukv, ev_w_out, od_w_in, od_sinks, od_w_out, loss_target, m_meta, m_norm_g, m_final_g, m_ev_w_in, m_ev_q_norm_g, m_ev_kv_norm_g, m_ev_w_uq, m_ev_w_ukv, m_ev_w_out, m_od_w_in, m_od_sinks, m_od_w_out, v_meta, v_norm_g, v_final_g, v_ev_w_in, v_ev_q_norm_g, v_ev_kv_norm_g, v_ev_w_uq, v_ev_w_ukv, v_ev_w_out, v_od_w_in, v_od_sinks, v_od_w_out):
    nbatch, seq, d = x.shape
    nb = seq // BLK + 1
    lp = nb * BLK
    shards = dict(ev_w_in=ev_w_in[0], ev_w_uq=ev_w_uq[0], ev_w_ukv=ev_w_ukv[0], ev_w_out=ev_w_out[0],
                  od_w_in=od_w_in[0], od_w_out=od_w_out[0])

    w_ie_blocks, meta_blocks = _all_gather([shards["ev_w_in"].astype(BF16), meta], "gather_weights")
    meta_full = meta_blocks.transpose(1, 0, 2).reshape(N_META, d)

    head = jnp.concatenate([jnp.zeros((N_PAD, d), F32), meta_full], axis=0)
    h0 = jnp.concatenate([jnp.broadcast_to(head[None], (nbatch, BLK, d)), x], axis=1).reshape(nbatch * lp, d)
    grads = _local_step(
        h0, loss_target.reshape(nbatch * seq, d), norm_g, final_g.reshape(1, d), ev_q_norm_g, ev_kv_norm_g,
        od_sinks, _ev_in_to_compute(_unchunk("ev_w_in", w_ie_blocks)),
        {name: shards[name].astype(BF16) for name in _LATE}, nbatch, nb)
    dh0 = grads["dh0"].reshape(nbatch, lp, d)
    grad_x = dh0[:, BLK:]

    d_meta = dh0[:, N_PAD:BLK].sum(axis=0).reshape(N_META, N_DEV, BLK).transpose(1, 0, 2)
    pad = lambda a, n: jnp.concatenate([a.reshape(1, -1), jnp.zeros((1, n - a.size), F32)], axis=1)
    loss_part = (0.5 / d * jnp.sum(grads["lossv"])).reshape(1, 1)
    rep = jnp.concatenate([grads["norm_g"].reshape(1, -1), grads["final_g"], grads["gq"], pad(loss_part, 256),
                           pad(grads["gkv"], 256), pad(grads["sinks"], 256)], axis=1).reshape(32, BLK)
    small = jnp.concatenate([d_meta, jnp.broadcast_to(rep[None], (N_DEV, 32, BLK))], axis=1)
    reduced = _exchange_sum([_chunk(name, grads[name]) for name in _LAST_GRADS] + [small],
                            grads["received"], "reduce_grads")
    g_shard = dict(zip(_LAST_GRADS + ("small",) + _EARLY_GRADS, reduced))
    red_small = g_shard.pop("small")
    rep = red_small[N_META:].reshape(1, -1)
    loss = rep[0, 3 * d + 256]
    g_small = dict(meta=red_small[:N_META], norm_g=rep[:, :2 * d].reshape(2, d), final_g=rep[:, 2 * d:3 * d],
                   ev_q_norm_g=rep[:, 3 * d:3 * d + 256], ev_kv_norm_g=rep[:, 3 * d + 512:3 * d + 640],
                   od_sinks=rep[:, 3 * d + 768:3 * d + 784])

    names = ["meta", "norm_g", "final_g", "ev_w_in", "ev_q_norm_g", "ev_kv_norm_g", "ev_w_uq", "ev_w_ukv",
             "ev_w_out", "od_w_in", "od_sinks", "od_w_out"]
    given = dict(meta=(meta, m_meta, v_meta), norm_g=(norm_g, m_norm_g, v_norm_g),
                 final_g=(final_g, m_final_g, v_final_g), ev_w_in=(ev_w_in, m_ev_w_in, v_ev_w_in),
                 ev_q_norm_g=(ev_q_norm_g, m_ev_q_norm_g, v_ev_q_norm_g),
                 ev_kv_norm_g=(ev_kv_norm_g, m_ev_kv_norm_g, v_ev_kv_norm_g),
                 ev_w_uq=(ev_w_uq, m_ev_w_uq, v_ev_w_uq), ev_w_ukv=(ev_w_ukv, m_ev_w_ukv, v_ev_w_ukv),
                 ev_w_out=(ev_w_out, m_ev_w_out, v_ev_w_out), od_w_in=(od_w_in, m_od_w_in, v_od_w_in),
                 od_sinks=(od_sinks, m_od_sinks, v_od_sinks), od_w_out=(od_w_out, m_od_w_out, v_od_w_out))
    ws, gs, ms, vs = [], [], [], []
    for name in names:
        g2 = g_shard[name] if name in g_shard else g_small[name]
        w, m, v = given[name]
        ws.append(w.reshape(g2.shape))
        ms.append(m.reshape(g2.shape))
        vs.append(v.reshape(g2.shape))
        gs.append(g2)
    upd = _adamw(ws, gs, ms, vs)
    shape_of = {name: given[name][0].shape for name in names}
    grads_out = [gs[k].reshape(shape_of[n]) for k, n in enumerate(names)]
    deltas = [upd[k][0].reshape(shape_of[n]) for k, n in enumerate(names)]
    new_m = [upd[k][1].reshape(shape_of[n]) for k, n in enumerate(names)]
    new_v = [upd[k][2].reshape(shape_of[n]) for k, n in enumerate(names)]
    return (loss, grad_x, *grads_out, *deltas, *new_m, *new_v)
```

```python
import jax
import jax.numpy as jnp
from jax import lax
from jax.experimental import pallas as pl
from jax.experimental.pallas import tpu as pltpu

F32 = jnp.float32
BF16 = jnp.bfloat16

D_MODEL = 1024
N_META = 16
BLK = 128
HALF = 64
N_PAD = BLK - N_META
NORM_EPS = 1e-6
NEG = -1e30
N_DEV = 8

SB_SCALE = 64 ** -0.5
MLA_SCALE = 96 ** -0.5
SWA_SCALE = 64 ** -0.5
ROPE_BASE = 10000.0

EV_IN_PAD = 3072
EV_MID_BLK = 4
EV_GMLA_BLK = 20
OD_K_BLK = 16
OD_V_BLK = 17

ADAM_LR = 0.001
ADAM_B1 = 0.9
ADAM_B2 = 0.999
ADAM_EPS = 1e-08
ADAM_WD = 0.01
ADAM_STEP = 10


def _dot(a, b):
    return lax.dot_general(a, b, (((1,), (0,)), ((), ())), preferred_element_type=F32)


def _dot_nt(a, b):
    return lax.dot_general(a, b, (((1,), (1,)), ((), ())), preferred_element_type=F32)


def _dot_tn(a, b):
    return lax.dot_general(a, b, (((0,), (0,)), ((), ())), preferred_element_type=F32)


def _sigmoid(x):
    return 1.0 / (1.0 + jnp.exp(-x))


def _iotas():
    row = lax.broadcasted_iota(jnp.int32, (BLK, BLK), 0)
    lane = lax.broadcasted_iota(jnp.int32, (BLK, BLK), 1)
    return row, lane


WIDE = 2 * BLK
HEAD_GROUPS = (range(0, 8),)


def _key_mask(i, first_key, width, strict):
    t_pos = i * BLK + lax.broadcasted_iota(jnp.int32, (BLK, width), 0)
    s_pos = first_key + lax.broadcasted_iota(jnp.int32, (BLK, width), 1)
    seen = (s_pos < t_pos) if strict else (s_pos <= t_pos)
    return seen & (s_pos >= N_PAD)


def _widen(x, width):
    return x if width == BLK else jnp.concatenate([x] * (width // BLK), axis=1)


def _over_key_blocks(n, block, reverse):
    pairs = n // 2
    last = pl.multiple_of((n - 1) * BLK, BLK)

    def step(jj, carry):
        jp = (pairs - 1 - jj) if reverse else jj
        block(pl.multiple_of(jp * WIDE, WIDE), WIDE)
        return carry

    if reverse:
        pl.when(n % 2 == 1)(lambda: block(last, BLK))
        lax.fori_loop(0, pairs, step, 0)
    else:
        lax.fori_loop(0, pairs, step, 0)
        pl.when(n % 2 == 1)(lambda: block(last, BLK))


def _rope(x, c, s1, s2):
    return x * c + pltpu.roll(x, BLK - 16, 1) * s1 + pltpu.roll(x, 16, 1) * s2


def _rope_t(x, c, s1, s2):
    return x * c - pltpu.roll(x, BLK - 16, 1) * s1 - pltpu.roll(x, 16, 1) * s2


def _params(sem, vmem_mb=None):
    kw = dict(dimension_semantics=sem)
    if vmem_mb is not None:
        kw["vmem_limit_bytes"] = vmem_mb << 20
    return pltpu.CompilerParams(**kw)


def _row_tile(t, cands):
    for c in cands:
        if t % c == 0:
            return c
    raise ValueError(t)


def _mm(a, w, mode, name, add=None, out_dtype=F32):
    pieces = list(a) if isinstance(a, (list, tuple)) else [a]
    m = pieces[0].shape[0]
    n = w.shape[1] if mode == "nn" else w.shape[0]
    tm = _row_tile(m, (256, 128))
    widths = [p.shape[1] for p in pieces]
    offs = [sum(widths[:i]) for i in range(len(widths))]

    def body(*refs):
        p_refs = refs[:len(pieces)]
        w_ref = refs[len(pieces)]
        o_ref = refs[-1]
        acc = None
        for p_ref, off, wd in zip(p_refs, offs, widths):
            x = p_ref[...].astype(BF16)
            part = _dot(x, w_ref[off:off + wd, :]) if mode == "nn" else _dot_nt(x, w_ref[:, off:off + wd])
            acc = part if acc is None else acc + part
        if add is not None:
            acc = acc + refs[len(pieces) + 1][...]
        o_ref[...] = acc.astype(o_ref.dtype)

    in_specs = [pl.BlockSpec((tm, wd), lambda i: (i, 0)) for wd in widths]
    in_specs.append(pl.BlockSpec(w.shape, lambda i: (0, 0)))
    args = pieces + [w]
    if add is not None:
        in_specs.append(pl.BlockSpec((tm, n), lambda i: (i, 0)))
        args.append(add)
    return pl.pallas_call(
        body, name=name, grid=(m // tm,), in_specs=in_specs,
        out_specs=pl.BlockSpec((tm, n), lambda i: (i, 0)),
        out_shape=jax.ShapeDtypeStruct((m, n), out_dtype),
        compiler_params=_params(("parallel",), 48),
    )(*args)


def _mm_tn(x, pieces, name):
    t, k = x.shape
    tt = _row_tile(t, (544, 256, 128))
    widths = [p.shape[1] for p in pieces]

    def body(*refs):
        x_ref = refs[0]
        d_refs = refs[1:1 + len(pieces)]
        o_refs = refs[1 + len(pieces):]

        @pl.when(pl.program_id(0) == 0)
        def _():
            for o_ref in o_refs:
                o_ref[...] = jnp.zeros_like(o_ref)

        xb = x_ref[...].astype(BF16)
        for d_ref, o_ref in zip(d_refs, o_refs):
            o_ref[...] += _dot_tn(xb, d_ref[...].astype(BF16))

    return pl.pallas_call(
        body, name=name, grid=(t // tt,),
        in_specs=[pl.BlockSpec((tt, k), lambda i: (i, 0))] + [pl.BlockSpec((tt, wd), lambda i: (i, 0)) for wd in widths],
        out_specs=[pl.BlockSpec((k, wd), lambda i: (0, 0)) for wd in widths],
        out_shape=[jax.ShapeDtypeStruct((k, wd), F32) for wd in widths],
        compiler_params=_params(("arbitrary",), 56),
    )(x, *pieces)


def _norm_mm(h, g, w, name):
    t, d = h.shape
    n = w.shape[1]
    tr = _row_tile(t, (256, 128))

    def body(h_ref, g_ref, w_ref, o_ref, hn_ref):
        x = h_ref[...]
        r = lax.rsqrt(jnp.mean(x * x, axis=1, keepdims=True) + NORM_EPS)
        hn = (x * r * g_ref[...]).astype(BF16)
        hn_ref[...] = hn
        o_ref[...] = _dot(hn, w_ref[...]).astype(o_ref.dtype)

    row = lambda width: pl.BlockSpec((tr, width), lambda i: (i, 0))
    return pl.pallas_call(
        body, name=name, grid=(t // tr,),
        in_specs=[row(d), pl.BlockSpec((1, d), lambda i: (0, 0)), pl.BlockSpec(w.shape, lambda i: (0, 0))],
        out_specs=[row(n), row(d)],
        out_shape=[jax.ShapeDtypeStruct((t, n), BF16), jax.ShapeDtypeStruct((t, d), BF16)],
        compiler_params=_params(("parallel",), 48),
    )(h, g, w)


def _mm_norm_bwd(pieces, w, h, g, dres, name):
    t, d = h.shape
    tr = _row_tile(t, (256, 128))
    widths = [p.shape[1] for p in pieces]
    offs = [sum(widths[:i]) for i in range(len(widths))]

    def body(*refs):
        p_refs = refs[:len(pieces)]
        w_ref, h_ref, g_ref, dres_ref, dh_ref, dg_ref = refs[len(pieces):]

        @pl.when(pl.program_id(0) == 0)
        def _():
            dg_ref[...] = jnp.zeros_like(dg_ref)

        dy = None
        for p_ref, off, wd in zip(p_refs, offs, widths):
            part = _dot_nt(p_ref[...].astype(BF16), w_ref[:, off:off + wd])
            dy = part if dy is None else dy + part
        x = h_ref[...]
        r = lax.rsqrt(jnp.mean(x * x, axis=1, keepdims=True) + NORM_EPS)
        nx = x * r
        dn = dy * g_ref[...]
        dh_ref[...] = dres_ref[...] + r * (dn - nx * jnp.mean(dn * nx, axis=1, keepdims=True))
        dg_ref[...] += jnp.sum(dy * nx, axis=0, keepdims=True)

    row = lambda width: pl.BlockSpec((tr, width), lambda i: (i, 0))
    vec = pl.BlockSpec((1, d), lambda i: (0, 0))
    return pl.pallas_call(
        body, name=name, grid=(t // tr,),
        in_specs=[row(wd) for wd in widths] + [pl.BlockSpec(w.shape, lambda i: (0, 0)), row(d), vec, row(d)],
        out_specs=[row(d), vec],
        out_shape=[jax.ShapeDtypeStruct((t, d), F32), jax.ShapeDtypeStruct((1, d), F32)],
        compiler_params=_params(("arbitrary",), 48),
    )(*pieces, w, h, g, dres)


def _final(h2, tgt, g, nbatch, nb):
    t, d = h2.shape

    def body(h_ref, t_ref, g_ref, dh_ref, loss_ref, dg_ref):
        b = pl.program_id(0)
        i = pl.program_id(1)

        @pl.when((b == 0) & (i == 0))
        def _():
            loss_ref[...] = jnp.zeros_like(loss_ref)
            dg_ref[...] = jnp.zeros_like(dg_ref)

        x = h_ref[...]
        r = lax.rsqrt(jnp.mean(x * x, axis=1, keepdims=True) + NORM_EPS)
        nx = x * r
        gg = g_ref[...]
        live = jnp.where(i >= 1, 1.0, 0.0)
        err = (nx * gg - t_ref[...]) * live
        loss_ref[...] += jnp.sum(err * err, axis=0, keepdims=True)
        dy = err * (1.0 / d)
        dn = dy * gg
        dh_ref[...] = r * (dn - nx * jnp.mean(dn * nx, axis=1, keepdims=True))
        dg_ref[...] += jnp.sum(dy * nx, axis=0, keepdims=True)

    vec = pl.BlockSpec((1, d), lambda b, i: (0, 0))
    return pl.pallas_call(
        body, name="final_loss", grid=(nbatch, nb),
        in_specs=[pl.BlockSpec((BLK, d), lambda b, i: (b * nb + i, 0)),
                  pl.BlockSpec((BLK, d), lambda b, i: (b * (nb - 1) + jnp.maximum(i - 1, 0), 0)),
                  vec],
        out_specs=[pl.BlockSpec((BLK, d), lambda b, i: (b * nb + i, 0)), vec, vec],
        out_shape=[jax.ShapeDtypeStruct((t, d), F32), jax.ShapeDtypeStruct((1, d), F32),
                   jax.ShapeDtypeStruct((1, d), F32)],
        compiler_params=_params(("arbitrary", "arbitrary")),
    )(h2, tgt, g)


def _direct_exchange(srcs, dsts, send_sems, recv_sems, local_sems, chunked):
    x, y, c = lax.axis_index("x"), lax.axis_index("y"), lax.axis_index("c")
    me = 4 * x + 2 * y + c
    arrays = range(len(srcs))
    local = [pltpu.make_async_copy(srcs[a].at[me] if chunked else srcs[a], dsts[a].at[me], local_sems.at[a])
             for a in arrays]
    sends, recvs = [], []
    for d in range(1, N_DEV):
        px = x + ((d >> 2) & 1) - 2 * x * ((d >> 2) & 1)
        py = y + ((d >> 1) & 1) - 2 * y * ((d >> 1) & 1)
        pc = c + (d & 1) - 2 * c * (d & 1)
        pid = 4 * px + 2 * py + pc
        for a in arrays:
            kw = dict(send_sem=send_sems.at[d - 1, a], recv_sem=recv_sems.at[d - 1, a],
                      device_id=(px, py, pc), device_id_type=pl.DeviceIdType.MESH)
            src = srcs[a].at[pid] if chunked else srcs[a]
            sends.append(pltpu.make_async_remote_copy(src_ref=src, dst_ref=dsts[a].at[me], **kw))
            recvs.append(pltpu.make_async_remote_copy(src_ref=src, dst_ref=dsts[a].at[pid], **kw))
    return local, sends, recvs


def _exchange_beside(first, last, srcs, dsts, sems, chunked):
    local, sends, recvs = _direct_exchange(srcs, dsts, *sems, chunked)

    @pl.when(first)
    def _():
        for cp in local + sends:
            cp.start()

    @pl.when(last)
    def _():
        for cp in local:
            cp.wait()
        for cp in recvs:
            cp.wait_recv()
        for cp in sends:
            cp.wait_send()


def _exchange_sems(n):
    return [pltpu.SemaphoreType.DMA((7, n)), pltpu.SemaphoreType.DMA((7, n)), pltpu.SemaphoreType.DMA((n,))]


def _sb_logits(z):
    log_beta = jnp.minimum(z, 0.0) - jnp.log(1.0 + jnp.exp(-jnp.abs(z)))
    return log_beta, log_beta - z


def _tri(width, after):
    j = lax.broadcasted_iota(jnp.int32, (width, width), 0)
    s = lax.broadcasted_iota(jnp.int32, (width, width), 1)
    return (j > s) if after else (j < s)


def _tri_ones(tri):
    return jnp.concatenate([tri.astype(BF16), jnp.ones((tri.shape[0], BLK), BF16)], axis=1)


def _block_sums(x, tri_ones, after):
    xb = x.astype(BF16)
    subs = [_dot(xb[:, s:s + BLK], tri_ones) for s in range(0, x.shape[1], BLK)]
    if len(subs) == 1:
        return subs[0][:, :BLK], subs[0][:, BLK:]
    first, second = subs
    total = first[:, BLK:] + second[:, BLK:]
    if after:
        return jnp.concatenate([first[:, :BLK] + second[:, BLK:], second[:, :BLK]], axis=1), total
    return jnp.concatenate([first[:, :BLK], second[:, :BLK] + first[:, BLK:]], axis=1), total


def _head_masked(x, lane, scale=None):
    out = []
    for h in range(8):
        xp = x[:, (h // 2) * BLK:(h // 2 + 1) * BLK]
        xm = jnp.where((lane >= HALF) if h % 2 else (lane < HALF), xp, jnp.zeros_like(xp))
        out.append(xm if scale is None else xm * scale)
    return jnp.concatenate(out, axis=1)


def _sb_fwd(proj, shards, nbatch, nb):
    lp = nb * BLK
    t = nbatch * lp

    ns = len(shards)

    def body(*refs):
        q_ref, k_ref, v_ref, g_ref = refs[:4]
        shard_refs = refs[4:4 + ns]
        o_ref, og_ref, cm_ref = refs[4 + ns:7 + ns]
        gathered_refs = refs[7 + ns:7 + 2 * ns]
        c_scr, qm_scr = refs[7 + 2 * ns:9 + 2 * ns]
        b = pl.program_id(0)
        i = pl.program_id(1)
        _exchange_beside((b == 0) & (i == 0), (b == nbatch - 1) & (i == nb - 1),
                         shard_refs, gathered_refs, refs[9 + 2 * ns:], chunked=False)
        _, lane = _iotas()
        lo_m = lane < HALF
        cm_ref[...] = jnp.zeros_like(cm_ref)
        c_scr[...] = jnp.zeros_like(c_scr)
        o_ref[...] = jnp.zeros_like(o_ref)
        qm_scr[...] = _head_masked(q_ref[...], lane, SB_SCALE)

        def block(off, width):
            mask = _key_mask(i, off, width, strict=True)
            upper = _tri_ones(_tri(BLK, after=True))
            onehot = lane == off // WIDE
            heads = range(8)
            hcs = [slice(h * BLK, (h + 1) * BLK) for h in heads]
            kbs = [k_ref[pl.ds(off, width), hc] for hc in hcs[:4]]
            vbs = [v_ref[pl.ds(off, width), hc] for hc in hcs[:4]]
            zs = [_dot_nt(qm_scr[:, hcs[h]], kbs[h // 2]) for h in heads]
            lbs, l1s = [], []
            for h in heads:
                log_beta, log_1m = _sb_logits(zs[h])
                lbs.append(log_beta)
                l1s.append(jnp.where(mask, log_1m, 0.0))
            css = [_block_sums(l1s[h], upper, after=True) for h in heads]
            avs = []
            for h in heads:
                c = c_scr[h]
                avs.append(jnp.where(mask, jnp.exp(lbs[h] + css[h][0] + _widen(c, width)), 0.0).astype(BF16))
                if width == WIDE:
                    cm_ref[:, hcs[h]] = jnp.where(onehot, c, cm_ref[:, hcs[h]])
                c_scr[h] = c + css[h][1]
            accs = [_dot(avs[h], vbs[h // 2]) for h in heads]
            for p in range(4):
                o_ref[:, hcs[p]] += jnp.where(lo_m, accs[2 * p], accs[2 * p + 1])

        _over_key_blocks(i + 1, block, reverse=True)
        g = g_ref[...].astype(F32)
        og_ref[...] = (o_ref[...] * g * _sigmoid(g)).astype(og_ref.dtype)

    tile = lambda col: pl.BlockSpec((BLK, 512), lambda b, i: (b * nb + i, col))
    full = lambda col: pl.BlockSpec((lp, 512), lambda b, i: (b, col))
    hbm = pl.BlockSpec(memory_space=pl.ANY)
    outs = pl.pallas_call(
        body, name="sb_fwd", grid=(nbatch, nb),
        in_specs=[tile(0), full(1), full(2), tile(3)] + [hbm] * ns,
        out_specs=[tile(0), tile(0), pl.BlockSpec((BLK, 1024), lambda b, i: (b * nb + i, 0))] + [hbm] * ns,
        out_shape=[jax.ShapeDtypeStruct((t, 512), F32), jax.ShapeDtypeStruct((t, 512), BF16),
                   jax.ShapeDtypeStruct((t, 1024), F32)]
        + [jax.ShapeDtypeStruct((N_DEV,) + s.shape, s.dtype) for s in shards],
        scratch_shapes=[pltpu.VMEM((8, BLK, BLK), F32), pltpu.VMEM((BLK, 1024), BF16)] + _exchange_sems(ns),
        compiler_params=_params(("arbitrary", "arbitrary"), 48),
    )(proj, proj, proj, proj, *shards)
    return outs[0], outs[1], outs[2], list(outs[3:])


def _sb_bwd(proj, o, dog, cm, chunked, nbatch, nb):
    lp = nb * BLK
    t = nbatch * lp
    ns = len(chunked)

    def body(*refs):
        q_ref, k_ref, v_ref, g_ref, o_ref, dog_ref, cm_ref = refs[:7]
        chunk_refs = refs[7:7 + ns]
        dq_ref, dk_ref, dv_ref, dg_ref = refs[7 + ns:11 + ns]
        received_refs = refs[11 + ns:11 + 2 * ns]
        c_scr, qm_scr, dom_scr, dq_scr = refs[11 + 2 * ns:15 + 2 * ns]
        b = pl.program_id(0)
        i = pl.program_id(1)
        _exchange_beside((b == 0) & (i == 0), (b == nbatch - 1) & (i == nb - 1),
                         chunk_refs, received_refs, refs[15 + 2 * ns:], chunked=True)

        @pl.when(i == 0)
        def _():
            dk_ref[...] = jnp.zeros_like(dk_ref)
            dv_ref[...] = jnp.zeros_like(dv_ref)

        _, lane = _iotas()
        lo_m = lane < HALF
        g = g_ref[...].astype(F32)
        sig = _sigmoid(g)
        dog_v = dog_ref[...]
        dg_ref[...] = (dog_v * o_ref[...] * (sig * (1.0 + g * (1.0 - sig)))).astype(dg_ref.dtype)
        dom_scr[...] = _head_masked((dog_v * g * sig).astype(BF16), lane)
        qm_scr[...] = _head_masked(q_ref[...], lane, SB_SCALE)
        c_scr[...] = jnp.zeros_like(c_scr)
        dq_scr[...] = jnp.zeros_like(dq_scr)

        def block(off, width):
            mask = _key_mask(i, off, width, strict=True)
            upper = _tri_ones(_tri(BLK, after=True))
            lower = _tri_ones(_tri(BLK, after=False))
            onehot = lane == off // WIDE
            hcs = [slice(h * BLK, (h + 1) * BLK) for h in range(8)]
            kbs = [k_ref[pl.ds(off, width), hc] for hc in hcs[:4]]
            vbs = [v_ref[pl.ds(off, width), hc] for hc in hcs[:4]]
            for heads in HEAD_GROUPS:
                zs = {h: _dot_nt(qm_scr[:, hcs[h]], kbs[h // 2]) for h in heads}
                dps = {h: _dot_nt(dom_scr[:, hcs[h]], vbs[h // 2]) for h in heads}
                lbs, l1s = {}, {}
                for h in heads:
                    lbs[h], l1s[h] = _sb_logits(zs[h])
                sufs = {h: _block_sums(jnp.where(mask, l1s[h], 0.0), upper, after=True)[0] for h in heads}
                prs, dzzs = {}, {}
                for h in heads:
                    expo = lbs[h] + sufs[h]
                    if width == WIDE:
                        expo = expo + jnp.sum(jnp.where(onehot, cm_ref[:, hcs[h]], 0.0), axis=1, keepdims=True)
                    pr = jnp.where(mask, jnp.exp(expo), 0.0)
                    dzzs[h] = pr * dps[h]
                    prs[h] = pr.astype(BF16)
                css = {h: _block_sums(dzzs[h], lower, after=False) for h in heads}
                dzbs = {}
                for h in heads:
                    c2 = c_scr[h]
                    prefix = css[h][0] + _widen(c2, width)
                    dz = jnp.where(mask, dzzs[h] * jnp.exp(l1s[h]) - jnp.exp(lbs[h]) * prefix, 0.0)
                    dzbs[h] = dz.astype(BF16)
                    c_scr[h] = c2 + css[h][1]
                dqs = {h: _dot(dzbs[h], kbs[h // 2]) for h in heads}
                dks = {h: _dot_tn(dzbs[h], qm_scr[:, hcs[h]]) for h in heads}
                dvs = {h: _dot_tn(prs[h], dom_scr[:, hcs[h]]) for h in heads}
                for p in sorted({h // 2 for h in heads}):
                    dq_scr[:, hcs[p]] += jnp.where(lo_m, dqs[2 * p], dqs[2 * p + 1])
                    dk_ref[pl.ds(off, width), hcs[p]] += dks[2 * p] + dks[2 * p + 1]
                    dv_ref[pl.ds(off, width), hcs[p]] += dvs[2 * p] + dvs[2 * p + 1]

        _over_key_blocks(i + 1, block, reverse=False)
        dq_ref[...] = (dq_scr[...] * SB_SCALE).astype(dq_ref.dtype)

    tile = lambda col: pl.BlockSpec((BLK, 512), lambda b, i: (b * nb + i, col))
    full = lambda col: pl.BlockSpec((lp, 512), lambda b, i: (b, col))
    acc = jax.ShapeDtypeStruct((t, 512), F32)
    once = jax.ShapeDtypeStruct((t, 512), BF16)
    hbm = pl.BlockSpec(memory_space=pl.ANY)
    outs = pl.pallas_call(
        body, name="sb_bwd", grid=(nbatch, nb),
        in_specs=[tile(0), full(1), full(2), tile(3), tile(0), tile(0),
                  pl.BlockSpec((BLK, 1024), lambda b, i: (b * nb + i, 0))] + [hbm] * ns,
        out_specs=[tile(0), full(0), full(0), tile(0)] + [hbm] * ns,
        out_shape=[once, acc, acc, once] + [jax.ShapeDtypeStruct(a.shape, a.dtype) for a in chunked],
        scratch_shapes=[pltpu.VMEM((8, BLK, BLK), F32), pltpu.VMEM((BLK, 1024), BF16),
                        pltpu.VMEM((BLK, 1024), BF16), pltpu.VMEM((BLK, 512), F32)] + _exchange_sems(ns),
        compiler_params=_params(("arbitrary", "arbitrary"), 56),
    )(proj, proj, proj, proj, o, dog, cm, *chunked)
    return outs[0], outs[1], outs[2], outs[3], list(outs[4:])


def _mla_prep(proj, gq, gkv, tabs, nbatch, nb):
    t = proj.shape[0]

    def body(mid_ref, gq_ref, gkv_ref, c_ref, s1_ref, s2_ref, cq_ref, ckv_ref, kr_ref):
        cq = mid_ref[:, 0:256].astype(F32)
        r = lax.rsqrt(jnp.mean(cq * cq, axis=1, keepdims=True) + NORM_EPS)
        cq_ref[...] = (cq * r * gq_ref[...]).astype(BF16)
        ckv = mid_ref[:, 256:384].astype(F32)
        r = lax.rsqrt(jnp.mean(ckv * ckv, axis=1, keepdims=True) + NORM_EPS)
        ckv_ref[...] = (ckv * r * gkv_ref[...]).astype(BF16)
        kr = mid_ref[:, 384:512].astype(F32)
        kr_ref[...] = _rope(kr, c_ref[...], s1_ref[...], s2_ref[...]).astype(BF16)

    tab = pl.BlockSpec((BLK, BLK), lambda b, i: (i, 0))
    rowspec = lambda w: pl.BlockSpec((BLK, w), lambda b, i: (b * nb + i, 0))
    return pl.pallas_call(
        body, name="mla_prep", grid=(nbatch, nb),
        in_specs=[pl.BlockSpec((BLK, 512), lambda b, i: (b * nb + i, EV_MID_BLK)),
                  pl.BlockSpec((1, 256), lambda b, i: (0, 0)), pl.BlockSpec((1, 128), lambda b, i: (0, 0)),
                  tab, tab, tab],
        out_specs=[rowspec(256), rowspec(128), rowspec(128)],
        out_shape=[jax.ShapeDtypeStruct((t, 256), BF16), jax.ShapeDtypeStruct((t, 128), BF16),
                   jax.ShapeDtypeStruct((t, 128), BF16)],
        compiler_params=_params(("parallel", "parallel")),
    )(proj, gq, gkv, *tabs)


def _mla_prep_bwd(proj, gq, gkv, tabs, dcqn, dckvn, dkrot, nbatch, nb):
    t = proj.shape[0]

    def body(mid_ref, gq_ref, gkv_ref, c_ref, s1_ref, s2_ref, dcq_ref, dckv_ref, dkr_ref,
             dmid_ref, dgq_ref, dgkv_ref):
        @pl.when((pl.program_id(0) == 0) & (pl.program_id(1) == 0))
        def _():
            dgq_ref[...] = jnp.zeros_like(dgq_ref)
            dgkv_ref[...] = jnp.zeros_like(dgkv_ref)

        def norm_bwd(x, gain, dy, dgain_ref):
            r = lax.rsqrt(jnp.mean(x * x, axis=1, keepdims=True) + NORM_EPS)
            nx = x * r
            dn = dy * gain
            dgain_ref[...] += jnp.sum(dy * nx, axis=0, keepdims=True)
            return r * (dn - nx * jnp.mean(dn * nx, axis=1, keepdims=True))

        dmid_ref[:, 0:256] = norm_bwd(
            mid_ref[:, 0:256].astype(F32), gq_ref[...], dcq_ref[...], dgq_ref).astype(BF16)
        dmid_ref[:, 256:384] = norm_bwd(
            mid_ref[:, 256:384].astype(F32), gkv_ref[...], dckv_ref[...], dgkv_ref).astype(BF16)
        dmid_ref[:, 384:512] = _rope_t(dkr_ref[...], c_ref[...], s1_ref[...], s2_ref[...]).astype(BF16)

    tab = pl.BlockSpec((BLK, BLK), lambda b, i: (i, 0))
    rowspec = lambda w: pl.BlockSpec((BLK, w), lambda b, i: (b * nb + i, 0))
    vq = pl.BlockSpec((1, 256), lambda b, i: (0, 0))
    vkv = pl.BlockSpec((1, 128), lambda b, i: (0, 0))
    return pl.pallas_call(
        body, name="mla_prep_bwd", grid=(nbatch, nb),
        in_specs=[pl.BlockSpec((BLK, 512), lambda b, i: (b * nb + i, EV_MID_BLK)), vq, vkv, tab, tab, tab,
                  rowspec(256), rowspec(128), rowspec(128)],
        out_specs=[rowspec(512), vq, vkv],
        out_shape=[jax.ShapeDtypeStruct((t, 512), BF16), jax.ShapeDtypeStruct((1, 256), F32),
                   jax.ShapeDtypeStruct((1, 128), F32)],
        compiler_params=_params(("arbitrary", "arbitrary")),
    )(proj, gq, gkv, *tabs, dcqn, dckvn, dkrot)


def _mla_scores(qf, kvb, krb, mask, lo_m):
    kf = jnp.where(lo_m, kvb, krb)
    return kf, jnp.where(mask, _dot_nt(qf, kf), NEG)


def _mla_fwd(qh, kvh, krot, proj, tabs, nbatch, nb):
    lp = nb * BLK
    t = nbatch * lp

    def body(q_ref, kv_ref, kr_ref, g_ref, c_ref, s1_ref, s2_ref, o_ref, og_ref, lse_ref,
             qf_scr, m_scr, l_scr, acc_scr):
        i = pl.program_id(1)
        row, lane = _iotas()
        lo_m = lane < HALF
        for h in range(8):
            hc = slice(h * BLK, (h + 1) * BLK)
            qf_scr[:, hc] = (_rope(q_ref[:, hc].astype(F32), c_ref[...], s1_ref[...], s2_ref[...])
                             * MLA_SCALE).astype(BF16)
        m_scr[...] = jnp.full(m_scr.shape, NEG, F32)
        l_scr[...] = jnp.zeros_like(l_scr)
        acc_scr[...] = jnp.zeros_like(acc_scr)

        def block(off, width):
            mask = _key_mask(i, off, width, strict=False)
            lo_k = lax.broadcasted_iota(jnp.int32, (width, BLK), 1) < HALF
            ones = jnp.ones((width, BLK), BF16)
            krb = kr_ref[pl.ds(off, width), :]
            heads = range(8)
            hcs = [slice(h * BLK, (h + 1) * BLK) for h in heads]
            kvbs = [kv_ref[pl.ds(off, width), hc] for hc in hcs]
            ss = [_mla_scores(qf_scr[:, hcs[h]], kvbs[h], krb, mask, lo_k)[1] for h in heads]
            ps, alphas = [], []
            for h in heads:
                m = m_scr[h]
                m2 = jnp.maximum(m, jnp.max(ss[h], axis=1, keepdims=True))
                ps.append(jnp.exp(ss[h] - _widen(m2, width)).astype(BF16))
                alphas.append(jnp.exp(m - m2))
                m_scr[h] = m2
            pvs = [_dot(ps[h], jnp.concatenate([kvbs[h], ones], axis=1)) for h in heads]
            for h in heads:
                l_scr[h] = alphas[h] * l_scr[h] + pvs[h][:, BLK:]
                acc_scr[h] = alphas[h] * acc_scr[h] + pvs[h][:, :BLK]

        _over_key_blocks(i + 1, block, reverse=False)
        lse = jnp.zeros((BLK, BLK), F32)
        for p in range(4):
            pc = slice(p * BLK, (p + 1) * BLK)
            o0 = acc_scr[2 * p] / l_scr[2 * p]
            o1 = acc_scr[2 * p + 1] / l_scr[2 * p + 1]
            o_ref[:, pc] = jnp.where(lo_m, pltpu.roll(o0, HALF, 1), o1)
            for h in (2 * p, 2 * p + 1):
                lse = lse + jnp.where(lane == h, m_scr[h] + jnp.log(l_scr[h]), 0.0)
        lse_ref[...] = lse
        g = g_ref[...].astype(F32)
        og_ref[...] = (o_ref[...] * g * _sigmoid(g)).astype(og_ref.dtype)

    tile = pl.BlockSpec((BLK, 512), lambda b, i: (b * nb + i, 0))
    tab = pl.BlockSpec((BLK, BLK), lambda b, i: (i, 0))
    heads = pltpu.VMEM((8, BLK, BLK), F32)
    return pl.pallas_call(
        body, name="mla_fwd", grid=(nbatch, nb),
        in_specs=[pl.BlockSpec((BLK, 1024), lambda b, i: (b * nb + i, 0)),
                  pl.BlockSpec((lp, 1024), lambda b, i: (b, 0)),
                  pl.BlockSpec((lp, BLK), lambda b, i: (b, 0)),
                  pl.BlockSpec((BLK, 512), lambda b, i: (b * nb + i, EV_GMLA_BLK // 4)),
                  tab, tab, tab],
        out_specs=[tile, tile, pl.BlockSpec((BLK, BLK), lambda b, i: (b * nb + i, 0))],
        out_shape=[jax.ShapeDtypeStruct((t, 512), F32), jax.ShapeDtypeStruct((t, 512), BF16),
                   jax.ShapeDtypeStruct((t, BLK), F32)],
        scratch_shapes=[pltpu.VMEM((BLK, 1024), BF16), heads, heads, heads],
        compiler_params=_params(("parallel", "arbitrary"), 48),
    )(qh, kvh, krot, proj, *tabs)


def _mla_bwd(qh, kvh, krot, proj, tabs, o, dog, lse, nbatch, nb):
    lp = nb * BLK
    t = nbatch * lp

    def body(q_ref, kv_ref, kr_ref, g_ref, c_ref, s1_ref, s2_ref, o_ref, dog_ref, lse_ref,
             dq_ref, dkv_ref, dkr_ref, dg_ref, qf_scr, do_scr, stat_scr, acc_scr):
        i = pl.program_id(1)

        @pl.when(i == 0)
        def _():
            dkv_ref[...] = jnp.zeros_like(dkv_ref)
            dkr_ref[...] = jnp.zeros_like(dkr_ref)

        row, lane = _iotas()
        lo_m = lane < HALF
        g = g_ref[...].astype(F32)
        sig = _sigmoid(g)
        dog_v = dog_ref[...]
        o_v = o_ref[...]
        dg_ref[...] = (dog_v * o_v * (sig * (1.0 + g * (1.0 - sig)))).astype(dg_ref.dtype)
        do = dog_v * g * sig
        do_o = do * o_v
        lse_blk = lse_ref[...]
        zero = jnp.zeros((BLK, BLK), F32)
        for h in range(8):
            hc = slice(h * BLK, (h + 1) * BLK)
            pc = slice((h // 2) * BLK, (h // 2 + 1) * BLK)
            qf_scr[:, hc] = (_rope(q_ref[:, hc].astype(F32), c_ref[...], s1_ref[...], s2_ref[...])
                             * MLA_SCALE).astype(BF16)
            dop = do[:, pc]
            do_src = dop if h % 2 else pltpu.roll(dop, HALF, 1)
            do_scr[:, hc] = jnp.where(lo_m, 0.0, do_src).astype(BF16)
            hm = (lane >= HALF) if h % 2 else lo_m
            stat_scr[h] = zero + jnp.sum(jnp.where(hm, do_o[:, pc], 0.0), axis=1, keepdims=True)
            stat_scr[8 + h] = zero + jnp.sum(jnp.where(lane == h, lse_blk, 0.0), axis=1, keepdims=True)
        acc_scr[...] = jnp.zeros_like(acc_scr)

        def block(off, width):
            mask = _key_mask(i, off, width, strict=False)
            lo_k = lax.broadcasted_iota(jnp.int32, (width, BLK), 1) < HALF
            krb = kr_ref[pl.ds(off, width), :]
            heads = range(8)
            hcs = [slice(h * BLK, (h + 1) * BLK) for h in heads]
            kvbs = [kv_ref[pl.ds(off, width), hc] for hc in hcs]
            qfs = [qf_scr[:, hc] for hc in hcs]
            dos = [do_scr[:, hc] for hc in hcs]
            scored = [_mla_scores(qfs[h], kvbs[h], krb, mask, lo_k) for h in heads]
            dps = [_dot_nt(dos[h], kvbs[h]) for h in heads]
            pbs, dss = [], []
            for h in heads:
                p = jnp.exp(scored[h][1] - _widen(stat_scr[8 + h], width))
                pbs.append(p.astype(BF16))
                dss.append((p * (dps[h] - _widen(stat_scr[h], width))).astype(BF16))
            dqs = [_dot(dss[h], scored[h][0]) for h in heads]
            dkfs = [_dot_tn(dss[h], qfs[h]) for h in heads]
            dvvs = [_dot_tn(pbs[h], dos[h]) for h in heads]
            dkr = jnp.zeros((width, BLK), F32)
            for h in heads:
                acc_scr[h] += dqs[h]
                dkv_ref[pl.ds(off, width), hcs[h]] += jnp.where(lo_k, dkfs[h], 0.0) + dvvs[h]
                dkr = dkr + jnp.where(lo_k, 0.0, dkfs[h])
            dkr_ref[pl.ds(off, width), :] += dkr

        _over_key_blocks(i + 1, block, reverse=False)
        for h in range(8):
            hc = slice(h * BLK, (h + 1) * BLK)
            dq_ref[:, hc] = _rope_t(acc_scr[h] * MLA_SCALE, c_ref[...], s1_ref[...], s2_ref[...]).astype(dq_ref.dtype)

    tile = lambda col: pl.BlockSpec((BLK, 512), lambda b, i: (b * nb + i, col))
    wide = pl.BlockSpec((BLK, 1024), lambda b, i: (b * nb + i, 0))
    full8 = pl.BlockSpec((lp, 1024), lambda b, i: (b, 0))
    full1 = pl.BlockSpec((lp, BLK), lambda b, i: (b, 0))
    tab = pl.BlockSpec((BLK, BLK), lambda b, i: (i, 0))
    return pl.pallas_call(
        body, name="mla_bwd", grid=(nbatch, nb),
        in_specs=[wide, full8, full1, tile(EV_GMLA_BLK // 4), tab, tab, tab, tile(0), tile(1),
                  pl.BlockSpec((BLK, BLK), lambda b, i: (b * nb + i, 0))],
        out_specs=[wide, full8, full1, tile(0)],
        out_shape=[jax.ShapeDtypeStruct((t, 1024), BF16), jax.ShapeDtypeStruct((t, 1024), F32),
                   jax.ShapeDtypeStruct((t, 128), F32), jax.ShapeDtypeStruct((t, 512), BF16)],
        scratch_shapes=[pltpu.VMEM((BLK, 1024), BF16), pltpu.VMEM((BLK, 1024), BF16),
                        pltpu.VMEM((16, BLK, BLK), F32), pltpu.VMEM((8, BLK, BLK), F32)],
        compiler_params=_params(("parallel", "arbitrary"), 56),
    )(qh, kvh, krot, proj, *tabs, o, dog, lse)


def _swa_setup(kk, i, k_refs, v_refs):
    row, lane = _iotas()
    own = (lane >= kk * HALF) & (lane < (kk + 1) * HALF)

    def dup(ref):
        x = ref[...].astype(F32)
        return jnp.where(own, x, pltpu.roll(x, HALF, 1)).astype(BF16)

    kcat = jnp.concatenate([dup(r) for r in k_refs], axis=0)
    vcat = jnp.concatenate([dup(r) for r in v_refs], axis=0)
    row2 = lax.broadcasted_iota(jnp.int32, (BLK, 2 * BLK), 0)
    lane2 = lax.broadcasted_iota(jnp.int32, (BLK, 2 * BLK), 1)
    is_meta = lane2 < BLK
    in_own = lane2 - BLK <= row2
    k_pos = jnp.where(is_meta, lane2, jnp.where(in_own, (i - 1) * BLK, (i - 2) * BLK) + lane2)
    d = i * BLK + row2 - k_pos
    mask = (d >= 0) & (k_pos >= jnp.where(is_meta, N_PAD, BLK))
    return lane, lane <= row, own, kcat, vcat, mask, d.astype(F32)


def _swa_fold(x, in_own):
    return jnp.concatenate([x[:, :BLK], jnp.where(in_own, x[:, 2 * BLK:], x[:, BLK:2 * BLK])], axis=1)


def _swa_unfold(x, in_own):
    w = x[:, BLK:]
    zero = jnp.zeros_like(w)
    return jnp.concatenate([x[:, :BLK], jnp.where(in_own, zero, w), jnp.where(in_own, w, zero)], axis=1)


def _swa_slope(kk, g_idx):
    return (2.0 ** (-(g_idx + 1) / 2.0)) * jnp.where(kk == 0, 1.0, 1.0 / 16.0)


def _swa_fwd(proj, sinks, nbatch, nb):
    lp = nb * BLK
    t = nbatch * lp

    def body(sink_ref, q_ref, ka, kb, kc, va, vb, vc, g_ref, o_ref, og_ref, lse_ref):
        kk = pl.program_id(1)
        i = pl.program_id(2)
        lane, in_own, own, kcat, vcat, mask, dist = _swa_setup(kk, i, (ka, kb, kc), (va, vb, vc))
        lo_m = lane < HALF
        heads = range(8)
        qms = []
        for h in heads:
            qp = q_ref[:, (h // 2) * BLK:(h // 2 + 1) * BLK]
            qms.append(jnp.where((lane >= HALF) if h % 2 else lo_m, qp, jnp.zeros_like(qp)) * SWA_SCALE)
        qks = [_dot_nt(qms[h], kcat) for h in heads]
        ps, ls, lses = [], [], []
        for h in heads:
            sink = sink_ref[kk, h]
            s = jnp.where(mask, _swa_fold(qks[h], in_own) - _swa_slope(kk, h) * dist, NEG)
            mx = jnp.maximum(jnp.max(s, axis=1, keepdims=True), sink)
            p = jnp.exp(s - mx)
            l = jnp.exp(sink - mx) + jnp.sum(p, axis=1, keepdims=True)
            ps.append(_swa_unfold(p.astype(BF16), in_own))
            ls.append(l)
            lses.append(mx + jnp.log(l))
        pvs = [_dot(ps[h], vcat) for h in heads]
        lse_out = jnp.zeros((BLK, BLK), F32)
        for m in range(4):
            cols = slice(m * BLK, (m + 1) * BLK)
            outp = jnp.where(lo_m, pvs[2 * m] / ls[2 * m], pvs[2 * m + 1] / ls[2 * m + 1])
            o_ref[:, cols] = outp
            g = g_ref[:, cols].astype(F32)
            og_ref[:, cols] = (outp * g * _sigmoid(g)).astype(og_ref.dtype)
            for h in (2 * m, 2 * m + 1):
                lse_out = lse_out + jnp.where(lane == h, lses[h], 0.0)
        lse_ref[...] = lse_out

    def kvspec(col, which):
        if which == 0:
            return pl.BlockSpec((BLK, BLK), lambda b, kk, i: (b * nb, col))
        if which == 1:
            return pl.BlockSpec((BLK, BLK), lambda b, kk, i: (b * nb + jnp.maximum(i - 1, 0), col))
        return pl.BlockSpec((BLK, BLK), lambda b, kk, i: (b * nb + i, col))

    wide = lambda c0: pl.BlockSpec((BLK, 512), lambda b, kk, i: (b * nb + i, c0 + kk))
    return pl.pallas_call(
        body, name="swa_fwd", grid=(nbatch, 2, nb),
        in_specs=[pl.BlockSpec(memory_space=pltpu.SMEM), wide(0),
                  kvspec(OD_K_BLK, 0), kvspec(OD_K_BLK, 1), kvspec(OD_K_BLK, 2),
                  kvspec(OD_V_BLK, 0), kvspec(OD_V_BLK, 1), kvspec(OD_V_BLK, 2), wide(2)],
        out_specs=[wide(0), wide(0), pl.BlockSpec((BLK, BLK), lambda b, kk, i: (b * nb + i, kk))],
        out_shape=[jax.ShapeDtypeStruct((t, 1024), F32), jax.ShapeDtypeStruct((t, 1024), BF16),
                   jax.ShapeDtypeStruct((t, 256), F32)],
        compiler_params=_params(("parallel", "parallel", "arbitrary")),
    )(sinks, proj, proj, proj, proj, proj, proj, proj, proj)


def _swa_bwd(proj, sinks, o, dog, lse, nbatch, nb):
    lp = nb * BLK
    t = nbatch * lp

    def body(sink_ref, q_ref, ka, kb, kc, va, vb, vc, g_ref, o_ref, dog_ref, lse_ref,
             dq_ref, dg_ref, dk_ref, dv_ref, dsink_ref):
        kk = pl.program_id(1)
        i = pl.program_id(2)

        @pl.when((kk == 0) & (i == 0))
        def _():
            dk_ref[...] = jnp.zeros_like(dk_ref)
            dv_ref[...] = jnp.zeros_like(dv_ref)

        @pl.when(i == 0)
        def _():
            dsink_ref[...] = jnp.zeros_like(dsink_ref)

        lane, in_own, own, kcat, vcat, mask, dist = _swa_setup(kk, i, (ka, kb, kc), (va, vb, vc))
        lo_m = lane < HALF
        row8 = lax.broadcasted_iota(jnp.int32, (8, BLK), 0)
        lse_blk = lse_ref[...]
        heads = range(8)
        qms, doms, deltas, lse_hs = [], [], [], []
        for m in range(4):
            cols = slice(m * BLK, (m + 1) * BLK)
            qp = q_ref[:, cols]
            g = g_ref[:, cols].astype(F32)
            sig = _sigmoid(g)
            dog_v = dog_ref[:, cols]
            o_v = o_ref[:, cols]
            dg_ref[:, cols] = (dog_v * o_v * (sig * (1.0 + g * (1.0 - sig)))).astype(dg_ref.dtype)
            do = dog_v * g * sig
            do_o = do * o_v
            dob = do.astype(BF16)
            for h in (2 * m, 2 * m + 1):
                hm = (lane >= HALF) if h % 2 else lo_m
                qms.append(jnp.where(hm, qp, jnp.zeros_like(qp)) * SWA_SCALE)
                doms.append(jnp.where(hm, dob, jnp.zeros_like(dob)))
                deltas.append(jnp.sum(jnp.where(hm, do_o, 0.0), axis=1, keepdims=True))
                lse_hs.append(jnp.sum(jnp.where(lane == h, lse_blk, 0.0), axis=1, keepdims=True))
        qks = [_dot_nt(qms[h], kcat) for h in heads]
        dps = [_dot_nt(doms[h], vcat) for h in heads]
        pbs, dss = [], []
        dsink = jnp.zeros((8, BLK), F32)
        for h in heads:
            s = jnp.where(mask, _swa_fold(qks[h], in_own) - _swa_slope(kk, h) * dist, NEG)
            p = jnp.exp(s - lse_hs[h])
            pbs.append(_swa_unfold(p.astype(BF16), in_own))
            dss.append(_swa_unfold((p * (_swa_fold(dps[h], in_own) - deltas[h])).astype(BF16), in_own))
            tot = jnp.sum(-jnp.exp(sink_ref[kk, h] - lse_hs[h]) * deltas[h], axis=0, keepdims=True)
            dsink = dsink + jnp.where(row8 == h, tot, 0.0)
        dsink_ref[...] += dsink
        dqs = [_dot(dss[h], kcat) for h in heads]
        dks = [_dot_tn(dss[h], qms[h]) for h in heads]
        dvs = [_dot_tn(pbs[h], doms[h]) for h in heads]
        for m in range(4):
            dq_ref[:, m * BLK:(m + 1) * BLK] = (
                jnp.where(lo_m, dqs[2 * m], dqs[2 * m + 1]) * SWA_SCALE).astype(dq_ref.dtype)
        dk = dks[0]
        dv = dvs[0]
        for h in range(1, 8):
            dk = dk + dks[h]
            dv = dv + dvs[h]
        offs = [0, pl.multiple_of(jnp.maximum(i - 1, 0) * BLK, BLK), pl.multiple_of(i * BLK, BLK)]
        for x in range(3):
            rows = slice(x * BLK, (x + 1) * BLK)
            dkx, dvx = dk[rows], dv[rows]
            dk_ref[pl.ds(offs[x], BLK), :] += jnp.where(own, dkx + pltpu.roll(dkx, HALF, 1), 0.0)
            dv_ref[pl.ds(offs[x], BLK), :] += jnp.where(own, dvx + pltpu.roll(dvx, HALF, 1), 0.0)

    def kvspec(col, which):
        if which == 0:
            return pl.BlockSpec((BLK, BLK), lambda b, kk, i: (b * nb, col))
        if which == 1:
            return pl.BlockSpec((BLK, BLK), lambda b, kk, i: (b * nb + jnp.maximum(i - 1, 0), col))
        return pl.BlockSpec((BLK, BLK), lambda b, kk, i: (b * nb + i, col))

    wide = lambda c0: pl.BlockSpec((BLK, 512), lambda b, kk, i: (b * nb + i, c0 + kk))
    full = pl.BlockSpec((lp, BLK), lambda b, kk, i: (b, 0))
    return pl.pallas_call(
        body, name="swa_bwd", grid=(nbatch, 2, nb),
        in_specs=[pl.BlockSpec(memory_space=pltpu.SMEM), wide(0),
                  kvspec(OD_K_BLK, 0), kvspec(OD_K_BLK, 1), kvspec(OD_K_BLK, 2),
                  kvspec(OD_V_BLK, 0), kvspec(OD_V_BLK, 1), kvspec(OD_V_BLK, 2), wide(2),
                  wide(0), wide(0), pl.BlockSpec((BLK, BLK), lambda b, kk, i: (b * nb + i, kk))],
        out_specs=[wide(0), wide(0), full, full,
                   pl.BlockSpec((8, BLK), lambda b, kk, i: (b * 2 + kk, 0))],
        out_shape=[jax.ShapeDtypeStruct((t, 1024), BF16), jax.ShapeDtypeStruct((t, 1024), BF16),
                   jax.ShapeDtypeStruct((t, 128), F32), jax.ShapeDtypeStruct((t, 128), F32),
                   jax.ShapeDtypeStruct((nbatch * 16, BLK), F32)],
        compiler_params=_params(("parallel", "arbitrary", "arbitrary")),
    )(sinks, proj, proj, proj, proj, proj, proj, proj, proj, o, dog, lse)


def _rope_tables(lp):
    pos = (jnp.arange(lp) - N_PAD).astype(F32)
    inv = ROPE_BASE ** (-jnp.arange(16, dtype=F32) / 16.0)
    ang = pos[:, None] * inv[None, :]
    cos, sin = jnp.cos(ang), jnp.sin(ang)
    z16 = jnp.zeros((lp, 16), F32)
    c = jnp.concatenate([jnp.ones((lp, 64), F32), cos, cos, jnp.zeros((lp, 32), F32)], axis=1)
    s1 = jnp.concatenate([jnp.zeros((lp, 64), F32), -sin, z16, jnp.zeros((lp, 32), F32)], axis=1)
    s2 = jnp.concatenate([jnp.zeros((lp, 64), F32), z16, sin, jnp.zeros((lp, 32), F32)], axis=1)
    return c, s1, s2


def _local_step(h0, tgt, norm_g, final_g, gq, gkv, sinks, w_ie, late_shards, nbatch, nb):
    lp = nb * BLK
    tabs = _rope_tables(lp)
    g0, g1 = norm_g[0:1], norm_g[1:2]
    sinks2 = sinks.reshape(2, 8)

    proj_e, hn0 = _norm_mm(h0, g0, w_ie, "proj_even")
    o_sb, og_sb, cm, gathered = _sb_fwd(proj_e, [late_shards[name] for name in _LATE], nbatch, nb)
    full = {name: _unchunk(name, blk) for name, blk in zip(_LATE, gathered)}
    w_uq, w_ukv, w_oe = _uq_to_compute(full["ev_w_uq"]), full["ev_w_ukv"], full["ev_w_out"]
    w_io, w_oo = _od_in_to_compute(full["od_w_in"]), full["od_w_out"]
    cqn, ckvn, krot = _mla_prep(proj_e, gq, gkv, tabs, nbatch, nb)
    qh = _mm(cqn, w_uq, "nn", "mla_uq", out_dtype=BF16)
    kvh = _mm(ckvn, w_ukv, "nn", "mla_ukv", out_dtype=BF16)
    o_mla, og_mla, lse_m = _mla_fwd(qh, kvh, krot, proj_e, tabs, nbatch, nb)
    h1 = _mm([og_sb, og_mla], w_oe, "nn", "out_even", add=h0)
    proj_o, hn1 = _norm_mm(h1, g1, w_io, "proj_odd")
    o_o, og_o, lse_o = _swa_fwd(proj_o, sinks2, nbatch, nb)
    h2 = _mm(og_o, w_oo, "nn", "out_odd", add=h1)
    dh2, lossv, d_final_g = _final(h2, tgt, final_g, nbatch, nb)

    dog_o = _mm(dh2, w_oo, "nt", "d_out_odd")
    d_w_oo, = _mm_tn(og_o, [dh2], "dw_out_odd")
    dq_o, dg_o, dk_o, dv_o, dsink = _swa_bwd(proj_o, sinks2, o_o, dog_o, lse_o, nbatch, nb)
    dproj_o = [dq_o, dg_o, dk_o, dv_o]
    dh1, d_g1 = _mm_norm_bwd(dproj_o, w_io, h1, g1, dh2, "d_proj_odd")
    dw_q, dw_g, dw_k, dw_v = _mm_tn(hn1, dproj_o, "dw_proj_odd")

    dog_e = _mm(dh1, w_oe, "nt", "d_out_even")
    d_w_oe_sb, = _mm_tn(og_sb, [dh1], "dw_out_even_sb")
    d_w_oe_mla, = _mm_tn(og_mla, [dh1], "dw_out_even_mla")
    early = dict(od_w_in=jnp.concatenate([dw_q, dw_k, dw_v, dw_g], axis=1), od_w_out=d_w_oo,
                 ev_w_out=jnp.concatenate([d_w_oe_sb, d_w_oe_mla], axis=0))
    dq_sb, dk_sb, dv_sb, dg_sb, received = _sb_bwd(
        proj_e, o_sb, dog_e, cm, [_chunk(name, early[name]) for name in _EARLY_GRADS], nbatch, nb)
    dqh, dkvh, dkrot, dg_mla = _mla_bwd(qh, kvh, krot, proj_e, tabs, o_mla, dog_e, lse_m, nbatch, nb)
    dcqn = _mm(dqh, w_uq, "nt", "d_mla_uq")
    d_w_uq, = _mm_tn(cqn, [dqh], "dw_mla_uq")
    dckvn = _mm(dkvh, w_ukv, "nt", "d_mla_ukv")
    d_w_ukv, = _mm_tn(ckvn, [dkvh], "dw_mla_ukv")
    dmid, d_gq, d_gkv = _mla_prep_bwd(proj_e, gq, gkv, tabs, dcqn, dckvn, dkrot, nbatch, nb)
    dproj_e = [dq_sb, dk_sb, dv_sb, dg_sb, dmid, dg_mla]
    dh0, d_g0 = _mm_norm_bwd(dproj_e, w_ie, h0, g0, dh1, "d_proj_even")
    dw_e = _mm_tn(hn0, dproj_e, "dw_proj_even")

    d_sinks = dsink.reshape(nbatch, 2, 8, BLK)[:, :, :, 0].sum(axis=0).reshape(1, 16)
    d_norm_g = jnp.concatenate([d_g0, d_g1], axis=0)
    d_ev_w_in = jnp.concatenate(list(dw_e[:4]) + [dw_e[4][:, :384], dw_e[4][:, 448:480], dw_e[5]], axis=1)
    return dict(lossv=lossv, dh0=dh0, norm_g=d_norm_g, final_g=d_final_g, gq=d_gq, gkv=d_gkv, sinks=d_sinks,
                ev_w_in=d_ev_w_in, ev_w_uq=_uq_from_compute(d_w_uq), ev_w_ukv=d_w_ukv, received=received)


def _ev_in_to_compute(w):
    z = lambda n: jnp.zeros((w.shape[0], n), w.dtype)
    return jnp.concatenate([w[:, :2432], z(64), w[:, 2432:2464], z(32), w[:, 2464:]], axis=1)


def _uq_to_compute(w):
    w3 = w.reshape(256, 8, 96)
    return jnp.concatenate([w3, jnp.zeros((256, 8, 32), w.dtype)], axis=2).reshape(256, 1024)


def _uq_from_compute(w):
    return w.reshape(256, 8, 128)[:, :, :96].reshape(256, 768)


def _od_in_to_compute(w):
    return jnp.concatenate([w[:, :1024], w[:, 1280:], w[:, 1024:1280]], axis=1)


_BIG = dict(ev_w_in=(1024, 2976, 1), ev_w_uq=(256, 768, 1), ev_w_ukv=(128, 1024, 1),
            ev_w_out=(1024, 1024, 0), od_w_in=(1024, 2304, 1), od_w_out=(1024, 1024, 0))
_LATE = ("ev_w_uq", "ev_w_ukv", "ev_w_out", "od_w_in", "od_w_out")
_EARLY_GRADS = ("od_w_in", "od_w_out", "ev_w_out")
_LAST_GRADS = ("ev_w_in", "ev_w_uq", "ev_w_ukv")


def _unchunk(name, blk):
    rows, cols, axis = _BIG[name]
    return blk.transpose(1, 0, 2).reshape(rows, cols) if axis == 1 else blk.reshape(rows, cols)


def _chunk(name, g):
    rows, cols, axis = _BIG[name]
    g = g.astype(BF16)
    return g.reshape(rows, N_DEV, cols // N_DEV).transpose(1, 0, 2) if axis == 1 else g.reshape(N_DEV, rows // N_DEV, cols)


def _all_gather(shards, name):
    n = len(shards)

    def body(*refs):
        xs, outs = refs[:n], refs[n:2 * n]
        send_sems, recv_sems, local_sems = refs[2 * n:]
        x, y, c = lax.axis_index("x"), lax.axis_index("y"), lax.axis_index("c")
        me, sibling = (x, y, c), (x, y, 1 - c)
        chips = [(1 - x, y), (x, 1 - y), (1 - x, 1 - y)]
        arrays = range(n)

        def copy(k, a, block, to, from_input=False):
            px, py, pc = block
            dst = outs[a].at[4 * px + 2 * py + pc]
            return pltpu.make_async_remote_copy(
                src_ref=xs[a] if from_input else dst, dst_ref=dst,
                send_sem=send_sems.at[k, a], recv_sem=recv_sems.at[k, a],
                device_id=to, device_id_type=pl.DeviceIdType.MESH)

        mine = [pltpu.make_async_copy(xs[a], outs[a].at[4 * x + 2 * y + c], local_sems.at[a]) for a in arrays]
        for cp in mine:
            cp.start()
        first = [copy(0, a, me, sibling, True) for a in arrays]
        for j, chip in enumerate(chips):
            first += [copy(1 + j, a, me, (*chip, c), True) for a in arrays]
        for cp in first:
            cp.start()
        passed = []
        for j, chip in enumerate(chips):
            for a in arrays:
                copy(1 + j, a, (*chip, c), me).wait_recv()
                passed.append(copy(4 + j, a, (*chip, c), sibling))
                passed[-1].start()
        for a in arrays:
            copy(0, a, sibling, me).wait_recv()
        for j, chip in enumerate(chips):
            for a in arrays:
                copy(4 + j, a, (*chip, 1 - c), me).wait_recv()
        for cp in first + passed:
            cp.wait_send()
        for cp in mine:
            cp.wait()

    hbm = pl.BlockSpec(memory_space=pl.ANY)
    return pl.pallas_call(
        body, name=name,
        out_shape=[jax.ShapeDtypeStruct((N_DEV,) + s.shape, s.dtype) for s in shards],
        in_specs=[hbm] * n, out_specs=[hbm] * n,
        scratch_shapes=[pltpu.SemaphoreType.DMA((7, n)), pltpu.SemaphoreType.DMA((7, n)),
                        pltpu.SemaphoreType.DMA((n,))],
    )(*shards)


def _exchange_sum(chunked, received, name):
    n, m = len(chunked), len(received)
    arrs = list(chunked) + list(received)

    def body(*refs):
        ins, outs = refs[:n + m], refs[n + m:2 * (n + m)]
        bufs = refs[2 * (n + m):3 * (n + m)]
        send_sems, recv_sems, local_sems, load_sems = refs[3 * (n + m):]
        loads = [pltpu.make_async_copy(ins[n + a], bufs[n + a], load_sems.at[a]) for a in range(m)]
        for cp in loads:
            cp.start()
        local, sends, recvs = _direct_exchange(ins[:n], bufs[:n], send_sems, recv_sems, local_sems, chunked=True)
        for cp in local + sends:
            cp.start()
        for a, cp in enumerate(loads):
            cp.wait()
            _sum_slots(bufs[n + a], outs[n + a])
        for cp in local:
            cp.wait()
        for cp in recvs:
            cp.wait_recv()
        for cp in sends:
            cp.wait_send()
        for a in range(n):
            _sum_slots(bufs[a], outs[a])

    hbm = pl.BlockSpec(memory_space=pl.ANY)
    vm = pl.BlockSpec(memory_space=pltpu.VMEM)
    return pl.pallas_call(
        body, name=name,
        out_shape=[jax.ShapeDtypeStruct(a.shape[1:], F32) for a in arrs],
        in_specs=[hbm] * (n + m), out_specs=[vm] * (n + m),
        scratch_shapes=[pltpu.VMEM(a.shape, a.dtype) for a in arrs] + _exchange_sems(n)
        + [pltpu.SemaphoreType.DMA((max(m, 1),))],
        compiler_params=pltpu.CompilerParams(vmem_limit_bytes=48 << 20),
    )(*arrs)


def _sum_slots(buf, out):
    rows = buf.shape[1]

    def add(sl):
        acc = buf[(0,) + sl].astype(F32)
        for k in range(1, N_DEV):
            acc = acc + buf[(k,) + sl].astype(F32)
        out[sl] = acc

    if rows > BLK and rows % BLK == 0:
        def step(r, carry):
            add((pl.ds(pl.multiple_of(r * BLK, BLK), BLK), slice(None)))
            return carry

        lax.fori_loop(0, rows // BLK, step, 0)
    else:
        add((slice(None), slice(None)))


def _adamw(ws, gs, ms, vs):
    n = len(ws)

    def body(*refs):
        ins, outs = refs[:4 * n], refs[4 * n:]
        for k in range(n):
            w_ref, g_ref, m_ref, v_ref = ins[4 * k:4 * k + 4]
            d_ref, nm_ref, nv_ref = outs[3 * k:3 * k + 3]

            def update(sl, w_ref=w_ref, g_ref=g_ref, m_ref=m_ref, v_ref=v_ref,
                       d_ref=d_ref, nm_ref=nm_ref, nv_ref=nv_ref):
                g = g_ref[sl]
                m = ADAM_B1 * m_ref[sl] + (1.0 - ADAM_B1) * g
                v = ADAM_B2 * v_ref[sl] + (1.0 - ADAM_B2) * (g * g)
                m_hat = m / (1.0 - ADAM_B1 ** ADAM_STEP)
                v_hat = v / (1.0 - ADAM_B2 ** ADAM_STEP)
                d_ref[sl] = -ADAM_LR * (m_hat / (jnp.sqrt(v_hat) + ADAM_EPS) + ADAM_WD * w_ref[sl])
                nm_ref[sl] = m
                nv_ref[sl] = v

            rows = w_ref.shape[0]
            if rows > BLK and rows % BLK == 0:
                def step(r, carry, update=update):
                    update((pl.ds(pl.multiple_of(r * BLK, BLK), BLK), slice(None)))
                    return carry

                lax.fori_loop(0, rows // BLK, step, 0)
            else:
                update((slice(None), slice(None)))

    args, out_shape = [], []
    for k in range(n):
        args += [ws[k], gs[k], ms[k], vs[k]]
        out_shape += [jax.ShapeDtypeStruct(ws[k].shape, F32)] * 3
    vm = pl.BlockSpec(memory_space=pltpu.VMEM)
    outs = pl.pallas_call(
        body, name="adamw", out_shape=out_shape,
        in_specs=[vm] * (4 * n), out_specs=[vm] * (3 * n),
        compiler_params=pltpu.CompilerParams(vmem_limit_bytes=48 << 20),
    )(*args)
    return [tuple(outs[3 * k:3 * k + 3]) for k in range(n)]


def kernel(x, meta, norm_g, final_g, ev_w_in, ev_q_norm_g, ev_kv_norm_g, ev_w_uq, ev_w_ukv, ev_w_out, od_w_in, od_sinks, od_w_out, loss_target, m_meta, m_norm_g, m_final_g, m_ev_w_in, m_ev_q_norm_g, m_ev_kv_norm_g, m_ev_w_uq, m_ev_w_ukv, m_ev_w_out, m_od_w_in, m_od_sinks, m_od_w_out, v_meta, v_norm_g, v_final_g, v_ev_w_in, v_ev_q_norm_g, v_ev_kv_norm_g, v_ev_w_uq, v_ev_w_ukv, v_ev_w_out, v_od_w_in, v_od_sinks, v_od_w_out):
    nbatch, seq, d = x.shape
    nb = seq // BLK + 1
    lp = nb * BLK
    shards = dict(ev_w_in=ev_w_in[0], ev_w_uq=ev_w_uq[0], ev_w_ukv=ev_w_ukv[0], ev_w_out=ev_w_out[0],
                  od_w_in=od_w_in[0], od_w_out=od_w_out[0])

    w_ie_blocks, meta_blocks = _all_gather([shards["ev_w_in"].astype(BF16), meta], "gather_weights")
    meta_full = meta_blocks.transpose(1, 0, 2).reshape(N_META, d)

    head = jnp.concatenate([jnp.zeros((N_PAD, d), F32), meta_full], axis=0)
    h0 = jnp.concatenate([jnp.broadcast_to(head[None], (nbatch, BLK, d)), x], axis=1).reshape(nbatch * lp, d)
    grads = _local_step(
        h0, loss_target.reshape(nbatch * seq, d), norm_g, final_g.reshape(1, d), ev_q_norm_g, ev_kv_norm_g,
        od_sinks, _ev_in_to_compute(_unchunk("ev_w_in", w_ie_blocks)),
        {name: shards[name].astype(BF16) for name in _LATE}, nbatch, nb)
    dh0 = grads["dh0"].reshape(nbatch, lp, d)
    grad_x = dh0[:, BLK:]

    d_meta = dh0[:, N_PAD:BLK].sum(axis=0).reshape(N_META, N_DEV, BLK).transpose(1, 0, 2)
    pad = lambda a, n: jnp.concatenate([a.reshape(1, -1), jnp.zeros((1, n - a.size), F32)], axis=1)
    loss_part = (0.5 / d * jnp.sum(grads["lossv"])).reshape(1, 1)
    rep = jnp.concatenate([grads["norm_g"].reshape(1, -1), grads["final_g"], grads["gq"], pad(loss_part, 256),
                           pad(grads["gkv"], 256), pad(grads["sinks"], 256)], axis=1).reshape(32, BLK)
    small = jnp.concatenate([d_meta, jnp.broadcast_to(rep[None], (N_DEV, 32, BLK))], axis=1)
    reduced = _exchange_sum([_chunk(name, grads[name]) for name in _LAST_GRADS] + [small],
                            grads["received"], "reduce_grads")
    g_shard = dict(zip(_LAST_GRADS + ("small",) + _EARLY_GRADS, reduced))
    red_small = g_shard.pop("small")
    rep = red_small[N_META:].reshape(1, -1)
    loss = rep[0, 3 * d + 256]
    g_small = dict(meta=red_small[:N_META], norm_g=rep[:, :2 * d].reshape(2, d), final_g=rep[:, 2 * d:3 * d],
                   ev_q_norm_g=rep[:, 3 * d:3 * d + 256], ev_kv_norm_g=rep[:, 3 * d + 512:3 * d + 640],
                   od_sinks=rep[:, 3 * d + 768:3 * d + 784])

    names = ["meta", "norm_g", "final_g", "ev_w_in", "ev_q_norm_g", "ev_kv_norm_g", "ev_w_uq", "ev_w_ukv",
             "ev_w_out", "od_w_in", "od_sinks", "od_w_out"]
    given = dict(meta=(meta, m_meta, v_meta), norm_g=(norm_g, m_norm_g, v_norm_g),
                 final_g=(final_g, m_final_g, v_final_g), ev_w_in=(ev_w_in, m_ev_w_in, v_ev_w_in),
                 ev_q_norm_g=(ev_q_norm_g, m_ev_q_norm_g, v_ev_q_norm_g),
                 ev_kv_norm_g=(ev_kv_norm_g, m_ev_kv_norm_g, v_ev_kv_norm_g),
                 ev_w_uq=(ev_w_uq, m_ev_w_uq, v_ev_w_uq), ev_w_ukv=(ev_w_ukv, m_ev_w_ukv, v_ev_w_ukv),
                 ev_w_out=(ev_w_out, m_ev_w_out, v_ev_w_out), od_w_in=(od_w_in, m_od_w_in, v_od_w_in),
                 od_sinks=(od_sinks, m_od_sinks, v_od_sinks), od_w_out=(od_w_out, m_od_w_out, v_od_w_out))
    ws, gs, ms, vs = [], [], [], []
    for name in names:
        g2 = g_shard[name] if name in g_shard else g_small[name]
        w, m, v = given[name]
        ws.append(w.reshape(g2.shape))
        ms.append(m.reshape(g2.shape))
        vs.append(v.reshape(g2.shape))
        gs.append(g2)
    upd = _adamw(ws, gs, ms, vs)
    shape_of = {name: given[name][0].shape for name in names}
    grads_out = [gs[k].reshape(shape_of[n]) for k, n in enumerate(names)]
    deltas = [upd[k][0].reshape(shape_of[n]) for k, n in enumerate(names)]
    new_m = [upd[k][1].reshape(shape_of[n]) for k, n in enumerate(names)]
    new_v = [upd[k][2].reshape(shape_of[n]) for k, n in enumerate(names)]
    return (loss, grad_x, *grads_out, *deltas, *new_m, *new_v)
```

```python
import jax
import jax.numpy as jnp
from jax import lax
from jax.experimental import pallas as pl
from jax.experimental.pallas import tpu as pltpu

F32 = jnp.float32
BF16 = jnp.bfloat16

D_MODEL = 1024
N_META = 16
BLK = 128
HALF = 64
N_PAD = BLK - N_META
NORM_EPS = 1e-6
NEG = -1e30
N_DEV = 8

SB_SCALE = 64 ** -0.5
MLA_SCALE = 96 ** -0.5
SWA_SCALE = 64 ** -0.5
ROPE_BASE = 10000.0

EV_IN_PAD = 3072
EV_MID_BLK = 4
EV_GMLA_BLK = 20
OD_K_BLK = 16
OD_V_BLK = 17

ADAM_LR = 0.001
ADAM_B1 = 0.9
ADAM_B2 = 0.999
ADAM_EPS = 1e-08
ADAM_WD = 0.01
ADAM_STEP = 10


def _dot(a, b):
    return lax.dot_general(a, b, (((1,), (0,)), ((), ())), preferred_element_type=F32)


def _dot_nt(a, b):
    return lax.dot_general(a, b, (((1,), (1,)), ((), ())), preferred_element_type=F32)


def _dot_tn(a, b):
    return lax.dot_general(a, b, (((0,), (0,)), ((), ())), preferred_element_type=F32)


def _sigmoid(x):
    return 1.0 / (1.0 + jnp.exp(-x))


def _iotas():
    row = lax.broadcasted_iota(jnp.int32, (BLK, BLK), 0)
    lane = lax.broadcasted_iota(jnp.int32, (BLK, BLK), 1)
    return row, lane


WIDE = 2 * BLK
HEAD_GROUPS = (range(0, 8),)


def _key_mask(i, first_key, width, strict):
    t_pos = i * BLK + lax.broadcasted_iota(jnp.int32, (BLK, width), 0)
    s_pos = first_key + lax.broadcasted_iota(jnp.int32, (BLK, width), 1)
    seen = (s_pos < t_pos) if strict else (s_pos <= t_pos)
    return seen & (s_pos >= N_PAD)


def _widen(x, width):
    return x if width == BLK else jnp.concatenate([x] * (width // BLK), axis=1)


def _over_key_blocks(n, block, reverse):
    pairs = n // 2
    last = pl.multiple_of((n - 1) * BLK, BLK)

    def step(jj, carry):
        jp = (pairs - 1 - jj) if reverse else jj
        off = pl.multiple_of(jp * WIDE, WIDE)
        edge = (jp == 0) | (jp == pairs - 1)
        pl.when(edge)(lambda: block(off, WIDE, True))
        pl.when(jnp.logical_not(edge))(lambda: block(off, WIDE, False))
        return carry

    if reverse:
        pl.when(n % 2 == 1)(lambda: block(last, BLK, True))
        lax.fori_loop(0, pairs, step, 0)
    else:
        lax.fori_loop(0, pairs, step, 0)
        pl.when(n % 2 == 1)(lambda: block(last, BLK, True))


def _sel(mask, x, fill):
    return x if mask is None else jnp.where(mask, x, fill)


def _rope(x, c, s1, s2):
    return x * c + pltpu.roll(x, BLK - 16, 1) * s1 + pltpu.roll(x, 16, 1) * s2


def _rope_t(x, c, s1, s2):
    return x * c - pltpu.roll(x, BLK - 16, 1) * s1 - pltpu.roll(x, 16, 1) * s2


def _params(sem, vmem_mb=None):
    kw = dict(dimension_semantics=sem)
    if vmem_mb is not None:
        kw["vmem_limit_bytes"] = vmem_mb << 20
    return pltpu.CompilerParams(**kw)


def _row_tile(t, cands):
    for c in cands:
        if t % c == 0:
            return c
    raise ValueError(t)


def _mm(a, w, mode, name, add=None, out_dtype=F32):
    pieces = list(a) if isinstance(a, (list, tuple)) else [a]
    m = pieces[0].shape[0]
    n = w.shape[1] if mode == "nn" else w.shape[0]
    tm = _row_tile(m, (544, 256, 128) if n <= 1024 else (256, 128))
    widths = [p.shape[1] for p in pieces]
    offs = [sum(widths[:i]) for i in range(len(widths))]

    def body(*refs):
        p_refs = refs[:len(pieces)]
        w_ref = refs[len(pieces)]
        o_ref = refs[-1]
        acc = None
        for p_ref, off, wd in zip(p_refs, offs, widths):
            x = p_ref[...].astype(BF16)
            part = _dot(x, w_ref[off:off + wd, :]) if mode == "nn" else _dot_nt(x, w_ref[:, off:off + wd])
            acc = part if acc is None else acc + part
        if add is not None:
            acc = acc + refs[len(pieces) + 1][...]
        o_ref[...] = acc.astype(o_ref.dtype)

    in_specs = [pl.BlockSpec((tm, wd), lambda i: (i, 0)) for wd in widths]
    in_specs.append(pl.BlockSpec(w.shape, lambda i: (0, 0)))
    args = pieces + [w]
    if add is not None:
        in_specs.append(pl.BlockSpec((tm, n), lambda i: (i, 0)))
        args.append(add)
    return pl.pallas_call(
        body, name=name, grid=(m // tm,), in_specs=in_specs,
        out_specs=pl.BlockSpec((tm, n), lambda i: (i, 0)),
        out_shape=jax.ShapeDtypeStruct((m, n), out_dtype),
        compiler_params=_params(("parallel",), 48),
    )(*args)


def _mm_tn(x, pieces, name):
    t, k = x.shape
    tt = _row_tile(t, (544, 256, 128))
    widths = [p.shape[1] for p in pieces]

    def body(*refs):
        x_ref = refs[0]
        d_refs = refs[1:1 + len(pieces)]
        o_refs = refs[1 + len(pieces):]

        @pl.when(pl.program_id(0) == 0)
        def _():
            for o_ref in o_refs:
                o_ref[...] = jnp.zeros_like(o_ref)

        xb = x_ref[...].astype(BF16)
        for d_ref, o_ref in zip(d_refs, o_refs):
            o_ref[...] += _dot_tn(xb, d_ref[...].astype(BF16))

    return pl.pallas_call(
        body, name=name, grid=(t // tt,),
        in_specs=[pl.BlockSpec((tt, k), lambda i: (i, 0))] + [pl.BlockSpec((tt, wd), lambda i: (i, 0)) for wd in widths],
        out_specs=[pl.BlockSpec((k, wd), lambda i: (0, 0)) for wd in widths],
        out_shape=[jax.ShapeDtypeStruct((k, wd), F32) for wd in widths],
        compiler_params=_params(("arbitrary",), 56),
    )(x, *pieces)


def _norm_mm(h, g, w, name):
    t, d = h.shape
    n = w.shape[1]
    tr = _row_tile(t, (256, 128))

    def body(h_ref, g_ref, w_ref, o_ref, hn_ref):
        x = h_ref[...]
        r = lax.rsqrt(jnp.mean(x * x, axis=1, keepdims=True) + NORM_EPS)
        hn = (x * r * g_ref[...]).astype(BF16)
        hn_ref[...] = hn
        o_ref[...] = _dot(hn, w_ref[...]).astype(o_ref.dtype)

    row = lambda width: pl.BlockSpec((tr, width), lambda i: (i, 0))
    return pl.pallas_call(
        body, name=name, grid=(t // tr,),
        in_specs=[row(d), pl.BlockSpec((1, d), lambda i: (0, 0)), pl.BlockSpec(w.shape, lambda i: (0, 0))],
        out_specs=[row(n), row(d)],
        out_shape=[jax.ShapeDtypeStruct((t, n), BF16), jax.ShapeDtypeStruct((t, d), BF16)],
        compiler_params=_params(("parallel",), 48),
    )(h, g, w)


def _mm_norm_bwd(pieces, w, h, g, dres, name):
    t, d = h.shape
    tr = _row_tile(t, (256, 128))
    widths = [p.shape[1] for p in pieces]
    offs = [sum(widths[:i]) for i in range(len(widths))]

    def body(*refs):
        p_refs = refs[:len(pieces)]
        w_ref, h_ref, g_ref, dres_ref, dh_ref, dg_ref = refs[len(pieces):]

        @pl.when(pl.program_id(0) == 0)
        def _():
            dg_ref[...] = jnp.zeros_like(dg_ref)

        dy = None
        for p_ref, off, wd in zip(p_refs, offs, widths):
            part = _dot_nt(p_ref[...].astype(BF16), w_ref[:, off:off + wd])
            dy = part if dy is None else dy + part
        x = h_ref[...]
        r = lax.rsqrt(jnp.mean(x * x, axis=1, keepdims=True) + NORM_EPS)
        nx = x * r
        dn = dy * g_ref[...]
        dh_ref[...] = dres_ref[...] + r * (dn - nx * jnp.mean(dn * nx, axis=1, keepdims=True))
        dg_ref[...] += jnp.sum(dy * nx, axis=0, keepdims=True)

    row = lambda width: pl.BlockSpec((tr, width), lambda i: (i, 0))
    vec = pl.BlockSpec((1, d), lambda i: (0, 0))
    return pl.pallas_call(
        body, name=name, grid=(t // tr,),
        in_specs=[row(wd) for wd in widths] + [pl.BlockSpec(w.shape, lambda i: (0, 0)), row(d), vec, row(d)],
        out_specs=[row(d), vec],
        out_shape=[jax.ShapeDtypeStruct((t, d), F32), jax.ShapeDtypeStruct((1, d), F32)],
        compiler_params=_params(("arbitrary",), 48),
    )(*pieces, w, h, g, dres)


def _final(h2, tgt, g, nbatch, nb):
    t, d = h2.shape

    def body(h_ref, t_ref, g_ref, dh_ref, loss_ref, dg_ref):
        b = pl.program_id(0)
        i = pl.program_id(1)

        @pl.when((b == 0) & (i == 0))
        def _():
            loss_ref[...] = jnp.zeros_like(loss_ref)
            dg_ref[...] = jnp.zeros_like(dg_ref)

        x = h_ref[...]
        r = lax.rsqrt(jnp.mean(x * x, axis=1, keepdims=True) + NORM_EPS)
        nx = x * r
        gg = g_ref[...]
        live = jnp.where(i >= 1, 1.0, 0.0)
        err = (nx * gg - t_ref[...]) * live
        loss_ref[...] += jnp.sum(err * err, axis=0, keepdims=True)
        dy = err * (1.0 / d)
        dn = dy * gg
        dh_ref[...] = r * (dn - nx * jnp.mean(dn * nx, axis=1, keepdims=True))
        dg_ref[...] += jnp.sum(dy * nx, axis=0, keepdims=True)

    vec = pl.BlockSpec((1, d), lambda b, i: (0, 0))
    return pl.pallas_call(
        body, name="final_loss", grid=(nbatch, nb),
        in_specs=[pl.BlockSpec((BLK, d), lambda b, i: (b * nb + i, 0)),
                  pl.BlockSpec((BLK, d), lambda b, i: (b * (nb - 1) + jnp.maximum(i - 1, 0), 0)),
                  vec],
        out_specs=[pl.BlockSpec((BLK, d), lambda b, i: (b * nb + i, 0)), vec, vec],
        out_shape=[jax.ShapeDtypeStruct((t, d), F32), jax.ShapeDtypeStruct((1, d), F32),
                   jax.ShapeDtypeStruct((1, d), F32)],
        compiler_params=_params(("arbitrary", "arbitrary")),
    )(h2, tgt, g)


def _direct_exchange(srcs, dsts, send_sems, recv_sems, local_sems, chunked):
    x, y, c = lax.axis_index("x"), lax.axis_index("y"), lax.axis_index("c")
    me = 4 * x + 2 * y + c
    arrays = range(len(srcs))
    local = [pltpu.make_async_copy(srcs[a].at[me] if chunked else srcs[a], dsts[a].at[me], local_sems.at[a])
             for a in arrays]
    sends, recvs = [], []
    for d in range(1, N_DEV):
        px = x + ((d >> 2) & 1) - 2 * x * ((d >> 2) & 1)
        py = y + ((d >> 1) & 1) - 2 * y * ((d >> 1) & 1)
        pc = c + (d & 1) - 2 * c * (d & 1)
        pid = 4 * px + 2 * py + pc
        for a in arrays:
            kw = dict(send_sem=send_sems.at[d - 1, a], recv_sem=recv_sems.at[d - 1, a],
                      device_id=(px, py, pc), device_id_type=pl.DeviceIdType.MESH)
            src = srcs[a].at[pid] if chunked else srcs[a]
            sends.append(pltpu.make_async_remote_copy(src_ref=src, dst_ref=dsts[a].at[me], **kw))
            recvs.append(pltpu.make_async_remote_copy(src_ref=src, dst_ref=dsts[a].at[pid], **kw))
    return local, sends, recvs


def _exchange_beside(first, last, srcs, dsts, sems, chunked):
    local, sends, recvs = _direct_exchange(srcs, dsts, *sems, chunked)

    @pl.when(first)
    def _():
        for cp in local + sends:
            cp.start()

    @pl.when(last)
    def _():
        for cp in local:
            cp.wait()
        for cp in recvs:
            cp.wait_recv()
        for cp in sends:
            cp.wait_send()


def _exchange_sems(n):
    return [pltpu.SemaphoreType.DMA((7, n)), pltpu.SemaphoreType.DMA((7, n)), pltpu.SemaphoreType.DMA((n,))]


def _sb_logits(z):
    log_beta = jnp.minimum(z, 0.0) - jnp.log(1.0 + jnp.exp(-jnp.abs(z)))
    return log_beta, log_beta - z


def _tri(width, after):
    j = lax.broadcasted_iota(jnp.int32, (width, width), 0)
    s = lax.broadcasted_iota(jnp.int32, (width, width), 1)
    return (j > s) if after else (j < s)


def _tri_ones(tri):
    return jnp.concatenate([tri.astype(BF16), jnp.ones((tri.shape[0], BLK), BF16)], axis=1)


def _block_sums(x, tri_ones, after):
    xb = x.astype(BF16)
    subs = [_dot(xb[:, s:s + BLK], tri_ones) for s in range(0, x.shape[1], BLK)]
    if len(subs) == 1:
        return subs[0][:, :BLK], subs[0][:, BLK:]
    first, second = subs
    total = first[:, BLK:] + second[:, BLK:]
    if after:
        return jnp.concatenate([first[:, :BLK] + second[:, BLK:], second[:, :BLK]], axis=1), total
    return jnp.concatenate([first[:, :BLK], second[:, :BLK] + first[:, BLK:]], axis=1), total


def _head_masked(x, lane, scale=None):
    out = []
    for h in range(8):
        xp = x[:, (h // 2) * BLK:(h // 2 + 1) * BLK]
        xm = jnp.where((lane >= HALF) if h % 2 else (lane < HALF), xp, jnp.zeros_like(xp))
        out.append(xm if scale is None else xm * scale)
    return jnp.concatenate(out, axis=1)


def _sb_fwd(proj, shards, nbatch, nb):
    lp = nb * BLK
    t = nbatch * lp

    ns = len(shards)

    def body(*refs):
        q_ref, k_ref, v_ref, g_ref = refs[:4]
        shard_refs = refs[4:4 + ns]
        o_ref, og_ref, cm_ref = refs[4 + ns:7 + ns]
        gathered_refs = refs[7 + ns:7 + 2 * ns]
        c_scr, qm_scr = refs[7 + 2 * ns:9 + 2 * ns]
        b = pl.program_id(0)
        i = pl.program_id(1)
        _exchange_beside((b == 0) & (i == 0), (b == nbatch - 1) & (i == nb - 1),
                         shard_refs, gathered_refs, refs[9 + 2 * ns:], chunked=False)
        _, lane = _iotas()
        lo_m = lane < HALF
        cm_ref[...] = jnp.zeros_like(cm_ref)
        c_scr[...] = jnp.zeros_like(c_scr)
        o_ref[...] = jnp.zeros_like(o_ref)
        qm_scr[...] = _head_masked(q_ref[...], lane, SB_SCALE)

        def block(off, width, edge):
            mask = _key_mask(i, off, width, strict=True) if edge else None
            upper = _tri_ones(_tri(BLK, after=True))
            onehot = lane == off // WIDE
            heads = range(8)
            hcs = [slice(h * BLK, (h + 1) * BLK) for h in heads]
            kbs = [k_ref[pl.ds(off, width), hc] for hc in hcs[:4]]
            vbs = [v_ref[pl.ds(off, width), hc] for hc in hcs[:4]]
            zs = [_dot_nt(qm_scr[:, hcs[h]], kbs[h // 2]) for h in heads]
            lbs, l1s = [], []
            for h in heads:
                log_beta, log_1m = _sb_logits(zs[h])
                lbs.append(log_beta)
                l1s.append(_sel(mask, log_1m, 0.0))
            css = [_block_sums(l1s[h], upper, after=True) for h in heads]
            avs = []
            for h in heads:
                c = c_scr[h]
                avs.append(_sel(mask, jnp.exp(lbs[h] + css[h][0] + _widen(c, width)), 0.0).astype(BF16))
                if width == WIDE:
                    cm_ref[:, hcs[h]] = jnp.where(onehot, c, cm_ref[:, hcs[h]])
                c_scr[h] = c + css[h][1]
            accs = [_dot(avs[h], vbs[h // 2]) for h in heads]
            for p in range(4):
                o_ref[:, hcs[p]] += jnp.where(lo_m, accs[2 * p], accs[2 * p + 1])

        _over_key_blocks(i + 1, block, reverse=True)
        g = g_ref[...].astype(F32)
        og_ref[...] = (o_ref[...] * g * _sigmoid(g)).astype(og_ref.dtype)

    tile = lambda col: pl.BlockSpec((BLK, 512), lambda b, i: (b * nb + i, col))
    full = lambda col: pl.BlockSpec((lp, 512), lambda b, i: (b, col))
    hbm = pl.BlockSpec(memory_space=pl.ANY)
    outs = pl.pallas_call(
        body, name="sb_fwd", grid=(nbatch, nb),
        in_specs=[tile(0), full(1), full(2), tile(3)] + [hbm] * ns,
        out_specs=[tile(0), tile(0), pl.BlockSpec((BLK, 1024), lambda b, i: (b * nb + i, 0))] + [hbm] * ns,
        out_shape=[jax.ShapeDtypeStruct((t, 512), F32), jax.ShapeDtypeStruct((t, 512), BF16),
                   jax.ShapeDtypeStruct((t, 1024), F32)]
        + [jax.ShapeDtypeStruct((N_DEV,) + s.shape, s.dtype) for s in shards],
        scratch_shapes=[pltpu.VMEM((8, BLK, BLK), F32), pltpu.VMEM((BLK, 1024), BF16)] + _exchange_sems(ns),
        compiler_params=_params(("arbitrary", "arbitrary"), 48),
    )(proj, proj, proj, proj, *shards)
    return outs[0], outs[1], outs[2], list(outs[3:])


def _sb_bwd(proj, o, dog, cm, chunked, nbatch, nb):
    lp = nb * BLK
    t = nbatch * lp
    ns = len(chunked)

    def body(*refs):
        q_ref, k_ref, v_ref, g_ref, o_ref, dog_ref, cm_ref = refs[:7]
        chunk_refs = refs[7:7 + ns]
        dq_ref, dk_ref, dv_ref, dg_ref = refs[7 + ns:11 + ns]
        received_refs = refs[11 + ns:11 + 2 * ns]
        c_scr, qm_scr, dom_scr, dq_scr = refs[11 + 2 * ns:15 + 2 * ns]
        b = pl.program_id(0)
        i = pl.program_id(1)
        _exchange_beside((b == 0) & (i == 0), (b == nbatch - 1) & (i == nb - 1),
                         chunk_refs, received_refs, refs[15 + 2 * ns:], chunked=True)

        @pl.when(i == 0)
        def _():
            dk_ref[...] = jnp.zeros_like(dk_ref)
            dv_ref[...] = jnp.zeros_like(dv_ref)

        _, lane = _iotas()
        lo_m = lane < HALF
        g = g_ref[...].astype(F32)
        sig = _sigmoid(g)
        dog_v = dog_ref[...]
        dg_ref[...] = (dog_v * o_ref[...] * (sig * (1.0 + g * (1.0 - sig)))).astype(dg_ref.dtype)
        dom_scr[...] = _head_masked((dog_v * g * sig).astype(BF16), lane)
        qm_scr[...] = _head_masked(q_ref[...], lane, SB_SCALE)
        c_scr[...] = jnp.zeros_like(c_scr)
        dq_scr[...] = jnp.zeros_like(dq_scr)

        def block(off, width, edge):
            mask = _key_mask(i, off, width, strict=True) if edge else None
            upper = _tri_ones(_tri(BLK, after=True))
            lower = _tri_ones(_tri(BLK, after=False))
            onehot = lane == off // WIDE
            hcs = [slice(h * BLK, (h + 1) * BLK) for h in range(8)]
            kbs = [k_ref[pl.ds(off, width), hc] for hc in hcs[:4]]
            vbs = [v_ref[pl.ds(off, width), hc] for hc in hcs[:4]]
            for heads in HEAD_GROUPS:
                zs = {h: _dot_nt(qm_scr[:, hcs[h]], kbs[h // 2]) for h in heads}
                dps = {h: _dot_nt(dom_scr[:, hcs[h]], vbs[h // 2]) for h in heads}
                lbs, l1s = {}, {}
                for h in heads:
                    lbs[h], l1s[h] = _sb_logits(zs[h])
                sufs = {h: _block_sums(_sel(mask, l1s[h], 0.0), upper, after=True)[0] for h in heads}
                prs, dzzs = {}, {}
                for h in heads:
                    expo = lbs[h] + sufs[h]
                    if width == WIDE:
                        expo = expo + jnp.sum(jnp.where(onehot, cm_ref[:, hcs[h]], 0.0), axis=1, keepdims=True)
                    pr = _sel(mask, jnp.exp(expo), 0.0)
                    dzzs[h] = pr * dps[h]
                    prs[h] = pr.astype(BF16)
                css = {h: _block_sums(dzzs[h], lower, after=False) for h in heads}
                dzbs = {}
                for h in heads:
                    c2 = c_scr[h]
                    prefix = css[h][0] + _widen(c2, width)
                    dz = _sel(mask, dzzs[h] * jnp.exp(l1s[h]) - jnp.exp(lbs[h]) * prefix, 0.0)
                    dzbs[h] = dz.astype(BF16)
                    c_scr[h] = c2 + css[h][1]
                dqs = {h: _dot(dzbs[h], kbs[h // 2]) for h in heads}
                dks = {h: _dot_tn(dzbs[h], qm_scr[:, hcs[h]]) for h in heads}
                dvs = {h: _dot_tn(prs[h], dom_scr[:, hcs[h]]) for h in heads}
                for p in sorted({h // 2 for h in heads}):
                    dq_scr[:, hcs[p]] += jnp.where(lo_m, dqs[2 * p], dqs[2 * p + 1])
                    dk_ref[pl.ds(off, width), hcs[p]] += dks[2 * p] + dks[2 * p + 1]
                    dv_ref[pl.ds(off, width), hcs[p]] += dvs[2 * p] + dvs[2 * p + 1]

        _over_key_blocks(i + 1, block, reverse=False)
        dq_ref[...] = (dq_scr[...] * SB_SCALE).astype(dq_ref.dtype)

    tile = lambda col: pl.BlockSpec((BLK, 512), lambda b, i: (b * nb + i, col))
    full = lambda col: pl.BlockSpec((lp, 512), lambda b, i: (b, col))
    acc = jax.ShapeDtypeStruct((t, 512), F32)
    once = jax.ShapeDtypeStruct((t, 512), BF16)
    hbm = pl.BlockSpec(memory_space=pl.ANY)
    outs = pl.pallas_call(
        body, name="sb_bwd", grid=(nbatch, nb),
        in_specs=[tile(0), full(1), full(2), tile(3), tile(0), tile(0),
                  pl.BlockSpec((BLK, 1024), lambda b, i: (b * nb + i, 0))] + [hbm] * ns,
        out_specs=[tile(0), full(0), full(0), tile(0)] + [hbm] * ns,
        out_shape=[once, acc, acc, once] + [jax.ShapeDtypeStruct(a.shape, a.dtype) for a in chunked],
        scratch_shapes=[pltpu.VMEM((8, BLK, BLK), F32), pltpu.VMEM((BLK, 1024), BF16),
                        pltpu.VMEM((BLK, 1024), BF16), pltpu.VMEM((BLK, 512), F32)] + _exchange_sems(ns),
        compiler_params=_params(("arbitrary", "arbitrary"), 56),
    )(proj, proj, proj, proj, o, dog, cm, *chunked)
    return outs[0], outs[1], outs[2], outs[3], list(outs[4:])


def _mla_prep(proj, gq, gkv, tabs, nbatch, nb):
    t = proj.shape[0]

    def body(mid_ref, gq_ref, gkv_ref, c_ref, s1_ref, s2_ref, cq_ref, ckv_ref, kr_ref):
        cq = mid_ref[:, 0:256].astype(F32)
        r = lax.rsqrt(jnp.mean(cq * cq, axis=1, keepdims=True) + NORM_EPS)
        cq_ref[...] = (cq * r * gq_ref[...]).astype(BF16)
        ckv = mid_ref[:, 256:384].astype(F32)
        r = lax.rsqrt(jnp.mean(ckv * ckv, axis=1, keepdims=True) + NORM_EPS)
        ckv_ref[...] = (ckv * r * gkv_ref[...]).astype(BF16)
        kr = mid_ref[:, 384:512].astype(F32)
        kr_ref[...] = _rope(kr, c_ref[...], s1_ref[...], s2_ref[...]).astype(BF16)

    tab = pl.BlockSpec((BLK, BLK), lambda b, i: (i, 0))
    rowspec = lambda w: pl.BlockSpec((BLK, w), lambda b, i: (b * nb + i, 0))
    return pl.pallas_call(
        body, name="mla_prep", grid=(nbatch, nb),
        in_specs=[pl.BlockSpec((BLK, 512), lambda b, i: (b * nb + i, EV_MID_BLK)),
                  pl.BlockSpec((1, 256), lambda b, i: (0, 0)), pl.BlockSpec((1, 128), lambda b, i: (0, 0)),
                  tab, tab, tab],
        out_specs=[rowspec(256), rowspec(128), rowspec(128)],
        out_shape=[jax.ShapeDtypeStruct((t, 256), BF16), jax.ShapeDtypeStruct((t, 128), BF16),
                   jax.ShapeDtypeStruct((t, 128), BF16)],
        compiler_params=_params(("parallel", "parallel")),
    )(proj, gq, gkv, *tabs)


def _mla_prep_bwd(proj, gq, gkv, tabs, dcqn, dckvn, dkrot, nbatch, nb):
    t = proj.shape[0]

    def body(mid_ref, gq_ref, gkv_ref, c_ref, s1_ref, s2_ref, dcq_ref, dckv_ref, dkr_ref,
             dmid_ref, dgq_ref, dgkv_ref):
        @pl.when((pl.program_id(0) == 0) & (pl.program_id(1) == 0))
        def _():
            dgq_ref[...] = jnp.zeros_like(dgq_ref)
            dgkv_ref[...] = jnp.zeros_like(dgkv_ref)

        def norm_bwd(x, gain, dy, dgain_ref):
            r = lax.rsqrt(jnp.mean(x * x, axis=1, keepdims=True) + NORM_EPS)
            nx = x * r
            dn = dy * gain
            dgain_ref[...] += jnp.sum(dy * nx, axis=0, keepdims=True)
            return r * (dn - nx * jnp.mean(dn * nx, axis=1, keepdims=True))

        dmid_ref[:, 0:256] = norm_bwd(
            mid_ref[:, 0:256].astype(F32), gq_ref[...], dcq_ref[...], dgq_ref).astype(BF16)
        dmid_ref[:, 256:384] = norm_bwd(
            mid_ref[:, 256:384].astype(F32), gkv_ref[...], dckv_ref[...], dgkv_ref).astype(BF16)
        dmid_ref[:, 384:512] = _rope_t(dkr_ref[...], c_ref[...], s1_ref[...], s2_ref[...]).astype(BF16)

    tab = pl.BlockSpec((BLK, BLK), lambda b, i: (i, 0))
    rowspec = lambda w: pl.BlockSpec((BLK, w), lambda b, i: (b * nb + i, 0))
    vq = pl.BlockSpec((1, 256), lambda b, i: (0, 0))
    vkv = pl.BlockSpec((1, 128), lambda b, i: (0, 0))
    return pl.pallas_call(
        body, name="mla_prep_bwd", grid=(nbatch, nb),
        in_specs=[pl.BlockSpec((BLK, 512), lambda b, i: (b * nb + i, EV_MID_BLK)), vq, vkv, tab, tab, tab,
                  rowspec(256), rowspec(128), rowspec(128)],
        out_specs=[rowspec(512), vq, vkv],
        out_shape=[jax.ShapeDtypeStruct((t, 512), BF16), jax.ShapeDtypeStruct((1, 256), F32),
                   jax.ShapeDtypeStruct((1, 128), F32)],
        compiler_params=_params(("arbitrary", "arbitrary")),
    )(proj, gq, gkv, *tabs, dcqn, dckvn, dkrot)


def _mla_scores(qf, kvb, krb, mask, lo_m):
    kf = jnp.where(lo_m, kvb, krb)
    return kf, _sel(mask, _dot_nt(qf, kf), NEG)


def _mla_fwd(qh, kvh, krot, proj, tabs, nbatch, nb):
    lp = nb * BLK
    t = nbatch * lp

    def body(q_ref, kv_ref, kr_ref, g_ref, c_ref, s1_ref, s2_ref, o_ref, og_ref, lse_ref,
             qf_scr, m_scr, l_scr, acc_scr):
        i = pl.program_id(1)
        row, lane = _iotas()
        lo_m = lane < HALF
        for h in range(8):
            hc = slice(h * BLK, (h + 1) * BLK)
            qf_scr[:, hc] = (_rope(q_ref[:, hc].astype(F32), c_ref[...], s1_ref[...], s2_ref[...])
                             * MLA_SCALE).astype(BF16)
        m_scr[...] = jnp.full(m_scr.shape, NEG, F32)
        l_scr[...] = jnp.zeros_like(l_scr)
        acc_scr[...] = jnp.zeros_like(acc_scr)

        def block(off, width, edge):
            mask = _key_mask(i, off, width, strict=False) if edge else None
            lo_k = lax.broadcasted_iota(jnp.int32, (width, BLK), 1) < HALF
            ones = jnp.ones((width, BLK), BF16)
            krb = kr_ref[pl.ds(off, width), :]
            heads = range(8)
            hcs = [slice(h * BLK, (h + 1) * BLK) for h in heads]
            kvbs = [kv_ref[pl.ds(off, width), hc] for hc in hcs]
            ss = [_mla_scores(qf_scr[:, hcs[h]], kvbs[h], krb, mask, lo_k)[1] for h in heads]
            ps, alphas = [], []
            for h in heads:
                m = m_scr[h]
                m2 = jnp.maximum(m, jnp.max(ss[h], axis=1, keepdims=True))
                ps.append(jnp.exp(ss[h] - _widen(m2, width)).astype(BF16))
                alphas.append(jnp.exp(m - m2))
                m_scr[h] = m2
            pvs = [_dot(ps[h], jnp.concatenate([kvbs[h], ones], axis=1)) for h in heads]
            for h in heads:
                l_scr[h] = alphas[h] * l_scr[h] + pvs[h][:, BLK:]
                acc_scr[h] = alphas[h] * acc_scr[h] + pvs[h][:, :BLK]

        _over_key_blocks(i + 1, block, reverse=False)
        lse = jnp.zeros((BLK, BLK), F32)
        for p in range(4):
            pc = slice(p * BLK, (p + 1) * BLK)
            o0 = acc_scr[2 * p] / l_scr[2 * p]
            o1 = acc_scr[2 * p + 1] / l_scr[2 * p + 1]
            o_ref[:, pc] = jnp.where(lo_m, pltpu.roll(o0, HALF, 1), o1)
            for h in (2 * p, 2 * p + 1):
                lse = lse + jnp.where(lane == h, m_scr[h] + jnp.log(l_scr[h]), 0.0)
        lse_ref[...] = lse
        g = g_ref[...].astype(F32)
        og_ref[...] = (o_ref[...] * g * _sigmoid(g)).astype(og_ref.dtype)

    tile = pl.BlockSpec((BLK, 512), lambda b, i: (b * nb + i, 0))
    tab = pl.BlockSpec((BLK, BLK), lambda b, i: (i, 0))
    heads = pltpu.VMEM((8, BLK, BLK), F32)
    return pl.pallas_call(
        body, name="mla_fwd", grid=(nbatch, nb),
        in_specs=[pl.BlockSpec((BLK, 1024), lambda b, i: (b * nb + i, 0)),
                  pl.BlockSpec((lp, 1024), lambda b, i: (b, 0)),
                  pl.BlockSpec((lp, BLK), lambda b, i: (b, 0)),
                  pl.BlockSpec((BLK, 512), lambda b, i: (b * nb + i, EV_GMLA_BLK // 4)),
                  tab, tab, tab],
        out_specs=[tile, tile, pl.BlockSpec((BLK, BLK), lambda b, i: (b * nb + i, 0))],
        out_shape=[jax.ShapeDtypeStruct((t, 512), F32), jax.ShapeDtypeStruct((t, 512), BF16),
                   jax.ShapeDtypeStruct((t, BLK), F32)],
        scratch_shapes=[pltpu.VMEM((BLK, 1024), BF16), heads, heads, heads],
        compiler_params=_params(("parallel", "arbitrary"), 48),
    )(qh, kvh, krot, proj, *tabs)


def _mla_bwd(qh, kvh, krot, proj, tabs, o, dog, lse, nbatch, nb):
    lp = nb * BLK
    t = nbatch * lp

    def body(q_ref, kv_ref, kr_ref, g_ref, c_ref, s1_ref, s2_ref, o_ref, dog_ref, lse_ref,
             dq_ref, dkv_ref, dkr_ref, dg_ref, qf_scr, do_scr, stat_scr, acc_scr):
        i = pl.program_id(1)

        @pl.when(i == 0)
        def _():
            dkv_ref[...] = jnp.zeros_like(dkv_ref)
            dkr_ref[...] = jnp.zeros_like(dkr_ref)

        row, lane = _iotas()
        lo_m = lane < HALF
        g = g_ref[...].astype(F32)
        sig = _sigmoid(g)
        dog_v = dog_ref[...]
        o_v = o_ref[...]
        dg_ref[...] = (dog_v * o_v * (sig * (1.0 + g * (1.0 - sig)))).astype(dg_ref.dtype)
        do = dog_v * g * sig
        do_o = do * o_v
        lse_blk = lse_ref[...]
        zero = jnp.zeros((BLK, BLK), F32)
        for h in range(8):
            hc = slice(h * BLK, (h + 1) * BLK)
            pc = slice((h // 2) * BLK, (h // 2 + 1) * BLK)
            qf_scr[:, hc] = (_rope(q_ref[:, hc].astype(F32), c_ref[...], s1_ref[...], s2_ref[...])
                             * MLA_SCALE).astype(BF16)
            dop = do[:, pc]
            do_src = dop if h % 2 else pltpu.roll(dop, HALF, 1)
            do_scr[:, hc] = jnp.where(lo_m, 0.0, do_src).astype(BF16)
            hm = (lane >= HALF) if h % 2 else lo_m
            stat_scr[h] = zero + jnp.sum(jnp.where(hm, do_o[:, pc], 0.0), axis=1, keepdims=True)
            stat_scr[8 + h] = zero + jnp.sum(jnp.where(lane == h, lse_blk, 0.0), axis=1, keepdims=True)
        acc_scr[...] = jnp.zeros_like(acc_scr)

        def block(off, width, edge):
            mask = _key_mask(i, off, width, strict=False) if edge else None
            lo_k = lax.broadcasted_iota(jnp.int32, (width, BLK), 1) < HALF
            krb = kr_ref[pl.ds(off, width), :]
            heads = range(8)
            hcs = [slice(h * BLK, (h + 1) * BLK) for h in heads]
            kvbs = [kv_ref[pl.ds(off, width), hc] for hc in hcs]
            qfs = [qf_scr[:, hc] for hc in hcs]
            dos = [do_scr[:, hc] for hc in hcs]
            scored = [_mla_scores(qfs[h], kvbs[h], krb, mask, lo_k) for h in heads]
            dps = [_dot_nt(dos[h], kvbs[h]) for h in heads]
            pbs, dss = [], []
            for h in heads:
                p = jnp.exp(scored[h][1] - _widen(stat_scr[8 + h], width))
                pbs.append(p.astype(BF16))
                dss.append((p * (dps[h] - _widen(stat_scr[h], width))).astype(BF16))
            dqs = [_dot(dss[h], scored[h][0]) for h in heads]
            dkfs = [_dot_tn(dss[h], qfs[h]) for h in heads]
            dvvs = [_dot_tn(pbs[h], dos[h]) for h in heads]
            dkr = jnp.zeros((width, BLK), F32)
            for h in heads:
                acc_scr[h] += dqs[h]
                dkv_ref[pl.ds(off, width), hcs[h]] += jnp.where(lo_k, dkfs[h], 0.0) + dvvs[h]
                dkr = dkr + jnp.where(lo_k, 0.0, dkfs[h])
            dkr_ref[pl.ds(off, width), :] += dkr

        _over_key_blocks(i + 1, block, reverse=False)
        for h in range(8):
            hc = slice(h * BLK, (h + 1) * BLK)
            dq_ref[:, hc] = _rope_t(acc_scr[h] * MLA_SCALE, c_ref[...], s1_ref[...], s2_ref[...]).astype(dq_ref.dtype)

    tile = lambda col: pl.BlockSpec((BLK, 512), lambda b, i: (b * nb + i, col))
    wide = pl.BlockSpec((BLK, 1024), lambda b, i: (b * nb + i, 0))
    full8 = pl.BlockSpec((lp, 1024), lambda b, i: (b, 0))
    full1 = pl.BlockSpec((lp, BLK), lambda b, i: (b, 0))
    tab = pl.BlockSpec((BLK, BLK), lambda b, i: (i, 0))
    return pl.pallas_call(
        body, name="mla_bwd", grid=(nbatch, nb),
        in_specs=[wide, full8, full1, tile(EV_GMLA_BLK // 4), tab, tab, tab, tile(0), tile(1),
                  pl.BlockSpec((BLK, BLK), lambda b, i: (b * nb + i, 0))],
        out_specs=[wide, full8, full1, tile(0)],
        out_shape=[jax.ShapeDtypeStruct((t, 1024), BF16), jax.ShapeDtypeStruct((t, 1024), F32),
                   jax.ShapeDtypeStruct((t, 128), F32), jax.ShapeDtypeStruct((t, 512), BF16)],
        scratch_shapes=[pltpu.VMEM((BLK, 1024), BF16), pltpu.VMEM((BLK, 1024), BF16),
                        pltpu.VMEM((16, BLK, BLK), F32), pltpu.VMEM((8, BLK, BLK), F32)],
        compiler_params=_params(("parallel", "arbitrary"), 56),
    )(qh, kvh, krot, proj, *tabs, o, dog, lse)


def _swa_setup(kk, i, k_refs, v_refs):
    row, lane = _iotas()
    own = (lane >= kk * HALF) & (lane < (kk + 1) * HALF)

    def dup(ref):
        x = ref[...].astype(F32)
        return jnp.where(own, x, pltpu.roll(x, HALF, 1)).astype(BF16)

    kcat = jnp.concatenate([dup(r) for r in k_refs], axis=0)
    vcat = jnp.concatenate([dup(r) for r in v_refs], axis=0)
    row2 = lax.broadcasted_iota(jnp.int32, (BLK, 2 * BLK), 0)
    lane2 = lax.broadcasted_iota(jnp.int32, (BLK, 2 * BLK), 1)
    is_meta = lane2 < BLK
    in_own = lane2 - BLK <= row2
    k_pos = jnp.where(is_meta, lane2, jnp.where(in_own, (i - 1) * BLK, (i - 2) * BLK) + lane2)
    d = i * BLK + row2 - k_pos
    mask = (d >= 0) & (k_pos >= jnp.where(is_meta, N_PAD, BLK))
    return lane, lane <= row, own, kcat, vcat, mask, d.astype(F32)


def _swa_fold(x, in_own):
    return jnp.concatenate([x[:, :BLK], jnp.where(in_own, x[:, 2 * BLK:], x[:, BLK:2 * BLK])], axis=1)


def _swa_unfold(x, in_own):
    w = x[:, BLK:]
    zero = jnp.zeros_like(w)
    return jnp.concatenate([x[:, :BLK], jnp.where(in_own, zero, w), jnp.where(in_own, w, zero)], axis=1)


def _swa_slope(kk, g_idx):
    return (2.0 ** (-(g_idx + 1) / 2.0)) * jnp.where(kk == 0, 1.0, 1.0 / 16.0)


def _swa_fwd(proj, sinks, nbatch, nb):
    lp = nb * BLK
    t = nbatch * lp

    def body(sink_ref, q_ref, ka, kb, kc, va, vb, vc, g_ref, o_ref, og_ref, lse_ref):
        kk = pl.program_id(1)
        i = pl.program_id(2)
        lane, in_own, own, kcat, vcat, mask, dist = _swa_setup(kk, i, (ka, kb, kc), (va, vb, vc))
        lo_m = lane < HALF
        heads = range(8)
        qms = []
        for h in heads:
            qp = q_ref[:, (h // 2) * BLK:(h // 2 + 1) * BLK]
            qms.append(jnp.where((lane >= HALF) if h % 2 else lo_m, qp, jnp.zeros_like(qp)) * SWA_SCALE)
        qks = [_dot_nt(qms[h], kcat) for h in heads]
        ps, ls, lses = [], [], []
        for h in heads:
            sink = sink_ref[kk, h]
            s = jnp.where(mask, _swa_fold(qks[h], in_own) - _swa_slope(kk, h) * dist, NEG)
            mx = jnp.maximum(jnp.max(s, axis=1, keepdims=True), sink)
            p = jnp.exp(s - mx)
            l = jnp.exp(sink - mx) + jnp.sum(p, axis=1, keepdims=True)
            ps.append(_swa_unfold(p.astype(BF16), in_own))
            ls.append(l)
            lses.append(mx + jnp.log(l))
        pvs = [_dot(ps[h], vcat) for h in heads]
        lse_out = jnp.zeros((BLK, BLK), F32)
        for m in range(4):
            cols = slice(m * BLK, (m + 1) * BLK)
            outp = jnp.where(lo_m, pvs[2 * m] / ls[2 * m], pvs[2 * m + 1] / ls[2 * m + 1])
            o_ref[:, cols] = outp
            g = g_ref[:, cols].astype(F32)
            og_ref[:, cols] = (outp * g * _sigmoid(g)).astype(og_ref.dtype)
            for h in (2 * m, 2 * m + 1):
                lse_out = lse_out + jnp.where(lane == h, lses[h], 0.0)
        lse_ref[...] = lse_out

    def kvspec(col, which):
        if which == 0:
            return pl.BlockSpec((BLK, BLK), lambda b, kk, i: (b * nb, col))
        if which == 1:
            return pl.BlockSpec((BLK, BLK), lambda b, kk, i: (b * nb + jnp.maximum(i - 1, 0), col))
        return pl.BlockSpec((BLK, BLK), lambda b, kk, i: (b * nb + i, col))

    wide = lambda c0: pl.BlockSpec((BLK, 512), lambda b, kk, i: (b * nb + i, c0 + kk))
    return pl.pallas_call(
        body, name="swa_fwd", grid=(nbatch, 2, nb),
        in_specs=[pl.BlockSpec(memory_space=pltpu.SMEM), wide(0),
                  kvspec(OD_K_BLK, 0), kvspec(OD_K_BLK, 1), kvspec(OD_K_BLK, 2),
                  kvspec(OD_V_BLK, 0), kvspec(OD_V_BLK, 1), kvspec(OD_V_BLK, 2), wide(2)],
        out_specs=[wide(0), wide(0), pl.BlockSpec((BLK, BLK), lambda b, kk, i: (b * nb + i, kk))],
        out_shape=[jax.ShapeDtypeStruct((t, 1024), F32), jax.ShapeDtypeStruct((t, 1024), BF16),
                   jax.ShapeDtypeStruct((t, 256), F32)],
        compiler_params=_params(("parallel", "parallel", "arbitrary")),
    )(sinks, proj, proj, proj, proj, proj, proj, proj, proj)


def _swa_bwd(proj, sinks, o, dog, lse, nbatch, nb):
    lp = nb * BLK
    t = nbatch * lp

    def body(sink_ref, q_ref, ka, kb, kc, va, vb, vc, g_ref, o_ref, dog_ref, lse_ref,
             dq_ref, dg_ref, dk_ref, dv_ref, dsink_ref):
        kk = pl.program_id(1)
        i = pl.program_id(2)

        @pl.when((kk == 0) & (i == 0))
        def _():
            dk_ref[...] = jnp.zeros_like(dk_ref)
            dv_ref[...] = jnp.zeros_like(dv_ref)

        @pl.when(i == 0)
        def _():
            dsink_ref[...] = jnp.zeros_like(dsink_ref)

        lane, in_own, own, kcat, vcat, mask, dist = _swa_setup(kk, i, (ka, kb, kc), (va, vb, vc))
        lo_m = lane < HALF
        row8 = lax.broadcasted_iota(jnp.int32, (8, BLK), 0)
        lse_blk = lse_ref[...]
        heads = range(8)
        qms, doms, deltas, lse_hs = [], [], [], []
        for m in range(4):
            cols = slice(m * BLK, (m + 1) * BLK)
            qp = q_ref[:, cols]
            g = g_ref[:, cols].astype(F32)
            sig = _sigmoid(g)
            dog_v = dog_ref[:, cols]
            o_v = o_ref[:, cols]
            dg_ref[:, cols] = (dog_v * o_v * (sig * (1.0 + g * (1.0 - sig)))).astype(dg_ref.dtype)
            do = dog_v * g * sig
            do_o = do * o_v
            dob = do.astype(BF16)
            for h in (2 * m, 2 * m + 1):
                hm = (lane >= HALF) if h % 2 else lo_m
                qms.append(jnp.where(hm, qp, jnp.zeros_like(qp)) * SWA_SCALE)
                doms.append(jnp.where(hm, dob, jnp.zeros_like(dob)))
                deltas.append(jnp.sum(jnp.where(hm, do_o, 0.0), axis=1, keepdims=True))
                lse_hs.append(jnp.sum(jnp.where(lane == h, lse_blk, 0.0), axis=1, keepdims=True))
        qks = [_dot_nt(qms[h], kcat) for h in heads]
        dps = [_dot_nt(doms[h], vcat) for h in heads]
        pbs, dss = [], []
        dsink = jnp.zeros((8, BLK), F32)
        for h in heads:
            s = jnp.where(mask, _swa_fold(qks[h], in_own) - _swa_slope(kk, h) * dist, NEG)
            p = jnp.exp(s - lse_hs[h])
            pbs.append(_swa_unfold(p.astype(BF16), in_own))
            dss.append(_swa_unfold((p * (_swa_fold(dps[h], in_own) - deltas[h])).astype(BF16), in_own))
            tot = jnp.sum(-jnp.exp(sink_ref[kk, h] - lse_hs[h]) * deltas[h], axis=0, keepdims=True)
            dsink = dsink + jnp.where(row8 == h, tot, 0.0)
        dsink_ref[...] += dsink
        dqs = [_dot(dss[h], kcat) for h in heads]
        dks = [_dot_tn(dss[h], qms[h]) for h in heads]
        dvs = [_dot_tn(pbs[h], doms[h]) for h in heads]
        for m in range(4):
            dq_ref[:, m * BLK:(m + 1) * BLK] = (
                jnp.where(lo_m, dqs[2 * m], dqs[2 * m + 1]) * SWA_SCALE).astype(dq_ref.dtype)
        dk = dks[0]
        dv = dvs[0]
        for h in range(1, 8):
            dk = dk + dks[h]
            dv = dv + dvs[h]
        offs = [0, pl.multiple_of(jnp.maximum(i - 1, 0) * BLK, BLK), pl.multiple_of(i * BLK, BLK)]
        for x in range(3):
            rows = slice(x * BLK, (x + 1) * BLK)
            dkx, dvx = dk[rows], dv[rows]
            dk_ref[pl.ds(offs[x], BLK), :] += jnp.where(own, dkx + pltpu.roll(dkx, HALF, 1), 0.0)
            dv_ref[pl.ds(offs[x], BLK), :] += jnp.where(own, dvx + pltpu.roll(dvx, HALF, 1), 0.0)

    def kvspec(col, which):
        if which == 0:
            return pl.BlockSpec((BLK, BLK), lambda b, kk, i: (b * nb, col))
        if which == 1:
            return pl.BlockSpec((BLK, BLK), lambda b, kk, i: (b * nb + jnp.maximum(i - 1, 0), col))
        return pl.BlockSpec((BLK, BLK), lambda b, kk, i: (b * nb + i, col))

    wide = lambda c0: pl.BlockSpec((BLK, 512), lambda b, kk, i: (b * nb + i, c0 + kk))
    full = pl.BlockSpec((lp, BLK), lambda b, kk, i: (b, 0))
    return pl.pallas_call(
        body, name="swa_bwd", grid=(nbatch, 2, nb),
        in_specs=[pl.BlockSpec(memory_space=pltpu.SMEM), wide(0),
                  kvspec(OD_K_BLK, 0), kvspec(OD_K_BLK, 1), kvspec(OD_K_BLK, 2),
                  kvspec(OD_V_BLK, 0), kvspec(OD_V_BLK, 1), kvspec(OD_V_BLK, 2), wide(2),
                  wide(0), wide(0), pl.BlockSpec((BLK, BLK), lambda b, kk, i: (b * nb + i, kk))],
        out_specs=[wide(0), wide(0), full, full,
                   pl.BlockSpec((8, BLK), lambda b, kk, i: (b * 2 + kk, 0))],
        out_shape=[jax.ShapeDtypeStruct((t, 1024), BF16), jax.ShapeDtypeStruct((t, 1024), BF16),
                   jax.ShapeDtypeStruct((t, 128), F32), jax.ShapeDtypeStruct((t, 128), F32),
                   jax.ShapeDtypeStruct((nbatch * 16, BLK), F32)],
        compiler_params=_params(("parallel", "arbitrary", "arbitrary")),
    )(sinks, proj, proj, proj, proj, proj, proj, proj, proj, o, dog, lse)


def _rope_tables(lp):
    pos = (jnp.arange(lp) - N_PAD).astype(F32)
    inv = ROPE_BASE ** (-jnp.arange(16, dtype=F32) / 16.0)
    ang = pos[:, None] * inv[None, :]
    cos, sin = jnp.cos(ang), jnp.sin(ang)
    z16 = jnp.zeros((lp, 16), F32)
    c = jnp.concatenate([jnp.ones((lp, 64), F32), cos, cos, jnp.zeros((lp, 32), F32)], axis=1)
    s1 = jnp.concatenate([jnp.zeros((lp, 64), F32), -sin, z16, jnp.zeros((lp, 32), F32)], axis=1)
    s2 = jnp.concatenate([jnp.zeros((lp, 64), F32), z16, sin, jnp.zeros((lp, 32), F32)], axis=1)
    return c, s1, s2


def _local_step(h0, tgt, norm_g, final_g, gq, gkv, sinks, w_ie, late_shards, nbatch, nb):
    lp = nb * BLK
    tabs = _rope_tables(lp)
    g0, g1 = norm_g[0:1], norm_g[1:2]
    sinks2 = sinks.reshape(2, 8)

    proj_e, hn0 = _norm_mm(h0, g0, w_ie, "proj_even")
    o_sb, og_sb, cm, gathered = _sb_fwd(proj_e, [late_shards[name] for name in _LATE], nbatch, nb)
    full = {name: _unchunk(name, blk) for name, blk in zip(_LATE, gathered)}
    w_uq, w_ukv, w_oe = _uq_to_compute(full["ev_w_uq"]), full["ev_w_ukv"], full["ev_w_out"]
    w_io, w_oo = _od_in_to_compute(full["od_w_in"]), full["od_w_out"]
    cqn, ckvn, krot = _mla_prep(proj_e, gq, gkv, tabs, nbatch, nb)
    qh = _mm(cqn, w_uq, "nn", "mla_uq", out_dtype=BF16)
    kvh = _mm(ckvn, w_ukv, "nn", "mla_ukv", out_dtype=BF16)
    o_mla, og_mla, lse_m = _mla_fwd(qh, kvh, krot, proj_e, tabs, nbatch, nb)
    h1 = _mm([og_sb, og_mla], w_oe, "nn", "out_even", add=h0)
    proj_o, hn1 = _norm_mm(h1, g1, w_io, "proj_odd")
    o_o, og_o, lse_o = _swa_fwd(proj_o, sinks2, nbatch, nb)
    h2 = _mm(og_o, w_oo, "nn", "out_odd", add=h1)
    dh2, lossv, d_final_g = _final(h2, tgt, final_g, nbatch, nb)

    dog_o = _mm(dh2, w_oo, "nt", "d_out_odd")
    d_w_oo, = _mm_tn(og_o, [dh2], "dw_out_odd")
    dq_o, dg_o, dk_o, dv_o, dsink = _swa_bwd(proj_o, sinks2, o_o, dog_o, lse_o, nbatch, nb)
    dproj_o = [dq_o, dg_o, dk_o, dv_o]
    dh1, d_g1 = _mm_norm_bwd(dproj_o, w_io, h1, g1, dh2, "d_proj_odd")
    dw_q, dw_g, dw_k, dw_v = _mm_tn(hn1, dproj_o, "dw_proj_odd")

    dog_e = _mm(dh1, w_oe, "nt", "d_out_even")
    d_w_oe_sb, = _mm_tn(og_sb, [dh1], "dw_out_even_sb")
    d_w_oe_mla, = _mm_tn(og_mla, [dh1], "dw_out_even_mla")
    early = dict(od_w_in=jnp.concatenate([dw_q, dw_k, dw_v, dw_g], axis=1), od_w_out=d_w_oo,
                 ev_w_out=jnp.concatenate([d_w_oe_sb, d_w_oe_mla], axis=0))
    dq_sb, dk_sb, dv_sb, dg_sb, received = _sb_bwd(
        proj_e, o_sb, dog_e, cm, [_chunk(name, early[name]) for name in _EARLY_GRADS], nbatch, nb)
    dqh, dkvh, dkrot, dg_mla = _mla_bwd(qh, kvh, krot, proj_e, tabs, o_mla, dog_e, lse_m, nbatch, nb)
    dcqn = _mm(dqh, w_uq, "nt", "d_mla_uq")
    d_w_uq, = _mm_tn(cqn, [dqh], "dw_mla_uq")
    dckvn = _mm(dkvh, w_ukv, "nt", "d_mla_ukv")
    d_w_ukv, = _mm_tn(ckvn, [dkvh], "dw_mla_ukv")
    dmid, d_gq, d_gkv = _mla_prep_bwd(proj_e, gq, gkv, tabs, dcqn, dckvn, dkrot, nbatch, nb)
    dproj_e = [dq_sb, dk_sb, dv_sb, dg_sb, dmid, dg_mla]
    dh0, d_g0 = _mm_norm_bwd(dproj_e, w_ie, h0, g0, dh1, "d_proj_even")
    dw_e = _mm_tn(hn0, dproj_e, "dw_proj_even")

    d_sinks = dsink.reshape(nbatch, 2, 8, BLK)[:, :, :, 0].sum(axis=0).reshape(1, 16)
    d_norm_g = jnp.concatenate([d_g0, d_g1], axis=0)
    d_ev_w_in = jnp.concatenate(list(dw_e[:4]) + [dw_e[4][:, :384], dw_e[4][:, 448:480], dw_e[5]], axis=1)
    return dict(lossv=lossv, dh0=dh0, norm_g=d_norm_g, final_g=d_final_g, gq=d_gq, gkv=d_gkv, sinks=d_sinks,
                ev_w_in=d_ev_w_in, ev_w_uq=_uq_from_compute(d_w_uq), ev_w_ukv=d_w_ukv, received=received)


def _ev_in_to_compute(w):
    z = lambda n: jnp.zeros((w.shape[0], n), w.dtype)
    return jnp.concatenate([w[:, :2432], z(64), w[:, 2432:2464], z(32), w[:, 2464:]], axis=1)


def _uq_to_compute(w):
    w3 = w.reshape(256, 8, 96)
    return jnp.concatenate([w3, jnp.zeros((256, 8, 32), w.dtype)], axis=2).reshape(256, 1024)


def _uq_from_compute(w):
    return w.reshape(256, 8, 128)[:, :, :96].reshape(256, 768)


def _od_in_to_compute(w):
    return jnp.concatenate([w[:, :1024], w[:, 1280:], w[:, 1024:1280]], axis=1)


_BIG = dict(ev_w_in=(1024, 2976, 1), ev_w_uq=(256, 768, 1), ev_w_ukv=(128, 1024, 1),
            ev_w_out=(1024, 1024, 0), od_w_in=(1024, 2304, 1), od_w_out=(1024, 1024, 0))
_LATE = ("ev_w_uq", "ev_w_ukv", "ev_w_out", "od_w_in", "od_w_out")
_EARLY_GRADS = ("od_w_in", "od_w_out", "ev_w_out")
_LAST_GRADS = ("ev_w_in", "ev_w_uq", "ev_w_ukv")


def _unchunk(name, blk):
    rows, cols, axis = _BIG[name]
    return blk.transpose(1, 0, 2).reshape(rows, cols) if axis == 1 else blk.reshape(rows, cols)


def _chunk(name, g):
    rows, cols, axis = _BIG[name]
    g = g.astype(BF16)
    return g.reshape(rows, N_DEV, cols // N_DEV).transpose(1, 0, 2) if axis == 1 else g.reshape(N_DEV, rows // N_DEV, cols)


def _all_gather(shards, name):
    n = len(shards)

    def body(*refs):
        xs, outs = refs[:n], refs[n:2 * n]
        send_sems, recv_sems, local_sems = refs[2 * n:]
        x, y, c = lax.axis_index("x"), lax.axis_index("y"), lax.axis_index("c")
        me, sibling = (x, y, c), (x, y, 1 - c)
        chips = [(1 - x, y), (x, 1 - y), (1 - x, 1 - y)]
        arrays = range(n)

        def copy(k, a, block, to, from_input=False):
            px, py, pc = block
            dst = outs[a].at[4 * px + 2 * py + pc]
            return pltpu.make_async_remote_copy(
                src_ref=xs[a] if from_input else dst, dst_ref=dst,
                send_sem=send_sems.at[k, a], recv_sem=recv_sems.at[k, a],
                device_id=to, device_id_type=pl.DeviceIdType.MESH)

        mine = [pltpu.make_async_copy(xs[a], outs[a].at[4 * x + 2 * y + c], local_sems.at[a]) for a in arrays]
        for cp in mine:
            cp.start()
        first = [copy(0, a, me, sibling, True) for a in arrays]
        for j, chip in enumerate(chips):
            first += [copy(1 + j, a, me, (*chip, c), True) for a in arrays]
        for cp in first:
            cp.start()
        passed = []
        for j, chip in enumerate(chips):
            for a in arrays:
                copy(1 + j, a, (*chip, c), me).wait_recv()
                passed.append(copy(4 + j, a, (*chip, c), sibling))
                passed[-1].start()
        for a in arrays:
            copy(0, a, sibling, me).wait_recv()
        for j, chip in enumerate(chips):
            for a in arrays:
                copy(4 + j, a, (*chip, 1 - c), me).wait_recv()
        for cp in first + passed:
            cp.wait_send()
        for cp in mine:
            cp.wait()

    hbm = pl.BlockSpec(memory_space=pl.ANY)
    return pl.pallas_call(
        body, name=name,
        out_shape=[jax.ShapeDtypeStruct((N_DEV,) + s.shape, s.dtype) for s in shards],
        in_specs=[hbm] * n, out_specs=[hbm] * n,
        scratch_shapes=[pltpu.SemaphoreType.DMA((7, n)), pltpu.SemaphoreType.DMA((7, n)),
                        pltpu.SemaphoreType.DMA((n,))],
    )(*shards)


def _exchange_sum(chunked, received, name):
    n, m = len(chunked), len(received)
    arrs = list(chunked) + list(received)

    def body(*refs):
        ins, outs = refs[:n + m], refs[n + m:2 * (n + m)]
        bufs = refs[2 * (n + m):3 * (n + m)]
        send_sems, recv_sems, local_sems, load_sems = refs[3 * (n + m):]
        loads = [pltpu.make_async_copy(ins[n + a], bufs[n + a], load_sems.at[a]) for a in range(m)]
        for cp in loads:
            cp.start()
        local, sends, recvs = _direct_exchange(ins[:n], bufs[:n], send_sems, recv_sems, local_sems, chunked=True)
        for cp in local + sends:
            cp.start()
        for a, cp in enumerate(loads):
            cp.wait()
            _sum_slots(bufs[n + a], outs[n + a])
        for cp in local:
            cp.wait()
        for cp in recvs:
            cp.wait_recv()
        for cp in sends:
            cp.wait_send()
        for a in range(n):
            _sum_slots(bufs[a], outs[a])

    hbm = pl.BlockSpec(memory_space=pl.ANY)
    vm = pl.BlockSpec(memory_space=pltpu.VMEM)
    return pl.pallas_call(
        body, name=name,
        out_shape=[jax.ShapeDtypeStruct(a.shape[1:], F32) for a in arrs],
        in_specs=[hbm] * (n + m), out_specs=[vm] * (n + m),
        scratch_shapes=[pltpu.VMEM(a.shape, a.dtype) for a in arrs] + _exchange_sems(n)
        + [pltpu.SemaphoreType.DMA((max(m, 1),))],
        compiler_params=pltpu.CompilerParams(vmem_limit_bytes=48 << 20),
    )(*arrs)


def _sum_slots(buf, out):
    rows = buf.shape[1]

    def add(sl):
        acc = buf[(0,) + sl].astype(F32)
        for k in range(1, N_DEV):
            acc = acc + buf[(k,) + sl].astype(F32)
        out[sl] = acc

    if rows > BLK and rows % BLK == 0:
        def step(r, carry):
            add((pl.ds(pl.multiple_of(r * BLK, BLK), BLK), slice(None)))
            return carry

        lax.fori_loop(0, rows // BLK, step, 0)
    else:
        add((slice(None), slice(None)))


def _adamw(ws, gs, ms, vs):
    n = len(ws)

    def body(*refs):
        ins, outs = refs[:4 * n], refs[4 * n:]
        for k in range(n):
            w_ref, g_ref, m_ref, v_ref = ins[4 * k:4 * k + 4]
            d_ref, nm_ref, nv_ref = outs[3 * k:3 * k + 3]

            def update(sl, w_ref=w_ref, g_ref=g_ref, m_ref=m_ref, v_ref=v_ref,
                       d_ref=d_ref, nm_ref=nm_ref, nv_ref=nv_ref):
                g = g_ref[sl]
                m = ADAM_B1 * m_ref[sl] + (1.0 - ADAM_B1) * g
                v = ADAM_B2 * v_ref[sl] + (1.0 - ADAM_B2) * (g * g)
                m_hat = m / (1.0 - ADAM_B1 ** ADAM_STEP)
                v_hat = v / (1.0 - ADAM_B2 ** ADAM_STEP)
                d_ref[sl] = -ADAM_LR * (m_hat / (jnp.sqrt(v_hat) + ADAM_EPS) + ADAM_WD * w_ref[sl])
                nm_ref[sl] = m
                nv_ref[sl] = v

            rows = w_ref.shape[0]
            if rows > BLK and rows % BLK == 0:
                def step(r, carry, update=update):
                    update((pl.ds(pl.multiple_of(r * BLK, BLK), BLK), slice(None)))
                    return carry

                lax.fori_loop(0, rows // BLK, step, 0)
            else:
                update((slice(None), slice(None)))

    args, out_shape = [], []
    for k in range(n):
        args += [ws[k], gs[k], ms[k], vs[k]]
        out_shape += [jax.ShapeDtypeStruct(ws[k].shape, F32)] * 3
    vm = pl.BlockSpec(memory_space=pltpu.VMEM)
    outs = pl.pallas_call(
        body, name="adamw", out_shape=out_shape,
        in_specs=[vm] * (4 * n), out_specs=[vm] * (3 * n),
        compiler_params=pltpu.CompilerParams(vmem_limit_bytes=48 << 20),
    )(*args)
    return [tuple(outs[3 * k:3 * k + 3]) for k in range(n)]


def kernel(x, meta, norm_g, final_g, ev_w_in, ev_q_norm_g, ev_kv_norm_g, ev_w_uq, ev_w_ukv, ev_w_out, od_w_in, od_sinks, od_w_out, loss_target, m_meta, m_norm_g, m_final_g, m_ev_w_in, m_ev_q_norm_g, m_ev_kv_norm_g, m_ev_w_uq, m_ev_w_ukv, m_ev_w_out, m_od_w_in, m_od_sinks, m_od_w_out, v_meta, v_norm_g, v_final_g, v_ev_w_in, v_ev_q_norm_g, v_ev_kv_norm_g, v_ev_w_uq, v_ev_w_ukv, v_ev_w_out, v_od_w_in, v_od_sinks, v_od_w_out):
    nbatch, seq, d = x.shape
    nb = seq // BLK + 1
    lp = nb * BLK
    shards = dict(ev_w_in=ev_w_in[0], ev_w_uq=ev_w_uq[0], ev_w_ukv=ev_w_ukv[0], ev_w_out=ev_w_out[0],
                  od_w_in=od_w_in[0], od_w_out=od_w_out[0])

    w_ie_blocks, meta_blocks = _all_gather([shards["ev_w_in"].astype(BF16), meta], "gather_weights")
    meta_full = meta_blocks.transpose(1, 0, 2).reshape(N_META, d)

    head = jnp.concatenate([jnp.zeros((N_PAD, d), F32), meta_full], axis=0)
    h0 = jnp.concatenate([jnp.broadcast_to(head[None], (nbatch, BLK, d)), x], axis=1).reshape(nbatch * lp, d)
    grads = _local_step(
        h0, loss_target.reshape(nbatch * seq, d), norm_g, final_g.reshape(1, d), ev_q_norm_g, ev_kv_norm_g,
        od_sinks, _ev_in_to_compute(_unchunk("ev_w_in", w_ie_blocks)),
        {name: shards[name].astype(BF16) for name in _LATE}, nbatch, nb)
    dh0 = grads["dh0"].reshape(nbatch, lp, d)
    grad_x = dh0[:, BLK:]

    d_meta = dh0[:, N_PAD:BLK].sum(axis=0).reshape(N_META, N_DEV, BLK).transpose(1, 0, 2)
    pad = lambda a, n: jnp.concatenate([a.reshape(1, -1), jnp.zeros((1, n - a.size), F32)], axis=1)
    loss_part = (0.5 / d * jnp.sum(grads["lossv"])).reshape(1, 1)
    rep = jnp.concatenate([grads["norm_g"].reshape(1, -1), grads["final_g"], grads["gq"], pad(loss_part, 256),
                           pad(grads["gkv"], 256), pad(grads["sinks"], 256)], axis=1).reshape(32, BLK)
    small = jnp.concatenate([d_meta, jnp.broadcast_to(rep[None], (N_DEV, 32, BLK))], axis=1)
    reduced = _exchange_sum([_chunk(name, grads[name]) for name in _LAST_GRADS] + [small],
                            grads["received"], "reduce_grads")
    g_shard = dict(zip(_LAST_GRADS + ("small",) + _EARLY_GRADS, reduced))
    red_small = g_shard.pop("small")
    rep = red_small[N_META:].reshape(1, -1)
    loss = rep[0, 3 * d + 256]
    g_small = dict(meta=red_small[:N_META], norm_g=rep[:, :2 * d].reshape(2, d), final_g=rep[:, 2 * d:3 * d],
                   ev_q_norm_g=rep[:, 3 * d:3 * d + 256], ev_kv_norm_g=rep[:, 3 * d + 512:3 * d + 640],
                   od_sinks=rep[:, 3 * d + 768:3 * d + 784])

    names = ["meta", "norm_g", "final_g", "ev_w_in", "ev_q_norm_g", "ev_kv_norm_g", "ev_w_uq", "ev_w_ukv",
             "ev_w_out", "od_w_in", "od_sinks", "od_w_out"]
    given = dict(meta=(meta, m_meta, v_meta), norm_g=(norm_g, m_norm_g, v_norm_g),
                 final_g=(final_g, m_final_g, v_final_g), ev_w_in=(ev_w_in, m_ev_w_in, v_ev_w_in),
                 ev_q_norm_g=(ev_q_norm_g, m_ev_q_norm_g, v_ev_q_norm_g),
                 ev_kv_norm_g=(ev_kv_norm_g, m_ev_kv_norm_g, v_ev_kv_norm_g),
                 ev_w_uq=(ev_w_uq, m_ev_w_uq, v_ev_w_uq), ev_w_ukv=(ev_w_ukv, m_ev_w_ukv, v_ev_w_ukv),
                 ev_w_out=(ev_w_out, m_ev_w_out, v_ev_w_out), od_w_in=(od_w_in, m_od_w_in, v_od_w_in),
                 od_sinks=(od_sinks, m_od_sinks, v_od_sinks), od_w_out=(od_w_out, m_od_w_out, v_od_w_out))
    ws, gs, ms, vs = [], [], [], []
    for name in names:
        g2 = g_shard[name] if name in g_shard else g_small[name]
        w, m, v = given[name]
        ws.append(w.reshape(g2.shape))
        ms.append(m.reshape(g2.shape))
        vs.append(v.reshape(g2.shape))
        gs.append(g2)
    upd = _adamw(ws, gs, ms, vs)
    shape_of = {name: given[name][0].shape for name in names}
    grads_out = [gs[k].reshape(shape_of[n]) for k, n in enumerate(names)]
    deltas = [upd[k][0].reshape(shape_of[n]) for k, n in enumerate(names)]
    new_m = [upd[k][1].reshape(shape_of[n]) for k, n in enumerate(names)]
    new_v = [upd[k][2].reshape(shape_of[n]) for k, n in enumerate(names)]
    return (loss, grad_x, *grads_out, *deltas, *new_m, *new_v)
```

```python
import jax
import jax.numpy as jnp
from jax import lax
from jax.experimental import pallas as pl
from jax.experimental.pallas import tpu as pltpu

F32 = jnp.float32
BF16 = jnp.bfloat16

D_MODEL = 1024
N_META = 16
BLK = 128
HALF = 64
N_PAD = BLK - N_META
NORM_EPS = 1e-6
NEG = -1e30
N_DEV = 8

SB_SCALE = 64 ** -0.5
MLA_SCALE = 96 ** -0.5
SWA_SCALE = 64 ** -0.5
ROPE_BASE = 10000.0

EV_IN_PAD = 3072
EV_MID_BLK = 4
EV_GMLA_BLK = 20
OD_K_BLK = 16
OD_V_BLK = 17

ADAM_LR = 0.001
ADAM_B1 = 0.9
ADAM_B2 = 0.999
ADAM_EPS = 1e-08
ADAM_WD = 0.01
ADAM_STEP = 10


def _dot(a, b):
    return lax.dot_general(a, b, (((1,), (0,)), ((), ())), preferred_element_type=F32)


def _dot_nt(a, b):
    return lax.dot_general(a, b, (((1,), (1,)), ((), ())), preferred_element_type=F32)


def _dot_tn(a, b):
    return lax.dot_general(a, b, (((0,), (0,)), ((), ())), preferred_element_type=F32)


def _sigmoid(x):
    return 1.0 / (1.0 + jnp.exp(-x))


def _iotas():
    row = lax.broadcasted_iota(jnp.int32, (BLK, BLK), 0)
    lane = lax.broadcasted_iota(jnp.int32, (BLK, BLK), 1)
    return row, lane


WIDE = 2 * BLK
HEAD_GROUPS = (range(0, 8),)


def _key_mask(i, first_key, width, strict):
    t_pos = i * BLK + lax.broadcasted_iota(jnp.int32, (BLK, width), 0)
    s_pos = first_key + lax.broadcasted_iota(jnp.int32, (BLK, width), 1)
    seen = (s_pos < t_pos) if strict else (s_pos <= t_pos)
    return seen & (s_pos >= N_PAD)


def _widen(x, width):
    return x if width == BLK else jnp.concatenate([x] * (width // BLK), axis=1)


def _over_key_blocks(n, block, reverse):
    pairs = n // 2
    last = pl.multiple_of((n - 1) * BLK, BLK)

    def step(jj, carry):
        jp = (pairs - 1 - jj) if reverse else jj
        off = pl.multiple_of(jp * WIDE, WIDE)
        edge = (jp == 0) | (jp == pairs - 1)
        pl.when(edge)(lambda: block(off, WIDE, True))
        pl.when(jnp.logical_not(edge))(lambda: block(off, WIDE, False))
        return carry

    if reverse:
        pl.when(n % 2 == 1)(lambda: block(last, BLK, True))
        lax.fori_loop(0, pairs, step, 0)
    else:
        lax.fori_loop(0, pairs, step, 0)
        pl.when(n % 2 == 1)(lambda: block(last, BLK, True))


def _sel(mask, x, fill):
    return x if mask is None else jnp.where(mask, x, fill)


def _rope(x, c, s1, s2):
    return x * c + pltpu.roll(x, BLK - 16, 1) * s1 + pltpu.roll(x, 16, 1) * s2


def _rope_t(x, c, s1, s2):
    return x * c - pltpu.roll(x, BLK - 16, 1) * s1 - pltpu.roll(x, 16, 1) * s2


def _params(sem, vmem_mb=None):
    kw = dict(dimension_semantics=sem)
    if vmem_mb is not None:
        kw["vmem_limit_bytes"] = vmem_mb << 20
    return pltpu.CompilerParams(**kw)


def _row_tile(t, cands):
    for c in cands:
        if t % c == 0:
            return c
    raise ValueError(t)


def _mm(a, w, mode, name, add=None, out_dtype=F32):
    pieces = list(a) if isinstance(a, (list, tuple)) else [a]
    m = pieces[0].shape[0]
    n = w.shape[1] if mode == "nn" else w.shape[0]
    tm = _row_tile(m, (544, 256, 128) if n <= 1024 else (256, 128))
    widths = [p.shape[1] for p in pieces]
    offs = [sum(widths[:i]) for i in range(len(widths))]

    def body(*refs):
        p_refs = refs[:len(pieces)]
        w_ref = refs[len(pieces)]
        o_ref = refs[-1]
        acc = None
        for p_ref, off, wd in zip(p_refs, offs, widths):
            x = p_ref[...].astype(BF16)
            part = _dot(x, w_ref[off:off + wd, :]) if mode == "nn" else _dot_nt(x, w_ref[:, off:off + wd])
            acc = part if acc is None else acc + part
        if add is not None:
            acc = acc + refs[len(pieces) + 1][...]
        o_ref[...] = acc.astype(o_ref.dtype)

    in_specs = [pl.BlockSpec((tm, wd), lambda i: (i, 0)) for wd in widths]
    in_specs.append(pl.BlockSpec(w.shape, lambda i: (0, 0)))
    args = pieces + [w]
    if add is not None:
        in_specs.append(pl.BlockSpec((tm, n), lambda i: (i, 0)))
        args.append(add)
    return pl.pallas_call(
        body, name=name, grid=(m // tm,), in_specs=in_specs,
        out_specs=pl.BlockSpec((tm, n), lambda i: (i, 0)),
        out_shape=jax.ShapeDtypeStruct((m, n), out_dtype),
        compiler_params=_params(("parallel",), 48),
    )(*args)


def _mm_tn(x, pieces, name):
    t, k = x.shape
    tt = _row_tile(t, (544, 256, 128))
    widths = [p.shape[1] for p in pieces]

    def body(*refs):
        x_ref = refs[0]
        d_refs = refs[1:1 + len(pieces)]
        o_refs = refs[1 + len(pieces):]

        @pl.when(pl.program_id(0) == 0)
        def _():
            for o_ref in o_refs:
                o_ref[...] = jnp.zeros_like(o_ref)

        xb = x_ref[...].astype(BF16)
        for d_ref, o_ref in zip(d_refs, o_refs):
            o_ref[...] += _dot_tn(xb, d_ref[...].astype(BF16))

    return pl.pallas_call(
        body, name=name, grid=(t // tt,),
        in_specs=[pl.BlockSpec((tt, k), lambda i: (i, 0))] + [pl.BlockSpec((tt, wd), lambda i: (i, 0)) for wd in widths],
        out_specs=[pl.BlockSpec((k, wd), lambda i: (0, 0)) for wd in widths],
        out_shape=[jax.ShapeDtypeStruct((k, wd), F32) for wd in widths],
        compiler_params=_params(("arbitrary",), 56),
    )(x, *pieces)


def _norm_mm(h, g, w, name):
    t, d = h.shape
    n = w.shape[1]
    tr = _row_tile(t, (544, 256, 128))

    def body(h_ref, g_ref, w_ref, o_ref, hn_ref):
        x = h_ref[...]
        r = lax.rsqrt(jnp.mean(x * x, axis=1, keepdims=True) + NORM_EPS)
        hn = (x * r * g_ref[...]).astype(BF16)
        hn_ref[...] = hn
        o_ref[...] = _dot(hn, w_ref[...]).astype(o_ref.dtype)

    row = lambda width: pl.BlockSpec((tr, width), lambda i: (i, 0))
    return pl.pallas_call(
        body, name=name, grid=(t // tr,),
        in_specs=[row(d), pl.BlockSpec((1, d), lambda i: (0, 0)), pl.BlockSpec(w.shape, lambda i: (0, 0))],
        out_specs=[row(n), row(d)],
        out_shape=[jax.ShapeDtypeStruct((t, n), BF16), jax.ShapeDtypeStruct((t, d), BF16)],
        compiler_params=_params(("parallel",), 48),
    )(h, g, w)


def _mm_norm_bwd(pieces, w, h, g, dres, name):
    t, d = h.shape
    tr = _row_tile(t, (544, 256, 128))
    widths = [p.shape[1] for p in pieces]
    offs = [sum(widths[:i]) for i in range(len(widths))]

    def body(*refs):
        p_refs = refs[:len(pieces)]
        w_ref, h_ref, g_ref, dres_ref, dh_ref, dg_ref = refs[len(pieces):]

        @pl.when(pl.program_id(0) == 0)
        def _():
            dg_ref[...] = jnp.zeros_like(dg_ref)

        dy = None
        for p_ref, off, wd in zip(p_refs, offs, widths):
            part = _dot_nt(p_ref[...].astype(BF16), w_ref[:, off:off + wd])
            dy = part if dy is None else dy + part
        x = h_ref[...]
        r = lax.rsqrt(jnp.mean(x * x, axis=1, keepdims=True) + NORM_EPS)
        nx = x * r
        dn = dy * g_ref[...]
        dh_ref[...] = dres_ref[...] + r * (dn - nx * jnp.mean(dn * nx, axis=1, keepdims=True))
        dg_ref[...] += jnp.sum(dy * nx, axis=0, keepdims=True)

    row = lambda width: pl.BlockSpec((tr, width), lambda i: (i, 0))
    vec = pl.BlockSpec((1, d), lambda i: (0, 0))
    return pl.pallas_call(
        body, name=name, grid=(t // tr,),
        in_specs=[row(wd) for wd in widths] + [pl.BlockSpec(w.shape, lambda i: (0, 0)), row(d), vec, row(d)],
        out_specs=[row(d), vec],
        out_shape=[jax.ShapeDtypeStruct((t, d), F32), jax.ShapeDtypeStruct((1, d), F32)],
        compiler_params=_params(("arbitrary",), 56),
    )(*pieces, w, h, g, dres)


def _final(h2, tgt, g, nbatch, nb):
    t, d = h2.shape

    def body(h_ref, t_ref, g_ref, dh_ref, loss_ref, dg_ref):
        b = pl.program_id(0)
        i = pl.program_id(1)

        @pl.when((b == 0) & (i == 0))
        def _():
            loss_ref[...] = jnp.zeros_like(loss_ref)
            dg_ref[...] = jnp.zeros_like(dg_ref)

        x = h_ref[...]
        r = lax.rsqrt(jnp.mean(x * x, axis=1, keepdims=True) + NORM_EPS)
        nx = x * r
        gg = g_ref[...]
        live = jnp.where(i >= 1, 1.0, 0.0)
        err = (nx * gg - t_ref[...]) * live
        loss_ref[...] += jnp.sum(err * err, axis=0, keepdims=True)
        dy = err * (1.0 / d)
        dn = dy * gg
        dh_ref[...] = r * (dn - nx * jnp.mean(dn * nx, axis=1, keepdims=True))
        dg_ref[...] += jnp.sum(dy * nx, axis=0, keepdims=True)

    vec = pl.BlockSpec((1, d), lambda b, i: (0, 0))
    return pl.pallas_call(
        body, name="final_loss", grid=(nbatch, nb),
        in_specs=[pl.BlockSpec((BLK, d), lambda b, i: (b * nb + i, 0)),
                  pl.BlockSpec((BLK, d), lambda b, i: (b * (nb - 1) + jnp.maximum(i - 1, 0), 0)),
                  vec],
        out_specs=[pl.BlockSpec((BLK, d), lambda b, i: (b * nb + i, 0)), vec, vec],
        out_shape=[jax.ShapeDtypeStruct((t, d), F32), jax.ShapeDtypeStruct((1, d), F32),
                   jax.ShapeDtypeStruct((1, d), F32)],
        compiler_params=_params(("arbitrary", "arbitrary")),
    )(h2, tgt, g)


def _direct_exchange(srcs, dsts, send_sems, recv_sems, local_sems, chunked):
    x, y, c = lax.axis_index("x"), lax.axis_index("y"), lax.axis_index("c")
    me = 4 * x + 2 * y + c
    arrays = range(len(srcs))
    local = [pltpu.make_async_copy(srcs[a].at[me] if chunked else srcs[a], dsts[a].at[me], local_sems.at[a])
             for a in arrays]
    sends, recvs = [], []
    for d in range(1, N_DEV):
        px = x + ((d >> 2) & 1) - 2 * x * ((d >> 2) & 1)
        py = y + ((d >> 1) & 1) - 2 * y * ((d >> 1) & 1)
        pc = c + (d & 1) - 2 * c * (d & 1)
        pid = 4 * px + 2 * py + pc
        for a in arrays:
            kw = dict(send_sem=send_sems.at[d - 1, a], recv_sem=recv_sems.at[d - 1, a],
                      device_id=(px, py, pc), device_id_type=pl.DeviceIdType.MESH)
            src = srcs[a].at[pid] if chunked else srcs[a]
            sends.append(pltpu.make_async_remote_copy(src_ref=src, dst_ref=dsts[a].at[me], **kw))
            recvs.append(pltpu.make_async_remote_copy(src_ref=src, dst_ref=dsts[a].at[pid], **kw))
    return local, sends, recvs


def _exchange_beside(first, last, srcs, dsts, sems, chunked):
    local, sends, recvs = _direct_exchange(srcs, dsts, *sems, chunked)

    @pl.when(first)
    def _():
        for cp in local + sends:
            cp.start()

    @pl.when(last)
    def _():
        for cp in local:
            cp.wait()
        for cp in recvs:
            cp.wait_recv()
        for cp in sends:
            cp.wait_send()


def _exchange_sems(n):
    return [pltpu.SemaphoreType.DMA((7, n)), pltpu.SemaphoreType.DMA((7, n)), pltpu.SemaphoreType.DMA((n,))]


def _sb_logits(z):
    log_beta = jnp.minimum(z, 0.0) - jnp.log(1.0 + jnp.exp(-jnp.abs(z)))
    return log_beta, log_beta - z


def _tri(width, after):
    j = lax.broadcasted_iota(jnp.int32, (width, width), 0)
    s = lax.broadcasted_iota(jnp.int32, (width, width), 1)
    return (j > s) if after else (j < s)


def _tri_ones(tri):
    return jnp.concatenate([tri.astype(BF16), jnp.ones((tri.shape[0], BLK), BF16)], axis=1)


def _block_sums(x, tri_ones, after):
    xb = x.astype(BF16)
    subs = [_dot(xb[:, s:s + BLK], tri_ones) for s in range(0, x.shape[1], BLK)]
    if len(subs) == 1:
        return subs[0][:, :BLK], subs[0][:, BLK:]
    first, second = subs
    total = first[:, BLK:] + second[:, BLK:]
    if after:
        return jnp.concatenate([first[:, :BLK] + second[:, BLK:], second[:, :BLK]], axis=1), total
    return jnp.concatenate([first[:, :BLK], second[:, :BLK] + first[:, BLK:]], axis=1), total


def _head_masked(x, lane, scale=None):
    out = []
    for h in range(8):
        xp = x[:, (h // 2) * BLK:(h // 2 + 1) * BLK]
        xm = jnp.where((lane >= HALF) if h % 2 else (lane < HALF), xp, jnp.zeros_like(xp))
        out.append(xm if scale is None else xm * scale)
    return jnp.concatenate(out, axis=1)


def _sb_fwd(proj, shards, nbatch, nb):
    lp = nb * BLK
    t = nbatch * lp

    ns = len(shards)

    def body(*refs):
        q_ref, k_ref, v_ref, g_ref = refs[:4]
        shard_refs = refs[4:4 + ns]
        o_ref, og_ref, cm_ref = refs[4 + ns:7 + ns]
        gathered_refs = refs[7 + ns:7 + 2 * ns]
        c_scr, qm_scr = refs[7 + 2 * ns:9 + 2 * ns]
        b = pl.program_id(0)
        i = pl.program_id(1)
        _exchange_beside((b == 0) & (i == 0), (b == nbatch - 1) & (i == nb - 1),
                         shard_refs, gathered_refs, refs[9 + 2 * ns:], chunked=False)
        _, lane = _iotas()
        lo_m = lane < HALF
        cm_ref[...] = jnp.zeros_like(cm_ref)
        c_scr[...] = jnp.zeros_like(c_scr)
        o_ref[...] = jnp.zeros_like(o_ref)
        qm_scr[...] = _head_masked(q_ref[...], lane, SB_SCALE)

        def block(off, width, edge):
            mask = _key_mask(i, off, width, strict=True) if edge else None
            upper = _tri_ones(_tri(BLK, after=True))
            onehot = lane == off // WIDE
            heads = range(8)
            hcs = [slice(h * BLK, (h + 1) * BLK) for h in heads]
            kbs = [k_ref[pl.ds(off, width), hc] for hc in hcs[:4]]
            vbs = [v_ref[pl.ds(off, width), hc] for hc in hcs[:4]]
            zs = [_dot_nt(qm_scr[:, hcs[h]], kbs[h // 2]) for h in heads]
            lbs, l1s = [], []
            for h in heads:
                log_beta, log_1m = _sb_logits(zs[h])
                lbs.append(log_beta)
                l1s.append(_sel(mask, log_1m, 0.0))
            css = [_block_sums(l1s[h], upper, after=True) for h in heads]
            avs = []
            for h in heads:
                c = c_scr[h]
                avs.append(_sel(mask, jnp.exp(lbs[h] + css[h][0] + _widen(c, width)), 0.0).astype(BF16))
                if width == WIDE:
                    cm_ref[:, hcs[h]] = jnp.where(onehot, c, cm_ref[:, hcs[h]])
                c_scr[h] = c + css[h][1]
            accs = [_dot(avs[h], vbs[h // 2]) for h in heads]
            for p in range(4):
                o_ref[:, hcs[p]] += jnp.where(lo_m, accs[2 * p], accs[2 * p + 1])

        _over_key_blocks(i + 1, block, reverse=True)
        g = g_ref[...].astype(F32)
        og_ref[...] = (o_ref[...] * g * _sigmoid(g)).astype(og_ref.dtype)

    tile = lambda col: pl.BlockSpec((BLK, 512), lambda b, i: (b * nb + i, col))
    full = lambda col: pl.BlockSpec((lp, 512), lambda b, i: (b, col))
    hbm = pl.BlockSpec(memory_space=pl.ANY)
    outs = pl.pallas_call(
        body, name="sb_fwd", grid=(nbatch, nb),
        in_specs=[tile(0), full(1), full(2), tile(3)] + [hbm] * ns,
        out_specs=[tile(0), tile(0), pl.BlockSpec((BLK, 1024), lambda b, i: (b * nb + i, 0))] + [hbm] * ns,
        out_shape=[jax.ShapeDtypeStruct((t, 512), F32), jax.ShapeDtypeStruct((t, 512), BF16),
                   jax.ShapeDtypeStruct((t, 1024), F32)]
        + [jax.ShapeDtypeStruct((N_DEV,) + s.shape, s.dtype) for s in shards],
        scratch_shapes=[pltpu.VMEM((8, BLK, BLK), F32), pltpu.VMEM((BLK, 1024), BF16)] + _exchange_sems(ns),
        compiler_params=_params(("arbitrary", "arbitrary"), 48),
    )(proj, proj, proj, proj, *shards)
    return outs[0], outs[1], outs[2], list(outs[3:])


def _sb_bwd(proj, o, dog, cm, chunked, nbatch, nb):
    lp = nb * BLK
    t = nbatch * lp
    ns = len(chunked)

    def body(*refs):
        q_ref, k_ref, v_ref, g_ref, o_ref, dog_ref, cm_ref = refs[:7]
        chunk_refs = refs[7:7 + ns]
        dq_ref, dk_ref, dv_ref, dg_ref = refs[7 + ns:11 + ns]
        received_refs = refs[11 + ns:11 + 2 * ns]
        c_scr, qm_scr, dom_scr, dq_scr = refs[11 + 2 * ns:15 + 2 * ns]
        b = pl.program_id(0)
        i = pl.program_id(1)
        _exchange_beside((b == 0) & (i == 0), (b == nbatch - 1) & (i == nb - 1),
                         chunk_refs, received_refs, refs[15 + 2 * ns:], chunked=True)

        @pl.when(i == 0)
        def _():
            dk_ref[...] = jnp.zeros_like(dk_ref)
            dv_ref[...] = jnp.zeros_like(dv_ref)

        _, lane = _iotas()
        lo_m = lane < HALF
        g = g_ref[...].astype(F32)
        sig = _sigmoid(g)
        dog_v = dog_ref[...]
        dg_ref[...] = (dog_v * o_ref[...] * (sig * (1.0 + g * (1.0 - sig)))).astype(dg_ref.dtype)
        dom_scr[...] = _head_masked((dog_v * g * sig).astype(BF16), lane)
        qm_scr[...] = _head_masked(q_ref[...], lane, SB_SCALE)
        c_scr[...] = jnp.zeros_like(c_scr)
        dq_scr[...] = jnp.zeros_like(dq_scr)

        def block(off, width, edge):
            mask = _key_mask(i, off, width, strict=True) if edge else None
            upper = _tri_ones(_tri(BLK, after=True))
            lower = _tri_ones(_tri(BLK, after=False))
            onehot = lane == off // WIDE
            hcs = [slice(h * BLK, (h + 1) * BLK) for h in range(8)]
            kbs = [k_ref[pl.ds(off, width), hc] for hc in hcs[:4]]
            vbs = [v_ref[pl.ds(off, width), hc] for hc in hcs[:4]]
            for heads in HEAD_GROUPS:
                zs = {h: _dot_nt(qm_scr[:, hcs[h]], kbs[h // 2]) for h in heads}
                dps = {h: _dot_nt(dom_scr[:, hcs[h]], vbs[h // 2]) for h in heads}
                lbs, l1s = {}, {}
                for h in heads:
                    lbs[h], l1s[h] = _sb_logits(zs[h])
                sufs = {h: _block_sums(_sel(mask, l1s[h], 0.0), upper, after=True)[0] for h in heads}
                prs, dzzs = {}, {}
                for h in heads:
                    expo = lbs[h] + sufs[h]
                    if width == WIDE:
                        expo = expo + jnp.sum(jnp.where(onehot, cm_ref[:, hcs[h]], 0.0), axis=1, keepdims=True)
                    pr = _sel(mask, jnp.exp(expo), 0.0)
                    dzzs[h] = pr * dps[h]
                    prs[h] = pr.astype(BF16)
                css = {h: _block_sums(dzzs[h], lower, after=False) for h in heads}
                dzbs = {}
                for h in heads:
                    c2 = c_scr[h]
                    prefix = css[h][0] + _widen(c2, width)
                    dz = _sel(mask, dzzs[h] * jnp.exp(l1s[h]) - jnp.exp(lbs[h]) * prefix, 0.0)
                    dzbs[h] = dz.astype(BF16)
                    c_scr[h] = c2 + css[h][1]
                dqs = {h: _dot(dzbs[h], kbs[h // 2]) for h in heads}
                dks = {h: _dot_tn(dzbs[h], qm_scr[:, hcs[h]]) for h in heads}
                dvs = {h: _dot_tn(prs[h], dom_scr[:, hcs[h]]) for h in heads}
                for p in sorted({h // 2 for h in heads}):
                    dq_scr[:, hcs[p]] += jnp.where(lo_m, dqs[2 * p], dqs[2 * p + 1])
                    dk_ref[pl.ds(off, width), hcs[p]] += dks[2 * p] + dks[2 * p + 1]
                    dv_ref[pl.ds(off, width), hcs[p]] += dvs[2 * p] + dvs[2 * p + 1]

        _over_key_blocks(i + 1, block, reverse=False)
        dq_ref[...] = (dq_scr[...] * SB_SCALE).astype(dq_ref.dtype)

    tile = lambda col: pl.BlockSpec((BLK, 512), lambda b, i: (b * nb + i, col))
    full = lambda col: pl.BlockSpec((lp, 512), lambda b, i: (b, col))
    acc = jax.ShapeDtypeStruct((t, 512), F32)
    once = jax.ShapeDtypeStruct((t, 512), BF16)
    hbm = pl.BlockSpec(memory_space=pl.ANY)
    outs = pl.pallas_call(
        body, name="sb_bwd", grid=(nbatch, nb),
        in_specs=[tile(0), full(1), full(2), tile(3), tile(0), tile(0),
                  pl.BlockSpec((BLK, 1024), lambda b, i: (b * nb + i, 0))] + [hbm] * ns,
        out_specs=[tile(0), full(0), full(0), tile(0)] + [hbm] * ns,
        out_shape=[once, acc, acc, once] + [jax.ShapeDtypeStruct(a.shape, a.dtype) for a in chunked],
        scratch_shapes=[pltpu.VMEM((8, BLK, BLK), F32), pltpu.VMEM((BLK, 1024), BF16),
                        pltpu.VMEM((BLK, 1024), BF16), pltpu.VMEM((BLK, 512), F32)] + _exchange_sems(ns),
        compiler_params=_params(("arbitrary", "arbitrary"), 56),
    )(proj, proj, proj, proj, o, dog, cm, *chunked)
    return outs[0], outs[1], outs[2], outs[3], list(outs[4:])


def _mla_prep(proj, gq, gkv, tabs, nbatch, nb):
    t = proj.shape[0]

    def body(mid_ref, gq_ref, gkv_ref, c_ref, s1_ref, s2_ref, cq_ref, ckv_ref, kr_ref):
        cq = mid_ref[:, 0:256].astype(F32)
        r = lax.rsqrt(jnp.mean(cq * cq, axis=1, keepdims=True) + NORM_EPS)
        cq_ref[...] = (cq * r * gq_ref[...]).astype(BF16)
        ckv = mid_ref[:, 256:384].astype(F32)
        r = lax.rsqrt(jnp.mean(ckv * ckv, axis=1, keepdims=True) + NORM_EPS)
        ckv_ref[...] = (ckv * r * gkv_ref[...]).astype(BF16)
        kr = mid_ref[:, 384:512].astype(F32)
        kr_ref[...] = _rope(kr, c_ref[...], s1_ref[...], s2_ref[...]).astype(BF16)

    tr = _row_tile(nb * BLK, (544, BLK))
    nt = nb * BLK // tr
    tab = pl.BlockSpec((tr, BLK), lambda b, i: (i, 0))
    rowspec = lambda w: pl.BlockSpec((tr, w), lambda b, i: (b * nt + i, 0))
    return pl.pallas_call(
        body, name="mla_prep", grid=(nbatch, nt),
        in_specs=[pl.BlockSpec((tr, 512), lambda b, i: (b * nt + i, EV_MID_BLK)),
                  pl.BlockSpec((1, 256), lambda b, i: (0, 0)), pl.BlockSpec((1, 128), lambda b, i: (0, 0)),
                  tab, tab, tab],
        out_specs=[rowspec(256), rowspec(128), rowspec(128)],
        out_shape=[jax.ShapeDtypeStruct((t, 256), BF16), jax.ShapeDtypeStruct((t, 128), BF16),
                   jax.ShapeDtypeStruct((t, 128), BF16)],
        compiler_params=_params(("parallel", "parallel")),
    )(proj, gq, gkv, *tabs)


def _mla_prep_bwd(proj, gq, gkv, tabs, dcqn, dckvn, dkrot, nbatch, nb):
    t = proj.shape[0]

    def body(mid_ref, gq_ref, gkv_ref, c_ref, s1_ref, s2_ref, dcq_ref, dckv_ref, dkr_ref,
             dmid_ref, dgq_ref, dgkv_ref):
        @pl.when((pl.program_id(0) == 0) & (pl.program_id(1) == 0))
        def _():
            dgq_ref[...] = jnp.zeros_like(dgq_ref)
            dgkv_ref[...] = jnp.zeros_like(dgkv_ref)

        def norm_bwd(x, gain, dy, dgain_ref):
            r = lax.rsqrt(jnp.mean(x * x, axis=1, keepdims=True) + NORM_EPS)
            nx = x * r
            dn = dy * gain
            dgain_ref[...] += jnp.sum(dy * nx, axis=0, keepdims=True)
            return r * (dn - nx * jnp.mean(dn * nx, axis=1, keepdims=True))

        dmid_ref[:, 0:256] = norm_bwd(
            mid_ref[:, 0:256].astype(F32), gq_ref[...], dcq_ref[...], dgq_ref).astype(BF16)
        dmid_ref[:, 256:384] = norm_bwd(
            mid_ref[:, 256:384].astype(F32), gkv_ref[...], dckv_ref[...], dgkv_ref).astype(BF16)
        dmid_ref[:, 384:512] = _rope_t(dkr_ref[...], c_ref[...], s1_ref[...], s2_ref[...]).astype(BF16)

    tr = _row_tile(nb * BLK, (544, BLK))
    nt = nb * BLK // tr
    tab = pl.BlockSpec((tr, BLK), lambda b, i: (i, 0))
    rowspec = lambda w: pl.BlockSpec((tr, w), lambda b, i: (b * nt + i, 0))
    vq = pl.BlockSpec((1, 256), lambda b, i: (0, 0))
    vkv = pl.BlockSpec((1, 128), lambda b, i: (0, 0))
    return pl.pallas_call(
        body, name="mla_prep_bwd", grid=(nbatch, nt),
        in_specs=[pl.BlockSpec((tr, 512), lambda b, i: (b * nt + i, EV_MID_BLK)), vq, vkv, tab, tab, tab,
                  rowspec(256), rowspec(128), rowspec(128)],
        out_specs=[rowspec(512), vq, vkv],
        out_shape=[jax.ShapeDtypeStruct((t, 512), BF16), jax.ShapeDtypeStruct((1, 256), F32),
                   jax.ShapeDtypeStruct((1, 128), F32)],
        compiler_params=_params(("arbitrary", "arbitrary")),
    )(proj, gq, gkv, *tabs, dcqn, dckvn, dkrot)


def _mla_scores(qf, kvb, krb, mask, lo_m):
    kf = jnp.where(lo_m, kvb, krb)
    return kf, _sel(mask, _dot_nt(qf, kf), NEG)


def _mla_fwd(qh, kvh, krot, proj, tabs, nbatch, nb):
    lp = nb * BLK
    t = nbatch * lp

    def body(q_ref, kv_ref, kr_ref, g_ref, c_ref, s1_ref, s2_ref, o_ref, og_ref, lse_ref,
             qf_scr, m_scr, l_scr, acc_scr):
        i = pl.program_id(1)
        row, lane = _iotas()
        lo_m = lane < HALF
        for h in range(8):
            hc = slice(h * BLK, (h + 1) * BLK)
            qf_scr[:, hc] = (_rope(q_ref[:, hc].astype(F32), c_ref[...], s1_ref[...], s2_ref[...])
                             * MLA_SCALE).astype(BF16)
        m_scr[...] = jnp.full(m_scr.shape, NEG, F32)
        l_scr[...] = jnp.zeros_like(l_scr)
        acc_scr[...] = jnp.zeros_like(acc_scr)

        def block(off, width, edge):
            mask = _key_mask(i, off, width, strict=False) if edge else None
            lo_k = lax.broadcasted_iota(jnp.int32, (width, BLK), 1) < HALF
            ones = jnp.ones((width, BLK), BF16)
            krb = kr_ref[pl.ds(off, width), :]
            heads = range(8)
            hcs = [slice(h * BLK, (h + 1) * BLK) for h in heads]
            kvbs = [kv_ref[pl.ds(off, width), hc] for hc in hcs]
            ss = [_mla_scores(qf_scr[:, hcs[h]], kvbs[h], krb, mask, lo_k)[1] for h in heads]
            ps, alphas = [], []
            for h in heads:
                m = m_scr[h]
                m2 = jnp.maximum(m, jnp.max(ss[h], axis=1, keepdims=True))
                ps.append(jnp.exp(ss[h] - _widen(m2, width)).astype(BF16))
                alphas.append(jnp.exp(m - m2))
                m_scr[h] = m2
            pvs = [_dot(ps[h], jnp.concatenate([kvbs[h], ones], axis=1)) for h in heads]
            for h in heads:
                l_scr[h] = alphas[h] * l_scr[h] + pvs[h][:, BLK:]
                acc_scr[h] = alphas[h] * acc_scr[h] + pvs[h][:, :BLK]

        _over_key_blocks(i + 1, block, reverse=False)
        lse = jnp.zeros((BLK, BLK), F32)
        for p in range(4):
            pc = slice(p * BLK, (p + 1) * BLK)
            o0 = acc_scr[2 * p] / l_scr[2 * p]
            o1 = acc_scr[2 * p + 1] / l_scr[2 * p + 1]
            o_ref[:, pc] = jnp.where(lo_m, pltpu.roll(o0, HALF, 1), o1)
            for h in (2 * p, 2 * p + 1):
                lse = lse + jnp.where(lane == h, m_scr[h] + jnp.log(l_scr[h]), 0.0)
        lse_ref[...] = lse
        g = g_ref[...].astype(F32)
        og_ref[...] = (o_ref[...] * g * _sigmoid(g)).astype(og_ref.dtype)

    tile = pl.BlockSpec((BLK, 512), lambda b, i: (b * nb + i, 0))
    tab = pl.BlockSpec((BLK, BLK), lambda b, i: (i, 0))
    heads = pltpu.VMEM((8, BLK, BLK), F32)
    return pl.pallas_call(
        body, name="mla_fwd", grid=(nbatch, nb),
        in_specs=[pl.BlockSpec((BLK, 1024), lambda b, i: (b * nb + i, 0)),
                  pl.BlockSpec((lp, 1024), lambda b, i: (b, 0)),
                  pl.BlockSpec((lp, BLK), lambda b, i: (b, 0)),
                  pl.BlockSpec((BLK, 512), lambda b, i: (b * nb + i, EV_GMLA_BLK // 4)),
                  tab, tab, tab],
        out_specs=[tile, tile, pl.BlockSpec((BLK, BLK), lambda b, i: (b * nb + i, 0))],
        out_shape=[jax.ShapeDtypeStruct((t, 512), F32), jax.ShapeDtypeStruct((t, 512), BF16),
                   jax.ShapeDtypeStruct((t, BLK), F32)],
        scratch_shapes=[pltpu.VMEM((BLK, 1024), BF16), heads, heads, heads],
        compiler_params=_params(("parallel", "arbitrary"), 48),
    )(qh, kvh, krot, proj, *tabs)


def _mla_bwd(qh, kvh, krot, proj, tabs, o, dog, lse, nbatch, nb):
    lp = nb * BLK
    t = nbatch * lp

    def body(q_ref, kv_ref, kr_ref, g_ref, c_ref, s1_ref, s2_ref, o_ref, dog_ref, lse_ref,
             dq_ref, dkv_ref, dkr_ref, dg_ref, qf_scr, do_scr, stat_scr, acc_scr):
        i = pl.program_id(1)

        @pl.when(i == 0)
        def _():
            dkv_ref[...] = jnp.zeros_like(dkv_ref)
            dkr_ref[...] = jnp.zeros_like(dkr_ref)

        row, lane = _iotas()
        lo_m = lane < HALF
        g = g_ref[...].astype(F32)
        sig = _sigmoid(g)
        dog_v = dog_ref[...]
        o_v = o_ref[...]
        dg_ref[...] = (dog_v * o_v * (sig * (1.0 + g * (1.0 - sig)))).astype(dg_ref.dtype)
        do = dog_v * g * sig
        do_o = do * o_v
        lse_blk = lse_ref[...]
        zero = jnp.zeros((BLK, BLK), F32)
        for h in range(8):
            hc = slice(h * BLK, (h + 1) * BLK)
            pc = slice((h // 2) * BLK, (h // 2 + 1) * BLK)
            qf_scr[:, hc] = (_rope(q_ref[:, hc].astype(F32), c_ref[...], s1_ref[...], s2_ref[...])
                             * MLA_SCALE).astype(BF16)
            dop = do[:, pc]
            do_src = dop if h % 2 else pltpu.roll(dop, HALF, 1)
            do_scr[:, hc] = jnp.where(lo_m, 0.0, do_src).astype(BF16)
            hm = (lane >= HALF) if h % 2 else lo_m
            stat_scr[h] = zero + jnp.sum(jnp.where(hm, do_o[:, pc], 0.0), axis=1, keepdims=True)
            stat_scr[8 + h] = zero + jnp.sum(jnp.where(lane == h, lse_blk, 0.0), axis=1, keepdims=True)
        acc_scr[...] = jnp.zeros_like(acc_scr)

        def block(off, width, edge):
            mask = _key_mask(i, off, width, strict=False) if edge else None
            lo_k = lax.broadcasted_iota(jnp.int32, (width, BLK), 1) < HALF
            krb = kr_ref[pl.ds(off, width), :]
            heads = range(8)
            hcs = [slice(h * BLK, (h + 1) * BLK) for h in heads]
            kvbs = [kv_ref[pl.ds(off, width), hc] for hc in hcs]
            qfs = [qf_scr[:, hc] for hc in hcs]
            dos = [do_scr[:, hc] for hc in hcs]
            scored = [_mla_scores(qfs[h], kvbs[h], krb, mask, lo_k) for h in heads]
            dps = [_dot_nt(dos[h], kvbs[h]) for h in heads]
            pbs, dss = [], []
            for h in heads:
                p = jnp.exp(scored[h][1] - _widen(stat_scr[8 + h], width))
                pbs.append(p.astype(BF16))
                dss.append((p * (dps[h] - _widen(stat_scr[h], width))).astype(BF16))
            dqs = [_dot(dss[h], scored[h][0]) for h in heads]
            dkfs = [_dot_tn(dss[h], qfs[h]) for h in heads]
            dvvs = [_dot_tn(pbs[h], dos[h]) for h in heads]
            dkr = jnp.zeros((width, BLK), F32)
            for h in heads:
                acc_scr[h] += dqs[h]
                dkv_ref[pl.ds(off, width), hcs[h]] += jnp.where(lo_k, dkfs[h], 0.0) + dvvs[h]
                dkr = dkr + jnp.where(lo_k, 0.0, dkfs[h])
            dkr_ref[pl.ds(off, width), :] += dkr

        _over_key_blocks(i + 1, block, reverse=False)
        for h in range(8):
            hc = slice(h * BLK, (h + 1) * BLK)
            dq_ref[:, hc] = _rope_t(acc_scr[h] * MLA_SCALE, c_ref[...], s1_ref[...], s2_ref[...]).astype(dq_ref.dtype)

    tile = lambda col: pl.BlockSpec((BLK, 512), lambda b, i: (b * nb + i, col))
    wide = pl.BlockSpec((BLK, 1024), lambda b, i: (b * nb + i, 0))
    full8 = pl.BlockSpec((lp, 1024), lambda b, i: (b, 0))
    full1 = pl.BlockSpec((lp, BLK), lambda b, i: (b, 0))
    tab = pl.BlockSpec((BLK, BLK), lambda b, i: (i, 0))
    return pl.pallas_call(
        body, name="mla_bwd", grid=(nbatch, nb),
        in_specs=[wide, full8, full1, tile(EV_GMLA_BLK // 4), tab, tab, tab, tile(0), tile(1),
                  pl.BlockSpec((BLK, BLK), lambda b, i: (b * nb + i, 0))],
        out_specs=[wide, full8, full1, tile(0)],
        out_shape=[jax.ShapeDtypeStruct((t, 1024), BF16), jax.ShapeDtypeStruct((t, 1024), F32),
                   jax.ShapeDtypeStruct((t, 128), F32), jax.ShapeDtypeStruct((t, 512), BF16)],
        scratch_shapes=[pltpu.VMEM((BLK, 1024), BF16), pltpu.VMEM((BLK, 1024), BF16),
                        pltpu.VMEM((16, BLK, BLK), F32), pltpu.VMEM((8, BLK, BLK), F32)],
        compiler_params=_params(("parallel", "arbitrary"), 56),
    )(qh, kvh, krot, proj, *tabs, o, dog, lse)


def _swa_setup(kk, i, k_refs, v_refs):
    row, lane = _iotas()
    own = (lane >= kk * HALF) & (lane < (kk + 1) * HALF)

    def dup(ref):
        x = ref[...].astype(F32)
        return jnp.where(own, x, pltpu.roll(x, HALF, 1)).astype(BF16)

    kcat = jnp.concatenate([dup(r) for r in k_refs], axis=0)
    vcat = jnp.concatenate([dup(r) for r in v_refs], axis=0)
    row2 = lax.broadcasted_iota(jnp.int32, (BLK, 2 * BLK), 0)
    lane2 = lax.broadcasted_iota(jnp.int32, (BLK, 2 * BLK), 1)
    is_meta = lane2 < BLK
    in_own = lane2 - BLK <= row2
    k_pos = jnp.where(is_meta, lane2, jnp.where(in_own, (i - 1) * BLK, (i - 2) * BLK) + lane2)
    d = i * BLK + row2 - k_pos
    mask = (d >= 0) & (k_pos >= jnp.where(is_meta, N_PAD, BLK))
    return lane, lane <= row, own, kcat, vcat, mask, d.astype(F32)


def _swa_fold(x, in_own):
    return jnp.concatenate([x[:, :BLK], jnp.where(in_own, x[:, 2 * BLK:], x[:, BLK:2 * BLK])], axis=1)


def _swa_unfold(x, in_own):
    w = x[:, BLK:]
    zero = jnp.zeros_like(w)
    return jnp.concatenate([x[:, :BLK], jnp.where(in_own, zero, w), jnp.where(in_own, w, zero)], axis=1)


def _swa_slope(kk, g_idx):
    return (2.0 ** (-(g_idx + 1) / 2.0)) * jnp.where(kk == 0, 1.0, 1.0 / 16.0)


def _swa_fwd(proj, sinks, nbatch, nb):
    lp = nb * BLK
    t = nbatch * lp

    def body(sink_ref, q_ref, ka, kb, kc, va, vb, vc, g_ref, o_ref, og_ref, lse_ref):
        kk = pl.program_id(1)
        i = pl.program_id(2)
        lane, in_own, own, kcat, vcat, mask, dist = _swa_setup(kk, i, (ka, kb, kc), (va, vb, vc))
        lo_m = lane < HALF
        heads = range(8)
        qms = []
        for h in heads:
            qp = q_ref[:, (h // 2) * BLK:(h // 2 + 1) * BLK]
            qms.append(jnp.where((lane >= HALF) if h % 2 else lo_m, qp, jnp.zeros_like(qp)) * SWA_SCALE)
        qks = [_dot_nt(qms[h], kcat) for h in heads]
        ps, ls, lses = [], [], []
        for h in heads:
            sink = sink_ref[kk, h]
            s = jnp.where(mask, _swa_fold(qks[h], in_own) - _swa_slope(kk, h) * dist, NEG)
            mx = jnp.maximum(jnp.max(s, axis=1, keepdims=True), sink)
            p = jnp.exp(s - mx)
            l = jnp.exp(sink - mx) + jnp.sum(p, axis=1, keepdims=True)
            ps.append(_swa_unfold(p.astype(BF16), in_own))
            ls.append(l)
            lses.append(mx + jnp.log(l))
        pvs = [_dot(ps[h], vcat) for h in heads]
        lse_out = jnp.zeros((BLK, BLK), F32)
        for m in range(4):
            cols = slice(m * BLK, (m + 1) * BLK)
            outp = jnp.where(lo_m, pvs[2 * m] / ls[2 * m], pvs[2 * m + 1] / ls[2 * m + 1])
            o_ref[:, cols] = outp
            g = g_ref[:, cols].astype(F32)
            og_ref[:, cols] = (outp * g * _sigmoid(g)).astype(og_ref.dtype)
            for h in (2 * m, 2 * m + 1):
                lse_out = lse_out + jnp.where(lane == h, lses[h], 0.0)
        lse_ref[...] = lse_out

    def kvspec(col, which):
        if which == 0:
            return pl.BlockSpec((BLK, BLK), lambda b, kk, i: (b * nb, col))
        if which == 1:
            return pl.BlockSpec((BLK, BLK), lambda b, kk, i: (b * nb + jnp.maximum(i - 1, 0), col))
        return pl.BlockSpec((BLK, BLK), lambda b, kk, i: (b * nb + i, col))

    wide = lambda c0: pl.BlockSpec((BLK, 512), lambda b, kk, i: (b * nb + i, c0 + kk))
    return pl.pallas_call(
        body, name="swa_fwd", grid=(nbatch, 2, nb),
        in_specs=[pl.BlockSpec(memory_space=pltpu.SMEM), wide(0),
                  kvspec(OD_K_BLK, 0), kvspec(OD_K_BLK, 1), kvspec(OD_K_BLK, 2),
                  kvspec(OD_V_BLK, 0), kvspec(OD_V_BLK, 1), kvspec(OD_V_BLK, 2), wide(2)],
        out_specs=[wide(0), wide(0), pl.BlockSpec((BLK, BLK), lambda b, kk, i: (b * nb + i, kk))],
        out_shape=[jax.ShapeDtypeStruct((t, 1024), F32), jax.ShapeDtypeStruct((t, 1024), BF16),
                   jax.ShapeDtypeStruct((t, 256), F32)],
        compiler_params=_params(("parallel", "parallel", "arbitrary")),
    )(sinks, proj, proj, proj, proj, proj, proj, proj, proj)


def _swa_bwd(proj, sinks, o, dog, lse, nbatch, nb):
    lp = nb * BLK
    t = nbatch * lp

    def body(sink_ref, q_ref, ka, kb, kc, va, vb, vc, g_ref, o_ref, dog_ref, lse_ref,
             dq_ref, dg_ref, dk_ref, dv_ref, dsink_ref):
        kk = pl.program_id(1)
        i = pl.program_id(2)

        @pl.when((kk == 0) & (i == 0))
        def _():
            dk_ref[...] = jnp.zeros_like(dk_ref)
            dv_ref[...] = jnp.zeros_like(dv_ref)

        @pl.when(i == 0)
        def _():
            dsink_ref[...] = jnp.zeros_like(dsink_ref)

        lane, in_own, own, kcat, vcat, mask, dist = _swa_setup(kk, i, (ka, kb, kc), (va, vb, vc))
        lo_m = lane < HALF
        row8 = lax.broadcasted_iota(jnp.int32, (8, BLK), 0)
        lse_blk = lse_ref[...]
        heads = range(8)
        qms, doms, deltas, lse_hs = [], [], [], []
        for m in range(4):
            cols = slice(m * BLK, (m + 1) * BLK)
            qp = q_ref[:, cols]
            g = g_ref[:, cols].astype(F32)
            sig = _sigmoid(g)
            dog_v = dog_ref[:, cols]
            o_v = o_ref[:, cols]
            dg_ref[:, cols] = (dog_v * o_v * (sig * (1.0 + g * (1.0 - sig)))).astype(dg_ref.dtype)
            do = dog_v * g * sig
            do_o = do * o_v
            dob = do.astype(BF16)
            for h in (2 * m, 2 * m + 1):
                hm = (lane >= HALF) if h % 2 else lo_m
                qms.append(jnp.where(hm, qp, jnp.zeros_like(qp)) * SWA_SCALE)
                doms.append(jnp.where(hm, dob, jnp.zeros_like(dob)))
                deltas.append(jnp.sum(jnp.where(hm, do_o, 0.0), axis=1, keepdims=True))
                lse_hs.append(jnp.sum(jnp.where(lane == h, lse_blk, 0.0), axis=1, keepdims=True))
        qks = [_dot_nt(qms[h], kcat) for h in heads]
        dps = [_dot_nt(doms[h], vcat) for h in heads]
        pbs, dss = [], []
        dsink = jnp.zeros((8, BLK), F32)
        for h in heads:
            s = jnp.where(mask, _swa_fold(qks[h], in_own) - _swa_slope(kk, h) * dist, NEG)
            p = jnp.exp(s - lse_hs[h])
            pbs.append(_swa_unfold(p.astype(BF16), in_own))
            dss.append(_swa_unfold((p * (_swa_fold(dps[h], in_own) - deltas[h])).astype(BF16), in_own))
            tot = jnp.sum(-jnp.exp(sink_ref[kk, h] - lse_hs[h]) * deltas[h], axis=0, keepdims=True)
            dsink = dsink + jnp.where(row8 == h, tot, 0.0)
        dsink_ref[...] += dsink
        dqs = [_dot(dss[h], kcat) for h in heads]
        dks = [_dot_tn(dss[h], qms[h]) for h in heads]
        dvs = [_dot_tn(pbs[h], doms[h]) for h in heads]
        for m in range(4):
            dq_ref[:, m * BLK:(m + 1) * BLK] = (
                jnp.where(lo_m, dqs[2 * m], dqs[2 * m + 1]) * SWA_SCALE).astype(dq_ref.dtype)
        dk = dks[0]
        dv = dvs[0]
        for h in range(1, 8):
            dk = dk + dks[h]
            dv = dv + dvs[h]
        offs = [0, pl.multiple_of(jnp.maximum(i - 1, 0) * BLK, BLK), pl.multiple_of(i * BLK, BLK)]
        for x in range(3):
            rows = slice(x * BLK, (x + 1) * BLK)
            dkx, dvx = dk[rows], dv[rows]
            dk_ref[pl.ds(offs[x], BLK), :] += jnp.where(own, dkx + pltpu.roll(dkx, HALF, 1), 0.0)
            dv_ref[pl.ds(offs[x], BLK), :] += jnp.where(own, dvx + pltpu.roll(dvx, HALF, 1), 0.0)

    def kvspec(col, which):
        if which == 0:
            return pl.BlockSpec((BLK, BLK), lambda b, kk, i: (b * nb, col))
        if which == 1:
            return pl.BlockSpec((BLK, BLK), lambda b, kk, i: (b * nb + jnp.maximum(i - 1, 0), col))
        return pl.BlockSpec((BLK, BLK), lambda b, kk, i: (b * nb + i, col))

    wide = lambda c0: pl.BlockSpec((BLK, 512), lambda b, kk, i: (b * nb + i, c0 + kk))
    full = pl.BlockSpec((lp, BLK), lambda b, kk, i: (b, 0))
    return pl.pallas_call(
        body, name="swa_bwd", grid=(nbatch, 2, nb),
        in_specs=[pl.BlockSpec(memory_space=pltpu.SMEM), wide(0),
                  kvspec(OD_K_BLK, 0), kvspec(OD_K_BLK, 1), kvspec(OD_K_BLK, 2),
                  kvspec(OD_V_BLK, 0), kvspec(OD_V_BLK, 1), kvspec(OD_V_BLK, 2), wide(2),
                  wide(0), wide(0), pl.BlockSpec((BLK, BLK), lambda b, kk, i: (b * nb + i, kk))],
        out_specs=[wide(0), wide(0), full, full,
                   pl.BlockSpec((8, BLK), lambda b, kk, i: (b * 2 + kk, 0))],
        out_shape=[jax.ShapeDtypeStruct((t, 1024), BF16), jax.ShapeDtypeStruct((t, 1024), BF16),
                   jax.ShapeDtypeStruct((t, 128), F32), jax.ShapeDtypeStruct((t, 128), F32),
                   jax.ShapeDtypeStruct((nbatch * 16, BLK), F32)],
        compiler_params=_params(("parallel", "arbitrary", "arbitrary")),
    )(sinks, proj, proj, proj, proj, proj, proj, proj, proj, o, dog, lse)


def _rope_tables(lp):
    pos = (jnp.arange(lp) - N_PAD).astype(F32)
    inv = ROPE_BASE ** (-jnp.arange(16, dtype=F32) / 16.0)
    ang = pos[:, None] * inv[None, :]
    cos, sin = jnp.cos(ang), jnp.sin(ang)
    z16 = jnp.zeros((lp, 16), F32)
    c = jnp.concatenate([jnp.ones((lp, 64), F32), cos, cos, jnp.zeros((lp, 32), F32)], axis=1)
    s1 = jnp.concatenate([jnp.zeros((lp, 64), F32), -sin, z16, jnp.zeros((lp, 32), F32)], axis=1)
    s2 = jnp.concatenate([jnp.zeros((lp, 64), F32), z16, sin, jnp.zeros((lp, 32), F32)], axis=1)
    return c, s1, s2


def _local_step(h0, tgt, norm_g, final_g, gq, gkv, sinks, w_ie, late_shards, nbatch, nb):
    lp = nb * BLK
    tabs = _rope_tables(lp)
    g0, g1 = norm_g[0:1], norm_g[1:2]
    sinks2 = sinks.reshape(2, 8)

    proj_e, hn0 = _norm_mm(h0, g0, w_ie, "proj_even")
    o_sb, og_sb, cm, gathered = _sb_fwd(proj_e, [late_shards[name] for name in _LATE], nbatch, nb)
    full = {name: _unchunk(name, blk) for name, blk in zip(_LATE, gathered)}
    w_uq, w_ukv, w_oe = _uq_to_compute(full["ev_w_uq"]), full["ev_w_ukv"], full["ev_w_out"]
    w_io, w_oo = _od_in_to_compute(full["od_w_in"]), full["od_w_out"]
    cqn, ckvn, krot = _mla_prep(proj_e, gq, gkv, tabs, nbatch, nb)
    qh = _mm(cqn, w_uq, "nn", "mla_uq", out_dtype=BF16)
    kvh = _mm(ckvn, w_ukv, "nn", "mla_ukv", out_dtype=BF16)
    o_mla, og_mla, lse_m = _mla_fwd(qh, kvh, krot, proj_e, tabs, nbatch, nb)
    h1 = _mm([og_sb, og_mla], w_oe, "nn", "out_even", add=h0)
    proj_o, hn1 = _norm_mm(h1, g1, w_io, "proj_odd")
    o_o, og_o, lse_o = _swa_fwd(proj_o, sinks2, nbatch, nb)
    h2 = _mm(og_o, w_oo, "nn", "out_odd", add=h1)
    dh2, lossv, d_final_g = _final(h2, tgt, final_g, nbatch, nb)

    dog_o = _mm(dh2, w_oo, "nt", "d_out_odd")
    d_w_oo, = _mm_tn(og_o, [dh2], "dw_out_odd")
    dq_o, dg_o, dk_o, dv_o, dsink = _swa_bwd(proj_o, sinks2, o_o, dog_o, lse_o, nbatch, nb)
    dproj_o = [dq_o, dg_o, dk_o, dv_o]
    dh1, d_g1 = _mm_norm_bwd(dproj_o, w_io, h1, g1, dh2, "d_proj_odd")
    dw_q, dw_g, dw_k, dw_v = _mm_tn(hn1, dproj_o, "dw_proj_odd")

    dog_e = _mm(dh1, w_oe, "nt", "d_out_even")
    d_w_oe_sb, = _mm_tn(og_sb, [dh1], "dw_out_even_sb")
    d_w_oe_mla, = _mm_tn(og_mla, [dh1], "dw_out_even_mla")
    early = dict(od_w_in=jnp.concatenate([dw_q, dw_k, dw_v, dw_g], axis=1), od_w_out=d_w_oo,
                 ev_w_out=jnp.concatenate([d_w_oe_sb, d_w_oe_mla], axis=0))
    dq_sb, dk_sb, dv_sb, dg_sb, received = _sb_bwd(
        proj_e, o_sb, dog_e, cm, [_chunk(name, early[name]) for name in _EARLY_GRADS], nbatch, nb)
    dqh, dkvh, dkrot, dg_mla = _mla_bwd(qh, kvh, krot, proj_e, tabs, o_mla, dog_e, lse_m, nbatch, nb)
    dcqn = _mm(dqh, w_uq, "nt", "d_mla_uq")
    d_w_uq, = _mm_tn(cqn, [dqh], "dw_mla_uq")
    dckvn = _mm(dkvh, w_ukv, "nt", "d_mla_ukv")
    d_w_ukv, = _mm_tn(ckvn, [dkvh], "dw_mla_ukv")
    dmid, d_gq, d_gkv = _mla_prep_bwd(proj_e, gq, gkv, tabs, dcqn, dckvn, dkrot, nbatch, nb)
    dproj_e = [dq_sb, dk_sb, dv_sb, dg_sb, dmid, dg_mla]
    dh0, d_g0 = _mm_norm_bwd(dproj_e, w_ie, h0, g0, dh1, "d_proj_even")
    dw_e = _mm_tn(hn0, dproj_e, "dw_proj_even")

    d_sinks = dsink.reshape(nbatch, 2, 8, BLK)[:, :, :, 0].sum(axis=0).reshape(1, 16)
    d_norm_g = jnp.concatenate([d_g0, d_g1], axis=0)
    d_ev_w_in = jnp.concatenate(list(dw_e[:4]) + [dw_e[4][:, :384], dw_e[4][:, 448:480], dw_e[5]], axis=1)
    return dict(lossv=lossv, dh0=dh0, norm_g=d_norm_g, final_g=d_final_g, gq=d_gq, gkv=d_gkv, sinks=d_sinks,
                ev_w_in=d_ev_w_in, ev_w_uq=_uq_from_compute(d_w_uq), ev_w_ukv=d_w_ukv, received=received)


def _ev_in_to_compute(w):
    z = lambda n: jnp.zeros((w.shape[0], n), w.dtype)
    return jnp.concatenate([w[:, :2432], z(64), w[:, 2432:2464], z(32), w[:, 2464:]], axis=1)


def _uq_to_compute(w):
    w3 = w.reshape(256, 8, 96)
    return jnp.concatenate([w3, jnp.zeros((256, 8, 32), w.dtype)], axis=2).reshape(256, 1024)


def _uq_from_compute(w):
    return w.reshape(256, 8, 128)[:, :, :96].reshape(256, 768)


def _od_in_to_compute(w):
    return jnp.concatenate([w[:, :1024], w[:, 1280:], w[:, 1024:1280]], axis=1)


_BIG = dict(ev_w_in=(1024, 2976, 1), ev_w_uq=(256, 768, 1), ev_w_ukv=(128, 1024, 1),
            ev_w_out=(1024, 1024, 0), od_w_in=(1024, 2304, 1), od_w_out=(1024, 1024, 0))
_LATE = ("ev_w_uq", "ev_w_ukv", "ev_w_out", "od_w_in", "od_w_out")
_EARLY_GRADS = ("od_w_in", "od_w_out", "ev_w_out")
_LAST_GRADS = ("ev_w_in", "ev_w_uq", "ev_w_ukv")


def _unchunk(name, blk):
    rows, cols, axis = _BIG[name]
    return blk.transpose(1, 0, 2).reshape(rows, cols) if axis == 1 else blk.reshape(rows, cols)


def _chunk(name, g):
    rows, cols, axis = _BIG[name]
    g = g.astype(BF16)
    return g.reshape(rows, N_DEV, cols // N_DEV).transpose(1, 0, 2) if axis == 1 else g.reshape(N_DEV, rows // N_DEV, cols)


def _all_gather(shards, name):
    n = len(shards)

    def body(*refs):
        xs, outs = refs[:n], refs[n:2 * n]
        send_sems, recv_sems, local_sems = refs[2 * n:]
        x, y, c = lax.axis_index("x"), lax.axis_index("y"), lax.axis_index("c")
        me, sibling = (x, y, c), (x, y, 1 - c)
        chips = [(1 - x, y), (x, 1 - y), (1 - x, 1 - y)]
        arrays = range(n)

        def copy(k, a, block, to, from_input=False):
            px, py, pc = block
            dst = outs[a].at[4 * px + 2 * py + pc]
            return pltpu.make_async_remote_copy(
                src_ref=xs[a] if from_input else dst, dst_ref=dst,
                send_sem=send_sems.at[k, a], recv_sem=recv_sems.at[k, a],
                device_id=to, device_id_type=pl.DeviceIdType.MESH)

        mine = [pltpu.make_async_copy(xs[a], outs[a].at[4 * x + 2 * y + c], local_sems.at[a]) for a in arrays]
        for cp in mine:
            cp.start()
        first = [copy(0, a, me, sibling, True) for a in arrays]
        for j, chip in enumerate(chips):
            first += [copy(1 + j, a, me, (*chip, c), True) for a in arrays]
        for cp in first:
            cp.start()
        passed = []
        for j, chip in enumerate(chips):
            for a in arrays:
                copy(1 + j, a, (*chip, c), me).wait_recv()
                passed.append(copy(4 + j, a, (*chip, c), sibling))
                passed[-1].start()
        for a in arrays:
            copy(0, a, sibling, me).wait_recv()
        for j, chip in enumerate(chips):
            for a in arrays:
                copy(4 + j, a, (*chip, 1 - c), me).wait_recv()
        for cp in first + passed:
            cp.wait_send()
        for cp in mine:
            cp.wait()

    hbm = pl.BlockSpec(memory_space=pl.ANY)
    return pl.pallas_call(
        body, name=name,
        out_shape=[jax.ShapeDtypeStruct((N_DEV,) + s.shape, s.dtype) for s in shards],
        in_specs=[hbm] * n, out_specs=[hbm] * n,
        scratch_shapes=[pltpu.SemaphoreType.DMA((7, n)), pltpu.SemaphoreType.DMA((7, n)),
                        pltpu.SemaphoreType.DMA((n,))],
    )(*shards)


def _exchange_sum(chunked, received, name):
    n, m = len(chunked), len(received)
    arrs = list(chunked) + list(received)

    def body(*refs):
        ins, outs = refs[:n + m], refs[n + m:2 * (n + m)]
        bufs = refs[2 * (n + m):3 * (n + m)]
        send_sems, recv_sems, local_sems, load_sems = refs[3 * (n + m):]
        loads = [pltpu.make_async_copy(ins[n + a], bufs[n + a], load_sems.at[a]) for a in range(m)]
        for cp in loads:
            cp.start()
        local, sends, recvs = _direct_exchange(ins[:n], bufs[:n], send_sems, recv_sems, local_sems, chunked=True)
        for cp in local + sends:
            cp.start()
        for a, cp in enumerate(loads):
            cp.wait()
            _sum_slots(bufs[n + a], outs[n + a])
        for cp in local:
            cp.wait()
        for cp in recvs:
            cp.wait_recv()
        for cp in sends:
            cp.wait_send()
        for a in range(n):
            _sum_slots(bufs[a], outs[a])

    hbm = pl.BlockSpec(memory_space=pl.ANY)
    vm = pl.BlockSpec(memory_space=pltpu.VMEM)
    return pl.pallas_call(
        body, name=name,
        out_shape=[jax.ShapeDtypeStruct(a.shape[1:], F32) for a in arrs],
        in_specs=[hbm] * (n + m), out_specs=[vm] * (n + m),
        scratch_shapes=[pltpu.VMEM(a.shape, a.dtype) for a in arrs] + _exchange_sems(n)
        + [pltpu.SemaphoreType.DMA((max(m, 1),))],
        compiler_params=pltpu.CompilerParams(vmem_limit_bytes=48 << 20),
    )(*arrs)


def _sum_slots(buf, out):
    rows = buf.shape[1]

    def add(sl):
        acc = buf[(0,) + sl].astype(F32)
        for k in range(1, N_DEV):
            acc = acc + buf[(k,) + sl].astype(F32)
        out[sl] = acc

    if rows > BLK and rows % BLK == 0:
        def step(r, carry):
            add((pl.ds(pl.multiple_of(r * BLK, BLK), BLK), slice(None)))
            return carry

        lax.fori_loop(0, rows // BLK, step, 0)
    else:
        add((slice(None), slice(None)))


def _adamw(ws, gs, ms, vs):
    n = len(ws)

    def body(*refs):
        ins, outs = refs[:4 * n], refs[4 * n:]
        for k in range(n):
            w_ref, g_ref, m_ref, v_ref = ins[4 * k:4 * k + 4]
            d_ref, nm_ref, nv_ref = outs[3 * k:3 * k + 3]

            def update(sl, w_ref=w_ref, g_ref=g_ref, m_ref=m_ref, v_ref=v_ref,
                       d_ref=d_ref, nm_ref=nm_ref, nv_ref=nv_ref):
                g = g_ref[sl]
                m = ADAM_B1 * m_ref[sl] + (1.0 - ADAM_B1) * g
                v = ADAM_B2 * v_ref[sl] + (1.0 - ADAM_B2) * (g * g)
                m_hat = m / (1.0 - ADAM_B1 ** ADAM_STEP)
                v_hat = v / (1.0 - ADAM_B2 ** ADAM_STEP)
                d_ref[sl] = -ADAM_LR * (m_hat / (jnp.sqrt(v_hat) + ADAM_EPS) + ADAM_WD * w_ref[sl])
                nm_ref[sl] = m
                nv_ref[sl] = v

            rows = w_ref.shape[0]
            if rows > BLK and rows % BLK == 0:
                def step(r, carry, update=update):
                    update((pl.ds(pl.multiple_of(r * BLK, BLK), BLK), slice(None)))
                    return carry

                lax.fori_loop(0, rows // BLK, step, 0)
            else:
                update((slice(None), slice(None)))

    args, out_shape = [], []
    for k in range(n):
        args += [ws[k], gs[k], ms[k], vs[k]]
        out_shape += [jax.ShapeDtypeStruct(ws[k].shape, F32)] * 3
    vm = pl.BlockSpec(memory_space=pltpu.VMEM)
    outs = pl.pallas_call(
        body, name="adamw", out_shape=out_shape,
        in_specs=[vm] * (4 * n), out_specs=[vm] * (3 * n),
        compiler_params=pltpu.CompilerParams(vmem_limit_bytes=48 << 20),
    )(*args)
    return [tuple(outs[3 * k:3 * k + 3]) for k in range(n)]


def kernel(x, meta, norm_g, final_g, ev_w_in, ev_q_norm_g, ev_kv_norm_g, ev_w_uq, ev_w_ukv, ev_w_out, od_w_in, od_sinks, od_w_out, loss_target, m_meta, m_norm_g, m_final_g, m_ev_w_in, m_ev_q_norm_g, m_ev_kv_norm_g, m_ev_w_uq, m_ev_w_ukv, m_ev_w_out, m_od_w_in, m_od_sinks, m_od_w_out, v_meta, v_norm_g, v_final_g, v_ev_w_in, v_ev_q_norm_g, v_ev_kv_norm_g, v_ev_w_uq, v_ev_w_ukv, v_ev_w_out, v_od_w_in, v_od_sinks, v_od_w_out):
    nbatch, seq, d = x.shape
    nb = seq // BLK + 1
    lp = nb * BLK
    shards = dict(ev_w_in=ev_w_in[0], ev_w_uq=ev_w_uq[0], ev_w_ukv=ev_w_ukv[0], ev_w_out=ev_w_out[0],
                  od_w_in=od_w_in[0], od_w_out=od_w_out[0])

    w_ie_blocks, meta_blocks = _all_gather([shards["ev_w_in"].astype(BF16), meta], "gather_weights")
    meta_full = meta_blocks.transpose(1, 0, 2).reshape(N_META, d)

    head = jnp.concatenate([jnp.zeros((N_PAD, d), F32), meta_full], axis=0)
    h0 = jnp.concatenate([jnp.broadcast_to(head[None], (nbatch, BLK, d)), x], axis=1).reshape(nbatch * lp, d)
    grads = _local_step(
        h0, loss_target.reshape(nbatch * seq, d), norm_g, final_g.reshape(1, d), ev_q_norm_g, ev_kv_norm_g,
        od_sinks, _ev_in_to_compute(_unchunk("ev_w_in", w_ie_blocks)),
        {name: shards[name].astype(BF16) for name in _LATE}, nbatch, nb)
    dh0 = grads["dh0"].reshape(nbatch, lp, d)
    grad_x = dh0[:, BLK:]

    d_meta = dh0[:, N_PAD:BLK].sum(axis=0).reshape(N_META, N_DEV, BLK).transpose(1, 0, 2)
    pad = lambda a, n: jnp.concatenate([a.reshape(1, -1), jnp.zeros((1, n - a.size), F32)], axis=1)
    loss_part = (0.5 / d * jnp.sum(grads["lossv"])).reshape(1, 1)
    rep = jnp.concatenate([grads["norm_g"].reshape(1, -1), grads["final_g"], grads["gq"], pad(loss_part, 256),
                           pad(grads["gkv"], 256), pad(grads["sinks"], 256)], axis=1).reshape(32, BLK)
    small = jnp.concatenate([d_meta, jnp.broadcast_to(rep[None], (N_DEV, 32, BLK))], axis=1)
    reduced = _exchange_sum([_chunk(name, grads[name]) for name in _LAST_GRADS] + [small],
                            grads["received"], "reduce_grads")
    g_shard = dict(zip(_LAST_GRADS + ("small",) + _EARLY_GRADS, reduced))
    red_small = g_shard.pop("small")
    rep = red_small[N_META:].reshape(1, -1)
    loss = rep[0, 3 * d + 256]
    g_small = dict(meta=red_small[:N_META], norm_g=rep[:, :2 * d].reshape(2, d), final_g=rep[:, 2 * d:3 * d],
                   ev_q_norm_g=rep[:, 3 * d:3 * d + 256], ev_kv_norm_g=rep[:, 3 * d + 512:3 * d + 640],
                   od_sinks=rep[:, 3 * d + 768:3 * d + 784])

    names = ["meta", "norm_g", "final_g", "ev_w_in", "ev_q_norm_g", "ev_kv_norm_g", "ev_w_uq", "ev_w_ukv",
             "ev_w_out", "od_w_in", "od_sinks", "od_w_out"]
    given = dict(meta=(meta, m_meta, v_meta), norm_g=(norm_g, m_norm_g, v_norm_g),
                 final_g=(final_g, m_final_g, v_final_g), ev_w_in=(ev_w_in, m_ev_w_in, v_ev_w_in),
                 ev_q_norm_g=(ev_q_norm_g, m_ev_q_norm_g, v_ev_q_norm_g),
                 ev_kv_norm_g=(ev_kv_norm_g, m_ev_kv_norm_g, v_ev_kv_norm_g),
                 ev_w_uq=(ev_w_uq, m_ev_w_uq, v_ev_w_uq), ev_w_ukv=(ev_w_ukv, m_ev_w_ukv, v_ev_w_ukv),
                 ev_w_out=(ev_w_out, m_ev_w_out, v_ev_w_out), od_w_in=(od_w_in, m_od_w_in, v_od_w_in),
                 od_sinks=(od_sinks, m_od_sinks, v_od_sinks), od_w_out=(od_w_out, m_od_w_out, v_od_w_out))
    ws, gs, ms, vs = [], [], [], []
    for name in names:
        g2 = g_shard[name] if name in g_shard else g_small[name]
        w, m, v = given[name]
        ws.append(w.reshape(g2.shape))
        ms.append(m.reshape(g2.shape))
        vs.append(v.reshape(g2.shape))
        gs.append(g2)
    upd = _adamw(ws, gs, ms, vs)
    shape_of = {name: given[name][0].shape for name in names}
    grads_out = [gs[k].reshape(shape_of[n]) for k, n in enumerate(names)]
    deltas = [upd[k][0].reshape(shape_of[n]) for k, n in enumerate(names)]
    new_m = [upd[k][1].reshape(shape_of[n]) for k, n in enumerate(names)]
    new_v = [upd[k][2].reshape(shape_of[n]) for k, n in enumerate(names)]
    return (loss, grad_x, *grads_out, *deltas, *new_m, *new_v)
```

```python
import jax
import jax.numpy as jnp
from jax import lax
from jax.experimental import pallas as pl
from jax.experimental.pallas import tpu as pltpu

F32 = jnp.float32
BF16 = jnp.bfloat16

D_MODEL = 1024
N_META = 16
BLK = 128
HALF = 64
N_PAD = BLK - N_META
NORM_EPS = 1e-6
NEG = -1e30
N_DEV = 8

SB_SCALE = 64 ** -0.5
MLA_SCALE = 96 ** -0.5
SWA_SCALE = 64 ** -0.5
ROPE_BASE = 10000.0

EV_IN_PAD = 3072
EV_MID_BLK = 4
EV_GMLA_BLK = 20
OD_K_BLK = 16
OD_V_BLK = 17

ADAM_LR = 0.001
ADAM_B1 = 0.9
ADAM_B2 = 0.999
ADAM_EPS = 1e-08
ADAM_WD = 0.01
ADAM_STEP = 10


def _dot(a, b):
    return lax.dot_general(a, b, (((1,), (0,)), ((), ())), preferred_element_type=F32)


def _dot_nt(a, b):
    return lax.dot_general(a, b, (((1,), (1,)), ((), ())), preferred_element_type=F32)


def _dot_tn(a, b):
    return lax.dot_general(a, b, (((0,), (0,)), ((), ())), preferred_element_type=F32)


def _sigmoid(x):
    return 1.0 / (1.0 + jnp.exp(-x))


def _iotas():
    row = lax.broadcasted_iota(jnp.int32, (BLK, BLK), 0)
    lane = lax.broadcasted_iota(jnp.int32, (BLK, BLK), 1)
    return row, lane


WIDE = 2 * BLK
HEAD_GROUPS = (range(0, 8),)


def _key_mask(i, first_key, width, strict):
    t_pos = i * BLK + lax.broadcasted_iota(jnp.int32, (BLK, width), 0)
    s_pos = first_key + lax.broadcasted_iota(jnp.int32, (BLK, width), 1)
    seen = (s_pos < t_pos) if strict else (s_pos <= t_pos)
    return seen & (s_pos >= N_PAD)


def _widen(x, width):
    return x if width == BLK else jnp.concatenate([x] * (width // BLK), axis=1)


def _over_key_blocks(n, block, reverse):
    pairs = n // 2
    last = pl.multiple_of((n - 1) * BLK, BLK)

    def step(jj, carry):
        jp = (pairs - 1 - jj) if reverse else jj
        off = pl.multiple_of(jp * WIDE, WIDE)
        edge = (jp == 0) | (jp == pairs - 1)
        pl.when(edge)(lambda: block(off, WIDE, True))
        pl.when(jnp.logical_not(edge))(lambda: block(off, WIDE, False))
        return carry

    if reverse:
        pl.when(n % 2 == 1)(lambda: block(last, BLK, True))
        lax.fori_loop(0, pairs, step, 0)
    else:
        lax.fori_loop(0, pairs, step, 0)
        pl.when(n % 2 == 1)(lambda: block(last, BLK, True))


def _sel(mask, x, fill):
    return x if mask is None else jnp.where(mask, x, fill)


def _rope(x, c, s1, s2):
    return x * c + pltpu.roll(x, BLK - 16, 1) * s1 + pltpu.roll(x, 16, 1) * s2


def _rope_t(x, c, s1, s2):
    return x * c - pltpu.roll(x, BLK - 16, 1) * s1 - pltpu.roll(x, 16, 1) * s2


def _params(sem, vmem_mb=None):
    kw = dict(dimension_semantics=sem)
    if vmem_mb is not None:
        kw["vmem_limit_bytes"] = vmem_mb << 20
    return pltpu.CompilerParams(**kw)


def _row_tile(t, cands):
    for c in cands:
        if t % c == 0:
            return c
    raise ValueError(t)


def _mm(a, w, mode, name, add=None, out_dtype=F32):
    pieces = list(a) if isinstance(a, (list, tuple)) else [a]
    m = pieces[0].shape[0]
    n = w.shape[1] if mode == "nn" else w.shape[0]
    tm = _row_tile(m, (544, 256, 128) if n <= 1024 else (256, 128))
    widths = [p.shape[1] for p in pieces]
    offs = [sum(widths[:i]) for i in range(len(widths))]

    def body(*refs):
        p_refs = refs[:len(pieces)]
        w_ref = refs[len(pieces)]
        o_ref = refs[-1]
        acc = None
        for p_ref, off, wd in zip(p_refs, offs, widths):
            x = p_ref[...].astype(BF16)
            part = _dot(x, w_ref[off:off + wd, :]) if mode == "nn" else _dot_nt(x, w_ref[:, off:off + wd])
            acc = part if acc is None else acc + part
        if add is not None:
            acc = acc + refs[len(pieces) + 1][...]
        o_ref[...] = acc.astype(o_ref.dtype)

    in_specs = [pl.BlockSpec((tm, wd), lambda i: (i, 0)) for wd in widths]
    in_specs.append(pl.BlockSpec(w.shape, lambda i: (0, 0)))
    args = pieces + [w]
    if add is not None:
        in_specs.append(pl.BlockSpec((tm, n), lambda i: (i, 0)))
        args.append(add)
    return pl.pallas_call(
        body, name=name, grid=(m // tm,), in_specs=in_specs,
        out_specs=pl.BlockSpec((tm, n), lambda i: (i, 0)),
        out_shape=jax.ShapeDtypeStruct((m, n), out_dtype),
        compiler_params=_params(("parallel",), 48),
    )(*args)


def _mm_tn(x, pieces, name):
    t, k = x.shape
    tt = _row_tile(t, (544, 256, 128))
    widths = [p.shape[1] for p in pieces]

    def body(*refs):
        x_ref = refs[0]
        d_refs = refs[1:1 + len(pieces)]
        o_refs = refs[1 + len(pieces):]

        @pl.when(pl.program_id(0) == 0)
        def _():
            for o_ref in o_refs:
                o_ref[...] = jnp.zeros_like(o_ref)

        xb = x_ref[...].astype(BF16)
        for d_ref, o_ref in zip(d_refs, o_refs):
            o_ref[...] += _dot_tn(xb, d_ref[...].astype(BF16))

    return pl.pallas_call(
        body, name=name, grid=(t // tt,),
        in_specs=[pl.BlockSpec((tt, k), lambda i: (i, 0))] + [pl.BlockSpec((tt, wd), lambda i: (i, 0)) for wd in widths],
        out_specs=[pl.BlockSpec((k, wd), lambda i: (0, 0)) for wd in widths],
        out_shape=[jax.ShapeDtypeStruct((k, wd), F32) for wd in widths],
        compiler_params=_params(("arbitrary",), 56),
    )(x, *pieces)


def _norm_mm(h, g, w, name):
    t, d = h.shape
    n = w.shape[1]
    tr = _row_tile(t, (544, 256, 128))

    def body(h_ref, g_ref, w_ref, o_ref, hn_ref):
        x = h_ref[...]
        r = lax.rsqrt(jnp.mean(x * x, axis=1, keepdims=True) + NORM_EPS)
        hn = (x * r * g_ref[...]).astype(BF16)
        hn_ref[...] = hn
        o_ref[...] = _dot(hn, w_ref[...]).astype(o_ref.dtype)

    row = lambda width: pl.BlockSpec((tr, width), lambda i: (i, 0))
    return pl.pallas_call(
        body, name=name, grid=(t // tr,),
        in_specs=[row(d), pl.BlockSpec((1, d), lambda i: (0, 0)), pl.BlockSpec(w.shape, lambda i: (0, 0))],
        out_specs=[row(n), row(d)],
        out_shape=[jax.ShapeDtypeStruct((t, n), BF16), jax.ShapeDtypeStruct((t, d), BF16)],
        compiler_params=_params(("parallel",), 48),
    )(h, g, w)


def _mm_norm_bwd(pieces, w, h, g, dres, name):
    t, d = h.shape
    tr = _row_tile(t, (544, 256, 128))
    widths = [p.shape[1] for p in pieces]
    offs = [sum(widths[:i]) for i in range(len(widths))]

    def body(*refs):
        p_refs = refs[:len(pieces)]
        w_ref, h_ref, g_ref, dres_ref, dh_ref, dg_ref = refs[len(pieces):]

        @pl.when(pl.program_id(0) == 0)
        def _():
            dg_ref[...] = jnp.zeros_like(dg_ref)

        dy = None
        for p_ref, off, wd in zip(p_refs, offs, widths):
            part = _dot_nt(p_ref[...].astype(BF16), w_ref[:, off:off + wd])
            dy = part if dy is None else dy + part
        x = h_ref[...]
        r = lax.rsqrt(jnp.mean(x * x, axis=1, keepdims=True) + NORM_EPS)
        nx = x * r
        dn = dy * g_ref[...]
        dh_ref[...] = dres_ref[...] + r * (dn - nx * jnp.mean(dn * nx, axis=1, keepdims=True))
        dg_ref[...] += jnp.sum(dy * nx, axis=0, keepdims=True)

    row = lambda width: pl.BlockSpec((tr, width), lambda i: (i, 0))
    vec = pl.BlockSpec((1, d), lambda i: (0, 0))
    return pl.pallas_call(
        body, name=name, grid=(t // tr,),
        in_specs=[row(wd) for wd in widths] + [pl.BlockSpec(w.shape, lambda i: (0, 0)), row(d), vec, row(d)],
        out_specs=[row(d), vec],
        out_shape=[jax.ShapeDtypeStruct((t, d), F32), jax.ShapeDtypeStruct((1, d), F32)],
        compiler_params=_params(("arbitrary",), 56),
    )(*pieces, w, h, g, dres)


def _final(h2, tgt, g, nbatch, nb):
    t, d = h2.shape

    def body(h_ref, t_ref, g_ref, dh_ref, loss_ref, dg_ref):
        b = pl.program_id(0)
        i = pl.program_id(1)

        @pl.when((b == 0) & (i == 0))
        def _():
            loss_ref[...] = jnp.zeros_like(loss_ref)
            dg_ref[...] = jnp.zeros_like(dg_ref)

        x = h_ref[...]
        r = lax.rsqrt(jnp.mean(x * x, axis=1, keepdims=True) + NORM_EPS)
        nx = x * r
        gg = g_ref[...]
        live = jnp.where(i >= 1, 1.0, 0.0)
        err = (nx * gg - t_ref[...]) * live
        loss_ref[...] += jnp.sum(err * err, axis=0, keepdims=True)
        dy = err * (1.0 / d)
        dn = dy * gg
        dh_ref[...] = r * (dn - nx * jnp.mean(dn * nx, axis=1, keepdims=True))
        dg_ref[...] += jnp.sum(dy * nx, axis=0, keepdims=True)

    vec = pl.BlockSpec((1, d), lambda b, i: (0, 0))
    return pl.pallas_call(
        body, name="final_loss", grid=(nbatch, nb),
        in_specs=[pl.BlockSpec((BLK, d), lambda b, i: (b * nb + i, 0)),
                  pl.BlockSpec((BLK, d), lambda b, i: (b * (nb - 1) + jnp.maximum(i - 1, 0), 0)),
                  vec],
        out_specs=[pl.BlockSpec((BLK, d), lambda b, i: (b * nb + i, 0)), vec, vec],
        out_shape=[jax.ShapeDtypeStruct((t, d), F32), jax.ShapeDtypeStruct((1, d), F32),
                   jax.ShapeDtypeStruct((1, d), F32)],
        compiler_params=_params(("arbitrary", "arbitrary")),
    )(h2, tgt, g)


def _direct_exchange(srcs, dsts, send_sems, recv_sems, local_sems, chunked):
    x, y, c = lax.axis_index("x"), lax.axis_index("y"), lax.axis_index("c")
    me = 4 * x + 2 * y + c
    arrays = range(len(srcs))
    local = [pltpu.make_async_copy(srcs[a].at[me] if chunked else srcs[a], dsts[a].at[me], local_sems.at[a])
             for a in arrays]
    sends, recvs = [], []
    for d in range(1, N_DEV):
        px = x + ((d >> 2) & 1) - 2 * x * ((d >> 2) & 1)
        py = y + ((d >> 1) & 1) - 2 * y * ((d >> 1) & 1)
        pc = c + (d & 1) - 2 * c * (d & 1)
        pid = 4 * px + 2 * py + pc
        for a in arrays:
            kw = dict(send_sem=send_sems.at[d - 1, a], recv_sem=recv_sems.at[d - 1, a],
                      device_id=(px, py, pc), device_id_type=pl.DeviceIdType.MESH)
            src = srcs[a].at[pid] if chunked else srcs[a]
            sends.append(pltpu.make_async_remote_copy(src_ref=src, dst_ref=dsts[a].at[me], **kw))
            recvs.append(pltpu.make_async_remote_copy(src_ref=src, dst_ref=dsts[a].at[pid], **kw))
    return local, sends, recvs


def _exchange_beside(first, last, srcs, dsts, sems, chunked):
    local, sends, recvs = _direct_exchange(srcs, dsts, *sems, chunked)

    @pl.when(first)
    def _():
        for cp in local + sends:
            cp.start()

    @pl.when(last)
    def _():
        for cp in local:
            cp.wait()
        for cp in recvs:
            cp.wait_recv()
        for cp in sends:
            cp.wait_send()


def _exchange_sems(n):
    return [pltpu.SemaphoreType.DMA((7, n)), pltpu.SemaphoreType.DMA((7, n)), pltpu.SemaphoreType.DMA((n,))]


def _sb_logits(z):
    log_beta = jnp.minimum(z, 0.0) - jnp.log(1.0 + jnp.exp(-jnp.abs(z)))
    return log_beta, log_beta - z


def _tri(width, after):
    j = lax.broadcasted_iota(jnp.int32, (width, width), 0)
    s = lax.broadcasted_iota(jnp.int32, (width, width), 1)
    return (j > s) if after else (j < s)


def _tri_ones(tri):
    return jnp.concatenate([tri.astype(BF16), jnp.ones((tri.shape[0], BLK), BF16)], axis=1)


def _block_sums(x, tri_ones, after):
    xb = x.astype(BF16)
    subs = [_dot(xb[:, s:s + BLK], tri_ones) for s in range(0, x.shape[1], BLK)]
    if len(subs) == 1:
        return subs[0][:, :BLK], subs[0][:, BLK:]
    first, second = subs
    total = first[:, BLK:] + second[:, BLK:]
    if after:
        return jnp.concatenate([first[:, :BLK] + second[:, BLK:], second[:, :BLK]], axis=1), total
    return jnp.concatenate([first[:, :BLK], second[:, :BLK] + first[:, BLK:]], axis=1), total


def _head_masked(x, lane, scale=None):
    out = []
    for h in range(8):
        xp = x[:, (h // 2) * BLK:(h // 2 + 1) * BLK]
        xm = jnp.where((lane >= HALF) if h % 2 else (lane < HALF), xp, jnp.zeros_like(xp))
        out.append(xm if scale is None else xm * scale)
    return jnp.concatenate(out, axis=1)


def _sb_fwd(proj, shards, nbatch, nb):
    lp = nb * BLK
    t = nbatch * lp

    ns = len(shards)

    def body(*refs):
        q_ref, k_ref, v_ref, g_ref = refs[:4]
        shard_refs = refs[4:4 + ns]
        o_ref, og_ref, cm_ref = refs[4 + ns:7 + ns]
        gathered_refs = refs[7 + ns:7 + 2 * ns]
        c_scr, qm_scr = refs[7 + 2 * ns:9 + 2 * ns]
        b = pl.program_id(0)
        i = pl.program_id(1)
        _exchange_beside((b == 0) & (i == 0), (b == nbatch - 1) & (i == nb - 1),
                         shard_refs, gathered_refs, refs[9 + 2 * ns:], chunked=False)
        _, lane = _iotas()
        lo_m = lane < HALF
        cm_ref[...] = jnp.zeros_like(cm_ref)
        c_scr[...] = jnp.zeros_like(c_scr)
        o_ref[...] = jnp.zeros_like(o_ref)
        qm_scr[...] = _head_masked(q_ref[...], lane, SB_SCALE)

        def block(off, width, edge):
            mask = _key_mask(i, off, width, strict=True) if edge else None
            upper = _tri_ones(_tri(BLK, after=True))
            onehot = lane == off // WIDE
            heads = range(8)
            hcs = [slice(h * BLK, (h + 1) * BLK) for h in heads]
            kbs = [k_ref[pl.ds(off, width), hc] for hc in hcs[:4]]
            vbs = [v_ref[pl.ds(off, width), hc] for hc in hcs[:4]]
            zs = [_dot_nt(qm_scr[:, hcs[h]], kbs[h // 2]) for h in heads]
            lbs, l1s = [], []
            for h in heads:
                log_beta, log_1m = _sb_logits(zs[h])
                lbs.append(log_beta)
                l1s.append(_sel(mask, log_1m, 0.0))
            css = [_block_sums(l1s[h], upper, after=True) for h in heads]
            avs = []
            for h in heads:
                c = c_scr[h]
                avs.append(_sel(mask, jnp.exp(lbs[h] + css[h][0] + _widen(c, width)), 0.0).astype(BF16))
                if width == WIDE:
                    cm_ref[:, hcs[h]] = jnp.where(onehot, c, cm_ref[:, hcs[h]])
                c_scr[h] = c + css[h][1]
            accs = [_dot(avs[h], vbs[h // 2]) for h in heads]
            for p in range(4):
                o_ref[:, hcs[p]] += jnp.where(lo_m, accs[2 * p], accs[2 * p + 1])

        _over_key_blocks(i + 1, block, reverse=True)
        g = g_ref[...].astype(F32)
        og_ref[...] = (o_ref[...] * g * _sigmoid(g)).astype(og_ref.dtype)

    tile = lambda col: pl.BlockSpec((BLK, 512), lambda b, i: (b * nb + i, col))
    full = lambda col: pl.BlockSpec((lp, 512), lambda b, i: (b, col))
    hbm = pl.BlockSpec(memory_space=pl.ANY)
    outs = pl.pallas_call(
        body, name="sb_fwd", grid=(nbatch, nb),
        in_specs=[tile(0), full(1), full(2), tile(3)] + [hbm] * ns,
        out_specs=[tile(0), tile(0), pl.BlockSpec((BLK, 1024), lambda b, i: (b * nb + i, 0))] + [hbm] * ns,
        out_shape=[jax.ShapeDtypeStruct((t, 512), F32), jax.ShapeDtypeStruct((t, 512), BF16),
                   jax.ShapeDtypeStruct((t, 1024), F32)]
        + [jax.ShapeDtypeStruct((N_DEV,) + s.shape, s.dtype) for s in shards],
        scratch_shapes=[pltpu.VMEM((8, BLK, BLK), F32), pltpu.VMEM((BLK, 1024), BF16)] + _exchange_sems(ns),
        compiler_params=_params(("arbitrary", "arbitrary"), 48),
    )(proj, proj, proj, proj, *shards)
    return outs[0], outs[1], outs[2], list(outs[3:])


def _sb_bwd(proj, o, dog, cm, chunked, nbatch, nb):
    lp = nb * BLK
    t = nbatch * lp
    ns = len(chunked)

    def body(*refs):
        q_ref, k_ref, v_ref, g_ref, o_ref, dog_ref, cm_ref = refs[:7]
        chunk_refs = refs[7:7 + ns]
        dq_ref, dk_ref, dv_ref, dg_ref = refs[7 + ns:11 + ns]
        received_refs = refs[11 + ns:11 + 2 * ns]
        c_scr, qm_scr, dom_scr, dq_scr = refs[11 + 2 * ns:15 + 2 * ns]
        b = pl.program_id(0)
        i = pl.program_id(1)
        _exchange_beside((b == 0) & (i == 0), (b == nbatch - 1) & (i == nb - 1),
                         chunk_refs, received_refs, refs[15 + 2 * ns:], chunked=True)

        @pl.when(i == 0)
        def _():
            dk_ref[...] = jnp.zeros_like(dk_ref)
            dv_ref[...] = jnp.zeros_like(dv_ref)

        _, lane = _iotas()
        lo_m = lane < HALF
        g = g_ref[...].astype(F32)
        sig = _sigmoid(g)
        dog_v = dog_ref[...]
        dg_ref[...] = (dog_v * o_ref[...] * (sig * (1.0 + g * (1.0 - sig)))).astype(dg_ref.dtype)
        dom_scr[...] = _head_masked((dog_v * g * sig).astype(BF16), lane)
        qm_scr[...] = _head_masked(q_ref[...], lane, SB_SCALE)
        c_scr[...] = jnp.zeros_like(c_scr)
        dq_scr[...] = jnp.zeros_like(dq_scr)

        def block(off, width, edge):
            mask = _key_mask(i, off, width, strict=True) if edge else None
            upper = _tri_ones(_tri(BLK, after=True))
            lower = _tri_ones(_tri(BLK, after=False))
            onehot = lane == off // WIDE
            hcs = [slice(h * BLK, (h + 1) * BLK) for h in range(8)]
            kbs = [k_ref[pl.ds(off, width), hc] for hc in hcs[:4]]
            vbs = [v_ref[pl.ds(off, width), hc] for hc in hcs[:4]]
            for heads in HEAD_GROUPS:
                zs = {h: _dot_nt(qm_scr[:, hcs[h]], kbs[h // 2]) for h in heads}
                dps = {h: _dot_nt(dom_scr[:, hcs[h]], vbs[h // 2]) for h in heads}
                lbs, l1s = {}, {}
                for h in heads:
                    lbs[h], l1s[h] = _sb_logits(zs[h])
                sufs = {h: _block_sums(_sel(mask, l1s[h], 0.0), upper, after=True)[0] for h in heads}
                prs, dzzs = {}, {}
                for h in heads:
                    expo = lbs[h] + sufs[h]
                    if width == WIDE:
                        expo = expo + jnp.sum(jnp.where(onehot, cm_ref[:, hcs[h]], 0.0), axis=1, keepdims=True)
                    pr = _sel(mask, jnp.exp(expo), 0.0)
                    dzzs[h] = pr * dps[h]
                    prs[h] = pr.astype(BF16)
                css = {h: _block_sums(dzzs[h], lower, after=False) for h in heads}
                dzbs = {}
                for h in heads:
                    c2 = c_scr[h]
                    prefix = css[h][0] + _widen(c2, width)
                    dz = _sel(mask, dzzs[h] * jnp.exp(l1s[h]) - jnp.exp(lbs[h]) * prefix, 0.0)
                    dzbs[h] = dz.astype(BF16)
                    c_scr[h] = c2 + css[h][1]
                dqs = {h: _dot(dzbs[h], kbs[h // 2]) for h in heads}
                dks = {h: _dot_tn(dzbs[h], qm_scr[:, hcs[h]]) for h in heads}
                dvs = {h: _dot_tn(prs[h], dom_scr[:, hcs[h]]) for h in heads}
                for p in sorted({h // 2 for h in heads}):
                    dq_scr[:, hcs[p]] += jnp.where(lo_m, dqs[2 * p], dqs[2 * p + 1])
                    dk_ref[pl.ds(off, width), hcs[p]] += dks[2 * p] + dks[2 * p + 1]
                    dv_ref[pl.ds(off, width), hcs[p]] += dvs[2 * p] + dvs[2 * p + 1]

        _over_key_blocks(i + 1, block, reverse=False)
        dq_ref[...] = (dq_scr[...] * SB_SCALE).astype(dq_ref.dtype)

    tile = lambda col: pl.BlockSpec((BLK, 512), lambda b, i: (b * nb + i, col))
    full = lambda col: pl.BlockSpec((lp, 512), lambda b, i: (b, col))
    acc = jax.ShapeDtypeStruct((t, 512), F32)
    once = jax.ShapeDtypeStruct((t, 512), BF16)
    hbm = pl.BlockSpec(memory_space=pl.ANY)
    outs = pl.pallas_call(
        body, name="sb_bwd", grid=(nbatch, nb),
        in_specs=[tile(0), full(1), full(2), tile(3), tile(0), tile(0),
                  pl.BlockSpec((BLK, 1024), lambda b, i: (b * nb + i, 0))] + [hbm] * ns,
        out_specs=[tile(0), full(0), full(0), tile(0)] + [hbm] * ns,
        out_shape=[once, acc, acc, once] + [jax.ShapeDtypeStruct(a.shape, a.dtype) for a in chunked],
        scratch_shapes=[pltpu.VMEM((8, BLK, BLK), F32), pltpu.VMEM((BLK, 1024), BF16),
                        pltpu.VMEM((BLK, 1024), BF16), pltpu.VMEM((BLK, 512), F32)] + _exchange_sems(ns),
        compiler_params=_params(("arbitrary", "arbitrary"), 56),
    )(proj, proj, proj, proj, o, dog, cm, *chunked)
    return outs[0], outs[1], outs[2], outs[3], list(outs[4:])


def _mla_prep(proj, gq, gkv, tabs, nbatch, nb):
    t = proj.shape[0]

    def body(mid_ref, gq_ref, gkv_ref, c_ref, s1_ref, s2_ref, cq_ref, ckv_ref, kr_ref):
        cq = mid_ref[:, 0:256].astype(F32)
        r = lax.rsqrt(jnp.mean(cq * cq, axis=1, keepdims=True) + NORM_EPS)
        cq_ref[...] = (cq * r * gq_ref[...]).astype(BF16)
        ckv = mid_ref[:, 256:384].astype(F32)
        r = lax.rsqrt(jnp.mean(ckv * ckv, axis=1, keepdims=True) + NORM_EPS)
        ckv_ref[...] = (ckv * r * gkv_ref[...]).astype(BF16)
        kr = mid_ref[:, 384:512].astype(F32)
        kr_ref[...] = _rope(kr, c_ref[...], s1_ref[...], s2_ref[...]).astype(BF16)

    tr = _row_tile(nb * BLK, (544, BLK))
    nt = nb * BLK // tr
    tab = pl.BlockSpec((tr, BLK), lambda b, i: (i, 0))
    rowspec = lambda w: pl.BlockSpec((tr, w), lambda b, i: (b * nt + i, 0))
    return pl.pallas_call(
        body, name="mla_prep", grid=(nbatch, nt),
        in_specs=[pl.BlockSpec((tr, 512), lambda b, i: (b * nt + i, EV_MID_BLK)),
                  pl.BlockSpec((1, 256), lambda b, i: (0, 0)), pl.BlockSpec((1, 128), lambda b, i: (0, 0)),
                  tab, tab, tab],
        out_specs=[rowspec(256), rowspec(128), rowspec(128)],
        out_shape=[jax.ShapeDtypeStruct((t, 256), BF16), jax.ShapeDtypeStruct((t, 128), BF16),
                   jax.ShapeDtypeStruct((t, 128), BF16)],
        compiler_params=_params(("parallel", "parallel")),
    )(proj, gq, gkv, *tabs)


def _mla_prep_bwd(proj, gq, gkv, tabs, dcqn, dckvn, dkrot, nbatch, nb):
    t = proj.shape[0]

    def body(mid_ref, gq_ref, gkv_ref, c_ref, s1_ref, s2_ref, dcq_ref, dckv_ref, dkr_ref,
             dmid_ref, dgq_ref, dgkv_ref):
        @pl.when((pl.program_id(0) == 0) & (pl.program_id(1) == 0))
        def _():
            dgq_ref[...] = jnp.zeros_like(dgq_ref)
            dgkv_ref[...] = jnp.zeros_like(dgkv_ref)

        def norm_bwd(x, gain, dy, dgain_ref):
            r = lax.rsqrt(jnp.mean(x * x, axis=1, keepdims=True) + NORM_EPS)
            nx = x * r
            dn = dy * gain
            dgain_ref[...] += jnp.sum(dy * nx, axis=0, keepdims=True)
            return r * (dn - nx * jnp.mean(dn * nx, axis=1, keepdims=True))

        dmid_ref[:, 0:256] = norm_bwd(
            mid_ref[:, 0:256].astype(F32), gq_ref[...], dcq_ref[...], dgq_ref).astype(BF16)
        dmid_ref[:, 256:384] = norm_bwd(
            mid_ref[:, 256:384].astype(F32), gkv_ref[...], dckv_ref[...], dgkv_ref).astype(BF16)
        dmid_ref[:, 384:512] = _rope_t(dkr_ref[...], c_ref[...], s1_ref[...], s2_ref[...]).astype(BF16)

    tr = _row_tile(nb * BLK, (544, BLK))
    nt = nb * BLK // tr
    tab = pl.BlockSpec((tr, BLK), lambda b, i: (i, 0))
    rowspec = lambda w: pl.BlockSpec((tr, w), lambda b, i: (b * nt + i, 0))
    vq = pl.BlockSpec((1, 256), lambda b, i: (0, 0))
    vkv = pl.BlockSpec((1, 128), lambda b, i: (0, 0))
    return pl.pallas_call(
        body, name="mla_prep_bwd", grid=(nbatch, nt),
        in_specs=[pl.BlockSpec((tr, 512), lambda b, i: (b * nt + i, EV_MID_BLK)), vq, vkv, tab, tab, tab,
                  rowspec(256), rowspec(128), rowspec(128)],
        out_specs=[rowspec(512), vq, vkv],
        out_shape=[jax.ShapeDtypeStruct((t, 512), BF16), jax.ShapeDtypeStruct((1, 256), F32),
                   jax.ShapeDtypeStruct((1, 128), F32)],
        compiler_params=_params(("arbitrary", "arbitrary")),
    )(proj, gq, gkv, *tabs, dcqn, dckvn, dkrot)


def _mla_scores(qf, kvb, krb, mask, lo_m):
    kf = jnp.where(lo_m, kvb, krb)
    return kf, _sel(mask, _dot_nt(qf, kf), NEG)


def _mla_fwd(qh, kvh, krot, proj, tabs, nbatch, nb):
    lp = nb * BLK
    t = nbatch * lp

    def body(q_ref, kv_ref, kr_ref, g_ref, c_ref, s1_ref, s2_ref, o_ref, og_ref, lse_ref,
             qf_scr, m_scr, l_scr, acc_scr):
        i = pl.program_id(1)
        row, lane = _iotas()
        lo_m = lane < HALF
        for h in range(8):
            hc = slice(h * BLK, (h + 1) * BLK)
            qf_scr[:, hc] = (_rope(q_ref[:, hc].astype(F32), c_ref[...], s1_ref[...], s2_ref[...])
                             * MLA_SCALE).astype(BF16)
        m_scr[...] = jnp.full(m_scr.shape, NEG, F32)
        l_scr[...] = jnp.zeros_like(l_scr)
        acc_scr[...] = jnp.zeros_like(acc_scr)

        def block(off, width, edge):
            mask = _key_mask(i, off, width, strict=False) if edge else None
            lo_k = lax.broadcasted_iota(jnp.int32, (width, BLK), 1) < HALF
            ones = jnp.ones((width, BLK), BF16)
            krb = kr_ref[pl.ds(off, width), :]
            heads = range(8)
            hcs = [slice(h * BLK, (h + 1) * BLK) for h in heads]
            kvbs = [kv_ref[pl.ds(off, width), hc] for hc in hcs]
            ss = [_mla_scores(qf_scr[:, hcs[h]], kvbs[h], krb, mask, lo_k)[1] for h in heads]
            ps, alphas = [], []
            for h in heads:
                m = m_scr[h]
                m2 = jnp.maximum(m, jnp.max(ss[h], axis=1, keepdims=True))
                ps.append(jnp.exp(ss[h] - _widen(m2, width)).astype(BF16))
                alphas.append(jnp.exp(m - m2))
                m_scr[h] = m2
            pvs = [_dot(ps[h], jnp.concatenate([kvbs[h], ones], axis=1)) for h in heads]
            for h in heads:
                l_scr[h] = alphas[h] * l_scr[h] + pvs[h][:, BLK:]
                acc_scr[h] = alphas[h] * acc_scr[h] + pvs[h][:, :BLK]

        _over_key_blocks(i + 1, block, reverse=False)
        lse = jnp.zeros((BLK, BLK), F32)
        for p in range(4):
            pc = slice(p * BLK, (p + 1) * BLK)
            o0 = acc_scr[2 * p] / l_scr[2 * p]
            o1 = acc_scr[2 * p + 1] / l_scr[2 * p + 1]
            o_ref[:, pc] = jnp.where(lo_m, pltpu.roll(o0, HALF, 1), o1)
            for h in (2 * p, 2 * p + 1):
                lse = lse + jnp.where(lane == h, m_scr[h] + jnp.log(l_scr[h]), 0.0)
        lse_ref[...] = lse
        g = g_ref[...].astype(F32)
        og_ref[...] = (o_ref[...] * g * _sigmoid(g)).astype(og_ref.dtype)

    tile = pl.BlockSpec((BLK, 512), lambda b, i: (b * nb + i, 0))
    tab = pl.BlockSpec((BLK, BLK), lambda b, i: (i, 0))
    heads = pltpu.VMEM((8, BLK, BLK), F32)
    return pl.pallas_call(
        body, name="mla_fwd", grid=(nbatch, nb),
        in_specs=[pl.BlockSpec((BLK, 1024), lambda b, i: (b * nb + i, 0)),
                  pl.BlockSpec((lp, 1024), lambda b, i: (b, 0)),
                  pl.BlockSpec((lp, BLK), lambda b, i: (b, 0)),
                  pl.BlockSpec((BLK, 512), lambda b, i: (b * nb + i, EV_GMLA_BLK // 4)),
                  tab, tab, tab],
        out_specs=[tile, tile, pl.BlockSpec((BLK, BLK), lambda b, i: (b * nb + i, 0))],
        out_shape=[jax.ShapeDtypeStruct((t, 512), F32), jax.ShapeDtypeStruct((t, 512), BF16),
                   jax.ShapeDtypeStruct((t, BLK), F32)],
        scratch_shapes=[pltpu.VMEM((BLK, 1024), BF16), heads, heads, heads],
        compiler_params=_params(("parallel", "arbitrary"), 48),
    )(qh, kvh, krot, proj, *tabs)


def _mla_bwd(qh, kvh, krot, proj, tabs, o, dog, lse, nbatch, nb):
    lp = nb * BLK
    t = nbatch * lp

    def body(q_ref, kv_ref, kr_ref, g_ref, c_ref, s1_ref, s2_ref, o_ref, dog_ref, lse_ref,
             dq_ref, dkv_ref, dkr_ref, dg_ref, qf_scr, do_scr, stat_scr, acc_scr):
        i = pl.program_id(1)

        @pl.when(i == 0)
        def _():
            dkv_ref[...] = jnp.zeros_like(dkv_ref)
            dkr_ref[...] = jnp.zeros_like(dkr_ref)

        row, lane = _iotas()
        lo_m = lane < HALF
        g = g_ref[...].astype(F32)
        sig = _sigmoid(g)
        dog_v = dog_ref[...]
        o_v = o_ref[...]
        dg_ref[...] = (dog_v * o_v * (sig * (1.0 + g * (1.0 - sig)))).astype(dg_ref.dtype)
        do = dog_v * g * sig
        do_o = do * o_v
        lse_blk = lse_ref[...]
        zero = jnp.zeros((BLK, BLK), F32)
        for h in range(8):
            hc = slice(h * BLK, (h + 1) * BLK)
            pc = slice((h // 2) * BLK, (h // 2 + 1) * BLK)
            qf_scr[:, hc] = (_rope(q_ref[:, hc].astype(F32), c_ref[...], s1_ref[...], s2_ref[...])
                             * MLA_SCALE).astype(BF16)
            dop = do[:, pc]
            do_src = dop if h % 2 else pltpu.roll(dop, HALF, 1)
            do_scr[:, hc] = jnp.where(lo_m, 0.0, do_src).astype(BF16)
            hm = (lane >= HALF) if h % 2 else lo_m
            stat_scr[h] = zero + jnp.sum(jnp.where(hm, do_o[:, pc], 0.0), axis=1, keepdims=True)
            stat_scr[8 + h] = zero + jnp.sum(jnp.where(lane == h, lse_blk, 0.0), axis=1, keepdims=True)
        acc_scr[...] = jnp.zeros_like(acc_scr)

        def block(off, width, edge):
            mask = _key_mask(i, off, width, strict=False) if edge else None
            lo_k = lax.broadcasted_iota(jnp.int32, (width, BLK), 1) < HALF
            krb = kr_ref[pl.ds(off, width), :]
            heads = range(8)
            hcs = [slice(h * BLK, (h + 1) * BLK) for h in heads]
            kvbs = [kv_ref[pl.ds(off, width), hc] for hc in hcs]
            qfs = [qf_scr[:, hc] for hc in hcs]
            dos = [do_scr[:, hc] for hc in hcs]
            scored = [_mla_scores(qfs[h], kvbs[h], krb, mask, lo_k) for h in heads]
            dps = [_dot_nt(dos[h], kvbs[h]) for h in heads]
            pbs, dss = [], []
            for h in heads:
                p = jnp.exp(scored[h][1] - _widen(stat_scr[8 + h], width))
                pbs.append(p.astype(BF16))
                dss.append((p * (dps[h] - _widen(stat_scr[h], width))).astype(BF16))
            dqs = [_dot(dss[h], scored[h][0]) for h in heads]
            dkfs = [_dot_tn(dss[h], qfs[h]) for h in heads]
            dvvs = [_dot_tn(pbs[h], dos[h]) for h in heads]
            dkr = jnp.zeros((width, BLK), F32)
            for h in heads:
                acc_scr[h] += dqs[h]
                dkv_ref[pl.ds(off, width), hcs[h]] += jnp.where(lo_k, dkfs[h], 0.0) + dvvs[h]
                dkr = dkr + jnp.where(lo_k, 0.0, dkfs[h])
            dkr_ref[pl.ds(off, width), :] += dkr

        _over_key_blocks(i + 1, block, reverse=False)
        for h in range(8):
            hc = slice(h * BLK, (h + 1) * BLK)
            dq_ref[:, hc] = _rope_t(acc_scr[h] * MLA_SCALE, c_ref[...], s1_ref[...], s2_ref[...]).astype(dq_ref.dtype)

    tile = lambda col: pl.BlockSpec((BLK, 512), lambda b, i: (b * nb + i, col))
    wide = pl.BlockSpec((BLK, 1024), lambda b, i: (b * nb + i, 0))
    full8 = pl.BlockSpec((lp, 1024), lambda b, i: (b, 0))
    full1 = pl.BlockSpec((lp, BLK), lambda b, i: (b, 0))
    tab = pl.BlockSpec((BLK, BLK), lambda b, i: (i, 0))
    return pl.pallas_call(
        body, name="mla_bwd", grid=(nbatch, nb),
        in_specs=[wide, full8, full1, tile(EV_GMLA_BLK // 4), tab, tab, tab, tile(0), tile(1),
                  pl.BlockSpec((BLK, BLK), lambda b, i: (b * nb + i, 0))],
        out_specs=[wide, full8, full1, tile(0)],
        out_shape=[jax.ShapeDtypeStruct((t, 1024), BF16), jax.ShapeDtypeStruct((t, 1024), F32),
                   jax.ShapeDtypeStruct((t, 128), F32), jax.ShapeDtypeStruct((t, 512), BF16)],
        scratch_shapes=[pltpu.VMEM((BLK, 1024), BF16), pltpu.VMEM((BLK, 1024), BF16),
                        pltpu.VMEM((16, BLK, BLK), F32), pltpu.VMEM((8, BLK, BLK), F32)],
        compiler_params=_params(("parallel", "arbitrary"), 56),
    )(qh, kvh, krot, proj, *tabs, o, dog, lse)


def _swa_setup(kk, i, k_refs, v_refs):
    row, lane = _iotas()
    own = (lane >= kk * HALF) & (lane < (kk + 1) * HALF)

    def dup(ref):
        x = ref[...].astype(F32)
        return jnp.where(own, x, pltpu.roll(x, HALF, 1)).astype(BF16)

    kcat = jnp.concatenate([dup(r) for r in k_refs], axis=0)
    vcat = jnp.concatenate([dup(r) for r in v_refs], axis=0)
    row2 = lax.broadcasted_iota(jnp.int32, (BLK, 2 * BLK), 0)
    lane2 = lax.broadcasted_iota(jnp.int32, (BLK, 2 * BLK), 1)
    is_meta = lane2 < BLK
    in_own = lane2 - BLK <= row2
    k_pos = jnp.where(is_meta, lane2, jnp.where(in_own, (i - 1) * BLK, (i - 2) * BLK) + lane2)
    d = i * BLK + row2 - k_pos
    mask = (d >= 0) & (k_pos >= jnp.where(is_meta, N_PAD, BLK))
    return lane, lane <= row, own, kcat, vcat, mask, d.astype(F32)


def _swa_fold(x, in_own):
    return jnp.concatenate([x[:, :BLK], jnp.where(in_own, x[:, 2 * BLK:], x[:, BLK:2 * BLK])], axis=1)


def _swa_unfold(x, in_own):
    w = x[:, BLK:]
    zero = jnp.zeros_like(w)
    return jnp.concatenate([x[:, :BLK], jnp.where(in_own, zero, w), jnp.where(in_own, w, zero)], axis=1)


def _swa_slope(kk, g_idx):
    return (2.0 ** (-(g_idx + 1) / 2.0)) * jnp.where(kk == 0, 1.0, 1.0 / 16.0)


def _swa_fwd(proj, sinks, nbatch, nb):
    lp = nb * BLK
    t = nbatch * lp

    def body(sink_ref, q_ref, ka, kb, kc, va, vb, vc, g_ref, o_ref, og_ref, lse_ref):
        kk = pl.program_id(1)
        i = pl.program_id(2)
        lane, in_own, own, kcat, vcat, mask, dist = _swa_setup(kk, i, (ka, kb, kc), (va, vb, vc))
        lo_m = lane < HALF
        heads = range(8)
        qms = []
        for h in heads:
            qp = q_ref[:, (h // 2) * BLK:(h // 2 + 1) * BLK]
            qms.append(jnp.where((lane >= HALF) if h % 2 else lo_m, qp, jnp.zeros_like(qp)) * SWA_SCALE)
        qks = [_dot_nt(qms[h], kcat) for h in heads]
        ps, ls, lses = [], [], []
        for h in heads:
            sink = sink_ref[kk, h]
            s = jnp.where(mask, _swa_fold(qks[h], in_own) - _swa_slope(kk, h) * dist, NEG)
            mx = jnp.maximum(jnp.max(s, axis=1, keepdims=True), sink)
            p = jnp.exp(s - mx)
            l = jnp.exp(sink - mx) + jnp.sum(p, axis=1, keepdims=True)
            ps.append(_swa_unfold(p.astype(BF16), in_own))
            ls.append(l)
            lses.append(mx + jnp.log(l))
        pvs = [_dot(ps[h], vcat) for h in heads]
        lse_out = jnp.zeros((BLK, BLK), F32)
        for m in range(4):
            cols = slice(m * BLK, (m + 1) * BLK)
            outp = jnp.where(lo_m, pvs[2 * m] / ls[2 * m], pvs[2 * m + 1] / ls[2 * m + 1])
            o_ref[:, cols] = outp
            g = g_ref[:, cols].astype(F32)
            og_ref[:, cols] = (outp * g * _sigmoid(g)).astype(og_ref.dtype)
            for h in (2 * m, 2 * m + 1):
                lse_out = lse_out + jnp.where(lane == h, lses[h], 0.0)
        lse_ref[...] = lse_out

    def kvspec(col, which):
        if which == 0:
            return pl.BlockSpec((BLK, BLK), lambda b, kk, i: (b * nb, col))
        if which == 1:
            return pl.BlockSpec((BLK, BLK), lambda b, kk, i: (b * nb + jnp.maximum(i - 1, 0), col))
        return pl.BlockSpec((BLK, BLK), lambda b, kk, i: (b * nb + i, col))

    wide = lambda c0: pl.BlockSpec((BLK, 512), lambda b, kk, i: (b * nb + i, c0 + kk))
    return pl.pallas_call(
        body, name="swa_fwd", grid=(nbatch, 2, nb),
        in_specs=[pl.BlockSpec(memory_space=pltpu.SMEM), wide(0),
                  kvspec(OD_K_BLK, 0), kvspec(OD_K_BLK, 1), kvspec(OD_K_BLK, 2),
                  kvspec(OD_V_BLK, 0), kvspec(OD_V_BLK, 1), kvspec(OD_V_BLK, 2), wide(2)],
        out_specs=[wide(0), wide(0), pl.BlockSpec((BLK, BLK), lambda b, kk, i: (b * nb + i, kk))],
        out_shape=[jax.ShapeDtypeStruct((t, 1024), F32), jax.ShapeDtypeStruct((t, 1024), BF16),
                   jax.ShapeDtypeStruct((t, 256), F32)],
        compiler_params=_params(("parallel", "parallel", "arbitrary")),
    )(sinks, proj, proj, proj, proj, proj, proj, proj, proj)


def _swa_bwd(proj, sinks, o, dog, lse, nbatch, nb):
    lp = nb * BLK
    t = nbatch * lp

    def body(sink_ref, q_ref, ka, kb, kc, va, vb, vc, g_ref, o_ref, dog_ref, lse_ref,
             dq_ref, dg_ref, dk_ref, dv_ref, dsink_ref):
        kk = pl.program_id(1)
        i = pl.program_id(2)

        @pl.when((kk == 0) & (i == 0))
        def _():
            dk_ref[...] = jnp.zeros_like(dk_ref)
            dv_ref[...] = jnp.zeros_like(dv_ref)

        @pl.when(i == 0)
        def _():
            dsink_ref[...] = jnp.zeros_like(dsink_ref)

        lane, in_own, own, kcat, vcat, mask, dist = _swa_setup(kk, i, (ka, kb, kc), (va, vb, vc))
        lo_m = lane < HALF
        row8 = lax.broadcasted_iota(jnp.int32, (8, BLK), 0)
        lse_blk = lse_ref[...]
        heads = range(8)
        qms, doms, deltas, lse_hs = [], [], [], []
        for m in range(4):
            cols = slice(m * BLK, (m + 1) * BLK)
            qp = q_ref[:, cols]
            g = g_ref[:, cols].astype(F32)
            sig = _sigmoid(g)
            dog_v = dog_ref[:, cols]
            o_v = o_ref[:, cols]
            dg_ref[:, cols] = (dog_v * o_v * (sig * (1.0 + g * (1.0 - sig)))).astype(dg_ref.dtype)
            do = dog_v * g * sig
            do_o = do * o_v
            dob = do.astype(BF16)
            for h in (2 * m, 2 * m + 1):
                hm = (lane >= HALF) if h % 2 else lo_m
                qms.append(jnp.where(hm, qp, jnp.zeros_like(qp)) * SWA_SCALE)
                doms.append(jnp.where(hm, dob, jnp.zeros_like(dob)))
                deltas.append(jnp.sum(jnp.where(hm, do_o, 0.0), axis=1, keepdims=True))
                lse_hs.append(jnp.sum(jnp.where(lane == h, lse_blk, 0.0), axis=1, keepdims=True))
        qks = [_dot_nt(qms[h], kcat) for h in heads]
        dps = [_dot_nt(doms[h], vcat) for h in heads]
        pbs, dss = [], []
        dsink = jnp.zeros((8, BLK), F32)
        for h in heads:
            s = jnp.where(mask, _swa_fold(qks[h], in_own) - _swa_slope(kk, h) * dist, NEG)
            p = jnp.exp(s - lse_hs[h])
            pbs.append(_swa_unfold(p.astype(BF16), in_own))
            dss.append(_swa_unfold((p * (_swa_fold(dps[h], in_own) - deltas[h])).astype(BF16), in_own))
            tot = jnp.sum(-jnp.exp(sink_ref[kk, h] - lse_hs[h]) * deltas[h], axis=0, keepdims=True)
            dsink = dsink + jnp.where(row8 == h, tot, 0.0)
        dsink_ref[...] += dsink
        dqs = [_dot(dss[h], kcat) for h in heads]
        dks = [_dot_tn(dss[h], qms[h]) for h in heads]
        dvs = [_dot_tn(pbs[h], doms[h]) for h in heads]
        for m in range(4):
            dq_ref[:, m * BLK:(m + 1) * BLK] = (
                jnp.where(lo_m, dqs[2 * m], dqs[2 * m + 1]) * SWA_SCALE).astype(dq_ref.dtype)
        dk = dks[0]
        dv = dvs[0]
        for h in range(1, 8):
            dk = dk + dks[h]
            dv = dv + dvs[h]
        offs = [0, pl.multiple_of(jnp.maximum(i - 1, 0) * BLK, BLK), pl.multiple_of(i * BLK, BLK)]
        for x in range(3):
            rows = slice(x * BLK, (x + 1) * BLK)
            dkx, dvx = dk[rows], dv[rows]
            dk_ref[pl.ds(offs[x], BLK), :] += jnp.where(own, dkx + pltpu.roll(dkx, HALF, 1), 0.0)
            dv_ref[pl.ds(offs[x], BLK), :] += jnp.where(own, dvx + pltpu.roll(dvx, HALF, 1), 0.0)

    def kvspec(col, which):
        if which == 0:
            return pl.BlockSpec((BLK, BLK), lambda b, kk, i: (b * nb, col))
        if which == 1:
            return pl.BlockSpec((BLK, BLK), lambda b, kk, i: (b * nb + jnp.maximum(i - 1, 0), col))
        return pl.BlockSpec((BLK, BLK), lambda b, kk, i: (b * nb + i, col))

    wide = lambda c0: pl.BlockSpec((BLK, 512), lambda b, kk, i: (b * nb + i, c0 + kk))
    full = pl.BlockSpec((lp, BLK), lambda b, kk, i: (b, 0))
    return pl.pallas_call(
        body, name="swa_bwd", grid=(nbatch, 2, nb),
        in_specs=[pl.BlockSpec(memory_space=pltpu.SMEM), wide(0),
                  kvspec(OD_K_BLK, 0), kvspec(OD_K_BLK, 1), kvspec(OD_K_BLK, 2),
                  kvspec(OD_V_BLK, 0), kvspec(OD_V_BLK, 1), kvspec(OD_V_BLK, 2), wide(2),
                  wide(0), wide(0), pl.BlockSpec((BLK, BLK), lambda b, kk, i: (b * nb + i, kk))],
        out_specs=[wide(0), wide(0), full, full,
                   pl.BlockSpec((8, BLK), lambda b, kk, i: (b * 2 + kk, 0))],
        out_shape=[jax.ShapeDtypeStruct((t, 1024), BF16), jax.ShapeDtypeStruct((t, 1024), BF16),
                   jax.ShapeDtypeStruct((t, 128), F32), jax.ShapeDtypeStruct((t, 128), F32),
                   jax.ShapeDtypeStruct((nbatch * 16, BLK), F32)],
        compiler_params=_params(("parallel", "arbitrary", "arbitrary")),
    )(sinks, proj, proj, proj, proj, proj, proj, proj, proj, o, dog, lse)


def _rope_tables(lp):
    pos = (jnp.arange(lp) - N_PAD).astype(F32)
    inv = ROPE_BASE ** (-jnp.arange(16, dtype=F32) / 16.0)
    ang = pos[:, None] * inv[None, :]
    cos, sin = jnp.cos(ang), jnp.sin(ang)
    z16 = jnp.zeros((lp, 16), F32)
    c = jnp.concatenate([jnp.ones((lp, 64), F32), cos, cos, jnp.zeros((lp, 32), F32)], axis=1)
    s1 = jnp.concatenate([jnp.zeros((lp, 64), F32), -sin, z16, jnp.zeros((lp, 32), F32)], axis=1)
    s2 = jnp.concatenate([jnp.zeros((lp, 64), F32), z16, sin, jnp.zeros((lp, 32), F32)], axis=1)
    return c, s1, s2


def _local_step(h0, tgt, norm_g, final_g, gq, gkv, sinks, w_ie, late_shards, nbatch, nb):
    lp = nb * BLK
    tabs = _rope_tables(lp)
    g0, g1 = norm_g[0:1], norm_g[1:2]
    sinks2 = sinks.reshape(2, 8)

    proj_e, hn0 = _norm_mm(h0, g0, w_ie, "proj_even")
    o_sb, og_sb, cm, gathered = _sb_fwd(proj_e, [late_shards[name] for name in _LATE], nbatch, nb)
    full = {name: _unchunk(name, blk) for name, blk in zip(_LATE, gathered)}
    w_uq, w_ukv, w_oe = _uq_to_compute(full["ev_w_uq"]), full["ev_w_ukv"], full["ev_w_out"]
    w_io, w_oo = _od_in_to_compute(full["od_w_in"]), full["od_w_out"]
    cqn, ckvn, krot = _mla_prep(proj_e, gq, gkv, tabs, nbatch, nb)
    qh = _mm(cqn, w_uq, "nn", "mla_uq", out_dtype=BF16)
    kvh = _mm(ckvn, w_ukv, "nn", "mla_ukv", out_dtype=BF16)
    o_mla, og_mla, lse_m = _mla_fwd(qh, kvh, krot, proj_e, tabs, nbatch, nb)
    h1 = _mm([og_sb, og_mla], w_oe, "nn", "out_even", add=h0)
    proj_o, hn1 = _norm_mm(h1, g1, w_io, "proj_odd")
    o_o, og_o, lse_o = _swa_fwd(proj_o, sinks2, nbatch, nb)
    h2 = _mm(og_o, w_oo, "nn", "out_odd", add=h1)
    dh2, lossv, d_final_g = _final(h2, tgt, final_g, nbatch, nb)

    dog_o = _mm(dh2, w_oo, "nt", "d_out_odd")
    d_w_oo, = _mm_tn(og_o, [dh2], "dw_out_odd")
    dq_o, dg_o, dk_o, dv_o, dsink = _swa_bwd(proj_o, sinks2, o_o, dog_o, lse_o, nbatch, nb)
    dproj_o = [dq_o, dg_o, dk_o, dv_o]
    dh1, d_g1 = _mm_norm_bwd(dproj_o, w_io, h1, g1, dh2, "d_proj_odd")
    dw_q, dw_g, dw_k, dw_v = _mm_tn(hn1, dproj_o, "dw_proj_odd")

    dog_e = _mm(dh1, w_oe, "nt", "d_out_even")
    d_w_oe_sb, = _mm_tn(og_sb, [dh1], "dw_out_even_sb")
    d_w_oe_mla, = _mm_tn(og_mla, [dh1], "dw_out_even_mla")
    early = dict(od_w_in=jnp.concatenate([dw_q, dw_k, dw_v, dw_g], axis=1), od_w_out=d_w_oo,
                 ev_w_out=jnp.concatenate([d_w_oe_sb, d_w_oe_mla], axis=0))
    dq_sb, dk_sb, dv_sb, dg_sb, received = _sb_bwd(
        proj_e, o_sb, dog_e, cm, [_chunk(name, early[name]) for name in _EARLY_GRADS], nbatch, nb)
    dqh, dkvh, dkrot, dg_mla = _mla_bwd(qh, kvh, krot, proj_e, tabs, o_mla, dog_e, lse_m, nbatch, nb)
    dcqn = _mm(dqh, w_uq, "nt", "d_mla_uq")
    d_w_uq, = _mm_tn(cqn, [dqh], "dw_mla_uq")
    dckvn = _mm(dkvh, w_ukv, "nt", "d_mla_ukv")
    d_w_ukv, = _mm_tn(ckvn, [dkvh], "dw_mla_ukv")
    dmid, d_gq, d_gkv = _mla_prep_bwd(proj_e, gq, gkv, tabs, dcqn, dckvn, dkrot, nbatch, nb)
    dproj_e = [dq_sb, dk_sb, dv_sb, dg_sb, dmid, dg_mla]
    dh0, d_g0 = _mm_norm_bwd(dproj_e, w_ie, h0, g0, dh1, "d_proj_even")
    dw_e = _mm_tn(hn0, dproj_e, "dw_proj_even")

    d_sinks = dsink.reshape(nbatch, 2, 8, BLK)[:, :, :, 0].sum(axis=0).reshape(1, 16)
    d_norm_g = jnp.concatenate([d_g0, d_g1], axis=0)
    d_ev_w_in = jnp.concatenate(list(dw_e[:4]) + [dw_e[4][:, :384], dw_e[4][:, 448:480], dw_e[5]], axis=1)
    return dict(lossv=lossv, dh0=dh0, norm_g=d_norm_g, final_g=d_final_g, gq=d_gq, gkv=d_gkv, sinks=d_sinks,
                ev_w_in=d_ev_w_in, ev_w_uq=_uq_from_compute(d_w_uq), ev_w_ukv=d_w_ukv, received=received)


def _ev_in_to_compute(w):
    z = lambda n: jnp.zeros((w.shape[0], n), w.dtype)
    return jnp.concatenate([w[:, :2432], z(64), w[:, 2432:2464], z(32), w[:, 2464:]], axis=1)


def _uq_to_compute(w):
    w3 = w.reshape(256, 8, 96)
    return jnp.concatenate([w3, jnp.zeros((256, 8, 32), w.dtype)], axis=2).reshape(256, 1024)


def _uq_from_compute(w):
    return w.reshape(256, 8, 128)[:, :, :96].reshape(256, 768)


def _od_in_to_compute(w):
    return jnp.concatenate([w[:, :1024], w[:, 1280:], w[:, 1024:1280]], axis=1)


_BIG = dict(ev_w_in=(1024, 2976, 1), ev_w_uq=(256, 768, 1), ev_w_ukv=(128, 1024, 1),
            ev_w_out=(1024, 1024, 0), od_w_in=(1024, 2304, 1), od_w_out=(1024, 1024, 0))
_LATE = ("ev_w_uq", "ev_w_ukv", "ev_w_out", "od_w_in", "od_w_out")
_EARLY_GRADS = ("od_w_in", "od_w_out", "ev_w_out")
_LAST_GRADS = ("ev_w_in", "ev_w_uq", "ev_w_ukv")


def _unchunk(name, blk):
    rows, cols, axis = _BIG[name]
    return blk.transpose(1, 0, 2).reshape(rows, cols) if axis == 1 else blk.reshape(rows, cols)


def _chunk(name, g):
    rows, cols, axis = _BIG[name]
    g = g.astype(BF16)
    return g.reshape(rows, N_DEV, cols // N_DEV).transpose(1, 0, 2) if axis == 1 else g.reshape(N_DEV, rows // N_DEV, cols)


def _all_gather(shards, name):
    n = len(shards)

    def body(*refs):
        xs, outs = refs[:n], refs[n:2 * n]
        send_sems, recv_sems, local_sems = refs[2 * n:]
        x, y, c = lax.axis_index("x"), lax.axis_index("y"), lax.axis_index("c")
        me, sibling = (x, y, c), (x, y, 1 - c)
        chips = [(1 - x, y), (x, 1 - y), (1 - x, 1 - y)]
        arrays = range(n)

        def copy(k, a, block, to, from_input=False):
            px, py, pc = block
            dst = outs[a].at[4 * px + 2 * py + pc]
            return pltpu.make_async_remote_copy(
                src_ref=xs[a] if from_input else dst, dst_ref=dst,
                send_sem=send_sems.at[k, a], recv_sem=recv_sems.at[k, a],
                device_id=to, device_id_type=pl.DeviceIdType.MESH)

        mine = [pltpu.make_async_copy(xs[a], outs[a].at[4 * x + 2 * y + c], local_sems.at[a]) for a in arrays]
        for cp in mine:
            cp.start()
        first = [copy(0, a, me, sibling, True) for a in arrays]
        for j, chip in enumerate(chips):
            first += [copy(1 + j, a, me, (*chip, c), True) for a in arrays]
        for cp in first:
            cp.start()
        passed = []
        for j, chip in enumerate(chips):
            for a in arrays:
                copy(1 + j, a, (*chip, c), me).wait_recv()
                passed.append(copy(4 + j, a, (*chip, c), sibling))
                passed[-1].start()
        for a in arrays:
            copy(0, a, sibling, me).wait_recv()
        for j, chip in enumerate(chips):
            for a in arrays:
                copy(4 + j, a, (*chip, 1 - c), me).wait_recv()
        for cp in first + passed:
            cp.wait_send()
        for cp in mine:
            cp.wait()

    hbm = pl.BlockSpec(memory_space=pl.ANY)
    return pl.pallas_call(
        body, name=name,
        out_shape=[jax.ShapeDtypeStruct((N_DEV,) + s.shape, s.dtype) for s in shards],
        in_specs=[hbm] * n, out_specs=[hbm] * n,
        scratch_shapes=[pltpu.SemaphoreType.DMA((7, n)), pltpu.SemaphoreType.DMA((7, n)),
                        pltpu.SemaphoreType.DMA((n,))],
    )(*shards)


def _exchange_sum(chunked, received, name):
    n, m = len(chunked), len(received)
    arrs = list(chunked) + list(received)

    def body(*refs):
        ins, outs = refs[:n + m], refs[n + m:2 * (n + m)]
        bufs = refs[2 * (n + m):3 * (n + m)]
        send_sems, recv_sems, local_sems, load_sems = refs[3 * (n + m):]
        loads = [pltpu.make_async_copy(ins[n + a], bufs[n + a], load_sems.at[a]) for a in range(m)]
        for cp in loads:
            cp.start()
        local, sends, recvs = _direct_exchange(ins[:n], bufs[:n], send_sems, recv_sems, local_sems, chunked=True)
        for cp in local + sends:
            cp.start()
        for a, cp in enumerate(loads):
            cp.wait()
            _sum_slots(bufs[n + a], outs[n + a])
        for cp in local:
            cp.wait()
        for cp in recvs:
            cp.wait_recv()
        for cp in sends:
            cp.wait_send()
        for a in range(n):
            _sum_slots(bufs[a], outs[a])

    hbm = pl.BlockSpec(memory_space=pl.ANY)
    vm = pl.BlockSpec(memory_space=pltpu.VMEM)
    return pl.pallas_call(
        body, name=name,
        out_shape=[jax.ShapeDtypeStruct(a.shape[1:], F32) for a in arrs],
        in_specs=[hbm] * (n + m), out_specs=[vm] * (n + m),
        scratch_shapes=[pltpu.VMEM(a.shape, a.dtype) for a in arrs] + _exchange_sems(n)
        + [pltpu.SemaphoreType.DMA((max(m, 1),))],
        compiler_params=pltpu.CompilerParams(vmem_limit_bytes=48 << 20),
    )(*arrs)


def _sum_slots(buf, out):
    rows = buf.shape[1]

    def add(sl):
        acc = buf[(0,) + sl].astype(F32)
        for k in range(1, N_DEV):
            acc = acc + buf[(k,) + sl].astype(F32)
        out[sl] = acc

    if rows > BLK and rows % BLK == 0:
        def step(r, carry):
            add((pl.ds(pl.multiple_of(r * BLK, BLK), BLK), slice(None)))
            return carry

        lax.fori_loop(0, rows // BLK, step, 0)
    else:
        add((slice(None), slice(None)))


def _adamw(ws, gs, ms, vs, name, steps):
    n = len(ws)

    def body(*refs):
        ins, outs = refs[:4 * n], refs[4 * n:]
        for k in range(n):
            w_ref, g_ref, m_ref, v_ref = ins[4 * k:4 * k + 4]
            d_ref, nm_ref, nv_ref = outs[3 * k:3 * k + 3]
            g = g_ref[...]
            m = ADAM_B1 * m_ref[...] + (1.0 - ADAM_B1) * g
            v = ADAM_B2 * v_ref[...] + (1.0 - ADAM_B2) * (g * g)
            m_hat = m / (1.0 - ADAM_B1 ** ADAM_STEP)
            v_hat = v / (1.0 - ADAM_B2 ** ADAM_STEP)
            d_ref[...] = -ADAM_LR * (m_hat / (jnp.sqrt(v_hat) + ADAM_EPS) + ADAM_WD * w_ref[...])
            nm_ref[...] = m
            nv_ref[...] = v

    args, out_shape, in_specs, out_specs = [], [], [], []
    for k in range(n):
        rows, cols = ws[k].shape
        spec = pl.BlockSpec((rows // steps, cols), lambda i: (i, 0))
        args += [ws[k], gs[k], ms[k], vs[k]]
        in_specs += [spec] * 4
        out_specs += [spec] * 3
        out_shape += [jax.ShapeDtypeStruct(ws[k].shape, F32)] * 3
    outs = pl.pallas_call(
        body, name=name, grid=(steps,), out_shape=out_shape, in_specs=in_specs, out_specs=out_specs,
        compiler_params=_params(("parallel",)),
    )(*args)
    return [tuple(outs[3 * k:3 * k + 3]) for k in range(n)]


def kernel(x, meta, norm_g, final_g, ev_w_in, ev_q_norm_g, ev_kv_norm_g, ev_w_uq, ev_w_ukv, ev_w_out, od_w_in, od_sinks, od_w_out, loss_target, m_meta, m_norm_g, m_final_g, m_ev_w_in, m_ev_q_norm_g, m_ev_kv_norm_g, m_ev_w_uq, m_ev_w_ukv, m_ev_w_out, m_od_w_in, m_od_sinks, m_od_w_out, v_meta, v_norm_g, v_final_g, v_ev_w_in, v_ev_q_norm_g, v_ev_kv_norm_g, v_ev_w_uq, v_ev_w_ukv, v_ev_w_out, v_od_w_in, v_od_sinks, v_od_w_out):
    nbatch, seq, d = x.shape
    nb = seq // BLK + 1
    lp = nb * BLK
    shards = dict(ev_w_in=ev_w_in[0], ev_w_uq=ev_w_uq[0], ev_w_ukv=ev_w_ukv[0], ev_w_out=ev_w_out[0],
                  od_w_in=od_w_in[0], od_w_out=od_w_out[0])

    w_ie_blocks, meta_blocks = _all_gather([shards["ev_w_in"].astype(BF16), meta], "gather_weights")
    meta_full = meta_blocks.transpose(1, 0, 2).reshape(N_META, d)

    head = jnp.concatenate([jnp.zeros((N_PAD, d), F32), meta_full], axis=0)
    h0 = jnp.concatenate([jnp.broadcast_to(head[None], (nbatch, BLK, d)), x], axis=1).reshape(nbatch * lp, d)
    grads = _local_step(
        h0, loss_target.reshape(nbatch * seq, d), norm_g, final_g.reshape(1, d), ev_q_norm_g, ev_kv_norm_g,
        od_sinks, _ev_in_to_compute(_unchunk("ev_w_in", w_ie_blocks)),
        {name: shards[name].astype(BF16) for name in _LATE}, nbatch, nb)
    dh0 = grads["dh0"].reshape(nbatch, lp, d)
    grad_x = dh0[:, BLK:]

    d_meta = dh0[:, N_PAD:BLK].sum(axis=0).reshape(N_META, N_DEV, BLK).transpose(1, 0, 2)
    pad = lambda a, n: jnp.concatenate([a.reshape(1, -1), jnp.zeros((1, n - a.size), F32)], axis=1)
    loss_part = (0.5 / d * jnp.sum(grads["lossv"])).reshape(1, 1)
    rep = jnp.concatenate([grads["norm_g"].reshape(1, -1), grads["final_g"], grads["gq"], pad(loss_part, 256),
                           pad(grads["gkv"], 256), pad(grads["sinks"], 256)], axis=1).reshape(32, BLK)
    small = jnp.concatenate([d_meta, jnp.broadcast_to(rep[None], (N_DEV, 32, BLK))], axis=1)
    reduced = _exchange_sum([_chunk(name, grads[name]) for name in _LAST_GRADS] + [small],
                            grads["received"], "reduce_grads")
    g_shard = dict(zip(_LAST_GRADS + ("small",) + _EARLY_GRADS, reduced))
    red_small = g_shard.pop("small")
    rep = red_small[N_META:].reshape(1, -1)
    loss = rep[0, 3 * d + 256]
    g_small = dict(meta=red_small[:N_META], norm_g=rep[:, :2 * d].reshape(2, d), final_g=rep[:, 2 * d:3 * d],
                   ev_q_norm_g=rep[:, 3 * d:3 * d + 256], ev_kv_norm_g=rep[:, 3 * d + 512:3 * d + 640],
                   od_sinks=rep[:, 3 * d + 768:3 * d + 784])

    names = ["meta", "norm_g", "final_g", "ev_w_in", "ev_q_norm_g", "ev_kv_norm_g", "ev_w_uq", "ev_w_ukv",
             "ev_w_out", "od_w_in", "od_sinks", "od_w_out"]
    given = dict(meta=(meta, m_meta, v_meta), norm_g=(norm_g, m_norm_g, v_norm_g),
                 final_g=(final_g, m_final_g, v_final_g), ev_w_in=(ev_w_in, m_ev_w_in, v_ev_w_in),
                 ev_q_norm_g=(ev_q_norm_g, m_ev_q_norm_g, v_ev_q_norm_g),
                 ev_kv_norm_g=(ev_kv_norm_g, m_ev_kv_norm_g, v_ev_kv_norm_g),
                 ev_w_uq=(ev_w_uq, m_ev_w_uq, v_ev_w_uq), ev_w_ukv=(ev_w_ukv, m_ev_w_ukv, v_ev_w_ukv),
                 ev_w_out=(ev_w_out, m_ev_w_out, v_ev_w_out), od_w_in=(od_w_in, m_od_w_in, v_od_w_in),
                 od_sinks=(od_sinks, m_od_sinks, v_od_sinks), od_w_out=(od_w_out, m_od_w_out, v_od_w_out))
    ws, gs, ms, vs = [], [], [], []
    for name in names:
        g2 = g_shard[name] if name in g_shard else g_small[name]
        w, m, v = given[name]
        ws.append(w.reshape(g2.shape))
        ms.append(m.reshape(g2.shape))
        vs.append(v.reshape(g2.shape))
        gs.append(g2)
    big = [k for k, name in enumerate(names) if name in _BIG]
    small = [k for k, name in enumerate(names) if name not in _BIG]
    pick = lambda xs, ks: [xs[k] for k in ks]
    upd_big = _adamw(pick(ws, big), pick(gs, big), pick(ms, big), pick(vs, big), "adamw", N_DEV)
    upd_small = _adamw(pick(ws, small), pick(gs, small), pick(ms, small), pick(vs, small), "adamw_small", 1)
    upd = dict(zip(big + small, upd_big + upd_small))
    shape_of = {name: given[name][0].shape for name in names}
    grads_out = [gs[k].reshape(shape_of[n]) for k, n in enumerate(names)]
    deltas = [upd[k][0].reshape(shape_of[n]) for k, n in enumerate(names)]
    new_m = [upd[k][1].reshape(shape_of[n]) for k, n in enumerate(names)]
    new_v = [upd[k][2].reshape(shape_of[n]) for k, n in enumerate(names)]
    return (loss, grad_x, *grads_out, *deltas, *new_m, *new_v)
```

```python
import jax
import jax.numpy as jnp
from jax import lax
from jax.experimental import pallas as pl
from jax.experimental.pallas import tpu as pltpu

F32 = jnp.float32
BF16 = jnp.bfloat16

D_MODEL = 1024
N_META = 16
BLK = 128
HALF = 64
N_PAD = BLK - N_META
NORM_EPS = 1e-6
NEG = -1e30
N_DEV = 8

SB_SCALE = 64 ** -0.5
MLA_SCALE = 96 ** -0.5
SWA_SCALE = 64 ** -0.5
ROPE_BASE = 10000.0

EV_IN_PAD = 3072
EV_MID_BLK = 4
EV_GMLA_BLK = 20
OD_K_BLK = 16
OD_V_BLK = 17

ADAM_LR = 0.001
ADAM_B1 = 0.9
ADAM_B2 = 0.999
ADAM_EPS = 1e-08
ADAM_WD = 0.01
ADAM_STEP = 10


def _dot(a, b):
    return lax.dot_general(a, b, (((1,), (0,)), ((), ())), preferred_element_type=F32)


def _dot_nt(a, b):
    return lax.dot_general(a, b, (((1,), (1,)), ((), ())), preferred_element_type=F32)


def _dot_tn(a, b):
    return lax.dot_general(a, b, (((0,), (0,)), ((), ())), preferred_element_type=F32)


def _sigmoid(x):
    return 1.0 / (1.0 + jnp.exp(-x))


def _iotas():
    row = lax.broadcasted_iota(jnp.int32, (BLK, BLK), 0)
    lane = lax.broadcasted_iota(jnp.int32, (BLK, BLK), 1)
    return row, lane


WIDE = 2 * BLK
HEAD_GROUPS = (range(0, 8),)


def _key_mask(i, first_key, width, strict):
    t_pos = i * BLK + lax.broadcasted_iota(jnp.int32, (BLK, width), 0)
    s_pos = first_key + lax.broadcasted_iota(jnp.int32, (BLK, width), 1)
    seen = (s_pos < t_pos) if strict else (s_pos <= t_pos)
    return seen & (s_pos >= N_PAD)


def _widen(x, width):
    return x if width == BLK else jnp.concatenate([x] * (width // BLK), axis=1)


def _over_key_blocks(n, block, reverse):
    pairs = n // 2
    last = pl.multiple_of((n - 1) * BLK, BLK)

    def step(jj, carry):
        jp = (pairs - 1 - jj) if reverse else jj
        off = pl.multiple_of(jp * WIDE, WIDE)
        edge = (jp == 0) | (jp == pairs - 1)
        pl.when(edge)(lambda: block(off, WIDE, True))
        pl.when(jnp.logical_not(edge))(lambda: block(off, WIDE, False))
        return carry

    if reverse:
        pl.when(n % 2 == 1)(lambda: block(last, BLK, True))
        lax.fori_loop(0, pairs, step, 0)
    else:
        lax.fori_loop(0, pairs, step, 0)
        pl.when(n % 2 == 1)(lambda: block(last, BLK, True))


def _sel(mask, x, fill):
    return x if mask is None else jnp.where(mask, x, fill)


def _rope(x, c, s1, s2):
    return x * c + pltpu.roll(x, BLK - 16, 1) * s1 + pltpu.roll(x, 16, 1) * s2


def _rope_t(x, c, s1, s2):
    return x * c - pltpu.roll(x, BLK - 16, 1) * s1 - pltpu.roll(x, 16, 1) * s2


def _params(sem, vmem_mb=None):
    kw = dict(dimension_semantics=sem)
    if vmem_mb is not None:
        kw["vmem_limit_bytes"] = vmem_mb << 20
    return pltpu.CompilerParams(**kw)


def _row_tile(t, cands):
    for c in cands:
        if t % c == 0:
            return c
    raise ValueError(t)


def _mm(a, w, mode, name, add=None, out_dtype=F32):
    pieces = list(a) if isinstance(a, (list, tuple)) else [a]
    m = pieces[0].shape[0]
    n = w.shape[1] if mode == "nn" else w.shape[0]
    tm = _row_tile(m, (544, 256, 128) if n <= 1024 else (256, 128))
    widths = [p.shape[1] for p in pieces]
    offs = [sum(widths[:i]) for i in range(len(widths))]

    def body(*refs):
        p_refs = refs[:len(pieces)]
        w_ref = refs[len(pieces)]
        o_ref = refs[-1]
        acc = None
        for p_ref, off, wd in zip(p_refs, offs, widths):
            x = p_ref[...].astype(BF16)
            part = _dot(x, w_ref[off:off + wd, :]) if mode == "nn" else _dot_nt(x, w_ref[:, off:off + wd])
            acc = part if acc is None else acc + part
        if add is not None:
            acc = acc + refs[len(pieces) + 1][...]
        o_ref[...] = acc.astype(o_ref.dtype)

    in_specs = [pl.BlockSpec((tm, wd), lambda i: (i, 0)) for wd in widths]
    in_specs.append(pl.BlockSpec(w.shape, lambda i: (0, 0)))
    args = pieces + [w]
    if add is not None:
        in_specs.append(pl.BlockSpec((tm, n), lambda i: (i, 0)))
        args.append(add)
    return pl.pallas_call(
        body, name=name, grid=(m // tm,), in_specs=in_specs,
        out_specs=pl.BlockSpec((tm, n), lambda i: (i, 0)),
        out_shape=jax.ShapeDtypeStruct((m, n), out_dtype),
        compiler_params=_params(("parallel",), 48),
    )(*args)


def _mm_tn(x, pieces, name):
    t, k = x.shape
    tt = _row_tile(t, (544, 256, 128))
    widths = [p.shape[1] for p in pieces]

    def body(*refs):
        x_ref = refs[0]
        d_refs = refs[1:1 + len(pieces)]
        o_refs = refs[1 + len(pieces):]

        @pl.when(pl.program_id(0) == 0)
        def _():
            for o_ref in o_refs:
                o_ref[...] = jnp.zeros_like(o_ref)

        xb = x_ref[...].astype(BF16)
        for d_ref, o_ref in zip(d_refs, o_refs):
            o_ref[...] += _dot_tn(xb, d_ref[...].astype(BF16))

    return pl.pallas_call(
        body, name=name, grid=(t // tt,),
        in_specs=[pl.BlockSpec((tt, k), lambda i: (i, 0))] + [pl.BlockSpec((tt, wd), lambda i: (i, 0)) for wd in widths],
        out_specs=[pl.BlockSpec((k, wd), lambda i: (0, 0)) for wd in widths],
        out_shape=[jax.ShapeDtypeStruct((k, wd), F32) for wd in widths],
        compiler_params=_params(("arbitrary",), 56),
    )(x, *pieces)


def _norm_mm(h, g, w, name):
    t, d = h.shape
    n = w.shape[1]
    tr = _row_tile(t, (544, 256, 128))

    def body(h_ref, g_ref, w_ref, o_ref, hn_ref):
        x = h_ref[...]
        r = lax.rsqrt(jnp.mean(x * x, axis=1, keepdims=True) + NORM_EPS)
        hn = (x * r * g_ref[...]).astype(BF16)
        hn_ref[...] = hn
        o_ref[...] = _dot(hn, w_ref[...]).astype(o_ref.dtype)

    row = lambda width: pl.BlockSpec((tr, width), lambda i: (i, 0))
    return pl.pallas_call(
        body, name=name, grid=(t // tr,),
        in_specs=[row(d), pl.BlockSpec((1, d), lambda i: (0, 0)), pl.BlockSpec(w.shape, lambda i: (0, 0))],
        out_specs=[row(n), row(d)],
        out_shape=[jax.ShapeDtypeStruct((t, n), BF16), jax.ShapeDtypeStruct((t, d), BF16)],
        compiler_params=_params(("parallel",), 48),
    )(h, g, w)


def _mm_norm_bwd(pieces, w, h, g, dres, name):
    t, d = h.shape
    tr = _row_tile(t, (544, 256, 128))
    widths = [p.shape[1] for p in pieces]
    offs = [sum(widths[:i]) for i in range(len(widths))]

    def body(*refs):
        p_refs = refs[:len(pieces)]
        w_ref, h_ref, g_ref, dres_ref, dh_ref, dg_ref = refs[len(pieces):]

        @pl.when(pl.program_id(0) == 0)
        def _():
            dg_ref[...] = jnp.zeros_like(dg_ref)

        dy = None
        for p_ref, off, wd in zip(p_refs, offs, widths):
            part = _dot_nt(p_ref[...].astype(BF16), w_ref[:, off:off + wd])
            dy = part if dy is None else dy + part
        x = h_ref[...]
        r = lax.rsqrt(jnp.mean(x * x, axis=1, keepdims=True) + NORM_EPS)
        nx = x * r
        dn = dy * g_ref[...]
        dh_ref[...] = dres_ref[...] + r * (dn - nx * jnp.mean(dn * nx, axis=1, keepdims=True))
        dg_ref[...] += jnp.sum(dy * nx, axis=0, keepdims=True)

    row = lambda width: pl.BlockSpec((tr, width), lambda i: (i, 0))
    vec = pl.BlockSpec((1, d), lambda i: (0, 0))
    return pl.pallas_call(
        body, name=name, grid=(t // tr,),
        in_specs=[row(wd) for wd in widths] + [pl.BlockSpec(w.shape, lambda i: (0, 0)), row(d), vec, row(d)],
        out_specs=[row(d), vec],
        out_shape=[jax.ShapeDtypeStruct((t, d), F32), jax.ShapeDtypeStruct((1, d), F32)],
        compiler_params=_params(("arbitrary",), 56),
    )(*pieces, w, h, g, dres)


def _final(h2, tgt, g, nbatch, nb):
    t, d = h2.shape

    def body(h_ref, t_ref, g_ref, dh_ref, loss_ref, dg_ref):
        b = pl.program_id(0)
        i = pl.program_id(1)

        @pl.when((b == 0) & (i == 0))
        def _():
            loss_ref[...] = jnp.zeros_like(loss_ref)
            dg_ref[...] = jnp.zeros_like(dg_ref)

        x = h_ref[...]
        r = lax.rsqrt(jnp.mean(x * x, axis=1, keepdims=True) + NORM_EPS)
        nx = x * r
        gg = g_ref[...]
        live = jnp.where(i >= 1, 1.0, 0.0)
        err = (nx * gg - t_ref[...]) * live
        loss_ref[...] += jnp.sum(err * err, axis=0, keepdims=True)
        dy = err * (1.0 / d)
        dn = dy * gg
        dh_ref[...] = r * (dn - nx * jnp.mean(dn * nx, axis=1, keepdims=True))
        dg_ref[...] += jnp.sum(dy * nx, axis=0, keepdims=True)

    vec = pl.BlockSpec((1, d), lambda b, i: (0, 0))
    return pl.pallas_call(
        body, name="final_loss", grid=(nbatch, nb),
        in_specs=[pl.BlockSpec((BLK, d), lambda b, i: (b * nb + i, 0)),
                  pl.BlockSpec((BLK, d), lambda b, i: (b * (nb - 1) + jnp.maximum(i - 1, 0), 0)),
                  vec],
        out_specs=[pl.BlockSpec((BLK, d), lambda b, i: (b * nb + i, 0)), vec, vec],
        out_shape=[jax.ShapeDtypeStruct((t, d), F32), jax.ShapeDtypeStruct((1, d), F32),
                   jax.ShapeDtypeStruct((1, d), F32)],
        compiler_params=_params(("arbitrary", "arbitrary")),
    )(h2, tgt, g)


def _direct_exchange(srcs, dsts, send_sems, recv_sems, local_sems, chunked):
    x, y, c = lax.axis_index("x"), lax.axis_index("y"), lax.axis_index("c")
    me = 4 * x + 2 * y + c
    arrays = range(len(srcs))
    local = [pltpu.make_async_copy(srcs[a].at[me] if chunked else srcs[a], dsts[a].at[me], local_sems.at[a])
             for a in arrays]
    sends, recvs = [], []
    for d in range(1, N_DEV):
        px = x + ((d >> 2) & 1) - 2 * x * ((d >> 2) & 1)
        py = y + ((d >> 1) & 1) - 2 * y * ((d >> 1) & 1)
        pc = c + (d & 1) - 2 * c * (d & 1)
        pid = 4 * px + 2 * py + pc
        for a in arrays:
            kw = dict(send_sem=send_sems.at[d - 1, a], recv_sem=recv_sems.at[d - 1, a],
                      device_id=(px, py, pc), device_id_type=pl.DeviceIdType.MESH)
            src = srcs[a].at[pid] if chunked else srcs[a]
            sends.append(pltpu.make_async_remote_copy(src_ref=src, dst_ref=dsts[a].at[me], **kw))
            recvs.append(pltpu.make_async_remote_copy(src_ref=src, dst_ref=dsts[a].at[pid], **kw))
    return local, sends, recvs


def _exchange_beside(first, last, srcs, dsts, sems, chunked):
    local, sends, recvs = _direct_exchange(srcs, dsts, *sems, chunked)

    @pl.when(first)
    def _():
        for cp in local + sends:
            cp.start()

    @pl.when(last)
    def _():
        for cp in local:
            cp.wait()
        for cp in recvs:
            cp.wait_recv()
        for cp in sends:
            cp.wait_send()


def _exchange_sems(n):
    return [pltpu.SemaphoreType.DMA((7, n)), pltpu.SemaphoreType.DMA((7, n)), pltpu.SemaphoreType.DMA((n,))]


def _sb_logits(z):
    log_beta = jnp.minimum(z, 0.0) - jnp.log(1.0 + jnp.exp(-jnp.abs(z)))
    return log_beta, log_beta - z


def _tri(width, after):
    j = lax.broadcasted_iota(jnp.int32, (width, width), 0)
    s = lax.broadcasted_iota(jnp.int32, (width, width), 1)
    return (j > s) if after else (j < s)


def _tri_ones(tri):
    return jnp.concatenate([tri.astype(BF16), jnp.ones((tri.shape[0], BLK), BF16)], axis=1)


def _block_sums(x, tri_ones, after):
    xb = x.astype(BF16)
    subs = [_dot(xb[:, s:s + BLK], tri_ones) for s in range(0, x.shape[1], BLK)]
    if len(subs) == 1:
        return subs[0][:, :BLK], subs[0][:, BLK:]
    first, second = subs
    total = first[:, BLK:] + second[:, BLK:]
    if after:
        return jnp.concatenate([first[:, :BLK] + second[:, BLK:], second[:, :BLK]], axis=1), total
    return jnp.concatenate([first[:, :BLK], second[:, :BLK] + first[:, BLK:]], axis=1), total


def _head_masked(x, lane, scale=None):
    out = []
    for h in range(8):
        xp = x[:, (h // 2) * BLK:(h // 2 + 1) * BLK]
        xm = jnp.where((lane >= HALF) if h % 2 else (lane < HALF), xp, jnp.zeros_like(xp))
        out.append(xm if scale is None else xm * scale)
    return jnp.concatenate(out, axis=1)


def _sb_fwd(proj, shards, nbatch, nb):
    lp = nb * BLK
    t = nbatch * lp

    ns = len(shards)

    def body(*refs):
        q_ref, k_ref, v_ref, g_ref = refs[:4]
        shard_refs = refs[4:4 + ns]
        o_ref, og_ref, cm_ref = refs[4 + ns:7 + ns]
        gathered_refs = refs[7 + ns:7 + 2 * ns]
        c_scr, qm_scr = refs[7 + 2 * ns:9 + 2 * ns]
        b = pl.program_id(0)
        i = pl.program_id(1)
        _exchange_beside((b == 0) & (i == 0), (b == nbatch - 1) & (i == nb - 1),
                         shard_refs, gathered_refs, refs[9 + 2 * ns:], chunked=False)
        _, lane = _iotas()
        lo_m = lane < HALF
        cm_ref[...] = jnp.zeros_like(cm_ref)
        c_scr[...] = jnp.zeros_like(c_scr)
        o_ref[...] = jnp.zeros_like(o_ref)
        qm_scr[...] = _head_masked(q_ref[...], lane, SB_SCALE)

        def block(off, width, edge):
            mask = _key_mask(i, off, width, strict=True) if edge else None
            upper = _tri_ones(_tri(BLK, after=True))
            onehot = lane == off // WIDE
            heads = range(8)
            hcs = [slice(h * BLK, (h + 1) * BLK) for h in heads]
            kbs = [k_ref[pl.ds(off, width), hc] for hc in hcs[:4]]
            vbs = [v_ref[pl.ds(off, width), hc] for hc in hcs[:4]]
            zs = [_dot_nt(qm_scr[:, hcs[h]], kbs[h // 2]) for h in heads]
            lbs, l1s = [], []
            for h in heads:
                log_beta, log_1m = _sb_logits(zs[h])
                lbs.append(log_beta)
                l1s.append(_sel(mask, log_1m, 0.0))
            css = [_block_sums(l1s[h], upper, after=True) for h in heads]
            avs = []
            for h in heads:
                c = c_scr[h]
                avs.append(_sel(mask, jnp.exp(lbs[h] + css[h][0] + _widen(c, width)), 0.0).astype(BF16))
                if width == WIDE:
                    cm_ref[:, hcs[h]] = jnp.where(onehot, c, cm_ref[:, hcs[h]])
                c_scr[h] = c + css[h][1]
            accs = [_dot(avs[h], vbs[h // 2]) for h in heads]
            for p in range(4):
                o_ref[:, hcs[p]] += jnp.where(lo_m, accs[2 * p], accs[2 * p + 1])

        _over_key_blocks(i + 1, block, reverse=True)
        g = g_ref[...].astype(F32)
        og_ref[...] = (o_ref[...] * g * _sigmoid(g)).astype(og_ref.dtype)

    tile = lambda col: pl.BlockSpec((BLK, 512), lambda b, i: (b * nb + i, col))
    full = lambda col: pl.BlockSpec((lp, 512), lambda b, i: (b, col))
    hbm = pl.BlockSpec(memory_space=pl.ANY)
    outs = pl.pallas_call(
        body, name="sb_fwd", grid=(nbatch, nb),
        in_specs=[tile(0), full(1), full(2), tile(3)] + [hbm] * ns,
        out_specs=[tile(0), tile(0), pl.BlockSpec((BLK, 1024), lambda b, i: (b * nb + i, 0))] + [hbm] * ns,
        out_shape=[jax.ShapeDtypeStruct((t, 512), F32), jax.ShapeDtypeStruct((t, 512), BF16),
                   jax.ShapeDtypeStruct((t, 1024), F32)]
        + [jax.ShapeDtypeStruct((N_DEV,) + s.shape, s.dtype) for s in shards],
        scratch_shapes=[pltpu.VMEM((8, BLK, BLK), F32), pltpu.VMEM((BLK, 1024), BF16)] + _exchange_sems(ns),
        compiler_params=_params(("arbitrary", "arbitrary"), 48),
    )(proj, proj, proj, proj, *shards)
    return outs[0], outs[1], outs[2], list(outs[3:])


def _sb_bwd(proj, o, dog, cm, chunked, nbatch, nb):
    lp = nb * BLK
    t = nbatch * lp
    ns = len(chunked)

    def body(*refs):
        q_ref, k_ref, v_ref, g_ref, o_ref, dog_ref, cm_ref = refs[:7]
        chunk_refs = refs[7:7 + ns]
        dq_ref, dk_ref, dv_ref, dg_ref = refs[7 + ns:11 + ns]
        received_refs = refs[11 + ns:11 + 2 * ns]
        c_scr, qm_scr, dom_scr, dq_scr = refs[11 + 2 * ns:15 + 2 * ns]
        b = pl.program_id(0)
        i = pl.program_id(1)
        _exchange_beside((b == 0) & (i == 0), (b == nbatch - 1) & (i == nb - 1),
                         chunk_refs, received_refs, refs[15 + 2 * ns:], chunked=True)

        @pl.when(i == 0)
        def _():
            dk_ref[...] = jnp.zeros_like(dk_ref)
            dv_ref[...] = jnp.zeros_like(dv_ref)

        _, lane = _iotas()
        lo_m = lane < HALF
        g = g_ref[...].astype(F32)
        sig = _sigmoid(g)
        dog_v = dog_ref[...]
        dg_ref[...] = (dog_v * o_ref[...] * (sig * (1.0 + g * (1.0 - sig)))).astype(dg_ref.dtype)
        dom_scr[...] = _head_masked((dog_v * g * sig).astype(BF16), lane)
        qm_scr[...] = _head_masked(q_ref[...], lane, SB_SCALE)
        c_scr[...] = jnp.zeros_like(c_scr)
        dq_scr[...] = jnp.zeros_like(dq_scr)

        def block(off, width, edge):
            mask = _key_mask(i, off, width, strict=True) if edge else None
            upper = _tri_ones(_tri(BLK, after=True))
            lower = _tri_ones(_tri(BLK, after=False))
            onehot = lane == off // WIDE
            hcs = [slice(h * BLK, (h + 1) * BLK) for h in range(8)]
            kbs = [k_ref[pl.ds(off, width), hc] for hc in hcs[:4]]
            vbs = [v_ref[pl.ds(off, width), hc] for hc in hcs[:4]]
            for heads in HEAD_GROUPS:
                zs = {h: _dot_nt(qm_scr[:, hcs[h]], kbs[h // 2]) for h in heads}
                dps = {h: _dot_nt(dom_scr[:, hcs[h]], vbs[h // 2]) for h in heads}
                lbs, l1s = {}, {}
                for h in heads:
                    lbs[h], l1s[h] = _sb_logits(zs[h])
                sufs = {h: _block_sums(_sel(mask, l1s[h], 0.0), upper, after=True)[0] for h in heads}
                prs, dzzs = {}, {}
                for h in heads:
                    expo = lbs[h] + sufs[h]
                    if width == WIDE:
                        expo = expo + jnp.sum(jnp.where(onehot, cm_ref[:, hcs[h]], 0.0), axis=1, keepdims=True)
                    pr = _sel(mask, jnp.exp(expo), 0.0)
                    dzzs[h] = pr * dps[h]
                    prs[h] = pr.astype(BF16)
                css = {h: _block_sums(dzzs[h], lower, after=False) for h in heads}
                dzbs = {}
                for h in heads:
                    c2 = c_scr[h]
                    prefix = css[h][0] + _widen(c2, width)
                    dz = _sel(mask, dzzs[h] * jnp.exp(l1s[h]) - jnp.exp(lbs[h]) * prefix, 0.0)
                    dzbs[h] = dz.astype(BF16)
                    c_scr[h] = c2 + css[h][1]
                dqs = {h: _dot(dzbs[h], kbs[h // 2]) for h in heads}
                dks = {h: _dot_tn(dzbs[h], qm_scr[:, hcs[h]]) for h in heads}
                dvs = {h: _dot_tn(prs[h], dom_scr[:, hcs[h]]) for h in heads}
                for p in sorted({h // 2 for h in heads}):
                    dq_scr[:, hcs[p]] += jnp.where(lo_m, dqs[2 * p], dqs[2 * p + 1])
                    dk_ref[pl.ds(off, width), hcs[p]] += dks[2 * p] + dks[2 * p + 1]
                    dv_ref[pl.ds(off, width), hcs[p]] += dvs[2 * p] + dvs[2 * p + 1]

        _over_key_blocks(i + 1, block, reverse=False)
        dq_ref[...] = (dq_scr[...] * SB_SCALE).astype(dq_ref.dtype)

    tile = lambda col: pl.BlockSpec((BLK, 512), lambda b, i: (b * nb + i, col))
    full = lambda col: pl.BlockSpec((lp, 512), lambda b, i: (b, col))
    acc = jax.ShapeDtypeStruct((t, 512), F32)
    once = jax.ShapeDtypeStruct((t, 512), BF16)
    hbm = pl.BlockSpec(memory_space=pl.ANY)
    outs = pl.pallas_call(
        body, name="sb_bwd", grid=(nbatch, nb),
        in_specs=[tile(0), full(1), full(2), tile(3), tile(0), tile(0),
                  pl.BlockSpec((BLK, 1024), lambda b, i: (b * nb + i, 0))] + [hbm] * ns,
        out_specs=[tile(0), full(0), full(0), tile(0)] + [hbm] * ns,
        out_shape=[once, acc, acc, once] + [jax.ShapeDtypeStruct(a.shape, a.dtype) for a in chunked],
        scratch_shapes=[pltpu.VMEM((8, BLK, BLK), F32), pltpu.VMEM((BLK, 1024), BF16),
                        pltpu.VMEM((BLK, 1024), BF16), pltpu.VMEM((BLK, 512), F32)] + _exchange_sems(ns),
        compiler_params=_params(("arbitrary", "arbitrary"), 56),
    )(proj, proj, proj, proj, o, dog, cm, *chunked)
    return outs[0], outs[1], outs[2], outs[3], list(outs[4:])


def _mla_prep(proj, gq, gkv, tabs, nbatch, nb):
    t = proj.shape[0]

    def body(mid_ref, gq_ref, gkv_ref, c_ref, s1_ref, s2_ref, cq_ref, ckv_ref, kr_ref):
        cq = mid_ref[:, 0:256].astype(F32)
        r = lax.rsqrt(jnp.mean(cq * cq, axis=1, keepdims=True) + NORM_EPS)
        cq_ref[...] = (cq * r * gq_ref[...]).astype(BF16)
        ckv = mid_ref[:, 256:384].astype(F32)
        r = lax.rsqrt(jnp.mean(ckv * ckv, axis=1, keepdims=True) + NORM_EPS)
        ckv_ref[...] = (ckv * r * gkv_ref[...]).astype(BF16)
        kr = mid_ref[:, 384:512].astype(F32)
        kr_ref[...] = _rope(kr, c_ref[...], s1_ref[...], s2_ref[...]).astype(BF16)

    tr = _row_tile(nb * BLK, (544, BLK))
    nt = nb * BLK // tr
    tab = pl.BlockSpec((tr, BLK), lambda b, i: (i, 0))
    rowspec = lambda w: pl.BlockSpec((tr, w), lambda b, i: (b * nt + i, 0))
    return pl.pallas_call(
        body, name="mla_prep", grid=(nbatch, nt),
        in_specs=[pl.BlockSpec((tr, 512), lambda b, i: (b * nt + i, EV_MID_BLK)),
                  pl.BlockSpec((1, 256), lambda b, i: (0, 0)), pl.BlockSpec((1, 128), lambda b, i: (0, 0)),
                  tab, tab, tab],
        out_specs=[rowspec(256), rowspec(128), rowspec(128)],
        out_shape=[jax.ShapeDtypeStruct((t, 256), BF16), jax.ShapeDtypeStruct((t, 128), BF16),
                   jax.ShapeDtypeStruct((t, 128), BF16)],
        compiler_params=_params(("parallel", "parallel")),
    )(proj, gq, gkv, *tabs)


def _mla_prep_bwd(proj, gq, gkv, tabs, dcqn, dckvn, dkrot, nbatch, nb):
    t = proj.shape[0]

    def body(mid_ref, gq_ref, gkv_ref, c_ref, s1_ref, s2_ref, dcq_ref, dckv_ref, dkr_ref,
             dmid_ref, dgq_ref, dgkv_ref):
        @pl.when((pl.program_id(0) == 0) & (pl.program_id(1) == 0))
        def _():
            dgq_ref[...] = jnp.zeros_like(dgq_ref)
            dgkv_ref[...] = jnp.zeros_like(dgkv_ref)

        def norm_bwd(x, gain, dy, dgain_ref):
            r = lax.rsqrt(jnp.mean(x * x, axis=1, keepdims=True) + NORM_EPS)
            nx = x * r
            dn = dy * gain
            dgain_ref[...] += jnp.sum(dy * nx, axis=0, keepdims=True)
            return r * (dn - nx * jnp.mean(dn * nx, axis=1, keepdims=True))

        dmid_ref[:, 0:256] = norm_bwd(
            mid_ref[:, 0:256].astype(F32), gq_ref[...], dcq_ref[...], dgq_ref).astype(BF16)
        dmid_ref[:, 256:384] = norm_bwd(
            mid_ref[:, 256:384].astype(F32), gkv_ref[...], dckv_ref[...], dgkv_ref).astype(BF16)
        dmid_ref[:, 384:512] = _rope_t(dkr_ref[...], c_ref[...], s1_ref[...], s2_ref[...]).astype(BF16)

    tr = _row_tile(nb * BLK, (544, BLK))
    nt = nb * BLK // tr
    tab = pl.BlockSpec((tr, BLK), lambda b, i: (i, 0))
    rowspec = lambda w: pl.BlockSpec((tr, w), lambda b, i: (b * nt + i, 0))
    vq = pl.BlockSpec((1, 256), lambda b, i: (0, 0))
    vkv = pl.BlockSpec((1, 128), lambda b, i: (0, 0))
    return pl.pallas_call(
        body, name="mla_prep_bwd", grid=(nbatch, nt),
        in_specs=[pl.BlockSpec((tr, 512), lambda b, i: (b * nt + i, EV_MID_BLK)), vq, vkv, tab, tab, tab,
                  rowspec(256), rowspec(128), rowspec(128)],
        out_specs=[rowspec(512), vq, vkv],
        out_shape=[jax.ShapeDtypeStruct((t, 512), BF16), jax.ShapeDtypeStruct((1, 256), F32),
                   jax.ShapeDtypeStruct((1, 128), F32)],
        compiler_params=_params(("arbitrary", "arbitrary")),
    )(proj, gq, gkv, *tabs, dcqn, dckvn, dkrot)


def _mla_scores(qf, kvb, krb, mask, lo_m):
    kf = jnp.where(lo_m, kvb, krb)
    return kf, _sel(mask, _dot_nt(qf, kf), NEG)


def _mla_fwd(qh, kvh, krot, proj, tabs, nbatch, nb):
    lp = nb * BLK
    t = nbatch * lp

    def body(q_ref, kv_ref, kr_ref, g_ref, c_ref, s1_ref, s2_ref, o_ref, og_ref, lse_ref,
             qf_scr, m_scr, l_scr, acc_scr):
        i = pl.program_id(1)
        row, lane = _iotas()
        lo_m = lane < HALF
        for h in range(8):
            hc = slice(h * BLK, (h + 1) * BLK)
            qf_scr[:, hc] = (_rope(q_ref[:, hc].astype(F32), c_ref[...], s1_ref[...], s2_ref[...])
                             * MLA_SCALE).astype(BF16)
        m_scr[...] = jnp.full(m_scr.shape, NEG, F32)
        l_scr[...] = jnp.zeros_like(l_scr)
        acc_scr[...] = jnp.zeros_like(acc_scr)

        def block(off, width, edge):
            mask = _key_mask(i, off, width, strict=False) if edge else None
            lo_k = lax.broadcasted_iota(jnp.int32, (width, BLK), 1) < HALF
            ones = jnp.ones((width, BLK), BF16)
            krb = kr_ref[pl.ds(off, width), :]
            heads = range(8)
            hcs = [slice(h * BLK, (h + 1) * BLK) for h in heads]
            kvbs = [kv_ref[pl.ds(off, width), hc] for hc in hcs]
            ss = [_mla_scores(qf_scr[:, hcs[h]], kvbs[h], krb, mask, lo_k)[1] for h in heads]
            ps, alphas = [], []
            for h in heads:
                m = m_scr[h]
                m2 = jnp.maximum(m, jnp.max(ss[h], axis=1, keepdims=True))
                ps.append(jnp.exp(ss[h] - _widen(m2, width)).astype(BF16))
                alphas.append(jnp.exp(m - m2))
                m_scr[h] = m2
            pvs = [_dot(ps[h], jnp.concatenate([kvbs[h], ones], axis=1)) for h in heads]
            for h in heads:
                l_scr[h] = alphas[h] * l_scr[h] + pvs[h][:, BLK:]
                acc_scr[h] = alphas[h] * acc_scr[h] + pvs[h][:, :BLK]

        _over_key_blocks(i + 1, block, reverse=False)
        lse = jnp.zeros((BLK, BLK), F32)
        for p in range(4):
            pc = slice(p * BLK, (p + 1) * BLK)
            o0 = acc_scr[2 * p] / l_scr[2 * p]
            o1 = acc_scr[2 * p + 1] / l_scr[2 * p + 1]
            o_ref[:, pc] = jnp.where(lo_m, pltpu.roll(o0, HALF, 1), o1)
            for h in (2 * p, 2 * p + 1):
                lse = lse + jnp.where(lane == h, m_scr[h] + jnp.log(l_scr[h]), 0.0)
        lse_ref[...] = lse
        g = g_ref[...].astype(F32)
        og_ref[...] = (o_ref[...] * g * _sigmoid(g)).astype(og_ref.dtype)

    tile = pl.BlockSpec((BLK, 512), lambda b, i: (b * nb + i, 0))
    tab = pl.BlockSpec((BLK, BLK), lambda b, i: (i, 0))
    heads = pltpu.VMEM((8, BLK, BLK), F32)
    return pl.pallas_call(
        body, name="mla_fwd", grid=(nbatch, nb),
        in_specs=[pl.BlockSpec((BLK, 1024), lambda b, i: (b * nb + i, 0)),
                  pl.BlockSpec((lp, 1024), lambda b, i: (b, 0)),
                  pl.BlockSpec((lp, BLK), lambda b, i: (b, 0)),
                  pl.BlockSpec((BLK, 512), lambda b, i: (b * nb + i, EV_GMLA_BLK // 4)),
                  tab, tab, tab],
        out_specs=[tile, tile, pl.BlockSpec((BLK, BLK), lambda b, i: (b * nb + i, 0))],
        out_shape=[jax.ShapeDtypeStruct((t, 512), F32), jax.ShapeDtypeStruct((t, 512), BF16),
                   jax.ShapeDtypeStruct((t, BLK), F32)],
        scratch_shapes=[pltpu.VMEM((BLK, 1024), BF16), heads, heads, heads],
        compiler_params=_params(("parallel", "arbitrary"), 48),
    )(qh, kvh, krot, proj, *tabs)


def _mla_bwd(qh, kvh, krot, proj, tabs, o, dog, lse, nbatch, nb):
    lp = nb * BLK
    t = nbatch * lp

    def body(q_ref, kv_ref, kr_ref, g_ref, c_ref, s1_ref, s2_ref, o_ref, dog_ref, lse_ref,
             dq_ref, dkv_ref, dkr_ref, dg_ref, qf_scr, do_scr, stat_scr, acc_scr):
        i = pl.program_id(1)

        @pl.when(i == 0)
        def _():
            dkv_ref[...] = jnp.zeros_like(dkv_ref)
            dkr_ref[...] = jnp.zeros_like(dkr_ref)

        row, lane = _iotas()
        lo_m = lane < HALF
        g = g_ref[...].astype(F32)
        sig = _sigmoid(g)
        dog_v = dog_ref[...]
        o_v = o_ref[...]
        dg_ref[...] = (dog_v * o_v * (sig * (1.0 + g * (1.0 - sig)))).astype(dg_ref.dtype)
        do = dog_v * g * sig
        do_o = do * o_v
        lse_blk = lse_ref[...]
        zero = jnp.zeros((BLK, BLK), F32)
        for h in range(8):
            hc = slice(h * BLK, (h + 1) * BLK)
            pc = slice((h // 2) * BLK, (h // 2 + 1) * BLK)
            qf_scr[:, hc] = (_rope(q_ref[:, hc].astype(F32), c_ref[...], s1_ref[...], s2_ref[...])
                             * MLA_SCALE).astype(BF16)
            dop = do[:, pc]
            do_src = dop if h % 2 else pltpu.roll(dop, HALF, 1)
            do_scr[:, hc] = jnp.where(lo_m, 0.0, do_src).astype(BF16)
            hm = (lane >= HALF) if h % 2 else lo_m
            stat_scr[h] = zero + jnp.sum(jnp.where(hm, do_o[:, pc], 0.0), axis=1, keepdims=True)
            stat_scr[8 + h] = zero + jnp.sum(jnp.where(lane == h, lse_blk, 0.0), axis=1, keepdims=True)
        acc_scr[...] = jnp.zeros_like(acc_scr)

        def block(off, width, edge):
            mask = _key_mask(i, off, width, strict=False) if edge else None
            lo_k = lax.broadcasted_iota(jnp.int32, (width, BLK), 1) < HALF
            krb = kr_ref[pl.ds(off, width), :]
            heads = range(8)
            hcs = [slice(h * BLK, (h + 1) * BLK) for h in heads]
            kvbs = [kv_ref[pl.ds(off, width), hc] for hc in hcs]
            qfs = [qf_scr[:, hc] for hc in hcs]
            dos = [do_scr[:, hc] for hc in hcs]
            scored = [_mla_scores(qfs[h], kvbs[h], krb, mask, lo_k) for h in heads]
            dps = [_dot_nt(dos[h], kvbs[h]) for h in heads]
            pbs, dss = [], []
            for h in heads:
                p = jnp.exp(scored[h][1] - _widen(stat_scr[8 + h], width))
                pbs.append(p.astype(BF16))
                dss.append((p * (dps[h] - _widen(stat_scr[h], width))).astype(BF16))
            dqs = [_dot(dss[h], scored[h][0]) for h in heads]
            dkfs = [_dot_tn(dss[h], qfs[h]) for h in heads]
            dvvs = [_dot_tn(pbs[h], dos[h]) for h in heads]
            dkr = jnp.zeros((width, BLK), F32)
            for h in heads:
                acc_scr[h] += dqs[h]
                dkv_ref[pl.ds(off, width), hcs[h]] += jnp.where(lo_k, dkfs[h], 0.0) + dvvs[h]
                dkr = dkr + jnp.where(lo_k, 0.0, dkfs[h])
            dkr_ref[pl.ds(off, width), :] += dkr

        _over_key_blocks(i + 1, block, reverse=False)
        for h in range(8):
            hc = slice(h * BLK, (h + 1) * BLK)
            dq_ref[:, hc] = _rope_t(acc_scr[h] * MLA_SCALE, c_ref[...], s1_ref[...], s2_ref[...]).astype(dq_ref.dtype)

    tile = lambda col: pl.BlockSpec((BLK, 512), lambda b, i: (b * nb + i, col))
    wide = pl.BlockSpec((BLK, 1024), lambda b, i: (b * nb + i, 0))
    full8 = pl.BlockSpec((lp, 1024), lambda b, i: (b, 0))
    full1 = pl.BlockSpec((lp, BLK), lambda b, i: (b, 0))
    tab = pl.BlockSpec((BLK, BLK), lambda b, i: (i, 0))
    return pl.pallas_call(
        body, name="mla_bwd", grid=(nbatch, nb),
        in_specs=[wide, full8, full1, tile(EV_GMLA_BLK // 4), tab, tab, tab, tile(0), tile(1),
                  pl.BlockSpec((BLK, BLK), lambda b, i: (b * nb + i, 0))],
        out_specs=[wide, full8, full1, tile(0)],
        out_shape=[jax.ShapeDtypeStruct((t, 1024), BF16), jax.ShapeDtypeStruct((t, 1024), F32),
                   jax.ShapeDtypeStruct((t, 128), F32), jax.ShapeDtypeStruct((t, 512), BF16)],
        scratch_shapes=[pltpu.VMEM((BLK, 1024), BF16), pltpu.VMEM((BLK, 1024), BF16),
                        pltpu.VMEM((16, BLK, BLK), F32), pltpu.VMEM((8, BLK, BLK), F32)],
        compiler_params=_params(("parallel", "arbitrary"), 56),
    )(qh, kvh, krot, proj, *tabs, o, dog, lse)


def _swa_setup(kk, i, k_refs, v_refs):
    row, lane = _iotas()
    own = (lane >= kk * HALF) & (lane < (kk + 1) * HALF)

    def dup(ref):
        x = ref[...].astype(F32)
        return jnp.where(own, x, pltpu.roll(x, HALF, 1)).astype(BF16)

    kcat = jnp.concatenate([dup(r) for r in k_refs], axis=0)
    vcat = jnp.concatenate([dup(r) for r in v_refs], axis=0)
    row2 = lax.broadcasted_iota(jnp.int32, (BLK, 2 * BLK), 0)
    lane2 = lax.broadcasted_iota(jnp.int32, (BLK, 2 * BLK), 1)
    is_meta = lane2 < BLK
    in_own = lane2 - BLK <= row2
    k_pos = jnp.where(is_meta, lane2, jnp.where(in_own, (i - 1) * BLK, (i - 2) * BLK) + lane2)
    d = i * BLK + row2 - k_pos
    mask = (d >= 0) & (k_pos >= jnp.where(is_meta, N_PAD, BLK))
    return lane, lane <= row, own, kcat, vcat, mask, d.astype(F32)


def _swa_fold(x, in_own):
    return jnp.concatenate([x[:, :BLK], jnp.where(in_own, x[:, 2 * BLK:], x[:, BLK:2 * BLK])], axis=1)


def _swa_unfold(x, in_own):
    w = x[:, BLK:]
    zero = jnp.zeros_like(w)
    return jnp.concatenate([x[:, :BLK], jnp.where(in_own, zero, w), jnp.where(in_own, w, zero)], axis=1)


def _swa_slope(kk, g_idx):
    return (2.0 ** (-(g_idx + 1) / 2.0)) * jnp.where(kk == 0, 1.0, 1.0 / 16.0)


def _swa_fwd(proj, sinks, nbatch, nb):
    lp = nb * BLK
    t = nbatch * lp

    def body(sink_ref, q_ref, ka, kb, kc, va, vb, vc, g_ref, o_ref, og_ref, lse_ref):
        kk = pl.program_id(1)
        i = pl.program_id(2)
        lane, in_own, own, kcat, vcat, mask, dist = _swa_setup(kk, i, (ka, kb, kc), (va, vb, vc))
        lo_m = lane < HALF
        heads = range(8)
        qms = []
        for h in heads:
            qp = q_ref[:, (h // 2) * BLK:(h // 2 + 1) * BLK]
            qms.append(jnp.where((lane >= HALF) if h % 2 else lo_m, qp, jnp.zeros_like(qp)) * SWA_SCALE)
        qks = [_dot_nt(qms[h], kcat) for h in heads]
        ps, ls, lses = [], [], []
        for h in heads:
            sink = sink_ref[kk, h]
            s = jnp.where(mask, _swa_fold(qks[h], in_own) - _swa_slope(kk, h) * dist, NEG)
            mx = jnp.maximum(jnp.max(s, axis=1, keepdims=True), sink)
            p = jnp.exp(s - mx)
            l = jnp.exp(sink - mx) + jnp.sum(p, axis=1, keepdims=True)
            ps.append(_swa_unfold(p.astype(BF16), in_own))
            ls.append(l)
            lses.append(mx + jnp.log(l))
        pvs = [_dot(ps[h], vcat) for h in heads]
        lse_out = jnp.zeros((BLK, BLK), F32)
        for m in range(4):
            cols = slice(m * BLK, (m + 1) * BLK)
            outp = jnp.where(lo_m, pvs[2 * m] / ls[2 * m], pvs[2 * m + 1] / ls[2 * m + 1])
            o_ref[:, cols] = outp
            g = g_ref[:, cols].astype(F32)
            og_ref[:, cols] = (outp * g * _sigmoid(g)).astype(og_ref.dtype)
            for h in (2 * m, 2 * m + 1):
                lse_out = lse_out + jnp.where(lane == h, lses[h], 0.0)
        lse_ref[...] = lse_out

    def kvspec(col, which):
        if which == 0:
            return pl.BlockSpec((BLK, BLK), lambda b, kk, i: (b * nb, col))
        if which == 1:
            return pl.BlockSpec((BLK, BLK), lambda b, kk, i: (b * nb + jnp.maximum(i - 1, 0), col))
        return pl.BlockSpec((BLK, BLK), lambda b, kk, i: (b * nb + i, col))

    wide = lambda c0: pl.BlockSpec((BLK, 512), lambda b, kk, i: (b * nb + i, c0 + kk))
    return pl.pallas_call(
        body, name="swa_fwd", grid=(nbatch, 2, nb),
        in_specs=[pl.BlockSpec(memory_space=pltpu.SMEM), wide(0),
                  kvspec(OD_K_BLK, 0), kvspec(OD_K_BLK, 1), kvspec(OD_K_BLK, 2),
                  kvspec(OD_V_BLK, 0), kvspec(OD_V_BLK, 1), kvspec(OD_V_BLK, 2), wide(2)],
        out_specs=[wide(0), wide(0), pl.BlockSpec((BLK, BLK), lambda b, kk, i: (b * nb + i, kk))],
        out_shape=[jax.ShapeDtypeStruct((t, 1024), F32), jax.ShapeDtypeStruct((t, 1024), BF16),
                   jax.ShapeDtypeStruct((t, 256), F32)],
        compiler_params=_params(("parallel", "parallel", "arbitrary")),
    )(sinks, proj, proj, proj, proj, proj, proj, proj, proj)


def _swa_bwd(proj, sinks, o, dog, lse, nbatch, nb):
    lp = nb * BLK
    t = nbatch * lp

    def body(sink_ref, q_ref, ka, kb, kc, va, vb, vc, g_ref, o_ref, dog_ref, lse_ref,
             dq_ref, dg_ref, dk_ref, dv_ref, dsink_ref):
        kk = pl.program_id(1)
        i = pl.program_id(2)

        @pl.when((kk == 0) & (i == 0))
        def _():
            dk_ref[...] = jnp.zeros_like(dk_ref)
            dv_ref[...] = jnp.zeros_like(dv_ref)

        @pl.when(i == 0)
        def _():
            dsink_ref[...] = jnp.zeros_like(dsink_ref)

        lane, in_own, own, kcat, vcat, mask, dist = _swa_setup(kk, i, (ka, kb, kc), (va, vb, vc))
        lo_m = lane < HALF
        row8 = lax.broadcasted_iota(jnp.int32, (8, BLK), 0)
        lse_blk = lse_ref[...]
        heads = range(8)
        qms, doms, deltas, lse_hs = [], [], [], []
        for m in range(4):
            cols = slice(m * BLK, (m + 1) * BLK)
            qp = q_ref[:, cols]
            g = g_ref[:, cols].astype(F32)
            sig = _sigmoid(g)
            dog_v = dog_ref[:, cols]
            o_v = o_ref[:, cols]
            dg_ref[:, cols] = (dog_v * o_v * (sig * (1.0 + g * (1.0 - sig)))).astype(dg_ref.dtype)
            do = dog_v * g * sig
            do_o = do * o_v
            dob = do.astype(BF16)
            for h in (2 * m, 2 * m + 1):
                hm = (lane >= HALF) if h % 2 else lo_m
                qms.append(jnp.where(hm, qp, jnp.zeros_like(qp)) * SWA_SCALE)
                doms.append(jnp.where(hm, dob, jnp.zeros_like(dob)))
                deltas.append(jnp.sum(jnp.where(hm, do_o, 0.0), axis=1, keepdims=True))
                lse_hs.append(jnp.sum(jnp.where(lane == h, lse_blk, 0.0), axis=1, keepdims=True))
        qks = [_dot_nt(qms[h], kcat) for h in heads]
        dps = [_dot_nt(doms[h], vcat) for h in heads]
        pbs, dss = [], []
        dsink = jnp.zeros((8, BLK), F32)
        for h in heads:
            s = jnp.where(mask, _swa_fold(qks[h], in_own) - _swa_slope(kk, h) * dist, NEG)
            p = jnp.exp(s - lse_hs[h])
            pbs.append(_swa_unfold(p.astype(BF16), in_own))
            dss.append(_swa_unfold((p * (_swa_fold(dps[h], in_own) - deltas[h])).astype(BF16), in_own))
            tot = jnp.sum(-jnp.exp(sink_ref[kk, h] - lse_hs[h]) * deltas[h], axis=0, keepdims=True)
            dsink = dsink + jnp.where(row8 == h, tot, 0.0)
        dsink_ref[...] += dsink
        dqs = [_dot(dss[h], kcat) for h in heads]
        dks = [_dot_tn(dss[h], qms[h]) for h in heads]
        dvs = [_dot_tn(pbs[h], doms[h]) for h in heads]
        for m in range(4):
            dq_ref[:, m * BLK:(m + 1) * BLK] = (
                jnp.where(lo_m, dqs[2 * m], dqs[2 * m + 1]) * SWA_SCALE).astype(dq_ref.dtype)
        dk = dks[0]
        dv = dvs[0]
        for h in range(1, 8):
            dk = dk + dks[h]
            dv = dv + dvs[h]
        offs = [0, pl.multiple_of(jnp.maximum(i - 1, 0) * BLK, BLK), pl.multiple_of(i * BLK, BLK)]
        for x in range(3):
            rows = slice(x * BLK, (x + 1) * BLK)
            dkx, dvx = dk[rows], dv[rows]
            dk_ref[pl.ds(offs[x], BLK), :] += jnp.where(own, dkx + pltpu.roll(dkx, HALF, 1), 0.0)
            dv_ref[pl.ds(offs[x], BLK), :] += jnp.where(own, dvx + pltpu.roll(dvx, HALF, 1), 0.0)

    def kvspec(col, which):
        if which == 0:
            return pl.BlockSpec((BLK, BLK), lambda b, kk, i: (b * nb, col))
        if which == 1:
            return pl.BlockSpec((BLK, BLK), lambda b, kk, i: (b * nb + jnp.maximum(i - 1, 0), col))
        return pl.BlockSpec((BLK, BLK), lambda b, kk, i: (b * nb + i, col))

    wide = lambda c0: pl.BlockSpec((BLK, 512), lambda b, kk, i: (b * nb + i, c0 + kk))
    full = pl.BlockSpec((lp, BLK), lambda b, kk, i: (b, 0))
    return pl.pallas_call(
        body, name="swa_bwd", grid=(nbatch, 2, nb),
        in_specs=[pl.BlockSpec(memory_space=pltpu.SMEM), wide(0),
                  kvspec(OD_K_BLK, 0), kvspec(OD_K_BLK, 1), kvspec(OD_K_BLK, 2),
                  kvspec(OD_V_BLK, 0), kvspec(OD_V_BLK, 1), kvspec(OD_V_BLK, 2), wide(2),
                  wide(0), wide(0), pl.BlockSpec((BLK, BLK), lambda b, kk, i: (b * nb + i, kk))],
        out_specs=[wide(0), wide(0), full, full,
                   pl.BlockSpec((8, BLK), lambda b, kk, i: (b * 2 + kk, 0))],
        out_shape=[jax.ShapeDtypeStruct((t, 1024), BF16), jax.ShapeDtypeStruct((t, 1024), BF16),
                   jax.ShapeDtypeStruct((t, 128), F32), jax.ShapeDtypeStruct((t, 128), F32),
                   jax.ShapeDtypeStruct((nbatch * 16, BLK), F32)],
        compiler_params=_params(("parallel", "arbitrary", "arbitrary")),
    )(sinks, proj, proj, proj, proj, proj, proj, proj, proj, o, dog, lse)


def _rope_tables(lp):
    pos = (jnp.arange(lp) - N_PAD).astype(F32)
    inv = ROPE_BASE ** (-jnp.arange(16, dtype=F32) / 16.0)
    ang = pos[:, None] * inv[None, :]
    cos, sin = jnp.cos(ang), jnp.sin(ang)
    z16 = jnp.zeros((lp, 16), F32)
    c = jnp.concatenate([jnp.ones((lp, 64), F32), cos, cos, jnp.zeros((lp, 32), F32)], axis=1)
    s1 = jnp.concatenate([jnp.zeros((lp, 64), F32), -sin, z16, jnp.zeros((lp, 32), F32)], axis=1)
    s2 = jnp.concatenate([jnp.zeros((lp, 64), F32), z16, sin, jnp.zeros((lp, 32), F32)], axis=1)
    return c, s1, s2


def _local_step(h0, tgt, norm_g, final_g, gq, gkv, sinks, w_ie, late_shards, nbatch, nb):
    lp = nb * BLK
    tabs = _rope_tables(lp)
    g0, g1 = norm_g[0:1], norm_g[1:2]
    sinks2 = sinks.reshape(2, 8)

    proj_e, hn0 = _norm_mm(h0, g0, w_ie, "proj_even")
    o_sb, og_sb, cm, gathered = _sb_fwd(proj_e, [late_shards[name] for name in _LATE], nbatch, nb)
    full = {name: _unchunk(name, blk) for name, blk in zip(_LATE, gathered)}
    w_uq, w_ukv, w_oe = _uq_to_compute(full["ev_w_uq"]), full["ev_w_ukv"], full["ev_w_out"]
    w_io, w_oo = _od_in_to_compute(full["od_w_in"]), full["od_w_out"]
    cqn, ckvn, krot = _mla_prep(proj_e, gq, gkv, tabs, nbatch, nb)
    qh = _mm(cqn, w_uq, "nn", "mla_uq", out_dtype=BF16)
    kvh = _mm(ckvn, w_ukv, "nn", "mla_ukv", out_dtype=BF16)
    o_mla, og_mla, lse_m = _mla_fwd(qh, kvh, krot, proj_e, tabs, nbatch, nb)
    h1 = _mm([og_sb, og_mla], w_oe, "nn", "out_even", add=h0)
    proj_o, hn1 = _norm_mm(h1, g1, w_io, "proj_odd")
    o_o, og_o, lse_o = _swa_fwd(proj_o, sinks2, nbatch, nb)
    h2 = _mm(og_o, w_oo, "nn", "out_odd", add=h1)
    dh2, lossv, d_final_g = _final(h2, tgt, final_g, nbatch, nb)

    dog_o = _mm(dh2, w_oo, "nt", "d_out_odd")
    d_w_oo, = _mm_tn(og_o, [dh2], "dw_out_odd")
    dq_o, dg_o, dk_o, dv_o, dsink = _swa_bwd(proj_o, sinks2, o_o, dog_o, lse_o, nbatch, nb)
    dproj_o = [dq_o, dg_o, dk_o, dv_o]
    dh1, d_g1 = _mm_norm_bwd(dproj_o, w_io, h1, g1, dh2, "d_proj_odd")
    dw_q, dw_g, dw_k, dw_v = _mm_tn(hn1, dproj_o, "dw_proj_odd")

    dog_e = _mm(dh1, w_oe, "nt", "d_out_even")
    d_w_oe_sb, = _mm_tn(og_sb, [dh1], "dw_out_even_sb")
    d_w_oe_mla, = _mm_tn(og_mla, [dh1], "dw_out_even_mla")
    early = dict(od_w_in=jnp.concatenate([dw_q, dw_k, dw_v, dw_g], axis=1), od_w_out=d_w_oo,
                 ev_w_out=jnp.concatenate([d_w_oe_sb, d_w_oe_mla], axis=0))
    dq_sb, dk_sb, dv_sb, dg_sb, received = _sb_bwd(
        proj_e, o_sb, dog_e, cm, [_chunk(name, early[name]) for name in _EARLY_GRADS], nbatch, nb)
    dqh, dkvh, dkrot, dg_mla = _mla_bwd(qh, kvh, krot, proj_e, tabs, o_mla, dog_e, lse_m, nbatch, nb)
    dcqn = _mm(dqh, w_uq, "nt", "d_mla_uq")
    d_w_uq, = _mm_tn(cqn, [dqh], "dw_mla_uq")
    dckvn = _mm(dkvh, w_ukv, "nt", "d_mla_ukv")
    d_w_ukv, = _mm_tn(ckvn, [dkvh], "dw_mla_ukv")
    dmid, d_gq, d_gkv = _mla_prep_bwd(proj_e, gq, gkv, tabs, dcqn, dckvn, dkrot, nbatch, nb)
    dproj_e = [dq_sb, dk_sb, dv_sb, dg_sb, dmid, dg_mla]
    dh0, d_g0 = _mm_norm_bwd(dproj_e, w_ie, h0, g0, dh1, "d_proj_even")
    dw_e = _mm_tn(hn0, dproj_e, "dw_proj_even")

    d_sinks = dsink.reshape(nbatch, 2, 8, BLK)[:, :, :, 0].sum(axis=0).reshape(1, 16)
    d_norm_g = jnp.concatenate([d_g0, d_g1], axis=0)
    last = [_ev_in_chunks(dw_e), _chunk("ev_w_uq", _uq_from_compute(d_w_uq)), _chunk("ev_w_ukv", d_w_ukv)]
    return dict(lossv=lossv, dh0=dh0, norm_g=d_norm_g, final_g=d_final_g, gq=d_gq, gkv=d_gkv, sinks=d_sinks,
                last=last, received=received)


EV_SHARD = 2976 // N_DEV


def _ev_in_from_blocks(blocks):
    def cols(lo, hi):
        parts = []
        for k in range(N_DEV):
            a, b = max(lo, k * EV_SHARD), min(hi, (k + 1) * EV_SHARD)
            if a < b:
                parts.append(blocks[k][:, a - k * EV_SHARD:b - k * EV_SHARD])
        return parts

    z = lambda n: jnp.zeros((blocks.shape[1], n), blocks.dtype)
    return jnp.concatenate(cols(0, 2432) + [z(64)] + cols(2432, 2464) + [z(32)] + cols(2464, 2976), axis=1)


def _ev_in_chunks(pieces):
    segments = [(pieces[0], 0, 512), (pieces[1], 0, 512), (pieces[2], 0, 512), (pieces[3], 0, 512),
                (pieces[4], 0, 384), (pieces[4], 448, 480), (pieces[5], 0, 512)]
    chunks, start = [[] for _ in range(N_DEV)], 0
    for arr, lo, hi in segments:
        for k in range(N_DEV):
            a, b = max(start, k * EV_SHARD), min(start + hi - lo, (k + 1) * EV_SHARD)
            if a < b:
                chunks[k].append(arr[:, lo + a - start:lo + b - start].astype(BF16))
        start += hi - lo
    return jnp.stack([jnp.concatenate(c, axis=1) for c in chunks], axis=0)


def _uq_to_compute(w):
    w3 = w.reshape(256, 8, 96)
    return jnp.concatenate([w3, jnp.zeros((256, 8, 32), w.dtype)], axis=2).reshape(256, 1024)


def _uq_from_compute(w):
    return w.reshape(256, 8, 128)[:, :, :96].reshape(256, 768)


def _od_in_to_compute(w):
    return jnp.concatenate([w[:, :1024], w[:, 1280:], w[:, 1024:1280]], axis=1)


_BIG = dict(ev_w_in=(1024, 2976, 1), ev_w_uq=(256, 768, 1), ev_w_ukv=(128, 1024, 1),
            ev_w_out=(1024, 1024, 0), od_w_in=(1024, 2304, 1), od_w_out=(1024, 1024, 0))
_LATE = ("ev_w_uq", "ev_w_ukv", "ev_w_out", "od_w_in", "od_w_out")
_EARLY_GRADS = ("od_w_in", "od_w_out", "ev_w_out")
_LAST_GRADS = ("ev_w_in", "ev_w_uq", "ev_w_ukv")


def _unchunk(name, blk):
    rows, cols, axis = _BIG[name]
    return blk.transpose(1, 0, 2).reshape(rows, cols) if axis == 1 else blk.reshape(rows, cols)


def _chunk(name, g):
    rows, cols, axis = _BIG[name]
    g = g.astype(BF16)
    return g.reshape(rows, N_DEV, cols // N_DEV).transpose(1, 0, 2) if axis == 1 else g.reshape(N_DEV, rows // N_DEV, cols)


def _all_gather(shards, name):
    n = len(shards)

    def body(*refs):
        xs, outs = refs[:n], refs[n:2 * n]
        send_sems, recv_sems, local_sems = refs[2 * n:]
        x, y, c = lax.axis_index("x"), lax.axis_index("y"), lax.axis_index("c")
        me, sibling = (x, y, c), (x, y, 1 - c)
        chips = [(1 - x, y), (x, 1 - y), (1 - x, 1 - y)]
        arrays = range(n)

        def copy(k, a, block, to, from_input=False):
            px, py, pc = block
            dst = outs[a].at[4 * px + 2 * py + pc]
            return pltpu.make_async_remote_copy(
                src_ref=xs[a] if from_input else dst, dst_ref=dst,
                send_sem=send_sems.at[k, a], recv_sem=recv_sems.at[k, a],
                device_id=to, device_id_type=pl.DeviceIdType.MESH)

        mine = [pltpu.make_async_copy(xs[a], outs[a].at[4 * x + 2 * y + c], local_sems.at[a]) for a in arrays]
        for cp in mine:
            cp.start()
        first = [copy(0, a, me, sibling, True) for a in arrays]
        for j, chip in enumerate(chips):
            first += [copy(1 + j, a, me, (*chip, c), True) for a in arrays]
        for cp in first:
            cp.start()
        passed = []
        for j, chip in enumerate(chips):
            for a in arrays:
                copy(1 + j, a, (*chip, c), me).wait_recv()
                passed.append(copy(4 + j, a, (*chip, c), sibling))
                passed[-1].start()
        for a in arrays:
            copy(0, a, sibling, me).wait_recv()
        for j, chip in enumerate(chips):
            for a in arrays:
                copy(4 + j, a, (*chip, 1 - c), me).wait_recv()
        for cp in first + passed:
            cp.wait_send()
        for cp in mine:
            cp.wait()

    hbm = pl.BlockSpec(memory_space=pl.ANY)
    return pl.pallas_call(
        body, name=name,
        out_shape=[jax.ShapeDtypeStruct((N_DEV,) + s.shape, s.dtype) for s in shards],
        in_specs=[hbm] * n, out_specs=[hbm] * n,
        scratch_shapes=[pltpu.SemaphoreType.DMA((7, n)), pltpu.SemaphoreType.DMA((7, n)),
                        pltpu.SemaphoreType.DMA((n,))],
    )(*shards)


def _exchange_sum(chunked, received, name):
    n, m = len(chunked), len(received)
    arrs = list(chunked) + list(received)

    def body(*refs):
        ins, outs = refs[:n + m], refs[n + m:2 * (n + m)]
        bufs = refs[2 * (n + m):3 * (n + m)]
        send_sems, recv_sems, local_sems, load_sems = refs[3 * (n + m):]
        loads = [pltpu.make_async_copy(ins[n + a], bufs[n + a], load_sems.at[a]) for a in range(m)]
        for cp in loads:
            cp.start()
        local, sends, recvs = _direct_exchange(ins[:n], bufs[:n], send_sems, recv_sems, local_sems, chunked=True)
        for cp in local + sends:
            cp.start()
        for a, cp in enumerate(loads):
            cp.wait()
            _sum_slots(bufs[n + a], outs[n + a])
        for cp in local:
            cp.wait()
        for cp in recvs:
            cp.wait_recv()
        for cp in sends:
            cp.wait_send()
        for a in range(n):
            _sum_slots(bufs[a], outs[a])

    hbm = pl.BlockSpec(memory_space=pl.ANY)
    vm = pl.BlockSpec(memory_space=pltpu.VMEM)
    return pl.pallas_call(
        body, name=name,
        out_shape=[jax.ShapeDtypeStruct(a.shape[1:], F32) for a in arrs],
        in_specs=[hbm] * (n + m), out_specs=[vm] * (n + m),
        scratch_shapes=[pltpu.VMEM(a.shape, a.dtype) for a in arrs] + _exchange_sems(n)
        + [pltpu.SemaphoreType.DMA((max(m, 1),))],
        compiler_params=pltpu.CompilerParams(vmem_limit_bytes=48 << 20),
    )(*arrs)


def _sum_slots(buf, out):
    rows = buf.shape[1]

    def add(sl):
        acc = buf[(0,) + sl].astype(F32)
        for k in range(1, N_DEV):
            acc = acc + buf[(k,) + sl].astype(F32)
        out[sl] = acc

    if rows > BLK and rows % BLK == 0:
        def step(r, carry):
            add((pl.ds(pl.multiple_of(r * BLK, BLK), BLK), slice(None)))
            return carry

        lax.fori_loop(0, rows // BLK, step, 0)
    else:
        add((slice(None), slice(None)))


def _adamw(ws, gs, ms, vs, name, steps):
    n = len(ws)

    def body(*refs):
        ins, outs = refs[:4 * n], refs[4 * n:]
        for k in range(n):
            w_ref, g_ref, m_ref, v_ref = ins[4 * k:4 * k + 4]
            d_ref, nm_ref, nv_ref = outs[3 * k:3 * k + 3]
            g = g_ref[...]
            m = ADAM_B1 * m_ref[...] + (1.0 - ADAM_B1) * g
            v = ADAM_B2 * v_ref[...] + (1.0 - ADAM_B2) * (g * g)
            m_hat = m / (1.0 - ADAM_B1 ** ADAM_STEP)
            v_hat = v / (1.0 - ADAM_B2 ** ADAM_STEP)
            d_ref[...] = -ADAM_LR * (m_hat / (jnp.sqrt(v_hat) + ADAM_EPS) + ADAM_WD * w_ref[...])
            nm_ref[...] = m
            nv_ref[...] = v

    args, out_shape, in_specs, out_specs = [], [], [], []
    for k in range(n):
        rows, cols = ws[k].shape
        spec = pl.BlockSpec((rows // steps, cols), lambda i: (i, 0))
        args += [ws[k], gs[k], ms[k], vs[k]]
        in_specs += [spec] * 4
        out_specs += [spec] * 3
        out_shape += [jax.ShapeDtypeStruct(ws[k].shape, F32)] * 3
    outs = pl.pallas_call(
        body, name=name, grid=(steps,), out_shape=out_shape, in_specs=in_specs, out_specs=out_specs,
        compiler_params=_params(("parallel",)),
    )(*args)
    return [tuple(outs[3 * k:3 * k + 3]) for k in range(n)]


def kernel(x, meta, norm_g, final_g, ev_w_in, ev_q_norm_g, ev_kv_norm_g, ev_w_uq, ev_w_ukv, ev_w_out, od_w_in, od_sinks, od_w_out, loss_target, m_meta, m_norm_g, m_final_g, m_ev_w_in, m_ev_q_norm_g, m_ev_kv_norm_g, m_ev_w_uq, m_ev_w_ukv, m_ev_w_out, m_od_w_in, m_od_sinks, m_od_w_out, v_meta, v_norm_g, v_final_g, v_ev_w_in, v_ev_q_norm_g, v_ev_kv_norm_g, v_ev_w_uq, v_ev_w_ukv, v_ev_w_out, v_od_w_in, v_od_sinks, v_od_w_out):
    nbatch, seq, d = x.shape
    nb = seq // BLK + 1
    lp = nb * BLK
    shards = dict(ev_w_in=ev_w_in[0], ev_w_uq=ev_w_uq[0], ev_w_ukv=ev_w_ukv[0], ev_w_out=ev_w_out[0],
                  od_w_in=od_w_in[0], od_w_out=od_w_out[0])

    w_ie_blocks, meta_blocks = _all_gather([shards["ev_w_in"].astype(BF16), meta], "gather_weights")
    meta_full = meta_blocks.transpose(1, 0, 2).reshape(N_META, d)

    head = jnp.concatenate([jnp.zeros((N_PAD, d), F32), meta_full], axis=0)
    h0 = jnp.concatenate([jnp.broadcast_to(head[None], (nbatch, BLK, d)), x], axis=1).reshape(nbatch * lp, d)
    grads = _local_step(
        h0, loss_target.reshape(nbatch * seq, d), norm_g, final_g.reshape(1, d), ev_q_norm_g, ev_kv_norm_g,
        od_sinks, _ev_in_from_blocks(w_ie_blocks),
        {name: shards[name].astype(BF16) for name in _LATE}, nbatch, nb)
    dh0 = grads["dh0"].reshape(nbatch, lp, d)
    grad_x = dh0[:, BLK:]

    d_meta = dh0[:, N_PAD:BLK].sum(axis=0).reshape(N_META, N_DEV, BLK).transpose(1, 0, 2)
    pad = lambda a, n: jnp.concatenate([a.reshape(1, -1), jnp.zeros((1, n - a.size), F32)], axis=1)
    loss_part = (0.5 / d * jnp.sum(grads["lossv"])).reshape(1, 1)
    rep = jnp.concatenate([grads["norm_g"].reshape(1, -1), grads["final_g"], grads["gq"], pad(loss_part, 256),
                           pad(grads["gkv"], 256), pad(grads["sinks"], 256)], axis=1).reshape(32, BLK)
    small = jnp.concatenate([d_meta, jnp.broadcast_to(rep[None], (N_DEV, 32, BLK))], axis=1)
    reduced = _exchange_sum(grads["last"] + [small], grads["received"], "reduce_grads")
    g_shard = dict(zip(_LAST_GRADS + ("small",) + _EARLY_GRADS, reduced))
    red_small = g_shard.pop("small")
    rep = red_small[N_META:].reshape(1, -1)
    loss = rep[0, 3 * d + 256]
    g_small = dict(meta=red_small[:N_META], norm_g=rep[:, :2 * d].reshape(2, d), final_g=rep[:, 2 * d:3 * d],
                   ev_q_norm_g=rep[:, 3 * d:3 * d + 256], ev_kv_norm_g=rep[:, 3 * d + 512:3 * d + 640],
                   od_sinks=rep[:, 3 * d + 768:3 * d + 784])

    names = ["meta", "norm_g", "final_g", "ev_w_in", "ev_q_norm_g", "ev_kv_norm_g", "ev_w_uq", "ev_w_ukv",
             "ev_w_out", "od_w_in", "od_sinks", "od_w_out"]
    given = dict(meta=(meta, m_meta, v_meta), norm_g=(norm_g, m_norm_g, v_norm_g),
                 final_g=(final_g, m_final_g, v_final_g), ev_w_in=(ev_w_in, m_ev_w_in, v_ev_w_in),
                 ev_q_norm_g=(ev_q_norm_g, m_ev_q_norm_g, v_ev_q_norm_g),
                 ev_kv_norm_g=(ev_kv_norm_g, m_ev_kv_norm_g, v_ev_kv_norm_g),
                 ev_w_uq=(ev_w_uq, m_ev_w_uq, v_ev_w_uq), ev_w_ukv=(ev_w_ukv, m_ev_w_ukv, v_ev_w_ukv),
                 ev_w_out=(ev_w_out, m_ev_w_out, v_ev_w_out), od_w_in=(od_w_in, m_od_w_in, v_od_w_in),
                 od_sinks=(od_sinks, m_od_sinks, v_od_sinks), od_w_out=(od_w_out, m_od_w_out, v_od_w_out))
    ws, gs, ms, vs = [], [], [], []
    for name in names:
        g2 = g_shard[name] if name in g_shard else g_small[name]
        w, m, v = given[name]
        ws.append(w.reshape(g2.shape))
        ms.append(m.reshape(g2.shape))
        vs.append(v.reshape(g2.shape))
        gs.append(g2)
    big = [k for k, name in enumerate(names) if name in _BIG]
    small = [k for k, name in enumerate(names) if name not in _BIG]
    pick = lambda xs, ks: [xs[k] for k in ks]
    upd_big = _adamw(pick(ws, big), pick(gs, big), pick(ms, big), pick(vs, big), "adamw", N_DEV)
    upd_small = _adamw(pick(ws, small), pick(gs, small), pick(ms, small), pick(vs, small), "adamw_small", 1)
    upd = dict(zip(big + small, upd_big + upd_small))
    shape_of = {name: given[name][0].shape for name in names}
    grads_out = [gs[k].reshape(shape_of[n]) for k, n in enumerate(names)]
    deltas = [upd[k][0].reshape(shape_of[n]) for k, n in enumerate(names)]
    new_m = [upd[k][1].reshape(shape_of[n]) for k, n in enumerate(names)]
    new_v = [upd[k][2].reshape(shape_of[n]) for k, n in enumerate(names)]
    return (loss, grad_x, *grads_out, *deltas, *new_m, *new_v)
```

```python
import jax
import jax.numpy as jnp
from jax import lax
from jax.experimental import pallas as pl
from jax.experimental.pallas import tpu as pltpu

F32 = jnp.float32
BF16 = jnp.bfloat16

D_MODEL = 1024
N_META = 16
BLK = 128
HALF = 64
N_PAD = BLK - N_META
NORM_EPS = 1e-6
NEG = -1e30
N_DEV = 8

SB_SCALE = 64 ** -0.5
MLA_SCALE = 96 ** -0.5
SWA_SCALE = 64 ** -0.5
ROPE_BASE = 10000.0

EV_IN_PAD = 3072
EV_MID_BLK = 4
EV_GMLA_BLK = 20
OD_K_BLK = 16
OD_V_BLK = 17

ADAM_LR = 0.001
ADAM_B1 = 0.9
ADAM_B2 = 0.999
ADAM_EPS = 1e-08
ADAM_WD = 0.01
ADAM_STEP = 10


def _dot(a, b):
    return lax.dot_general(a, b, (((1,), (0,)), ((), ())), preferred_element_type=F32)


def _dot_nt(a, b):
    return lax.dot_general(a, b, (((1,), (1,)), ((), ())), preferred_element_type=F32)


def _dot_tn(a, b):
    return lax.dot_general(a, b, (((0,), (0,)), ((), ())), preferred_element_type=F32)


def _sigmoid(x):
    return 1.0 / (1.0 + jnp.exp(-x))


def _iotas():
    row = lax.broadcasted_iota(jnp.int32, (BLK, BLK), 0)
    lane = lax.broadcasted_iota(jnp.int32, (BLK, BLK), 1)
    return row, lane


WIDE = 2 * BLK
HEAD_GROUPS = (range(0, 8),)


def _key_mask(i, first_key, width, strict):
    t_pos = i * BLK + lax.broadcasted_iota(jnp.int32, (BLK, width), 0)
    s_pos = first_key + lax.broadcasted_iota(jnp.int32, (BLK, width), 1)
    seen = (s_pos < t_pos) if strict else (s_pos <= t_pos)
    return seen & (s_pos >= N_PAD)


def _widen(x, width):
    return x if width == BLK else jnp.concatenate([x] * (width // BLK), axis=1)


def _over_key_blocks(n, block, reverse):
    pairs = n // 2
    last = pl.multiple_of((n - 1) * BLK, BLK)

    def step(jj, carry):
        jp = (pairs - 1 - jj) if reverse else jj
        off = pl.multiple_of(jp * WIDE, WIDE)
        edge = (jp == 0) | (jp == pairs - 1)
        pl.when(edge)(lambda: block(off, WIDE, True))
        pl.when(jnp.logical_not(edge))(lambda: block(off, WIDE, False))
        return carry

    if reverse:
        pl.when(n % 2 == 1)(lambda: block(last, BLK, True))
        lax.fori_loop(0, pairs, step, 0)
    else:
        lax.fori_loop(0, pairs, step, 0)
        pl.when(n % 2 == 1)(lambda: block(last, BLK, True))


def _sel(mask, x, fill):
    return x if mask is None else jnp.where(mask, x, fill)


def _rope(x, c, s1, s2):
    return x * c + pltpu.roll(x, BLK - 16, 1) * s1 + pltpu.roll(x, 16, 1) * s2


def _rope_t(x, c, s1, s2):
    return x * c - pltpu.roll(x, BLK - 16, 1) * s1 - pltpu.roll(x, 16, 1) * s2


def _params(sem, vmem_mb=None):
    kw = dict(dimension_semantics=sem)
    if vmem_mb is not None:
        kw["vmem_limit_bytes"] = vmem_mb << 20
    return pltpu.CompilerParams(**kw)


def _row_tile(t, cands):
    for c in cands:
        if t % c == 0:
            return c
    raise ValueError(t)


def _mm(a, w, mode, name, add=None, out_dtype=F32):
    pieces = list(a) if isinstance(a, (list, tuple)) else [a]
    m = pieces[0].shape[0]
    n = w.shape[1] if mode == "nn" else w.shape[0]
    tm = _row_tile(m, (544, 256, 128) if n <= 1024 else (256, 128))
    widths = [p.shape[1] for p in pieces]
    offs = [sum(widths[:i]) for i in range(len(widths))]

    def body(*refs):
        p_refs = refs[:len(pieces)]
        w_ref = refs[len(pieces)]
        o_ref = refs[-1]
        acc = None
        for p_ref, off, wd in zip(p_refs, offs, widths):
            x = p_ref[...].astype(BF16)
            part = _dot(x, w_ref[off:off + wd, :]) if mode == "nn" else _dot_nt(x, w_ref[:, off:off + wd])
            acc = part if acc is None else acc + part
        if add is not None:
            acc = acc + refs[len(pieces) + 1][...]
        o_ref[...] = acc.astype(o_ref.dtype)

    in_specs = [pl.BlockSpec((tm, wd), lambda i: (i, 0)) for wd in widths]
    in_specs.append(pl.BlockSpec(w.shape, lambda i: (0, 0)))
    args = pieces + [w]
    if add is not None:
        in_specs.append(pl.BlockSpec((tm, n), lambda i: (i, 0)))
        args.append(add)
    return pl.pallas_call(
        body, name=name, grid=(m // tm,), in_specs=in_specs,
        out_specs=pl.BlockSpec((tm, n), lambda i: (i, 0)),
        out_shape=jax.ShapeDtypeStruct((m, n), out_dtype),
        compiler_params=_params(("parallel",), 48),
    )(*args)


def _mm_tn(x, pieces, name):
    t, k = x.shape
    tt = _row_tile(t, (544, 256, 128))
    widths = [p.shape[1] for p in pieces]

    def body(*refs):
        x_ref = refs[0]
        d_refs = refs[1:1 + len(pieces)]
        o_refs = refs[1 + len(pieces):]

        @pl.when(pl.program_id(0) == 0)
        def _():
            for o_ref in o_refs:
                o_ref[...] = jnp.zeros_like(o_ref)

        xb = x_ref[...].astype(BF16)
        for d_ref, o_ref in zip(d_refs, o_refs):
            o_ref[...] += _dot_tn(xb, d_ref[...].astype(BF16))

    return pl.pallas_call(
        body, name=name, grid=(t // tt,),
        in_specs=[pl.BlockSpec((tt, k), lambda i: (i, 0))] + [pl.BlockSpec((tt, wd), lambda i: (i, 0)) for wd in widths],
        out_specs=[pl.BlockSpec((k, wd), lambda i: (0, 0)) for wd in widths],
        out_shape=[jax.ShapeDtypeStruct((k, wd), F32) for wd in widths],
        compiler_params=_params(("arbitrary",), 56),
    )(x, *pieces)


def _norm_mm(h, g, w, name):
    t, d = h.shape
    n = w.shape[1]
    tr = _row_tile(t, (544, 256, 128))

    def body(h_ref, g_ref, w_ref, o_ref, hn_ref):
        x = h_ref[...]
        r = lax.rsqrt(jnp.mean(x * x, axis=1, keepdims=True) + NORM_EPS)
        hn = (x * r * g_ref[...]).astype(BF16)
        hn_ref[...] = hn
        o_ref[...] = _dot(hn, w_ref[...]).astype(o_ref.dtype)

    row = lambda width: pl.BlockSpec((tr, width), lambda i: (i, 0))
    return pl.pallas_call(
        body, name=name, grid=(t // tr,),
        in_specs=[row(d), pl.BlockSpec((1, d), lambda i: (0, 0)), pl.BlockSpec(w.shape, lambda i: (0, 0))],
        out_specs=[row(n), row(d)],
        out_shape=[jax.ShapeDtypeStruct((t, n), BF16), jax.ShapeDtypeStruct((t, d), BF16)],
        compiler_params=_params(("parallel",), 48),
    )(h, g, w)


def _mm_norm_bwd(pieces, w, h, g, dres, name):
    t, d = h.shape
    tr = _row_tile(t, (544, 256, 128))
    widths = [p.shape[1] for p in pieces]
    offs = [sum(widths[:i]) for i in range(len(widths))]

    def body(*refs):
        p_refs = refs[:len(pieces)]
        w_ref, h_ref, g_ref, dres_ref, dh_ref, dg_ref = refs[len(pieces):]

        @pl.when(pl.program_id(0) == 0)
        def _():
            dg_ref[...] = jnp.zeros_like(dg_ref)

        dy = None
        for p_ref, off, wd in zip(p_refs, offs, widths):
            part = _dot_nt(p_ref[...].astype(BF16), w_ref[:, off:off + wd])
            dy = part if dy is None else dy + part
        x = h_ref[...]
        r = lax.rsqrt(jnp.mean(x * x, axis=1, keepdims=True) + NORM_EPS)
        nx = x * r
        dn = dy * g_ref[...]
        dh_ref[...] = dres_ref[...] + r * (dn - nx * jnp.mean(dn * nx, axis=1, keepdims=True))
        dg_ref[...] += jnp.sum(dy * nx, axis=0, keepdims=True)

    row = lambda width: pl.BlockSpec((tr, width), lambda i: (i, 0))
    vec = pl.BlockSpec((1, d), lambda i: (0, 0))
    return pl.pallas_call(
        body, name=name, grid=(t // tr,),
        in_specs=[row(wd) for wd in widths] + [pl.BlockSpec(w.shape, lambda i: (0, 0)), row(d), vec, row(d)],
        out_specs=[row(d), vec],
        out_shape=[jax.ShapeDtypeStruct((t, d), F32), jax.ShapeDtypeStruct((1, d), F32)],
        compiler_params=_params(("arbitrary",), 56),
    )(*pieces, w, h, g, dres)


def _final(h2, tgt, g, nbatch, nb):
    t, d = h2.shape

    def body(h_ref, t_ref, g_ref, dh_ref, loss_ref, dg_ref):
        b = pl.program_id(0)
        i = pl.program_id(1)

        @pl.when((b == 0) & (i == 0))
        def _():
            loss_ref[...] = jnp.zeros_like(loss_ref)
            dg_ref[...] = jnp.zeros_like(dg_ref)

        x = h_ref[...]
        r = lax.rsqrt(jnp.mean(x * x, axis=1, keepdims=True) + NORM_EPS)
        nx = x * r
        gg = g_ref[...]
        live = jnp.where(i >= 1, 1.0, 0.0)
        err = (nx * gg - t_ref[...]) * live
        loss_ref[...] += jnp.sum(err * err, axis=0, keepdims=True)
        dy = err * (1.0 / d)
        dn = dy * gg
        dh_ref[...] = r * (dn - nx * jnp.mean(dn * nx, axis=1, keepdims=True))
        dg_ref[...] += jnp.sum(dy * nx, axis=0, keepdims=True)

    vec = pl.BlockSpec((1, d), lambda b, i: (0, 0))
    return pl.pallas_call(
        body, name="final_loss", grid=(nbatch, nb),
        in_specs=[pl.BlockSpec((BLK, d), lambda b, i: (b * nb + i, 0)),
                  pl.BlockSpec((BLK, d), lambda b, i: (b * (nb - 1) + jnp.maximum(i - 1, 0), 0)),
                  vec],
        out_specs=[pl.BlockSpec((BLK, d), lambda b, i: (b * nb + i, 0)), vec, vec],
        out_shape=[jax.ShapeDtypeStruct((t, d), F32), jax.ShapeDtypeStruct((1, d), F32),
                   jax.ShapeDtypeStruct((1, d), F32)],
        compiler_params=_params(("arbitrary", "arbitrary")),
    )(h2, tgt, g)


def _direct_exchange(srcs, dsts, send_sems, recv_sems, local_sems, chunked):
    x, y, c = lax.axis_index("x"), lax.axis_index("y"), lax.axis_index("c")
    me = 4 * x + 2 * y + c
    arrays = range(len(srcs))
    local = [pltpu.make_async_copy(srcs[a].at[me] if chunked else srcs[a], dsts[a].at[me], local_sems.at[a])
             for a in arrays]
    sends, recvs = [], []
    for d in range(1, N_DEV):
        px = x + ((d >> 2) & 1) - 2 * x * ((d >> 2) & 1)
        py = y + ((d >> 1) & 1) - 2 * y * ((d >> 1) & 1)
        pc = c + (d & 1) - 2 * c * (d & 1)
        pid = 4 * px + 2 * py + pc
        for a in arrays:
            kw = dict(send_sem=send_sems.at[d - 1, a], recv_sem=recv_sems.at[d - 1, a],
                      device_id=(px, py, pc), device_id_type=pl.DeviceIdType.MESH)
            src = srcs[a].at[pid] if chunked else srcs[a]
            sends.append(pltpu.make_async_remote_copy(src_ref=src, dst_ref=dsts[a].at[me], **kw))
            recvs.append(pltpu.make_async_remote_copy(src_ref=src, dst_ref=dsts[a].at[pid], **kw))
    return local, sends, recvs


def _exchange_beside(first, last, srcs, dsts, sems, chunked):
    local, sends, recvs = _direct_exchange(srcs, dsts, *sems, chunked)

    @pl.when(first)
    def _():
        for cp in local + sends:
            cp.start()

    @pl.when(last)
    def _():
        for cp in local:
            cp.wait()
        for cp in recvs:
            cp.wait_recv()
        for cp in sends:
            cp.wait_send()


def _exchange_sems(n):
    return [pltpu.SemaphoreType.DMA((7, n)), pltpu.SemaphoreType.DMA((7, n)), pltpu.SemaphoreType.DMA((n,))]


def _sb_logits(z):
    log_beta = jnp.minimum(z, 0.0) - jnp.log(1.0 + jnp.exp(-jnp.abs(z)))
    return log_beta, log_beta - z


def _tri(width, after):
    j = lax.broadcasted_iota(jnp.int32, (width, width), 0)
    s = lax.broadcasted_iota(jnp.int32, (width, width), 1)
    return (j > s) if after else (j < s)


def _tri_ones(tri):
    return jnp.concatenate([tri.astype(BF16), jnp.ones((tri.shape[0], BLK), BF16)], axis=1)


def _block_sums(x, tri_ones, after):
    xb = x.astype(BF16)
    subs = [_dot(xb[:, s:s + BLK], tri_ones) for s in range(0, x.shape[1], BLK)]
    if len(subs) == 1:
        return subs[0][:, :BLK], subs[0][:, BLK:]
    first, second = subs
    total = first[:, BLK:] + second[:, BLK:]
    if after:
        return jnp.concatenate([first[:, :BLK] + second[:, BLK:], second[:, :BLK]], axis=1), total
    return jnp.concatenate([first[:, :BLK], second[:, :BLK] + first[:, BLK:]], axis=1), total


def _head_masked(x, lane, scale=None):
    out = []
    for h in range(8):
        xp = x[:, (h // 2) * BLK:(h // 2 + 1) * BLK]
        xm = jnp.where((lane >= HALF) if h % 2 else (lane < HALF), xp, jnp.zeros_like(xp))
        out.append(xm if scale is None else xm * scale)
    return jnp.concatenate(out, axis=1)


def _sb_fwd(proj, shards, nbatch, nb):
    lp = nb * BLK
    t = nbatch * lp

    ns = len(shards)

    def body(*refs):
        q_ref, k_ref, v_ref, g_ref = refs[:4]
        shard_refs = refs[4:4 + ns]
        o_ref, og_ref, cm_ref = refs[4 + ns:7 + ns]
        gathered_refs = refs[7 + ns:7 + 2 * ns]
        c_scr, qm_scr = refs[7 + 2 * ns:9 + 2 * ns]
        b = pl.program_id(0)
        i = pl.program_id(1)
        _exchange_beside((b == 0) & (i == 0), (b == nbatch - 1) & (i == nb - 1),
                         shard_refs, gathered_refs, refs[9 + 2 * ns:], chunked=False)
        _, lane = _iotas()
        lo_m = lane < HALF
        cm_ref[...] = jnp.zeros_like(cm_ref)
        c_scr[...] = jnp.zeros_like(c_scr)
        o_ref[...] = jnp.zeros_like(o_ref)
        qm_scr[...] = _head_masked(q_ref[...], lane, SB_SCALE)

        def block(off, width, edge):
            mask = _key_mask(i, off, width, strict=True) if edge else None
            upper = _tri_ones(_tri(BLK, after=True))
            onehot = lane == off // WIDE
            heads = range(8)
            hcs = [slice(h * BLK, (h + 1) * BLK) for h in heads]
            kbs = [k_ref[pl.ds(off, width), hc] for hc in hcs[:4]]
            vbs = [v_ref[pl.ds(off, width), hc] for hc in hcs[:4]]
            zs = [_dot_nt(qm_scr[:, hcs[h]], kbs[h // 2]) for h in heads]
            lbs, l1s = [], []
            for h in heads:
                log_beta, log_1m = _sb_logits(zs[h])
                lbs.append(log_beta)
                l1s.append(_sel(mask, log_1m, 0.0))
            css = [_block_sums(l1s[h], upper, after=True) for h in heads]
            avs = []
            for h in heads:
                c = c_scr[h]
                avs.append(_sel(mask, jnp.exp(lbs[h] + css[h][0] + _widen(c, width)), 0.0).astype(BF16))
                if width == WIDE:
                    cm_ref[:, hcs[h]] = jnp.where(onehot, c, cm_ref[:, hcs[h]])
                c_scr[h] = c + css[h][1]
            accs = [_dot(avs[h], vbs[h // 2]) for h in heads]
            for p in range(4):
                o_ref[:, hcs[p]] += jnp.where(lo_m, accs[2 * p], accs[2 * p + 1])

        _over_key_blocks(i + 1, block, reverse=True)
        g = g_ref[...].astype(F32)
        og_ref[...] = (o_ref[...] * g * _sigmoid(g)).astype(og_ref.dtype)

    tile = lambda col: pl.BlockSpec((BLK, 512), lambda b, i: (b * nb + i, col))
    full = lambda col: pl.BlockSpec((lp, 512), lambda b, i: (b, col))
    hbm = pl.BlockSpec(memory_space=pl.ANY)
    outs = pl.pallas_call(
        body, name="sb_fwd", grid=(nbatch, nb),
        in_specs=[tile(0), full(1), full(2), tile(3)] + [hbm] * ns,
        out_specs=[tile(0), tile(0), pl.BlockSpec((BLK, 1024), lambda b, i: (b * nb + i, 0))] + [hbm] * ns,
        out_shape=[jax.ShapeDtypeStruct((t, 512), F32), jax.ShapeDtypeStruct((t, 512), BF16),
                   jax.ShapeDtypeStruct((t, 1024), F32)]
        + [jax.ShapeDtypeStruct((N_DEV,) + s.shape, s.dtype) for s in shards],
        scratch_shapes=[pltpu.VMEM((8, BLK, BLK), F32), pltpu.VMEM((BLK, 1024), BF16)] + _exchange_sems(ns),
        compiler_params=_params(("arbitrary", "arbitrary"), 48),
    )(proj, proj, proj, proj, *shards)
    return outs[0], outs[1], outs[2], list(outs[3:])


def _sb_bwd(proj, o, dog, cm, chunked, nbatch, nb):
    lp = nb * BLK
    t = nbatch * lp
    ns = len(chunked)

    def body(*refs):
        q_ref, k_ref, v_ref, g_ref, o_ref, dog_ref, cm_ref = refs[:7]
        chunk_refs = refs[7:7 + ns]
        dq_ref, dk_ref, dv_ref, dg_ref = refs[7 + ns:11 + ns]
        received_refs = refs[11 + ns:11 + 2 * ns]
        c_scr, qm_scr, dom_scr, dq_scr = refs[11 + 2 * ns:15 + 2 * ns]
        b = pl.program_id(0)
        i = pl.program_id(1)
        _exchange_beside((b == 0) & (i == 0), (b == nbatch - 1) & (i == nb - 1),
                         chunk_refs, received_refs, refs[15 + 2 * ns:], chunked=True)

        @pl.when(i == 0)
        def _():
            dk_ref[...] = jnp.zeros_like(dk_ref)
            dv_ref[...] = jnp.zeros_like(dv_ref)

        _, lane = _iotas()
        lo_m = lane < HALF
        g = g_ref[...].astype(F32)
        sig = _sigmoid(g)
        dog_v = dog_ref[...]
        dg_ref[...] = (dog_v * o_ref[...] * (sig * (1.0 + g * (1.0 - sig)))).astype(dg_ref.dtype)
        dom_scr[...] = _head_masked((dog_v * g * sig).astype(BF16), lane)
        qm_scr[...] = _head_masked(q_ref[...], lane, SB_SCALE)
        c_scr[...] = jnp.zeros_like(c_scr)
        dq_scr[...] = jnp.zeros_like(dq_scr)

        def block(off, width, edge):
            mask = _key_mask(i, off, width, strict=True) if edge else None
            upper = _tri_ones(_tri(BLK, after=True))
            lower = _tri_ones(_tri(BLK, after=False))
            onehot = lane == off // WIDE
            hcs = [slice(h * BLK, (h + 1) * BLK) for h in range(8)]
            kbs = [k_ref[pl.ds(off, width), hc] for hc in hcs[:4]]
            vbs = [v_ref[pl.ds(off, width), hc] for hc in hcs[:4]]
            for heads in HEAD_GROUPS:
                zs = {h: _dot_nt(qm_scr[:, hcs[h]], kbs[h // 2]) for h in heads}
                dps = {h: _dot_nt(dom_scr[:, hcs[h]], vbs[h // 2]) for h in heads}
                lbs, l1s = {}, {}
                for h in heads:
                    lbs[h], l1s[h] = _sb_logits(zs[h])
                sufs = {h: _block_sums(_sel(mask, l1s[h], 0.0), upper, after=True)[0] for h in heads}
                prs, dzzs = {}, {}
                for h in heads:
                    expo = lbs[h] + sufs[h]
                    if width == WIDE:
                        expo = expo + jnp.sum(jnp.where(onehot, cm_ref[:, hcs[h]], 0.0), axis=1, keepdims=True)
                    pr = _sel(mask, jnp.exp(expo), 0.0)
                    dzzs[h] = pr * dps[h]
                    prs[h] = pr.astype(BF16)
                css = {h: _block_sums(dzzs[h], lower, after=False) for h in heads}
                dzbs = {}
                for h in heads:
                    c2 = c_scr[h]
                    prefix = css[h][0] + _widen(c2, width)
                    dz = _sel(mask, dzzs[h] * jnp.exp(l1s[h]) - jnp.exp(lbs[h]) * prefix, 0.0)
                    dzbs[h] = dz.astype(BF16)
                    c_scr[h] = c2 + css[h][1]
                dqs = {h: _dot(dzbs[h], kbs[h // 2]) for h in heads}
                dks = {h: _dot_tn(dzbs[h], qm_scr[:, hcs[h]]) for h in heads}
                dvs = {h: _dot_tn(prs[h], dom_scr[:, hcs[h]]) for h in heads}
                for p in sorted({h // 2 for h in heads}):
                    dq_scr[:, hcs[p]] += jnp.where(lo_m, dqs[2 * p], dqs[2 * p + 1])
                    dk_ref[pl.ds(off, width), hcs[p]] += dks[2 * p] + dks[2 * p + 1]
                    dv_ref[pl.ds(off, width), hcs[p]] += dvs[2 * p] + dvs[2 * p + 1]

        _over_key_blocks(i + 1, block, reverse=False)
        dq_ref[...] = (dq_scr[...] * SB_SCALE).astype(dq_ref.dtype)

    tile = lambda col: pl.BlockSpec((BLK, 512), lambda b, i: (b * nb + i, col))
    full = lambda col: pl.BlockSpec((lp, 512), lambda b, i: (b, col))
    acc = jax.ShapeDtypeStruct((t, 512), F32)
    once = jax.ShapeDtypeStruct((t, 512), BF16)
    hbm = pl.BlockSpec(memory_space=pl.ANY)
    outs = pl.pallas_call(
        body, name="sb_bwd", grid=(nbatch, nb),
        in_specs=[tile(0), full(1), full(2), tile(3), tile(0), tile(0),
                  pl.BlockSpec((BLK, 1024), lambda b, i: (b * nb + i, 0))] + [hbm] * ns,
        out_specs=[tile(0), full(0), full(0), tile(0)] + [hbm] * ns,
        out_shape=[once, acc, acc, once] + [jax.ShapeDtypeStruct(a.shape, a.dtype) for a in chunked],
        scratch_shapes=[pltpu.VMEM((8, BLK, BLK), F32), pltpu.VMEM((BLK, 1024), BF16),
                        pltpu.VMEM((BLK, 1024), BF16), pltpu.VMEM((BLK, 512), F32)] + _exchange_sems(ns),
        compiler_params=_params(("arbitrary", "arbitrary"), 56),
    )(proj, proj, proj, proj, o, dog, cm, *chunked)
    return outs[0], outs[1], outs[2], outs[3], list(outs[4:])


def _mla_prep(proj, gq, gkv, tabs, nbatch, nb):
    t = proj.shape[0]

    def body(mid_ref, gq_ref, gkv_ref, c_ref, s1_ref, s2_ref, cq_ref, ckv_ref, kr_ref):
        cq = mid_ref[:, 0:256].astype(F32)
        r = lax.rsqrt(jnp.mean(cq * cq, axis=1, keepdims=True) + NORM_EPS)
        cq_ref[...] = (cq * r * gq_ref[...]).astype(BF16)
        ckv = mid_ref[:, 256:384].astype(F32)
        r = lax.rsqrt(jnp.mean(ckv * ckv, axis=1, keepdims=True) + NORM_EPS)
        ckv_ref[...] = (ckv * r * gkv_ref[...]).astype(BF16)
        kr = mid_ref[:, 384:512].astype(F32)
        kr_ref[...] = _rope(kr, c_ref[...], s1_ref[...], s2_ref[...]).astype(BF16)

    tr = _row_tile(nb * BLK, (544, BLK))
    nt = nb * BLK // tr
    tab = pl.BlockSpec((tr, BLK), lambda b, i: (i, 0))
    rowspec = lambda w: pl.BlockSpec((tr, w), lambda b, i: (b * nt + i, 0))
    return pl.pallas_call(
        body, name="mla_prep", grid=(nbatch, nt),
        in_specs=[pl.BlockSpec((tr, 512), lambda b, i: (b * nt + i, EV_MID_BLK)),
                  pl.BlockSpec((1, 256), lambda b, i: (0, 0)), pl.BlockSpec((1, 128), lambda b, i: (0, 0)),
                  tab, tab, tab],
        out_specs=[rowspec(256), rowspec(128), rowspec(128)],
        out_shape=[jax.ShapeDtypeStruct((t, 256), BF16), jax.ShapeDtypeStruct((t, 128), BF16),
                   jax.ShapeDtypeStruct((t, 128), BF16)],
        compiler_params=_params(("parallel", "parallel")),
    )(proj, gq, gkv, *tabs)


def _mla_prep_bwd(proj, gq, gkv, tabs, dcqn, dckvn, dkrot, nbatch, nb):
    t = proj.shape[0]

    def body(mid_ref, gq_ref, gkv_ref, c_ref, s1_ref, s2_ref, dcq_ref, dckv_ref, dkr_ref,
             dmid_ref, dgq_ref, dgkv_ref):
        @pl.when((pl.program_id(0) == 0) & (pl.program_id(1) == 0))
        def _():
            dgq_ref[...] = jnp.zeros_like(dgq_ref)
            dgkv_ref[...] = jnp.zeros_like(dgkv_ref)

        def norm_bwd(x, gain, dy, dgain_ref):
            r = lax.rsqrt(jnp.mean(x * x, axis=1, keepdims=True) + NORM_EPS)
            nx = x * r
            dn = dy * gain
            dgain_ref[...] += jnp.sum(dy * nx, axis=0, keepdims=True)
            return r * (dn - nx * jnp.mean(dn * nx, axis=1, keepdims=True))

        dmid_ref[:, 0:256] = norm_bwd(
            mid_ref[:, 0:256].astype(F32), gq_ref[...], dcq_ref[...], dgq_ref).astype(BF16)
        dmid_ref[:, 256:384] = norm_bwd(
            mid_ref[:, 256:384].astype(F32), gkv_ref[...], dckv_ref[...], dgkv_ref).astype(BF16)
        dmid_ref[:, 384:512] = _rope_t(dkr_ref[...], c_ref[...], s1_ref[...], s2_ref[...]).astype(BF16)

    tr = _row_tile(nb * BLK, (544, BLK))
    nt = nb * BLK // tr
    tab = pl.BlockSpec((tr, BLK), lambda b, i: (i, 0))
    rowspec = lambda w: pl.BlockSpec((tr, w), lambda b, i: (b * nt + i, 0))
    vq = pl.BlockSpec((1, 256), lambda b, i: (0, 0))
    vkv = pl.BlockSpec((1, 128), lambda b, i: (0, 0))
    return pl.pallas_call(
        body, name="mla_prep_bwd", grid=(nbatch, nt),
        in_specs=[pl.BlockSpec((tr, 512), lambda b, i: (b * nt + i, EV_MID_BLK)), vq, vkv, tab, tab, tab,
                  rowspec(256), rowspec(128), rowspec(128)],
        out_specs=[rowspec(512), vq, vkv],
        out_shape=[jax.ShapeDtypeStruct((t, 512), BF16), jax.ShapeDtypeStruct((1, 256), F32),
                   jax.ShapeDtypeStruct((1, 128), F32)],
        compiler_params=_params(("arbitrary", "arbitrary")),
    )(proj, gq, gkv, *tabs, dcqn, dckvn, dkrot)


def _mla_scores(qf, kvb, krb, mask, lo_m):
    kf = jnp.where(lo_m, kvb, krb)
    return kf, _sel(mask, _dot_nt(qf, kf), NEG)


def _mla_fwd(qh, kvh, krot, proj, tabs, nbatch, nb):
    lp = nb * BLK
    t = nbatch * lp

    def body(q_ref, kv_ref, kr_ref, g_ref, c_ref, s1_ref, s2_ref, o_ref, og_ref, lse_ref,
             qf_scr, m_scr, l_scr, acc_scr):
        i = pl.program_id(1)
        row, lane = _iotas()
        lo_m = lane < HALF
        for h in range(8):
            hc = slice(h * BLK, (h + 1) * BLK)
            qf_scr[:, hc] = (_rope(q_ref[:, hc].astype(F32), c_ref[...], s1_ref[...], s2_ref[...])
                             * MLA_SCALE).astype(BF16)
        m_scr[...] = jnp.full(m_scr.shape, NEG, F32)
        l_scr[...] = jnp.zeros_like(l_scr)
        acc_scr[...] = jnp.zeros_like(acc_scr)

        def block(off, width, edge):
            mask = _key_mask(i, off, width, strict=False) if edge else None
            lo_k = lax.broadcasted_iota(jnp.int32, (width, BLK), 1) < HALF
            ones = jnp.ones((width, BLK), BF16)
            krb = kr_ref[pl.ds(off, width), :]
            heads = range(8)
            hcs = [slice(h * BLK, (h + 1) * BLK) for h in heads]
            kvbs = [kv_ref[pl.ds(off, width), hc] for hc in hcs]
            ss = [_mla_scores(qf_scr[:, hcs[h]], kvbs[h], krb, mask, lo_k)[1] for h in heads]
            ps, alphas = [], []
            for h in heads:
                m = m_scr[h]
                m2 = jnp.maximum(m, jnp.max(ss[h], axis=1, keepdims=True))
                ps.append(jnp.exp(ss[h] - _widen(m2, width)).astype(BF16))
                alphas.append(jnp.exp(m - m2))
                m_scr[h] = m2
            pvs = [_dot(ps[h], jnp.concatenate([kvbs[h], ones], axis=1)) for h in heads]
            for h in heads:
                l_scr[h] = alphas[h] * l_scr[h] + pvs[h][:, BLK:]
                acc_scr[h] = alphas[h] * acc_scr[h] + pvs[h][:, :BLK]

        _over_key_blocks(i + 1, block, reverse=False)
        lse = jnp.zeros((BLK, BLK), F32)
        for p in range(4):
            pc = slice(p * BLK, (p + 1) * BLK)
            o0 = acc_scr[2 * p] / l_scr[2 * p]
            o1 = acc_scr[2 * p + 1] / l_scr[2 * p + 1]
            o_ref[:, pc] = jnp.where(lo_m, pltpu.roll(o0, HALF, 1), o1)
            for h in (2 * p, 2 * p + 1):
                lse = lse + jnp.where(lane == h, m_scr[h] + jnp.log(l_scr[h]), 0.0)
        lse_ref[...] = lse
        g = g_ref[...].astype(F32)
        og_ref[...] = (o_ref[...] * g * _sigmoid(g)).astype(og_ref.dtype)

    tile = pl.BlockSpec((BLK, 512), lambda b, i: (b * nb + i, 0))
    tab = pl.BlockSpec((BLK, BLK), lambda b, i: (i, 0))
    heads = pltpu.VMEM((8, BLK, BLK), F32)
    return pl.pallas_call(
        body, name="mla_fwd", grid=(nbatch, nb),
        in_specs=[pl.BlockSpec((BLK, 1024), lambda b, i: (b * nb + i, 0)),
                  pl.BlockSpec((lp, 1024), lambda b, i: (b, 0)),
                  pl.BlockSpec((lp, BLK), lambda b, i: (b, 0)),
                  pl.BlockSpec((BLK, 512), lambda b, i: (b * nb + i, EV_GMLA_BLK // 4)),
                  tab, tab, tab],
        out_specs=[tile, tile, pl.BlockSpec((BLK, BLK), lambda b, i: (b * nb + i, 0))],
        out_shape=[jax.ShapeDtypeStruct((t, 512), F32), jax.ShapeDtypeStruct((t, 512), BF16),
                   jax.ShapeDtypeStruct((t, BLK), F32)],
        scratch_shapes=[pltpu.VMEM((BLK, 1024), BF16), heads, heads, heads],
        compiler_params=_params(("parallel", "arbitrary"), 48),
    )(qh, kvh, krot, proj, *tabs)


def _mla_bwd(qh, kvh, krot, proj, tabs, o, dog, lse, nbatch, nb):
    lp = nb * BLK
    t = nbatch * lp

    def body(q_ref, kv_ref, kr_ref, g_ref, c_ref, s1_ref, s2_ref, o_ref, dog_ref, lse_ref,
             dq_ref, dkv_ref, dkr_ref, dg_ref, qf_scr, do_scr, stat_scr, acc_scr):
        i = pl.program_id(1)

        @pl.when(i == 0)
        def _():
            dkv_ref[...] = jnp.zeros_like(dkv_ref)
            dkr_ref[...] = jnp.zeros_like(dkr_ref)

        row, lane = _iotas()
        lo_m = lane < HALF
        g = g_ref[...].astype(F32)
        sig = _sigmoid(g)
        dog_v = dog_ref[...]
        o_v = o_ref[...]
        dg_ref[...] = (dog_v * o_v * (sig * (1.0 + g * (1.0 - sig)))).astype(dg_ref.dtype)
        do = dog_v * g * sig
        do_o = do * o_v
        lse_blk = lse_ref[...]
        zero = jnp.zeros((BLK, BLK), F32)
        for h in range(8):
            hc = slice(h * BLK, (h + 1) * BLK)
            pc = slice((h // 2) * BLK, (h // 2 + 1) * BLK)
            qf_scr[:, hc] = (_rope(q_ref[:, hc].astype(F32), c_ref[...], s1_ref[...], s2_ref[...])
                             * MLA_SCALE).astype(BF16)
            dop = do[:, pc]
            do_src = dop if h % 2 else pltpu.roll(dop, HALF, 1)
            do_scr[:, hc] = jnp.where(lo_m, 0.0, do_src).astype(BF16)
            hm = (lane >= HALF) if h % 2 else lo_m
            stat_scr[h] = zero + jnp.sum(jnp.where(hm, do_o[:, pc], 0.0), axis=1, keepdims=True)
            stat_scr[8 + h] = zero + jnp.sum(jnp.where(lane == h, lse_blk, 0.0), axis=1, keepdims=True)
        acc_scr[...] = jnp.zeros_like(acc_scr)

        def block(off, width, edge):
            mask = _key_mask(i, off, width, strict=False) if edge else None
            lo_k = lax.broadcasted_iota(jnp.int32, (width, BLK), 1) < HALF
            krb = kr_ref[pl.ds(off, width), :]
            heads = range(8)
            hcs = [slice(h * BLK, (h + 1) * BLK) for h in heads]
            kvbs = [kv_ref[pl.ds(off, width), hc] for hc in hcs]
            qfs = [qf_scr[:, hc] for hc in hcs]
            dos = [do_scr[:, hc] for hc in hcs]
            scored = [_mla_scores(qfs[h], kvbs[h], krb, mask, lo_k) for h in heads]
            dps = [_dot_nt(dos[h], kvbs[h]) for h in heads]
            pbs, dss = [], []
            for h in heads:
                p = jnp.exp(scored[h][1] - _widen(stat_scr[8 + h], width))
                pbs.append(p.astype(BF16))
                dss.append((p * (dps[h] - _widen(stat_scr[h], width))).astype(BF16))
            dqs = [_dot(dss[h], scored[h][0]) for h in heads]
            dkfs = [_dot_tn(dss[h], qfs[h]) for h in heads]
            dvvs = [_dot_tn(pbs[h], dos[h]) for h in heads]
            dkr = jnp.zeros((width, BLK), F32)
            for h in heads:
                acc_scr[h] += dqs[h]
                dkv_ref[pl.ds(off, width), hcs[h]] += jnp.where(lo_k, dkfs[h], 0.0) + dvvs[h]
                dkr = dkr + jnp.where(lo_k, 0.0, dkfs[h])
            dkr_ref[pl.ds(off, width), :] += dkr

        _over_key_blocks(i + 1, block, reverse=False)
        for h in range(8):
            hc = slice(h * BLK, (h + 1) * BLK)
            dq_ref[:, hc] = _rope_t(acc_scr[h] * MLA_SCALE, c_ref[...], s1_ref[...], s2_ref[...]).astype(dq_ref.dtype)

    tile = lambda col: pl.BlockSpec((BLK, 512), lambda b, i: (b * nb + i, col))
    wide = pl.BlockSpec((BLK, 1024), lambda b, i: (b * nb + i, 0))
    full8 = pl.BlockSpec((lp, 1024), lambda b, i: (b, 0))
    full1 = pl.BlockSpec((lp, BLK), lambda b, i: (b, 0))
    tab = pl.BlockSpec((BLK, BLK), lambda b, i: (i, 0))
    return pl.pallas_call(
        body, name="mla_bwd", grid=(nbatch, nb),
        in_specs=[wide, full8, full1, tile(EV_GMLA_BLK // 4), tab, tab, tab, tile(0), tile(1),
                  pl.BlockSpec((BLK, BLK), lambda b, i: (b * nb + i, 0))],
        out_specs=[wide, full8, full1, tile(0)],
        out_shape=[jax.ShapeDtypeStruct((t, 1024), BF16), jax.ShapeDtypeStruct((t, 1024), F32),
                   jax.ShapeDtypeStruct((t, 128), F32), jax.ShapeDtypeStruct((t, 512), BF16)],
        scratch_shapes=[pltpu.VMEM((BLK, 1024), BF16), pltpu.VMEM((BLK, 1024), BF16),
                        pltpu.VMEM((16, BLK, BLK), F32), pltpu.VMEM((8, BLK, BLK), F32)],
        compiler_params=_params(("parallel", "arbitrary"), 56),
    )(qh, kvh, krot, proj, *tabs, o, dog, lse)


def _swa_setup(kk, i, k_refs, v_refs):
    row, lane = _iotas()
    own = (lane >= kk * HALF) & (lane < (kk + 1) * HALF)

    def dup(ref):
        x = ref[...].astype(F32)
        return jnp.where(own, x, pltpu.roll(x, HALF, 1)).astype(BF16)

    kcat = jnp.concatenate([dup(r) for r in k_refs], axis=0)
    vcat = jnp.concatenate([dup(r) for r in v_refs], axis=0)
    row2 = lax.broadcasted_iota(jnp.int32, (BLK, 2 * BLK), 0)
    lane2 = lax.broadcasted_iota(jnp.int32, (BLK, 2 * BLK), 1)
    is_meta = lane2 < BLK
    in_own = lane2 - BLK <= row2
    k_pos = jnp.where(is_meta, lane2, jnp.where(in_own, (i - 1) * BLK, (i - 2) * BLK) + lane2)
    d = i * BLK + row2 - k_pos
    mask = (d >= 0) & (k_pos >= jnp.where(is_meta, N_PAD, BLK))
    return lane, lane <= row, own, kcat, vcat, mask, d.astype(F32)


def _swa_fold(x, in_own):
    return jnp.concatenate([x[:, :BLK], jnp.where(in_own, x[:, 2 * BLK:], x[:, BLK:2 * BLK])], axis=1)


def _swa_unfold(x, in_own):
    w = x[:, BLK:]
    zero = jnp.zeros_like(w)
    return jnp.concatenate([x[:, :BLK], jnp.where(in_own, zero, w), jnp.where(in_own, w, zero)], axis=1)


def _swa_slope(kk, g_idx):
    return (2.0 ** (-(g_idx + 1) / 2.0)) * jnp.where(kk == 0, 1.0, 1.0 / 16.0)


def _swa_fwd(proj, sinks, nbatch, nb):
    lp = nb * BLK
    t = nbatch * lp

    def body(sink_ref, q_ref, ka, kb, kc, va, vb, vc, g_ref, o_ref, og_ref, lse_ref):
        kk = pl.program_id(1)
        i = pl.program_id(2)
        lane, in_own, own, kcat, vcat, mask, dist = _swa_setup(kk, i, (ka, kb, kc), (va, vb, vc))
        lo_m = lane < HALF
        heads = range(8)
        qms = []
        for h in heads:
            qp = q_ref[:, (h // 2) * BLK:(h // 2 + 1) * BLK]
            qms.append(jnp.where((lane >= HALF) if h % 2 else lo_m, qp, jnp.zeros_like(qp)) * SWA_SCALE)
        qks = [_dot_nt(qms[h], kcat) for h in heads]
        ps, ls, lses = [], [], []
        for h in heads:
            sink = sink_ref[kk, h]
            s = jnp.where(mask, _swa_fold(qks[h], in_own) - _swa_slope(kk, h) * dist, NEG)
            mx = jnp.maximum(jnp.max(s, axis=1, keepdims=True), sink)
            p = jnp.exp(s - mx)
            l = jnp.exp(sink - mx) + jnp.sum(p, axis=1, keepdims=True)
            ps.append(_swa_unfold(p.astype(BF16), in_own))
            ls.append(l)
            lses.append(mx + jnp.log(l))
        pvs = [_dot(ps[h], vcat) for h in heads]
        lse_out = jnp.zeros((BLK, BLK), F32)
        for m in range(4):
            cols = slice(m * BLK, (m + 1) * BLK)
            outp = jnp.where(lo_m, pvs[2 * m] / ls[2 * m], pvs[2 * m + 1] / ls[2 * m + 1])
            o_ref[:, cols] = outp
            g = g_ref[:, cols].astype(F32)
            og_ref[:, cols] = (outp * g * _sigmoid(g)).astype(og_ref.dtype)
            for h in (2 * m, 2 * m + 1):
                lse_out = lse_out + jnp.where(lane == h, lses[h], 0.0)
        lse_ref[...] = lse_out

    def kvspec(col, which):
        if which == 0:
            return pl.BlockSpec((BLK, BLK), lambda b, kk, i: (b * nb, col))
        if which == 1:
            return pl.BlockSpec((BLK, BLK), lambda b, kk, i: (b * nb + jnp.maximum(i - 1, 0), col))
        return pl.BlockSpec((BLK, BLK), lambda b, kk, i: (b * nb + i, col))

    wide = lambda c0: pl.BlockSpec((BLK, 512), lambda b, kk, i: (b * nb + i, c0 + kk))
    return pl.pallas_call(
        body, name="swa_fwd", grid=(nbatch, 2, nb),
        in_specs=[pl.BlockSpec(memory_space=pltpu.SMEM), wide(0),
                  kvspec(OD_K_BLK, 0), kvspec(OD_K_BLK, 1), kvspec(OD_K_BLK, 2),
                  kvspec(OD_V_BLK, 0), kvspec(OD_V_BLK, 1), kvspec(OD_V_BLK, 2), wide(2)],
        out_specs=[wide(0), wide(0), pl.BlockSpec((BLK, BLK), lambda b, kk, i: (b * nb + i, kk))],
        out_shape=[jax.ShapeDtypeStruct((t, 1024), F32), jax.ShapeDtypeStruct((t, 1024), BF16),
                   jax.ShapeDtypeStruct((t, 256), F32)],
        compiler_params=_params(("parallel", "parallel", "arbitrary")),
    )(sinks, proj, proj, proj, proj, proj, proj, proj, proj)


def _swa_bwd(proj, sinks, o, dog, lse, nbatch, nb):
    lp = nb * BLK
    t = nbatch * lp

    def body(sink_ref, q_ref, ka, kb, kc, va, vb, vc, g_ref, o_ref, dog_ref, lse_ref,
             dq_ref, dg_ref, dk_ref, dv_ref, dsink_ref):
        kk = pl.program_id(1)
        i = pl.program_id(2)

        @pl.when((kk == 0) & (i == 0))
        def _():
            dk_ref[...] = jnp.zeros_like(dk_ref)
            dv_ref[...] = jnp.zeros_like(dv_ref)

        @pl.when(i == 0)
        def _():
            dsink_ref[...] = jnp.zeros_like(dsink_ref)

        lane, in_own, own, kcat, vcat, mask, dist = _swa_setup(kk, i, (ka, kb, kc), (va, vb, vc))
        lo_m = lane < HALF
        row8 = lax.broadcasted_iota(jnp.int32, (8, BLK), 0)
        lse_blk = lse_ref[...]
        heads = range(8)
        qms, doms, deltas, lse_hs = [], [], [], []
        for m in range(4):
            cols = slice(m * BLK, (m + 1) * BLK)
            qp = q_ref[:, cols]
            g = g_ref[:, cols].astype(F32)
            sig = _sigmoid(g)
            dog_v = dog_ref[:, cols]
            o_v = o_ref[:, cols]
            dg_ref[:, cols] = (dog_v * o_v * (sig * (1.0 + g * (1.0 - sig)))).astype(dg_ref.dtype)
            do = dog_v * g * sig
            do_o = do * o_v
            dob = do.astype(BF16)
            for h in (2 * m, 2 * m + 1):
                hm = (lane >= HALF) if h % 2 else lo_m
                qms.append(jnp.where(hm, qp, jnp.zeros_like(qp)) * SWA_SCALE)
                doms.append(jnp.where(hm, dob, jnp.zeros_like(dob)))
                deltas.append(jnp.sum(jnp.where(hm, do_o, 0.0), axis=1, keepdims=True))
                lse_hs.append(jnp.sum(jnp.where(lane == h, lse_blk, 0.0), axis=1, keepdims=True))
        qks = [_dot_nt(qms[h], kcat) for h in heads]
        dps = [_dot_nt(doms[h], vcat) for h in heads]
        pbs, dss = [], []
        dsink = jnp.zeros((8, BLK), F32)
        for h in heads:
            s = jnp.where(mask, _swa_fold(qks[h], in_own) - _swa_slope(kk, h) * dist, NEG)
            p = jnp.exp(s - lse_hs[h])
            pbs.append(_swa_unfold(p.astype(BF16), in_own))
            dss.append(_swa_unfold((p * (_swa_fold(dps[h], in_own) - deltas[h])).astype(BF16), in_own))
            tot = jnp.sum(-jnp.exp(sink_ref[kk, h] - lse_hs[h]) * deltas[h], axis=0, keepdims=True)
            dsink = dsink + jnp.where(row8 == h, tot, 0.0)
        dsink_ref[...] += dsink
        dqs = [_dot(dss[h], kcat) for h in heads]
        dks = [_dot_tn(dss[h], qms[h]) for h in heads]
        dvs = [_dot_tn(pbs[h], doms[h]) for h in heads]
        for m in range(4):
            dq_ref[:, m * BLK:(m + 1) * BLK] = (
                jnp.where(lo_m, dqs[2 * m], dqs[2 * m + 1]) * SWA_SCALE).astype(dq_ref.dtype)
        dk = dks[0]
        dv = dvs[0]
        for h in range(1, 8):
            dk = dk + dks[h]
            dv = dv + dvs[h]
        offs = [0, pl.multiple_of(jnp.maximum(i - 1, 0) * BLK, BLK), pl.multiple_of(i * BLK, BLK)]
        for x in range(3):
            rows = slice(x * BLK, (x + 1) * BLK)
            dkx, dvx = dk[rows], dv[rows]
            dk_ref[pl.ds(offs[x], BLK), :] += jnp.where(own, dkx + pltpu.roll(dkx, HALF, 1), 0.0)
            dv_ref[pl.ds(offs[x], BLK), :] += jnp.where(own, dvx + pltpu.roll(dvx, HALF, 1), 0.0)

    def kvspec(col, which):
        if which == 0:
            return pl.BlockSpec((BLK, BLK), lambda b, kk, i: (b * nb, col))
        if which == 1:
            return pl.BlockSpec((BLK, BLK), lambda b, kk, i: (b * nb + jnp.maximum(i - 1, 0), col))
        return pl.BlockSpec((BLK, BLK), lambda b, kk, i: (b * nb + i, col))

    wide = lambda c0: pl.BlockSpec((BLK, 512), lambda b, kk, i: (b * nb + i, c0 + kk))
    full = pl.BlockSpec((lp, BLK), lambda b, kk, i: (b, 0))
    return pl.pallas_call(
        body, name="swa_bwd", grid=(nbatch, 2, nb),
        in_specs=[pl.BlockSpec(memory_space=pltpu.SMEM), wide(0),
                  kvspec(OD_K_BLK, 0), kvspec(OD_K_BLK, 1), kvspec(OD_K_BLK, 2),
                  kvspec(OD_V_BLK, 0), kvspec(OD_V_BLK, 1), kvspec(OD_V_BLK, 2), wide(2),
                  wide(0), wide(0), pl.BlockSpec((BLK, BLK), lambda b, kk, i: (b * nb + i, kk))],
        out_specs=[wide(0), wide(0), full, full,
                   pl.BlockSpec((8, BLK), lambda b, kk, i: (b * 2 + kk, 0))],
        out_shape=[jax.ShapeDtypeStruct((t, 1024), BF16), jax.ShapeDtypeStruct((t, 1024), BF16),
                   jax.ShapeDtypeStruct((t, 128), F32), jax.ShapeDtypeStruct((t, 128), F32),
                   jax.ShapeDtypeStruct((nbatch * 16, BLK), F32)],
        compiler_params=_params(("parallel", "arbitrary", "arbitrary")),
    )(sinks, proj, proj, proj, proj, proj, proj, proj, proj, o, dog, lse)


def _rope_tables(lp):
    pos = (jnp.arange(lp) - N_PAD).astype(F32)
    inv = ROPE_BASE ** (-jnp.arange(16, dtype=F32) / 16.0)
    ang = pos[:, None] * inv[None, :]
    cos, sin = jnp.cos(ang), jnp.sin(ang)
    z16 = jnp.zeros((lp, 16), F32)
    c = jnp.concatenate([jnp.ones((lp, 64), F32), cos, cos, jnp.zeros((lp, 32), F32)], axis=1)
    s1 = jnp.concatenate([jnp.zeros((lp, 64), F32), -sin, z16, jnp.zeros((lp, 32), F32)], axis=1)
    s2 = jnp.concatenate([jnp.zeros((lp, 64), F32), z16, sin, jnp.zeros((lp, 32), F32)], axis=1)
    return c, s1, s2


def _local_step(h0, tgt, norm_g, final_g, gq, gkv, sinks, w_ie, late_shards, nbatch, nb):
    lp = nb * BLK
    tabs = _rope_tables(lp)
    g0, g1 = norm_g[0:1], norm_g[1:2]
    sinks2 = sinks.reshape(2, 8)

    proj_e, hn0 = _norm_mm(h0, g0, w_ie, "proj_even")
    o_sb, og_sb, cm, gathered = _sb_fwd(proj_e, [late_shards[name] for name in _LATE], nbatch, nb)
    full = {name: _unchunk(name, blk) for name, blk in zip(_LATE, gathered) if _BIG[name][2] == 0 or name == "ev_w_ukv"}
    w_ukv, w_oe, w_oo = full["ev_w_ukv"], full["ev_w_out"], full["od_w_out"]
    w_uq = _from_blocks(gathered[_LATE.index("ev_w_uq")], _UQ_PLAN)
    w_io = _from_blocks(gathered[_LATE.index("od_w_in")], _OD_IN_PLAN)
    cqn, ckvn, krot = _mla_prep(proj_e, gq, gkv, tabs, nbatch, nb)
    qh = _mm(cqn, w_uq, "nn", "mla_uq", out_dtype=BF16)
    kvh = _mm(ckvn, w_ukv, "nn", "mla_ukv", out_dtype=BF16)
    o_mla, og_mla, lse_m = _mla_fwd(qh, kvh, krot, proj_e, tabs, nbatch, nb)
    h1 = _mm([og_sb, og_mla], w_oe, "nn", "out_even", add=h0)
    proj_o, hn1 = _norm_mm(h1, g1, w_io, "proj_odd")
    o_o, og_o, lse_o = _swa_fwd(proj_o, sinks2, nbatch, nb)
    h2 = _mm(og_o, w_oo, "nn", "out_odd", add=h1)
    dh2, lossv, d_final_g = _final(h2, tgt, final_g, nbatch, nb)

    dog_o = _mm(dh2, w_oo, "nt", "d_out_odd")
    d_w_oo, = _mm_tn(og_o, [dh2], "dw_out_odd")
    dq_o, dg_o, dk_o, dv_o, dsink = _swa_bwd(proj_o, sinks2, o_o, dog_o, lse_o, nbatch, nb)
    dproj_o = [dq_o, dg_o, dk_o, dv_o]
    dh1, d_g1 = _mm_norm_bwd(dproj_o, w_io, h1, g1, dh2, "d_proj_odd")
    dw_q, dw_g, dw_k, dw_v = _mm_tn(hn1, dproj_o, "dw_proj_odd")

    dog_e = _mm(dh1, w_oe, "nt", "d_out_even")
    d_w_oe_sb, = _mm_tn(og_sb, [dh1], "dw_out_even_sb")
    d_w_oe_mla, = _mm_tn(og_mla, [dh1], "dw_out_even_mla")
    early = dict(od_w_in=_to_chunks([(dw_q, 0, 1024), (dw_k, 0, 128), (dw_v, 0, 128), (dw_g, 0, 1024)], 288),
                 od_w_out=_chunk("od_w_out", d_w_oo),
                 ev_w_out=_chunk("ev_w_out", jnp.concatenate([d_w_oe_sb, d_w_oe_mla], axis=0)))
    dq_sb, dk_sb, dv_sb, dg_sb, received = _sb_bwd(
        proj_e, o_sb, dog_e, cm, [early[name] for name in _EARLY_GRADS], nbatch, nb)
    dqh, dkvh, dkrot, dg_mla = _mla_bwd(qh, kvh, krot, proj_e, tabs, o_mla, dog_e, lse_m, nbatch, nb)
    dcqn = _mm(dqh, w_uq, "nt", "d_mla_uq")
    d_w_uq, = _mm_tn(cqn, [dqh], "dw_mla_uq")
    dckvn = _mm(dkvh, w_ukv, "nt", "d_mla_ukv")
    d_w_ukv, = _mm_tn(ckvn, [dkvh], "dw_mla_ukv")
    dmid, d_gq, d_gkv = _mla_prep_bwd(proj_e, gq, gkv, tabs, dcqn, dckvn, dkrot, nbatch, nb)
    dproj_e = [dq_sb, dk_sb, dv_sb, dg_sb, dmid, dg_mla]
    dh0, d_g0 = _mm_norm_bwd(dproj_e, w_ie, h0, g0, dh1, "d_proj_even")
    dw_e = _mm_tn(hn0, dproj_e, "dw_proj_even")

    d_sinks = dsink.reshape(nbatch, 2, 8, BLK)[:, :, :, 0].sum(axis=0).reshape(1, 16)
    d_norm_g = jnp.concatenate([d_g0, d_g1], axis=0)
    ev_in_segments = [(dw_e[0], 0, 512), (dw_e[1], 0, 512), (dw_e[2], 0, 512), (dw_e[3], 0, 512),
                      (dw_e[4], 0, 384), (dw_e[4], 448, 480), (dw_e[5], 0, 512)]
    last = [_to_chunks(ev_in_segments, 372), _to_chunks([(d_w_uq, 128 * h, 128 * h + 96) for h in range(8)], 96),
            _chunk("ev_w_ukv", d_w_ukv)]
    return dict(lossv=lossv, dh0=dh0, norm_g=d_norm_g, final_g=d_final_g, gq=d_gq, gkv=d_gkv, sinks=d_sinks,
                last=last, received=received)


def _from_blocks(blocks, plan):
    shard = blocks.shape[2]
    parts = []
    for item in plan:
        if isinstance(item, int):
            parts.append(jnp.zeros((blocks.shape[1], item), blocks.dtype))
            continue
        lo, hi = item
        for k in range(N_DEV):
            a, b = max(lo, k * shard), min(hi, (k + 1) * shard)
            if a < b:
                parts.append(blocks[k][:, a - k * shard:b - k * shard])
    return jnp.concatenate(parts, axis=1)


def _to_chunks(segments, shard):
    chunks, start = [[] for _ in range(N_DEV)], 0
    for arr, lo, hi in segments:
        for k in range(N_DEV):
            a, b = max(start, k * shard), min(start + hi - lo, (k + 1) * shard)
            if a < b:
                chunks[k].append(arr[:, lo + a - start:lo + b - start].astype(BF16))
        start += hi - lo
    return jnp.stack([jnp.concatenate(c, axis=1) for c in chunks], axis=0)


_EV_IN_PLAN = [(0, 2432), 64, (2432, 2464), 32, (2464, 2976)]
_UQ_PLAN = [item for h in range(8) for item in ((96 * h, 96 * h + 96), 32)]
_OD_IN_PLAN = [(0, 1024), (1280, 2304), (1024, 1280)]


_BIG = dict(ev_w_in=(1024, 2976, 1), ev_w_uq=(256, 768, 1), ev_w_ukv=(128, 1024, 1),
            ev_w_out=(1024, 1024, 0), od_w_in=(1024, 2304, 1), od_w_out=(1024, 1024, 0))
_LATE = ("ev_w_uq", "ev_w_ukv", "ev_w_out", "od_w_in", "od_w_out")
_EARLY_GRADS = ("od_w_in", "od_w_out", "ev_w_out")
_LAST_GRADS = ("ev_w_in", "ev_w_uq", "ev_w_ukv")


def _unchunk(name, blk):
    rows, cols, axis = _BIG[name]
    return blk.transpose(1, 0, 2).reshape(rows, cols) if axis == 1 else blk.reshape(rows, cols)


def _chunk(name, g):
    rows, cols, axis = _BIG[name]
    g = g.astype(BF16)
    return g.reshape(rows, N_DEV, cols // N_DEV).transpose(1, 0, 2) if axis == 1 else g.reshape(N_DEV, rows // N_DEV, cols)


def _all_gather(shards, name):
    n = len(shards)

    def body(*refs):
        xs, outs = refs[:n], refs[n:2 * n]
        send_sems, recv_sems, local_sems = refs[2 * n:]
        x, y, c = lax.axis_index("x"), lax.axis_index("y"), lax.axis_index("c")
        me, sibling = (x, y, c), (x, y, 1 - c)
        chips = [(1 - x, y), (x, 1 - y), (1 - x, 1 - y)]
        arrays = range(n)

        def copy(k, a, block, to, from_input=False):
            px, py, pc = block
            dst = outs[a].at[4 * px + 2 * py + pc]
            return pltpu.make_async_remote_copy(
                src_ref=xs[a] if from_input else dst, dst_ref=dst,
                send_sem=send_sems.at[k, a], recv_sem=recv_sems.at[k, a],
                device_id=to, device_id_type=pl.DeviceIdType.MESH)

        mine = [pltpu.make_async_copy(xs[a], outs[a].at[4 * x + 2 * y + c], local_sems.at[a]) for a in arrays]
        for cp in mine:
            cp.start()
        first = [copy(0, a, me, sibling, True) for a in arrays]
        for j, chip in enumerate(chips):
            first += [copy(1 + j, a, me, (*chip, c), True) for a in arrays]
        for cp in first:
            cp.start()
        passed = []
        for j, chip in enumerate(chips):
            for a in arrays:
                copy(1 + j, a, (*chip, c), me).wait_recv()
                passed.append(copy(4 + j, a, (*chip, c), sibling))
                passed[-1].start()
        for a in arrays:
            copy(0, a, sibling, me).wait_recv()
        for j, chip in enumerate(chips):
            for a in arrays:
                copy(4 + j, a, (*chip, 1 - c), me).wait_recv()
        for cp in first + passed:
            cp.wait_send()
        for cp in mine:
            cp.wait()

    hbm = pl.BlockSpec(memory_space=pl.ANY)
    return pl.pallas_call(
        body, name=name,
        out_shape=[jax.ShapeDtypeStruct((N_DEV,) + s.shape, s.dtype) for s in shards],
        in_specs=[hbm] * n, out_specs=[hbm] * n,
        scratch_shapes=[pltpu.SemaphoreType.DMA((7, n)), pltpu.SemaphoreType.DMA((7, n)),
                        pltpu.SemaphoreType.DMA((n,))],
    )(*shards)


def _exchange_sum(chunked, received, name):
    n, m = len(chunked), len(received)
    arrs = list(chunked) + list(received)

    def body(*refs):
        ins, outs = refs[:n + m], refs[n + m:2 * (n + m)]
        bufs = refs[2 * (n + m):3 * (n + m)]
        send_sems, recv_sems, local_sems, load_sems = refs[3 * (n + m):]
        loads = [pltpu.make_async_copy(ins[n + a], bufs[n + a], load_sems.at[a]) for a in range(m)]
        for cp in loads:
            cp.start()
        local, sends, recvs = _direct_exchange(ins[:n], bufs[:n], send_sems, recv_sems, local_sems, chunked=True)
        for cp in local + sends:
            cp.start()
        for a, cp in enumerate(loads):
            cp.wait()
            _sum_slots(bufs[n + a], outs[n + a])
        for cp in local:
            cp.wait()
        for cp in recvs:
            cp.wait_recv()
        for cp in sends:
            cp.wait_send()
        for a in range(n):
            _sum_slots(bufs[a], outs[a])

    hbm = pl.BlockSpec(memory_space=pl.ANY)
    vm = pl.BlockSpec(memory_space=pltpu.VMEM)
    return pl.pallas_call(
        body, name=name,
        out_shape=[jax.ShapeDtypeStruct(a.shape[1:], F32) for a in arrs],
        in_specs=[hbm] * (n + m), out_specs=[vm] * (n + m),
        scratch_shapes=[pltpu.VMEM(a.shape, a.dtype) for a in arrs] + _exchange_sems(n)
        + [pltpu.SemaphoreType.DMA((max(m, 1),))],
        compiler_params=pltpu.CompilerParams(vmem_limit_bytes=48 << 20),
    )(*arrs)


def _sum_slots(buf, out):
    rows = buf.shape[1]

    def add(sl):
        acc = buf[(0,) + sl].astype(F32)
        for k in range(1, N_DEV):
            acc = acc + buf[(k,) + sl].astype(F32)
        out[sl] = acc

    if rows > BLK and rows % BLK == 0:
        def step(r, carry):
            add((pl.ds(pl.multiple_of(r * BLK, BLK), BLK), slice(None)))
            return carry

        lax.fori_loop(0, rows // BLK, step, 0)
    else:
        add((slice(None), slice(None)))


def _adamw(ws, gs, ms, vs, name, steps):
    n = len(ws)

    def body(*refs):
        ins, outs = refs[:4 * n], refs[4 * n:]
        for k in range(n):
            w_ref, g_ref, m_ref, v_ref = ins[4 * k:4 * k + 4]
            d_ref, nm_ref, nv_ref = outs[3 * k:3 * k + 3]
            g = g_ref[...]
            m = ADAM_B1 * m_ref[...] + (1.0 - ADAM_B1) * g
            v = ADAM_B2 * v_ref[...] + (1.0 - ADAM_B2) * (g * g)
            m_hat = m / (1.0 - ADAM_B1 ** ADAM_STEP)
            v_hat = v / (1.0 - ADAM_B2 ** ADAM_STEP)
            d_ref[...] = -ADAM_LR * (m_hat / (jnp.sqrt(v_hat) + ADAM_EPS) + ADAM_WD * w_ref[...])
            nm_ref[...] = m
            nv_ref[...] = v

    args, out_shape, in_specs, out_specs = [], [], [], []
    for k in range(n):
        rows, cols = ws[k].shape
        spec = pl.BlockSpec((rows // steps, cols), lambda i: (i, 0))
        args += [ws[k], gs[k], ms[k], vs[k]]
        in_specs += [spec] * 4
        out_specs += [spec] * 3
        out_shape += [jax.ShapeDtypeStruct(ws[k].shape, F32)] * 3
    outs = pl.pallas_call(
        body, name=name, grid=(steps,), out_shape=out_shape, in_specs=in_specs, out_specs=out_specs,
        compiler_params=_params(("parallel",)),
    )(*args)
    return [tuple(outs[3 * k:3 * k + 3]) for k in range(n)]


def kernel(x, meta, norm_g, final_g, ev_w_in, ev_q_norm_g, ev_kv_norm_g, ev_w_uq, ev_w_ukv, ev_w_out, od_w_in, od_sinks, od_w_out, loss_target, m_meta, m_norm_g, m_final_g, m_ev_w_in, m_ev_q_norm_g, m_ev_kv_norm_g, m_ev_w_uq, m_ev_w_ukv, m_ev_w_out, m_od_w_in, m_od_sinks, m_od_w_out, v_meta, v_norm_g, v_final_g, v_ev_w_in, v_ev_q_norm_g, v_ev_kv_norm_g, v_ev_w_uq, v_ev_w_ukv, v_ev_w_out, v_od_w_in, v_od_sinks, v_od_w_out):
    nbatch, seq, d = x.shape
    nb = seq // BLK + 1
    lp = nb * BLK
    shards = dict(ev_w_in=ev_w_in[0], ev_w_uq=ev_w_uq[0], ev_w_ukv=ev_w_ukv[0], ev_w_out=ev_w_out[0],
                  od_w_in=od_w_in[0], od_w_out=od_w_out[0])

    w_ie_blocks, meta_blocks = _all_gather([shards["ev_w_in"].astype(BF16), meta], "gather_weights")
    meta_full = meta_blocks.transpose(1, 0, 2).reshape(N_META, d)

    head = jnp.concatenate([jnp.zeros((N_PAD, d), F32), meta_full], axis=0)
    h0 = jnp.concatenate([jnp.broadcast_to(head[None], (nbatch, BLK, d)), x], axis=1).reshape(nbatch * lp, d)
    grads = _local_step(
        h0, loss_target.reshape(nbatch * seq, d), norm_g, final_g.reshape(1, d), ev_q_norm_g, ev_kv_norm_g,
        od_sinks, _from_blocks(w_ie_blocks, _EV_IN_PLAN),
        {name: shards[name].astype(BF16) for name in _LATE}, nbatch, nb)
    dh0 = grads["dh0"].reshape(nbatch, lp, d)
    grad_x = dh0[:, BLK:]

    d_meta = dh0[:, N_PAD:BLK].sum(axis=0).reshape(N_META, N_DEV, BLK).transpose(1, 0, 2)
    pad = lambda a, n: jnp.concatenate([a.reshape(1, -1), jnp.zeros((1, n - a.size), F32)], axis=1)
    loss_part = (0.5 / d * jnp.sum(grads["lossv"])).reshape(1, 1)
    rep = jnp.concatenate([grads["norm_g"].reshape(1, -1), grads["final_g"], grads["gq"], pad(loss_part, 256),
                           pad(grads["gkv"], 256), pad(grads["sinks"], 256)], axis=1).reshape(32, BLK)
    small = jnp.concatenate([d_meta, jnp.broadcast_to(rep[None], (N_DEV, 32, BLK))], axis=1)
    reduced = _exchange_sum(grads["last"] + [small], grads["received"], "reduce_grads")
    g_shard = dict(zip(_LAST_GRADS + ("small",) + _EARLY_GRADS, reduced))
    red_small = g_shard.pop("small")
    rep = red_small[N_META:].reshape(1, -1)
    loss = rep[0, 3 * d + 256]
    g_small = dict(meta=red_small[:N_META], norm_g=rep[:, :2 * d].reshape(2, d), final_g=rep[:, 2 * d:3 * d],
                   ev_q_norm_g=rep[:, 3 * d:3 * d + 256], ev_kv_norm_g=rep[:, 3 * d + 512:3 * d + 640],
                   od_sinks=rep[:, 3 * d + 768:3 * d + 784])

    names = ["meta", "norm_g", "final_g", "ev_w_in", "ev_q_norm_g", "ev_kv_norm_g", "ev_w_uq", "ev_w_ukv",
             "ev_w_out", "od_w_in", "od_sinks", "od_w_out"]
    given = dict(meta=(meta, m_meta, v_meta), norm_g=(norm_g, m_norm_g, v_norm_g),
                 final_g=(final_g, m_final_g, v_final_g), ev_w_in=(ev_w_in, m_ev_w_in, v_ev_w_in),
                 ev_q_norm_g=(ev_q_norm_g, m_ev_q_norm_g, v_ev_q_norm_g),
                 ev_kv_norm_g=(ev_kv_norm_g, m_ev_kv_norm_g, v_ev_kv_norm_g),
                 ev_w_uq=(ev_w_uq, m_ev_w_uq, v_ev_w_uq), ev_w_ukv=(ev_w_ukv, m_ev_w_ukv, v_ev_w_ukv),
                 ev_w_out=(ev_w_out, m_ev_w_out, v_ev_w_out), od_w_in=(od_w_in, m_od_w_in, v_od_w_in),
                 od_sinks=(od_sinks, m_od_sinks, v_od_sinks), od_w_out=(od_w_out, m_od_w_out, v_od_w_out))
    ws, gs, ms, vs = [], [], [], []
    for name in names:
        g2 = g_shard[name] if name in g_shard else g_small[name]
        w, m, v = given[name]
        ws.append(w.reshape(g2.shape))
        ms.append(m.reshape(g2.shape))
        vs.append(v.reshape(g2.shape))
        gs.append(g2)
    big = [k for k, name in enumerate(names) if name in _BIG]
    small = [k for k, name in enumerate(names) if name not in _BIG]
    pick = lambda xs, ks: [xs[k] for k in ks]
    upd_big = _adamw(pick(ws, big), pick(gs, big), pick(ms, big), pick(vs, big), "adamw", N_DEV)
    upd_small = _adamw(pick(ws, small), pick(gs, small), pick(ms, small), pick(vs, small), "adamw_small", 1)
    upd = dict(zip(big + small, upd_big + upd_small))
    shape_of = {name: given[name][0].shape for name in names}
    grads_out = [gs[k].reshape(shape_of[n]) for k, n in enumerate(names)]
    deltas = [upd[k][0].reshape(shape_of[n]) for k, n in enumerate(names)]
    new_m = [upd[k][1].reshape(shape_of[n]) for k, n in enumerate(names)]
    new_v = [upd[k][2].reshape(shape_of[n]) for k, n in enumerate(names)]
    return (loss, grad_x, *grads_out, *deltas, *new_m, *new_v)
```

```python
import jax
import jax.numpy as jnp
from jax import lax
from jax.experimental import pallas as pl
from jax.experimental.pallas import tpu as pltpu

F32 = jnp.float32
BF16 = jnp.bfloat16

D_MODEL = 1024
N_META = 16
BLK = 128
HALF = 64
N_PAD = BLK - N_META
NORM_EPS = 1e-6
NEG = -1e30
N_DEV = 8

SB_SCALE = 64 ** -0.5
MLA_SCALE = 96 ** -0.5
SWA_SCALE = 64 ** -0.5
ROPE_BASE = 10000.0

EV_IN_PAD = 3072
EV_MID_BLK = 4
EV_GMLA_BLK = 20
OD_K_BLK = 16
OD_V_BLK = 17

ADAM_LR = 0.001
ADAM_B1 = 0.9
ADAM_B2 = 0.999
ADAM_EPS = 1e-08
ADAM_WD = 0.01
ADAM_STEP = 10


def _dot(a, b):
    return lax.dot_general(a, b, (((1,), (0,)), ((), ())), preferred_element_type=F32)


def _dot_nt(a, b):
    return lax.dot_general(a, b, (((1,), (1,)), ((), ())), preferred_element_type=F32)


def _dot_tn(a, b):
    return lax.dot_general(a, b, (((0,), (0,)), ((), ())), preferred_element_type=F32)


def _sigmoid(x):
    return 1.0 / (1.0 + jnp.exp(-x))


def _iotas():
    row = lax.broadcasted_iota(jnp.int32, (BLK, BLK), 0)
    lane = lax.broadcasted_iota(jnp.int32, (BLK, BLK), 1)
    return row, lane


WIDE = 2 * BLK
HEAD_GROUPS = (range(0, 8),)


def _key_mask(i, first_key, width, strict):
    t_pos = i * BLK + lax.broadcasted_iota(jnp.int32, (BLK, width), 0)
    s_pos = first_key + lax.broadcasted_iota(jnp.int32, (BLK, width), 1)
    seen = (s_pos < t_pos) if strict else (s_pos <= t_pos)
    return seen & (s_pos >= N_PAD)


def _widen(x, width):
    return x if width == BLK else jnp.concatenate([x] * (width // BLK), axis=1)


def _over_key_blocks(n, block, reverse):
    pairs = n // 2
    last = pl.multiple_of((n - 1) * BLK, BLK)

    def step(jj, carry):
        jp = (pairs - 1 - jj) if reverse else jj
        off = pl.multiple_of(jp * WIDE, WIDE)
        edge = (jp == 0) | (jp == pairs - 1)
        pl.when(edge)(lambda: block(off, WIDE, True))
        pl.when(jnp.logical_not(edge))(lambda: block(off, WIDE, False))
        return carry

    if reverse:
        pl.when(n % 2 == 1)(lambda: block(last, BLK, True))
        lax.fori_loop(0, pairs, step, 0)
    else:
        lax.fori_loop(0, pairs, step, 0)
        pl.when(n % 2 == 1)(lambda: block(last, BLK, True))


def _sel(mask, x, fill):
    return x if mask is None else jnp.where(mask, x, fill)


def _rope(x, c, s1, s2):
    return x * c + pltpu.roll(x, BLK - 16, 1) * s1 + pltpu.roll(x, 16, 1) * s2


def _rope_t(x, c, s1, s2):
    return x * c - pltpu.roll(x, BLK - 16, 1) * s1 - pltpu.roll(x, 16, 1) * s2


def _params(sem, vmem_mb=None):
    kw = dict(dimension_semantics=sem)
    if vmem_mb is not None:
        kw["vmem_limit_bytes"] = vmem_mb << 20
    return pltpu.CompilerParams(**kw)


def _row_tile(t, cands):
    for c in cands:
        if t % c == 0:
            return c
    raise ValueError(t)


def _mm(a, w, mode, name, add=None, out_dtype=F32):
    pieces = list(a) if isinstance(a, (list, tuple)) else [a]
    m = pieces[0].shape[0]
    n = w.shape[1] if mode == "nn" else w.shape[0]
    tm = _row_tile(m, (544, 256, 128) if n <= 1024 else (256, 128))
    widths = [p.shape[1] for p in pieces]
    offs = [sum(widths[:i]) for i in range(len(widths))]

    def body(*refs):
        p_refs = refs[:len(pieces)]
        w_ref = refs[len(pieces)]
        o_ref = refs[-1]
        acc = None
        for p_ref, off, wd in zip(p_refs, offs, widths):
            x = p_ref[...].astype(BF16)
            part = _dot(x, w_ref[off:off + wd, :]) if mode == "nn" else _dot_nt(x, w_ref[:, off:off + wd])
            acc = part if acc is None else acc + part
        if add is not None:
            acc = acc + refs[len(pieces) + 1][...]
        o_ref[...] = acc.astype(o_ref.dtype)

    in_specs = [pl.BlockSpec((tm, wd), lambda i: (i, 0)) for wd in widths]
    in_specs.append(pl.BlockSpec(w.shape, lambda i: (0, 0)))
    args = pieces + [w]
    if add is not None:
        in_specs.append(pl.BlockSpec((tm, n), lambda i: (i, 0)))
        args.append(add)
    return pl.pallas_call(
        body, name=name, grid=(m // tm,), in_specs=in_specs,
        out_specs=pl.BlockSpec((tm, n), lambda i: (i, 0)),
        out_shape=jax.ShapeDtypeStruct((m, n), out_dtype),
        compiler_params=_params(("parallel",), 48),
    )(*args)


def _mm_tn(x, pieces, name):
    t, k = x.shape
    tt = _row_tile(t, (544, 256, 128))
    widths = [p.shape[1] for p in pieces]

    def body(*refs):
        x_ref = refs[0]
        d_refs = refs[1:1 + len(pieces)]
        o_refs = refs[1 + len(pieces):]

        @pl.when(pl.program_id(0) == 0)
        def _():
            for o_ref in o_refs:
                o_ref[...] = jnp.zeros_like(o_ref)

        xb = x_ref[...].astype(BF16)
        for d_ref, o_ref in zip(d_refs, o_refs):
            o_ref[...] += _dot_tn(xb, d_ref[...].astype(BF16))

    return pl.pallas_call(
        body, name=name, grid=(t // tt,),
        in_specs=[pl.BlockSpec((tt, k), lambda i: (i, 0))] + [pl.BlockSpec((tt, wd), lambda i: (i, 0)) for wd in widths],
        out_specs=[pl.BlockSpec((k, wd), lambda i: (0, 0)) for wd in widths],
        out_shape=[jax.ShapeDtypeStruct((k, wd), F32) for wd in widths],
        compiler_params=_params(("arbitrary",), 56),
    )(x, *pieces)


def _norm_mm(h, g, w, name):
    t, d = h.shape
    n = w.shape[1]
    tr = _row_tile(t, (544, 256, 128))

    def body(h_ref, g_ref, w_ref, o_ref, hn_ref):
        x = h_ref[...]
        r = lax.rsqrt(jnp.mean(x * x, axis=1, keepdims=True) + NORM_EPS)
        hn = (x * r * g_ref[...]).astype(BF16)
        hn_ref[...] = hn
        o_ref[...] = _dot(hn, w_ref[...]).astype(o_ref.dtype)

    row = lambda width: pl.BlockSpec((tr, width), lambda i: (i, 0))
    return pl.pallas_call(
        body, name=name, grid=(t // tr,),
        in_specs=[row(d), pl.BlockSpec((1, d), lambda i: (0, 0)), pl.BlockSpec(w.shape, lambda i: (0, 0))],
        out_specs=[row(n), row(d)],
        out_shape=[jax.ShapeDtypeStruct((t, n), BF16), jax.ShapeDtypeStruct((t, d), BF16)],
        compiler_params=_params(("parallel",), 48),
    )(h, g, w)


def _mm_norm_bwd(pieces, w, h, g, dres, name, chunked=()):
    t, d = h.shape
    tr = _row_tile(t, (544, 256, 128))
    steps = t // tr
    widths = [p.shape[1] for p in pieces]
    offs = [sum(widths[:i]) for i in range(len(widths))]
    np_, ns = len(pieces), len(chunked)

    def body(*refs):
        p_refs = refs[:np_]
        w_ref, h_ref, g_ref, dres_ref = refs[np_:np_ + 4]
        chunk_refs = refs[np_ + 4:np_ + 4 + ns]
        dh_ref, dg_ref = refs[np_ + 4 + ns:np_ + 6 + ns]
        received_refs = refs[np_ + 6 + ns:np_ + 6 + 2 * ns]
        i = pl.program_id(0)
        if ns:
            _exchange_beside(i == 0, i == steps - 1, chunk_refs, received_refs, refs[np_ + 6 + 2 * ns:], chunked=True)

        @pl.when(i == 0)
        def _():
            dg_ref[...] = jnp.zeros_like(dg_ref)

        dy = None
        for p_ref, off, wd in zip(p_refs, offs, widths):
            part = _dot_nt(p_ref[...].astype(BF16), w_ref[:, off:off + wd])
            dy = part if dy is None else dy + part
        x = h_ref[...]
        r = lax.rsqrt(jnp.mean(x * x, axis=1, keepdims=True) + NORM_EPS)
        nx = x * r
        dn = dy * g_ref[...]
        dh_ref[...] = dres_ref[...] + r * (dn - nx * jnp.mean(dn * nx, axis=1, keepdims=True))
        dg_ref[...] += jnp.sum(dy * nx, axis=0, keepdims=True)

    row = lambda width: pl.BlockSpec((tr, width), lambda i: (i, 0))
    vec = pl.BlockSpec((1, d), lambda i: (0, 0))
    hbm = pl.BlockSpec(memory_space=pl.ANY)
    outs = pl.pallas_call(
        body, name=name, grid=(steps,),
        in_specs=[row(wd) for wd in widths] + [pl.BlockSpec(w.shape, lambda i: (0, 0)), row(d), vec, row(d)]
        + [hbm] * ns,
        out_specs=[row(d), vec] + [hbm] * ns,
        out_shape=[jax.ShapeDtypeStruct((t, d), F32), jax.ShapeDtypeStruct((1, d), F32)]
        + [jax.ShapeDtypeStruct(a.shape, a.dtype) for a in chunked],
        scratch_shapes=_exchange_sems(ns) if ns else [],
        compiler_params=_params(("arbitrary",), 56),
    )(*pieces, w, h, g, dres, *chunked)
    return outs[0], outs[1], list(outs[2:])


def _final(h2, tgt, g, nbatch, nb):
    t, d = h2.shape

    def body(h_ref, t_ref, g_ref, dh_ref, loss_ref, dg_ref):
        b = pl.program_id(0)
        i = pl.program_id(1)

        @pl.when((b == 0) & (i == 0))
        def _():
            loss_ref[...] = jnp.zeros_like(loss_ref)
            dg_ref[...] = jnp.zeros_like(dg_ref)

        x = h_ref[...]
        r = lax.rsqrt(jnp.mean(x * x, axis=1, keepdims=True) + NORM_EPS)
        nx = x * r
        gg = g_ref[...]
        live = jnp.where(i >= 1, 1.0, 0.0)
        err = (nx * gg - t_ref[...]) * live
        loss_ref[...] += jnp.sum(err * err, axis=0, keepdims=True)
        dy = err * (1.0 / d)
        dn = dy * gg
        dh_ref[...] = r * (dn - nx * jnp.mean(dn * nx, axis=1, keepdims=True))
        dg_ref[...] += jnp.sum(dy * nx, axis=0, keepdims=True)

    vec = pl.BlockSpec((1, d), lambda b, i: (0, 0))
    return pl.pallas_call(
        body, name="final_loss", grid=(nbatch, nb),
        in_specs=[pl.BlockSpec((BLK, d), lambda b, i: (b * nb + i, 0)),
                  pl.BlockSpec((BLK, d), lambda b, i: (b * (nb - 1) + jnp.maximum(i - 1, 0), 0)),
                  vec],
        out_specs=[pl.BlockSpec((BLK, d), lambda b, i: (b * nb + i, 0)), vec, vec],
        out_shape=[jax.ShapeDtypeStruct((t, d), F32), jax.ShapeDtypeStruct((1, d), F32),
                   jax.ShapeDtypeStruct((1, d), F32)],
        compiler_params=_params(("arbitrary", "arbitrary")),
    )(h2, tgt, g)


def _direct_exchange(srcs, dsts, send_sems, recv_sems, local_sems, chunked):
    x, y, c = lax.axis_index("x"), lax.axis_index("y"), lax.axis_index("c")
    me = 4 * x + 2 * y + c
    arrays = range(len(srcs))
    local = [pltpu.make_async_copy(srcs[a].at[me] if chunked else srcs[a], dsts[a].at[me], local_sems.at[a])
             for a in arrays]
    sends, recvs = [], []
    for d in range(1, N_DEV):
        px = x + ((d >> 2) & 1) - 2 * x * ((d >> 2) & 1)
        py = y + ((d >> 1) & 1) - 2 * y * ((d >> 1) & 1)
        pc = c + (d & 1) - 2 * c * (d & 1)
        pid = 4 * px + 2 * py + pc
        for a in arrays:
            kw = dict(send_sem=send_sems.at[d - 1, a], recv_sem=recv_sems.at[d - 1, a],
                      device_id=(px, py, pc), device_id_type=pl.DeviceIdType.MESH)
            src = srcs[a].at[pid] if chunked else srcs[a]
            sends.append(pltpu.make_async_remote_copy(src_ref=src, dst_ref=dsts[a].at[me], **kw))
            recvs.append(pltpu.make_async_remote_copy(src_ref=src, dst_ref=dsts[a].at[pid], **kw))
    return local, sends, recvs


def _exchange_beside(first, last, srcs, dsts, sems, chunked):
    local, sends, recvs = _direct_exchange(srcs, dsts, *sems, chunked)

    @pl.when(first)
    def _():
        for cp in local + sends:
            cp.start()

    @pl.when(last)
    def _():
        for cp in local:
            cp.wait()
        for cp in recvs:
            cp.wait_recv()
        for cp in sends:
            cp.wait_send()


def _exchange_sems(n):
    return [pltpu.SemaphoreType.DMA((7, n)), pltpu.SemaphoreType.DMA((7, n)), pltpu.SemaphoreType.DMA((n,))]


def _sb_logits(z):
    log_beta = jnp.minimum(z, 0.0) - jnp.log(1.0 + jnp.exp(-jnp.abs(z)))
    return log_beta, log_beta - z


def _tri(width, after):
    j = lax.broadcasted_iota(jnp.int32, (width, width), 0)
    s = lax.broadcasted_iota(jnp.int32, (width, width), 1)
    return (j > s) if after else (j < s)


def _tri_ones(tri):
    return jnp.concatenate([tri.astype(BF16), jnp.ones((tri.shape[0], BLK), BF16)], axis=1)


def _block_sums(x, tri_ones, after):
    xb = x.astype(BF16)
    subs = [_dot(xb[:, s:s + BLK], tri_ones) for s in range(0, x.shape[1], BLK)]
    if len(subs) == 1:
        return subs[0][:, :BLK], subs[0][:, BLK:]
    first, second = subs
    total = first[:, BLK:] + second[:, BLK:]
    if after:
        return jnp.concatenate([first[:, :BLK] + second[:, BLK:], second[:, :BLK]], axis=1), total
    return jnp.concatenate([first[:, :BLK], second[:, :BLK] + first[:, BLK:]], axis=1), total


def _head_masked(x, lane, scale=None):
    out = []
    for h in range(8):
        xp = x[:, (h // 2) * BLK:(h // 2 + 1) * BLK]
        xm = jnp.where((lane >= HALF) if h % 2 else (lane < HALF), xp, jnp.zeros_like(xp))
        out.append(xm if scale is None else xm * scale)
    return jnp.concatenate(out, axis=1)


def _sb_fwd(proj, shards, nbatch, nb):
    lp = nb * BLK
    t = nbatch * lp

    ns = len(shards)

    def body(*refs):
        q_ref, k_ref, v_ref, g_ref = refs[:4]
        shard_refs = refs[4:4 + ns]
        o_ref, og_ref, cm_ref = refs[4 + ns:7 + ns]
        gathered_refs = refs[7 + ns:7 + 2 * ns]
        c_scr, qm_scr = refs[7 + 2 * ns:9 + 2 * ns]
        b = pl.program_id(0)
        i = pl.program_id(1)
        _exchange_beside((b == 0) & (i == 0), (b == nbatch - 1) & (i == nb - 1),
                         shard_refs, gathered_refs, refs[9 + 2 * ns:], chunked=False)
        _, lane = _iotas()
        lo_m = lane < HALF
        cm_ref[...] = jnp.zeros_like(cm_ref)
        c_scr[...] = jnp.zeros_like(c_scr)
        o_ref[...] = jnp.zeros_like(o_ref)
        qm_scr[...] = _head_masked(q_ref[...], lane, SB_SCALE)

        def block(off, width, edge):
            mask = _key_mask(i, off, width, strict=True) if edge else None
            upper = _tri_ones(_tri(BLK, after=True))
            onehot = lane == off // WIDE
            heads = range(8)
            hcs = [slice(h * BLK, (h + 1) * BLK) for h in heads]
            kbs = [k_ref[pl.ds(off, width), hc] for hc in hcs[:4]]
            vbs = [v_ref[pl.ds(off, width), hc] for hc in hcs[:4]]
            zs = [_dot_nt(qm_scr[:, hcs[h]], kbs[h // 2]) for h in heads]
            lbs, l1s = [], []
            for h in heads:
                log_beta, log_1m = _sb_logits(zs[h])
                lbs.append(log_beta)
                l1s.append(_sel(mask, log_1m, 0.0))
            css = [_block_sums(l1s[h], upper, after=True) for h in heads]
            avs = []
            for h in heads:
                c = c_scr[h]
                avs.append(_sel(mask, jnp.exp(lbs[h] + css[h][0] + _widen(c, width)), 0.0).astype(BF16))
                if width == WIDE:
                    cm_ref[:, hcs[h]] = jnp.where(onehot, c, cm_ref[:, hcs[h]])
                c_scr[h] = c + css[h][1]
            accs = [_dot(avs[h], vbs[h // 2]) for h in heads]
            for p in range(4):
                o_ref[:, hcs[p]] += jnp.where(lo_m, accs[2 * p], accs[2 * p + 1])

        _over_key_blocks(i + 1, block, reverse=True)
        g = g_ref[...].astype(F32)
        og_ref[...] = (o_ref[...] * g * _sigmoid(g)).astype(og_ref.dtype)

    tile = lambda col: pl.BlockSpec((BLK, 512), lambda b, i: (b * nb + i, col))
    full = lambda col: pl.BlockSpec((lp, 512), lambda b, i: (b, col))
    hbm = pl.BlockSpec(memory_space=pl.ANY)
    outs = pl.pallas_call(
        body, name="sb_fwd", grid=(nbatch, nb),
        in_specs=[tile(0), full(1), full(2), tile(3)] + [hbm] * ns,
        out_specs=[tile(0), tile(0), pl.BlockSpec((BLK, 1024), lambda b, i: (b * nb + i, 0))] + [hbm] * ns,
        out_shape=[jax.ShapeDtypeStruct((t, 512), F32), jax.ShapeDtypeStruct((t, 512), BF16),
                   jax.ShapeDtypeStruct((t, 1024), F32)]
        + [jax.ShapeDtypeStruct((N_DEV,) + s.shape, s.dtype) for s in shards],
        scratch_shapes=[pltpu.VMEM((8, BLK, BLK), F32), pltpu.VMEM((BLK, 1024), BF16)] + _exchange_sems(ns),
        compiler_params=_params(("arbitrary", "arbitrary"), 48),
    )(proj, proj, proj, proj, *shards)
    return outs[0], outs[1], outs[2], list(outs[3:])


def _sb_bwd(proj, o, dog, cm, chunked, nbatch, nb):
    lp = nb * BLK
    t = nbatch * lp
    ns = len(chunked)

    def body(*refs):
        q_ref, k_ref, v_ref, g_ref, o_ref, dog_ref, cm_ref = refs[:7]
        chunk_refs = refs[7:7 + ns]
        dq_ref, dk_ref, dv_ref, dg_ref = refs[7 + ns:11 + ns]
        received_refs = refs[11 + ns:11 + 2 * ns]
        c_scr, qm_scr, dom_scr, dq_scr = refs[11 + 2 * ns:15 + 2 * ns]
        b = pl.program_id(0)
        i = pl.program_id(1)
        _exchange_beside((b == 0) & (i == 0), (b == nbatch - 1) & (i == nb - 1),
                         chunk_refs, received_refs, refs[15 + 2 * ns:], chunked=True)

        @pl.when(i == 0)
        def _():
            dk_ref[...] = jnp.zeros_like(dk_ref)
            dv_ref[...] = jnp.zeros_like(dv_ref)

        _, lane = _iotas()
        lo_m = lane < HALF
        g = g_ref[...].astype(F32)
        sig = _sigmoid(g)
        dog_v = dog_ref[...]
        dg_ref[...] = (dog_v * o_ref[...] * (sig * (1.0 + g * (1.0 - sig)))).astype(dg_ref.dtype)
        dom_scr[...] = _head_masked((dog_v * g * sig).astype(BF16), lane)
        qm_scr[...] = _head_masked(q_ref[...], lane, SB_SCALE)
        c_scr[...] = jnp.zeros_like(c_scr)
        dq_scr[...] = jnp.zeros_like(dq_scr)

        def block(off, width, edge):
            mask = _key_mask(i, off, width, strict=True) if edge else None
            upper = _tri_ones(_tri(BLK, after=True))
            lower = _tri_ones(_tri(BLK, after=False))
            onehot = lane == off // WIDE
            hcs = [slice(h * BLK, (h + 1) * BLK) for h in range(8)]
            kbs = [k_ref[pl.ds(off, width), hc] for hc in hcs[:4]]
            vbs = [v_ref[pl.ds(off, width), hc] for hc in hcs[:4]]
            for heads in HEAD_GROUPS:
                zs = {h: _dot_nt(qm_scr[:, hcs[h]], kbs[h // 2]) for h in heads}
                dps = {h: _dot_nt(dom_scr[:, hcs[h]], vbs[h // 2]) for h in heads}
                lbs, l1s = {}, {}
                for h in heads:
                    lbs[h], l1s[h] = _sb_logits(zs[h])
                sufs = {h: _block_sums(_sel(mask, l1s[h], 0.0), upper, after=True)[0] for h in heads}
                prs, dzzs = {}, {}
                for h in heads:
                    expo = lbs[h] + sufs[h]
                    if width == WIDE:
                        expo = expo + jnp.sum(jnp.where(onehot, cm_ref[:, hcs[h]], 0.0), axis=1, keepdims=True)
                    pr = _sel(mask, jnp.exp(expo), 0.0)
                    dzzs[h] = pr * dps[h]
                    prs[h] = pr.astype(BF16)
                css = {h: _block_sums(dzzs[h], lower, after=False) for h in heads}
                dzbs = {}
                for h in heads:
                    c2 = c_scr[h]
                    prefix = css[h][0] + _widen(c2, width)
                    dz = _sel(mask, dzzs[h] * jnp.exp(l1s[h]) - jnp.exp(lbs[h]) * prefix, 0.0)
                    dzbs[h] = dz.astype(BF16)
                    c_scr[h] = c2 + css[h][1]
                dqs = {h: _dot(dzbs[h], kbs[h // 2]) for h in heads}
                dks = {h: _dot_tn(dzbs[h], qm_scr[:, hcs[h]]) for h in heads}
                dvs = {h: _dot_tn(prs[h], dom_scr[:, hcs[h]]) for h in heads}
                for p in sorted({h // 2 for h in heads}):
                    dq_scr[:, hcs[p]] += jnp.where(lo_m, dqs[2 * p], dqs[2 * p + 1])
                    dk_ref[pl.ds(off, width), hcs[p]] += dks[2 * p] + dks[2 * p + 1]
                    dv_ref[pl.ds(off, width), hcs[p]] += dvs[2 * p] + dvs[2 * p + 1]

        _over_key_blocks(i + 1, block, reverse=False)
        dq_ref[...] = (dq_scr[...] * SB_SCALE).astype(dq_ref.dtype)

    tile = lambda col: pl.BlockSpec((BLK, 512), lambda b, i: (b * nb + i, col))
    full = lambda col: pl.BlockSpec((lp, 512), lambda b, i: (b, col))
    acc = jax.ShapeDtypeStruct((t, 512), F32)
    once = jax.ShapeDtypeStruct((t, 512), BF16)
    hbm = pl.BlockSpec(memory_space=pl.ANY)
    outs = pl.pallas_call(
        body, name="sb_bwd", grid=(nbatch, nb),
        in_specs=[tile(0), full(1), full(2), tile(3), tile(0), tile(0),
                  pl.BlockSpec((BLK, 1024), lambda b, i: (b * nb + i, 0))] + [hbm] * ns,
        out_specs=[tile(0), full(0), full(0), tile(0)] + [hbm] * ns,
        out_shape=[once, acc, acc, once] + [jax.ShapeDtypeStruct(a.shape, a.dtype) for a in chunked],
        scratch_shapes=[pltpu.VMEM((8, BLK, BLK), F32), pltpu.VMEM((BLK, 1024), BF16),
                        pltpu.VMEM((BLK, 1024), BF16), pltpu.VMEM((BLK, 512), F32)] + _exchange_sems(ns),
        compiler_params=_params(("arbitrary", "arbitrary"), 56),
    )(proj, proj, proj, proj, o, dog, cm, *chunked)
    return outs[0], outs[1], outs[2], outs[3], list(outs[4:])


def _mla_prep(proj, gq, gkv, tabs, nbatch, nb):
    t = proj.shape[0]

    def body(mid_ref, gq_ref, gkv_ref, c_ref, s1_ref, s2_ref, cq_ref, ckv_ref, kr_ref):
        cq = mid_ref[:, 0:256].astype(F32)
        r = lax.rsqrt(jnp.mean(cq * cq, axis=1, keepdims=True) + NORM_EPS)
        cq_ref[...] = (cq * r * gq_ref[...]).astype(BF16)
        ckv = mid_ref[:, 256:384].astype(F32)
        r = lax.rsqrt(jnp.mean(ckv * ckv, axis=1, keepdims=True) + NORM_EPS)
        ckv_ref[...] = (ckv * r * gkv_ref[...]).astype(BF16)
        kr = mid_ref[:, 384:512].astype(F32)
        kr_ref[...] = _rope(kr, c_ref[...], s1_ref[...], s2_ref[...]).astype(BF16)

    tr = _row_tile(nb * BLK, (544, BLK))
    nt = nb * BLK // tr
    tab = pl.BlockSpec((tr, BLK), lambda b, i: (i, 0))
    rowspec = lambda w: pl.BlockSpec((tr, w), lambda b, i: (b * nt + i, 0))
    return pl.pallas_call(
        body, name="mla_prep", grid=(nbatch, nt),
        in_specs=[pl.BlockSpec((tr, 512), lambda b, i: (b * nt + i, EV_MID_BLK)),
                  pl.BlockSpec((1, 256), lambda b, i: (0, 0)), pl.BlockSpec((1, 128), lambda b, i: (0, 0)),
                  tab, tab, tab],
        out_specs=[rowspec(256), rowspec(128), rowspec(128)],
        out_shape=[jax.ShapeDtypeStruct((t, 256), BF16), jax.ShapeDtypeStruct((t, 128), BF16),
                   jax.ShapeDtypeStruct((t, 128), BF16)],
        compiler_params=_params(("parallel", "parallel")),
    )(proj, gq, gkv, *tabs)


def _mla_prep_bwd(proj, gq, gkv, tabs, dcqn, dckvn, dkrot, nbatch, nb):
    t = proj.shape[0]

    def body(mid_ref, gq_ref, gkv_ref, c_ref, s1_ref, s2_ref, dcq_ref, dckv_ref, dkr_ref,
             dmid_ref, dgq_ref, dgkv_ref):
        @pl.when((pl.program_id(0) == 0) & (pl.program_id(1) == 0))
        def _():
            dgq_ref[...] = jnp.zeros_like(dgq_ref)
            dgkv_ref[...] = jnp.zeros_like(dgkv_ref)

        def norm_bwd(x, gain, dy, dgain_ref):
            r = lax.rsqrt(jnp.mean(x * x, axis=1, keepdims=True) + NORM_EPS)
            nx = x * r
            dn = dy * gain
            dgain_ref[...] += jnp.sum(dy * nx, axis=0, keepdims=True)
            return r * (dn - nx * jnp.mean(dn * nx, axis=1, keepdims=True))

        dmid_ref[:, 0:256] = norm_bwd(
            mid_ref[:, 0:256].astype(F32), gq_ref[...], dcq_ref[...], dgq_ref).astype(BF16)
        dmid_ref[:, 256:384] = norm_bwd(
            mid_ref[:, 256:384].astype(F32), gkv_ref[...], dckv_ref[...], dgkv_ref).astype(BF16)
        dmid_ref[:, 384:512] = _rope_t(dkr_ref[...], c_ref[...], s1_ref[...], s2_ref[...]).astype(BF16)

    tr = _row_tile(nb * BLK, (544, BLK))
    nt = nb * BLK // tr
    tab = pl.BlockSpec((tr, BLK), lambda b, i: (i, 0))
    rowspec = lambda w: pl.BlockSpec((tr, w), lambda b, i: (b * nt + i, 0))
    vq = pl.BlockSpec((1, 256), lambda b, i: (0, 0))
    vkv = pl.BlockSpec((1, 128), lambda b, i: (0, 0))
    return pl.pallas_call(
        body, name="mla_prep_bwd", grid=(nbatch, nt),
        in_specs=[pl.BlockSpec((tr, 512), lambda b, i: (b * nt + i, EV_MID_BLK)), vq, vkv, tab, tab, tab,
                  rowspec(256), rowspec(128), rowspec(128)],
        out_specs=[rowspec(512), vq, vkv],
        out_shape=[jax.ShapeDtypeStruct((t, 512), BF16), jax.ShapeDtypeStruct((1, 256), F32),
                   jax.ShapeDtypeStruct((1, 128), F32)],
        compiler_params=_params(("arbitrary", "arbitrary")),
    )(proj, gq, gkv, *tabs, dcqn, dckvn, dkrot)


def _mla_scores(qf, kvb, krb, mask, lo_m):
    kf = jnp.where(lo_m, kvb, krb)
    return kf, _sel(mask, _dot_nt(qf, kf), NEG)


def _mla_fwd(qh, kvh, krot, proj, tabs, nbatch, nb):
    lp = nb * BLK
    t = nbatch * lp

    def body(q_ref, kv_ref, kr_ref, g_ref, c_ref, s1_ref, s2_ref, o_ref, og_ref, lse_ref,
             qf_scr, m_scr, l_scr, acc_scr):
        i = pl.program_id(1)
        row, lane = _iotas()
        lo_m = lane < HALF
        for h in range(8):
            hc = slice(h * BLK, (h + 1) * BLK)
            qf_scr[:, hc] = (_rope(q_ref[:, hc].astype(F32), c_ref[...], s1_ref[...], s2_ref[...])
                             * MLA_SCALE).astype(BF16)
        m_scr[...] = jnp.full(m_scr.shape, NEG, F32)
        l_scr[...] = jnp.zeros_like(l_scr)
        acc_scr[...] = jnp.zeros_like(acc_scr)

        def block(off, width, edge):
            mask = _key_mask(i, off, width, strict=False) if edge else None
            lo_k = lax.broadcasted_iota(jnp.int32, (width, BLK), 1) < HALF
            ones = jnp.ones((width, BLK), BF16)
            krb = kr_ref[pl.ds(off, width), :]
            heads = range(8)
            hcs = [slice(h * BLK, (h + 1) * BLK) for h in heads]
            kvbs = [kv_ref[pl.ds(off, width), hc] for hc in hcs]
            ss = [_mla_scores(qf_scr[:, hcs[h]], kvbs[h], krb, mask, lo_k)[1] for h in heads]
            ps, alphas = [], []
            for h in heads:
                m = m_scr[h]
                m2 = jnp.maximum(m, jnp.max(ss[h], axis=1, keepdims=True))
                ps.append(jnp.exp(ss[h] - _widen(m2, width)).astype(BF16))
                alphas.append(jnp.exp(m - m2))
                m_scr[h] = m2
            pvs = [_dot(ps[h], jnp.concatenate([kvbs[h], ones], axis=1)) for h in heads]
            for h in heads:
                l_scr[h] = alphas[h] * l_scr[h] + pvs[h][:, BLK:]
                acc_scr[h] = alphas[h] * acc_scr[h] + pvs[h][:, :BLK]

        _over_key_blocks(i + 1, block, reverse=False)
        lse = jnp.zeros((BLK, BLK), F32)
        for p in range(4):
            pc = slice(p * BLK, (p + 1) * BLK)
            o0 = acc_scr[2 * p] / l_scr[2 * p]
            o1 = acc_scr[2 * p + 1] / l_scr[2 * p + 1]
            o_ref[:, pc] = jnp.where(lo_m, pltpu.roll(o0, HALF, 1), o1)
            for h in (2 * p, 2 * p + 1):
                lse = lse + jnp.where(lane == h, m_scr[h] + jnp.log(l_scr[h]), 0.0)
        lse_ref[...] = lse
        g = g_ref[...].astype(F32)
        og_ref[...] = (o_ref[...] * g * _sigmoid(g)).astype(og_ref.dtype)

    tile = pl.BlockSpec((BLK, 512), lambda b, i: (b * nb + i, 0))
    tab = pl.BlockSpec((BLK, BLK), lambda b, i: (i, 0))
    heads = pltpu.VMEM((8, BLK, BLK), F32)
    return pl.pallas_call(
        body, name="mla_fwd", grid=(nbatch, nb),
        in_specs=[pl.BlockSpec((BLK, 1024), lambda b, i: (b * nb + i, 0)),
                  pl.BlockSpec((lp, 1024), lambda b, i: (b, 0)),
                  pl.BlockSpec((lp, BLK), lambda b, i: (b, 0)),
                  pl.BlockSpec((BLK, 512), lambda b, i: (b * nb + i, EV_GMLA_BLK // 4)),
                  tab, tab, tab],
        out_specs=[tile, tile, pl.BlockSpec((BLK, BLK), lambda b, i: (b * nb + i, 0))],
        out_shape=[jax.ShapeDtypeStruct((t, 512), F32), jax.ShapeDtypeStruct((t, 512), BF16),
                   jax.ShapeDtypeStruct((t, BLK), F32)],
        scratch_shapes=[pltpu.VMEM((BLK, 1024), BF16), heads, heads, heads],
        compiler_params=_params(("parallel", "arbitrary"), 48),
    )(qh, kvh, krot, proj, *tabs)


def _mla_bwd(qh, kvh, krot, proj, tabs, o, dog, lse, nbatch, nb):
    lp = nb * BLK
    t = nbatch * lp

    def body(q_ref, kv_ref, kr_ref, g_ref, c_ref, s1_ref, s2_ref, o_ref, dog_ref, lse_ref,
             dq_ref, dkv_ref, dkr_ref, dg_ref, qf_scr, do_scr, stat_scr, acc_scr):
        i = pl.program_id(1)

        @pl.when(i == 0)
        def _():
            dkv_ref[...] = jnp.zeros_like(dkv_ref)
            dkr_ref[...] = jnp.zeros_like(dkr_ref)

        row, lane = _iotas()
        lo_m = lane < HALF
        g = g_ref[...].astype(F32)
        sig = _sigmoid(g)
        dog_v = dog_ref[...]
        o_v = o_ref[...]
        dg_ref[...] = (dog_v * o_v * (sig * (1.0 + g * (1.0 - sig)))).astype(dg_ref.dtype)
        do = dog_v * g * sig
        do_o = do * o_v
        lse_blk = lse_ref[...]
        zero = jnp.zeros((BLK, BLK), F32)
        for h in range(8):
            hc = slice(h * BLK, (h + 1) * BLK)
            pc = slice((h // 2) * BLK, (h // 2 + 1) * BLK)
            qf_scr[:, hc] = (_rope(q_ref[:, hc].astype(F32), c_ref[...], s1_ref[...], s2_ref[...])
                             * MLA_SCALE).astype(BF16)
            dop = do[:, pc]
            do_src = dop if h % 2 else pltpu.roll(dop, HALF, 1)
            do_scr[:, hc] = jnp.where(lo_m, 0.0, do_src).astype(BF16)
            hm = (lane >= HALF) if h % 2 else lo_m
            stat_scr[h] = zero + jnp.sum(jnp.where(hm, do_o[:, pc], 0.0), axis=1, keepdims=True)
            stat_scr[8 + h] = zero + jnp.sum(jnp.where(lane == h, lse_blk, 0.0), axis=1, keepdims=True)
        acc_scr[...] = jnp.zeros_like(acc_scr)

        def block(off, width, edge):
            mask = _key_mask(i, off, width, strict=False) if edge else None
            lo_k = lax.broadcasted_iota(jnp.int32, (width, BLK), 1) < HALF
            krb = kr_ref[pl.ds(off, width), :]
            heads = range(8)
            hcs = [slice(h * BLK, (h + 1) * BLK) for h in heads]
            kvbs = [kv_ref[pl.ds(off, width), hc] for hc in hcs]
            qfs = [qf_scr[:, hc] for hc in hcs]
            dos = [do_scr[:, hc] for hc in hcs]
            scored = [_mla_scores(qfs[h], kvbs[h], krb, mask, lo_k) for h in heads]
            dps = [_dot_nt(dos[h], kvbs[h]) for h in heads]
            pbs, dss = [], []
            for h in heads:
                p = jnp.exp(scored[h][1] - _widen(stat_scr[8 + h], width))
                pbs.append(p.astype(BF16))
                dss.append((p * (dps[h] - _widen(stat_scr[h], width))).astype(BF16))
            dqs = [_dot(dss[h], scored[h][0]) for h in heads]
            dkfs = [_dot_tn(dss[h], qfs[h]) for h in heads]
            dvvs = [_dot_tn(pbs[h], dos[h]) for h in heads]
            dkr = jnp.zeros((width, BLK), F32)
            for h in heads:
                acc_scr[h] += dqs[h]
                dkv_ref[pl.ds(off, width), hcs[h]] += jnp.where(lo_k, dkfs[h], 0.0) + dvvs[h]
                dkr = dkr + jnp.where(lo_k, 0.0, dkfs[h])
            dkr_ref[pl.ds(off, width), :] += dkr

        _over_key_blocks(i + 1, block, reverse=False)
        for h in range(8):
            hc = slice(h * BLK, (h + 1) * BLK)
            dq_ref[:, hc] = _rope_t(acc_scr[h] * MLA_SCALE, c_ref[...], s1_ref[...], s2_ref[...]).astype(dq_ref.dtype)

    tile = lambda col: pl.BlockSpec((BLK, 512), lambda b, i: (b * nb + i, col))
    wide = pl.BlockSpec((BLK, 1024), lambda b, i: (b * nb + i, 0))
    full8 = pl.BlockSpec((lp, 1024), lambda b, i: (b, 0))
    full1 = pl.BlockSpec((lp, BLK), lambda b, i: (b, 0))
    tab = pl.BlockSpec((BLK, BLK), lambda b, i: (i, 0))
    return pl.pallas_call(
        body, name="mla_bwd", grid=(nbatch, nb),
        in_specs=[wide, full8, full1, tile(EV_GMLA_BLK // 4), tab, tab, tab, tile(0), tile(1),
                  pl.BlockSpec((BLK, BLK), lambda b, i: (b * nb + i, 0))],
        out_specs=[wide, full8, full1, tile(0)],
        out_shape=[jax.ShapeDtypeStruct((t, 1024), BF16), jax.ShapeDtypeStruct((t, 1024), F32),
                   jax.ShapeDtypeStruct((t, 128), F32), jax.ShapeDtypeStruct((t, 512), BF16)],
        scratch_shapes=[pltpu.VMEM((BLK, 1024), BF16), pltpu.VMEM((BLK, 1024), BF16),
                        pltpu.VMEM((16, BLK, BLK), F32), pltpu.VMEM((8, BLK, BLK), F32)],
        compiler_params=_params(("parallel", "arbitrary"), 56),
    )(qh, kvh, krot, proj, *tabs, o, dog, lse)


def _swa_setup(kk, i, k_refs, v_refs):
    row, lane = _iotas()
    own = (lane >= kk * HALF) & (lane < (kk + 1) * HALF)

    def dup(ref):
        x = ref[...].astype(F32)
        return jnp.where(own, x, pltpu.roll(x, HALF, 1)).astype(BF16)

    kcat = jnp.concatenate([dup(r) for r in k_refs], axis=0)
    vcat = jnp.concatenate([dup(r) for r in v_refs], axis=0)
    row2 = lax.broadcasted_iota(jnp.int32, (BLK, 2 * BLK), 0)
    lane2 = lax.broadcasted_iota(jnp.int32, (BLK, 2 * BLK), 1)
    is_meta = lane2 < BLK
    in_own = lane2 - BLK <= row2
    k_pos = jnp.where(is_meta, lane2, jnp.where(in_own, (i - 1) * BLK, (i - 2) * BLK) + lane2)
    d = i * BLK + row2 - k_pos
    mask = (d >= 0) & (k_pos >= jnp.where(is_meta, N_PAD, BLK))
    return lane, lane <= row, own, kcat, vcat, mask, d.astype(F32)


def _swa_fold(x, in_own):
    return jnp.concatenate([x[:, :BLK], jnp.where(in_own, x[:, 2 * BLK:], x[:, BLK:2 * BLK])], axis=1)


def _swa_unfold(x, in_own):
    w = x[:, BLK:]
    zero = jnp.zeros_like(w)
    return jnp.concatenate([x[:, :BLK], jnp.where(in_own, zero, w), jnp.where(in_own, w, zero)], axis=1)


def _swa_slope(kk, g_idx):
    return (2.0 ** (-(g_idx + 1) / 2.0)) * jnp.where(kk == 0, 1.0, 1.0 / 16.0)


def _swa_fwd(proj, sinks, nbatch, nb):
    lp = nb * BLK
    t = nbatch * lp

    def body(sink_ref, q_ref, ka, kb, kc, va, vb, vc, g_ref, o_ref, og_ref, lse_ref):
        kk = pl.program_id(1)
        i = pl.program_id(2)
        lane, in_own, own, kcat, vcat, mask, dist = _swa_setup(kk, i, (ka, kb, kc), (va, vb, vc))
        lo_m = lane < HALF
        heads = range(8)
        qms = []
        for h in heads:
            qp = q_ref[:, (h // 2) * BLK:(h // 2 + 1) * BLK]
            qms.append(jnp.where((lane >= HALF) if h % 2 else lo_m, qp, jnp.zeros_like(qp)) * SWA_SCALE)
        qks = [_dot_nt(qms[h], kcat) for h in heads]
        ps, ls, lses = [], [], []
        for h in heads:
            sink = sink_ref[kk, h]
            s = jnp.where(mask, _swa_fold(qks[h], in_own) - _swa_slope(kk, h) * dist, NEG)
            mx = jnp.maximum(jnp.max(s, axis=1, keepdims=True), sink)
            p = jnp.exp(s - mx)
            l = jnp.exp(sink - mx) + jnp.sum(p, axis=1, keepdims=True)
            ps.append(_swa_unfold(p.astype(BF16), in_own))
            ls.append(l)
            lses.append(mx + jnp.log(l))
        pvs = [_dot(ps[h], vcat) for h in heads]
        lse_out = jnp.zeros((BLK, BLK), F32)
        for m in range(4):
            cols = slice(m * BLK, (m + 1) * BLK)
            outp = jnp.where(lo_m, pvs[2 * m] / ls[2 * m], pvs[2 * m + 1] / ls[2 * m + 1])
            o_ref[:, cols] = outp
            g = g_ref[:, cols].astype(F32)
            og_ref[:, cols] = (outp * g * _sigmoid(g)).astype(og_ref.dtype)
            for h in (2 * m, 2 * m + 1):
                lse_out = lse_out + jnp.where(lane == h, lses[h], 0.0)
        lse_ref[...] = lse_out

    def kvspec(col, which):
        if which == 0:
            return pl.BlockSpec((BLK, BLK), lambda b, kk, i: (b * nb, col))
        if which == 1:
            return pl.BlockSpec((BLK, BLK), lambda b, kk, i: (b * nb + jnp.maximum(i - 1, 0), col))
        return pl.BlockSpec((BLK, BLK), lambda b, kk, i: (b * nb + i, col))

    wide = lambda c0: pl.BlockSpec((BLK, 512), lambda b, kk, i: (b * nb + i, c0 + kk))
    return pl.pallas_call(
        body, name="swa_fwd", grid=(nbatch, 2, nb),
        in_specs=[pl.BlockSpec(memory_space=pltpu.SMEM), wide(0),
                  kvspec(OD_K_BLK, 0), kvspec(OD_K_BLK, 1), kvspec(OD_K_BLK, 2),
                  kvspec(OD_V_BLK, 0), kvspec(OD_V_BLK, 1), kvspec(OD_V_BLK, 2), wide(2)],
        out_specs=[wide(0), wide(0), pl.BlockSpec((BLK, BLK), lambda b, kk, i: (b * nb + i, kk))],
        out_shape=[jax.ShapeDtypeStruct((t, 1024), F32), jax.ShapeDtypeStruct((t, 1024), BF16),
                   jax.ShapeDtypeStruct((t, 256), F32)],
        compiler_params=_params(("parallel", "parallel", "arbitrary")),
    )(sinks, proj, proj, proj, proj, proj, proj, proj, proj)


def _swa_bwd(proj, sinks, o, dog, lse, nbatch, nb):
    lp = nb * BLK
    t = nbatch * lp

    def body(sink_ref, q_ref, ka, kb, kc, va, vb, vc, g_ref, o_ref, dog_ref, lse_ref,
             dq_ref, dg_ref, dk_ref, dv_ref, dsink_ref):
        kk = pl.program_id(1)
        i = pl.program_id(2)

        @pl.when((kk == 0) & (i == 0))
        def _():
            dk_ref[...] = jnp.zeros_like(dk_ref)
            dv_ref[...] = jnp.zeros_like(dv_ref)

        @pl.when(i == 0)
        def _():
            dsink_ref[...] = jnp.zeros_like(dsink_ref)

        lane, in_own, own, kcat, vcat, mask, dist = _swa_setup(kk, i, (ka, kb, kc), (va, vb, vc))
        lo_m = lane < HALF
        row8 = lax.broadcasted_iota(jnp.int32, (8, BLK), 0)
        lse_blk = lse_ref[...]
        heads = range(8)
        qms, doms, deltas, lse_hs = [], [], [], []
        for m in range(4):
            cols = slice(m * BLK, (m + 1) * BLK)
            qp = q_ref[:, cols]
            g = g_ref[:, cols].astype(F32)
            sig = _sigmoid(g)
            dog_v = dog_ref[:, cols]
            o_v = o_ref[:, cols]
            dg_ref[:, cols] = (dog_v * o_v * (sig * (1.0 + g * (1.0 - sig)))).astype(dg_ref.dtype)
            do = dog_v * g * sig
            do_o = do * o_v
            dob = do.astype(BF16)
            for h in (2 * m, 2 * m + 1):
                hm = (lane >= HALF) if h % 2 else lo_m
                qms.append(jnp.where(hm, qp, jnp.zeros_like(qp)) * SWA_SCALE)
                doms.append(jnp.where(hm, dob, jnp.zeros_like(dob)))
                deltas.append(jnp.sum(jnp.where(hm, do_o, 0.0), axis=1, keepdims=True))
                lse_hs.append(jnp.sum(jnp.where(lane == h, lse_blk, 0.0), axis=1, keepdims=True))
        qks = [_dot_nt(qms[h], kcat) for h in heads]
        dps = [_dot_nt(doms[h], vcat) for h in heads]
        pbs, dss = [], []
        dsink = jnp.zeros((8, BLK), F32)
        for h in heads:
            s = jnp.where(mask, _swa_fold(qks[h], in_own) - _swa_slope(kk, h) * dist, NEG)
            p = jnp.exp(s - lse_hs[h])
            pbs.append(_swa_unfold(p.astype(BF16), in_own))
            dss.append(_swa_unfold((p * (_swa_fold(dps[h], in_own) - deltas[h])).astype(BF16), in_own))
            tot = jnp.sum(-jnp.exp(sink_ref[kk, h] - lse_hs[h]) * deltas[h], axis=0, keepdims=True)
            dsink = dsink + jnp.where(row8 == h, tot, 0.0)
        dsink_ref[...] += dsink
        dqs = [_dot(dss[h], kcat) for h in heads]
        dks = [_dot_tn(dss[h], qms[h]) for h in heads]
        dvs = [_dot_tn(pbs[h], doms[h]) for h in heads]
        for m in range(4):
            dq_ref[:, m * BLK:(m + 1) * BLK] = (
                jnp.where(lo_m, dqs[2 * m], dqs[2 * m + 1]) * SWA_SCALE).astype(dq_ref.dtype)
        dk = dks[0]
        dv = dvs[0]
        for h in range(1, 8):
            dk = dk + dks[h]
            dv = dv + dvs[h]
        offs = [0, pl.multiple_of(jnp.maximum(i - 1, 0) * BLK, BLK), pl.multiple_of(i * BLK, BLK)]
        for x in range(3):
            rows = slice(x * BLK, (x + 1) * BLK)
            dkx, dvx = dk[rows], dv[rows]
            dk_ref[pl.ds(offs[x], BLK), :] += jnp.where(own, dkx + pltpu.roll(dkx, HALF, 1), 0.0)
            dv_ref[pl.ds(offs[x], BLK), :] += jnp.where(own, dvx + pltpu.roll(dvx, HALF, 1), 0.0)

    def kvspec(col, which):
        if which == 0:
            return pl.BlockSpec((BLK, BLK), lambda b, kk, i: (b * nb, col))
        if which == 1:
            return pl.BlockSpec((BLK, BLK), lambda b, kk, i: (b * nb + jnp.maximum(i - 1, 0), col))
        return pl.BlockSpec((BLK, BLK), lambda b, kk, i: (b * nb + i, col))

    wide = lambda c0: pl.BlockSpec((BLK, 512), lambda b, kk, i: (b * nb + i, c0 + kk))
    full = pl.BlockSpec((lp, BLK), lambda b, kk, i: (b, 0))
    return pl.pallas_call(
        body, name="swa_bwd", grid=(nbatch, 2, nb),
        in_specs=[pl.BlockSpec(memory_space=pltpu.SMEM), wide(0),
                  kvspec(OD_K_BLK, 0), kvspec(OD_K_BLK, 1), kvspec(OD_K_BLK, 2),
                  kvspec(OD_V_BLK, 0), kvspec(OD_V_BLK, 1), kvspec(OD_V_BLK, 2), wide(2),
                  wide(0), wide(0), pl.BlockSpec((BLK, BLK), lambda b, kk, i: (b * nb + i, kk))],
        out_specs=[wide(0), wide(0), full, full,
                   pl.BlockSpec((8, BLK), lambda b, kk, i: (b * 2 + kk, 0))],
        out_shape=[jax.ShapeDtypeStruct((t, 1024), BF16), jax.ShapeDtypeStruct((t, 1024), BF16),
                   jax.ShapeDtypeStruct((t, 128), F32), jax.ShapeDtypeStruct((t, 128), F32),
                   jax.ShapeDtypeStruct((nbatch * 16, BLK), F32)],
        compiler_params=_params(("parallel", "arbitrary", "arbitrary")),
    )(sinks, proj, proj, proj, proj, proj, proj, proj, proj, o, dog, lse)


def _rope_tables(lp):
    pos = (jnp.arange(lp) - N_PAD).astype(F32)
    inv = ROPE_BASE ** (-jnp.arange(16, dtype=F32) / 16.0)
    ang = pos[:, None] * inv[None, :]
    cos, sin = jnp.cos(ang), jnp.sin(ang)
    z16 = jnp.zeros((lp, 16), F32)
    c = jnp.concatenate([jnp.ones((lp, 64), F32), cos, cos, jnp.zeros((lp, 32), F32)], axis=1)
    s1 = jnp.concatenate([jnp.zeros((lp, 64), F32), -sin, z16, jnp.zeros((lp, 32), F32)], axis=1)
    s2 = jnp.concatenate([jnp.zeros((lp, 64), F32), z16, sin, jnp.zeros((lp, 32), F32)], axis=1)
    return c, s1, s2


def _local_step(h0, tgt, norm_g, final_g, gq, gkv, sinks, w_ie, late_shards, nbatch, nb):
    lp = nb * BLK
    tabs = _rope_tables(lp)
    g0, g1 = norm_g[0:1], norm_g[1:2]
    sinks2 = sinks.reshape(2, 8)

    proj_e, hn0 = _norm_mm(h0, g0, w_ie, "proj_even")
    o_sb, og_sb, cm, gathered = _sb_fwd(proj_e, [late_shards[name] for name in _LATE], nbatch, nb)
    full = {name: _unchunk(name, blk) for name, blk in zip(_LATE, gathered) if _BIG[name][2] == 0 or name == "ev_w_ukv"}
    w_ukv, w_oe, w_oo = full["ev_w_ukv"], full["ev_w_out"], full["od_w_out"]
    w_uq = _from_blocks(gathered[_LATE.index("ev_w_uq")], _UQ_PLAN)
    w_io = _from_blocks(gathered[_LATE.index("od_w_in")], _OD_IN_PLAN)
    cqn, ckvn, krot = _mla_prep(proj_e, gq, gkv, tabs, nbatch, nb)
    qh = _mm(cqn, w_uq, "nn", "mla_uq", out_dtype=BF16)
    kvh = _mm(ckvn, w_ukv, "nn", "mla_ukv", out_dtype=BF16)
    o_mla, og_mla, lse_m = _mla_fwd(qh, kvh, krot, proj_e, tabs, nbatch, nb)
    h1 = _mm([og_sb, og_mla], w_oe, "nn", "out_even", add=h0)
    proj_o, hn1 = _norm_mm(h1, g1, w_io, "proj_odd")
    o_o, og_o, lse_o = _swa_fwd(proj_o, sinks2, nbatch, nb)
    h2 = _mm(og_o, w_oo, "nn", "out_odd", add=h1)
    dh2, lossv, d_final_g = _final(h2, tgt, final_g, nbatch, nb)

    dog_o = _mm(dh2, w_oo, "nt", "d_out_odd")
    d_w_oo, = _mm_tn(og_o, [dh2], "dw_out_odd")
    dq_o, dg_o, dk_o, dv_o, dsink = _swa_bwd(proj_o, sinks2, o_o, dog_o, lse_o, nbatch, nb)
    dproj_o = [dq_o, dg_o, dk_o, dv_o]
    dh1, d_g1, _ = _mm_norm_bwd(dproj_o, w_io, h1, g1, dh2, "d_proj_odd")
    dw_q, dw_g, dw_k, dw_v = _mm_tn(hn1, dproj_o, "dw_proj_odd")

    dog_e = _mm(dh1, w_oe, "nt", "d_out_even")
    d_w_oe_sb, = _mm_tn(og_sb, [dh1], "dw_out_even_sb")
    d_w_oe_mla, = _mm_tn(og_mla, [dh1], "dw_out_even_mla")
    early = dict(od_w_in=_to_chunks([(dw_q, 0, 1024), (dw_k, 0, 128), (dw_v, 0, 128), (dw_g, 0, 1024)], 288),
                 od_w_out=_chunk("od_w_out", d_w_oo),
                 ev_w_out=_chunk("ev_w_out", jnp.concatenate([d_w_oe_sb, d_w_oe_mla], axis=0)))
    dq_sb, dk_sb, dv_sb, dg_sb, received = _sb_bwd(
        proj_e, o_sb, dog_e, cm, [early[name] for name in _EARLY_GRADS], nbatch, nb)
    dqh, dkvh, dkrot, dg_mla = _mla_bwd(qh, kvh, krot, proj_e, tabs, o_mla, dog_e, lse_m, nbatch, nb)
    dcqn = _mm(dqh, w_uq, "nt", "d_mla_uq")
    d_w_uq, = _mm_tn(cqn, [dqh], "dw_mla_uq")
    dckvn = _mm(dkvh, w_ukv, "nt", "d_mla_ukv")
    d_w_ukv, = _mm_tn(ckvn, [dkvh], "dw_mla_ukv")
    dmid, d_gq, d_gkv = _mla_prep_bwd(proj_e, gq, gkv, tabs, dcqn, dckvn, dkrot, nbatch, nb)
    dproj_e = [dq_sb, dk_sb, dv_sb, dg_sb, dmid, dg_mla]
    dw_e = _mm_tn(hn0, dproj_e, "dw_proj_even")
    ev_in_segments = [(dw_e[0], 0, 512), (dw_e[1], 0, 512), (dw_e[2], 0, 512), (dw_e[3], 0, 512),
                      (dw_e[4], 0, 384), (dw_e[4], 448, 480), (dw_e[5], 0, 512)]
    last = [_to_chunks(ev_in_segments, 372), _to_chunks([(d_w_uq, 128 * h, 128 * h + 96) for h in range(8)], 96),
            _chunk("ev_w_ukv", d_w_ukv)]
    dh0, d_g0, received_last = _mm_norm_bwd(dproj_e, w_ie, h0, g0, dh1, "d_proj_even", chunked=last)

    d_sinks = dsink.reshape(nbatch, 2, 8, BLK)[:, :, :, 0].sum(axis=0).reshape(1, 16)
    d_norm_g = jnp.concatenate([d_g0, d_g1], axis=0)
    return dict(lossv=lossv, dh0=dh0, norm_g=d_norm_g, final_g=d_final_g, gq=d_gq, gkv=d_gkv, sinks=d_sinks,
                received=received + received_last)


def _from_blocks(blocks, plan):
    shard = blocks.shape[2]
    parts = []
    for item in plan:
        if isinstance(item, int):
            parts.append(jnp.zeros((blocks.shape[1], item), blocks.dtype))
            continue
        lo, hi = item
        for k in range(N_DEV):
            a, b = max(lo, k * shard), min(hi, (k + 1) * shard)
            if a < b:
                parts.append(blocks[k][:, a - k * shard:b - k * shard])
    return jnp.concatenate(parts, axis=1)


def _to_chunks(segments, shard):
    chunks, start = [[] for _ in range(N_DEV)], 0
    for arr, lo, hi in segments:
        for k in range(N_DEV):
            a, b = max(start, k * shard), min(start + hi - lo, (k + 1) * shard)
            if a < b:
                chunks[k].append(arr[:, lo + a - start:lo + b - start].astype(BF16))
        start += hi - lo
    return jnp.stack([jnp.concatenate(c, axis=1) for c in chunks], axis=0)


_EV_IN_PLAN = [(0, 2432), 64, (2432, 2464), 32, (2464, 2976)]
_UQ_PLAN = [item for h in range(8) for item in ((96 * h, 96 * h + 96), 32)]
_OD_IN_PLAN = [(0, 1024), (1280, 2304), (1024, 1280)]


_BIG = dict(ev_w_in=(1024, 2976, 1), ev_w_uq=(256, 768, 1), ev_w_ukv=(128, 1024, 1),
            ev_w_out=(1024, 1024, 0), od_w_in=(1024, 2304, 1), od_w_out=(1024, 1024, 0))
_LATE = ("ev_w_uq", "ev_w_ukv", "ev_w_out", "od_w_in", "od_w_out")
_EARLY_GRADS = ("od_w_in", "od_w_out", "ev_w_out")
_LAST_GRADS = ("ev_w_in", "ev_w_uq", "ev_w_ukv")


def _unchunk(name, blk):
    rows, cols, axis = _BIG[name]
    return blk.transpose(1, 0, 2).reshape(rows, cols) if axis == 1 else blk.reshape(rows, cols)


def _chunk(name, g):
    rows, cols, axis = _BIG[name]
    g = g.astype(BF16)
    return g.reshape(rows, N_DEV, cols // N_DEV).transpose(1, 0, 2) if axis == 1 else g.reshape(N_DEV, rows // N_DEV, cols)


def _all_gather(shards, name):
    n = len(shards)

    def body(*refs):
        xs, outs = refs[:n], refs[n:2 * n]
        send_sems, recv_sems, local_sems = refs[2 * n:]
        x, y, c = lax.axis_index("x"), lax.axis_index("y"), lax.axis_index("c")
        me, sibling = (x, y, c), (x, y, 1 - c)
        chips = [(1 - x, y), (x, 1 - y), (1 - x, 1 - y)]
        arrays = range(n)

        def copy(k, a, block, to, from_input=False):
            px, py, pc = block
            dst = outs[a].at[4 * px + 2 * py + pc]
            return pltpu.make_async_remote_copy(
                src_ref=xs[a] if from_input else dst, dst_ref=dst,
                send_sem=send_sems.at[k, a], recv_sem=recv_sems.at[k, a],
                device_id=to, device_id_type=pl.DeviceIdType.MESH)

        mine = [pltpu.make_async_copy(xs[a], outs[a].at[4 * x + 2 * y + c], local_sems.at[a]) for a in arrays]
        for cp in mine:
            cp.start()
        first = [copy(0, a, me, sibling, True) for a in arrays]
        for j, chip in enumerate(chips):
            first += [copy(1 + j, a, me, (*chip, c), True) for a in arrays]
        for cp in first:
            cp.start()
        passed = []
        for j, chip in enumerate(chips):
            for a in arrays:
                copy(1 + j, a, (*chip, c), me).wait_recv()
                passed.append(copy(4 + j, a, (*chip, c), sibling))
                passed[-1].start()
        for a in arrays:
            copy(0, a, sibling, me).wait_recv()
        for j, chip in enumerate(chips):
            for a in arrays:
                copy(4 + j, a, (*chip, 1 - c), me).wait_recv()
        for cp in first + passed:
            cp.wait_send()
        for cp in mine:
            cp.wait()

    hbm = pl.BlockSpec(memory_space=pl.ANY)
    return pl.pallas_call(
        body, name=name,
        out_shape=[jax.ShapeDtypeStruct((N_DEV,) + s.shape, s.dtype) for s in shards],
        in_specs=[hbm] * n, out_specs=[hbm] * n,
        scratch_shapes=[pltpu.SemaphoreType.DMA((7, n)), pltpu.SemaphoreType.DMA((7, n)),
                        pltpu.SemaphoreType.DMA((n,))],
    )(*shards)


def _exchange_sum(chunked, received, name):
    n, m = len(chunked), len(received)
    arrs = list(chunked) + list(received)

    def body(*refs):
        ins, outs = refs[:n + m], refs[n + m:2 * (n + m)]
        bufs = refs[2 * (n + m):3 * (n + m)]
        send_sems, recv_sems, local_sems, load_sems = refs[3 * (n + m):]
        loads = [pltpu.make_async_copy(ins[n + a], bufs[n + a], load_sems.at[a]) for a in range(m)]
        for cp in loads:
            cp.start()
        local, sends, recvs = _direct_exchange(ins[:n], bufs[:n], send_sems, recv_sems, local_sems, chunked=True)
        for cp in local + sends:
            cp.start()
        for a, cp in enumerate(loads):
            cp.wait()
            _sum_slots(bufs[n + a], outs[n + a])
        for cp in local:
            cp.wait()
        for cp in recvs:
            cp.wait_recv()
        for cp in sends:
            cp.wait_send()
        for a in range(n):
            _sum_slots(bufs[a], outs[a])

    hbm = pl.BlockSpec(memory_space=pl.ANY)
    vm = pl.BlockSpec(memory_space=pltpu.VMEM)
    return pl.pallas_call(
        body, name=name,
        out_shape=[jax.ShapeDtypeStruct(a.shape[1:], F32) for a in arrs],
        in_specs=[hbm] * (n + m), out_specs=[vm] * (n + m),
        scratch_shapes=[pltpu.VMEM(a.shape, a.dtype) for a in arrs] + _exchange_sems(n)
        + [pltpu.SemaphoreType.DMA((max(m, 1),))],
        compiler_params=pltpu.CompilerParams(vmem_limit_bytes=48 << 20),
    )(*arrs)


def _sum_slots(buf, out):
    rows = buf.shape[1]

    def add(sl):
        acc = buf[(0,) + sl].astype(F32)
        for k in range(1, N_DEV):
            acc = acc + buf[(k,) + sl].astype(F32)
        out[sl] = acc

    if rows > BLK and rows % BLK == 0:
        def step(r, carry):
            add((pl.ds(pl.multiple_of(r * BLK, BLK), BLK), slice(None)))
            return carry

        lax.fori_loop(0, rows // BLK, step, 0)
    else:
        add((slice(None), slice(None)))


def _adamw(ws, gs, ms, vs, name, steps):
    n = len(ws)

    def body(*refs):
        ins, outs = refs[:4 * n], refs[4 * n:]
        for k in range(n):
            w_ref, g_ref, m_ref, v_ref = ins[4 * k:4 * k + 4]
            d_ref, nm_ref, nv_ref = outs[3 * k:3 * k + 3]
            g = g_ref[...]
            m = ADAM_B1 * m_ref[...] + (1.0 - ADAM_B1) * g
            v = ADAM_B2 * v_ref[...] + (1.0 - ADAM_B2) * (g * g)
            m_hat = m / (1.0 - ADAM_B1 ** ADAM_STEP)
            v_hat = v / (1.0 - ADAM_B2 ** ADAM_STEP)
            d_ref[...] = -ADAM_LR * (m_hat / (jnp.sqrt(v_hat) + ADAM_EPS) + ADAM_WD * w_ref[...])
            nm_ref[...] = m
            nv_ref[...] = v

    args, out_shape, in_specs, out_specs = [], [], [], []
    for k in range(n):
        rows, cols = ws[k].shape
        spec = pl.BlockSpec((rows // steps, cols), lambda i: (i, 0))
        args += [ws[k], gs[k], ms[k], vs[k]]
        in_specs += [spec] * 4
        out_specs += [spec] * 3
        out_shape += [jax.ShapeDtypeStruct(ws[k].shape, F32)] * 3
    outs = pl.pallas_call(
        body, name=name, grid=(steps,), out_shape=out_shape, in_specs=in_specs, out_specs=out_specs,
        compiler_params=_params(("parallel",)),
    )(*args)
    return [tuple(outs[3 * k:3 * k + 3]) for k in range(n)]


def kernel(x, meta, norm_g, final_g, ev_w_in, ev_q_norm_g, ev_kv_norm_g, ev_w_uq, ev_w_ukv, ev_w_out, od_w_in, od_sinks, od_w_out, loss_target, m_meta, m_norm_g, m_final_g, m_ev_w_in, m_ev_q_norm_g, m_ev_kv_norm_g, m_ev_w_uq, m_ev_w_ukv, m_ev_w_out, m_od_w_in, m_od_sinks, m_od_w_out, v_meta, v_norm_g, v_final_g, v_ev_w_in, v_ev_q_norm_g, v_ev_kv_norm_g, v_ev_w_uq, v_ev_w_ukv, v_ev_w_out, v_od_w_in, v_od_sinks, v_od_w_out):
    nbatch, seq, d = x.shape
    nb = seq // BLK + 1
    lp = nb * BLK
    shards = dict(ev_w_in=ev_w_in[0], ev_w_uq=ev_w_uq[0], ev_w_ukv=ev_w_ukv[0], ev_w_out=ev_w_out[0],
                  od_w_in=od_w_in[0], od_w_out=od_w_out[0])

    w_ie_blocks, meta_blocks = _all_gather([shards["ev_w_in"].astype(BF16), meta], "gather_weights")
    meta_full = meta_blocks.transpose(1, 0, 2).reshape(N_META, d)

    head = jnp.concatenate([jnp.zeros((N_PAD, d), F32), meta_full], axis=0)
    h0 = jnp.concatenate([jnp.broadcast_to(head[None], (nbatch, BLK, d)), x], axis=1).reshape(nbatch * lp, d)
    grads = _local_step(
        h0, loss_target.reshape(nbatch * seq, d), norm_g, final_g.reshape(1, d), ev_q_norm_g, ev_kv_norm_g,
        od_sinks, _from_blocks(w_ie_blocks, _EV_IN_PLAN),
        {name: shards[name].astype(BF16) for name in _LATE}, nbatch, nb)
    dh0 = grads["dh0"].reshape(nbatch, lp, d)
    grad_x = dh0[:, BLK:]

    d_meta = dh0[:, N_PAD:BLK].sum(axis=0).reshape(N_META, N_DEV, BLK).transpose(1, 0, 2)
    pad = lambda a, n: jnp.concatenate([a.reshape(1, -1), jnp.zeros((1, n - a.size), F32)], axis=1)
    loss_part = (0.5 / d * jnp.sum(grads["lossv"])).reshape(1, 1)
    rep = jnp.concatenate([grads["norm_g"].reshape(1, -1), grads["final_g"], grads["gq"], pad(loss_part, 256),
                           pad(grads["gkv"], 256), pad(grads["sinks"], 256)], axis=1).reshape(32, BLK)
    small = jnp.concatenate([d_meta, jnp.broadcast_to(rep[None], (N_DEV, 32, BLK))], axis=1)
    reduced = _exchange_sum([small], grads["received"], "reduce_grads")
    g_shard = dict(zip(("small",) + _EARLY_GRADS + _LAST_GRADS, reduced))
    red_small = g_shard.pop("small")
    rep = red_small[N_META:].reshape(1, -1)
    loss = rep[0, 3 * d + 256]
    g_small = dict(meta=red_small[:N_META], norm_g=rep[:, :2 * d].reshape(2, d), final_g=rep[:, 2 * d:3 * d],
                   ev_q_norm_g=rep[:, 3 * d:3 * d + 256], ev_kv_norm_g=rep[:, 3 * d + 512:3 * d + 640],
                   od_sinks=rep[:, 3 * d + 768:3 * d + 784])

    names = ["meta", "norm_g", "final_g", "ev_w_in", "ev_q_norm_g", "ev_kv_norm_g", "ev_w_uq", "ev_w_ukv",
             "ev_w_out", "od_w_in", "od_sinks", "od_w_out"]
    given = dict(meta=(meta, m_meta, v_meta), norm_g=(norm_g, m_norm_g, v_norm_g),
                 final_g=(final_g, m_final_g, v_final_g), ev_w_in=(ev_w_in, m_ev_w_in, v_ev_w_in),
                 ev_q_norm_g=(ev_q_norm_g, m_ev_q_norm_g, v_ev_q_norm_g),
                 ev_kv_norm_g=(ev_kv_norm_g, m_ev_kv_norm_g, v_ev_kv_norm_g),
                 ev_w_uq=(ev_w_uq, m_ev_w_uq, v_ev_w_uq), ev_w_ukv=(ev_w_ukv, m_ev_w_ukv, v_ev_w_ukv),
                 ev_w_out=(ev_w_out, m_ev_w_out, v_ev_w_out), od_w_in=(od_w_in, m_od_w_in, v_od_w_in),
                 od_sinks=(od_sinks, m_od_sinks, v_od_sinks), od_w_out=(od_w_out, m_od_w_out, v_od_w_out))
    ws, gs, ms, vs = [], [], [], []
    for name in names:
        g2 = g_shard[name] if name in g_shard else g_small[name]
        w, m, v = given[name]
        ws.append(w.reshape(g2.shape))
        ms.append(m.reshape(g2.shape))
        vs.append(v.reshape(g2.shape))
        gs.append(g2)
    big = [k for k, name in enumerate(names) if name in _BIG]
    small = [k for k, name in enumerate(names) if name not in _BIG]
    pick = lambda xs, ks: [xs[k] for k in ks]
    upd_big = _adamw(pick(ws, big), pick(gs, big), pick(ms, big), pick(vs, big), "adamw", N_DEV)
    upd_small = _adamw(pick(ws, small), pick(gs, small), pick(ms, small), pick(vs, small), "adamw_small", 1)
    upd = dict(zip(big + small, upd_big + upd_small))
    shape_of = {name: given[name][0].shape for name in names}
    grads_out = [gs[k].reshape(shape_of[n]) for k, n in enumerate(names)]
    deltas = [upd[k][0].reshape(shape_of[n]) for k, n in enumerate(names)]
    new_m = [upd[k][1].reshape(shape_of[n]) for k, n in enumerate(names)]
    new_v = [upd[k][2].reshape(shape_of[n]) for k, n in enumerate(names)]
    return (loss, grad_x, *grads_out, *deltas, *new_m, *new_v)
```

```python
import jax
import jax.numpy as jnp
from jax import lax
from jax.experimental import pallas as pl
from jax.experimental.pallas import tpu as pltpu

F32 = jnp.float32
BF16 = jnp.bfloat16

D_MODEL = 1024
N_META = 16
BLK = 128
HALF = 64
N_PAD = BLK - N_META
NORM_EPS = 1e-6
NEG = -1e30
N_DEV = 8

SB_SCALE = 64 ** -0.5
MLA_SCALE = 96 ** -0.5
SWA_SCALE = 64 ** -0.5
ROPE_BASE = 10000.0

EV_IN_PAD = 3072
EV_MID_BLK = 4
EV_GMLA_BLK = 20
OD_K_BLK = 16
OD_V_BLK = 17

ADAM_LR = 0.001
ADAM_B1 = 0.9
ADAM_B2 = 0.999
ADAM_EPS = 1e-08
ADAM_WD = 0.01
ADAM_STEP = 10


def _dot(a, b):
    return lax.dot_general(a, b, (((1,), (0,)), ((), ())), preferred_element_type=F32)


def _dot_nt(a, b):
    return lax.dot_general(a, b, (((1,), (1,)), ((), ())), preferred_element_type=F32)


def _dot_tn(a, b):
    return lax.dot_general(a, b, (((0,), (0,)), ((), ())), preferred_element_type=F32)


def _sigmoid(x):
    return 1.0 / (1.0 + jnp.exp(-x))


def _iotas():
    row = lax.broadcasted_iota(jnp.int32, (BLK, BLK), 0)
    lane = lax.broadcasted_iota(jnp.int32, (BLK, BLK), 1)
    return row, lane


WIDE = 2 * BLK
HEAD_GROUPS = (range(0, 8),)


def _key_mask(i, first_key, width, strict):
    t_pos = i * BLK + lax.broadcasted_iota(jnp.int32, (BLK, width), 0)
    s_pos = first_key + lax.broadcasted_iota(jnp.int32, (BLK, width), 1)
    seen = (s_pos < t_pos) if strict else (s_pos <= t_pos)
    return seen & (s_pos >= N_PAD)


def _widen(x, width):
    return x if width == BLK else jnp.concatenate([x] * (width // BLK), axis=1)


def _over_key_blocks(n, block, reverse):
    pairs = n // 2
    last = pl.multiple_of((n - 1) * BLK, BLK)

    def step(jj, carry):
        jp = (pairs - 1 - jj) if reverse else jj
        off = pl.multiple_of(jp * WIDE, WIDE)
        edge = (jp == 0) | (jp == pairs - 1)
        pl.when(edge)(lambda: block(off, WIDE, True))
        pl.when(jnp.logical_not(edge))(lambda: block(off, WIDE, False))
        return carry

    if reverse:
        pl.when(n % 2 == 1)(lambda: block(last, BLK, True))
        lax.fori_loop(0, pairs, step, 0)
    else:
        lax.fori_loop(0, pairs, step, 0)
        pl.when(n % 2 == 1)(lambda: block(last, BLK, True))


def _sel(mask, x, fill):
    return x if mask is None else jnp.where(mask, x, fill)


def _rope(x, c, s1, s2):
    return x * c + pltpu.roll(x, BLK - 16, 1) * s1 + pltpu.roll(x, 16, 1) * s2


def _rope_t(x, c, s1, s2):
    return x * c - pltpu.roll(x, BLK - 16, 1) * s1 - pltpu.roll(x, 16, 1) * s2


def _params(sem, vmem_mb=None):
    kw = dict(dimension_semantics=sem)
    if vmem_mb is not None:
        kw["vmem_limit_bytes"] = vmem_mb << 20
    return pltpu.CompilerParams(**kw)


def _row_tile(t, cands):
    for c in cands:
        if t % c == 0:
            return c
    raise ValueError(t)


def _mm(a, w, mode, name, add=None, out_dtype=F32):
    pieces = list(a) if isinstance(a, (list, tuple)) else [a]
    m = pieces[0].shape[0]
    n = w.shape[1] if mode == "nn" else w.shape[0]
    tm = _row_tile(m, (544, 256, 128) if n <= 1024 else (256, 128))
    widths = [p.shape[1] for p in pieces]
    offs = [sum(widths[:i]) for i in range(len(widths))]

    def body(*refs):
        p_refs = refs[:len(pieces)]
        w_ref = refs[len(pieces)]
        o_ref = refs[-1]
        acc = None
        for p_ref, off, wd in zip(p_refs, offs, widths):
            x = p_ref[...].astype(BF16)
            part = _dot(x, w_ref[off:off + wd, :]) if mode == "nn" else _dot_nt(x, w_ref[:, off:off + wd])
            acc = part if acc is None else acc + part
        if add is not None:
            acc = acc + refs[len(pieces) + 1][...]
        o_ref[...] = acc.astype(o_ref.dtype)

    in_specs = [pl.BlockSpec((tm, wd), lambda i: (i, 0)) for wd in widths]
    in_specs.append(pl.BlockSpec(w.shape, lambda i: (0, 0)))
    args = pieces + [w]
    if add is not None:
        in_specs.append(pl.BlockSpec((tm, n), lambda i: (i, 0)))
        args.append(add)
    return pl.pallas_call(
        body, name=name, grid=(m // tm,), in_specs=in_specs,
        out_specs=pl.BlockSpec((tm, n), lambda i: (i, 0)),
        out_shape=jax.ShapeDtypeStruct((m, n), out_dtype),
        compiler_params=_params(("parallel",), 48),
    )(*args)


def _mm_tn(x, pieces, name, xcols=None, chunked=()):
    t = x.shape[0]
    k, xj = (x.shape[1], 0) if xcols is None else xcols
    tt = _row_tile(t, (544, 256, 128))
    steps = t // tt
    widths = [p.shape[1] for p in pieces]
    np_, ns = len(pieces), len(chunked)

    def body(*refs):
        x_ref = refs[0]
        d_refs = refs[1:1 + np_]
        chunk_refs = refs[1 + np_:1 + np_ + ns]
        o_refs = refs[1 + np_ + ns:1 + 2 * np_ + ns]
        received_refs = refs[1 + 2 * np_ + ns:1 + 2 * np_ + 2 * ns]
        i = pl.program_id(0)
        if ns:
            _exchange_beside(i == 0, i == steps - 1, chunk_refs, received_refs, refs[1 + 2 * np_ + 2 * ns:],
                             chunked=True)

        @pl.when(i == 0)
        def _():
            for o_ref in o_refs:
                o_ref[...] = jnp.zeros_like(o_ref)

        xb = x_ref[...].astype(BF16)
        for d_ref, o_ref in zip(d_refs, o_refs):
            o_ref[...] += _dot_tn(xb, d_ref[...].astype(BF16))

    hbm = pl.BlockSpec(memory_space=pl.ANY)
    outs = pl.pallas_call(
        body, name=name, grid=(steps,),
        in_specs=[pl.BlockSpec((tt, k), lambda i: (i, xj))] + [pl.BlockSpec((tt, wd), lambda i: (i, 0)) for wd in widths]
        + [hbm] * ns,
        out_specs=[pl.BlockSpec((k, wd), lambda i: (0, 0)) for wd in widths] + [hbm] * ns,
        out_shape=[jax.ShapeDtypeStruct((k, wd), F32) for wd in widths]
        + [jax.ShapeDtypeStruct(a.shape, a.dtype) for a in chunked],
        scratch_shapes=_exchange_sems(ns) if ns else [],
        compiler_params=_params(("arbitrary",), 56),
    )(x, *pieces, *chunked)
    return (list(outs[:np_]), list(outs[np_:])) if ns else outs


def _norm_mm(h, g, w, name):
    t, d = h.shape
    n = w.shape[1]
    tr = _row_tile(t, (544, 256, 128))

    def body(h_ref, g_ref, w_ref, o_ref, hn_ref):
        x = h_ref[...]
        r = lax.rsqrt(jnp.mean(x * x, axis=1, keepdims=True) + NORM_EPS)
        hn = (x * r * g_ref[...]).astype(BF16)
        hn_ref[...] = hn
        o_ref[...] = _dot(hn, w_ref[...]).astype(o_ref.dtype)

    row = lambda width: pl.BlockSpec((tr, width), lambda i: (i, 0))
    return pl.pallas_call(
        body, name=name, grid=(t // tr,),
        in_specs=[row(d), pl.BlockSpec((1, d), lambda i: (0, 0)), pl.BlockSpec(w.shape, lambda i: (0, 0))],
        out_specs=[row(n), row(d)],
        out_shape=[jax.ShapeDtypeStruct((t, n), BF16), jax.ShapeDtypeStruct((t, d), BF16)],
        compiler_params=_params(("parallel",), 48),
    )(h, g, w)


def _mm_norm_bwd(pieces, w, h, g, dres, name, chunked=()):
    t, d = h.shape
    tr = _row_tile(t, (544, 256, 128))
    steps = t // tr
    widths = [p.shape[1] for p in pieces]
    offs = [sum(widths[:i]) for i in range(len(widths))]
    np_, ns = len(pieces), len(chunked)

    def body(*refs):
        p_refs = refs[:np_]
        w_ref, h_ref, g_ref, dres_ref = refs[np_:np_ + 4]
        chunk_refs = refs[np_ + 4:np_ + 4 + ns]
        dh_ref, dg_ref = refs[np_ + 4 + ns:np_ + 6 + ns]
        received_refs = refs[np_ + 6 + ns:np_ + 6 + 2 * ns]
        i = pl.program_id(0)
        if ns:
            _exchange_beside(i == 0, i == steps - 1, chunk_refs, received_refs, refs[np_ + 6 + 2 * ns:], chunked=True)

        @pl.when(i == 0)
        def _():
            dg_ref[...] = jnp.zeros_like(dg_ref)

        dy = None
        for p_ref, off, wd in zip(p_refs, offs, widths):
            part = _dot_nt(p_ref[...].astype(BF16), w_ref[:, off:off + wd])
            dy = part if dy is None else dy + part
        x = h_ref[...]
        r = lax.rsqrt(jnp.mean(x * x, axis=1, keepdims=True) + NORM_EPS)
        nx = x * r
        dn = dy * g_ref[...]
        dh_ref[...] = dres_ref[...] + r * (dn - nx * jnp.mean(dn * nx, axis=1, keepdims=True))
        dg_ref[...] += jnp.sum(dy * nx, axis=0, keepdims=True)

    row = lambda width: pl.BlockSpec((tr, width), lambda i: (i, 0))
    vec = pl.BlockSpec((1, d), lambda i: (0, 0))
    hbm = pl.BlockSpec(memory_space=pl.ANY)
    outs = pl.pallas_call(
        body, name=name, grid=(steps,),
        in_specs=[row(wd) for wd in widths] + [pl.BlockSpec(w.shape, lambda i: (0, 0)), row(d), vec, row(d)]
        + [hbm] * ns,
        out_specs=[row(d), vec] + [hbm] * ns,
        out_shape=[jax.ShapeDtypeStruct((t, d), F32), jax.ShapeDtypeStruct((1, d), F32)]
        + [jax.ShapeDtypeStruct(a.shape, a.dtype) for a in chunked],
        scratch_shapes=_exchange_sems(ns) if ns else [],
        compiler_params=_params(("arbitrary",), 56),
    )(*pieces, w, h, g, dres, *chunked)
    return outs[0], outs[1], list(outs[2:])


def _final(h2, tgt, g, nbatch, nb):
    t, d = h2.shape

    def body(h_ref, t_ref, g_ref, dh_ref, loss_ref, dg_ref):
        b = pl.program_id(0)
        i = pl.program_id(1)

        @pl.when((b == 0) & (i == 0))
        def _():
            loss_ref[...] = jnp.zeros_like(loss_ref)
            dg_ref[...] = jnp.zeros_like(dg_ref)

        x = h_ref[...]
        r = lax.rsqrt(jnp.mean(x * x, axis=1, keepdims=True) + NORM_EPS)
        nx = x * r
        gg = g_ref[...]
        live = jnp.where(i >= 1, 1.0, 0.0)
        err = (nx * gg - t_ref[...]) * live
        loss_ref[...] += jnp.sum(err * err, axis=0, keepdims=True)
        dy = err * (1.0 / d)
        dn = dy * gg
        dh_ref[...] = r * (dn - nx * jnp.mean(dn * nx, axis=1, keepdims=True))
        dg_ref[...] += jnp.sum(dy * nx, axis=0, keepdims=True)

    vec = pl.BlockSpec((1, d), lambda b, i: (0, 0))
    return pl.pallas_call(
        body, name="final_loss", grid=(nbatch, nb),
        in_specs=[pl.BlockSpec((BLK, d), lambda b, i: (b * nb + i, 0)),
                  pl.BlockSpec((BLK, d), lambda b, i: (b * (nb - 1) + jnp.maximum(i - 1, 0), 0)),
                  vec],
        out_specs=[pl.BlockSpec((BLK, d), lambda b, i: (b * nb + i, 0)), vec, vec],
        out_shape=[jax.ShapeDtypeStruct((t, d), F32), jax.ShapeDtypeStruct((1, d), F32),
                   jax.ShapeDtypeStruct((1, d), F32)],
        compiler_params=_params(("arbitrary", "arbitrary")),
    )(h2, tgt, g)


def _direct_exchange(srcs, dsts, send_sems, recv_sems, local_sems, chunked):
    x, y, c = lax.axis_index("x"), lax.axis_index("y"), lax.axis_index("c")
    me = 4 * x + 2 * y + c
    arrays = range(len(srcs))
    local = [pltpu.make_async_copy(srcs[a].at[me] if chunked else srcs[a], dsts[a].at[me], local_sems.at[a])
             for a in arrays]
    sends, recvs = [], []
    for d in range(1, N_DEV):
        px = x + ((d >> 2) & 1) - 2 * x * ((d >> 2) & 1)
        py = y + ((d >> 1) & 1) - 2 * y * ((d >> 1) & 1)
        pc = c + (d & 1) - 2 * c * (d & 1)
        pid = 4 * px + 2 * py + pc
        for a in arrays:
            kw = dict(send_sem=send_sems.at[d - 1, a], recv_sem=recv_sems.at[d - 1, a],
                      device_id=(px, py, pc), device_id_type=pl.DeviceIdType.MESH)
            src = srcs[a].at[pid] if chunked else srcs[a]
            sends.append(pltpu.make_async_remote_copy(src_ref=src, dst_ref=dsts[a].at[me], **kw))
            recvs.append(pltpu.make_async_remote_copy(src_ref=src, dst_ref=dsts[a].at[pid], **kw))
    return local, sends, recvs


def _exchange_beside(first, last, srcs, dsts, sems, chunked):
    local, sends, recvs = _direct_exchange(srcs, dsts, *sems, chunked)

    @pl.when(first)
    def _():
        for cp in local + sends:
            cp.start()

    @pl.when(last)
    def _():
        for cp in local:
            cp.wait()
        for cp in recvs:
            cp.wait_recv()
        for cp in sends:
            cp.wait_send()


def _exchange_sems(n):
    return [pltpu.SemaphoreType.DMA((7, n)), pltpu.SemaphoreType.DMA((7, n)), pltpu.SemaphoreType.DMA((n,))]


def _sb_logits(z):
    log_beta = jnp.minimum(z, 0.0) - jnp.log(1.0 + jnp.exp(-jnp.abs(z)))
    return log_beta, log_beta - z


def _tri(width, after):
    j = lax.broadcasted_iota(jnp.int32, (width, width), 0)
    s = lax.broadcasted_iota(jnp.int32, (width, width), 1)
    return (j > s) if after else (j < s)


def _tri_ones(tri):
    return jnp.concatenate([tri.astype(BF16), jnp.ones((tri.shape[0], BLK), BF16)], axis=1)


def _block_sums(x, tri_ones, after):
    xb = x.astype(BF16)
    subs = [_dot(xb[:, s:s + BLK], tri_ones) for s in range(0, x.shape[1], BLK)]
    if len(subs) == 1:
        return subs[0][:, :BLK], subs[0][:, BLK:]
    first, second = subs
    total = first[:, BLK:] + second[:, BLK:]
    if after:
        return jnp.concatenate([first[:, :BLK] + second[:, BLK:], second[:, :BLK]], axis=1), total
    return jnp.concatenate([first[:, :BLK], second[:, :BLK] + first[:, BLK:]], axis=1), total


def _head_masked(x, lane, scale=None):
    out = []
    for h in range(8):
        xp = x[:, (h // 2) * BLK:(h // 2 + 1) * BLK]
        xm = jnp.where((lane >= HALF) if h % 2 else (lane < HALF), xp, jnp.zeros_like(xp))
        out.append(xm if scale is None else xm * scale)
    return jnp.concatenate(out, axis=1)


def _sb_fwd(proj, shards, nbatch, nb):
    lp = nb * BLK
    t = nbatch * lp

    ns = len(shards)

    def body(*refs):
        q_ref, k_ref, v_ref, g_ref = refs[:4]
        shard_refs = refs[4:4 + ns]
        o_ref, og_ref, cm_ref = refs[4 + ns:7 + ns]
        gathered_refs = refs[7 + ns:7 + 2 * ns]
        c_scr, qm_scr = refs[7 + 2 * ns:9 + 2 * ns]
        b = pl.program_id(0)
        i = pl.program_id(1)
        _exchange_beside((b == 0) & (i == 0), (b == nbatch - 1) & (i == nb - 1),
                         shard_refs, gathered_refs, refs[9 + 2 * ns:], chunked=False)
        _, lane = _iotas()
        lo_m = lane < HALF
        cm_ref[...] = jnp.zeros_like(cm_ref)
        c_scr[...] = jnp.zeros_like(c_scr)
        o_ref[...] = jnp.zeros_like(o_ref)
        qm_scr[...] = _head_masked(q_ref[...], lane, SB_SCALE)

        def block(off, width, edge):
            mask = _key_mask(i, off, width, strict=True) if edge else None
            upper = _tri_ones(_tri(BLK, after=True))
            onehot = lane == off // WIDE
            heads = range(8)
            hcs = [slice(h * BLK, (h + 1) * BLK) for h in heads]
            kbs = [k_ref[pl.ds(off, width), hc] for hc in hcs[:4]]
            vbs = [v_ref[pl.ds(off, width), hc] for hc in hcs[:4]]
            zs = [_dot_nt(qm_scr[:, hcs[h]], kbs[h // 2]) for h in heads]
            lbs, l1s = [], []
            for h in heads:
                log_beta, log_1m = _sb_logits(zs[h])
                lbs.append(log_beta)
                l1s.append(_sel(mask, log_1m, 0.0))
            css = [_block_sums(l1s[h], upper, after=True) for h in heads]
            avs = []
            for h in heads:
                c = c_scr[h]
                avs.append(_sel(mask, jnp.exp(lbs[h] + css[h][0] + _widen(c, width)), 0.0).astype(BF16))
                if width == WIDE:
                    cm_ref[:, hcs[h]] = jnp.where(onehot, c, cm_ref[:, hcs[h]])
                c_scr[h] = c + css[h][1]
            accs = [_dot(avs[h], vbs[h // 2]) for h in heads]
            for p in range(4):
                o_ref[:, hcs[p]] += jnp.where(lo_m, accs[2 * p], accs[2 * p + 1])

        _over_key_blocks(i + 1, block, reverse=True)
        g = g_ref[...].astype(F32)
        og_ref[...] = (o_ref[...] * g * _sigmoid(g)).astype(og_ref.dtype)

    tile = lambda col: pl.BlockSpec((BLK, 512), lambda b, i: (b * nb + i, col))
    full = lambda col: pl.BlockSpec((lp, 512), lambda b, i: (b, col))
    hbm = pl.BlockSpec(memory_space=pl.ANY)
    outs = pl.pallas_call(
        body, name="sb_fwd", grid=(nbatch, nb),
        in_specs=[tile(0), full(1), full(2), tile(3)] + [hbm] * ns,
        out_specs=[tile(0), tile(0), pl.BlockSpec((BLK, 1024), lambda b, i: (b * nb + i, 0))] + [hbm] * ns,
        out_shape=[jax.ShapeDtypeStruct((t, 512), F32), jax.ShapeDtypeStruct((t, 512), BF16),
                   jax.ShapeDtypeStruct((t, 1024), F32)]
        + [jax.ShapeDtypeStruct((N_DEV,) + s.shape, s.dtype) for s in shards],
        scratch_shapes=[pltpu.VMEM((8, BLK, BLK), F32), pltpu.VMEM((BLK, 1024), BF16)] + _exchange_sems(ns),
        compiler_params=_params(("arbitrary", "arbitrary"), 48),
    )(proj, proj, proj, proj, *shards)
    return outs[0], outs[1], outs[2], list(outs[3:])


def _sb_bwd(proj, o, dog, cm, chunked, nbatch, nb):
    lp = nb * BLK
    t = nbatch * lp
    ns = len(chunked)

    def body(*refs):
        q_ref, k_ref, v_ref, g_ref, o_ref, dog_ref, cm_ref = refs[:7]
        chunk_refs = refs[7:7 + ns]
        dq_ref, dk_ref, dv_ref, dg_ref = refs[7 + ns:11 + ns]
        received_refs = refs[11 + ns:11 + 2 * ns]
        c_scr, qm_scr, dom_scr, dq_scr = refs[11 + 2 * ns:15 + 2 * ns]
        b = pl.program_id(0)
        i = pl.program_id(1)
        _exchange_beside((b == 0) & (i == 0), (b == nbatch - 1) & (i == nb - 1),
                         chunk_refs, received_refs, refs[15 + 2 * ns:], chunked=True)

        @pl.when(i == 0)
        def _():
            dk_ref[...] = jnp.zeros_like(dk_ref)
            dv_ref[...] = jnp.zeros_like(dv_ref)

        _, lane = _iotas()
        lo_m = lane < HALF
        g = g_ref[...].astype(F32)
        sig = _sigmoid(g)
        dog_v = dog_ref[...]
        dg_ref[...] = (dog_v * o_ref[...] * (sig * (1.0 + g * (1.0 - sig)))).astype(dg_ref.dtype)
        dom_scr[...] = _head_masked((dog_v * g * sig).astype(BF16), lane)
        qm_scr[...] = _head_masked(q_ref[...], lane, SB_SCALE)
        c_scr[...] = jnp.zeros_like(c_scr)
        dq_scr[...] = jnp.zeros_like(dq_scr)

        def block(off, width, edge):
            mask = _key_mask(i, off, width, strict=True) if edge else None
            upper = _tri_ones(_tri(BLK, after=True))
            lower = _tri_ones(_tri(BLK, after=False))
            onehot = lane == off // WIDE
            hcs = [slice(h * BLK, (h + 1) * BLK) for h in range(8)]
            kbs = [k_ref[pl.ds(off, width), hc] for hc in hcs[:4]]
            vbs = [v_ref[pl.ds(off, width), hc] for hc in hcs[:4]]
            for heads in HEAD_GROUPS:
                zs = {h: _dot_nt(qm_scr[:, hcs[h]], kbs[h // 2]) for h in heads}
                dps = {h: _dot_nt(dom_scr[:, hcs[h]], vbs[h // 2]) for h in heads}
                lbs, l1s = {}, {}
                for h in heads:
                    lbs[h], l1s[h] = _sb_logits(zs[h])
                sufs = {h: _block_sums(_sel(mask, l1s[h], 0.0), upper, after=True)[0] for h in heads}
                prs, dzzs = {}, {}
                for h in heads:
                    expo = lbs[h] + sufs[h]
                    if width == WIDE:
                        expo = expo + jnp.sum(jnp.where(onehot, cm_ref[:, hcs[h]], 0.0), axis=1, keepdims=True)
                    pr = _sel(mask, jnp.exp(expo), 0.0)
                    dzzs[h] = pr * dps[h]
                    prs[h] = pr.astype(BF16)
                css = {h: _block_sums(dzzs[h], lower, after=False) for h in heads}
                dzbs = {}
                for h in heads:
                    c2 = c_scr[h]
                    prefix = css[h][0] + _widen(c2, width)
                    dz = _sel(mask, dzzs[h] * jnp.exp(l1s[h]) - jnp.exp(lbs[h]) * prefix, 0.0)
                    dzbs[h] = dz.astype(BF16)
                    c_scr[h] = c2 + css[h][1]
                dqs = {h: _dot(dzbs[h], kbs[h // 2]) for h in heads}
                dks = {h: _dot_tn(dzbs[h], qm_scr[:, hcs[h]]) for h in heads}
                dvs = {h: _dot_tn(prs[h], dom_scr[:, hcs[h]]) for h in heads}
                for p in sorted({h // 2 for h in heads}):
                    dq_scr[:, hcs[p]] += jnp.where(lo_m, dqs[2 * p], dqs[2 * p + 1])
                    dk_ref[pl.ds(off, width), hcs[p]] += dks[2 * p] + dks[2 * p + 1]
                    dv_ref[pl.ds(off, width), hcs[p]] += dvs[2 * p] + dvs[2 * p + 1]

        _over_key_blocks(i + 1, block, reverse=False)
        dq_ref[...] = (dq_scr[...] * SB_SCALE).astype(dq_ref.dtype)

    tile = lambda col: pl.BlockSpec((BLK, 512), lambda b, i: (b * nb + i, col))
    full = lambda col: pl.BlockSpec((lp, 512), lambda b, i: (b, col))
    acc = jax.ShapeDtypeStruct((t, 512), F32)
    once = jax.ShapeDtypeStruct((t, 512), BF16)
    hbm = pl.BlockSpec(memory_space=pl.ANY)
    outs = pl.pallas_call(
        body, name="sb_bwd", grid=(nbatch, nb),
        in_specs=[tile(0), full(1), full(2), tile(3), tile(0), tile(0),
                  pl.BlockSpec((BLK, 1024), lambda b, i: (b * nb + i, 0))] + [hbm] * ns,
        out_specs=[tile(0), full(0), full(0), tile(0)] + [hbm] * ns,
        out_shape=[once, acc, acc, once] + [jax.ShapeDtypeStruct(a.shape, a.dtype) for a in chunked],
        scratch_shapes=[pltpu.VMEM((8, BLK, BLK), F32), pltpu.VMEM((BLK, 1024), BF16),
                        pltpu.VMEM((BLK, 1024), BF16), pltpu.VMEM((BLK, 512), F32)] + _exchange_sems(ns),
        compiler_params=_params(("arbitrary", "arbitrary"), 56),
    )(proj, proj, proj, proj, o, dog, cm, *chunked)
    return outs[0], outs[1], outs[2], outs[3], list(outs[4:])


def _mla_prep(proj, gq, gkv, tabs, nbatch, nb):
    t = proj.shape[0]

    def body(mid_ref, gq_ref, gkv_ref, c_ref, s1_ref, s2_ref, cq_ref, ckv_ref, kr_ref):
        cq = mid_ref[:, 0:256].astype(F32)
        r = lax.rsqrt(jnp.mean(cq * cq, axis=1, keepdims=True) + NORM_EPS)
        cq_ref[...] = (cq * r * gq_ref[...]).astype(BF16)
        ckv = mid_ref[:, 256:384].astype(F32)
        r = lax.rsqrt(jnp.mean(ckv * ckv, axis=1, keepdims=True) + NORM_EPS)
        ckv_ref[...] = (ckv * r * gkv_ref[...]).astype(BF16)
        kr = mid_ref[:, 384:512].astype(F32)
        kr_ref[...] = _rope(kr, c_ref[...], s1_ref[...], s2_ref[...]).astype(BF16)

    tr = _row_tile(nb * BLK, (544, BLK))
    nt = nb * BLK // tr
    tab = pl.BlockSpec((tr, BLK), lambda b, i: (i, 0))
    rowspec = lambda w: pl.BlockSpec((tr, w), lambda b, i: (b * nt + i, 0))
    return pl.pallas_call(
        body, name="mla_prep", grid=(nbatch, nt),
        in_specs=[pl.BlockSpec((tr, 512), lambda b, i: (b * nt + i, EV_MID_BLK)),
                  pl.BlockSpec((1, 256), lambda b, i: (0, 0)), pl.BlockSpec((1, 128), lambda b, i: (0, 0)),
                  tab, tab, tab],
        out_specs=[rowspec(256), rowspec(128), rowspec(128)],
        out_shape=[jax.ShapeDtypeStruct((t, 256), BF16), jax.ShapeDtypeStruct((t, 128), BF16),
                   jax.ShapeDtypeStruct((t, 128), BF16)],
        compiler_params=_params(("parallel", "parallel")),
    )(proj, gq, gkv, *tabs)


def _mla_prep_bwd(proj, gq, gkv, tabs, dcqn, dckvn, dkrot, nbatch, nb):
    t = proj.shape[0]

    def body(mid_ref, gq_ref, gkv_ref, c_ref, s1_ref, s2_ref, dcq_ref, dckv_ref, dkr_ref,
             dmid_ref, dgq_ref, dgkv_ref):
        @pl.when((pl.program_id(0) == 0) & (pl.program_id(1) == 0))
        def _():
            dgq_ref[...] = jnp.zeros_like(dgq_ref)
            dgkv_ref[...] = jnp.zeros_like(dgkv_ref)

        def norm_bwd(x, gain, dy, dgain_ref):
            r = lax.rsqrt(jnp.mean(x * x, axis=1, keepdims=True) + NORM_EPS)
            nx = x * r
            dn = dy * gain
            dgain_ref[...] += jnp.sum(dy * nx, axis=0, keepdims=True)
            return r * (dn - nx * jnp.mean(dn * nx, axis=1, keepdims=True))

        dmid_ref[:, 0:256] = norm_bwd(
            mid_ref[:, 0:256].astype(F32), gq_ref[...], dcq_ref[...], dgq_ref).astype(BF16)
        dmid_ref[:, 256:384] = norm_bwd(
            mid_ref[:, 256:384].astype(F32), gkv_ref[...], dckv_ref[...], dgkv_ref).astype(BF16)
        dmid_ref[:, 384:512] = _rope_t(dkr_ref[...], c_ref[...], s1_ref[...], s2_ref[...]).astype(BF16)

    tr = _row_tile(nb * BLK, (544, BLK))
    nt = nb * BLK // tr
    tab = pl.BlockSpec((tr, BLK), lambda b, i: (i, 0))
    rowspec = lambda w: pl.BlockSpec((tr, w), lambda b, i: (b * nt + i, 0))
    vq = pl.BlockSpec((1, 256), lambda b, i: (0, 0))
    vkv = pl.BlockSpec((1, 128), lambda b, i: (0, 0))
    return pl.pallas_call(
        body, name="mla_prep_bwd", grid=(nbatch, nt),
        in_specs=[pl.BlockSpec((tr, 512), lambda b, i: (b * nt + i, EV_MID_BLK)), vq, vkv, tab, tab, tab,
                  rowspec(256), rowspec(128), rowspec(128)],
        out_specs=[rowspec(512), vq, vkv],
        out_shape=[jax.ShapeDtypeStruct((t, 512), BF16), jax.ShapeDtypeStruct((1, 256), F32),
                   jax.ShapeDtypeStruct((1, 128), F32)],
        compiler_params=_params(("arbitrary", "arbitrary")),
    )(proj, gq, gkv, *tabs, dcqn, dckvn, dkrot)


def _mla_scores(qf, kvb, krb, mask, lo_m):
    kf = jnp.where(lo_m, kvb, krb)
    return kf, _sel(mask, _dot_nt(qf, kf), NEG)


def _mla_fwd(qh, kvh, krot, proj, tabs, nbatch, nb):
    lp = nb * BLK
    t = nbatch * lp

    def body(q_ref, kv_ref, kr_ref, g_ref, c_ref, s1_ref, s2_ref, o_ref, og_ref, lse_ref,
             qf_scr, m_scr, l_scr, acc_scr):
        i = pl.program_id(1)
        row, lane = _iotas()
        lo_m = lane < HALF
        for h in range(8):
            hc = slice(h * BLK, (h + 1) * BLK)
            qf_scr[:, hc] = (_rope(q_ref[:, hc].astype(F32), c_ref[...], s1_ref[...], s2_ref[...])
                             * MLA_SCALE).astype(BF16)
        m_scr[...] = jnp.full(m_scr.shape, NEG, F32)
        l_scr[...] = jnp.zeros_like(l_scr)
        acc_scr[...] = jnp.zeros_like(acc_scr)

        def block(off, width, edge):
            mask = _key_mask(i, off, width, strict=False) if edge else None
            lo_k = lax.broadcasted_iota(jnp.int32, (width, BLK), 1) < HALF
            ones = jnp.ones((width, BLK), BF16)
            krb = kr_ref[pl.ds(off, width), :]
            heads = range(8)
            hcs = [slice(h * BLK, (h + 1) * BLK) for h in heads]
            kvbs = [kv_ref[pl.ds(off, width), hc] for hc in hcs]
            ss = [_mla_scores(qf_scr[:, hcs[h]], kvbs[h], krb, mask, lo_k)[1] for h in heads]
            ps, alphas = [], []
            for h in heads:
                m = m_scr[h]
                m2 = jnp.maximum(m, jnp.max(ss[h], axis=1, keepdims=True))
                ps.append(jnp.exp(ss[h] - _widen(m2, width)).astype(BF16))
                alphas.append(jnp.exp(m - m2))
                m_scr[h] = m2
            pvs = [_dot(ps[h], jnp.concatenate([kvbs[h], ones], axis=1)) for h in heads]
            for h in heads:
                l_scr[h] = alphas[h] * l_scr[h] + pvs[h][:, BLK:]
                acc_scr[h] = alphas[h] * acc_scr[h] + pvs[h][:, :BLK]

        _over_key_blocks(i + 1, block, reverse=False)
        lse = jnp.zeros((BLK, BLK), F32)
        for p in range(4):
            pc = slice(p * BLK, (p + 1) * BLK)
            o0 = acc_scr[2 * p] / l_scr[2 * p]
            o1 = acc_scr[2 * p + 1] / l_scr[2 * p + 1]
            o_ref[:, pc] = jnp.where(lo_m, pltpu.roll(o0, HALF, 1), o1)
            for h in (2 * p, 2 * p + 1):
                lse = lse + jnp.where(lane == h, m_scr[h] + jnp.log(l_scr[h]), 0.0)
        lse_ref[...] = lse
        g = g_ref[...].astype(F32)
        og_ref[...] = (o_ref[...] * g * _sigmoid(g)).astype(og_ref.dtype)

    tile = pl.BlockSpec((BLK, 512), lambda b, i: (b * nb + i, 0))
    tab = pl.BlockSpec((BLK, BLK), lambda b, i: (i, 0))
    heads = pltpu.VMEM((8, BLK, BLK), F32)
    return pl.pallas_call(
        body, name="mla_fwd", grid=(nbatch, nb),
        in_specs=[pl.BlockSpec((BLK, 1024), lambda b, i: (b * nb + i, 0)),
                  pl.BlockSpec((lp, 1024), lambda b, i: (b, 0)),
                  pl.BlockSpec((lp, BLK), lambda b, i: (b, 0)),
                  pl.BlockSpec((BLK, 512), lambda b, i: (b * nb + i, EV_GMLA_BLK // 4)),
                  tab, tab, tab],
        out_specs=[tile, tile, pl.BlockSpec((BLK, BLK), lambda b, i: (b * nb + i, 0))],
        out_shape=[jax.ShapeDtypeStruct((t, 512), F32), jax.ShapeDtypeStruct((t, 512), BF16),
                   jax.ShapeDtypeStruct((t, BLK), F32)],
        scratch_shapes=[pltpu.VMEM((BLK, 1024), BF16), heads, heads, heads],
        compiler_params=_params(("parallel", "arbitrary"), 48),
    )(qh, kvh, krot, proj, *tabs)


def _mla_bwd(qh, kvh, krot, proj, tabs, o, dog, lse, nbatch, nb):
    lp = nb * BLK
    t = nbatch * lp

    def body(q_ref, kv_ref, kr_ref, g_ref, c_ref, s1_ref, s2_ref, o_ref, dog_ref, lse_ref,
             dq_ref, dkv_ref, dkr_ref, dg_ref, qf_scr, do_scr, stat_scr, acc_scr):
        i = pl.program_id(1)

        @pl.when(i == 0)
        def _():
            dkv_ref[...] = jnp.zeros_like(dkv_ref)
            dkr_ref[...] = jnp.zeros_like(dkr_ref)

        row, lane = _iotas()
        lo_m = lane < HALF
        g = g_ref[...].astype(F32)
        sig = _sigmoid(g)
        dog_v = dog_ref[...]
        o_v = o_ref[...]
        dg_ref[...] = (dog_v * o_v * (sig * (1.0 + g * (1.0 - sig)))).astype(dg_ref.dtype)
        do = dog_v * g * sig
        do_o = do * o_v
        lse_blk = lse_ref[...]
        zero = jnp.zeros((BLK, BLK), F32)
        for h in range(8):
            hc = slice(h * BLK, (h + 1) * BLK)
            pc = slice((h // 2) * BLK, (h // 2 + 1) * BLK)
            qf_scr[:, hc] = (_rope(q_ref[:, hc].astype(F32), c_ref[...], s1_ref[...], s2_ref[...])
                             * MLA_SCALE).astype(BF16)
            dop = do[:, pc]
            do_src = dop if h % 2 else pltpu.roll(dop, HALF, 1)
            do_scr[:, hc] = jnp.where(lo_m, 0.0, do_src).astype(BF16)
            hm = (lane >= HALF) if h % 2 else lo_m
            stat_scr[h] = zero + jnp.sum(jnp.where(hm, do_o[:, pc], 0.0), axis=1, keepdims=True)
            stat_scr[8 + h] = zero + jnp.sum(jnp.where(lane == h, lse_blk, 0.0), axis=1, keepdims=True)
        acc_scr[...] = jnp.zeros_like(acc_scr)

        def block(off, width, edge):
            mask = _key_mask(i, off, width, strict=False) if edge else None
            lo_k = lax.broadcasted_iota(jnp.int32, (width, BLK), 1) < HALF
            krb = kr_ref[pl.ds(off, width), :]
            heads = range(8)
            hcs = [slice(h * BLK, (h + 1) * BLK) for h in heads]
            kvbs = [kv_ref[pl.ds(off, width), hc] for hc in hcs]
            qfs = [qf_scr[:, hc] for hc in hcs]
            dos = [do_scr[:, hc] for hc in hcs]
            scored = [_mla_scores(qfs[h], kvbs[h], krb, mask, lo_k) for h in heads]
            dps = [_dot_nt(dos[h], kvbs[h]) for h in heads]
            pbs, dss = [], []
            for h in heads:
                p = jnp.exp(scored[h][1] - _widen(stat_scr[8 + h], width))
                pbs.append(p.astype(BF16))
                dss.append((p * (dps[h] - _widen(stat_scr[h], width))).astype(BF16))
            dqs = [_dot(dss[h], scored[h][0]) for h in heads]
            dkfs = [_dot_tn(dss[h], qfs[h]) for h in heads]
            dvvs = [_dot_tn(pbs[h], dos[h]) for h in heads]
            dkr = jnp.zeros((width, BLK), F32)
            for h in heads:
                acc_scr[h] += dqs[h]
                dkv_ref[pl.ds(off, width), hcs[h]] += jnp.where(lo_k, dkfs[h], 0.0) + dvvs[h]
                dkr = dkr + jnp.where(lo_k, 0.0, dkfs[h])
            dkr_ref[pl.ds(off, width), :] += dkr

        _over_key_blocks(i + 1, block, reverse=False)
        for h in range(8):
            hc = slice(h * BLK, (h + 1) * BLK)
            dq_ref[:, hc] = _rope_t(acc_scr[h] * MLA_SCALE, c_ref[...], s1_ref[...], s2_ref[...]).astype(dq_ref.dtype)

    tile = lambda col: pl.BlockSpec((BLK, 512), lambda b, i: (b * nb + i, col))
    wide = pl.BlockSpec((BLK, 1024), lambda b, i: (b * nb + i, 0))
    full8 = pl.BlockSpec((lp, 1024), lambda b, i: (b, 0))
    full1 = pl.BlockSpec((lp, BLK), lambda b, i: (b, 0))
    tab = pl.BlockSpec((BLK, BLK), lambda b, i: (i, 0))
    return pl.pallas_call(
        body, name="mla_bwd", grid=(nbatch, nb),
        in_specs=[wide, full8, full1, tile(EV_GMLA_BLK // 4), tab, tab, tab, tile(0), tile(1),
                  pl.BlockSpec((BLK, BLK), lambda b, i: (b * nb + i, 0))],
        out_specs=[wide, full8, full1, tile(0)],
        out_shape=[jax.ShapeDtypeStruct((t, 1024), BF16), jax.ShapeDtypeStruct((t, 1024), F32),
                   jax.ShapeDtypeStruct((t, 128), F32), jax.ShapeDtypeStruct((t, 512), BF16)],
        scratch_shapes=[pltpu.VMEM((BLK, 1024), BF16), pltpu.VMEM((BLK, 1024), BF16),
                        pltpu.VMEM((16, BLK, BLK), F32), pltpu.VMEM((8, BLK, BLK), F32)],
        compiler_params=_params(("parallel", "arbitrary"), 56),
    )(qh, kvh, krot, proj, *tabs, o, dog, lse)


def _swa_setup(kk, i, k_refs, v_refs):
    row, lane = _iotas()
    own = (lane >= kk * HALF) & (lane < (kk + 1) * HALF)

    def dup(ref):
        x = ref[...].astype(F32)
        return jnp.where(own, x, pltpu.roll(x, HALF, 1)).astype(BF16)

    kcat = jnp.concatenate([dup(r) for r in k_refs], axis=0)
    vcat = jnp.concatenate([dup(r) for r in v_refs], axis=0)
    row2 = lax.broadcasted_iota(jnp.int32, (BLK, 2 * BLK), 0)
    lane2 = lax.broadcasted_iota(jnp.int32, (BLK, 2 * BLK), 1)
    is_meta = lane2 < BLK
    in_own = lane2 - BLK <= row2
    k_pos = jnp.where(is_meta, lane2, jnp.where(in_own, (i - 1) * BLK, (i - 2) * BLK) + lane2)
    d = i * BLK + row2 - k_pos
    mask = (d >= 0) & (k_pos >= jnp.where(is_meta, N_PAD, BLK))
    return lane, lane <= row, own, kcat, vcat, mask, d.astype(F32)


def _swa_fold(x, in_own):
    return jnp.concatenate([x[:, :BLK], jnp.where(in_own, x[:, 2 * BLK:], x[:, BLK:2 * BLK])], axis=1)


def _swa_unfold(x, in_own):
    w = x[:, BLK:]
    zero = jnp.zeros_like(w)
    return jnp.concatenate([x[:, :BLK], jnp.where(in_own, zero, w), jnp.where(in_own, w, zero)], axis=1)


def _swa_slope(kk, g_idx):
    return (2.0 ** (-(g_idx + 1) / 2.0)) * jnp.where(kk == 0, 1.0, 1.0 / 16.0)


def _swa_fwd(proj, sinks, nbatch, nb):
    lp = nb * BLK
    t = nbatch * lp

    def body(sink_ref, q_ref, ka, kb, kc, va, vb, vc, g_ref, o_ref, og_ref, lse_ref):
        kk = pl.program_id(1)
        i = pl.program_id(2)
        lane, in_own, own, kcat, vcat, mask, dist = _swa_setup(kk, i, (ka, kb, kc), (va, vb, vc))
        lo_m = lane < HALF
        heads = range(8)
        qms = []
        for h in heads:
            qp = q_ref[:, (h // 2) * BLK:(h // 2 + 1) * BLK]
            qms.append(jnp.where((lane >= HALF) if h % 2 else lo_m, qp, jnp.zeros_like(qp)) * SWA_SCALE)
        qks = [_dot_nt(qms[h], kcat) for h in heads]
        ps, ls, lses = [], [], []
        for h in heads:
            sink = sink_ref[kk, h]
            s = jnp.where(mask, _swa_fold(qks[h], in_own) - _swa_slope(kk, h) * dist, NEG)
            mx = jnp.maximum(jnp.max(s, axis=1, keepdims=True), sink)
            p = jnp.exp(s - mx)
            l = jnp.exp(sink - mx) + jnp.sum(p, axis=1, keepdims=True)
            ps.append(_swa_unfold(p.astype(BF16), in_own))
            ls.append(l)
            lses.append(mx + jnp.log(l))
        pvs = [_dot(ps[h], vcat) for h in heads]
        lse_out = jnp.zeros((BLK, BLK), F32)
        for m in range(4):
            cols = slice(m * BLK, (m + 1) * BLK)
            outp = jnp.where(lo_m, pvs[2 * m] / ls[2 * m], pvs[2 * m + 1] / ls[2 * m + 1])
            o_ref[:, cols] = outp
            g = g_ref[:, cols].astype(F32)
            og_ref[:, cols] = (outp * g * _sigmoid(g)).astype(og_ref.dtype)
            for h in (2 * m, 2 * m + 1):
                lse_out = lse_out + jnp.where(lane == h, lses[h], 0.0)
        lse_ref[...] = lse_out

    def kvspec(col, which):
        if which == 0:
            return pl.BlockSpec((BLK, BLK), lambda b, kk, i: (b * nb, col))
        if which == 1:
            return pl.BlockSpec((BLK, BLK), lambda b, kk, i: (b * nb + jnp.maximum(i - 1, 0), col))
        return pl.BlockSpec((BLK, BLK), lambda b, kk, i: (b * nb + i, col))

    wide = lambda c0: pl.BlockSpec((BLK, 512), lambda b, kk, i: (b * nb + i, c0 + kk))
    return pl.pallas_call(
        body, name="swa_fwd", grid=(nbatch, 2, nb),
        in_specs=[pl.BlockSpec(memory_space=pltpu.SMEM), wide(0),
                  kvspec(OD_K_BLK, 0), kvspec(OD_K_BLK, 1), kvspec(OD_K_BLK, 2),
                  kvspec(OD_V_BLK, 0), kvspec(OD_V_BLK, 1), kvspec(OD_V_BLK, 2), wide(2)],
        out_specs=[wide(0), wide(0), pl.BlockSpec((BLK, BLK), lambda b, kk, i: (b * nb + i, kk))],
        out_shape=[jax.ShapeDtypeStruct((t, 1024), F32), jax.ShapeDtypeStruct((t, 1024), BF16),
                   jax.ShapeDtypeStruct((t, 256), F32)],
        compiler_params=_params(("parallel", "parallel", "arbitrary")),
    )(sinks, proj, proj, proj, proj, proj, proj, proj, proj)


def _swa_bwd(proj, sinks, o, dog, lse, nbatch, nb):
    lp = nb * BLK
    t = nbatch * lp

    def body(sink_ref, q_ref, ka, kb, kc, va, vb, vc, g_ref, o_ref, dog_ref, lse_ref,
             dq_ref, dg_ref, dk_ref, dv_ref, dsink_ref):
        kk = pl.program_id(1)
        i = pl.program_id(2)

        @pl.when((kk == 0) & (i == 0))
        def _():
            dk_ref[...] = jnp.zeros_like(dk_ref)
            dv_ref[...] = jnp.zeros_like(dv_ref)

        @pl.when(i == 0)
        def _():
            dsink_ref[...] = jnp.zeros_like(dsink_ref)

        lane, in_own, own, kcat, vcat, mask, dist = _swa_setup(kk, i, (ka, kb, kc), (va, vb, vc))
        lo_m = lane < HALF
        row8 = lax.broadcasted_iota(jnp.int32, (8, BLK), 0)
        lse_blk = lse_ref[...]
        heads = range(8)
        qms, doms, deltas, lse_hs = [], [], [], []
        for m in range(4):
            cols = slice(m * BLK, (m + 1) * BLK)
            qp = q_ref[:, cols]
            g = g_ref[:, cols].astype(F32)
            sig = _sigmoid(g)
            dog_v = dog_ref[:, cols]
            o_v = o_ref[:, cols]
            dg_ref[:, cols] = (dog_v * o_v * (sig * (1.0 + g * (1.0 - sig)))).astype(dg_ref.dtype)
            do = dog_v * g * sig
            do_o = do * o_v
            dob = do.astype(BF16)
            for h in (2 * m, 2 * m + 1):
                hm = (lane >= HALF) if h % 2 else lo_m
                qms.append(jnp.where(hm, qp, jnp.zeros_like(qp)) * SWA_SCALE)
                doms.append(jnp.where(hm, dob, jnp.zeros_like(dob)))
                deltas.append(jnp.sum(jnp.where(hm, do_o, 0.0), axis=1, keepdims=True))
                lse_hs.append(jnp.sum(jnp.where(lane == h, lse_blk, 0.0), axis=1, keepdims=True))
        qks = [_dot_nt(qms[h], kcat) for h in heads]
        dps = [_dot_nt(doms[h], vcat) for h in heads]
        pbs, dss = [], []
        dsink = jnp.zeros((8, BLK), F32)
        for h in heads:
            s = jnp.where(mask, _swa_fold(qks[h], in_own) - _swa_slope(kk, h) * dist, NEG)
            p = jnp.exp(s - lse_hs[h])
            pbs.append(_swa_unfold(p.astype(BF16), in_own))
            dss.append(_swa_unfold((p * (_swa_fold(dps[h], in_own) - deltas[h])).astype(BF16), in_own))
            tot = jnp.sum(-jnp.exp(sink_ref[kk, h] - lse_hs[h]) * deltas[h], axis=0, keepdims=True)
            dsink = dsink + jnp.where(row8 == h, tot, 0.0)
        dsink_ref[...] += dsink
        dqs = [_dot(dss[h], kcat) for h in heads]
        dks = [_dot_tn(dss[h], qms[h]) for h in heads]
        dvs = [_dot_tn(pbs[h], doms[h]) for h in heads]
        for m in range(4):
            dq_ref[:, m * BLK:(m + 1) * BLK] = (
                jnp.where(lo_m, dqs[2 * m], dqs[2 * m + 1]) * SWA_SCALE).astype(dq_ref.dtype)
        dk = dks[0]
        dv = dvs[0]
        for h in range(1, 8):
            dk = dk + dks[h]
            dv = dv + dvs[h]
        offs = [0, pl.multiple_of(jnp.maximum(i - 1, 0) * BLK, BLK), pl.multiple_of(i * BLK, BLK)]
        for x in range(3):
            rows = slice(x * BLK, (x + 1) * BLK)
            dkx, dvx = dk[rows], dv[rows]
            dk_ref[pl.ds(offs[x], BLK), :] += jnp.where(own, dkx + pltpu.roll(dkx, HALF, 1), 0.0)
            dv_ref[pl.ds(offs[x], BLK), :] += jnp.where(own, dvx + pltpu.roll(dvx, HALF, 1), 0.0)

    def kvspec(col, which):
        if which == 0:
            return pl.BlockSpec((BLK, BLK), lambda b, kk, i: (b * nb, col))
        if which == 1:
            return pl.BlockSpec((BLK, BLK), lambda b, kk, i: (b * nb + jnp.maximum(i - 1, 0), col))
        return pl.BlockSpec((BLK, BLK), lambda b, kk, i: (b * nb + i, col))

    wide = lambda c0: pl.BlockSpec((BLK, 512), lambda b, kk, i: (b * nb + i, c0 + kk))
    full = pl.BlockSpec((lp, BLK), lambda b, kk, i: (b, 0))
    return pl.pallas_call(
        body, name="swa_bwd", grid=(nbatch, 2, nb),
        in_specs=[pl.BlockSpec(memory_space=pltpu.SMEM), wide(0),
                  kvspec(OD_K_BLK, 0), kvspec(OD_K_BLK, 1), kvspec(OD_K_BLK, 2),
                  kvspec(OD_V_BLK, 0), kvspec(OD_V_BLK, 1), kvspec(OD_V_BLK, 2), wide(2),
                  wide(0), wide(0), pl.BlockSpec((BLK, BLK), lambda b, kk, i: (b * nb + i, kk))],
        out_specs=[wide(0), wide(0), full, full,
                   pl.BlockSpec((8, BLK), lambda b, kk, i: (b * 2 + kk, 0))],
        out_shape=[jax.ShapeDtypeStruct((t, 1024), BF16), jax.ShapeDtypeStruct((t, 1024), BF16),
                   jax.ShapeDtypeStruct((t, 128), F32), jax.ShapeDtypeStruct((t, 128), F32),
                   jax.ShapeDtypeStruct((nbatch * 16, BLK), F32)],
        compiler_params=_params(("parallel", "arbitrary", "arbitrary")),
    )(sinks, proj, proj, proj, proj, proj, proj, proj, proj, o, dog, lse)


def _rope_tables(lp):
    pos = (jnp.arange(lp) - N_PAD).astype(F32)
    inv = ROPE_BASE ** (-jnp.arange(16, dtype=F32) / 16.0)
    ang = pos[:, None] * inv[None, :]
    cos, sin = jnp.cos(ang), jnp.sin(ang)
    z16 = jnp.zeros((lp, 16), F32)
    c = jnp.concatenate([jnp.ones((lp, 64), F32), cos, cos, jnp.zeros((lp, 32), F32)], axis=1)
    s1 = jnp.concatenate([jnp.zeros((lp, 64), F32), -sin, z16, jnp.zeros((lp, 32), F32)], axis=1)
    s2 = jnp.concatenate([jnp.zeros((lp, 64), F32), z16, sin, jnp.zeros((lp, 32), F32)], axis=1)
    return c, s1, s2


def _local_step(h0, tgt, norm_g, final_g, gq, gkv, sinks, w_ie, late_shards, nbatch, nb):
    lp = nb * BLK
    tabs = _rope_tables(lp)
    g0, g1 = norm_g[0:1], norm_g[1:2]
    sinks2 = sinks.reshape(2, 8)

    proj_e, hn0 = _norm_mm(h0, g0, w_ie, "proj_even")
    o_sb, og_sb, cm, gathered = _sb_fwd(proj_e, [late_shards[name] for name in _LATE], nbatch, nb)
    full = {name: _unchunk(name, blk) for name, blk in zip(_LATE, gathered) if _BIG[name][2] == 0 or name == "ev_w_ukv"}
    w_ukv, w_oe, w_oo = full["ev_w_ukv"], full["ev_w_out"], full["od_w_out"]
    w_uq = _from_blocks(gathered[_LATE.index("ev_w_uq")], _UQ_PLAN)
    w_io = _from_blocks(gathered[_LATE.index("od_w_in")], _OD_IN_PLAN)
    cqn, ckvn, krot = _mla_prep(proj_e, gq, gkv, tabs, nbatch, nb)
    qh = _mm(cqn, w_uq, "nn", "mla_uq", out_dtype=BF16)
    kvh = _mm(ckvn, w_ukv, "nn", "mla_ukv", out_dtype=BF16)
    o_mla, og_mla, lse_m = _mla_fwd(qh, kvh, krot, proj_e, tabs, nbatch, nb)
    h1 = _mm([og_sb, og_mla], w_oe, "nn", "out_even", add=h0)
    proj_o, hn1 = _norm_mm(h1, g1, w_io, "proj_odd")
    o_o, og_o, lse_o = _swa_fwd(proj_o, sinks2, nbatch, nb)
    h2 = _mm(og_o, w_oo, "nn", "out_odd", add=h1)
    dh2, lossv, d_final_g = _final(h2, tgt, final_g, nbatch, nb)

    dog_o = _mm(dh2, w_oo, "nt", "d_out_odd")
    d_w_oo, = _mm_tn(og_o, [dh2], "dw_out_odd")
    dq_o, dg_o, dk_o, dv_o, dsink = _swa_bwd(proj_o, sinks2, o_o, dog_o, lse_o, nbatch, nb)
    dproj_o = [dq_o, dg_o, dk_o, dv_o]
    dh1, d_g1, _ = _mm_norm_bwd(dproj_o, w_io, h1, g1, dh2, "d_proj_odd")
    dw_q, dw_g, dw_k, dw_v = _mm_tn(hn1, dproj_o, "dw_proj_odd")

    dog_e = _mm(dh1, w_oe, "nt", "d_out_even")
    d_w_oe_sb, = _mm_tn(og_sb, [dh1], "dw_out_even_sb")
    d_w_oe_mla, = _mm_tn(og_mla, [dh1], "dw_out_even_mla")
    early = dict(od_w_in=_to_chunks([(dw_q, 0, 1024), (dw_k, 0, 128), (dw_v, 0, 128), (dw_g, 0, 1024)], 288),
                 od_w_out=_chunk("od_w_out", d_w_oo),
                 ev_w_out=_chunk("ev_w_out", jnp.concatenate([d_w_oe_sb, d_w_oe_mla], axis=0)))
    dq_sb, dk_sb, dv_sb, dg_sb, received = _sb_bwd(
        proj_e, o_sb, dog_e, cm, [early[name] for name in _EARLY_GRADS], nbatch, nb)
    dqh, dkvh, dkrot, dg_mla = _mla_bwd(qh, kvh, krot, proj_e, tabs, o_mla, dog_e, lse_m, nbatch, nb)
    dcqn = _mm(dqh, w_uq, "nt", "d_mla_uq")
    d_w_uq, = _mm_tn(cqn, [dqh], "dw_mla_uq")
    dckvn = _mm(dkvh, w_ukv, "nt", "d_mla_ukv")
    d_w_ukv, = _mm_tn(ckvn, [dkvh], "dw_mla_ukv")
    dmid, d_gq, d_gkv = _mla_prep_bwd(proj_e, gq, gkv, tabs, dcqn, dckvn, dkrot, nbatch, nb)
    dproj_e = [dq_sb, dk_sb, dv_sb, dg_sb, dmid, dg_mla]
    def ev_in_chunks(dw):
        return _to_chunks([(dw[0], 0, 512), (dw[1], 0, 512), (dw[2], 0, 512), (dw[3], 0, 512),
                           (dw[4], 0, 384), (dw[4], 448, 480), (dw[5], 0, 512)], 372)

    half = D_MODEL // 2
    dw_upper = _mm_tn(hn0, dproj_e, "dw_proj_even_upper", xcols=(half, 0))
    beside = [ev_in_chunks(dw_upper), _to_chunks([(d_w_uq, 128 * h, 128 * h + 96) for h in range(8)], 96),
              _chunk("ev_w_ukv", d_w_ukv)]
    dw_lower, received_mid = _mm_tn(hn0, dproj_e, "dw_proj_even_lower", xcols=(half, 1), chunked=beside)
    dh0, d_g0, received_last = _mm_norm_bwd(dproj_e, w_ie, h0, g0, dh1, "d_proj_even",
                                            chunked=[ev_in_chunks(dw_lower)])

    d_sinks = dsink.reshape(nbatch, 2, 8, BLK)[:, :, :, 0].sum(axis=0).reshape(1, 16)
    d_norm_g = jnp.concatenate([d_g0, d_g1], axis=0)
    return dict(lossv=lossv, dh0=dh0, norm_g=d_norm_g, final_g=d_final_g, gq=d_gq, gkv=d_gkv, sinks=d_sinks,
                received=received + received_mid + received_last)


def _from_blocks(blocks, plan):
    shard = blocks.shape[2]
    parts = []
    for item in plan:
        if isinstance(item, int):
            parts.append(jnp.zeros((blocks.shape[1], item), blocks.dtype))
            continue
        lo, hi = item
        for k in range(N_DEV):
            a, b = max(lo, k * shard), min(hi, (k + 1) * shard)
            if a < b:
                parts.append(blocks[k][:, a - k * shard:b - k * shard])
    return jnp.concatenate(parts, axis=1)


def _to_chunks(segments, shard):
    chunks, start = [[] for _ in range(N_DEV)], 0
    for arr, lo, hi in segments:
        for k in range(N_DEV):
            a, b = max(start, k * shard), min(start + hi - lo, (k + 1) * shard)
            if a < b:
                chunks[k].append(arr[:, lo + a - start:lo + b - start].astype(BF16))
        start += hi - lo
    return jnp.stack([jnp.concatenate(c, axis=1) for c in chunks], axis=0)


_EV_IN_PLAN = [(0, 2432), 64, (2432, 2464), 32, (2464, 2976)]
_UQ_PLAN = [item for h in range(8) for item in ((96 * h, 96 * h + 96), 32)]
_OD_IN_PLAN = [(0, 1024), (1280, 2304), (1024, 1280)]


_BIG = dict(ev_w_in=(1024, 2976, 1), ev_w_uq=(256, 768, 1), ev_w_ukv=(128, 1024, 1),
            ev_w_out=(1024, 1024, 0), od_w_in=(1024, 2304, 1), od_w_out=(1024, 1024, 0))
_LATE = ("ev_w_uq", "ev_w_ukv", "ev_w_out", "od_w_in", "od_w_out")
_EARLY_GRADS = ("od_w_in", "od_w_out", "ev_w_out")
_LAST_GRADS = ("ev_w_in", "ev_w_uq", "ev_w_ukv")


def _unchunk(name, blk):
    rows, cols, axis = _BIG[name]
    return blk.transpose(1, 0, 2).reshape(rows, cols) if axis == 1 else blk.reshape(rows, cols)


def _chunk(name, g):
    rows, cols, axis = _BIG[name]
    g = g.astype(BF16)
    return g.reshape(rows, N_DEV, cols // N_DEV).transpose(1, 0, 2) if axis == 1 else g.reshape(N_DEV, rows // N_DEV, cols)


def _all_gather(shards, name):
    n = len(shards)

    def body(*refs):
        xs, outs = refs[:n], refs[n:2 * n]
        send_sems, recv_sems, local_sems = refs[2 * n:]
        x, y, c = lax.axis_index("x"), lax.axis_index("y"), lax.axis_index("c")
        me, sibling = (x, y, c), (x, y, 1 - c)
        chips = [(1 - x, y), (x, 1 - y), (1 - x, 1 - y)]
        arrays = range(n)

        def copy(k, a, block, to, from_input=False):
            px, py, pc = block
            dst = outs[a].at[4 * px + 2 * py + pc]
            return pltpu.make_async_remote_copy(
                src_ref=xs[a] if from_input else dst, dst_ref=dst,
                send_sem=send_sems.at[k, a], recv_sem=recv_sems.at[k, a],
                device_id=to, device_id_type=pl.DeviceIdType.MESH)

        mine = [pltpu.make_async_copy(xs[a], outs[a].at[4 * x + 2 * y + c], local_sems.at[a]) for a in arrays]
        for cp in mine:
            cp.start()
        first = [copy(0, a, me, sibling, True) for a in arrays]
        for j, chip in enumerate(chips):
            first += [copy(1 + j, a, me, (*chip, c), True) for a in arrays]
        for cp in first:
            cp.start()
        passed = []
        for j, chip in enumerate(chips):
            for a in arrays:
                copy(1 + j, a, (*chip, c), me).wait_recv()
                passed.append(copy(4 + j, a, (*chip, c), sibling))
                passed[-1].start()
        for a in arrays:
            copy(0, a, sibling, me).wait_recv()
        for j, chip in enumerate(chips):
            for a in arrays:
                copy(4 + j, a, (*chip, 1 - c), me).wait_recv()
        for cp in first + passed:
            cp.wait_send()
        for cp in mine:
            cp.wait()

    hbm = pl.BlockSpec(memory_space=pl.ANY)
    return pl.pallas_call(
        body, name=name,
        out_shape=[jax.ShapeDtypeStruct((N_DEV,) + s.shape, s.dtype) for s in shards],
        in_specs=[hbm] * n, out_specs=[hbm] * n,
        scratch_shapes=[pltpu.SemaphoreType.DMA((7, n)), pltpu.SemaphoreType.DMA((7, n)),
                        pltpu.SemaphoreType.DMA((n,))],
    )(*shards)


def _exchange_sum(chunked, received, name):
    n, m = len(chunked), len(received)
    arrs = list(chunked) + list(received)

    def body(*refs):
        ins, outs = refs[:n + m], refs[n + m:2 * (n + m)]
        bufs = refs[2 * (n + m):3 * (n + m)]
        send_sems, recv_sems, local_sems, load_sems = refs[3 * (n + m):]
        loads = [pltpu.make_async_copy(ins[n + a], bufs[n + a], load_sems.at[a]) for a in range(m)]
        for cp in loads:
            cp.start()
        local, sends, recvs = _direct_exchange(ins[:n], bufs[:n], send_sems, recv_sems, local_sems, chunked=True)
        for cp in local + sends:
            cp.start()
        for a, cp in enumerate(loads):
            cp.wait()
            _sum_slots(bufs[n + a], outs[n + a])
        for cp in local:
            cp.wait()
        for cp in recvs:
            cp.wait_recv()
        for cp in sends:
            cp.wait_send()
        for a in range(n):
            _sum_slots(bufs[a], outs[a])

    hbm = pl.BlockSpec(memory_space=pl.ANY)
    vm = pl.BlockSpec(memory_space=pltpu.VMEM)
    return pl.pallas_call(
        body, name=name,
        out_shape=[jax.ShapeDtypeStruct(a.shape[1:], F32) for a in arrs],
        in_specs=[hbm] * (n + m), out_specs=[vm] * (n + m),
        scratch_shapes=[pltpu.VMEM(a.shape, a.dtype) for a in arrs] + _exchange_sems(n)
        + [pltpu.SemaphoreType.DMA((max(m, 1),))],
        compiler_params=pltpu.CompilerParams(vmem_limit_bytes=48 << 20),
    )(*arrs)


def _sum_slots(buf, out):
    rows = buf.shape[1]

    def add(sl):
        acc = buf[(0,) + sl].astype(F32)
        for k in range(1, N_DEV):
            acc = acc + buf[(k,) + sl].astype(F32)
        out[sl] = acc

    if rows > BLK and rows % BLK == 0:
        def step(r, carry):
            add((pl.ds(pl.multiple_of(r * BLK, BLK), BLK), slice(None)))
            return carry

        lax.fori_loop(0, rows // BLK, step, 0)
    else:
        add((slice(None), slice(None)))


def _adamw(ws, gs, ms, vs, name, steps):
    n = len(ws)

    def body(*refs):
        ins, outs = refs[:4 * n], refs[4 * n:]
        for k in range(n):
            w_ref, g_ref, m_ref, v_ref = ins[4 * k:4 * k + 4]
            d_ref, nm_ref, nv_ref = outs[3 * k:3 * k + 3]
            g = g_ref[...]
            m = ADAM_B1 * m_ref[...] + (1.0 - ADAM_B1) * g
            v = ADAM_B2 * v_ref[...] + (1.0 - ADAM_B2) * (g * g)
            m_hat = m / (1.0 - ADAM_B1 ** ADAM_STEP)
            v_hat = v / (1.0 - ADAM_B2 ** ADAM_STEP)
            d_ref[...] = -ADAM_LR * (m_hat / (jnp.sqrt(v_hat) + ADAM_EPS) + ADAM_WD * w_ref[...])
            nm_ref[...] = m
            nv_ref[...] = v

    args, out_shape, in_specs, out_specs = [], [], [], []
    for k in range(n):
        rows, cols = ws[k].shape
        spec = pl.BlockSpec((rows // steps, cols), lambda i: (i, 0))
        args += [ws[k], gs[k], ms[k], vs[k]]
        in_specs += [spec] * 4
        out_specs += [spec] * 3
        out_shape += [jax.ShapeDtypeStruct(ws[k].shape, F32)] * 3
    outs = pl.pallas_call(
        body, name=name, grid=(steps,), out_shape=out_shape, in_specs=in_specs, out_specs=out_specs,
        compiler_params=_params(("parallel",)),
    )(*args)
    return [tuple(outs[3 * k:3 * k + 3]) for k in range(n)]


def kernel(x, meta, norm_g, final_g, ev_w_in, ev_q_norm_g, ev_kv_norm_g, ev_w_uq, ev_w_ukv, ev_w_out, od_w_in, od_sinks, od_w_out, loss_target, m_meta, m_norm_g, m_final_g, m_ev_w_in, m_ev_q_norm_g, m_ev_kv_norm_g, m_ev_w_uq, m_ev_w_ukv, m_ev_w_out, m_od_w_in, m_od_sinks, m_od_w_out, v_meta, v_norm_g, v_final_g, v_ev_w_in, v_ev_q_norm_g, v_ev_kv_norm_g, v_ev_w_uq, v_ev_w_ukv, v_ev_w_out, v_od_w_in, v_od_sinks, v_od_w_out):
    nbatch, seq, d = x.shape
    nb = seq // BLK + 1
    lp = nb * BLK
    shards = dict(ev_w_in=ev_w_in[0], ev_w_uq=ev_w_uq[0], ev_w_ukv=ev_w_ukv[0], ev_w_out=ev_w_out[0],
                  od_w_in=od_w_in[0], od_w_out=od_w_out[0])

    w_ie_blocks, meta_blocks = _all_gather([shards["ev_w_in"].astype(BF16), meta], "gather_weights")
    meta_full = meta_blocks.transpose(1, 0, 2).reshape(N_META, d)

    head = jnp.concatenate([jnp.zeros((N_PAD, d), F32), meta_full], axis=0)
    h0 = jnp.concatenate([jnp.broadcast_to(head[None], (nbatch, BLK, d)), x], axis=1).reshape(nbatch * lp, d)
    grads = _local_step(
        h0, loss_target.reshape(nbatch * seq, d), norm_g, final_g.reshape(1, d), ev_q_norm_g, ev_kv_norm_g,
        od_sinks, _from_blocks(w_ie_blocks, _EV_IN_PLAN),
        {name: shards[name].astype(BF16) for name in _LATE}, nbatch, nb)
    dh0 = grads["dh0"].reshape(nbatch, lp, d)
    grad_x = dh0[:, BLK:]

    d_meta = dh0[:, N_PAD:BLK].sum(axis=0).reshape(N_META, N_DEV, BLK).transpose(1, 0, 2)
    pad = lambda a, n: jnp.concatenate([a.reshape(1, -1), jnp.zeros((1, n - a.size), F32)], axis=1)
    loss_part = (0.5 / d * jnp.sum(grads["lossv"])).reshape(1, 1)
    rep = jnp.concatenate([grads["norm_g"].reshape(1, -1), grads["final_g"], grads["gq"], pad(loss_part, 256),
                           pad(grads["gkv"], 256), pad(grads["sinks"], 256)], axis=1).reshape(32, BLK)
    small = jnp.concatenate([d_meta, jnp.broadcast_to(rep[None], (N_DEV, 32, BLK))], axis=1)
    reduced = _exchange_sum([small], grads["received"], "reduce_grads")
    g_shard = dict(zip(("small",) + _EARLY_GRADS + ("ev_w_in_upper", "ev_w_uq", "ev_w_ukv", "ev_w_in_lower"), reduced))
    g_shard["ev_w_in"] = jnp.concatenate([g_shard.pop("ev_w_in_upper"), g_shard.pop("ev_w_in_lower")], axis=0)
    red_small = g_shard.pop("small")
    rep = red_small[N_META:].reshape(1, -1)
    loss = rep[0, 3 * d + 256]
    g_small = dict(meta=red_small[:N_META], norm_g=rep[:, :2 * d].reshape(2, d), final_g=rep[:, 2 * d:3 * d],
                   ev_q_norm_g=rep[:, 3 * d:3 * d + 256], ev_kv_norm_g=rep[:, 3 * d + 512:3 * d + 640],
                   od_sinks=rep[:, 3 * d + 768:3 * d + 784])

    names = ["meta", "norm_g", "final_g", "ev_w_in", "ev_q_norm_g", "ev_kv_norm_g", "ev_w_uq", "ev_w_ukv",
             "ev_w_out", "od_w_in", "od_sinks", "od_w_out"]
    given = dict(meta=(meta, m_meta, v_meta), norm_g=(norm_g, m_norm_g, v_norm_g),
                 final_g=(final_g, m_final_g, v_final_g), ev_w_in=(ev_w_in, m_ev_w_in, v_ev_w_in),
                 ev_q_norm_g=(ev_q_norm_g, m_ev_q_norm_g, v_ev_q_norm_g),
                 ev_kv_norm_g=(ev_kv_norm_g, m_ev_kv_norm_g, v_ev_kv_norm_g),
                 ev_w_uq=(ev_w_uq, m_ev_w_uq, v_ev_w_uq), ev_w_ukv=(ev_w_ukv, m_ev_w_ukv, v_ev_w_ukv),
                 ev_w_out=(ev_w_out, m_ev_w_out, v_ev_w_out), od_w_in=(od_w_in, m_od_w_in, v_od_w_in),
                 od_sinks=(od_sinks, m_od_sinks, v_od_sinks), od_w_out=(od_w_out, m_od_w_out, v_od_w_out))
    ws, gs, ms, vs = [], [], [], []
    for name in names:
        g2 = g_shard[name] if name in g_shard else g_small[name]
        w, m, v = given[name]
        ws.append(w.reshape(g2.shape))
        ms.append(m.reshape(g2.shape))
        vs.append(v.reshape(g2.shape))
        gs.append(g2)
    big = [k for k, name in enumerate(names) if name in _BIG]
    small = [k for k, name in enumerate(names) if name not in _BIG]
    pick = lambda xs, ks: [xs[k] for k in ks]
    upd_big = _adamw(pick(ws, big), pick(gs, big), pick(ms, big), pick(vs, big), "adamw", N_DEV)
    upd_small = _adamw(pick(ws, small), pick(gs, small), pick(ms, small), pick(vs, small), "adamw_small", 1)
    upd = dict(zip(big + small, upd_big + upd_small))
    shape_of = {name: given[name][0].shape for name in names}
    grads_out = [gs[k].reshape(shape_of[n]) for k, n in enumerate(names)]
    deltas = [upd[k][0].reshape(shape_of[n]) for k, n in enumerate(names)]
    new_m = [upd[k][1].reshape(shape_of[n]) for k, n in enumerate(names)]
    new_v = [upd[k][2].reshape(shape_of[n]) for k, n in enumerate(names)]
    return (loss, grad_x, *grads_out, *deltas, *new_m, *new_v)
```
